```python
import jax, jax.numpy as jnp
from jax import lax
import numpy as np

D_MODEL = 1024
BATCH = 16
SEQ = 256
DEPTH = 1
DEC_BATCH = 2
DEC_SEQ = 1024
PAST_LEN = 256

GRID_W = 64
HEAD_DIM = 64
H_A = 8
H_B = 8
KV_B = 2
G_B = H_B // KV_B
W_A = H_A * HEAD_DIM
W_B = H_B * HEAD_DIM
KV_W_B = KV_B * HEAD_DIM
MIX_WIDTH = W_A + W_B
IN_WIDTH = 3 * W_A + W_B + 2 * KV_W_B
NA_ROWS = 8
NA_COLS = 16
SWA_WINDOW = 128
BLOCK = 128
D_FF = 2816
ROPE_THETA = 10000.0
EPS = 1e-6
NEG_INF = -1e30

kernel_name = 'hybrid_natten_swa_diffusion_step'


def rmsnorm(x, g):
    xf = x.astype(jnp.float32)
    var = jnp.mean(xf * xf, axis=-1, keepdims=True)
    return (xf * lax.rsqrt(var + EPS)).astype(x.dtype) * g


def adaln(cond, w_mod, b_mod):
    m = (jax.nn.silu(cond) @ w_mod + b_mod)[:, None, :]
    return jnp.split(m, 6, axis=-1)


def modulate(h, shift, scale):
    return h * (1 + scale) + shift


def project(h, w_in):
    B, T, _ = h.shape
    z = h @ w_in
    qa, ka, va, qb, kb, vb = jnp.split(z, [W_A, 2 * W_A, 3 * W_A, 3 * W_A + W_B, 3 * W_A + W_B + KV_W_B], axis=-1)
    r = lambda t, n: t.reshape(B, T, n, HEAD_DIM)
    return r(qa, H_A), r(ka, H_A), r(va, H_A), r(qb, H_B), r(kb, KV_B), r(vb, KV_B)


def rope_axis(x, pos):
    n = x.shape[-1] // 2
    inv = 1.0 / (ROPE_THETA ** (jnp.arange(n, dtype=jnp.float32) / n))
    ang = pos.astype(jnp.float32)[:, None] * inv[None, :]
    cos = jnp.cos(ang)[:, None, :].astype(x.dtype)
    sin = jnp.sin(ang)[:, None, :].astype(x.dtype)
    x1, x2 = x[..., :n], x[..., n:]
    return jnp.concatenate([x1 * cos - x2 * sin, x2 * cos + x1 * sin], axis=-1)


def rope_2d(x):
    t = jnp.arange(x.shape[1])
    half = x.shape[-1] // 2
    return jnp.concatenate([rope_axis(x[..., :half], t // GRID_W), rope_axis(x[..., half:], t % GRID_W)], axis=-1)


def ctx_self_attn(q, k, v, sink):
    B, L, KV, G, D = q.shape
    nb = L // BLOCK
    qb = jnp.moveaxis(q.reshape(B, nb, BLOCK, KV, G, D), 1, 0)
    scale = D ** -0.5

    def one(qblk):
        s = jnp.einsum('bqkgd,blkd->bkgql', qblk, k).astype(jnp.float32) * scale
        if sink is not None:
            sk = jnp.broadcast_to(sink.astype(jnp.float32)[None, :, :, None, None], s.shape[:-1] + (1,))
            p = jax.nn.softmax(jnp.concatenate([sk, s], axis=-1), axis=-1)[..., 1:]
        else:
            p = jax.nn.softmax(s, axis=-1)
        return jnp.einsum('bkgql,blkd->bqkgd', p.astype(v.dtype), v)

    o = lax.map(one, qb)
    return jnp.moveaxis(o, 0, 1).reshape(B, L, KV * G * D)


def neighborhood_attn(q, k, v, ck, cv, rpb):
    B, T, H, D = q.shape
    rows = T // GRID_W
    kr = min(NA_ROWS, rows)
    kc = NA_COLS
    r = jnp.arange(rows)
    c = jnp.arange(GRID_W)
    row_start = jnp.clip(r - kr // 2, 0, rows - kr)
    row_idx = row_start[:, None] + jnp.arange(kr)[None, :]
    col_start = jnp.clip(c - kc // 2, 0, GRID_W - kc)
    col_valid = (c[None, :] >= col_start[:, None]) & (c[None, :] < col_start[:, None] + kc)
    kg = k.reshape(B, rows, GRID_W, H, D)[:, row_idx].reshape(B, rows, kr * GRID_W, H, D)
    vg = v.reshape(B, rows, GRID_W, H, D)[:, row_idx].reshape(B, rows, kr * GRID_W, H, D)
    qg = q.reshape(B, rows, GRID_W, H, D)
    drow = row_idx - r[:, None] + (NA_ROWS - 1)
    dcol = jnp.clip(c[None, :] - c[:, None], -(kc - 1), kc - 1) + (kc - 1)
    bias = rpb[:, drow[:, None, :, None], dcol[None, :, None, :]]
    bias = bias.reshape(H, rows, GRID_W, kr * GRID_W).astype(jnp.float32)
    valid = jnp.broadcast_to(col_valid[:, None, :], (GRID_W, kr, GRID_W)).reshape(GRID_W, kr * GRID_W)
    scale = D ** -0.5
    s_nb = jnp.einsum('brqhd,brkhd->bhrqk', qg, kg).astype(jnp.float32) * scale + bias[None]
    s_nb = jnp.where(valid[None, None, None], s_nb, NEG_INF)
    s_cx = jnp.einsum('brqhd,blhd->bhrql', qg, ck).astype(jnp.float32) * scale
    p = jax.nn.softmax(jnp.concatenate([s_nb, s_cx], axis=-1), axis=-1).astype(v.dtype)
    n = kr * GRID_W
    o = (jnp.einsum('bhrqk,brkhd->brqhd', p[..., :n], vg)
         + jnp.einsum('bhrql,blhd->brqhd', p[..., n:], cv))
    return o.reshape(B, T, H * D)


def window_attn(q, k, v, ck, cv, sink):
    B, T, KV, G, D = q.shape
    nb = T // BLOCK
    qb = q.reshape(B, nb, BLOCK, KV, G, D)

    def band(x):
        xb = jnp.pad(x.reshape(B, nb, BLOCK, KV, D), ((0, 0), (1, 1), (0, 0), (0, 0), (0, 0)))
        return jnp.concatenate([xb[:, :-2], xb[:, 1:-1], xb[:, 2:]], axis=2)

    kb, vb = band(k), band(v)
    blk = jnp.arange(nb)[:, None]
    qpos = blk * BLOCK + jnp.arange(BLOCK)[None, :]
    kpos = (blk - 1) * BLOCK + jnp.arange(3 * BLOCK)[None, :]
    valid = ((jnp.abs(qpos[:, :, None] - kpos[:, None, :]) <= SWA_WINDOW)
             & (kpos[:, None, :] >= 0) & (kpos[:, None, :] < T))
    scale = D ** -0.5
    s = jnp.einsum('bnqkgd,bnskd->bnkgqs', qb, kb).astype(jnp.float32) * scale
    s = jnp.where(valid[None, :, None, None], s, NEG_INF)
    sc = jnp.einsum('bnqkgd,blkd->bnkgql', qb, ck).astype(jnp.float32) * scale
    sk = jnp.broadcast_to(sink.astype(jnp.float32)[None, None, :, :, None, None], s.shape[:-1] + (1,))
    p = jax.nn.softmax(jnp.concatenate([sk, s, sc], axis=-1), axis=-1).astype(v.dtype)
    nbk = 3 * BLOCK
    o = (jnp.einsum('bnkgqs,bnskd->bnqkgd', p[..., 1:1 + nbk], vb)
         + jnp.einsum('bnkgql,blkd->bnqkgd', p[..., 1 + nbk:], cv))
    return o.reshape(B, T, KV * G * D)


def merge_heads(oa, ob, g_a, g_b, w_out):
    return jnp.concatenate([rmsnorm(oa, g_a), rmsnorm(ob, g_b)], axis=-1) @ w_out


def conv_ffn(h, w_up, conv_w, conv_b, w_down):
    u = h @ w_up
    up = jnp.pad(u, ((0, 0), (1, 1), (0, 0)))
    u = up[:, :-2] * conv_w[0] + up[:, 1:-1] * conv_w[1] + up[:, 2:] * conv_w[2] + conv_b
    gate, val = jnp.split(u, 2, axis=-1)
    return (jax.nn.silu(gate) * val) @ w_down


def setup_inputs(seed: int = 0) -> dict:
    key = jax.random.key(seed)
    ks = jax.random.split(key, 24)
    nrm = lambda k, shape, s: jax.random.normal(k, shape, jnp.float32) * s
    gain = lambda k, shape: 1.0 + 0.01 * jax.random.normal(k, shape, jnp.float32)
    return {
        'x_prompt': nrm(ks[0], (BATCH, SEQ, D_MODEL), 1.0),
        'x_sample': nrm(ks[1], (DEC_BATCH, DEC_SEQ, D_MODEL), 1.0),
        'cache_a_k': nrm(ks[2], (DEC_BATCH, DEPTH, PAST_LEN, H_A, HEAD_DIM), 1.0),
        'cache_a_v': nrm(ks[3], (DEC_BATCH, DEPTH, PAST_LEN, H_A, HEAD_DIM), 1.0),
        'cache_b_k': nrm(ks[4], (DEC_BATCH, DEPTH, PAST_LEN, KV_B, HEAD_DIM), 1.0),
        'cache_b_v': nrm(ks[5], (DEC_BATCH, DEPTH, PAST_LEN, KV_B, HEAD_DIM), 1.0),
        'c': nrm(ks[6], (DEC_BATCH, D_MODEL), 1.0),
        'c_ctx': nrm(ks[7], (D_MODEL,), 1.0),
        'w_mod': nrm(ks[8], (DEPTH, D_MODEL, 6 * D_MODEL), 0.5 * D_MODEL ** -0.5),
        'b_mod': nrm(ks[9], (DEPTH, 6 * D_MODEL), 0.01),
        'g_mix_pre': gain(ks[10], (DEPTH, D_MODEL)),
        'g_mix_post': gain(ks[11], (DEPTH, D_MODEL)),
        'g_ffn_pre': gain(ks[12], (DEPTH, D_MODEL)),
        'g_ffn_post': gain(ks[13], (DEPTH, D_MODEL)),
        'w_in': nrm(ks[14], (DEPTH, D_MODEL, IN_WIDTH), D_MODEL ** -0.5),
        'rpb_a': nrm(ks[15], (DEPTH, H_A, 2 * NA_ROWS - 1, 2 * NA_COLS - 1), 0.1),
        'sink_b': nrm(ks[16], (DEPTH, KV_B, G_B), 0.5),
        'g_grp_a': gain(ks[17], (DEPTH, W_A)),
        'g_grp_b': gain(ks[18], (DEPTH, W_B)),
        'w_out': nrm(ks[19], (DEPTH, MIX_WIDTH, D_MODEL), MIX_WIDTH ** -0.5),
        'w_up': nrm(ks[20], (DEPTH, D_MODEL, 2 * D_FF), D_MODEL ** -0.5),
        'conv_w': nrm(ks[21], (DEPTH, 3, 2 * D_FF), 3 ** -0.5),
        'conv_b': nrm(ks[22], (DEPTH, 2 * D_FF), 0.01),
        'w_down': nrm(ks[23], (DEPTH, D_FF, D_MODEL), D_FF ** -0.5),
    }


def reference(x_prompt, x_sample, cache_a_k, cache_a_v, cache_b_k, cache_b_v, c, c_ctx,
              w_mod, b_mod, g_mix_pre, g_mix_post, g_ffn_pre, g_ffn_post, w_in, rpb_a, sink_b,
              g_grp_a, g_grp_b, w_out, w_up, conv_w, conv_b, w_down):
    xp = x_prompt
    xs = x_sample
    Bs, T, _ = x_sample.shape
    new_ak, new_av, new_bk, new_bv = [], [], [], []
    for l in range(DEPTH):
        sh1, sc1, gt1, sh2, sc2, gt2 = adaln(c_ctx[None], w_mod[l], b_mod[l])
        h = modulate(rmsnorm(xp, g_mix_pre[l]), sh1, sc1)
        qa, ka, va, qb, kb, vb = project(h, w_in[l])
        Bp, L = xp.shape[0], xp.shape[1]
        oa = ctx_self_attn(qa[:, :, :, None, :], ka, va, None)
        ob = ctx_self_attn(qb.reshape(Bp, L, KV_B, G_B, HEAD_DIM), kb, vb, sink_b[l])
        xp = xp + gt1 * rmsnorm(merge_heads(oa, ob, g_grp_a[l], g_grp_b[l], w_out[l]), g_mix_post[l])
        h = modulate(rmsnorm(xp, g_ffn_pre[l]), sh2, sc2)
        xp = xp + gt2 * rmsnorm(conv_ffn(h, w_up[l], conv_w[l], conv_b[l], w_down[l]), g_ffn_post[l])
        new_ak.append(ka)
        new_av.append(va)
        new_bk.append(kb)
        new_bv.append(vb)

        sh1, sc1, gt1, sh2, sc2, gt2 = adaln(c, w_mod[l], b_mod[l])
        h = modulate(rmsnorm(xs, g_mix_pre[l]), sh1, sc1)
        qa, ka, va, qb, kb, vb = project(h, w_in[l])
        qb = rope_2d(qb).reshape(Bs, T, KV_B, G_B, HEAD_DIM)
        kb = rope_2d(kb)
        oa = neighborhood_attn(qa, ka, va, cache_a_k[:, l], cache_a_v[:, l], rpb_a[l])
        ob = window_attn(qb, kb, vb, cache_b_k[:, l], cache_b_v[:, l], sink_b[l])
        xs = xs + gt1 * rmsnorm(merge_heads(oa, ob, g_grp_a[l], g_grp_b[l], w_out[l]), g_mix_post[l])
        h = modulate(rmsnorm(xs, g_ffn_pre[l]), sh2, sc2)
        xs = xs + gt2 * rmsnorm(conv_ffn(h, w_up[l], conv_w[l], conv_b[l], w_down[l]), g_ffn_post[l])
    new_a_k = jnp.stack(new_ak, axis=1)
    new_a_v = jnp.stack(new_av, axis=1)
    new_b_k = jnp.stack(new_bk, axis=1)
    new_b_v = jnp.stack(new_bv, axis=1)
    return (xp, xs, new_a_k, new_a_v, new_b_k, new_b_v)
```

```python
import functools

import numpy as np
import jax
import jax.numpy as jnp
from jax import lax
from jax.experimental import pallas as pl
from jax.experimental.pallas import tpu as pltpu

D_MODEL = 1024
HEAD_DIM = 64
H_A = 8
H_B = 8
KV_B = 2
G_B = H_B // KV_B
W_A = H_A * HEAD_DIM
W_B = H_B * HEAD_DIM
KV_W_B = KV_B * HEAD_DIM
IN_WIDTH = 3 * W_A + W_B + 2 * KV_W_B
GRID_W = 64
NA_ROWS = 8
NA_COLS = 16
SWA_WINDOW = 128
BLOCK = 128
D_FF = 2816
ROPE_THETA = 10000.0
EPS = 1e-6
NEG_INF = -1e30
Q_SCALE = HEAD_DIM ** -0.5

LANES = 128
FF_CHUNK = 256
TOKEN_TILE = 512
FFN_TILE = 1024
VMEM_LIMIT = 48 * 1024 * 1024

C_QA, C_KA, C_VA, C_QB, C_KB, C_VB = 0, W_A, 2 * W_A, 3 * W_A, 3 * W_A + W_B, 3 * W_A + W_B + KV_W_B
C_QB_SW = IN_WIDTH
C_KB_SW = IN_WIDTH + W_B
IN_WIDTH_EXT = IN_WIDTH + W_B + KV_W_B

_QB_HEAD_ORDER = tuple(kv * G_B + g for g in range(G_B) for kv in range(KV_B))

_BF16 = jnp.bfloat16
_F32 = jnp.float32


def _dot(a, b):
    return jnp.dot(a, b, preferred_element_type=_F32)


def _dot_nt(a, b):
    return lax.dot_general(a, b, (((1,), (1,)), ((), ())), preferred_element_type=_F32)


def _rms(x, g):
    var = jnp.mean(x * x, axis=-1, keepdims=True)
    return x * lax.rsqrt(var + EPS) * g


def _stack_heads(q2, lo):
    zero = jnp.zeros_like(q2)
    return jnp.concatenate([jnp.where(lo, q2, zero), jnp.where(lo, zero, q2)], axis=0)


def _unstack_heads(o2, lo):
    m = o2.shape[0] // 2
    return jnp.where(lo, o2[:m], o2[m:])


def _softmax_pv(scores, values, sink=None):
    m = scores[0].max(axis=-1, keepdims=True)
    for s in scores[1:]:
        m = jnp.maximum(m, s.max(axis=-1, keepdims=True))
    if sink is not None:
        m = jnp.maximum(m, sink)
    denom = None
    out = None
    for s, v in zip(scores, values):
        e = jnp.exp(s - m)
        l = e.sum(axis=-1, keepdims=True)
        o = _dot(e.astype(_BF16), v)
        denom = l if denom is None else denom + l
        out = o if out is None else out + o
    if sink is not None:
        denom = denom + jnp.exp(sink - m)
    return out / denom


def _mod_kernel(c_ref, w_ref, b_ref, o_ref):
    c = c_ref[...]
    s = (c * jax.nn.sigmoid(c)).astype(_BF16)
    o_ref[...] = _dot(s, w_ref[...].astype(_BF16)) + b_ref[...]


def _mod_call(cond8, w_mod, b_mod):
    n = w_mod.shape[1]
    tn = 1024
    return pl.pallas_call(
        _mod_kernel,
        grid=(n // tn,),
        in_specs=[
            pl.BlockSpec((8, D_MODEL), lambda i: (0, 0)),
            pl.BlockSpec((D_MODEL, tn), lambda i: (0, i)),
            pl.BlockSpec((1, tn), lambda i: (0, i)),
        ],
        out_specs=pl.BlockSpec((8, tn), lambda i: (0, i)),
        out_shape=jax.ShapeDtypeStruct((8, n), _F32),
        compiler_params=pltpu.CompilerParams(dimension_semantics=("parallel",)),
        name="mod",
    )(cond8, w_mod, b_mod.reshape(1, n))


def _inproj_kernel(*refs, rope):
    if rope:
        x_ref, mod_ref, g_ref, w_ref, cos_ref, sin_ref, q_ref, ka_ref, va_ref, kb_ref, vb_ref = refs
    else:
        x_ref, mod_ref, g_ref, w_ref, q_ref, ka_ref, va_ref, kb_ref, vb_ref = refs
    mod = mod_ref[0]
    sh1 = mod[:, 0:D_MODEL]
    sc1 = mod[:, D_MODEL:2 * D_MODEL]
    h = (_rms(x_ref[...], g_ref[...]) * (1 + sc1) + sh1).astype(_BF16)

    def proj(c0, n):
        return _dot(h, w_ref[:, c0:c0 + n])

    q_ref[:, 0:W_A] = (proj(C_QA, W_A) * Q_SCALE).astype(q_ref.dtype)
    ka_ref[...] = proj(C_KA, W_A).astype(ka_ref.dtype)
    va_ref[...] = proj(C_VA, W_A).astype(va_ref.dtype)
    vb_ref[...] = proj(C_VB, KV_W_B).astype(vb_ref.dtype)
    if rope:
        cos = cos_ref[...]
        sin = sin_ref[...]
        for j in range(W_B // LANES):
            zq = proj(C_QB + LANES * j, LANES)
            zs = proj(C_QB_SW + LANES * j, LANES)
            q_ref[:, W_A + LANES * j:W_A + LANES * (j + 1)] = (
                (zq * cos + zs * sin) * Q_SCALE).astype(q_ref.dtype)
        kb_ref[...] = (proj(C_KB, KV_W_B) * cos + proj(C_KB_SW, KV_W_B) * sin).astype(kb_ref.dtype)
    else:
        q_ref[:, W_A:W_A + W_B] = (proj(C_QB, W_B) * Q_SCALE).astype(q_ref.dtype)
        kb_ref[...] = proj(C_KB, KV_W_B).astype(kb_ref.dtype)


def _inproj_call(x2d, mod3, mod_row, g_pre, w_ext, rope_tabs, kv_dtype, name):
    t = x2d.shape[0]
    tm = TOKEN_TILE
    rope = rope_tabs is not None
    in_specs = [
        pl.BlockSpec((tm, D_MODEL), lambda i: (i, 0)),
        pl.BlockSpec((1, 1, 6 * D_MODEL), lambda i: (mod_row(i), 0, 0)),
        pl.BlockSpec((1, D_MODEL), lambda i: (0, 0)),
        pl.BlockSpec((D_MODEL, IN_WIDTH_EXT), lambda i: (0, 0)),
    ]
    args = [x2d, mod3, g_pre.reshape(1, D_MODEL), w_ext]
    if rope:
        tiles_per_seq = rope_tabs[0].shape[0] // tm
        in_specs += [pl.BlockSpec((tm, LANES), lambda i: (i % tiles_per_seq, 0))] * 2
        args += list(rope_tabs)
    out_shape = (
        jax.ShapeDtypeStruct((t, W_A + W_B), _BF16),
        jax.ShapeDtypeStruct((t, W_A), kv_dtype),
        jax.ShapeDtypeStruct((t, W_A), kv_dtype),
        jax.ShapeDtypeStruct((t, KV_W_B), kv_dtype),
        jax.ShapeDtypeStruct((t, KV_W_B), kv_dtype),
    )
    out_specs = (
        pl.BlockSpec((tm, W_A + W_B), lambda i: (i, 0)),
        pl.BlockSpec((tm, W_A), lambda i: (i, 0)),
        pl.BlockSpec((tm, W_A), lambda i: (i, 0)),
        pl.BlockSpec((tm, KV_W_B), lambda i: (i, 0)),
        pl.BlockSpec((tm, KV_W_B), lambda i: (i, 0)),
    )
    return pl.pallas_call(
        functools.partial(_inproj_kernel, rope=rope),
        grid=(t // tm,),
        in_specs=in_specs,
        out_specs=out_specs,
        out_shape=out_shape,
        compiler_params=pltpu.CompilerParams(
            dimension_semantics=("parallel",), vmem_limit_bytes=VMEM_LIMIT),
        name=name,
    )(*args)


def _ctx_attn_kernel(sink_ref, q_ref, ka_ref, va_ref, kb_ref, vb_ref, oa_ref, ob_ref):
    l = q_ref.shape[0]
    lo = lax.broadcasted_iota(jnp.int32, (l, LANES), 1) < HEAD_DIM
    row_lo = lax.broadcasted_iota(jnp.int32, (2 * l, 1), 0) < l
    for p in range(W_A // LANES):
        sl = slice(LANES * p, LANES * (p + 1))
        qs = _stack_heads(q_ref[:, sl], lo)
        s = _dot_nt(qs, ka_ref[:, sl].astype(_BF16))
        o2 = _softmax_pv([s], [va_ref[:, sl].astype(_BF16)])
        oa_ref[:, sl] = _unstack_heads(o2, lo)
    kb = kb_ref[...].astype(_BF16)
    vb = vb_ref[...].astype(_BF16)
    for j in range(G_B):
        sl = slice(LANES * j, LANES * (j + 1))
        qs = _stack_heads(q_ref[:, W_A + LANES * j:W_A + LANES * (j + 1)], lo)
        s = _dot_nt(qs, kb)
        sink = jnp.where(row_lo, sink_ref[0, j], sink_ref[1, j])
        o2 = _softmax_pv([s], [vb], sink)
        ob_ref[:, sl] = _unstack_heads(o2, lo)


def _ctx_attn_call(sink, q, ka, va, kb, vb, seq):
    t = q.shape[0]
    row = lambda b: (b, 0)
    return pl.pallas_call(
        _ctx_attn_kernel,
        grid=(t // seq,),
        in_specs=[
            pl.BlockSpec(memory_space=pltpu.SMEM),
            pl.BlockSpec((seq, W_A + W_B), row),
            pl.BlockSpec((seq, W_A), row),
            pl.BlockSpec((seq, W_A), row),
            pl.BlockSpec((seq, KV_W_B), row),
            pl.BlockSpec((seq, KV_W_B), row),
        ],
        out_specs=(pl.BlockSpec((seq, W_A), row), pl.BlockSpec((seq, W_B), row)),
        out_shape=(jax.ShapeDtypeStruct((t, W_A), _F32), jax.ShapeDtypeStruct((t, W_B), _F32)),
        compiler_params=pltpu.CompilerParams(
            dimension_semantics=("parallel",), vmem_limit_bytes=VMEM_LIMIT),
        name="ctx_attn",
    )(sink, q, ka, va, kb, vb)


def _na_kernel(q_ref, k_ref, v_ref, ck_ref, cv_ref, rpb_ref, o_ref, bias_ref):
    rows = q_ref.shape[0] // GRID_W
    kr = min(NA_ROWS, rows)
    lane = lax.broadcasted_iota(jnp.int32, (GRID_W, LANES), 1)
    lo = lane < HEAD_DIM
    cq = lax.broadcasted_iota(jnp.int32, (GRID_W, LANES), 0)
    ck = lane & (GRID_W - 1)
    cs = jnp.clip(cq - NA_COLS // 2, 0, GRID_W - NA_COLS)
    valid = (ck >= cs) & (ck < cs + NA_COLS)

    n_drow = 2 * NA_ROWS - 1
    for hh in range(2):
        t_lo, t_hi = [], []
        for d in range(n_drow):
            base = jnp.broadcast_to(rpb_ref[hh, d:d + 1, :], (GRID_W, LANES))
            t_lo.append(pltpu.roll(base, LANES - (NA_COLS - 1), 1, stride=1, stride_axis=0))
            t_hi.append(pltpu.roll(base, GRID_W - (NA_COLS - 1), 1, stride=1, stride_axis=0))
        pairs = {}
        for d0 in range(NA_ROWS):
            for j in range(kr // 2):
                d = d0 + 2 * j
                if d not in pairs:
                    pairs[d] = jnp.where(valid, jnp.where(lo, t_lo[d], t_hi[d + 1]), NEG_INF)
                bias_ref[d0, GRID_W * hh:GRID_W * (hh + 1), LANES * j:LANES * (j + 1)] = pairs[d]

    ckb = ck_ref[0].astype(_BF16)
    cvb = cv_ref[0].astype(_BF16)

    def body(r, carry):
        rs = jnp.clip(r - kr // 2, 0, rows - kr)
        d0 = rs - r + (NA_ROWS - 1)
        q0 = pl.multiple_of(r * GRID_W, GRID_W)
        k0 = pl.multiple_of(rs * GRID_W, GRID_W)
        qs = _stack_heads(q_ref[pl.ds(q0, GRID_W), :], lo)
        kw = k_ref[pl.ds(k0, kr * GRID_W), :]
        vw = v_ref[pl.ds(k0, kr * GRID_W), :]
        s_nb = _dot_nt(qs, kw) + bias_ref[d0]
        s_cx = _dot_nt(qs, ckb)
        o2 = _softmax_pv([s_nb, s_cx], [vw, cvb])
        o_ref[pl.ds(q0, GRID_W), :] = _unstack_heads(o2, lo)
        return carry

    lax.fori_loop(0, rows, body, 0)


def _na_call(q, k, v, ck, cv, rpb_pad, nb, seq):
    past = ck.shape[1]
    kr = min(NA_ROWS, seq // GRID_W)
    blk = lambda b, p: (b, p)
    return pl.pallas_call(
        _na_kernel,
        grid=(nb, W_A // LANES),
        in_specs=[
            pl.BlockSpec((seq, LANES), blk),
            pl.BlockSpec((seq, LANES), blk),
            pl.BlockSpec((seq, LANES), blk),
            pl.BlockSpec((1, past, LANES), lambda b, p: (b, 0, p)),
            pl.BlockSpec((1, past, LANES), lambda b, p: (b, 0, p)),
            pl.BlockSpec((2, 2 * NA_ROWS - 1, LANES), lambda b, p: (p, 0, 0)),
        ],
        out_specs=pl.BlockSpec((seq, LANES), blk),
        out_shape=jax.ShapeDtypeStruct((nb * seq, W_A), _F32),
        scratch_shapes=[pltpu.VMEM((NA_ROWS, 2 * GRID_W, kr * GRID_W), _F32)],
        compiler_params=pltpu.CompilerParams(
            dimension_semantics=("parallel", "parallel"), vmem_limit_bytes=VMEM_LIMIT),
        name="na_attn",
    )(q, k, v, ck, cv, rpb_pad)


def _win_kernel(sink_ref, q_ref, k_ref, v_ref, ck_ref, cv_ref, o_ref):
    t = q_ref.shape[0]
    nblk = t // BLOCK
    band = 3 * BLOCK
    j = pl.program_id(1)
    lo = lax.broadcasted_iota(jnp.int32, (BLOCK, LANES), 1) < HEAD_DIM
    row_lo = lax.broadcasted_iota(jnp.int32, (2 * BLOCK, 1), 0) < BLOCK
    sink = jnp.where(row_lo, sink_ref[0, j], sink_ref[1, j])
    qi = lax.broadcasted_iota(jnp.int32, (2 * BLOCK, band), 0) & (BLOCK - 1)
    kj = lax.broadcasted_iota(jnp.int32, (2 * BLOCK, band), 1)
    rel = qi - kj
    ckb = ck_ref[0].astype(_BF16)
    cvb = cv_ref[0].astype(_BF16)

    def body(n, carry):
        start = jnp.clip((n - 1) * BLOCK, 0, t - band)
        q0 = pl.multiple_of(n * BLOCK, BLOCK)
        k0 = pl.multiple_of(start, BLOCK)
        qs = _stack_heads(q_ref[pl.ds(q0, BLOCK), :], lo)
        s = _dot_nt(qs, k_ref[pl.ds(k0, band), :])
        s = jnp.where(jnp.abs(rel + (n * BLOCK - start)) <= SWA_WINDOW, s, NEG_INF)
        s_cx = _dot_nt(qs, ckb)
        o2 = _softmax_pv([s, s_cx], [v_ref[pl.ds(k0, band), :], cvb], sink)
        o_ref[pl.ds(q0, BLOCK), :] = _unstack_heads(o2, lo)
        return carry

    lax.fori_loop(0, nblk, body, 0)


def _win_call(sink, q, k, v, ck, cv, nb, seq):
    past = ck.shape[1]
    return pl.pallas_call(
        _win_kernel,
        grid=(nb, G_B),
        in_specs=[
            pl.BlockSpec(memory_space=pltpu.SMEM),
            pl.BlockSpec((seq, LANES), lambda b, j: (b, W_A // LANES + j)),
            pl.BlockSpec((seq, LANES), lambda b, j: (b, 0)),
            pl.BlockSpec((seq, LANES), lambda b, j: (b, 0)),
            pl.BlockSpec((1, past, LANES), lambda b, j: (b, 0, 0)),
            pl.BlockSpec((1, past, LANES), lambda b, j: (b, 0, 0)),
        ],
        out_specs=pl.BlockSpec((seq, LANES), lambda b, j: (b, j)),
        out_shape=jax.ShapeDtypeStruct((nb * seq, W_B), _F32),
        compiler_params=pltpu.CompilerParams(
            dimension_semantics=("parallel", "parallel"), vmem_limit_bytes=VMEM_LIMIT),
        name="win_attn",
    )(sink, q, k, v, ck, cv)


def _merge_kernel(oa_ref, ob_ref, x_ref, mod_ref, ga_ref, gb_ref, gpost_ref, gffn_ref, w_ref,
                  x1_ref, h2_ref):
    mod = mod_ref[0]
    gt1 = mod[:, 2 * D_MODEL:3 * D_MODEL]
    sh2 = mod[:, 3 * D_MODEL:4 * D_MODEL]
    sc2 = mod[:, 4 * D_MODEL:5 * D_MODEL]
    a = _rms(oa_ref[...], ga_ref[...]).astype(_BF16)
    b = _rms(ob_ref[...], gb_ref[...]).astype(_BF16)
    y = _dot(a, w_ref[0:W_A, :]) + _dot(b, w_ref[W_A:W_A + W_B, :])
    x1 = x_ref[...] + gt1 * _rms(y, gpost_ref[...])
    x1_ref[...] = x1
    h2_ref[...] = (_rms(x1, gffn_ref[...]) * (1 + sc2) + sh2).astype(h2_ref.dtype)


def _merge_call(oa, ob, x2d, mod3, mod_row, g_a, g_b, g_post, g_ffn, w_out, name):
    t = x2d.shape[0]
    tm = TOKEN_TILE
    row = lambda i: (i, 0)
    const = lambda i: (0, 0)
    return pl.pallas_call(
        _merge_kernel,
        grid=(t // tm,),
        in_specs=[
            pl.BlockSpec((tm, W_A), row),
            pl.BlockSpec((tm, W_B), row),
            pl.BlockSpec((tm, D_MODEL), row),
            pl.BlockSpec((1, 1, 6 * D_MODEL), lambda i: (mod_row(i), 0, 0)),
            pl.BlockSpec((1, W_A), const),
            pl.BlockSpec((1, W_B), const),
            pl.BlockSpec((1, D_MODEL), const),
            pl.BlockSpec((1, D_MODEL), const),
            pl.BlockSpec((W_A + W_B, D_MODEL), const),
        ],
        out_specs=(pl.BlockSpec((tm, D_MODEL), row), pl.BlockSpec((tm, D_MODEL), row)),
        out_shape=(jax.ShapeDtypeStruct((t, D_MODEL), _F32), jax.ShapeDtypeStruct((t, D_MODEL), _BF16)),
        compiler_params=pltpu.CompilerParams(
            dimension_semantics=("parallel",), vmem_limit_bytes=VMEM_LIMIT),
        name=name,
    )(oa, ob, x2d, mod3, g_a.reshape(1, W_A), g_b.reshape(1, W_B), g_post.reshape(1, D_MODEL),
      g_ffn.reshape(1, D_MODEL), w_out)


def _ffn_kernel(h2_ref, x1_ref, mod_ref, gpost_ref, wg_ref, wv_ref, cwg_ref, cwv_ref, cbg_ref, cbv_ref,
                wd_ref, out_ref, acc_ref, *, seq_len):
    c = pl.program_id(1)
    h2 = h2_ref[...]
    tm = h2.shape[0]
    pos = lax.broadcasted_iota(jnp.int32, (tm, 1), 0) & (seq_len - 1)
    first = pos == 0
    last = pos == seq_len - 1

    def conv(u, cw_ref, cb_ref):
        prev = jnp.where(first, 0.0, pltpu.roll(u, 1, 0))
        nxt = jnp.where(last, 0.0, pltpu.roll(u, tm - 1, 0))
        return prev * cw_ref[0:1, :] + u * cw_ref[1:2, :] + nxt * cw_ref[2:3, :] + cb_ref[...]

    gate = conv(_dot(h2, wg_ref[...]), cwg_ref, cbg_ref)
    val = conv(_dot(h2, wv_ref[...]), cwv_ref, cbv_ref)
    act = (gate * jax.nn.sigmoid(gate) * val).astype(_BF16)
    part = _dot(act, wd_ref[...])

    @pl.when(c == 0)
    def _():
        acc_ref[...] = part

    @pl.when(c > 0)
    def _():
        acc_ref[...] += part

    @pl.when(c == pl.num_programs(1) - 1)
    def _():
        gt2 = mod_ref[0][:, 5 * D_MODEL:6 * D_MODEL]
        out_ref[...] = x1_ref[...] + gt2 * _rms(acc_ref[...], gpost_ref[...])


def _ffn_call(h2, x1, mod3, mod_row, g_post, w_up, conv_w, conv_b, w_down, seq_len, name):
    t = h2.shape[0]
    tm = FFN_TILE
    tc = FF_CHUNK
    nc = D_FF // tc
    row = lambda i, c: (i, 0)
    return pl.pallas_call(
        functools.partial(_ffn_kernel, seq_len=seq_len),
        grid=(t // tm, nc),
        in_specs=[
            pl.BlockSpec((tm, D_MODEL), row),
            pl.BlockSpec((tm, D_MODEL), row),
            pl.BlockSpec((1, 1, 6 * D_MODEL), lambda i, c: (mod_row(i), 0, 0)),
            pl.BlockSpec((1, D_MODEL), lambda i, c: (0, 0)),
            pl.BlockSpec((D_MODEL, tc), lambda i, c: (0, c)),
            pl.BlockSpec((D_MODEL, tc), lambda i, c: (0, nc + c)),
            pl.BlockSpec((3, tc), lambda i, c: (0, c)),
            pl.BlockSpec((3, tc), lambda i, c: (0, nc + c)),
            pl.BlockSpec((1, tc), lambda i, c: (0, c)),
            pl.BlockSpec((1, tc), lambda i, c: (0, nc + c)),
            pl.BlockSpec((tc, D_MODEL), lambda i, c: (c, 0)),
        ],
        out_specs=pl.BlockSpec((tm, D_MODEL), row),
        out_shape=jax.ShapeDtypeStruct((t, D_MODEL), _F32),
        scratch_shapes=[pltpu.VMEM((tm, D_MODEL), _F32)],
        compiler_params=pltpu.CompilerParams(
            dimension_semantics=("parallel", "arbitrary"), vmem_limit_bytes=VMEM_LIMIT),
        name=name,
    )(h2, x1, mod3, g_post.reshape(1, D_MODEL), w_up, w_up, conv_w, conv_w,
      conv_b.reshape(1, 2 * D_FF), conv_b.reshape(1, 2 * D_FF), w_down)


def _rope_tables(seq):
    n = HEAD_DIM // 4
    t = np.arange(seq)
    lane = np.arange(HEAD_DIM)
    pos = np.where(lane[None, :] < HEAD_DIM // 2, (t // GRID_W)[:, None], (t % GRID_W)[:, None])
    inv = 1.0 / (ROPE_THETA ** (np.arange(n, dtype=np.float64) / n))
    ang = pos.astype(np.float64) * inv[lane % n][None, :]
    sign = np.where((lane & n) == 0, -1.0, 1.0)[None, :]
    cos = np.tile(np.cos(ang), (1, LANES // HEAD_DIM)).astype(np.float32)
    sin = np.tile(np.sin(ang) * sign, (1, LANES // HEAD_DIM)).astype(np.float32)
    return jnp.asarray(cos), jnp.asarray(sin)


def _in_proj_columns():
    n = HEAD_DIM // 4
    dd = np.arange(HEAD_DIM)
    qb = np.concatenate([C_QB + h * HEAD_DIM + dd for h in _QB_HEAD_ORDER])
    qb_sw = np.concatenate([C_QB + h * HEAD_DIM + (dd ^ n) for h in _QB_HEAD_ORDER])
    kb_sw = np.concatenate([C_KB + h * HEAD_DIM + (dd ^ n) for h in range(KV_B)])
    cols = np.concatenate([np.arange(C_QB), qb, np.arange(C_KB, IN_WIDTH), qb_sw, kb_sw])
    assert cols.shape[0] == IN_WIDTH_EXT
    return cols


def _ob_rows():
    dd = np.arange(HEAD_DIM)
    return np.concatenate([h * HEAD_DIM + dd for h in _QB_HEAD_ORDER])


def kernel(x_prompt, x_sample, cache_a_k, cache_a_v, cache_b_k, cache_b_v, c, c_ctx, w_mod, b_mod,
           g_mix_pre, g_mix_post, g_ffn_pre, g_ffn_post, w_in, rpb_a, sink_b, g_grp_a, g_grp_b,
           w_out, w_up, conv_w, conv_b, w_down):
    bp, lp, _ = x_prompt.shape
    bs, ts, _ = x_sample.shape
    depth = w_in.shape[0]
    assert depth == 1
    past = cache_a_k.shape[2]
    xp = x_prompt.reshape(bp * lp, D_MODEL)
    xs = x_sample.reshape(bs * ts, D_MODEL)
    cond8 = jnp.concatenate([c_ctx[None], c, jnp.zeros((8 - 1 - bs, D_MODEL), _F32)], axis=0)
    ob_rows = _ob_rows()
    cos_tab, sin_tab = _rope_tables(ts)
    n_drow = 2 * NA_ROWS - 1

    l = 0
    mod3 = _mod_call(cond8, w_mod[l], b_mod[l]).reshape(8, 1, 6 * D_MODEL)
    w_ext = jnp.take(w_in[l], _in_proj_columns(), axis=1).astype(_BF16)
    w_out_p = jnp.concatenate([w_out[l][:W_A], jnp.take(w_out[l][W_A:], ob_rows, axis=0)], axis=0).astype(_BF16)
    g_b_p = jnp.take(g_grp_b[l], ob_rows)
    w_up_b = w_up[l].astype(_BF16)
    w_down_b = w_down[l].astype(_BF16)
    rpb_pad = jnp.pad(rpb_a[l], ((0, 0), (0, 0), (0, LANES - (2 * NA_COLS - 1))))

    prompt_row = lambda i: 0
    s_tiles = ts // TOKEN_TILE
    sample_row = lambda i: 1 + i // s_tiles
    sample_row_ffn = lambda i: 1 + i // (ts // FFN_TILE)

    q_p, ka_p, va_p, kb_p, vb_p = _inproj_call(xp, mod3, prompt_row, g_mix_pre[l], w_ext, None, _F32,
                                               "inproj_prompt")
    oa_p, ob_p = _ctx_attn_call(sink_b[l], q_p, ka_p, va_p, kb_p, vb_p, lp)
    x1_p, h2_p = _merge_call(oa_p, ob_p, xp, mod3, prompt_row, g_grp_a[l], g_b_p, g_mix_post[l],
                             g_ffn_pre[l], w_out_p, "merge_prompt")
    y_p = _ffn_call(h2_p, x1_p, mod3, prompt_row, g_ffn_post[l], w_up_b, conv_w[l], conv_b[l], w_down_b,
                    lp, "ffn_prompt")

    q_s, ka_s, va_s, kb_s, vb_s = _inproj_call(xs, mod3, sample_row, g_mix_pre[l], w_ext,
                                               (cos_tab, sin_tab), _BF16, "inproj_sample")
    oa_s = _na_call(q_s, ka_s, va_s, cache_a_k[:, l].reshape(bs, past, W_A),
                    cache_a_v[:, l].reshape(bs, past, W_A), rpb_pad, bs, ts)
    ob_s = _win_call(sink_b[l], q_s, kb_s, vb_s, cache_b_k[:, l].reshape(bs, past, KV_W_B),
                     cache_b_v[:, l].reshape(bs, past, KV_W_B), bs, ts)
    x1_s, h2_s = _merge_call(oa_s, ob_s, xs, mod3, sample_row, g_grp_a[l], g_b_p, g_mix_post[l],
                             g_ffn_pre[l], w_out_p, "merge_sample")
    y_s = _ffn_call(h2_s, x1_s, mod3, sample_row_ffn, g_ffn_post[l], w_up_b, conv_w[l], conv_b[l],
                    w_down_b, ts, "ffn_sample")

    new_a_k = ka_p.reshape(bp, 1, lp, H_A, HEAD_DIM)
    new_a_v = va_p.reshape(bp, 1, lp, H_A, HEAD_DIM)
    new_b_k = kb_p.reshape(bp, 1, lp, KV_B, HEAD_DIM)
    new_b_v = vb_p.reshape(bp, 1, lp, KV_B, HEAD_DIM)
    return (y_p.reshape(bp, lp, D_MODEL), y_s.reshape(bs, ts, D_MODEL), new_a_k, new_a_v, new_b_k, new_b_v)
```

```python
import functools

import numpy as np
import jax
import jax.numpy as jnp
from jax import lax
from jax.experimental import pallas as pl
from jax.experimental.pallas import tpu as pltpu

D_MODEL = 1024
HEAD_DIM = 64
H_A = 8
H_B = 8
KV_B = 2
G_B = H_B // KV_B
W_A = H_A * HEAD_DIM
W_B = H_B * HEAD_DIM
KV_W_B = KV_B * HEAD_DIM
IN_WIDTH = 3 * W_A + W_B + 2 * KV_W_B
GRID_W = 64
NA_ROWS = 8
NA_COLS = 16
SWA_WINDOW = 128
BLOCK = 128
D_FF = 2816
ROPE_THETA = 10000.0
EPS = 1e-6
NEG_INF = -1e30
Q_SCALE = HEAD_DIM ** -0.5

LANES = 128
FF_CHUNK = 256
TOKEN_TILE = 512
FFN_TILE = 1024
FFN_ROW_BLOCK = 256
VMEM_LIMIT = 48 * 1024 * 1024

C_QA, C_KA, C_VA, C_QB, C_KB, C_VB = 0, W_A, 2 * W_A, 3 * W_A, 3 * W_A + W_B, 3 * W_A + W_B + KV_W_B
C_QB_SW = IN_WIDTH
C_KB_SW = IN_WIDTH + W_B
IN_WIDTH_EXT = IN_WIDTH + W_B + KV_W_B

_QB_HEAD_ORDER = tuple(kv * G_B + g for g in range(G_B) for kv in range(KV_B))

_BF16 = jnp.bfloat16
_F32 = jnp.float32


def _dot(a, b):
    return jnp.dot(a, b, preferred_element_type=_F32)


def _dot_nt(a, b):
    return lax.dot_general(a, b, (((1,), (1,)), ((), ())), preferred_element_type=_F32)


def _rms(x, g):
    var = jnp.mean(x * x, axis=-1, keepdims=True)
    return x * lax.rsqrt(var + EPS) * g


def _stack_heads(q2, lo):
    zero = jnp.zeros_like(q2)
    return jnp.concatenate([jnp.where(lo, q2, zero), jnp.where(lo, zero, q2)], axis=0)


def _unstack_heads(o2, lo):
    m = o2.shape[0] // 2
    return jnp.where(lo, o2[:m], o2[m:])


def _softmax_pv(scores, values, sink=None):
    m = scores[0].max(axis=-1, keepdims=True)
    for s in scores[1:]:
        m = jnp.maximum(m, s.max(axis=-1, keepdims=True))
    if sink is not None:
        m = jnp.maximum(m, sink)
    denom = None
    out = None
    for s, v in zip(scores, values):
        e = jnp.exp(s - m)
        l = e.sum(axis=-1, keepdims=True)
        o = _dot(e.astype(_BF16), v)
        denom = l if denom is None else denom + l
        out = o if out is None else out + o
    if sink is not None:
        denom = denom + jnp.exp(sink - m)
    return out / denom


def _mod_kernel(c_ref, w_ref, b_ref, o_ref):
    c = c_ref[...]
    s = (c * jax.nn.sigmoid(c)).astype(_BF16)
    o_ref[...] = _dot(s, w_ref[...].astype(_BF16)) + b_ref[...]


def _mod_call(cond8, w_mod, b_mod):
    n = w_mod.shape[1]
    tn = 1024
    return pl.pallas_call(
        _mod_kernel,
        grid=(n // tn,),
        in_specs=[
            pl.BlockSpec((8, D_MODEL), lambda i: (0, 0)),
            pl.BlockSpec((D_MODEL, tn), lambda i: (0, i)),
            pl.BlockSpec((1, tn), lambda i: (0, i)),
        ],
        out_specs=pl.BlockSpec((8, tn), lambda i: (0, i)),
        out_shape=jax.ShapeDtypeStruct((8, n), _F32),
        compiler_params=pltpu.CompilerParams(dimension_semantics=("parallel",)),
        name="mod",
    )(cond8, w_mod, b_mod.reshape(1, n))


def _inproj_kernel(*refs, rope):
    if rope:
        x_ref, mod_ref, g_ref, w_ref, cos_ref, sin_ref, q_ref, ka_ref, va_ref, kb_ref, vb_ref = refs
    else:
        x_ref, mod_ref, g_ref, w_ref, q_ref, ka_ref, va_ref, kb_ref, vb_ref = refs
    mod = mod_ref[0]
    sh1 = mod[:, 0:D_MODEL]
    sc1 = mod[:, D_MODEL:2 * D_MODEL]
    h = (_rms(x_ref[...], g_ref[...]) * (1 + sc1) + sh1).astype(_BF16)

    def proj(c0, n):
        return _dot(h, w_ref[:, c0:c0 + n])

    q_ref[:, 0:W_A] = (proj(C_QA, W_A) * Q_SCALE).astype(q_ref.dtype)
    ka_ref[...] = proj(C_KA, W_A).astype(ka_ref.dtype)
    va_ref[...] = proj(C_VA, W_A).astype(va_ref.dtype)
    vb_ref[...] = proj(C_VB, KV_W_B).astype(vb_ref.dtype)
    if rope:
        cos = cos_ref[...]
        sin = sin_ref[...]
        for j in range(W_B // LANES):
            zq = proj(C_QB + LANES * j, LANES)
            zs = proj(C_QB_SW + LANES * j, LANES)
            q_ref[:, W_A + LANES * j:W_A + LANES * (j + 1)] = (
                (zq * cos + zs * sin) * Q_SCALE).astype(q_ref.dtype)
        kb_ref[...] = (proj(C_KB, KV_W_B) * cos + proj(C_KB_SW, KV_W_B) * sin).astype(kb_ref.dtype)
    else:
        q_ref[:, W_A:W_A + W_B] = (proj(C_QB, W_B) * Q_SCALE).astype(q_ref.dtype)
        kb_ref[...] = proj(C_KB, KV_W_B).astype(kb_ref.dtype)


def _inproj_call(x2d, mod3, mod_row, g_pre, w_ext, rope_tabs, kv_dtype, name):
    t = x2d.shape[0]
    tm = TOKEN_TILE
    rope = rope_tabs is not None
    in_specs = [
        pl.BlockSpec((tm, D_MODEL), lambda i: (i, 0)),
        pl.BlockSpec((1, 1, 6 * D_MODEL), lambda i: (mod_row(i), 0, 0)),
        pl.BlockSpec((1, D_MODEL), lambda i: (0, 0)),
        pl.BlockSpec((D_MODEL, IN_WIDTH_EXT), lambda i: (0, 0)),
    ]
    args = [x2d, mod3, g_pre.reshape(1, D_MODEL), w_ext]
    if rope:
        tiles_per_seq = rope_tabs[0].shape[0] // tm
        in_specs += [pl.BlockSpec((tm, LANES), lambda i: (i % tiles_per_seq, 0))] * 2
        args += list(rope_tabs)
    out_shape = (
        jax.ShapeDtypeStruct((t, W_A + W_B), _BF16),
        jax.ShapeDtypeStruct((t, W_A), kv_dtype),
        jax.ShapeDtypeStruct((t, W_A), kv_dtype),
        jax.ShapeDtypeStruct((t, KV_W_B), kv_dtype),
        jax.ShapeDtypeStruct((t, KV_W_B), kv_dtype),
    )
    out_specs = (
        pl.BlockSpec((tm, W_A + W_B), lambda i: (i, 0)),
        pl.BlockSpec((tm, W_A), lambda i: (i, 0)),
        pl.BlockSpec((tm, W_A), lambda i: (i, 0)),
        pl.BlockSpec((tm, KV_W_B), lambda i: (i, 0)),
        pl.BlockSpec((tm, KV_W_B), lambda i: (i, 0)),
    )
    return pl.pallas_call(
        functools.partial(_inproj_kernel, rope=rope),
        grid=(t // tm,),
        in_specs=in_specs,
        out_specs=out_specs,
        out_shape=out_shape,
        compiler_params=pltpu.CompilerParams(
            dimension_semantics=("parallel",), vmem_limit_bytes=VMEM_LIMIT),
        name=name,
    )(*args)


def _ctx_attn_kernel(sink_ref, q_ref, ka_ref, va_ref, kb_ref, vb_ref, oa_ref, ob_ref):
    l = q_ref.shape[0]
    lo = lax.broadcasted_iota(jnp.int32, (l, LANES), 1) < HEAD_DIM
    row_lo = lax.broadcasted_iota(jnp.int32, (2 * l, 1), 0) < l
    for p in range(W_A // LANES):
        sl = slice(LANES * p, LANES * (p + 1))
        qs = _stack_heads(q_ref[:, sl], lo)
        s = _dot_nt(qs, ka_ref[:, sl].astype(_BF16))
        o2 = _softmax_pv([s], [va_ref[:, sl].astype(_BF16)])
        oa_ref[:, sl] = _unstack_heads(o2, lo)
    kb = kb_ref[...].astype(_BF16)
    vb = vb_ref[...].astype(_BF16)
    for j in range(G_B):
        sl = slice(LANES * j, LANES * (j + 1))
        qs = _stack_heads(q_ref[:, W_A + LANES * j:W_A + LANES * (j + 1)], lo)
        s = _dot_nt(qs, kb)
        sink = jnp.where(row_lo, sink_ref[0, j], sink_ref[1, j])
        o2 = _softmax_pv([s], [vb], sink)
        ob_ref[:, sl] = _unstack_heads(o2, lo)


def _ctx_attn_call(sink, q, ka, va, kb, vb, seq):
    t = q.shape[0]
    row = lambda b: (b, 0)
    return pl.pallas_call(
        _ctx_attn_kernel,
        grid=(t // seq,),
        in_specs=[
            pl.BlockSpec(memory_space=pltpu.SMEM),
            pl.BlockSpec((seq, W_A + W_B), row),
            pl.BlockSpec((seq, W_A), row),
            pl.BlockSpec((seq, W_A), row),
            pl.BlockSpec((seq, KV_W_B), row),
            pl.BlockSpec((seq, KV_W_B), row),
        ],
        out_specs=(pl.BlockSpec((seq, W_A), row), pl.BlockSpec((seq, W_B), row)),
        out_shape=(jax.ShapeDtypeStruct((t, W_A), _F32), jax.ShapeDtypeStruct((t, W_B), _F32)),
        compiler_params=pltpu.CompilerParams(
            dimension_semantics=("parallel",), vmem_limit_bytes=VMEM_LIMIT),
        name="ctx_attn",
    )(sink, q, ka, va, kb, vb)


def _na_kernel(q_ref, k_ref, v_ref, ck_ref, cv_ref, rpb_ref, o_ref, bias_ref):
    rows = q_ref.shape[0] // GRID_W
    kr = min(NA_ROWS, rows)
    lane = lax.broadcasted_iota(jnp.int32, (GRID_W, LANES), 1)
    lo = lane < HEAD_DIM
    cq = lax.broadcasted_iota(jnp.int32, (GRID_W, LANES), 0)
    ck = lane & (GRID_W - 1)
    cs = jnp.clip(cq - NA_COLS // 2, 0, GRID_W - NA_COLS)
    valid = (ck >= cs) & (ck < cs + NA_COLS)

    n_drow = 2 * NA_ROWS - 1
    for hh in range(2):
        t_lo, t_hi = [], []
        for d in range(n_drow):
            base = jnp.broadcast_to(rpb_ref[hh, d:d + 1, :], (GRID_W, LANES))
            t_lo.append(pltpu.roll(base, LANES - (NA_COLS - 1), 1, stride=1, stride_axis=0))
            t_hi.append(pltpu.roll(base, GRID_W - (NA_COLS - 1), 1, stride=1, stride_axis=0))
        pairs = {}
        for d0 in range(NA_ROWS):
            for j in range(kr // 2):
                d = d0 + 2 * j
                if d not in pairs:
                    pairs[d] = jnp.where(valid, jnp.where(lo, t_lo[d], t_hi[d + 1]), NEG_INF)
                bias_ref[d0, GRID_W * hh:GRID_W * (hh + 1), LANES * j:LANES * (j + 1)] = pairs[d]

    ckb = ck_ref[0].astype(_BF16)
    cvb = cv_ref[0].astype(_BF16)

    def body(r, carry):
        rs = jnp.clip(r - kr // 2, 0, rows - kr)
        d0 = rs - r + (NA_ROWS - 1)
        q0 = pl.multiple_of(r * GRID_W, GRID_W)
        k0 = pl.multiple_of(rs * GRID_W, GRID_W)
        qs = _stack_heads(q_ref[pl.ds(q0, GRID_W), :], lo)
        kw = k_ref[pl.ds(k0, kr * GRID_W), :]
        vw = v_ref[pl.ds(k0, kr * GRID_W), :]
        s_nb = _dot_nt(qs, kw) + bias_ref[d0]
        s_cx = _dot_nt(qs, ckb)
        o2 = _softmax_pv([s_nb, s_cx], [vw, cvb])
        o_ref[pl.ds(q0, GRID_W), :] = _unstack_heads(o2, lo)
        return carry

    lax.fori_loop(0, rows, body, 0)


def _na_call(q, k, v, ck, cv, rpb_pad, nb, seq):
    past = ck.shape[1]
    kr = min(NA_ROWS, seq // GRID_W)
    blk = lambda b, p: (b, p)
    return pl.pallas_call(
        _na_kernel,
        grid=(nb, W_A // LANES),
        in_specs=[
            pl.BlockSpec((seq, LANES), blk),
            pl.BlockSpec((seq, LANES), blk),
            pl.BlockSpec((seq, LANES), blk),
            pl.BlockSpec((1, past, LANES), lambda b, p: (b, 0, p)),
            pl.BlockSpec((1, past, LANES), lambda b, p: (b, 0, p)),
            pl.BlockSpec((2, 2 * NA_ROWS - 1, LANES), lambda b, p: (p, 0, 0)),
        ],
        out_specs=pl.BlockSpec((seq, LANES), blk),
        out_shape=jax.ShapeDtypeStruct((nb * seq, W_A), _F32),
        scratch_shapes=[pltpu.VMEM((NA_ROWS, 2 * GRID_W, kr * GRID_W), _F32)],
        compiler_params=pltpu.CompilerParams(
            dimension_semantics=("parallel", "parallel"), vmem_limit_bytes=VMEM_LIMIT),
        name="na_attn",
    )(q, k, v, ck, cv, rpb_pad)


def _win_kernel(sink_ref, q_ref, k_ref, v_ref, ck_ref, cv_ref, o_ref):
    t = q_ref.shape[0]
    nblk = t // BLOCK
    band = 3 * BLOCK
    j = pl.program_id(1)
    lo = lax.broadcasted_iota(jnp.int32, (BLOCK, LANES), 1) < HEAD_DIM
    row_lo = lax.broadcasted_iota(jnp.int32, (2 * BLOCK, 1), 0) < BLOCK
    sink = jnp.where(row_lo, sink_ref[0, j], sink_ref[1, j])
    qi = lax.broadcasted_iota(jnp.int32, (2 * BLOCK, band), 0) & (BLOCK - 1)
    kj = lax.broadcasted_iota(jnp.int32, (2 * BLOCK, band), 1)
    rel = qi - kj
    ckb = ck_ref[0].astype(_BF16)
    cvb = cv_ref[0].astype(_BF16)

    def body(n, carry):
        start = jnp.clip((n - 1) * BLOCK, 0, t - band)
        q0 = pl.multiple_of(n * BLOCK, BLOCK)
        k0 = pl.multiple_of(start, BLOCK)
        qs = _stack_heads(q_ref[pl.ds(q0, BLOCK), :], lo)
        s = _dot_nt(qs, k_ref[pl.ds(k0, band), :])
        s = jnp.where(jnp.abs(rel + (n * BLOCK - start)) <= SWA_WINDOW, s, NEG_INF)
        s_cx = _dot_nt(qs, ckb)
        o2 = _softmax_pv([s, s_cx], [v_ref[pl.ds(k0, band), :], cvb], sink)
        o_ref[pl.ds(q0, BLOCK), :] = _unstack_heads(o2, lo)
        return carry

    lax.fori_loop(0, nblk, body, 0)


def _win_call(sink, q, k, v, ck, cv, nb, seq):
    past = ck.shape[1]
    return pl.pallas_call(
        _win_kernel,
        grid=(nb, G_B),
        in_specs=[
            pl.BlockSpec(memory_space=pltpu.SMEM),
            pl.BlockSpec((seq, LANES), lambda b, j: (b, W_A // LANES + j)),
            pl.BlockSpec((seq, LANES), lambda b, j: (b, 0)),
            pl.BlockSpec((seq, LANES), lambda b, j: (b, 0)),
            pl.BlockSpec((1, past, LANES), lambda b, j: (b, 0, 0)),
            pl.BlockSpec((1, past, LANES), lambda b, j: (b, 0, 0)),
        ],
        out_specs=pl.BlockSpec((seq, LANES), lambda b, j: (b, j)),
        out_shape=jax.ShapeDtypeStruct((nb * seq, W_B), _F32),
        compiler_params=pltpu.CompilerParams(
            dimension_semantics=("parallel", "parallel"), vmem_limit_bytes=VMEM_LIMIT),
        name="win_attn",
    )(sink, q, k, v, ck, cv)


def _merge_kernel(oa_ref, ob_ref, x_ref, mod_ref, ga_ref, gb_ref, gpost_ref, gffn_ref, w_ref,
                  x1_ref, h2_ref):
    mod = mod_ref[0]
    gt1 = mod[:, 2 * D_MODEL:3 * D_MODEL]
    sh2 = mod[:, 3 * D_MODEL:4 * D_MODEL]
    sc2 = mod[:, 4 * D_MODEL:5 * D_MODEL]
    a = _rms(oa_ref[...], ga_ref[...]).astype(_BF16)
    b = _rms(ob_ref[...], gb_ref[...]).astype(_BF16)
    y = _dot(a, w_ref[0:W_A, :]) + _dot(b, w_ref[W_A:W_A + W_B, :])
    x1 = x_ref[...] + gt1 * _rms(y, gpost_ref[...])
    x1_ref[...] = x1
    h2_ref[...] = (_rms(x1, gffn_ref[...]) * (1 + sc2) + sh2).astype(h2_ref.dtype)


def _merge_call(oa, ob, x2d, mod3, mod_row, g_a, g_b, g_post, g_ffn, w_out, name):
    t = x2d.shape[0]
    tm = TOKEN_TILE
    row = lambda i: (i, 0)
    const = lambda i: (0, 0)
    return pl.pallas_call(
        _merge_kernel,
        grid=(t // tm,),
        in_specs=[
            pl.BlockSpec((tm, W_A), row),
            pl.BlockSpec((tm, W_B), row),
            pl.BlockSpec((tm, D_MODEL), row),
            pl.BlockSpec((1, 1, 6 * D_MODEL), lambda i: (mod_row(i), 0, 0)),
            pl.BlockSpec((1, W_A), const),
            pl.BlockSpec((1, W_B), const),
            pl.BlockSpec((1, D_MODEL), const),
            pl.BlockSpec((1, D_MODEL), const),
            pl.BlockSpec((W_A + W_B, D_MODEL), const),
        ],
        out_specs=(pl.BlockSpec((tm, D_MODEL), row), pl.BlockSpec((tm, D_MODEL), row)),
        out_shape=(jax.ShapeDtypeStruct((t, D_MODEL), _F32), jax.ShapeDtypeStruct((t, D_MODEL), _BF16)),
        compiler_params=pltpu.CompilerParams(
            dimension_semantics=("parallel",), vmem_limit_bytes=VMEM_LIMIT),
        name=name,
    )(oa, ob, x2d, mod3, g_a.reshape(1, W_A), g_b.reshape(1, W_B), g_post.reshape(1, D_MODEL),
      g_ffn.reshape(1, D_MODEL), w_out)


def _ffn_kernel(h2_ref, x1_ref, mod_ref, gpost_ref, wg_ref, wv_ref, cwg_ref, cwv_ref, cbg_ref, cbv_ref,
                wd_ref, out_ref, act_ref, *, seq_len):
    c = pl.program_id(1)
    tm = h2_ref.shape[0]
    pos = lax.broadcasted_iota(jnp.int32, (tm, 1), 0) & (seq_len - 1)
    first = pos == 0
    last = pos == seq_len - 1

    def conv(u, cw_ref, cb_ref):
        prev = jnp.where(first, 0.0, pltpu.roll(u, 1, 0))
        nxt = jnp.where(last, 0.0, pltpu.roll(u, tm - 1, 0))
        return prev * cw_ref[0:1, :] + u * cw_ref[1:2, :] + nxt * cw_ref[2:3, :] + cb_ref[...]

    def up(w_ref):
        w = w_ref[...]
        return jnp.concatenate(
            [_dot(h2_ref[r:r + FFN_ROW_BLOCK, :], w) for r in range(0, tm, FFN_ROW_BLOCK)], axis=0)

    gate = conv(up(wg_ref), cwg_ref, cbg_ref)
    val = conv(up(wv_ref), cwv_ref, cbv_ref)
    tc = gate.shape[1]
    act_ref[:, pl.ds(pl.multiple_of(c * tc, tc), tc)] = (gate * jax.nn.sigmoid(gate) * val).astype(_BF16)

    @pl.when(c == pl.num_programs(1) - 1)
    def _():
        gt2 = mod_ref[0][:, 5 * D_MODEL:6 * D_MODEL]
        y = _dot(act_ref[...], wd_ref[...])
        out_ref[...] = x1_ref[...] + gt2 * _rms(y, gpost_ref[...])


def _ffn_call(h2, x1, mod3, mod_row, g_post, w_up, conv_w, conv_b, w_down, seq_len, name):
    t = h2.shape[0]
    tm = FFN_TILE
    tc = FF_CHUNK
    nc = D_FF // tc
    row = lambda i, c: (i, 0)
    return pl.pallas_call(
        functools.partial(_ffn_kernel, seq_len=seq_len),
        grid=(t // tm, nc),
        in_specs=[
            pl.BlockSpec((tm, D_MODEL), row),
            pl.BlockSpec((tm, D_MODEL), row),
            pl.BlockSpec((1, 1, 6 * D_MODEL), lambda i, c: (mod_row(i), 0, 0)),
            pl.BlockSpec((1, D_MODEL), lambda i, c: (0, 0)),
            pl.BlockSpec((D_MODEL, tc), lambda i, c: (0, c)),
            pl.BlockSpec((D_MODEL, tc), lambda i, c: (0, nc + c)),
            pl.BlockSpec((3, tc), lambda i, c: (0, c)),
            pl.BlockSpec((3, tc), lambda i, c: (0, nc + c)),
            pl.BlockSpec((1, tc), lambda i, c: (0, c)),
            pl.BlockSpec((1, tc), lambda i, c: (0, nc + c)),
            pl.BlockSpec((D_FF, D_MODEL), lambda i, c: (0, 0)),
        ],
        out_specs=pl.BlockSpec((tm, D_MODEL), row),
        out_shape=jax.ShapeDtypeStruct((t, D_MODEL), _F32),
        scratch_shapes=[pltpu.VMEM((tm, D_FF), _BF16)],
        compiler_params=pltpu.CompilerParams(
            dimension_semantics=("parallel", "arbitrary"), vmem_limit_bytes=VMEM_LIMIT),
        name=name,
    )(h2, x1, mod3, g_post.reshape(1, D_MODEL), w_up, w_up, conv_w, conv_w,
      conv_b.reshape(1, 2 * D_FF), conv_b.reshape(1, 2 * D_FF), w_down)


def _rope_tables(seq):
    n = HEAD_DIM // 4
    t = np.arange(seq)
    lane = np.arange(HEAD_DIM)
    pos = np.where(lane[None, :] < HEAD_DIM // 2, (t // GRID_W)[:, None], (t % GRID_W)[:, None])
    inv = 1.0 / (ROPE_THETA ** (np.arange(n, dtype=np.float64) / n))
    ang = pos.astype(np.float64) * inv[lane % n][None, :]
    sign = np.where((lane & n) == 0, -1.0, 1.0)[None, :]
    cos = np.tile(np.cos(ang), (1, LANES // HEAD_DIM)).astype(np.float32)
    sin = np.tile(np.sin(ang) * sign, (1, LANES // HEAD_DIM)).astype(np.float32)
    return jnp.asarray(cos), jnp.asarray(sin)


def _in_proj_columns():
    n = HEAD_DIM // 4
    dd = np.arange(HEAD_DIM)
    qb = np.concatenate([C_QB + h * HEAD_DIM + dd for h in _QB_HEAD_ORDER])
    qb_sw = np.concatenate([C_QB + h * HEAD_DIM + (dd ^ n) for h in _QB_HEAD_ORDER])
    kb_sw = np.concatenate([C_KB + h * HEAD_DIM + (dd ^ n) for h in range(KV_B)])
    cols = np.concatenate([np.arange(C_QB), qb, np.arange(C_KB, IN_WIDTH), qb_sw, kb_sw])
    assert cols.shape[0] == IN_WIDTH_EXT
    return cols


def _ob_rows():
    dd = np.arange(HEAD_DIM)
    return np.concatenate([h * HEAD_DIM + dd for h in _QB_HEAD_ORDER])


def kernel(x_prompt, x_sample, cache_a_k, cache_a_v, cache_b_k, cache_b_v, c, c_ctx, w_mod, b_mod,
           g_mix_pre, g_mix_post, g_ffn_pre, g_ffn_post, w_in, rpb_a, sink_b, g_grp_a, g_grp_b,
           w_out, w_up, conv_w, conv_b, w_down):
    bp, lp, _ = x_prompt.shape
    bs, ts, _ = x_sample.shape
    depth = w_in.shape[0]
    assert depth == 1
    past = cache_a_k.shape[2]
    xp = x_prompt.reshape(bp * lp, D_MODEL)
    xs = x_sample.reshape(bs * ts, D_MODEL)
    cond8 = jnp.concatenate([c_ctx[None], c, jnp.zeros((8 - 1 - bs, D_MODEL), _F32)], axis=0)
    ob_rows = _ob_rows()
    cos_tab, sin_tab = _rope_tables(ts)
    n_drow = 2 * NA_ROWS - 1

    l = 0
    mod3 = _mod_call(cond8, w_mod[l], b_mod[l]).reshape(8, 1, 6 * D_MODEL)
    w_ext = jnp.take(w_in[l], _in_proj_columns(), axis=1).astype(_BF16)
    w_out_p = jnp.concatenate([w_out[l][:W_A], jnp.take(w_out[l][W_A:], ob_rows, axis=0)], axis=0).astype(_BF16)
    g_b_p = jnp.take(g_grp_b[l], ob_rows)
    w_up_b = w_up[l].astype(_BF16)
    w_down_b = w_down[l].astype(_BF16)
    rpb_pad = jnp.pad(rpb_a[l], ((0, 0), (0, 0), (0, LANES - (2 * NA_COLS - 1))))

    prompt_row = lambda i: 0
    s_tiles = ts // TOKEN_TILE
    sample_row = lambda i: 1 + i // s_tiles
    sample_row_ffn = lambda i: 1 + i // (ts // FFN_TILE)

    q_p, ka_p, va_p, kb_p, vb_p = _inproj_call(xp, mod3, prompt_row, g_mix_pre[l], w_ext, None, _F32,
                                               "inproj_prompt")
    oa_p, ob_p = _ctx_attn_call(sink_b[l], q_p, ka_p, va_p, kb_p, vb_p, lp)
    x1_p, h2_p = _merge_call(oa_p, ob_p, xp, mod3, prompt_row, g_grp_a[l], g_b_p, g_mix_post[l],
                             g_ffn_pre[l], w_out_p, "merge_prompt")
    y_p = _ffn_call(h2_p, x1_p, mod3, prompt_row, g_ffn_post[l], w_up_b, conv_w[l], conv_b[l], w_down_b,
                    lp, "ffn_prompt")

    q_s, ka_s, va_s, kb_s, vb_s = _inproj_call(xs, mod3, sample_row, g_mix_pre[l], w_ext,
                                               (cos_tab, sin_tab), _BF16, "inproj_sample")
    oa_s = _na_call(q_s, ka_s, va_s, cache_a_k[:, l].reshape(bs, past, W_A),
                    cache_a_v[:, l].reshape(bs, past, W_A), rpb_pad, bs, ts)
    ob_s = _win_call(sink_b[l], q_s, kb_s, vb_s, cache_b_k[:, l].reshape(bs, past, KV_W_B),
                     cache_b_v[:, l].reshape(bs, past, KV_W_B), bs, ts)
    x1_s, h2_s = _merge_call(oa_s, ob_s, xs, mod3, sample_row, g_grp_a[l], g_b_p, g_mix_post[l],
                             g_ffn_pre[l], w_out_p, "merge_sample")
    y_s = _ffn_call(h2_s, x1_s, mod3, sample_row_ffn, g_ffn_post[l], w_up_b, conv_w[l], conv_b[l],
                    w_down_b, ts, "ffn_sample")

    new_a_k = ka_p.reshape(bp, 1, lp, H_A, HEAD_DIM)
    new_a_v = va_p.reshape(bp, 1, lp, H_A, HEAD_DIM)
    new_b_k = kb_p.reshape(bp, 1, lp, KV_B, HEAD_DIM)
    new_b_v = vb_p.reshape(bp, 1, lp, KV_B, HEAD_DIM)
    return (y_p.reshape(bp, lp, D_MODEL), y_s.reshape(bs, ts, D_MODEL), new_a_k, new_a_v, new_b_k, new_b_v)
```

```python
import functools

import numpy as np
import jax
import jax.numpy as jnp
from jax import lax
from jax.experimental import pallas as pl
from jax.experimental.pallas import tpu as pltpu

D_MODEL = 1024
HEAD_DIM = 64
H_A = 8
H_B = 8
KV_B = 2
G_B = H_B // KV_B
W_A = H_A * HEAD_DIM
W_B = H_B * HEAD_DIM
KV_W_B = KV_B * HEAD_DIM
IN_WIDTH = 3 * W_A + W_B + 2 * KV_W_B
GRID_W = 64
NA_ROWS = 8
NA_COLS = 16
SWA_WINDOW = 128
BLOCK = 128
D_FF = 2816
ROPE_THETA = 10000.0
EPS = 1e-6
NEG_INF = -1e30
Q_SCALE = HEAD_DIM ** -0.5
ROPE_HALF = HEAD_DIM // 4

LANES = 128
FF_CHUNK = 256
TOKEN_TILE = 512
FFN_TILE = 1024
FFN_ROW_BLOCK = 256
CAST_COLS = 512
VMEM_LIMIT = 48 * 1024 * 1024

C_QA, C_KA, C_VA, C_QB, C_KB, C_VB = 0, W_A, 2 * W_A, 3 * W_A, 3 * W_A + W_B, 3 * W_A + W_B + KV_W_B

_QB_HEAD_ORDER = tuple(kv * G_B + g for g in range(G_B) for kv in range(KV_B))

_BF16 = jnp.bfloat16
_F32 = jnp.float32


def _dot(a, b):
    return jnp.dot(a, b, preferred_element_type=_F32)


def _dot_nt(a, b):
    return lax.dot_general(a, b, (((1,), (1,)), ((), ())), preferred_element_type=_F32)


def _rms(x, g):
    var = jnp.mean(x * x, axis=-1, keepdims=True)
    return x * lax.rsqrt(var + EPS) * g


def _stack_heads(q2, lo):
    zero = jnp.zeros_like(q2)
    return jnp.concatenate([jnp.where(lo, q2, zero), jnp.where(lo, zero, q2)], axis=0)


def _unstack_heads(o2, lo):
    m = o2.shape[0] // 2
    return jnp.where(lo, o2[:m], o2[m:])


def _softmax_pv(scores, pv, sink=None):
    m = scores[0].max(axis=-1, keepdims=True)
    for s in scores[1:]:
        m = jnp.maximum(m, s.max(axis=-1, keepdims=True))
    if sink is not None:
        m = jnp.maximum(m, sink)
    denom = None
    out = None
    for s, f in zip(scores, pv):
        e = jnp.exp(s - m)
        l = e.sum(axis=-1, keepdims=True)
        o = f(e.astype(_BF16))
        denom = l if denom is None else denom + l
        out = o if out is None else out + o
    if sink is not None:
        denom = denom + jnp.exp(sink - m)
    return out / denom


def _resident(shape):
    return pl.BlockSpec(shape, lambda *_: (0,) * len(shape), pipeline_mode=pl.Buffered(1))


def _mod_kernel(c_ref, w0_ref, w1_ref, b_ref, o_ref):
    c = c_ref[...]
    s = (c * jax.nn.sigmoid(c)).astype(_BF16)
    k = w0_ref.shape[0]
    o_ref[...] = (_dot(s[:, :k], w0_ref[...].astype(_BF16))
                  + _dot(s[:, k:], w1_ref[...].astype(_BF16)) + b_ref[...])


def _mod_call(cond8, w_mod, b_mod):
    n = w_mod.shape[1]
    tn = 1024
    kh = D_MODEL // 2
    return pl.pallas_call(
        _mod_kernel,
        grid=(n // tn,),
        in_specs=[
            pl.BlockSpec((8, D_MODEL), lambda i: (0, 0)),
            pl.BlockSpec((kh, tn), lambda i: (0, i)),
            pl.BlockSpec((kh, tn), lambda i: (1, i)),
            pl.BlockSpec((1, tn), lambda i: (0, i)),
        ],
        out_specs=pl.BlockSpec((8, tn), lambda i: (0, i)),
        out_shape=jax.ShapeDtypeStruct((8, n), _F32),
        compiler_params=pltpu.CompilerParams(dimension_semantics=("parallel",)),
        name="mod",
    )(cond8, w_mod, w_mod, b_mod.reshape(1, n))


def _prepare_in_weight(w_ref, w_scr):
    for c0 in list(range(0, C_QB, CAST_COLS)) + [C_KB]:
        n = min(CAST_COLS, IN_WIDTH - c0)
        w_scr[:, c0:c0 + n] = w_ref[:, c0:c0 + n].astype(_BF16)
    lo = lax.broadcasted_iota(jnp.int32, (D_MODEL, LANES), 1) < HEAD_DIM
    for j in range(G_B):
        h_lo, h_hi = _QB_HEAD_ORDER[2 * j], _QB_HEAD_ORDER[2 * j + 1]

        def head_at(h, want_hi):
            blk = w_ref[:, C_QB + LANES * (h // 2):C_QB + LANES * (h // 2 + 1)]
            return blk if (h % 2 == 1) == want_hi else pltpu.roll(blk, HEAD_DIM, 1)

        w_scr[:, C_QB + LANES * j:C_QB + LANES * (j + 1)] = jnp.where(
            lo, head_at(h_lo, False), head_at(h_hi, True)).astype(_BF16)


def _rope(z, cos, sin):
    lane = lax.broadcasted_iota(jnp.int32, z.shape, 1)
    partner = jnp.where((lane & ROPE_HALF) == 0,
                        pltpu.roll(z, LANES - ROPE_HALF, 1), pltpu.roll(z, ROPE_HALF, 1))
    return z * cos + partner * sin


def _inproj_kernel(*refs, rope, kv_transposed):
    if rope:
        x_ref, mod_ref, g_ref, w_ref, cos_ref, sin_ref, q_ref, ka_ref, va_ref, kb_ref, vb_ref, w_scr = refs
    else:
        x_ref, mod_ref, g_ref, w_ref, q_ref, ka_ref, va_ref, kb_ref, vb_ref, w_scr = refs

    @pl.when(pl.program_id(0) == 0)
    def _():
        _prepare_in_weight(w_ref, w_scr)

    mod = mod_ref[0]
    sh1 = mod[:, 0:D_MODEL]
    sc1 = mod[:, D_MODEL:2 * D_MODEL]
    h = (_rms(x_ref[...], g_ref[...]) * (1 + sc1) + sh1).astype(_BF16)

    def proj(c0, n):
        return _dot(h, w_scr[:, c0:c0 + n])

    def put_kv(ref, z):
        if not kv_transposed:
            ref[...] = z.astype(ref.dtype)
            return
        nseq, _, l = ref.shape
        for b in range(nseq):
            for c0 in range(0, z.shape[1], LANES):
                ref[b, c0:c0 + LANES, :] = z[b * l:(b + 1) * l, c0:c0 + LANES].T.astype(ref.dtype)

    q_ref[:, 0:W_A] = (proj(C_QA, W_A) * Q_SCALE).astype(q_ref.dtype)
    put_kv(ka_ref, proj(C_KA, W_A))
    put_kv(va_ref, proj(C_VA, W_A))
    put_kv(vb_ref, proj(C_VB, KV_W_B))
    if rope:
        cos = cos_ref[...]
        sin = sin_ref[...]
        for j in range(W_B // LANES):
            zq = proj(C_QB + LANES * j, LANES)
            q_ref[:, W_A + LANES * j:W_A + LANES * (j + 1)] = (
                _rope(zq, cos, sin) * Q_SCALE).astype(q_ref.dtype)
        put_kv(kb_ref, _rope(proj(C_KB, KV_W_B), cos, sin))
    else:
        q_ref[:, W_A:W_A + W_B] = (proj(C_QB, W_B) * Q_SCALE).astype(q_ref.dtype)
        put_kv(kb_ref, proj(C_KB, KV_W_B))


def _inproj_call(x2d, mod3, mod_row, g_pre, w_in, rope_tabs, seq, kv_transposed, name):
    t = x2d.shape[0]
    tm = TOKEN_TILE
    rope = rope_tabs is not None
    in_specs = [
        pl.BlockSpec((tm, D_MODEL), lambda i: (i, 0)),
        pl.BlockSpec((1, 1, 6 * D_MODEL), lambda i: (mod_row(i), 0, 0)),
        pl.BlockSpec((1, D_MODEL), lambda i: (0, 0)),
        _resident((D_MODEL, IN_WIDTH)),
    ]
    args = [x2d, mod3, g_pre.reshape(1, D_MODEL), w_in]
    if rope:
        tiles_per_seq = rope_tabs[0].shape[0] // tm
        in_specs += [pl.BlockSpec((tm, LANES), lambda i: (i % tiles_per_seq, 0))] * 2
        args += list(rope_tabs)
    if kv_transposed:
        nseq = tm // seq
        kv_shape = lambda w: jax.ShapeDtypeStruct((t // seq, w, seq), _F32)
        kv_spec = lambda w: pl.BlockSpec((nseq, w, seq), lambda i: (i, 0, 0))
    else:
        kv_shape = lambda w: jax.ShapeDtypeStruct((t, w), _BF16)
        kv_spec = lambda w: pl.BlockSpec((tm, w), lambda i: (i, 0))
    out_shape = (jax.ShapeDtypeStruct((t, W_A + W_B), _BF16),
                 kv_shape(W_A), kv_shape(W_A), kv_shape(KV_W_B), kv_shape(KV_W_B))
    out_specs = (pl.BlockSpec((tm, W_A + W_B), lambda i: (i, 0)),
                 kv_spec(W_A), kv_spec(W_A), kv_spec(KV_W_B), kv_spec(KV_W_B))
    return pl.pallas_call(
        functools.partial(_inproj_kernel, rope=rope, kv_transposed=kv_transposed),
        grid=(t // tm,),
        in_specs=in_specs,
        out_specs=out_specs,
        out_shape=out_shape,
        scratch_shapes=[pltpu.VMEM((D_MODEL, IN_WIDTH), _BF16)],
        compiler_params=pltpu.CompilerParams(
            dimension_semantics=("arbitrary",), vmem_limit_bytes=VMEM_LIMIT),
        name=name,
    )(*args)


def _ctx_attn_kernel(sink_ref, q_ref, ka_ref, va_ref, kb_ref, vb_ref, oa_ref, ob_ref):
    l = q_ref.shape[0]
    lo = lax.broadcasted_iota(jnp.int32, (l, LANES), 1) < HEAD_DIM
    row_lo = lax.broadcasted_iota(jnp.int32, (2 * l, 1), 0) < l
    for p in range(W_A // LANES):
        sl = slice(LANES * p, LANES * (p + 1))
        qs = _stack_heads(q_ref[:, sl], lo)
        s = _dot(qs, ka_ref[0, sl, :].astype(_BF16))
        vt = va_ref[0, sl, :].astype(_BF16)
        o2 = _softmax_pv([s], [lambda e, vt=vt: _dot_nt(e, vt)])
        oa_ref[:, sl] = _unstack_heads(o2, lo)
    kb = kb_ref[0].astype(_BF16)
    vb = vb_ref[0].astype(_BF16)
    for j in range(G_B):
        sl = slice(LANES * j, LANES * (j + 1))
        qs = _stack_heads(q_ref[:, W_A + LANES * j:W_A + LANES * (j + 1)], lo)
        s = _dot(qs, kb)
        sink = jnp.where(row_lo, sink_ref[0, j], sink_ref[1, j])
        o2 = _softmax_pv([s], [lambda e: _dot_nt(e, vb)], sink)
        ob_ref[:, sl] = _unstack_heads(o2, lo)


def _ctx_attn_call(sink, q, ka, va, kb, vb, seq):
    t = q.shape[0]
    row = lambda b: (b, 0)
    seq3 = lambda b: (b, 0, 0)
    return pl.pallas_call(
        _ctx_attn_kernel,
        grid=(t // seq,),
        in_specs=[
            pl.BlockSpec(memory_space=pltpu.SMEM),
            pl.BlockSpec((seq, W_A + W_B), row),
            pl.BlockSpec((1, W_A, seq), seq3),
            pl.BlockSpec((1, W_A, seq), seq3),
            pl.BlockSpec((1, KV_W_B, seq), seq3),
            pl.BlockSpec((1, KV_W_B, seq), seq3),
        ],
        out_specs=(pl.BlockSpec((seq, W_A), row), pl.BlockSpec((seq, W_B), row)),
        out_shape=(jax.ShapeDtypeStruct((t, W_A), _F32), jax.ShapeDtypeStruct((t, W_B), _F32)),
        compiler_params=pltpu.CompilerParams(
            dimension_semantics=("parallel",), vmem_limit_bytes=VMEM_LIMIT),
        name="ctx_attn",
    )(sink, q, ka, va, kb, vb)


def _na_kernel(q_ref, k_ref, v_ref, ck_ref, cv_ref, rpb_ref, o_ref, bias_ref):
    rows = q_ref.shape[0] // GRID_W
    kr = min(NA_ROWS, rows)
    lane = lax.broadcasted_iota(jnp.int32, (GRID_W, LANES), 1)
    lo = lane < HEAD_DIM
    cq = lax.broadcasted_iota(jnp.int32, (GRID_W, LANES), 0)
    ck = lane & (GRID_W - 1)
    cs = jnp.clip(cq - NA_COLS // 2, 0, GRID_W - NA_COLS)
    valid = (ck >= cs) & (ck < cs + NA_COLS)

    n_drow = 2 * NA_ROWS - 1
    for hh in range(2):
        t_lo, t_hi = [], []
        for d in range(n_drow):
            base = jnp.broadcast_to(rpb_ref[hh, d:d + 1, :], (GRID_W, LANES))
            t_lo.append(pltpu.roll(base, LANES - (NA_COLS - 1), 1, stride=1, stride_axis=0))
            t_hi.append(pltpu.roll(base, GRID_W - (NA_COLS - 1), 1, stride=1, stride_axis=0))
        pairs = {}
        for d0 in range(NA_ROWS):
            for j in range(kr // 2):
                d = d0 + 2 * j
                if d not in pairs:
                    pairs[d] = jnp.where(valid, jnp.where(lo, t_lo[d], t_hi[d + 1]), NEG_INF)
                bias_ref[d0, GRID_W * hh:GRID_W * (hh + 1), LANES * j:LANES * (j + 1)] = pairs[d]

    ckt = ck_ref[0].astype(_BF16)
    cvt = cv_ref[0].astype(_BF16)

    def body(r, carry):
        rs = jnp.clip(r - kr // 2, 0, rows - kr)
        d0 = rs - r + (NA_ROWS - 1)
        q0 = pl.multiple_of(r * GRID_W, GRID_W)
        k0 = pl.multiple_of(rs * GRID_W, GRID_W)
        qs = _stack_heads(q_ref[pl.ds(q0, GRID_W), :], lo)
        kw = k_ref[pl.ds(k0, kr * GRID_W), :]
        vw = v_ref[pl.ds(k0, kr * GRID_W), :]
        s_nb = _dot_nt(qs, kw) + bias_ref[d0]
        s_cx = _dot(qs, ckt)
        o2 = _softmax_pv([s_nb, s_cx], [lambda e: _dot(e, vw), lambda e: _dot_nt(e, cvt)])
        o_ref[pl.ds(q0, GRID_W), :] = _unstack_heads(o2, lo)
        return carry

    lax.fori_loop(0, rows, body, 0)


def _na_call(q, k, v, ckt, cvt, rpb_pad, nb, seq):
    past = ckt.shape[2]
    kr = min(NA_ROWS, seq // GRID_W)
    blk = lambda b, p: (b, p)
    return pl.pallas_call(
        _na_kernel,
        grid=(nb, W_A // LANES),
        in_specs=[
            pl.BlockSpec((seq, LANES), blk),
            pl.BlockSpec((seq, LANES), blk),
            pl.BlockSpec((seq, LANES), blk),
            pl.BlockSpec((1, LANES, past), lambda b, p: (b, p, 0)),
            pl.BlockSpec((1, LANES, past), lambda b, p: (b, p, 0)),
            pl.BlockSpec((2, 2 * NA_ROWS - 1, LANES), lambda b, p: (p, 0, 0)),
        ],
        out_specs=pl.BlockSpec((seq, LANES), blk),
        out_shape=jax.ShapeDtypeStruct((nb * seq, W_A), _F32),
        scratch_shapes=[pltpu.VMEM((NA_ROWS, 2 * GRID_W, kr * GRID_W), _F32)],
        compiler_params=pltpu.CompilerParams(
            dimension_semantics=("parallel", "parallel"), vmem_limit_bytes=VMEM_LIMIT),
        name="na_attn",
    )(q, k, v, ckt, cvt, rpb_pad)


def _win_kernel(sink_ref, q_ref, k_ref, v_ref, ck_ref, cv_ref, o_ref):
    t = q_ref.shape[0]
    nblk = t // BLOCK
    band = 3 * BLOCK
    j = pl.program_id(1)
    lo = lax.broadcasted_iota(jnp.int32, (BLOCK, LANES), 1) < HEAD_DIM
    row_lo = lax.broadcasted_iota(jnp.int32, (2 * BLOCK, 1), 0) < BLOCK
    sink = jnp.where(row_lo, sink_ref[0, j], sink_ref[1, j])
    qi = lax.broadcasted_iota(jnp.int32, (2 * BLOCK, band), 0) & (BLOCK - 1)
    kj = lax.broadcasted_iota(jnp.int32, (2 * BLOCK, band), 1)
    rel = qi - kj
    ckt = ck_ref[0].astype(_BF16)
    cvt = cv_ref[0].astype(_BF16)

    def body(n, carry):
        start = jnp.clip((n - 1) * BLOCK, 0, t - band)
        q0 = pl.multiple_of(n * BLOCK, BLOCK)
        k0 = pl.multiple_of(start, BLOCK)
        qs = _stack_heads(q_ref[pl.ds(q0, BLOCK), :], lo)
        s = _dot_nt(qs, k_ref[pl.ds(k0, band), :])
        s = jnp.where(jnp.abs(rel + (n * BLOCK - start)) <= SWA_WINDOW, s, NEG_INF)
        s_cx = _dot(qs, ckt)
        vband = v_ref[pl.ds(k0, band), :]
        o2 = _softmax_pv([s, s_cx], [lambda e: _dot(e, vband), lambda e: _dot_nt(e, cvt)], sink)
        o_ref[pl.ds(q0, BLOCK), :] = _unstack_heads(o2, lo)
        return carry

    lax.fori_loop(0, nblk, body, 0)


def _win_call(sink, q, k, v, ckt, cvt, nb, seq):
    past = ckt.shape[2]
    return pl.pallas_call(
        _win_kernel,
        grid=(nb, G_B),
        in_specs=[
            pl.BlockSpec(memory_space=pltpu.SMEM),
            pl.BlockSpec((seq, LANES), lambda b, j: (b, W_A // LANES + j)),
            pl.BlockSpec((seq, LANES), lambda b, j: (b, 0)),
            pl.BlockSpec((seq, LANES), lambda b, j: (b, 0)),
            pl.BlockSpec((1, LANES, past), lambda b, j: (b, 0, 0)),
            pl.BlockSpec((1, LANES, past), lambda b, j: (b, 0, 0)),
        ],
        out_specs=pl.BlockSpec((seq, LANES), lambda b, j: (b, j)),
        out_shape=jax.ShapeDtypeStruct((nb * seq, W_B), _F32),
        compiler_params=pltpu.CompilerParams(
            dimension_semantics=("parallel", "parallel"), vmem_limit_bytes=VMEM_LIMIT),
        name="win_attn",
    )(sink, q, k, v, ckt, cvt)


def _prepare_out_weight(w_ref, w_scr):
    for r0 in range(0, W_A, CAST_COLS):
        w_scr[r0:r0 + CAST_COLS, :] = w_ref[r0:r0 + CAST_COLS, :].astype(_BF16)
    for i, h in enumerate(_QB_HEAD_ORDER):
        w_scr[W_A + HEAD_DIM * i:W_A + HEAD_DIM * (i + 1), :] = (
            w_ref[W_A + HEAD_DIM * h:W_A + HEAD_DIM * (h + 1), :].astype(_BF16))


def _merge_kernel(oa_ref, ob_ref, x_ref, mod_ref, ga_ref, gb_ref, gpost_ref, gffn_ref, w_ref,
                  x1_ref, h2_ref, w_scr):
    @pl.when(pl.program_id(0) == 0)
    def _():
        _prepare_out_weight(w_ref, w_scr)

    mod = mod_ref[0]
    gt1 = mod[:, 2 * D_MODEL:3 * D_MODEL]
    sh2 = mod[:, 3 * D_MODEL:4 * D_MODEL]
    sc2 = mod[:, 4 * D_MODEL:5 * D_MODEL]
    a = _rms(oa_ref[...], ga_ref[...]).astype(_BF16)
    b = _rms(ob_ref[...], gb_ref[...]).astype(_BF16)
    y = _dot(a, w_scr[0:W_A, :]) + _dot(b, w_scr[W_A:W_A + W_B, :])
    x1 = x_ref[...] + gt1 * _rms(y, gpost_ref[...])
    x1_ref[...] = x1
    h2_ref[...] = (_rms(x1, gffn_ref[...]) * (1 + sc2) + sh2).astype(h2_ref.dtype)


def _merge_call(oa, ob, x2d, mod3, mod_row, g_a, g_b, g_post, g_ffn, w_out, name):
    t = x2d.shape[0]
    tm = TOKEN_TILE
    row = lambda i: (i, 0)
    const = lambda i: (0, 0)
    return pl.pallas_call(
        _merge_kernel,
        grid=(t // tm,),
        in_specs=[
            pl.BlockSpec((tm, W_A), row),
            pl.BlockSpec((tm, W_B), row),
            pl.BlockSpec((tm, D_MODEL), row),
            pl.BlockSpec((1, 1, 6 * D_MODEL), lambda i: (mod_row(i), 0, 0)),
            pl.BlockSpec((1, W_A), const),
            pl.BlockSpec((1, W_B), const),
            pl.BlockSpec((1, D_MODEL), const),
            pl.BlockSpec((1, D_MODEL), const),
            _resident((W_A + W_B, D_MODEL)),
        ],
        out_specs=(pl.BlockSpec((tm, D_MODEL), row), pl.BlockSpec((tm, D_MODEL), row)),
        out_shape=(jax.ShapeDtypeStruct((t, D_MODEL), _F32), jax.ShapeDtypeStruct((t, D_MODEL), _BF16)),
        scratch_shapes=[pltpu.VMEM((W_A + W_B, D_MODEL), _BF16)],
        compiler_params=pltpu.CompilerParams(
            dimension_semantics=("arbitrary",), vmem_limit_bytes=VMEM_LIMIT),
        name=name,
    )(oa, ob, x2d, mod3, g_a.reshape(1, W_A), g_b.reshape(1, W_B), g_post.reshape(1, D_MODEL),
      g_ffn.reshape(1, D_MODEL), w_out)


def _ffn_kernel(h2_ref, x1_ref, mod_ref, gpost_ref, wg_ref, wv_ref, cwg_ref, cwv_ref, cbg_ref, cbv_ref,
                wd_ref, out_ref, act_ref, *, seq_len):
    c = pl.program_id(1)
    tm = h2_ref.shape[0]
    pos = lax.broadcasted_iota(jnp.int32, (tm, 1), 0) & (seq_len - 1)
    first = pos == 0
    last = pos == seq_len - 1

    def conv(u, cw_ref, cb_ref):
        prev = jnp.where(first, 0.0, pltpu.roll(u, 1, 0))
        nxt = jnp.where(last, 0.0, pltpu.roll(u, tm - 1, 0))
        return prev * cw_ref[0:1, :] + u * cw_ref[1:2, :] + nxt * cw_ref[2:3, :] + cb_ref[...]

    def up(w_ref):
        w = w_ref[...]
        return jnp.concatenate(
            [_dot(h2_ref[r:r + FFN_ROW_BLOCK, :], w) for r in range(0, tm, FFN_ROW_BLOCK)], axis=0)

    gate = conv(up(wg_ref), cwg_ref, cbg_ref)
    val = conv(up(wv_ref), cwv_ref, cbv_ref)
    tc = gate.shape[1]
    act_ref[:, pl.ds(pl.multiple_of(c * tc, tc), tc)] = (gate * jax.nn.sigmoid(gate) * val).astype(_BF16)

    @pl.when(c == pl.num_programs(1) - 1)
    def _():
        gt2 = mod_ref[0][:, 5 * D_MODEL:6 * D_MODEL]
        y = _dot(act_ref[...], wd_ref[...])
        out_ref[...] = x1_ref[...] + gt2 * _rms(y, gpost_ref[...])


def _ffn_call(h2, x1, mod3, mod_row, g_post, w_up, conv_w, conv_b, w_down, seq_len, name):
    t = h2.shape[0]
    tm = FFN_TILE
    tc = FF_CHUNK
    nc = D_FF // tc
    row = lambda i, c: (i, 0)
    return pl.pallas_call(
        functools.partial(_ffn_kernel, seq_len=seq_len),
        grid=(t // tm, nc),
        in_specs=[
            pl.BlockSpec((tm, D_MODEL), row),
            pl.BlockSpec((tm, D_MODEL), row),
            pl.BlockSpec((1, 1, 6 * D_MODEL), lambda i, c: (mod_row(i), 0, 0)),
            pl.BlockSpec((1, D_MODEL), lambda i, c: (0, 0)),
            pl.BlockSpec((D_MODEL, tc), lambda i, c: (0, c)),
            pl.BlockSpec((D_MODEL, tc), lambda i, c: (0, nc + c)),
            pl.BlockSpec((3, tc), lambda i, c: (0, c)),
            pl.BlockSpec((3, tc), lambda i, c: (0, nc + c)),
            pl.BlockSpec((1, tc), lambda i, c: (0, c)),
            pl.BlockSpec((1, tc), lambda i, c: (0, nc + c)),
            _resident((D_FF, D_MODEL)),
        ],
        out_specs=pl.BlockSpec((tm, D_MODEL), row),
        out_shape=jax.ShapeDtypeStruct((t, D_MODEL), _F32),
        scratch_shapes=[pltpu.VMEM((tm, D_FF), _BF16)],
        compiler_params=pltpu.CompilerParams(
            dimension_semantics=("parallel", "arbitrary"), vmem_limit_bytes=VMEM_LIMIT),
        name=name,
    )(h2, x1, mod3, g_post.reshape(1, D_MODEL), w_up, w_up, conv_w, conv_w,
      conv_b.reshape(1, 2 * D_FF), conv_b.reshape(1, 2 * D_FF), w_down)


def _rope_tables(seq):
    n = ROPE_HALF
    t = np.arange(seq)
    lane = np.arange(HEAD_DIM)
    pos = np.where(lane[None, :] < HEAD_DIM // 2, (t // GRID_W)[:, None], (t % GRID_W)[:, None])
    inv = 1.0 / (ROPE_THETA ** (np.arange(n, dtype=np.float64) / n))
    ang = pos.astype(np.float64) * inv[lane % n][None, :]
    sign = np.where((lane & n) == 0, -1.0, 1.0)[None, :]
    cos = np.tile(np.cos(ang), (1, LANES // HEAD_DIM)).astype(np.float32)
    sin = np.tile(np.sin(ang) * sign, (1, LANES // HEAD_DIM)).astype(np.float32)
    return jnp.asarray(cos), jnp.asarray(sin)


def _to_head_dim_token(cache_l):
    b, l, h, d = cache_l.shape
    return jnp.transpose(cache_l, (0, 2, 3, 1)).reshape(b, h * d, l)


def _from_head_dim_token(kv_t, heads):
    b, _, l = kv_t.shape
    return jnp.transpose(kv_t.reshape(b, 1, heads, HEAD_DIM, l), (0, 1, 4, 2, 3))


def kernel(x_prompt, x_sample, cache_a_k, cache_a_v, cache_b_k, cache_b_v, c, c_ctx, w_mod, b_mod,
           g_mix_pre, g_mix_post, g_ffn_pre, g_ffn_post, w_in, rpb_a, sink_b, g_grp_a, g_grp_b,
           w_out, w_up, conv_w, conv_b, w_down):
    bp, lp, _ = x_prompt.shape
    bs, ts, _ = x_sample.shape
    assert w_in.shape[0] == 1
    xp = x_prompt.reshape(bp * lp, D_MODEL)
    xs = x_sample.reshape(bs * ts, D_MODEL)
    cond8 = jnp.concatenate([c_ctx[None], c, jnp.zeros((8 - 1 - bs, D_MODEL), _F32)], axis=0)
    cos_tab, sin_tab = _rope_tables(ts)

    l = 0
    mod3 = _mod_call(cond8, w_mod[l], b_mod[l]).reshape(8, 1, 6 * D_MODEL)
    g_b_p = g_grp_b[l].reshape(KV_B, G_B, HEAD_DIM).transpose(1, 0, 2).reshape(W_B)
    w_up_b = w_up[l].astype(_BF16)
    w_down_b = w_down[l].astype(_BF16)
    rpb_pad = jnp.pad(rpb_a[l], ((0, 0), (0, 0), (0, LANES - (2 * NA_COLS - 1))))

    prompt_row = lambda i: 0
    s_tiles = ts // TOKEN_TILE
    sample_row = lambda i: 1 + i // s_tiles
    sample_row_ffn = lambda i: 1 + i // (ts // FFN_TILE)

    q_p, kat_p, vat_p, kbt_p, vbt_p = _inproj_call(xp, mod3, prompt_row, g_mix_pre[l], w_in[l], None, lp,
                                                   True, "inproj_prompt")
    oa_p, ob_p = _ctx_attn_call(sink_b[l], q_p, kat_p, vat_p, kbt_p, vbt_p, lp)
    x1_p, h2_p = _merge_call(oa_p, ob_p, xp, mod3, prompt_row, g_grp_a[l], g_b_p, g_mix_post[l],
                             g_ffn_pre[l], w_out[l], "merge_prompt")
    y_p = _ffn_call(h2_p, x1_p, mod3, prompt_row, g_ffn_post[l], w_up_b, conv_w[l], conv_b[l], w_down_b,
                    lp, "ffn_prompt")

    q_s, ka_s, va_s, kb_s, vb_s = _inproj_call(xs, mod3, sample_row, g_mix_pre[l], w_in[l],
                                               (cos_tab, sin_tab), ts, False, "inproj_sample")
    oa_s = _na_call(q_s, ka_s, va_s, _to_head_dim_token(cache_a_k[:, l]), _to_head_dim_token(cache_a_v[:, l]),
                    rpb_pad, bs, ts)
    ob_s = _win_call(sink_b[l], q_s, kb_s, vb_s, _to_head_dim_token(cache_b_k[:, l]),
                     _to_head_dim_token(cache_b_v[:, l]), bs, ts)
    x1_s, h2_s = _merge_call(oa_s, ob_s, xs, mod3, sample_row, g_grp_a[l], g_b_p, g_mix_post[l],
                             g_ffn_pre[l], w_out[l], "merge_sample")
    y_s = _ffn_call(h2_s, x1_s, mod3, sample_row_ffn, g_ffn_post[l], w_up_b, conv_w[l], conv_b[l],
                    w_down_b, ts, "ffn_sample")

    return (y_p.reshape(bp, lp, D_MODEL), y_s.reshape(bs, ts, D_MODEL),
            _from_head_dim_token(kat_p, H_A), _from_head_dim_token(vat_p, H_A),
            _from_head_dim_token(kbt_p, KV_B), _from_head_dim_token(vbt_p, KV_B))
```

```python
import functools

import numpy as np
import jax
import jax.numpy as jnp
from jax import lax
from jax.experimental import pallas as pl
from jax.experimental.pallas import tpu as pltpu

D_MODEL = 1024
HEAD_DIM = 64
H_A = 8
H_B = 8
KV_B = 2
G_B = H_B // KV_B
W_A = H_A * HEAD_DIM
W_B = H_B * HEAD_DIM
KV_W_B = KV_B * HEAD_DIM
IN_WIDTH = 3 * W_A + W_B + 2 * KV_W_B
GRID_W = 64
NA_ROWS = 8
NA_COLS = 16
SWA_WINDOW = 128
BLOCK = 128
D_FF = 2816
ROPE_THETA = 10000.0
EPS = 1e-6
NEG_INF = -1e30
Q_SCALE = HEAD_DIM ** -0.5
ROPE_HALF = HEAD_DIM // 4

LANES = 128
FF_CHUNK = 256
TOKEN_TILE = 512
FFN_TILE = 1024
FFN_ROW_BLOCK = 256
ATTN_UNROLL = 4
CAST_COLS = 512
VMEM_LIMIT = 48 * 1024 * 1024
TAIL_VMEM_LIMIT = 56 * 1024 * 1024

C_QA, C_KA, C_VA, C_QB, C_KB, C_VB = 0, W_A, 2 * W_A, 3 * W_A, 3 * W_A + W_B, 3 * W_A + W_B + KV_W_B

_QB_HEAD_ORDER = tuple(kv * G_B + g for g in range(G_B) for kv in range(KV_B))

_BF16 = jnp.bfloat16
_F32 = jnp.float32


def _dot(a, b):
    return jnp.dot(a, b, preferred_element_type=_F32)


def _dot_nt(a, b):
    return lax.dot_general(a, b, (((1,), (1,)), ((), ())), preferred_element_type=_F32)


def _rms(x, g):
    var = jnp.mean(x * x, axis=-1, keepdims=True)
    return x * lax.rsqrt(var + EPS) * g


def _stack_heads(q2, lo):
    zero = jnp.zeros_like(q2)
    return jnp.concatenate([jnp.where(lo, q2, zero), jnp.where(lo, zero, q2)], axis=0)


def _unstack_heads(o2, lo):
    m = o2.shape[0] // 2
    return jnp.where(lo, o2[:m], o2[m:])


def _softmax_pv(scores, pv, sink=None):
    m = scores[0].max(axis=-1, keepdims=True)
    for s in scores[1:]:
        m = jnp.maximum(m, s.max(axis=-1, keepdims=True))
    if sink is not None:
        m = jnp.maximum(m, sink)
    denom = None
    out = None
    for s, f in zip(scores, pv):
        e = jnp.exp(s - m)
        l = e.sum(axis=-1, keepdims=True)
        o = f(e.astype(_BF16))
        denom = l if denom is None else denom + l
        out = o if out is None else out + o
    if sink is not None:
        denom = denom + jnp.exp(sink - m)
    return out / denom


def _resident(shape):
    return pl.BlockSpec(shape, lambda *_: (0,) * len(shape), pipeline_mode=pl.Buffered(1))


def _mod_kernel(c_ref, w0_ref, w1_ref, b_ref, o_ref):
    c = c_ref[...]
    s = (c * jax.nn.sigmoid(c)).astype(_BF16)
    k = w0_ref.shape[0]
    o_ref[...] = (_dot(s[:, :k], w0_ref[...].astype(_BF16))
                  + _dot(s[:, k:], w1_ref[...].astype(_BF16)) + b_ref[...])


def _mod_call(cond8, w_mod, b_mod):
    n = w_mod.shape[1]
    tn = 1024
    kh = D_MODEL // 2
    return pl.pallas_call(
        _mod_kernel,
        grid=(n // tn,),
        in_specs=[
            pl.BlockSpec((8, D_MODEL), lambda i: (0, 0)),
            pl.BlockSpec((kh, tn), lambda i: (0, i)),
            pl.BlockSpec((kh, tn), lambda i: (1, i)),
            pl.BlockSpec((1, tn), lambda i: (0, i)),
        ],
        out_specs=pl.BlockSpec((8, tn), lambda i: (0, i)),
        out_shape=jax.ShapeDtypeStruct((8, n), _F32),
        compiler_params=pltpu.CompilerParams(dimension_semantics=("parallel",)),
        name="mod",
    )(cond8, w_mod, w_mod, b_mod.reshape(1, n))


def _prepare_in_weight(w_ref, w_scr):
    for c0 in list(range(0, C_QB, CAST_COLS)) + [C_KB]:
        n = min(CAST_COLS, IN_WIDTH - c0)
        w_scr[:, c0:c0 + n] = w_ref[:, c0:c0 + n].astype(_BF16)
    lo = lax.broadcasted_iota(jnp.int32, (D_MODEL, LANES), 1) < HEAD_DIM
    for j in range(G_B):
        h_lo, h_hi = _QB_HEAD_ORDER[2 * j], _QB_HEAD_ORDER[2 * j + 1]

        def head_at(h, want_hi):
            blk = w_ref[:, C_QB + LANES * (h // 2):C_QB + LANES * (h // 2 + 1)]
            return blk if (h % 2 == 1) == want_hi else pltpu.roll(blk, HEAD_DIM, 1)

        w_scr[:, C_QB + LANES * j:C_QB + LANES * (j + 1)] = jnp.where(
            lo, head_at(h_lo, False), head_at(h_hi, True)).astype(_BF16)


def _rope(z, cos, sin):
    lane = lax.broadcasted_iota(jnp.int32, z.shape, 1)
    partner = jnp.where((lane & ROPE_HALF) == 0,
                        pltpu.roll(z, LANES - ROPE_HALF, 1), pltpu.roll(z, ROPE_HALF, 1))
    return z * cos + partner * sin


def _inproj_kernel(*refs, rope, kv_transposed):
    if rope:
        x_ref, mod_ref, g_ref, w_ref, cos_ref, sin_ref, q_ref, ka_ref, va_ref, kb_ref, vb_ref, w_scr = refs
    else:
        x_ref, mod_ref, g_ref, w_ref, q_ref, ka_ref, va_ref, kb_ref, vb_ref, w_scr = refs

    @pl.when(pl.program_id(0) == 0)
    def _():
        _prepare_in_weight(w_ref, w_scr)

    mod = mod_ref[0]
    sh1 = mod[:, 0:D_MODEL]
    sc1 = mod[:, D_MODEL:2 * D_MODEL]
    h = (_rms(x_ref[...], g_ref[...]) * (1 + sc1) + sh1).astype(_BF16)

    def proj(c0, n):
        return _dot(h, w_scr[:, c0:c0 + n])

    def put_kv(ref, z):
        if not kv_transposed:
            ref[...] = z.astype(ref.dtype)
            return
        nseq, _, l = ref.shape
        for b in range(nseq):
            for c0 in range(0, z.shape[1], LANES):
                ref[b, c0:c0 + LANES, :] = z[b * l:(b + 1) * l, c0:c0 + LANES].T.astype(ref.dtype)

    q_ref[:, 0:W_A] = (proj(C_QA, W_A) * Q_SCALE).astype(q_ref.dtype)
    put_kv(ka_ref, proj(C_KA, W_A))
    put_kv(va_ref, proj(C_VA, W_A))
    put_kv(vb_ref, proj(C_VB, KV_W_B))
    if rope:
        cos = cos_ref[...]
        sin = sin_ref[...]
        for j in range(W_B // LANES):
            zq = proj(C_QB + LANES * j, LANES)
            q_ref[:, W_A + LANES * j:W_A + LANES * (j + 1)] = (
                _rope(zq, cos, sin) * Q_SCALE).astype(q_ref.dtype)
        put_kv(kb_ref, _rope(proj(C_KB, KV_W_B), cos, sin))
    else:
        q_ref[:, W_A:W_A + W_B] = (proj(C_QB, W_B) * Q_SCALE).astype(q_ref.dtype)
        put_kv(kb_ref, proj(C_KB, KV_W_B))


def _inproj_call(x2d, mod3, mod_row, g_pre, w_in, rope_tabs, seq, kv_transposed, name):
    t = x2d.shape[0]
    tm = TOKEN_TILE
    rope = rope_tabs is not None
    in_specs = [
        pl.BlockSpec((tm, D_MODEL), lambda i: (i, 0)),
        pl.BlockSpec((1, 1, 6 * D_MODEL), lambda i: (mod_row(i), 0, 0)),
        pl.BlockSpec((1, D_MODEL), lambda i: (0, 0)),
        _resident((D_MODEL, IN_WIDTH)),
    ]
    args = [x2d, mod3, g_pre.reshape(1, D_MODEL), w_in]
    if rope:
        tiles_per_seq = rope_tabs[0].shape[0] // tm
        in_specs += [pl.BlockSpec((tm, LANES), lambda i: (i % tiles_per_seq, 0))] * 2
        args += list(rope_tabs)
    if kv_transposed:
        nseq = tm // seq
        kv_shape = lambda w: jax.ShapeDtypeStruct((t // seq, w, seq), _F32)
        kv_spec = lambda w: pl.BlockSpec((nseq, w, seq), lambda i: (i, 0, 0))
    else:
        kv_shape = lambda w: jax.ShapeDtypeStruct((t, w), _BF16)
        kv_spec = lambda w: pl.BlockSpec((tm, w), lambda i: (i, 0))
    out_shape = (jax.ShapeDtypeStruct((t, W_A + W_B), _BF16),
                 kv_shape(W_A), kv_shape(W_A), kv_shape(KV_W_B), kv_shape(KV_W_B))
    out_specs = (pl.BlockSpec((tm, W_A + W_B), lambda i: (i, 0)),
                 kv_spec(W_A), kv_spec(W_A), kv_spec(KV_W_B), kv_spec(KV_W_B))
    return pl.pallas_call(
        functools.partial(_inproj_kernel, rope=rope, kv_transposed=kv_transposed),
        grid=(t // tm,),
        in_specs=in_specs,
        out_specs=out_specs,
        out_shape=out_shape,
        scratch_shapes=[pltpu.VMEM((D_MODEL, IN_WIDTH), _BF16)],
        compiler_params=pltpu.CompilerParams(
            dimension_semantics=("arbitrary",), vmem_limit_bytes=VMEM_LIMIT),
        name=name,
    )(*args)


def _ctx_attn_kernel(sink_ref, q_ref, ka_ref, va_ref, kb_ref, vb_ref, ga_ref, gb_ref, a_ref, b_ref):
    l = q_ref.shape[0]
    lo = lax.broadcasted_iota(jnp.int32, (l, LANES), 1) < HEAD_DIM
    row_lo = lax.broadcasted_iota(jnp.int32, (2 * l, 1), 0) < l
    oa = []
    for p in range(W_A // LANES):
        sl = slice(LANES * p, LANES * (p + 1))
        qs = _stack_heads(q_ref[:, sl], lo)
        s = _dot(qs, ka_ref[0, sl, :].astype(_BF16))
        vt = va_ref[0, sl, :].astype(_BF16)
        oa.append(_unstack_heads(_softmax_pv([s], [lambda e, vt=vt: _dot_nt(e, vt)]), lo))
    a_ref[...] = _rms(jnp.concatenate(oa, axis=1), ga_ref[...]).astype(a_ref.dtype)
    kb = kb_ref[0].astype(_BF16)
    vb = vb_ref[0].astype(_BF16)
    ob = []
    for j in range(G_B):
        qs = _stack_heads(q_ref[:, W_A + LANES * j:W_A + LANES * (j + 1)], lo)
        s = _dot(qs, kb)
        sink = jnp.where(row_lo, sink_ref[0, j], sink_ref[1, j])
        ob.append(_unstack_heads(_softmax_pv([s], [lambda e: _dot_nt(e, vb)], sink), lo))
    b_ref[...] = _rms(jnp.concatenate(ob, axis=1), gb_ref[...]).astype(b_ref.dtype)


def _ctx_attn_call(sink, q, ka, va, kb, vb, g_a, g_b, seq):
    t = q.shape[0]
    row = lambda b: (b, 0)
    seq3 = lambda b: (b, 0, 0)
    const = lambda b: (0, 0)
    return pl.pallas_call(
        _ctx_attn_kernel,
        grid=(t // seq,),
        in_specs=[
            pl.BlockSpec(memory_space=pltpu.SMEM),
            pl.BlockSpec((seq, W_A + W_B), row),
            pl.BlockSpec((1, W_A, seq), seq3),
            pl.BlockSpec((1, W_A, seq), seq3),
            pl.BlockSpec((1, KV_W_B, seq), seq3),
            pl.BlockSpec((1, KV_W_B, seq), seq3),
            pl.BlockSpec((1, W_A), const),
            pl.BlockSpec((1, W_B), const),
        ],
        out_specs=(pl.BlockSpec((seq, W_A), row), pl.BlockSpec((seq, W_B), row)),
        out_shape=(jax.ShapeDtypeStruct((t, W_A), _BF16), jax.ShapeDtypeStruct((t, W_B), _BF16)),
        compiler_params=pltpu.CompilerParams(
            dimension_semantics=("parallel",), vmem_limit_bytes=VMEM_LIMIT),
        name="ctx_attn",
    )(sink, q, ka, va, kb, vb, g_a.reshape(1, W_A), g_b.reshape(1, W_B))


def _na_kernel(q_ref, k_ref, v_ref, ck_ref, cv_ref, rpb_ref, ga_ref, a_ref, bias_ref, o_scr):
    rows = q_ref.shape[0] // GRID_W
    kr = min(NA_ROWS, rows)
    lane = lax.broadcasted_iota(jnp.int32, (GRID_W, LANES), 1)
    lo = lane < HEAD_DIM
    cq = lax.broadcasted_iota(jnp.int32, (GRID_W, LANES), 0)
    ck = lane & (GRID_W - 1)
    cs = jnp.clip(cq - NA_COLS // 2, 0, GRID_W - NA_COLS)
    valid = (ck >= cs) & (ck < cs + NA_COLS)
    n_drow = 2 * NA_ROWS - 1

    for p in range(W_A // LANES):
        sl = slice(LANES * p, LANES * (p + 1))
        for hh in range(2):
            t_lo, t_hi = [], []
            for d in range(n_drow):
                base = jnp.broadcast_to(rpb_ref[2 * p + hh, d:d + 1, :], (GRID_W, LANES))
                t_lo.append(pltpu.roll(base, LANES - (NA_COLS - 1), 1, stride=1, stride_axis=0))
                t_hi.append(pltpu.roll(base, GRID_W - (NA_COLS - 1), 1, stride=1, stride_axis=0))
            pairs = {}
            for d0 in range(NA_ROWS):
                for j in range(kr // 2):
                    d = d0 + 2 * j
                    if d not in pairs:
                        pairs[d] = jnp.where(valid, jnp.where(lo, t_lo[d], t_hi[d + 1]), NEG_INF)
                    bias_ref[d0, GRID_W * hh:GRID_W * (hh + 1), LANES * j:LANES * (j + 1)] = pairs[d]

        ckt = ck_ref[0, sl, :].astype(_BF16)
        cvt = cv_ref[0, sl, :].astype(_BF16)

        def body(r, carry):
            rs = jnp.clip(r - kr // 2, 0, rows - kr)
            d0 = rs - r + (NA_ROWS - 1)
            q0 = pl.multiple_of(r * GRID_W, GRID_W)
            k0 = pl.multiple_of(rs * GRID_W, GRID_W)
            qs = _stack_heads(q_ref[pl.ds(q0, GRID_W), sl], lo)
            kw = k_ref[pl.ds(k0, kr * GRID_W), sl]
            vw = v_ref[pl.ds(k0, kr * GRID_W), sl]
            s_nb = _dot_nt(qs, kw) + bias_ref[d0]
            s_cx = _dot(qs, ckt)
            o2 = _softmax_pv([s_nb, s_cx], [lambda e: _dot(e, vw), lambda e: _dot_nt(e, cvt)])
            o_scr[pl.ds(q0, GRID_W), sl] = _unstack_heads(o2, lo)
            return carry

        lax.fori_loop(0, rows, body, 0, unroll=ATTN_UNROLL)

    a_ref[...] = _rms(o_scr[...], ga_ref[...]).astype(a_ref.dtype)


def _na_call(q, k, v, ckt, cvt, rpb_pad, g_a, nb, seq):
    past = ckt.shape[2]
    kr = min(NA_ROWS, seq // GRID_W)
    blk = lambda b: (b, 0)
    return pl.pallas_call(
        _na_kernel,
        grid=(nb,),
        in_specs=[
            pl.BlockSpec((seq, W_A), blk),
            pl.BlockSpec((seq, W_A), blk),
            pl.BlockSpec((seq, W_A), blk),
            pl.BlockSpec((1, W_A, past), lambda b: (b, 0, 0)),
            pl.BlockSpec((1, W_A, past), lambda b: (b, 0, 0)),
            pl.BlockSpec((H_A, 2 * NA_ROWS - 1, LANES), lambda b: (0, 0, 0)),
            pl.BlockSpec((1, W_A), lambda b: (0, 0)),
        ],
        out_specs=pl.BlockSpec((seq, W_A), blk),
        out_shape=jax.ShapeDtypeStruct((nb * seq, W_A), _BF16),
        scratch_shapes=[pltpu.VMEM((NA_ROWS, 2 * GRID_W, kr * GRID_W), _F32),
                        pltpu.VMEM((seq, W_A), _F32)],
        compiler_params=pltpu.CompilerParams(
            dimension_semantics=("parallel",), vmem_limit_bytes=VMEM_LIMIT),
        name="na_attn",
    )(q, k, v, ckt, cvt, rpb_pad, g_a.reshape(1, W_A))


def _win_kernel(sink_ref, q_ref, k_ref, v_ref, ck_ref, cv_ref, gb_ref, b_ref, o_scr):
    t = q_ref.shape[0]
    nblk = t // BLOCK
    band = 3 * BLOCK
    lo = lax.broadcasted_iota(jnp.int32, (BLOCK, LANES), 1) < HEAD_DIM
    row_lo = lax.broadcasted_iota(jnp.int32, (2 * BLOCK, 1), 0) < BLOCK
    qi = lax.broadcasted_iota(jnp.int32, (2 * BLOCK, band), 0) & (BLOCK - 1)
    kj = lax.broadcasted_iota(jnp.int32, (2 * BLOCK, band), 1)
    rel = qi - kj
    ckt = ck_ref[0].astype(_BF16)
    cvt = cv_ref[0].astype(_BF16)

    for j in range(G_B):
        sl = slice(LANES * j, LANES * (j + 1))
        sink = jnp.where(row_lo, sink_ref[0, j], sink_ref[1, j])

        def body(n, carry):
            start = jnp.clip((n - 1) * BLOCK, 0, t - band)
            q0 = pl.multiple_of(n * BLOCK, BLOCK)
            k0 = pl.multiple_of(start, BLOCK)
            qs = _stack_heads(q_ref[pl.ds(q0, BLOCK), sl], lo)
            s = _dot_nt(qs, k_ref[pl.ds(k0, band), :])
            s = jnp.where(jnp.abs(rel + (n * BLOCK - start)) <= SWA_WINDOW, s, NEG_INF)
            s_cx = _dot(qs, ckt)
            vband = v_ref[pl.ds(k0, band), :]
            o2 = _softmax_pv([s, s_cx], [lambda e: _dot(e, vband), lambda e: _dot_nt(e, cvt)], sink)
            o_scr[pl.ds(q0, BLOCK), sl] = _unstack_heads(o2, lo)
            return carry

        lax.fori_loop(0, nblk, body, 0, unroll=ATTN_UNROLL)

    b_ref[...] = _rms(o_scr[...], gb_ref[...]).astype(b_ref.dtype)


def _win_call(sink, q, k, v, ckt, cvt, g_b, nb, seq):
    past = ckt.shape[2]
    return pl.pallas_call(
        _win_kernel,
        grid=(nb,),
        in_specs=[
            pl.BlockSpec(memory_space=pltpu.SMEM),
            pl.BlockSpec((seq, W_B), lambda b: (b, W_A // W_B)),
            pl.BlockSpec((seq, KV_W_B), lambda b: (b, 0)),
            pl.BlockSpec((seq, KV_W_B), lambda b: (b, 0)),
            pl.BlockSpec((1, KV_W_B, past), lambda b: (b, 0, 0)),
            pl.BlockSpec((1, KV_W_B, past), lambda b: (b, 0, 0)),
            pl.BlockSpec((1, W_B), lambda b: (0, 0)),
        ],
        out_specs=pl.BlockSpec((seq, W_B), lambda b: (b, 0)),
        out_shape=jax.ShapeDtypeStruct((nb * seq, W_B), _BF16),
        scratch_shapes=[pltpu.VMEM((seq, W_B), _F32)],
        compiler_params=pltpu.CompilerParams(
            dimension_semantics=("parallel",), vmem_limit_bytes=VMEM_LIMIT),
        name="win_attn",
    )(sink, q, k, v, ckt, cvt, g_b.reshape(1, W_B))


def _prepare_out_weight(w_ref, w_scr):
    for r0 in range(0, W_A, CAST_COLS):
        w_scr[r0:r0 + CAST_COLS, :] = w_ref[r0:r0 + CAST_COLS, :].astype(_BF16)
    for i, h in enumerate(_QB_HEAD_ORDER):
        w_scr[W_A + HEAD_DIM * i:W_A + HEAD_DIM * (i + 1), :] = (
            w_ref[W_A + HEAD_DIM * h:W_A + HEAD_DIM * (h + 1), :].astype(_BF16))


def _tail_kernel(a_ref, b_ref, x_ref, mod_ref, gpost_ref, gffn_ref, gpost2_ref, wo_ref, wg_ref, wv_ref,
                 cwg_ref, cwv_ref, cbg_ref, cbv_ref, wd_ref, out_ref, wo_scr, x1_scr, h2_scr, act_scr,
                 *, seq_len):
    i = pl.program_id(0)
    c = pl.program_id(1)
    tm = x_ref.shape[0]
    mod = mod_ref[0]

    @pl.when((i == 0) & (c == 0))
    def _():
        _prepare_out_weight(wo_ref, wo_scr)

    @pl.when(c == 0)
    def _():
        gt1 = mod[:, 2 * D_MODEL:3 * D_MODEL]
        sh2 = mod[:, 3 * D_MODEL:4 * D_MODEL]
        sc2 = mod[:, 4 * D_MODEL:5 * D_MODEL]
        y = _dot(a_ref[...], wo_scr[0:W_A, :]) + _dot(b_ref[...], wo_scr[W_A:W_A + W_B, :])
        x1 = x_ref[...] + gt1 * _rms(y, gpost_ref[...])
        x1_scr[...] = x1
        h2_scr[...] = (_rms(x1, gffn_ref[...]) * (1 + sc2) + sh2).astype(h2_scr.dtype)

    pos = lax.broadcasted_iota(jnp.int32, (tm, 1), 0) & (seq_len - 1)
    first = pos == 0
    last = pos == seq_len - 1

    def conv(u, cw_ref, cb_ref):
        prev = jnp.where(first, 0.0, pltpu.roll(u, 1, 0))
        nxt = jnp.where(last, 0.0, pltpu.roll(u, tm - 1, 0))
        return prev * cw_ref[0:1, :] + u * cw_ref[1:2, :] + nxt * cw_ref[2:3, :] + cb_ref[...]

    def up(w_ref):
        w = w_ref[...]
        return jnp.concatenate(
            [_dot(h2_scr[r:r + FFN_ROW_BLOCK, :], w) for r in range(0, tm, FFN_ROW_BLOCK)], axis=0)

    gate = conv(up(wg_ref), cwg_ref, cbg_ref)
    val = conv(up(wv_ref), cwv_ref, cbv_ref)
    tc = gate.shape[1]
    act_scr[:, pl.ds(pl.multiple_of(c * tc, tc), tc)] = (gate * jax.nn.sigmoid(gate) * val).astype(_BF16)

    @pl.when(c == pl.num_programs(1) - 1)
    def _():
        gt2 = mod[:, 5 * D_MODEL:6 * D_MODEL]
        y = _dot(act_scr[...], wd_ref[...])
        out_ref[...] = x1_scr[...] + gt2 * _rms(y, gpost2_ref[...])


def _tail_call(a, b, x2d, mod3, mod_row, g_post, g_ffn, g_post2, w_out, w_up, conv_w, conv_b, w_down,
               seq_len, name):
    t = x2d.shape[0]
    tm = FFN_TILE
    tc = FF_CHUNK
    nc = D_FF // tc
    row = lambda i, c: (i, 0)
    const = lambda i, c: (0, 0)
    return pl.pallas_call(
        functools.partial(_tail_kernel, seq_len=seq_len),
        grid=(t // tm, nc),
        in_specs=[
            pl.BlockSpec((tm, W_A), row),
            pl.BlockSpec((tm, W_B), row),
            pl.BlockSpec((tm, D_MODEL), row),
            pl.BlockSpec((1, 1, 6 * D_MODEL), lambda i, c: (mod_row(i), 0, 0)),
            pl.BlockSpec((1, D_MODEL), const),
            pl.BlockSpec((1, D_MODEL), const),
            pl.BlockSpec((1, D_MODEL), const),
            _resident((W_A + W_B, D_MODEL)),
            pl.BlockSpec((D_MODEL, tc), lambda i, c: (0, c)),
            pl.BlockSpec((D_MODEL, tc), lambda i, c: (0, nc + c)),
            pl.BlockSpec((3, tc), lambda i, c: (0, c)),
            pl.BlockSpec((3, tc), lambda i, c: (0, nc + c)),
            pl.BlockSpec((1, tc), lambda i, c: (0, c)),
            pl.BlockSpec((1, tc), lambda i, c: (0, nc + c)),
            _resident((D_FF, D_MODEL)),
        ],
        out_specs=pl.BlockSpec((tm, D_MODEL), row),
        out_shape=jax.ShapeDtypeStruct((t, D_MODEL), _F32),
        scratch_shapes=[
            pltpu.VMEM((W_A + W_B, D_MODEL), _BF16),
            pltpu.VMEM((tm, D_MODEL), _F32),
            pltpu.VMEM((tm, D_MODEL), _BF16),
            pltpu.VMEM((tm, D_FF), _BF16),
        ],
        compiler_params=pltpu.CompilerParams(
            dimension_semantics=("arbitrary", "arbitrary"), vmem_limit_bytes=TAIL_VMEM_LIMIT),
        name=name,
    )(a, b, x2d, mod3, g_post.reshape(1, D_MODEL), g_ffn.reshape(1, D_MODEL), g_post2.reshape(1, D_MODEL),
      w_out, w_up, w_up, conv_w, conv_w, conv_b.reshape(1, 2 * D_FF), conv_b.reshape(1, 2 * D_FF), w_down)


def _rope_tables(seq):
    n = ROPE_HALF
    t = np.arange(seq)
    lane = np.arange(HEAD_DIM)
    pos = np.where(lane[None, :] < HEAD_DIM // 2, (t // GRID_W)[:, None], (t % GRID_W)[:, None])
    inv = 1.0 / (ROPE_THETA ** (np.arange(n, dtype=np.float64) / n))
    ang = pos.astype(np.float64) * inv[lane % n][None, :]
    sign = np.where((lane & n) == 0, -1.0, 1.0)[None, :]
    cos = np.tile(np.cos(ang), (1, LANES // HEAD_DIM)).astype(np.float32)
    sin = np.tile(np.sin(ang) * sign, (1, LANES // HEAD_DIM)).astype(np.float32)
    return jnp.asarray(cos), jnp.asarray(sin)


def _to_head_dim_token(cache_l):
    b, l, h, d = cache_l.shape
    return jnp.transpose(cache_l, (0, 2, 3, 1)).reshape(b, h * d, l)


def _from_head_dim_token(kv_t, heads):
    b, _, l = kv_t.shape
    return jnp.transpose(kv_t.reshape(b, 1, heads, HEAD_DIM, l), (0, 1, 4, 2, 3))


def kernel(x_prompt, x_sample, cache_a_k, cache_a_v, cache_b_k, cache_b_v, c, c_ctx, w_mod, b_mod,
           g_mix_pre, g_mix_post, g_ffn_pre, g_ffn_post, w_in, rpb_a, sink_b, g_grp_a, g_grp_b,
           w_out, w_up, conv_w, conv_b, w_down):
    bp, lp, _ = x_prompt.shape
    bs, ts, _ = x_sample.shape
    assert w_in.shape[0] == 1
    xp = x_prompt.reshape(bp * lp, D_MODEL)
    xs = x_sample.reshape(bs * ts, D_MODEL)
    cond8 = jnp.concatenate([c_ctx[None], c, jnp.zeros((8 - 1 - bs, D_MODEL), _F32)], axis=0)
    cos_tab, sin_tab = _rope_tables(ts)

    l = 0
    mod3 = _mod_call(cond8, w_mod[l], b_mod[l]).reshape(8, 1, 6 * D_MODEL)
    g_b_p = g_grp_b[l].reshape(KV_B, G_B, HEAD_DIM).transpose(1, 0, 2).reshape(W_B)
    w_up_b = w_up[l].astype(_BF16)
    w_down_b = w_down[l].astype(_BF16)
    rpb_pad = jnp.pad(rpb_a[l], ((0, 0), (0, 0), (0, LANES - (2 * NA_COLS - 1))))

    prompt_row = lambda i: 0
    s_tiles = ts // TOKEN_TILE
    sample_row = lambda i: 1 + i // s_tiles
    sample_row_ffn = lambda i: 1 + i // (ts // FFN_TILE)

    q_p, kat_p, vat_p, kbt_p, vbt_p = _inproj_call(xp, mod3, prompt_row, g_mix_pre[l], w_in[l], None, lp,
                                                   True, "inproj_prompt")
    a_p, b_p = _ctx_attn_call(sink_b[l], q_p, kat_p, vat_p, kbt_p, vbt_p, g_grp_a[l], g_b_p, lp)
    y_p = _tail_call(a_p, b_p, xp, mod3, prompt_row, g_mix_post[l], g_ffn_pre[l], g_ffn_post[l], w_out[l],
                     w_up_b, conv_w[l], conv_b[l], w_down_b, lp, "tail_prompt")

    q_s, ka_s, va_s, kb_s, vb_s = _inproj_call(xs, mod3, sample_row, g_mix_pre[l], w_in[l],
                                               (cos_tab, sin_tab), ts, False, "inproj_sample")
    a_s = _na_call(q_s, ka_s, va_s, _to_head_dim_token(cache_a_k[:, l]), _to_head_dim_token(cache_a_v[:, l]),
                   rpb_pad, g_grp_a[l], bs, ts)
    b_s = _win_call(sink_b[l], q_s, kb_s, vb_s, _to_head_dim_token(cache_b_k[:, l]),
                    _to_head_dim_token(cache_b_v[:, l]), g_b_p, bs, ts)
    y_s = _tail_call(a_s, b_s, xs, mod3, sample_row_ffn, g_mix_post[l], g_ffn_pre[l], g_ffn_post[l], w_out[l],
                     w_up_b, conv_w[l], conv_b[l], w_down_b, ts, "tail_sample")

    return (y_p.reshape(bp, lp, D_MODEL), y_s.reshape(bs, ts, D_MODEL),
            _from_head_dim_token(kat_p, H_A), _from_head_dim_token(vat_p, H_A),
            _from_head_dim_token(kbt_p, KV_B), _from_head_dim_token(vbt_p, KV_B))
```

```python
import functools

import numpy as np
import jax
import jax.numpy as jnp
from jax import lax
from jax.experimental import pallas as pl
from jax.experimental.pallas import tpu as pltpu

D_MODEL = 1024
HEAD_DIM = 64
H_A = 8
H_B = 8
KV_B = 2
G_B = H_B // KV_B
W_A = H_A * HEAD_DIM
W_B = H_B * HEAD_DIM
KV_W_B = KV_B * HEAD_DIM
IN_WIDTH = 3 * W_A + W_B + 2 * KV_W_B
GRID_W = 64
NA_ROWS = 8
NA_COLS = 16
SWA_WINDOW = 128
BLOCK = 128
D_FF = 2816
ROPE_THETA = 10000.0
EPS = 1e-6
NEG_INF = -1e30
Q_SCALE = HEAD_DIM ** -0.5
ROPE_HALF = HEAD_DIM // 4

LANES = 128
FF_CHUNK = 256
TOKEN_TILE = 512
FFN_TILE = 1024
FFN_ROW_BLOCK = 256
PROJ_ROW_BLOCK = 256
MOD_STREAMS = 4
ATTN_UNROLL = 4
CAST_COLS = 512
VMEM_LIMIT = 48 * 1024 * 1024
TAIL_VMEM_LIMIT = 56 * 1024 * 1024

C_QA, C_KA, C_VA, C_QB, C_KB, C_VB = 0, W_A, 2 * W_A, 3 * W_A, 3 * W_A + W_B, 3 * W_A + W_B + KV_W_B

_QB_HEAD_ORDER = tuple(kv * G_B + g for g in range(G_B) for kv in range(KV_B))

_BF16 = jnp.bfloat16
_F32 = jnp.float32


def _dot(a, b):
    return jnp.dot(a, b, preferred_element_type=_F32)


def _dot_nt(a, b):
    return lax.dot_general(a, b, (((1,), (1,)), ((), ())), preferred_element_type=_F32)


def _rms(x, g):
    var = jnp.mean(x * x, axis=-1, keepdims=True)
    return x * lax.rsqrt(var + EPS) * g


def _stack_heads(q2, lo):
    zero = jnp.zeros_like(q2)
    return jnp.concatenate([jnp.where(lo, q2, zero), jnp.where(lo, zero, q2)], axis=0)


def _unstack_heads(o2, lo):
    m = o2.shape[0] // 2
    return jnp.where(lo, o2[:m], o2[m:])


def _softmax_pv(scores, pv, sink=None):
    m = scores[0].max(axis=-1, keepdims=True)
    for s in scores[1:]:
        m = jnp.maximum(m, s.max(axis=-1, keepdims=True))
    if sink is not None:
        m = jnp.maximum(m, sink)
    denom = None
    out = None
    for s, f in zip(scores, pv):
        e = jnp.exp(s - m)
        l = e.sum(axis=-1, keepdims=True)
        o = f(e.astype(_BF16))
        denom = l if denom is None else denom + l
        out = o if out is None else out + o
    if sink is not None:
        denom = denom + jnp.exp(sink - m)
    return out / denom


def _resident(shape):
    return pl.BlockSpec(shape, lambda *_: (0,) * len(shape), pipeline_mode=pl.Buffered(1))


def _mod_kernel(c_ref, *refs):
    w_refs, b_ref, o_ref = refs[:-2], refs[-2], refs[-1]
    c = c_ref[...]
    s = (c * jax.nn.sigmoid(c)).astype(_BF16)
    k = w_refs[0].shape[0]
    acc = b_ref[...]
    for j, w_ref in enumerate(w_refs):
        acc = acc + _dot(s[:, j * k:(j + 1) * k], w_ref[...].astype(_BF16))
    o_ref[...] = acc


def _mod_call(cond8, w_mod, b_mod):
    n = w_mod.shape[1]
    tn = 1024
    kh = D_MODEL // MOD_STREAMS
    return pl.pallas_call(
        _mod_kernel,
        grid=(n // tn,),
        in_specs=[pl.BlockSpec((8, D_MODEL), lambda i: (0, 0))]
        + [pl.BlockSpec((kh, tn), lambda i, j=j: (j, i)) for j in range(MOD_STREAMS)]
        + [pl.BlockSpec((1, tn), lambda i: (0, i))],
        out_specs=pl.BlockSpec((8, tn), lambda i: (0, i)),
        out_shape=jax.ShapeDtypeStruct((8, n), _F32),
        compiler_params=pltpu.CompilerParams(dimension_semantics=("parallel",)),
        name="mod",
    )(cond8, *([w_mod] * MOD_STREAMS), b_mod.reshape(1, n))


def _prepare_in_weight(w_ref, w_scr):
    for c0 in list(range(0, C_QB, CAST_COLS)) + [C_KB]:
        n = min(CAST_COLS, IN_WIDTH - c0)
        w_scr[:, c0:c0 + n] = w_ref[:, c0:c0 + n].astype(_BF16)
    lo = lax.broadcasted_iota(jnp.int32, (D_MODEL, LANES), 1) < HEAD_DIM
    for j in range(G_B):
        h_lo, h_hi = _QB_HEAD_ORDER[2 * j], _QB_HEAD_ORDER[2 * j + 1]

        def head_at(h, want_hi):
            blk = w_ref[:, C_QB + LANES * (h // 2):C_QB + LANES * (h // 2 + 1)]
            return blk if (h % 2 == 1) == want_hi else pltpu.roll(blk, HEAD_DIM, 1)

        w_scr[:, C_QB + LANES * j:C_QB + LANES * (j + 1)] = jnp.where(
            lo, head_at(h_lo, False), head_at(h_hi, True)).astype(_BF16)


def _rope(z, cos, sin):
    lane = lax.broadcasted_iota(jnp.int32, z.shape, 1)
    partner = jnp.where((lane & ROPE_HALF) == 0,
                        pltpu.roll(z, LANES - ROPE_HALF, 1), pltpu.roll(z, ROPE_HALF, 1))
    return z * cos + partner * sin


def _inproj_kernel(*refs, rope, kv_transposed):
    if rope:
        x_ref, mod_ref, g_ref, w_ref, cos_ref, sin_ref, q_ref, ka_ref, va_ref, kb_ref, vb_ref, w_scr = refs
    else:
        x_ref, mod_ref, g_ref, w_ref, q_ref, ka_ref, va_ref, kb_ref, vb_ref, w_scr = refs

    @pl.when(pl.program_id(0) == 0)
    def _():
        _prepare_in_weight(w_ref, w_scr)

    mod = mod_ref[0]
    sh1 = mod[:, 0:D_MODEL]
    sc1 = mod[:, D_MODEL:2 * D_MODEL]
    tm = x_ref.shape[0]
    rb = PROJ_ROW_BLOCK

    for r in range(0, tm, rb):
        rows = slice(r, r + rb)
        h = (_rms(x_ref[rows, :], g_ref[...]) * (1 + sc1) + sh1).astype(_BF16)

        def proj(c0, n):
            return _dot(h, w_scr[:, c0:c0 + n])

        def put_kv(ref, z):
            if not kv_transposed:
                ref[rows, :] = z.astype(ref.dtype)
                return
            l = ref.shape[2]
            for t0 in range(0, rb, l):
                for c0 in range(0, z.shape[1], LANES):
                    ref[(r + t0) // l, c0:c0 + LANES, :] = z[t0:t0 + l, c0:c0 + LANES].T.astype(ref.dtype)

        q_ref[rows, 0:W_A] = (proj(C_QA, W_A) * Q_SCALE).astype(q_ref.dtype)
        put_kv(ka_ref, proj(C_KA, W_A))
        put_kv(va_ref, proj(C_VA, W_A))
        put_kv(vb_ref, proj(C_VB, KV_W_B))
        if rope:
            cos = cos_ref[rows, :]
            sin = sin_ref[rows, :]
            for j in range(W_B // LANES):
                zq = proj(C_QB + LANES * j, LANES)
                q_ref[rows, W_A + LANES * j:W_A + LANES * (j + 1)] = (
                    _rope(zq, cos, sin) * Q_SCALE).astype(q_ref.dtype)
            put_kv(kb_ref, _rope(proj(C_KB, KV_W_B), cos, sin))
        else:
            q_ref[rows, W_A:W_A + W_B] = (proj(C_QB, W_B) * Q_SCALE).astype(q_ref.dtype)
            put_kv(kb_ref, proj(C_KB, KV_W_B))


def _inproj_call(x2d, mod3, mod_row, g_pre, w_in, rope_tabs, seq, kv_transposed, name):
    t = x2d.shape[0]
    tm = TOKEN_TILE
    rope = rope_tabs is not None
    in_specs = [
        pl.BlockSpec((tm, D_MODEL), lambda i: (i, 0)),
        pl.BlockSpec((1, 1, 6 * D_MODEL), lambda i: (mod_row(i), 0, 0)),
        pl.BlockSpec((1, D_MODEL), lambda i: (0, 0)),
        _resident((D_MODEL, IN_WIDTH)),
    ]
    args = [x2d, mod3, g_pre.reshape(1, D_MODEL), w_in]
    if rope:
        tiles_per_seq = rope_tabs[0].shape[0] // tm
        in_specs += [pl.BlockSpec((tm, LANES), lambda i: (i % tiles_per_seq, 0))] * 2
        args += list(rope_tabs)
    if kv_transposed:
        nseq = tm // seq
        kv_shape = lambda w: jax.ShapeDtypeStruct((t // seq, w, seq), _F32)
        kv_spec = lambda w: pl.BlockSpec((nseq, w, seq), lambda i: (i, 0, 0))
    else:
        kv_shape = lambda w: jax.ShapeDtypeStruct((t, w), _BF16)
        kv_spec = lambda w: pl.BlockSpec((tm, w), lambda i: (i, 0))
    out_shape = (jax.ShapeDtypeStruct((t, W_A + W_B), _BF16),
                 kv_shape(W_A), kv_shape(W_A), kv_shape(KV_W_B), kv_shape(KV_W_B))
    out_specs = (pl.BlockSpec((tm, W_A + W_B), lambda i: (i, 0)),
                 kv_spec(W_A), kv_spec(W_A), kv_spec(KV_W_B), kv_spec(KV_W_B))
    return pl.pallas_call(
        functools.partial(_inproj_kernel, rope=rope, kv_transposed=kv_transposed),
        grid=(t // tm,),
        in_specs=in_specs,
        out_specs=out_specs,
        out_shape=out_shape,
        scratch_shapes=[pltpu.VMEM((D_MODEL, IN_WIDTH), _BF16)],
        compiler_params=pltpu.CompilerParams(
            dimension_semantics=("arbitrary",), vmem_limit_bytes=VMEM_LIMIT),
        name=name,
    )(*args)


def _ctx_attn_kernel(sink_ref, q_ref, ka_ref, va_ref, kb_ref, vb_ref, ga_ref, gb_ref, a_ref, b_ref):
    l = q_ref.shape[0]
    lo = lax.broadcasted_iota(jnp.int32, (l, LANES), 1) < HEAD_DIM
    row_lo = lax.broadcasted_iota(jnp.int32, (2 * l, 1), 0) < l
    oa = []
    for p in range(W_A // LANES):
        sl = slice(LANES * p, LANES * (p + 1))
        qs = _stack_heads(q_ref[:, sl], lo)
        s = _dot(qs, ka_ref[0, sl, :].astype(_BF16))
        vt = va_ref[0, sl, :].astype(_BF16)
        oa.append(_unstack_heads(_softmax_pv([s], [lambda e, vt=vt: _dot_nt(e, vt)]), lo))
    a_ref[...] = _rms(jnp.concatenate(oa, axis=1), ga_ref[...]).astype(a_ref.dtype)
    kb = kb_ref[0].astype(_BF16)
    vb = vb_ref[0].astype(_BF16)
    ob = []
    for j in range(G_B):
        qs = _stack_heads(q_ref[:, W_A + LANES * j:W_A + LANES * (j + 1)], lo)
        s = _dot(qs, kb)
        sink = jnp.where(row_lo, sink_ref[0, j], sink_ref[1, j])
        ob.append(_unstack_heads(_softmax_pv([s], [lambda e: _dot_nt(e, vb)], sink), lo))
    b_ref[...] = _rms(jnp.concatenate(ob, axis=1), gb_ref[...]).astype(b_ref.dtype)


def _ctx_attn_call(sink, q, ka, va, kb, vb, g_a, g_b, seq):
    t = q.shape[0]
    row = lambda b: (b, 0)
    seq3 = lambda b: (b, 0, 0)
    const = lambda b: (0, 0)
    return pl.pallas_call(
        _ctx_attn_kernel,
        grid=(t // seq,),
        in_specs=[
            pl.BlockSpec(memory_space=pltpu.SMEM),
            pl.BlockSpec((seq, W_A + W_B), row),
            pl.BlockSpec((1, W_A, seq), seq3),
            pl.BlockSpec((1, W_A, seq), seq3),
            pl.BlockSpec((1, KV_W_B, seq), seq3),
            pl.BlockSpec((1, KV_W_B, seq), seq3),
            pl.BlockSpec((1, W_A), const),
            pl.BlockSpec((1, W_B), const),
        ],
        out_specs=(pl.BlockSpec((seq, W_A), row), pl.BlockSpec((seq, W_B), row)),
        out_shape=(jax.ShapeDtypeStruct((t, W_A), _BF16), jax.ShapeDtypeStruct((t, W_B), _BF16)),
        compiler_params=pltpu.CompilerParams(
            dimension_semantics=("parallel",), vmem_limit_bytes=VMEM_LIMIT),
        name="ctx_attn",
    )(sink, q, ka, va, kb, vb, g_a.reshape(1, W_A), g_b.reshape(1, W_B))


def _na_kernel(q_ref, k_ref, v_ref, ck_ref, cv_ref, rpb_ref, ga_ref, a_ref, bias_ref, o_scr):
    rows = q_ref.shape[0] // GRID_W
    kr = min(NA_ROWS, rows)
    lane = lax.broadcasted_iota(jnp.int32, (GRID_W, LANES), 1)
    lo = lane < HEAD_DIM
    cq = lax.broadcasted_iota(jnp.int32, (GRID_W, LANES), 0)
    ck = lane & (GRID_W - 1)
    cs = jnp.clip(cq - NA_COLS // 2, 0, GRID_W - NA_COLS)
    valid = (ck >= cs) & (ck < cs + NA_COLS)
    n_drow = 2 * NA_ROWS - 1

    for p in range(W_A // LANES):
        sl = slice(LANES * p, LANES * (p + 1))
        for hh in range(2):
            t_lo, t_hi = [], []
            for d in range(n_drow):
                base = jnp.broadcast_to(rpb_ref[2 * p + hh, d:d + 1, :], (GRID_W, LANES))
                t_lo.append(pltpu.roll(base, LANES - (NA_COLS - 1), 1, stride=1, stride_axis=0))
                t_hi.append(pltpu.roll(base, GRID_W - (NA_COLS - 1), 1, stride=1, stride_axis=0))
            pairs = {}
            for d0 in range(NA_ROWS):
                for j in range(kr // 2):
                    d = d0 + 2 * j
                    if d not in pairs:
                        pairs[d] = jnp.where(valid, jnp.where(lo, t_lo[d], t_hi[d + 1]), NEG_INF)
                    bias_ref[d0, GRID_W * hh:GRID_W * (hh + 1), LANES * j:LANES * (j + 1)] = pairs[d]

        ckt = ck_ref[0, sl, :].astype(_BF16)
        cvt = cv_ref[0, sl, :].astype(_BF16)

        def body(r, carry):
            rs = jnp.clip(r - kr // 2, 0, rows - kr)
            d0 = rs - r + (NA_ROWS - 1)
            q0 = pl.multiple_of(r * GRID_W, GRID_W)
            k0 = pl.multiple_of(rs * GRID_W, GRID_W)
            qs = _stack_heads(q_ref[pl.ds(q0, GRID_W), sl], lo)
            kw = k_ref[pl.ds(k0, kr * GRID_W), sl]
            vw = v_ref[pl.ds(k0, kr * GRID_W), sl]
            s_nb = _dot_nt(qs, kw) + bias_ref[d0]
            s_cx = _dot(qs, ckt)
            o2 = _softmax_pv([s_nb, s_cx], [lambda e: _dot(e, vw), lambda e: _dot_nt(e, cvt)])
            o_scr[pl.ds(q0, GRID_W), sl] = _unstack_heads(o2, lo)
            return carry

        lax.fori_loop(0, rows, body, 0, unroll=ATTN_UNROLL)

    a_ref[...] = _rms(o_scr[...], ga_ref[...]).astype(a_ref.dtype)


def _na_call(q, k, v, ckt, cvt, rpb_pad, g_a, nb, seq):
    past = ckt.shape[2]
    kr = min(NA_ROWS, seq // GRID_W)
    blk = lambda b: (b, 0)
    return pl.pallas_call(
        _na_kernel,
        grid=(nb,),
        in_specs=[
            pl.BlockSpec((seq, W_A), blk),
            pl.BlockSpec((seq, W_A), blk),
            pl.BlockSpec((seq, W_A), blk),
            pl.BlockSpec((1, W_A, past), lambda b: (b, 0, 0)),
            pl.BlockSpec((1, W_A, past), lambda b: (b, 0, 0)),
            pl.BlockSpec((H_A, 2 * NA_ROWS - 1, LANES), lambda b: (0, 0, 0)),
            pl.BlockSpec((1, W_A), lambda b: (0, 0)),
        ],
        out_specs=pl.BlockSpec((seq, W_A), blk),
        out_shape=jax.ShapeDtypeStruct((nb * seq, W_A), _BF16),
        scratch_shapes=[pltpu.VMEM((NA_ROWS, 2 * GRID_W, kr * GRID_W), _F32),
                        pltpu.VMEM((seq, W_A), _F32)],
        compiler_params=pltpu.CompilerParams(
            dimension_semantics=("parallel",), vmem_limit_bytes=VMEM_LIMIT),
        name="na_attn",
    )(q, k, v, ckt, cvt, rpb_pad, g_a.reshape(1, W_A))


def _win_kernel(sink_ref, q_ref, k_ref, v_ref, ck_ref, cv_ref, gb_ref, b_ref, o_scr):
    t = q_ref.shape[0]
    nblk = t // BLOCK
    band = 3 * BLOCK
    lo = lax.broadcasted_iota(jnp.int32, (BLOCK, LANES), 1) < HEAD_DIM
    row_lo = lax.broadcasted_iota(jnp.int32, (2 * BLOCK, 1), 0) < BLOCK
    qi = lax.broadcasted_iota(jnp.int32, (2 * BLOCK, band), 0) & (BLOCK - 1)
    kj = lax.broadcasted_iota(jnp.int32, (2 * BLOCK, band), 1)
    rel = qi - kj
    ckt = ck_ref[0].astype(_BF16)
    cvt = cv_ref[0].astype(_BF16)

    for j in range(G_B):
        sl = slice(LANES * j, LANES * (j + 1))
        sink = jnp.where(row_lo, sink_ref[0, j], sink_ref[1, j])

        def body(n, carry):
            start = jnp.clip((n - 1) * BLOCK, 0, t - band)
            q0 = pl.multiple_of(n * BLOCK, BLOCK)
            k0 = pl.multiple_of(start, BLOCK)
            qs = _stack_heads(q_ref[pl.ds(q0, BLOCK), sl], lo)
            s = _dot_nt(qs, k_ref[pl.ds(k0, band), :])
            s = jnp.where(jnp.abs(rel + (n * BLOCK - start)) <= SWA_WINDOW, s, NEG_INF)
            s_cx = _dot(qs, ckt)
            vband = v_ref[pl.ds(k0, band), :]
            o2 = _softmax_pv([s, s_cx], [lambda e: _dot(e, vband), lambda e: _dot_nt(e, cvt)], sink)
            o_scr[pl.ds(q0, BLOCK), sl] = _unstack_heads(o2, lo)
            return carry

        lax.fori_loop(0, nblk, body, 0, unroll=ATTN_UNROLL)

    b_ref[...] = _rms(o_scr[...], gb_ref[...]).astype(b_ref.dtype)


def _win_call(sink, q, k, v, ckt, cvt, g_b, nb, seq):
    past = ckt.shape[2]
    return pl.pallas_call(
        _win_kernel,
        grid=(nb,),
        in_specs=[
            pl.BlockSpec(memory_space=pltpu.SMEM),
            pl.BlockSpec((seq, W_B), lambda b: (b, W_A // W_B)),
            pl.BlockSpec((seq, KV_W_B), lambda b: (b, 0)),
            pl.BlockSpec((seq, KV_W_B), lambda b: (b, 0)),
            pl.BlockSpec((1, KV_W_B, past), lambda b: (b, 0, 0)),
            pl.BlockSpec((1, KV_W_B, past), lambda b: (b, 0, 0)),
            pl.BlockSpec((1, W_B), lambda b: (0, 0)),
        ],
        out_specs=pl.BlockSpec((seq, W_B), lambda b: (b, 0)),
        out_shape=jax.ShapeDtypeStruct((nb * seq, W_B), _BF16),
        scratch_shapes=[pltpu.VMEM((seq, W_B), _F32)],
        compiler_params=pltpu.CompilerParams(
            dimension_semantics=("parallel",), vmem_limit_bytes=VMEM_LIMIT),
        name="win_attn",
    )(sink, q, k, v, ckt, cvt, g_b.reshape(1, W_B))


def _prepare_out_weight(w_ref, w_scr):
    for r0 in range(0, W_A, CAST_COLS):
        w_scr[r0:r0 + CAST_COLS, :] = w_ref[r0:r0 + CAST_COLS, :].astype(_BF16)
    for i, h in enumerate(_QB_HEAD_ORDER):
        w_scr[W_A + HEAD_DIM * i:W_A + HEAD_DIM * (i + 1), :] = (
            w_ref[W_A + HEAD_DIM * h:W_A + HEAD_DIM * (h + 1), :].astype(_BF16))


def _tail_kernel(a_ref, b_ref, x_ref, mod_ref, gpost_ref, gffn_ref, gpost2_ref, wo_ref, wg_ref, wv_ref,
                 cw_ref, cb_ref, wd_ref, out_ref, wo_scr, x1_scr, h2_scr, act_scr, *, seq_len):
    i = pl.program_id(0)
    c = pl.program_id(1)
    nc = pl.num_programs(1)
    tm = x_ref.shape[0]
    tc = wg_ref.shape[1]
    mod = mod_ref[0]

    @pl.when((i == 0) & (c == 0))
    def _():
        _prepare_out_weight(wo_ref, wo_scr)

    @pl.when(c == 0)
    def _():
        gt1 = mod[:, 2 * D_MODEL:3 * D_MODEL]
        sh2 = mod[:, 3 * D_MODEL:4 * D_MODEL]
        sc2 = mod[:, 4 * D_MODEL:5 * D_MODEL]
        for r in range(0, tm, FFN_ROW_BLOCK):
            rows = slice(r, r + FFN_ROW_BLOCK)
            y = _dot(a_ref[rows, :], wo_scr[0:W_A, :]) + _dot(b_ref[rows, :], wo_scr[W_A:W_A + W_B, :])
            x1 = x_ref[rows, :] + gt1 * _rms(y, gpost_ref[...])
            x1_scr[rows, :] = x1
            h2_scr[rows, :] = (_rms(x1, gffn_ref[...]) * (1 + sc2) + sh2).astype(h2_scr.dtype)

    sub = lax.broadcasted_iota(jnp.int32, (8, 1), 0)

    def zero_row(x, row):
        r0 = row - row % 8
        slab = jnp.where(sub == row % 8, 0.0, x[r0:r0 + 8])
        parts = ([x[:r0]] if r0 else []) + [slab] + ([x[r0 + 8:]] if r0 + 8 < x.shape[0] else [])
        return jnp.concatenate(parts, axis=0)

    def conv_up(w_ref, col0):
        w = w_ref[...]
        u = jnp.concatenate(
            [_dot(h2_scr[r:r + FFN_ROW_BLOCK, :], w) for r in range(0, tm, FFN_ROW_BLOCK)], axis=0)
        cw = cw_ref[:, pl.ds(pl.multiple_of(col0, tc), tc)]
        cb = cb_ref[:, pl.ds(pl.multiple_of(col0, tc), tc)]
        prev = pltpu.roll(u, 1, 0)
        nxt = pltpu.roll(u, tm - 1, 0)
        for s0 in range(0, tm, seq_len):
            prev = zero_row(prev, s0)
            nxt = zero_row(nxt, s0 + seq_len - 1)
        return prev * cw[0:1, :] + u * cw[1:2, :] + nxt * cw[2:3, :] + cb

    gate = conv_up(wg_ref, c * tc)
    val = conv_up(wv_ref, D_FF + c * tc)
    half = 0.5 * gate
    act_scr[:, pl.ds(pl.multiple_of(c * tc, tc), tc)] = ((half + half * jnp.tanh(half)) * val).astype(_BF16)

    @pl.when(c == nc - 1)
    def _():
        gt2 = mod[:, 5 * D_MODEL:6 * D_MODEL]
        for r in range(0, tm, FFN_ROW_BLOCK):
            rows = slice(r, r + FFN_ROW_BLOCK)
            y = _dot(act_scr[rows, :], wd_ref[...])
            out_ref[rows, :] = x1_scr[rows, :] + gt2 * _rms(y, gpost2_ref[...])


def _tail_call(a, b, x2d, mod3, mod_row, g_post, g_ffn, g_post2, w_out, w_up, conv_w, conv_b, w_down,
               seq_len, name):
    t = x2d.shape[0]
    tm = FFN_TILE
    tc = FF_CHUNK
    nc = D_FF // tc
    row = lambda i, c: (i, 0)
    const = lambda i, c: (0, 0)
    return pl.pallas_call(
        functools.partial(_tail_kernel, seq_len=seq_len),
        grid=(t // tm, nc),
        in_specs=[
            pl.BlockSpec((tm, W_A), row),
            pl.BlockSpec((tm, W_B), row),
            pl.BlockSpec((tm, D_MODEL), row),
            pl.BlockSpec((1, 1, 6 * D_MODEL), lambda i, c: (mod_row(i), 0, 0)),
            pl.BlockSpec((1, D_MODEL), const),
            pl.BlockSpec((1, D_MODEL), const),
            pl.BlockSpec((1, D_MODEL), const),
            _resident((W_A + W_B, D_MODEL)),
            pl.BlockSpec((D_MODEL, tc), lambda i, c: (0, c)),
            pl.BlockSpec((D_MODEL, tc), lambda i, c: (0, nc + c)),
            _resident((3, 2 * D_FF)),
            _resident((1, 2 * D_FF)),
            _resident((D_FF, D_MODEL)),
        ],
        out_specs=pl.BlockSpec((tm, D_MODEL), row),
        out_shape=jax.ShapeDtypeStruct((t, D_MODEL), _F32),
        scratch_shapes=[
            pltpu.VMEM((W_A + W_B, D_MODEL), _BF16),
            pltpu.VMEM((tm, D_MODEL), _F32),
            pltpu.VMEM((tm, D_MODEL), _BF16),
            pltpu.VMEM((tm, D_FF), _BF16),
        ],
        compiler_params=pltpu.CompilerParams(
            dimension_semantics=("arbitrary", "arbitrary"), vmem_limit_bytes=TAIL_VMEM_LIMIT),
        name=name,
    )(a, b, x2d, mod3, g_post.reshape(1, D_MODEL), g_ffn.reshape(1, D_MODEL), g_post2.reshape(1, D_MODEL),
      w_out, w_up, w_up, conv_w, conv_b.reshape(1, 2 * D_FF), w_down)


def _rope_tables(seq):
    n = ROPE_HALF
    t = np.arange(seq)
    lane = np.arange(HEAD_DIM)
    pos = np.where(lane[None, :] < HEAD_DIM // 2, (t // GRID_W)[:, None], (t % GRID_W)[:, None])
    inv = 1.0 / (ROPE_THETA ** (np.arange(n, dtype=np.float64) / n))
    ang = pos.astype(np.float64) * inv[lane % n][None, :]
    sign = np.where((lane & n) == 0, -1.0, 1.0)[None, :]
    cos = np.tile(np.cos(ang), (1, LANES // HEAD_DIM)).astype(np.float32)
    sin = np.tile(np.sin(ang) * sign, (1, LANES // HEAD_DIM)).astype(np.float32)
    return jnp.asarray(cos), jnp.asarray(sin)


def _to_head_dim_token(cache_l):
    b, l, h, d = cache_l.shape
    return jnp.transpose(cache_l, (0, 2, 3, 1)).reshape(b, h * d, l)


def _from_head_dim_token(kv_t, heads):
    b, _, l = kv_t.shape
    return jnp.transpose(kv_t.reshape(b, 1, heads, HEAD_DIM, l), (0, 1, 4, 2, 3))


def kernel(x_prompt, x_sample, cache_a_k, cache_a_v, cache_b_k, cache_b_v, c, c_ctx, w_mod, b_mod,
           g_mix_pre, g_mix_post, g_ffn_pre, g_ffn_post, w_in, rpb_a, sink_b, g_grp_a, g_grp_b,
           w_out, w_up, conv_w, conv_b, w_down):
    bp, lp, _ = x_prompt.shape
    bs, ts, _ = x_sample.shape
    assert w_in.shape[0] == 1
    xp = x_prompt.reshape(bp * lp, D_MODEL)
    xs = x_sample.reshape(bs * ts, D_MODEL)
    cond8 = jnp.concatenate([c_ctx[None], c, jnp.zeros((8 - 1 - bs, D_MODEL), _F32)], axis=0)
    cos_tab, sin_tab = _rope_tables(ts)

    l = 0
    mod3 = _mod_call(cond8, w_mod[l], b_mod[l]).reshape(8, 1, 6 * D_MODEL)
    g_b_p = g_grp_b[l].reshape(KV_B, G_B, HEAD_DIM).transpose(1, 0, 2).reshape(W_B)
    w_up_b = w_up[l].astype(_BF16)
    w_down_b = w_down[l].astype(_BF16)
    rpb_pad = jnp.pad(rpb_a[l], ((0, 0), (0, 0), (0, LANES - (2 * NA_COLS - 1))))

    prompt_row = lambda i: 0
    s_tiles = ts // TOKEN_TILE
    sample_row = lambda i: 1 + i // s_tiles
    sample_row_ffn = lambda i: 1 + i // (ts // FFN_TILE)

    q_p, kat_p, vat_p, kbt_p, vbt_p = _inproj_call(xp, mod3, prompt_row, g_mix_pre[l], w_in[l], None, lp,
                                                   True, "inproj_prompt")
    a_p, b_p = _ctx_attn_call(sink_b[l], q_p, kat_p, vat_p, kbt_p, vbt_p, g_grp_a[l], g_b_p, lp)
    y_p = _tail_call(a_p, b_p, xp, mod3, prompt_row, g_mix_post[l], g_ffn_pre[l], g_ffn_post[l], w_out[l],
                     w_up_b, conv_w[l], conv_b[l], w_down_b, lp, "tail_prompt")

    q_s, ka_s, va_s, kb_s, vb_s = _inproj_call(xs, mod3, sample_row, g_mix_pre[l], w_in[l],
                                               (cos_tab, sin_tab), ts, False, "inproj_sample")
    a_s = _na_call(q_s, ka_s, va_s, _to_head_dim_token(cache_a_k[:, l]), _to_head_dim_token(cache_a_v[:, l]),
                   rpb_pad, g_grp_a[l], bs, ts)
    b_s = _win_call(sink_b[l], q_s, kb_s, vb_s, _to_head_dim_token(cache_b_k[:, l]),
                    _to_head_dim_token(cache_b_v[:, l]), g_b_p, bs, ts)
    y_s = _tail_call(a_s, b_s, xs, mod3, sample_row_ffn, g_mix_post[l], g_ffn_pre[l], g_ffn_post[l], w_out[l],
                     w_up_b, conv_w[l], conv_b[l], w_down_b, ts, "tail_sample")

    return (y_p.reshape(bp, lp, D_MODEL), y_s.reshape(bs, ts, D_MODEL),
            _from_head_dim_token(kat_p, H_A), _from_head_dim_token(vat_p, H_A),
            _from_head_dim_token(kbt_p, KV_B), _from_head_dim_token(vbt_p, KV_B))
```

```python
import functools

import numpy as np
import jax
import jax.numpy as jnp
from jax import lax
from jax.experimental import pallas as pl
from jax.experimental.pallas import tpu as pltpu

D_MODEL = 1024
HEAD_DIM = 64
H_A = 8
H_B = 8
KV_B = 2
G_B = H_B // KV_B
W_A = H_A * HEAD_DIM
W_B = H_B * HEAD_DIM
KV_W_B = KV_B * HEAD_DIM
IN_WIDTH = 3 * W_A + W_B + 2 * KV_W_B
GRID_W = 64
NA_ROWS = 8
NA_COLS = 16
SWA_WINDOW = 128
BLOCK = 128
D_FF = 2816
ROPE_THETA = 10000.0
EPS = 1e-6
NEG_INF = -1e30
Q_SCALE = HEAD_DIM ** -0.5
ROPE_HALF = HEAD_DIM // 4

LANES = 128
FF_CHUNK = 256
TOKEN_TILE = 512
FFN_TILE = 1024
FFN_ROW_BLOCK = 256
PROJ_ROW_BLOCK = 256
MOD_STREAMS = 4
ATTN_UNROLL = 4
CAST_COLS = 512
VMEM_LIMIT = 48 * 1024 * 1024
TAIL_VMEM_LIMIT = 56 * 1024 * 1024

C_QA, C_KA, C_VA, C_QB, C_KB, C_VB = 0, W_A, 2 * W_A, 3 * W_A, 3 * W_A + W_B, 3 * W_A + W_B + KV_W_B

_QB_HEAD_ORDER = tuple(kv * G_B + g for g in range(G_B) for kv in range(KV_B))

_BF16 = jnp.bfloat16
_F32 = jnp.float32


def _dot(a, b):
    return jnp.dot(a, b, preferred_element_type=_F32)


def _dot_nt(a, b):
    return lax.dot_general(a, b, (((1,), (1,)), ((), ())), preferred_element_type=_F32)


def _rms(x, g):
    var = jnp.mean(x * x, axis=-1, keepdims=True)
    return x * lax.rsqrt(var + EPS) * g


def _stack_heads(q2, lo):
    zero = jnp.zeros_like(q2)
    return jnp.concatenate([jnp.where(lo, q2, zero), jnp.where(lo, zero, q2)], axis=0)


def _unstack_heads(o2, lo):
    m = o2.shape[0] // 2
    return jnp.where(lo, o2[:m], o2[m:])


def _softmax_pv(scores, pv, sink=None):
    m = scores[0].max(axis=-1, keepdims=True)
    for s in scores[1:]:
        m = jnp.maximum(m, s.max(axis=-1, keepdims=True))
    if sink is not None:
        m = jnp.maximum(m, sink)
    denom = None
    out = None
    for s, f in zip(scores, pv):
        e = jnp.exp(s - m)
        l = e.sum(axis=-1, keepdims=True)
        o = f(e.astype(_BF16))
        denom = l if denom is None else denom + l
        out = o if out is None else out + o
    if sink is not None:
        denom = denom + jnp.exp(sink - m)
    return out / denom


def _resident(shape):
    return pl.BlockSpec(shape, lambda *_: (0,) * len(shape), pipeline_mode=pl.Buffered(1))


def _mod_kernel(c_ref, *refs):
    w_refs, b_ref, o_ref = refs[:-2], refs[-2], refs[-1]
    c = c_ref[...]
    s = (c * jax.nn.sigmoid(c)).astype(_BF16)
    k = w_refs[0].shape[0]
    acc = b_ref[...]
    for j, w_ref in enumerate(w_refs):
        acc = acc + _dot(s[:, j * k:(j + 1) * k], w_ref[...].astype(_BF16))
    o_ref[...] = acc


def _mod_call(cond8, w_mod, b_mod):
    n = w_mod.shape[1]
    tn = 1024
    kh = D_MODEL // MOD_STREAMS
    return pl.pallas_call(
        _mod_kernel,
        grid=(n // tn,),
        in_specs=[pl.BlockSpec((8, D_MODEL), lambda i: (0, 0))]
        + [pl.BlockSpec((kh, tn), lambda i, j=j: (j, i)) for j in range(MOD_STREAMS)]
        + [pl.BlockSpec((1, tn), lambda i: (0, i))],
        out_specs=pl.BlockSpec((8, tn), lambda i: (0, i)),
        out_shape=jax.ShapeDtypeStruct((8, n), _F32),
        compiler_params=pltpu.CompilerParams(dimension_semantics=("parallel",)),
        name="mod",
    )(cond8, *([w_mod] * MOD_STREAMS), b_mod.reshape(1, n))


def _prepare_in_weight(w_ref, w_scr):
    for c0 in list(range(0, C_QB, CAST_COLS)) + [C_KB]:
        n = min(CAST_COLS, IN_WIDTH - c0)
        w_scr[:, c0:c0 + n] = w_ref[:, c0:c0 + n].astype(_BF16)
    lo = lax.broadcasted_iota(jnp.int32, (D_MODEL, LANES), 1) < HEAD_DIM
    for j in range(G_B):
        h_lo, h_hi = _QB_HEAD_ORDER[2 * j], _QB_HEAD_ORDER[2 * j + 1]

        def head_at(h, want_hi):
            blk = w_ref[:, C_QB + LANES * (h // 2):C_QB + LANES * (h // 2 + 1)]
            return blk if (h % 2 == 1) == want_hi else pltpu.roll(blk, HEAD_DIM, 1)

        w_scr[:, C_QB + LANES * j:C_QB + LANES * (j + 1)] = jnp.where(
            lo, head_at(h_lo, False), head_at(h_hi, True)).astype(_BF16)


def _rope(z, cos, sin):
    lane = lax.broadcasted_iota(jnp.int32, z.shape, 1)
    partner = jnp.where((lane & ROPE_HALF) == 0,
                        pltpu.roll(z, LANES - ROPE_HALF, 1), pltpu.roll(z, ROPE_HALF, 1))
    return z * cos + partner * sin


def _inproj_kernel(*refs, rope, kv_transposed):
    if rope:
        x_ref, mod_ref, g_ref, w_ref, cos_ref, sin_ref, q_ref, ka_ref, va_ref, kb_ref, vb_ref, w_scr = refs
    else:
        x_ref, mod_ref, g_ref, w_ref, q_ref, ka_ref, va_ref, kb_ref, vb_ref, w_scr = refs

    @pl.when(pl.program_id(0) == 0)
    def _():
        _prepare_in_weight(w_ref, w_scr)

    mod = mod_ref[0]
    sh1 = mod[:, 0:D_MODEL]
    sc1 = mod[:, D_MODEL:2 * D_MODEL]
    tm = x_ref.shape[0]
    rb = PROJ_ROW_BLOCK

    for r in range(0, tm, rb):
        rows = slice(r, r + rb)
        h = (_rms(x_ref[rows, :], g_ref[...]) * (1 + sc1) + sh1).astype(_BF16)

        def proj(c0, n):
            return _dot(h, w_scr[:, c0:c0 + n])

        def put_kv(ref, z):
            if not kv_transposed:
                ref[rows, :] = z.astype(ref.dtype)
                return
            l = ref.shape[2]
            for t0 in range(0, rb, l):
                for c0 in range(0, z.shape[1], LANES):
                    ref[(r + t0) // l, c0:c0 + LANES, :] = z[t0:t0 + l, c0:c0 + LANES].T.astype(ref.dtype)

        q_ref[rows, 0:W_A] = (proj(C_QA, W_A) * Q_SCALE).astype(q_ref.dtype)
        put_kv(ka_ref, proj(C_KA, W_A))
        put_kv(va_ref, proj(C_VA, W_A))
        put_kv(vb_ref, proj(C_VB, KV_W_B))
        if rope:
            cos = cos_ref[rows, :]
            sin = sin_ref[rows, :]
            for j in range(W_B // LANES):
                zq = proj(C_QB + LANES * j, LANES)
                q_ref[rows, W_A + LANES * j:W_A + LANES * (j + 1)] = (
                    _rope(zq, cos, sin) * Q_SCALE).astype(q_ref.dtype)
            put_kv(kb_ref, _rope(proj(C_KB, KV_W_B), cos, sin))
        else:
            q_ref[rows, W_A:W_A + W_B] = (proj(C_QB, W_B) * Q_SCALE).astype(q_ref.dtype)
            put_kv(kb_ref, proj(C_KB, KV_W_B))


def _inproj_call(x2d, mod3, mod_row, g_pre, w_in, rope_tabs, seq, kv_transposed, name):
    t = x2d.shape[0]
    tm = TOKEN_TILE
    rope = rope_tabs is not None
    in_specs = [
        pl.BlockSpec((tm, D_MODEL), lambda i: (i, 0)),
        pl.BlockSpec((1, 1, 6 * D_MODEL), lambda i: (mod_row(i), 0, 0)),
        pl.BlockSpec((1, D_MODEL), lambda i: (0, 0)),
        _resident((D_MODEL, IN_WIDTH)),
    ]
    args = [x2d, mod3, g_pre.reshape(1, D_MODEL), w_in]
    if rope:
        tiles_per_seq = rope_tabs[0].shape[0] // tm
        in_specs += [pl.BlockSpec((tm, LANES), lambda i: (i % tiles_per_seq, 0))] * 2
        args += list(rope_tabs)
    if kv_transposed:
        nseq = tm // seq
        kv_shape = lambda w: jax.ShapeDtypeStruct((t // seq, w, seq), _F32)
        kv_spec = lambda w: pl.BlockSpec((nseq, w, seq), lambda i: (i, 0, 0))
    else:
        kv_shape = lambda w: jax.ShapeDtypeStruct((t, w), _BF16)
        kv_spec = lambda w: pl.BlockSpec((tm, w), lambda i: (i, 0))
    out_shape = (jax.ShapeDtypeStruct((t, W_A + W_B), _BF16),
                 kv_shape(W_A), kv_shape(W_A), kv_shape(KV_W_B), kv_shape(KV_W_B))
    out_specs = (pl.BlockSpec((tm, W_A + W_B), lambda i: (i, 0)),
                 kv_spec(W_A), kv_spec(W_A), kv_spec(KV_W_B), kv_spec(KV_W_B))
    return pl.pallas_call(
        functools.partial(_inproj_kernel, rope=rope, kv_transposed=kv_transposed),
        grid=(t // tm,),
        in_specs=in_specs,
        out_specs=out_specs,
        out_shape=out_shape,
        scratch_shapes=[pltpu.VMEM((D_MODEL, IN_WIDTH), _BF16)],
        compiler_params=pltpu.CompilerParams(
            dimension_semantics=("arbitrary",), vmem_limit_bytes=VMEM_LIMIT),
        name=name,
    )(*args)


def _ctx_attn_kernel(sink_ref, q_ref, ka_ref, va_ref, kb_ref, vb_ref, ga_ref, gb_ref, a_ref, b_ref):
    l = q_ref.shape[0]
    lo = lax.broadcasted_iota(jnp.int32, (l, LANES), 1) < HEAD_DIM
    row_lo = lax.broadcasted_iota(jnp.int32, (2 * l, 1), 0) < l
    oa = []
    for p in range(W_A // LANES):
        sl = slice(LANES * p, LANES * (p + 1))
        qs = _stack_heads(q_ref[:, sl], lo)
        s = _dot(qs, ka_ref[0, sl, :].astype(_BF16))
        vt = va_ref[0, sl, :].astype(_BF16)
        oa.append(_unstack_heads(_softmax_pv([s], [lambda e, vt=vt: _dot_nt(e, vt)]), lo))
    a_ref[...] = _rms(jnp.concatenate(oa, axis=1), ga_ref[...]).astype(a_ref.dtype)
    kb = kb_ref[0].astype(_BF16)
    vb = vb_ref[0].astype(_BF16)
    ob = []
    for j in range(G_B):
        qs = _stack_heads(q_ref[:, W_A + LANES * j:W_A + LANES * (j + 1)], lo)
        s = _dot(qs, kb)
        sink = jnp.where(row_lo, sink_ref[0, j], sink_ref[1, j])
        ob.append(_unstack_heads(_softmax_pv([s], [lambda e: _dot_nt(e, vb)], sink), lo))
    b_ref[...] = _rms(jnp.concatenate(ob, axis=1), gb_ref[...]).astype(b_ref.dtype)


def _ctx_attn_call(sink, q, ka, va, kb, vb, g_a, g_b, seq):
    t = q.shape[0]
    row = lambda b: (b, 0)
    seq3 = lambda b: (b, 0, 0)
    const = lambda b: (0, 0)
    return pl.pallas_call(
        _ctx_attn_kernel,
        grid=(t // seq,),
        in_specs=[
            pl.BlockSpec(memory_space=pltpu.SMEM),
            pl.BlockSpec((seq, W_A + W_B), row),
            pl.BlockSpec((1, W_A, seq), seq3),
            pl.BlockSpec((1, W_A, seq), seq3),
            pl.BlockSpec((1, KV_W_B, seq), seq3),
            pl.BlockSpec((1, KV_W_B, seq), seq3),
            pl.BlockSpec((1, W_A), const),
            pl.BlockSpec((1, W_B), const),
        ],
        out_specs=(pl.BlockSpec((seq, W_A), row), pl.BlockSpec((seq, W_B), row)),
        out_shape=(jax.ShapeDtypeStruct((t, W_A), _BF16), jax.ShapeDtypeStruct((t, W_B), _BF16)),
        compiler_params=pltpu.CompilerParams(
            dimension_semantics=("parallel",), vmem_limit_bytes=VMEM_LIMIT),
        name="ctx_attn",
    )(sink, q, ka, va, kb, vb, g_a.reshape(1, W_A), g_b.reshape(1, W_B))


def _na_kernel(q_ref, k_ref, v_ref, ck_ref, cv_ref, rpb_ref, ga_ref, a_ref, bias_ref, o_scr):
    rows = q_ref.shape[0] // GRID_W
    kr = min(NA_ROWS, rows)
    lane = lax.broadcasted_iota(jnp.int32, (GRID_W, LANES), 1)
    lo = lane < HEAD_DIM
    cq = lax.broadcasted_iota(jnp.int32, (GRID_W, LANES), 0)
    ck = lane & (GRID_W - 1)
    cs = jnp.clip(cq - NA_COLS // 2, 0, GRID_W - NA_COLS)
    valid = (ck >= cs) & (ck < cs + NA_COLS)
    n_drow = 2 * NA_ROWS - 1

    for p in range(W_A // LANES):
        sl = slice(LANES * p, LANES * (p + 1))
        for hh in range(2):
            t_lo, t_hi = [], []
            for d in range(n_drow):
                base = jnp.broadcast_to(rpb_ref[2 * p + hh, d:d + 1, :], (GRID_W, LANES))
                t_lo.append(pltpu.roll(base, LANES - (NA_COLS - 1), 1, stride=1, stride_axis=0))
                t_hi.append(pltpu.roll(base, GRID_W - (NA_COLS - 1), 1, stride=1, stride_axis=0))
            pairs = {}
            for d0 in range(NA_ROWS):
                for j in range(kr // 2):
                    d = d0 + 2 * j
                    if d not in pairs:
                        pairs[d] = jnp.where(valid, jnp.where(lo, t_lo[d], t_hi[d + 1]), NEG_INF)
                    bias_ref[d0, GRID_W * hh:GRID_W * (hh + 1), LANES * j:LANES * (j + 1)] = pairs[d]

        ckt = ck_ref[0, sl, :].astype(_BF16)
        cvt = cv_ref[0, sl, :].astype(_BF16)

        def body(r, carry):
            rs = jnp.clip(r - kr // 2, 0, rows - kr)
            d0 = rs - r + (NA_ROWS - 1)
            q0 = pl.multiple_of(r * GRID_W, GRID_W)
            k0 = pl.multiple_of(rs * GRID_W, GRID_W)
            qs = _stack_heads(q_ref[pl.ds(q0, GRID_W), sl], lo)
            kw = k_ref[pl.ds(k0, kr * GRID_W), sl]
            vw = v_ref[pl.ds(k0, kr * GRID_W), sl]
            s_nb = _dot_nt(qs, kw) + bias_ref[d0]
            s_cx = _dot(qs, ckt)
            o2 = _softmax_pv([s_nb, s_cx], [lambda e: _dot(e, vw), lambda e: _dot_nt(e, cvt)])
            o_scr[pl.ds(q0, GRID_W), sl] = _unstack_heads(o2, lo)
            return carry

        lax.fori_loop(0, rows, body, 0, unroll=ATTN_UNROLL)

    a_ref[...] = _rms(o_scr[...], ga_ref[...]).astype(a_ref.dtype)


def _na_call(q, k, v, ckt, cvt, rpb_pad, g_a, nb, seq):
    past = ckt.shape[2]
    kr = min(NA_ROWS, seq // GRID_W)
    blk = lambda b: (b, 0)
    return pl.pallas_call(
        _na_kernel,
        grid=(nb,),
        in_specs=[
            pl.BlockSpec((seq, W_A), blk),
            pl.BlockSpec((seq, W_A), blk),
            pl.BlockSpec((seq, W_A), blk),
            pl.BlockSpec((1, W_A, past), lambda b: (b, 0, 0)),
            pl.BlockSpec((1, W_A, past), lambda b: (b, 0, 0)),
            pl.BlockSpec((H_A, 2 * NA_ROWS - 1, LANES), lambda b: (0, 0, 0)),
            pl.BlockSpec((1, W_A), lambda b: (0, 0)),
        ],
        out_specs=pl.BlockSpec((seq, W_A), blk),
        out_shape=jax.ShapeDtypeStruct((nb * seq, W_A), _BF16),
        scratch_shapes=[pltpu.VMEM((NA_ROWS, 2 * GRID_W, kr * GRID_W), _F32),
                        pltpu.VMEM((seq, W_A), _F32)],
        compiler_params=pltpu.CompilerParams(
            dimension_semantics=("parallel",), vmem_limit_bytes=VMEM_LIMIT),
        name="na_attn",
    )(q, k, v, ckt, cvt, rpb_pad, g_a.reshape(1, W_A))


def _win_kernel(sink_ref, q_ref, k_ref, v_ref, ck_ref, cv_ref, gb_ref, b_ref, o_scr):
    t = q_ref.shape[0]
    nblk = t // BLOCK
    band = 3 * BLOCK
    lo = lax.broadcasted_iota(jnp.int32, (BLOCK, LANES), 1) < HEAD_DIM
    row_lo = lax.broadcasted_iota(jnp.int32, (2 * BLOCK, 1), 0) < BLOCK
    qi = lax.broadcasted_iota(jnp.int32, (2 * BLOCK, band), 0) & (BLOCK - 1)
    kj = lax.broadcasted_iota(jnp.int32, (2 * BLOCK, band), 1)
    rel = qi - kj
    ckt = ck_ref[0].astype(_BF16)
    cvt = cv_ref[0].astype(_BF16)

    for j in range(G_B):
        sl = slice(LANES * j, LANES * (j + 1))
        sink = jnp.where(row_lo, sink_ref[0, j], sink_ref[1, j])

        def body(n, carry):
            start = jnp.clip((n - 1) * BLOCK, 0, t - band)
            q0 = pl.multiple_of(n * BLOCK, BLOCK)
            k0 = pl.multiple_of(start, BLOCK)
            qs = _stack_heads(q_ref[pl.ds(q0, BLOCK), sl], lo)
            s = _dot_nt(qs, k_ref[pl.ds(k0, band), :])
            s = jnp.where(jnp.abs(rel + (n * BLOCK - start)) <= SWA_WINDOW, s, NEG_INF)
            s_cx = _dot(qs, ckt)
            vband = v_ref[pl.ds(k0, band), :]
            o2 = _softmax_pv([s, s_cx], [lambda e: _dot(e, vband), lambda e: _dot_nt(e, cvt)], sink)
            o_scr[pl.ds(q0, BLOCK), sl] = _unstack_heads(o2, lo)
            return carry

        lax.fori_loop(0, nblk, body, 0, unroll=ATTN_UNROLL)

    b_ref[...] = _rms(o_scr[...], gb_ref[...]).astype(b_ref.dtype)


def _win_call(sink, q, k, v, ckt, cvt, g_b, nb, seq):
    past = ckt.shape[2]
    return pl.pallas_call(
        _win_kernel,
        grid=(nb,),
        in_specs=[
            pl.BlockSpec(memory_space=pltpu.SMEM),
            pl.BlockSpec((seq, W_B), lambda b: (b, W_A // W_B)),
            pl.BlockSpec((seq, KV_W_B), lambda b: (b, 0)),
            pl.BlockSpec((seq, KV_W_B), lambda b: (b, 0)),
            pl.BlockSpec((1, KV_W_B, past), lambda b: (b, 0, 0)),
            pl.BlockSpec((1, KV_W_B, past), lambda b: (b, 0, 0)),
            pl.BlockSpec((1, W_B), lambda b: (0, 0)),
        ],
        out_specs=pl.BlockSpec((seq, W_B), lambda b: (b, 0)),
        out_shape=jax.ShapeDtypeStruct((nb * seq, W_B), _BF16),
        scratch_shapes=[pltpu.VMEM((seq, W_B), _F32)],
        compiler_params=pltpu.CompilerParams(
            dimension_semantics=("parallel",), vmem_limit_bytes=VMEM_LIMIT),
        name="win_attn",
    )(sink, q, k, v, ckt, cvt, g_b.reshape(1, W_B))


def _prepare_out_weight(w_ref, w_scr):
    for r0 in range(0, W_A, CAST_COLS):
        w_scr[r0:r0 + CAST_COLS, :] = w_ref[r0:r0 + CAST_COLS, :].astype(_BF16)
    for i, h in enumerate(_QB_HEAD_ORDER):
        w_scr[W_A + HEAD_DIM * i:W_A + HEAD_DIM * (i + 1), :] = (
            w_ref[W_A + HEAD_DIM * h:W_A + HEAD_DIM * (h + 1), :].astype(_BF16))


def _tail_kernel(*refs, seq_len, cast_weights):
    (a_ref, b_ref, x_ref, mod_ref, gpost_ref, gffn_ref, gpost2_ref, wo_ref, wg_ref, wv_ref,
     cw_ref, cb_ref, wd_ref) = refs[:13]
    if cast_weights:
        out_ref, wg_out, wv_out, wd_out, wo_scr, x1_scr, h2_scr, act_scr, wd_scr = refs[13:]
    else:
        out_ref, wo_scr, x1_scr, h2_scr, act_scr = refs[13:]
    i = pl.program_id(0)
    c = pl.program_id(1)
    nc = pl.num_programs(1)
    tm = x_ref.shape[0]
    tc = wg_ref.shape[1]
    mod = mod_ref[0]

    @pl.when((i == 0) & (c == 0))
    def _():
        _prepare_out_weight(wo_ref, wo_scr)

    @pl.when(c == 0)
    def _():
        gt1 = mod[:, 2 * D_MODEL:3 * D_MODEL]
        sh2 = mod[:, 3 * D_MODEL:4 * D_MODEL]
        sc2 = mod[:, 4 * D_MODEL:5 * D_MODEL]
        for r in range(0, tm, FFN_ROW_BLOCK):
            rows = slice(r, r + FFN_ROW_BLOCK)
            y = _dot(a_ref[rows, :], wo_scr[0:W_A, :]) + _dot(b_ref[rows, :], wo_scr[W_A:W_A + W_B, :])
            x1 = x_ref[rows, :] + gt1 * _rms(y, gpost_ref[...])
            x1_scr[rows, :] = x1
            h2_scr[rows, :] = (_rms(x1, gffn_ref[...]) * (1 + sc2) + sh2).astype(h2_scr.dtype)

    if cast_weights:
        wg = wg_ref[...].astype(_BF16)
        wv = wv_ref[...].astype(_BF16)
        wd = wd_ref[...].astype(_BF16)
        wg_out[...] = wg
        wv_out[...] = wv
        wd_out[...] = wd
        wd_scr[pl.ds(pl.multiple_of(c * tc, tc), tc), :] = wd
    else:
        wg = wg_ref[...]
        wv = wv_ref[...]

    sub = lax.broadcasted_iota(jnp.int32, (8, 1), 0)

    def zero_row(x, row):
        r0 = row - row % 8
        slab = jnp.where(sub == row % 8, 0.0, x[r0:r0 + 8])
        parts = ([x[:r0]] if r0 else []) + [slab] + ([x[r0 + 8:]] if r0 + 8 < x.shape[0] else [])
        return jnp.concatenate(parts, axis=0)

    def conv_up(w, col0):
        u = jnp.concatenate(
            [_dot(h2_scr[r:r + FFN_ROW_BLOCK, :], w) for r in range(0, tm, FFN_ROW_BLOCK)], axis=0)
        cw = cw_ref[:, pl.ds(pl.multiple_of(col0, tc), tc)]
        cb = cb_ref[:, pl.ds(pl.multiple_of(col0, tc), tc)]
        prev = pltpu.roll(u, 1, 0)
        nxt = pltpu.roll(u, tm - 1, 0)
        for s0 in range(0, tm, seq_len):
            prev = zero_row(prev, s0)
            nxt = zero_row(nxt, s0 + seq_len - 1)
        return prev * cw[0:1, :] + u * cw[1:2, :] + nxt * cw[2:3, :] + cb

    gate = conv_up(wg, c * tc)
    val = conv_up(wv, D_FF + c * tc)
    half = 0.5 * gate
    act_scr[:, pl.ds(pl.multiple_of(c * tc, tc), tc)] = ((half + half * jnp.tanh(half)) * val).astype(_BF16)

    @pl.when(c == nc - 1)
    def _():
        gt2 = mod[:, 5 * D_MODEL:6 * D_MODEL]
        w_down = wd_scr if cast_weights else wd_ref
        for r in range(0, tm, FFN_ROW_BLOCK):
            rows = slice(r, r + FFN_ROW_BLOCK)
            y = _dot(act_scr[rows, :], w_down[...])
            out_ref[rows, :] = x1_scr[rows, :] + gt2 * _rms(y, gpost2_ref[...])


def _tail_call(a, b, x2d, mod3, mod_row, g_post, g_ffn, g_post2, w_out, w_gate, w_val, conv_w, conv_b, w_down,
               seq_len, name):
    t = x2d.shape[0]
    tm = FFN_TILE
    tc = FF_CHUNK
    nc = D_FF // tc
    cast_weights = w_down.dtype == _F32
    row = lambda i, c: (i, 0)
    const = lambda i, c: (0, 0)
    (wg_arr, wg_blk), (wv_arr, wv_blk) = w_gate, w_val
    tile_mode = dict(pipeline_mode=pl.Buffered(1)) if cast_weights else {}
    in_specs = [
        pl.BlockSpec((tm, W_A), row, **tile_mode),
        pl.BlockSpec((tm, W_B), row, **tile_mode),
        pl.BlockSpec((tm, D_MODEL), row, **tile_mode),
        pl.BlockSpec((1, 1, 6 * D_MODEL), lambda i, c: (mod_row(i), 0, 0)),
        pl.BlockSpec((1, D_MODEL), const),
        pl.BlockSpec((1, D_MODEL), const),
        pl.BlockSpec((1, D_MODEL), const),
        _resident((W_A + W_B, D_MODEL)),
        pl.BlockSpec((D_MODEL, tc), lambda i, c: (0, wg_blk + c)),
        pl.BlockSpec((D_MODEL, tc), lambda i, c: (0, wv_blk + c)),
        _resident((3, 2 * D_FF)),
        _resident((1, 2 * D_FF)),
        pl.BlockSpec((tc, D_MODEL), lambda i, c: (c, 0)) if cast_weights else _resident((D_FF, D_MODEL)),
    ]
    out_specs = [pl.BlockSpec((tm, D_MODEL), row)]
    out_shape = [jax.ShapeDtypeStruct((t, D_MODEL), _F32)]
    scratch = [
        pltpu.VMEM((W_A + W_B, D_MODEL), _BF16),
        pltpu.VMEM((tm, D_MODEL), _F32),
        pltpu.VMEM((tm, D_MODEL), _BF16),
        pltpu.VMEM((tm, D_FF), _BF16),
    ]
    if cast_weights:
        out_specs += [pl.BlockSpec((D_MODEL, tc), lambda i, c: (0, c)),
                      pl.BlockSpec((D_MODEL, tc), lambda i, c: (0, c)),
                      pl.BlockSpec((tc, D_MODEL), lambda i, c: (c, 0))]
        out_shape += [jax.ShapeDtypeStruct((D_MODEL, D_FF), _BF16),
                      jax.ShapeDtypeStruct((D_MODEL, D_FF), _BF16),
                      jax.ShapeDtypeStruct((D_FF, D_MODEL), _BF16)]
        scratch += [pltpu.VMEM((D_FF, D_MODEL), _BF16)]
    return pl.pallas_call(
        functools.partial(_tail_kernel, seq_len=seq_len, cast_weights=cast_weights),
        grid=(t // tm, nc),
        in_specs=in_specs,
        out_specs=out_specs,
        out_shape=out_shape,
        scratch_shapes=scratch,
        compiler_params=pltpu.CompilerParams(
            dimension_semantics=("arbitrary", "arbitrary"), vmem_limit_bytes=TAIL_VMEM_LIMIT),
        name=name,
    )(a, b, x2d, mod3, g_post.reshape(1, D_MODEL), g_ffn.reshape(1, D_MODEL), g_post2.reshape(1, D_MODEL),
      w_out, wg_arr, wv_arr, conv_w, conv_b.reshape(1, 2 * D_FF), w_down)


def _rope_tables(seq):
    n = ROPE_HALF
    t = np.arange(seq)
    lane = np.arange(HEAD_DIM)
    pos = np.where(lane[None, :] < HEAD_DIM // 2, (t // GRID_W)[:, None], (t % GRID_W)[:, None])
    inv = 1.0 / (ROPE_THETA ** (np.arange(n, dtype=np.float64) / n))
    ang = pos.astype(np.float64) * inv[lane % n][None, :]
    sign = np.where((lane & n) == 0, -1.0, 1.0)[None, :]
    cos = np.tile(np.cos(ang), (1, LANES // HEAD_DIM)).astype(np.float32)
    sin = np.tile(np.sin(ang) * sign, (1, LANES // HEAD_DIM)).astype(np.float32)
    return jnp.asarray(cos), jnp.asarray(sin)


def _to_head_dim_token(cache_l):
    b, l, h, d = cache_l.shape
    return jnp.transpose(cache_l, (0, 2, 3, 1)).reshape(b, h * d, l)


def _from_head_dim_token(kv_t, heads):
    b, _, l = kv_t.shape
    return jnp.transpose(kv_t.reshape(b, 1, heads, HEAD_DIM, l), (0, 1, 4, 2, 3))


def kernel(x_prompt, x_sample, cache_a_k, cache_a_v, cache_b_k, cache_b_v, c, c_ctx, w_mod, b_mod,
           g_mix_pre, g_mix_post, g_ffn_pre, g_ffn_post, w_in, rpb_a, sink_b, g_grp_a, g_grp_b,
           w_out, w_up, conv_w, conv_b, w_down):
    bp, lp, _ = x_prompt.shape
    bs, ts, _ = x_sample.shape
    assert w_in.shape[0] == 1
    xp = x_prompt.reshape(bp * lp, D_MODEL)
    xs = x_sample.reshape(bs * ts, D_MODEL)
    cond8 = jnp.concatenate([c_ctx[None], c, jnp.zeros((8 - 1 - bs, D_MODEL), _F32)], axis=0)
    cos_tab, sin_tab = _rope_tables(ts)

    l = 0
    mod3 = _mod_call(cond8, w_mod[l], b_mod[l]).reshape(8, 1, 6 * D_MODEL)
    g_b_p = g_grp_b[l].reshape(KV_B, G_B, HEAD_DIM).transpose(1, 0, 2).reshape(W_B)
    rpb_pad = jnp.pad(rpb_a[l], ((0, 0), (0, 0), (0, LANES - (2 * NA_COLS - 1))))

    prompt_row = lambda i: 0
    s_tiles = ts // TOKEN_TILE
    sample_row = lambda i: 1 + i // s_tiles
    sample_row_ffn = lambda i: 1 + i // (ts // FFN_TILE)

    q_s, ka_s, va_s, kb_s, vb_s = _inproj_call(xs, mod3, sample_row, g_mix_pre[l], w_in[l],
                                               (cos_tab, sin_tab), ts, False, "inproj_sample")
    a_s = _na_call(q_s, ka_s, va_s, _to_head_dim_token(cache_a_k[:, l]), _to_head_dim_token(cache_a_v[:, l]),
                   rpb_pad, g_grp_a[l], bs, ts)
    b_s = _win_call(sink_b[l], q_s, kb_s, vb_s, _to_head_dim_token(cache_b_k[:, l]),
                    _to_head_dim_token(cache_b_v[:, l]), g_b_p, bs, ts)
    y_s, w_gate_b, w_val_b, w_down_b = _tail_call(
        a_s, b_s, xs, mod3, sample_row_ffn, g_mix_post[l], g_ffn_pre[l], g_ffn_post[l], w_out[l],
        (w_up[l], 0), (w_up[l], D_FF // FF_CHUNK), conv_w[l], conv_b[l], w_down[l], ts, "tail_sample")

    q_p, kat_p, vat_p, kbt_p, vbt_p = _inproj_call(xp, mod3, prompt_row, g_mix_pre[l], w_in[l], None, lp,
                                                   True, "inproj_prompt")
    a_p, b_p = _ctx_attn_call(sink_b[l], q_p, kat_p, vat_p, kbt_p, vbt_p, g_grp_a[l], g_b_p, lp)
    (y_p,) = _tail_call(a_p, b_p, xp, mod3, prompt_row, g_mix_post[l], g_ffn_pre[l], g_ffn_post[l], w_out[l],
                        (w_gate_b, 0), (w_val_b, 0), conv_w[l], conv_b[l], w_down_b, lp, "tail_prompt")

    return (y_p.reshape(bp, lp, D_MODEL), y_s.reshape(bs, ts, D_MODEL),
            _from_head_dim_token(kat_p, H_A), _from_head_dim_token(vat_p, H_A),
            _from_head_dim_token(kbt_p, KV_B), _from_head_dim_token(vbt_p, KV_B))
```

```python
import functools

import numpy as np
import jax
import jax.numpy as jnp
from jax import lax
from jax.experimental import pallas as pl
from jax.experimental.pallas import tpu as pltpu

D_MODEL = 1024
HEAD_DIM = 64
H_A = 8
H_B = 8
KV_B = 2
G_B = H_B // KV_B
W_A = H_A * HEAD_DIM
W_B = H_B * HEAD_DIM
KV_W_B = KV_B * HEAD_DIM
IN_WIDTH = 3 * W_A + W_B + 2 * KV_W_B
GRID_W = 64
NA_ROWS = 8
NA_COLS = 16
SWA_WINDOW = 128
BLOCK = 128
D_FF = 2816
ROPE_THETA = 10000.0
EPS = 1e-6
NEG_INF = -1e30
Q_SCALE = HEAD_DIM ** -0.5
ROPE_HALF = HEAD_DIM // 4

LANES = 128
FF_CHUNK = 256
TOKEN_TILE = 512
FFN_TILE = 1024
FFN_ROW_BLOCK = 128
DOWN_ROW_BLOCK = 256
PROJ_ROW_BLOCK = 256
MOD_STREAMS = 4
ATTN_UNROLL = 4
NA_QROWS = 2
CAST_COLS = 512
VMEM_LIMIT = 48 * 1024 * 1024
TAIL_VMEM_LIMIT = 56 * 1024 * 1024

C_QA, C_KA, C_VA, C_QB, C_KB, C_VB = 0, W_A, 2 * W_A, 3 * W_A, 3 * W_A + W_B, 3 * W_A + W_B + KV_W_B

_QB_HEAD_ORDER = tuple(kv * G_B + g for g in range(G_B) for kv in range(KV_B))

_BF16 = jnp.bfloat16
_F32 = jnp.float32


def _dot(a, b):
    return jnp.dot(a, b, preferred_element_type=_F32)


def _dot_nt(a, b):
    return lax.dot_general(a, b, (((1,), (1,)), ((), ())), preferred_element_type=_F32)


def _rms(x, g):
    var = jnp.mean(x * x, axis=-1, keepdims=True)
    return x * lax.rsqrt(var + EPS) * g


def _stack_heads(q2, lo):
    zero = jnp.zeros_like(q2)
    return jnp.concatenate([jnp.where(lo, q2, zero), jnp.where(lo, zero, q2)], axis=0)


def _unstack_heads(o2, lo):
    m = o2.shape[0] // 2
    return jnp.where(lo, o2[:m], o2[m:])


def _softmax_pv(scores, pv, sink=None):
    m = scores[0].max(axis=-1, keepdims=True)
    for s in scores[1:]:
        m = jnp.maximum(m, s.max(axis=-1, keepdims=True))
    if sink is not None:
        m = jnp.maximum(m, sink)
    denom = None
    out = None
    for s, f in zip(scores, pv):
        e = jnp.exp(s - m)
        l = e.sum(axis=-1, keepdims=True)
        o = f(e.astype(_BF16))
        denom = l if denom is None else denom + l
        out = o if out is None else out + o
    if sink is not None:
        denom = denom + jnp.exp(sink - m)
    return out / denom


def _resident(shape):
    return pl.BlockSpec(shape, lambda *_: (0,) * len(shape), pipeline_mode=pl.Buffered(1))


def _mod_kernel(c_ref, *refs):
    w_refs, b_ref, o_ref = refs[:-2], refs[-2], refs[-1]
    c = c_ref[...]
    s = (c * jax.nn.sigmoid(c)).astype(_BF16)
    k = w_refs[0].shape[0]
    acc = b_ref[...]
    for j, w_ref in enumerate(w_refs):
        acc = acc + _dot(s[:, j * k:(j + 1) * k], w_ref[...].astype(_BF16))
    o_ref[...] = acc


def _mod_call(cond8, w_mod, b_mod):
    n = w_mod.shape[1]
    tn = 1024
    kh = D_MODEL // MOD_STREAMS
    return pl.pallas_call(
        _mod_kernel,
        grid=(n // tn,),
        in_specs=[pl.BlockSpec((8, D_MODEL), lambda i: (0, 0))]
        + [pl.BlockSpec((kh, tn), lambda i, j=j: (j, i)) for j in range(MOD_STREAMS)]
        + [pl.BlockSpec((1, tn), lambda i: (0, i))],
        out_specs=pl.BlockSpec((8, tn), lambda i: (0, i)),
        out_shape=jax.ShapeDtypeStruct((8, n), _F32),
        compiler_params=pltpu.CompilerParams(dimension_semantics=("parallel",)),
        name="mod",
    )(cond8, *([w_mod] * MOD_STREAMS), b_mod.reshape(1, n))


def _prepare_in_weight(w_ref, w_scr):
    for c0 in list(range(0, C_QB, CAST_COLS)) + [C_KB]:
        n = min(CAST_COLS, IN_WIDTH - c0)
        w_scr[:, c0:c0 + n] = w_ref[:, c0:c0 + n].astype(_BF16)
    lo = lax.broadcasted_iota(jnp.int32, (D_MODEL, LANES), 1) < HEAD_DIM
    for j in range(G_B):
        h_lo, h_hi = _QB_HEAD_ORDER[2 * j], _QB_HEAD_ORDER[2 * j + 1]

        def head_at(h, want_hi):
            blk = w_ref[:, C_QB + LANES * (h // 2):C_QB + LANES * (h // 2 + 1)]
            return blk if (h % 2 == 1) == want_hi else pltpu.roll(blk, HEAD_DIM, 1)

        w_scr[:, C_QB + LANES * j:C_QB + LANES * (j + 1)] = jnp.where(
            lo, head_at(h_lo, False), head_at(h_hi, True)).astype(_BF16)


def _rope(z, cos, sin):
    lane = lax.broadcasted_iota(jnp.int32, z.shape, 1)
    partner = jnp.where((lane & ROPE_HALF) == 0,
                        pltpu.roll(z, LANES - ROPE_HALF, 1), pltpu.roll(z, ROPE_HALF, 1))
    return z * cos + partner * sin


def _inproj_kernel(*refs, rope, kv_transposed):
    if rope:
        x_ref, mod_ref, g_ref, w_ref, cos_ref, sin_ref, q_ref, ka_ref, va_ref, kb_ref, vb_ref, w_scr = refs
    else:
        x_ref, mod_ref, g_ref, w_ref, q_ref, ka_ref, va_ref, kb_ref, vb_ref, w_scr = refs

    @pl.when(pl.program_id(0) == 0)
    def _():
        _prepare_in_weight(w_ref, w_scr)

    mod = mod_ref[0]
    sh1 = mod[:, 0:D_MODEL]
    sc1 = mod[:, D_MODEL:2 * D_MODEL]
    tm = x_ref.shape[0]
    rb = PROJ_ROW_BLOCK

    for r in range(0, tm, rb):
        rows = slice(r, r + rb)
        h = (_rms(x_ref[rows, :], g_ref[...]) * (1 + sc1) + sh1).astype(_BF16)

        def proj(c0, n):
            return _dot(h, w_scr[:, c0:c0 + n])

        def put_kv(ref, z):
            if not kv_transposed:
                ref[rows, :] = z.astype(ref.dtype)
                return
            l = ref.shape[2]
            for t0 in range(0, rb, l):
                for c0 in range(0, z.shape[1], LANES):
                    ref[(r + t0) // l, c0:c0 + LANES, :] = z[t0:t0 + l, c0:c0 + LANES].T.astype(ref.dtype)

        q_ref[rows, 0:W_A] = (proj(C_QA, W_A) * Q_SCALE).astype(q_ref.dtype)
        put_kv(ka_ref, proj(C_KA, W_A))
        put_kv(va_ref, proj(C_VA, W_A))
        put_kv(vb_ref, proj(C_VB, KV_W_B))
        if rope:
            cos = cos_ref[rows, :]
            sin = sin_ref[rows, :]
            for j in range(W_B // LANES):
                zq = proj(C_QB + LANES * j, LANES)
                q_ref[rows, W_A + LANES * j:W_A + LANES * (j + 1)] = (
                    _rope(zq, cos, sin) * Q_SCALE).astype(q_ref.dtype)
            put_kv(kb_ref, _rope(proj(C_KB, KV_W_B), cos, sin))
        else:
            q_ref[rows, W_A:W_A + W_B] = (proj(C_QB, W_B) * Q_SCALE).astype(q_ref.dtype)
            put_kv(kb_ref, proj(C_KB, KV_W_B))


def _inproj_call(x2d, mod3, mod_row, g_pre, w_in, rope_tabs, seq, kv_transposed, name):
    t = x2d.shape[0]
    tm = TOKEN_TILE
    rope = rope_tabs is not None
    in_specs = [
        pl.BlockSpec((tm, D_MODEL), lambda i: (i, 0)),
        pl.BlockSpec((1, 1, 6 * D_MODEL), lambda i: (mod_row(i), 0, 0)),
        pl.BlockSpec((1, D_MODEL), lambda i: (0, 0)),
        _resident((D_MODEL, IN_WIDTH)),
    ]
    args = [x2d, mod3, g_pre.reshape(1, D_MODEL), w_in]
    if rope:
        tiles_per_seq = rope_tabs[0].shape[0] // tm
        in_specs += [pl.BlockSpec((tm, LANES), lambda i: (i % tiles_per_seq, 0))] * 2
        args += list(rope_tabs)
    if kv_transposed:
        nseq = tm // seq
        kv_shape = lambda w: jax.ShapeDtypeStruct((t // seq, w, seq), _F32)
        kv_spec = lambda w: pl.BlockSpec((nseq, w, seq), lambda i: (i, 0, 0))
    else:
        kv_shape = lambda w: jax.ShapeDtypeStruct((t, w), _BF16)
        kv_spec = lambda w: pl.BlockSpec((tm, w), lambda i: (i, 0))
    out_shape = (jax.ShapeDtypeStruct((t, W_A + W_B), _BF16),
                 kv_shape(W_A), kv_shape(W_A), kv_shape(KV_W_B), kv_shape(KV_W_B))
    out_specs = (pl.BlockSpec((tm, W_A + W_B), lambda i: (i, 0)),
                 kv_spec(W_A), kv_spec(W_A), kv_spec(KV_W_B), kv_spec(KV_W_B))
    return pl.pallas_call(
        functools.partial(_inproj_kernel, rope=rope, kv_transposed=kv_transposed),
        grid=(t // tm,),
        in_specs=in_specs,
        out_specs=out_specs,
        out_shape=out_shape,
        scratch_shapes=[pltpu.VMEM((D_MODEL, IN_WIDTH), _BF16)],
        compiler_params=pltpu.CompilerParams(
            dimension_semantics=("arbitrary",), vmem_limit_bytes=VMEM_LIMIT),
        name=name,
    )(*args)


def _ctx_attn_kernel(sink_ref, q_ref, ka_ref, va_ref, kb_ref, vb_ref, ga_ref, gb_ref, a_ref, b_ref):
    l = q_ref.shape[0]
    lo = lax.broadcasted_iota(jnp.int32, (l, LANES), 1) < HEAD_DIM
    row_lo = lax.broadcasted_iota(jnp.int32, (2 * l, 1), 0) < l
    oa = []
    for p in range(W_A // LANES):
        sl = slice(LANES * p, LANES * (p + 1))
        qs = _stack_heads(q_ref[:, sl], lo)
        s = _dot(qs, ka_ref[0, sl, :].astype(_BF16))
        vt = va_ref[0, sl, :].astype(_BF16)
        oa.append(_unstack_heads(_softmax_pv([s], [lambda e, vt=vt: _dot_nt(e, vt)]), lo))
    a_ref[...] = _rms(jnp.concatenate(oa, axis=1), ga_ref[...]).astype(a_ref.dtype)
    kb = kb_ref[0].astype(_BF16)
    vb = vb_ref[0].astype(_BF16)
    ob = []
    for j in range(G_B):
        qs = _stack_heads(q_ref[:, W_A + LANES * j:W_A + LANES * (j + 1)], lo)
        s = _dot(qs, kb)
        sink = jnp.where(row_lo, sink_ref[0, j], sink_ref[1, j])
        ob.append(_unstack_heads(_softmax_pv([s], [lambda e: _dot_nt(e, vb)], sink), lo))
    b_ref[...] = _rms(jnp.concatenate(ob, axis=1), gb_ref[...]).astype(b_ref.dtype)


def _ctx_attn_call(sink, q, ka, va, kb, vb, g_a, g_b, seq):
    t = q.shape[0]
    row = lambda b: (b, 0)
    seq3 = lambda b: (b, 0, 0)
    const = lambda b: (0, 0)
    return pl.pallas_call(
        _ctx_attn_kernel,
        grid=(t // seq,),
        in_specs=[
            pl.BlockSpec(memory_space=pltpu.SMEM),
            pl.BlockSpec((seq, W_A + W_B), row),
            pl.BlockSpec((1, W_A, seq), seq3),
            pl.BlockSpec((1, W_A, seq), seq3),
            pl.BlockSpec((1, KV_W_B, seq), seq3),
            pl.BlockSpec((1, KV_W_B, seq), seq3),
            pl.BlockSpec((1, W_A), const),
            pl.BlockSpec((1, W_B), const),
        ],
        out_specs=(pl.BlockSpec((seq, W_A), row), pl.BlockSpec((seq, W_B), row)),
        out_shape=(jax.ShapeDtypeStruct((t, W_A), _BF16), jax.ShapeDtypeStruct((t, W_B), _BF16)),
        compiler_params=pltpu.CompilerParams(
            dimension_semantics=("parallel",), vmem_limit_bytes=VMEM_LIMIT),
        name="ctx_attn",
    )(sink, q, ka, va, kb, vb, g_a.reshape(1, W_A), g_b.reshape(1, W_B))


def _na_plan(rows):
    kr = min(NA_ROWS, rows)
    win = kr + NA_QROWS
    row_start = lambda r: min(max(r - kr // 2, 0), rows - kr)
    blocks, classes = [], []
    for r0 in range(0, rows, NA_QROWS):
        ws = min(row_start(r0), rows - win)
        keys = []
        for r in range(r0, r0 + NA_QROWS):
            key = (ws - r + NA_ROWS - 1, row_start(r) - ws)
            if key not in classes:
                classes.append(key)
            keys.append(classes.index(key))
        blocks.append((r0, ws, keys))
    return kr, win, blocks, classes


def _na_kernel(q_ref, k_ref, v_ref, ck_ref, cv_ref, rpb_ref, ga_ref, a_ref, bias_ref, o_scr):
    rows = q_ref.shape[0] // GRID_W
    kr, win, blocks, classes = _na_plan(rows)
    m = NA_QROWS * GRID_W
    lane = lax.broadcasted_iota(jnp.int32, (GRID_W, LANES), 1)
    lo = lane < HEAD_DIM
    lo_q = lax.broadcasted_iota(jnp.int32, (m, LANES), 1) < HEAD_DIM
    cq = lax.broadcasted_iota(jnp.int32, (GRID_W, LANES), 0)
    ck = lane & (GRID_W - 1)
    cs = jnp.clip(cq - NA_COLS // 2, 0, GRID_W - NA_COLS)
    valid = (ck >= cs) & (ck < cs + NA_COLS)
    neg = jnp.full((GRID_W, LANES), NEG_INF, _F32)
    n_drow = 2 * NA_ROWS - 1

    for p in range(W_A // LANES):
        sl = slice(LANES * p, LANES * (p + 1))
        for hh in range(2):
            t_lo, t_hi = [], []
            for d in range(n_drow):
                base = jnp.broadcast_to(rpb_ref[2 * p + hh, d:d + 1, :], (GRID_W, LANES))
                t_lo.append(pltpu.roll(base, LANES - (NA_COLS - 1), 1, stride=1, stride_axis=0))
                t_hi.append(pltpu.roll(base, GRID_W - (NA_COLS - 1), 1, stride=1, stride_axis=0))
            tiles = {}
            for ci, (d_first, off) in enumerate(classes):
                for j in range(win // 2):
                    d_of = lambda i: d_first + i if off <= i < off + kr else None
                    key = (d_of(2 * j), d_of(2 * j + 1))
                    if key not in tiles:
                        left = neg if key[0] is None else t_lo[key[0]]
                        right = neg if key[1] is None else t_hi[key[1]]
                        tiles[key] = jnp.where(valid, jnp.where(lo, left, right), NEG_INF)
                    bias_ref[ci, GRID_W * hh:GRID_W * (hh + 1), LANES * j:LANES * (j + 1)] = tiles[key]

        ckt = ck_ref[0, sl, :].astype(_BF16)
        cvt = cv_ref[0, sl, :].astype(_BF16)

        for r0, ws, keys in blocks:
            q0, k0 = r0 * GRID_W, ws * GRID_W
            qs = _stack_heads(q_ref[q0:q0 + m, sl], lo_q)
            kw = k_ref[k0:k0 + win * GRID_W, sl]
            vw = v_ref[k0:k0 + win * GRID_W, sl]
            bias = jnp.concatenate(
                [bias_ref[ci, GRID_W * hh:GRID_W * (hh + 1), :] for hh in range(2) for ci in keys], axis=0)
            s_nb = _dot_nt(qs, kw) + bias
            s_cx = _dot(qs, ckt)
            o2 = _softmax_pv([s_nb, s_cx], [lambda e, vw=vw: _dot(e, vw), lambda e: _dot_nt(e, cvt)])
            o_scr[q0:q0 + m, sl] = _unstack_heads(o2, lo_q)

    a_ref[...] = _rms(o_scr[...], ga_ref[...]).astype(a_ref.dtype)


def _na_call(q, k, v, ckt, cvt, rpb_pad, g_a, nb, seq):
    past = ckt.shape[2]
    _, win, _, classes = _na_plan(seq // GRID_W)
    blk = lambda b: (b, 0)
    return pl.pallas_call(
        _na_kernel,
        grid=(nb,),
        in_specs=[
            pl.BlockSpec((seq, W_A), blk),
            pl.BlockSpec((seq, W_A), blk),
            pl.BlockSpec((seq, W_A), blk),
            pl.BlockSpec((1, W_A, past), lambda b: (b, 0, 0)),
            pl.BlockSpec((1, W_A, past), lambda b: (b, 0, 0)),
            pl.BlockSpec((H_A, 2 * NA_ROWS - 1, LANES), lambda b: (0, 0, 0)),
            pl.BlockSpec((1, W_A), lambda b: (0, 0)),
        ],
        out_specs=pl.BlockSpec((seq, W_A), blk),
        out_shape=jax.ShapeDtypeStruct((nb * seq, W_A), _BF16),
        scratch_shapes=[pltpu.VMEM((len(classes), 2 * GRID_W, win * GRID_W), _F32),
                        pltpu.VMEM((seq, W_A), _F32)],
        compiler_params=pltpu.CompilerParams(
            dimension_semantics=("parallel",), vmem_limit_bytes=VMEM_LIMIT),
        name="na_attn",
    )(q, k, v, ckt, cvt, rpb_pad, g_a.reshape(1, W_A))


def _win_kernel(sink_ref, q_ref, k_ref, v_ref, ck_ref, cv_ref, gb_ref, b_ref, o_scr):
    t = q_ref.shape[0]
    nblk = t // BLOCK
    band = 3 * BLOCK
    lo = lax.broadcasted_iota(jnp.int32, (BLOCK, LANES), 1) < HEAD_DIM
    row_lo = lax.broadcasted_iota(jnp.int32, (2 * BLOCK, 1), 0) < BLOCK
    qi = lax.broadcasted_iota(jnp.int32, (2 * BLOCK, band), 0) & (BLOCK - 1)
    kj = lax.broadcasted_iota(jnp.int32, (2 * BLOCK, band), 1)
    rel = qi - kj
    ckt = ck_ref[0].astype(_BF16)
    cvt = cv_ref[0].astype(_BF16)

    for j in range(G_B):
        sl = slice(LANES * j, LANES * (j + 1))
        sink = jnp.where(row_lo, sink_ref[0, j], sink_ref[1, j])

        def body(n, carry):
            start = jnp.clip((n - 1) * BLOCK, 0, t - band)
            q0 = pl.multiple_of(n * BLOCK, BLOCK)
            k0 = pl.multiple_of(start, BLOCK)
            qs = _stack_heads(q_ref[pl.ds(q0, BLOCK), sl], lo)
            s = _dot_nt(qs, k_ref[pl.ds(k0, band), :])
            s = jnp.where(jnp.abs(rel + (n * BLOCK - start)) <= SWA_WINDOW, s, NEG_INF)
            s_cx = _dot(qs, ckt)
            vband = v_ref[pl.ds(k0, band), :]
            o2 = _softmax_pv([s, s_cx], [lambda e: _dot(e, vband), lambda e: _dot_nt(e, cvt)], sink)
            o_scr[pl.ds(q0, BLOCK), sl] = _unstack_heads(o2, lo)
            return carry

        lax.fori_loop(0, nblk, body, 0, unroll=ATTN_UNROLL)

    b_ref[...] = _rms(o_scr[...], gb_ref[...]).astype(b_ref.dtype)


def _win_call(sink, q, k, v, ckt, cvt, g_b, nb, seq):
    past = ckt.shape[2]
    return pl.pallas_call(
        _win_kernel,
        grid=(nb,),
        in_specs=[
            pl.BlockSpec(memory_space=pltpu.SMEM),
            pl.BlockSpec((seq, W_B), lambda b: (b, W_A // W_B)),
            pl.BlockSpec((seq, KV_W_B), lambda b: (b, 0)),
            pl.BlockSpec((seq, KV_W_B), lambda b: (b, 0)),
            pl.BlockSpec((1, KV_W_B, past), lambda b: (b, 0, 0)),
            pl.BlockSpec((1, KV_W_B, past), lambda b: (b, 0, 0)),
            pl.BlockSpec((1, W_B), lambda b: (0, 0)),
        ],
        out_specs=pl.BlockSpec((seq, W_B), lambda b: (b, 0)),
        out_shape=jax.ShapeDtypeStruct((nb * seq, W_B), _BF16),
        scratch_shapes=[pltpu.VMEM((seq, W_B), _F32)],
        compiler_params=pltpu.CompilerParams(
            dimension_semantics=("parallel",), vmem_limit_bytes=VMEM_LIMIT),
        name="win_attn",
    )(sink, q, k, v, ckt, cvt, g_b.reshape(1, W_B))


def _prepare_out_weight(w_ref, w_scr):
    for r0 in range(0, W_A, CAST_COLS):
        w_scr[r0:r0 + CAST_COLS, :] = w_ref[r0:r0 + CAST_COLS, :].astype(_BF16)
    for i, h in enumerate(_QB_HEAD_ORDER):
        w_scr[W_A + HEAD_DIM * i:W_A + HEAD_DIM * (i + 1), :] = (
            w_ref[W_A + HEAD_DIM * h:W_A + HEAD_DIM * (h + 1), :].astype(_BF16))


def _tail_kernel(a_ref, b_ref, x_ref, mod_ref, gpost_ref, gffn_ref, gpost2_ref, wo_ref, wg_ref, wv_ref,
                 cw_ref, cb_ref, wd_ref, out_ref, wo_scr, x1_scr, h2_scr, act_scr, *, seq_len):
    i = pl.program_id(0)
    c = pl.program_id(1)
    nc = pl.num_programs(1)
    tm = x_ref.shape[0]
    tc = wg_ref.shape[1]
    mod = mod_ref[0]

    @pl.when((i == 0) & (c == 0))
    def _():
        _prepare_out_weight(wo_ref, wo_scr)

    @pl.when(c == 0)
    def _():
        gt1 = mod[:, 2 * D_MODEL:3 * D_MODEL]
        sh2 = mod[:, 3 * D_MODEL:4 * D_MODEL]
        sc2 = mod[:, 4 * D_MODEL:5 * D_MODEL]
        for r in range(0, tm, FFN_ROW_BLOCK):
            rows = slice(r, r + FFN_ROW_BLOCK)
            y = _dot(a_ref[rows, :], wo_scr[0:W_A, :]) + _dot(b_ref[rows, :], wo_scr[W_A:W_A + W_B, :])
            x1 = x_ref[rows, :] + gt1 * _rms(y, gpost_ref[...])
            x1_scr[rows, :] = x1
            h2_scr[rows, :] = (_rms(x1, gffn_ref[...]) * (1 + sc2) + sh2).astype(h2_scr.dtype)

    sub = lax.broadcasted_iota(jnp.int32, (8, 1), 0)

    def zero_row(x, row):
        r0 = row - row % 8
        slab = jnp.where(sub == row % 8, 0.0, x[r0:r0 + 8])
        parts = ([x[:r0]] if r0 else []) + [slab] + ([x[r0 + 8:]] if r0 + 8 < x.shape[0] else [])
        return jnp.concatenate(parts, axis=0)

    def conv_up(w_ref, col0):
        w = w_ref[...]
        u = jnp.concatenate(
            [_dot(h2_scr[r:r + FFN_ROW_BLOCK, :], w) for r in range(0, tm, FFN_ROW_BLOCK)], axis=0)
        cw = cw_ref[:, pl.ds(pl.multiple_of(col0, tc), tc)]
        cb = cb_ref[:, pl.ds(pl.multiple_of(col0, tc), tc)]
        prev = pltpu.roll(u, 1, 0)
        nxt = pltpu.roll(u, tm - 1, 0)
        for s0 in range(0, tm, seq_len):
            prev = zero_row(prev, s0)
            nxt = zero_row(nxt, s0 + seq_len - 1)
        return prev * cw[0:1, :] + u * cw[1:2, :] + nxt * cw[2:3, :] + cb

    gate = conv_up(wg_ref, c * tc)
    val = conv_up(wv_ref, D_FF + c * tc)
    half = 0.5 * gate
    act_scr[:, pl.ds(pl.multiple_of(c * tc, tc), tc)] = ((half + half * jnp.tanh(half)) * val).astype(_BF16)

    @pl.when(c == nc - 1)
    def _():
        gt2 = mod[:, 5 * D_MODEL:6 * D_MODEL]
        for r in range(0, tm, DOWN_ROW_BLOCK):
            rows = slice(r, r + DOWN_ROW_BLOCK)
            y = _dot(act_scr[rows, :], wd_ref[...])
            out_ref[rows, :] = x1_scr[rows, :] + gt2 * _rms(y, gpost2_ref[...])


def _tail_call(a, b, x2d, mod3, mod_row, g_post, g_ffn, g_post2, w_out, w_up, conv_w, conv_b, w_down,
               seq_len, name):
    t = x2d.shape[0]
    tm = FFN_TILE
    tc = FF_CHUNK
    nc = D_FF // tc
    row = lambda i, c: (i, 0)
    const = lambda i, c: (0, 0)
    return pl.pallas_call(
        functools.partial(_tail_kernel, seq_len=seq_len),
        grid=(t // tm, nc),
        in_specs=[
            pl.BlockSpec((tm, W_A), row),
            pl.BlockSpec((tm, W_B), row),
            pl.BlockSpec((tm, D_MODEL), row),
            pl.BlockSpec((1, 1, 6 * D_MODEL), lambda i, c: (mod_row(i), 0, 0)),
            pl.BlockSpec((1, D_MODEL), const),
            pl.BlockSpec((1, D_MODEL), const),
            pl.BlockSpec((1, D_MODEL), const),
            _resident((W_A + W_B, D_MODEL)),
            pl.BlockSpec((D_MODEL, tc), lambda i, c: (0, c)),
            pl.BlockSpec((D_MODEL, tc), lambda i, c: (0, nc + c)),
            _resident((3, 2 * D_FF)),
            _resident((1, 2 * D_FF)),
            _resident((D_FF, D_MODEL)),
        ],
        out_specs=pl.BlockSpec((tm, D_MODEL), row),
        out_shape=jax.ShapeDtypeStruct((t, D_MODEL), _F32),
        scratch_shapes=[
            pltpu.VMEM((W_A + W_B, D_MODEL), _BF16),
            pltpu.VMEM((tm, D_MODEL), _F32),
            pltpu.VMEM((tm, D_MODEL), _BF16),
            pltpu.VMEM((tm, D_FF), _BF16),
        ],
        compiler_params=pltpu.CompilerParams(
            dimension_semantics=("arbitrary", "arbitrary"), vmem_limit_bytes=TAIL_VMEM_LIMIT),
        name=name,
    )(a, b, x2d, mod3, g_post.reshape(1, D_MODEL), g_ffn.reshape(1, D_MODEL), g_post2.reshape(1, D_MODEL),
      w_out, w_up, w_up, conv_w, conv_b.reshape(1, 2 * D_FF), w_down)


def _rope_tables(seq):
    n = ROPE_HALF
    t = np.arange(seq)
    lane = np.arange(HEAD_DIM)
    pos = np.where(lane[None, :] < HEAD_DIM // 2, (t // GRID_W)[:, None], (t % GRID_W)[:, None])
    inv = 1.0 / (ROPE_THETA ** (np.arange(n, dtype=np.float64) / n))
    ang = pos.astype(np.float64) * inv[lane % n][None, :]
    sign = np.where((lane & n) == 0, -1.0, 1.0)[None, :]
    cos = np.tile(np.cos(ang), (1, LANES // HEAD_DIM)).astype(np.float32)
    sin = np.tile(np.sin(ang) * sign, (1, LANES // HEAD_DIM)).astype(np.float32)
    return jnp.asarray(cos), jnp.asarray(sin)


def _to_head_dim_token(cache_l):
    b, l, h, d = cache_l.shape
    return jnp.transpose(cache_l, (0, 2, 3, 1)).reshape(b, h * d, l)


def _from_head_dim_token(kv_t, heads):
    b, _, l = kv_t.shape
    return jnp.transpose(kv_t.reshape(b, 1, heads, HEAD_DIM, l), (0, 1, 4, 2, 3))


def kernel(x_prompt, x_sample, cache_a_k, cache_a_v, cache_b_k, cache_b_v, c, c_ctx, w_mod, b_mod,
           g_mix_pre, g_mix_post, g_ffn_pre, g_ffn_post, w_in, rpb_a, sink_b, g_grp_a, g_grp_b,
           w_out, w_up, conv_w, conv_b, w_down):
    bp, lp, _ = x_prompt.shape
    bs, ts, _ = x_sample.shape
    assert w_in.shape[0] == 1
    xp = x_prompt.reshape(bp * lp, D_MODEL)
    xs = x_sample.reshape(bs * ts, D_MODEL)
    cond8 = jnp.concatenate([c_ctx[None], c, jnp.zeros((8 - 1 - bs, D_MODEL), _F32)], axis=0)
    cos_tab, sin_tab = _rope_tables(ts)

    l = 0
    mod3 = _mod_call(cond8, w_mod[l], b_mod[l]).reshape(8, 1, 6 * D_MODEL)
    g_b_p = g_grp_b[l].reshape(KV_B, G_B, HEAD_DIM).transpose(1, 0, 2).reshape(W_B)
    w_up_b = w_up[l].astype(_BF16)
    w_down_b = w_down[l].astype(_BF16)
    rpb_pad = jnp.pad(rpb_a[l], ((0, 0), (0, 0), (0, LANES - (2 * NA_COLS - 1))))

    prompt_row = lambda i: 0
    s_tiles = ts // TOKEN_TILE
    sample_row = lambda i: 1 + i // s_tiles
    sample_row_ffn = lambda i: 1 + i // (ts // FFN_TILE)

    q_p, kat_p, vat_p, kbt_p, vbt_p = _inproj_call(xp, mod3, prompt_row, g_mix_pre[l], w_in[l], None, lp,
                                                   True, "inproj_prompt")
    a_p, b_p = _ctx_attn_call(sink_b[l], q_p, kat_p, vat_p, kbt_p, vbt_p, g_grp_a[l], g_b_p, lp)
    y_p = _tail_call(a_p, b_p, xp, mod3, prompt_row, g_mix_post[l], g_ffn_pre[l], g_ffn_post[l], w_out[l],
                     w_up_b, conv_w[l], conv_b[l], w_down_b, lp, "tail_prompt")

    q_s, ka_s, va_s, kb_s, vb_s = _inproj_call(xs, mod3, sample_row, g_mix_pre[l], w_in[l],
                                               (cos_tab, sin_tab), ts, False, "inproj_sample")
    a_s = _na_call(q_s, ka_s, va_s, _to_head_dim_token(cache_a_k[:, l]), _to_head_dim_token(cache_a_v[:, l]),
                   rpb_pad, g_grp_a[l], bs, ts)
    b_s = _win_call(sink_b[l], q_s, kb_s, vb_s, _to_head_dim_token(cache_b_k[:, l]),
                    _to_head_dim_token(cache_b_v[:, l]), g_b_p, bs, ts)
    y_s = _tail_call(a_s, b_s, xs, mod3, sample_row_ffn, g_mix_post[l], g_ffn_pre[l], g_ffn_post[l], w_out[l],
                     w_up_b, conv_w[l], conv_b[l], w_down_b, ts, "tail_sample")

    return (y_p.reshape(bp, lp, D_MODEL), y_s.reshape(bs, ts, D_MODEL),
            _from_head_dim_token(kat_p, H_A), _from_head_dim_token(vat_p, H_A),
            _from_head_dim_token(kbt_p, KV_B), _from_head_dim_token(vbt_p, KV_B))
```

```python
import functools

import numpy as np
import jax
import jax.numpy as jnp
from jax import lax
from jax.experimental import pallas as pl
from jax.experimental.pallas import tpu as pltpu

D_MODEL = 1024
HEAD_DIM = 64
H_A = 8
H_B = 8
KV_B = 2
G_B = H_B // KV_B
W_A = H_A * HEAD_DIM
W_B = H_B * HEAD_DIM
KV_W_B = KV_B * HEAD_DIM
IN_WIDTH = 3 * W_A + W_B + 2 * KV_W_B
GRID_W = 64
NA_ROWS = 8
NA_COLS = 16
SWA_WINDOW = 128
BLOCK = 128
D_FF = 2816
ROPE_THETA = 10000.0
EPS = 1e-6
NEG_INF = -1e30
Q_SCALE = HEAD_DIM ** -0.5
ROPE_HALF = HEAD_DIM // 4

LANES = 128
FF_CHUNK = 256
TOKEN_TILE = 512
FFN_TILE = 1024
FFN_ROW_BLOCK = 256
DOWN_ROW_BLOCK = 256
PROJ_ROW_BLOCK = 256
MOD_STREAMS = 2
MOD_ROW_BLOCK = 128
NA_QROWS = 2
CAST_COLS = 512
VMEM_LIMIT = 48 * 1024 * 1024
TAIL_VMEM_LIMIT = 56 * 1024 * 1024

C_QA, C_KA, C_VA, C_QB, C_KB, C_VB = 0, W_A, 2 * W_A, 3 * W_A, 3 * W_A + W_B, 3 * W_A + W_B + KV_W_B

_QB_HEAD_ORDER = tuple(kv * G_B + g for g in range(G_B) for kv in range(KV_B))

_BF16 = jnp.bfloat16
_F32 = jnp.float32


def _dot(a, b):
    return jnp.dot(a, b, preferred_element_type=_F32)


def _dot_nt(a, b):
    return lax.dot_general(a, b, (((1,), (1,)), ((), ())), preferred_element_type=_F32)


def _rms(x, g):
    var = jnp.mean(x * x, axis=-1, keepdims=True)
    return x * lax.rsqrt(var + EPS) * g


def _stack_heads(q2, lo):
    zero = jnp.zeros_like(q2)
    return jnp.concatenate([jnp.where(lo, q2, zero), jnp.where(lo, zero, q2)], axis=0)


def _unstack_heads(o2, lo):
    m = o2.shape[0] // 2
    return jnp.where(lo, o2[:m], o2[m:])


def _softmax_pv(scores, pv, sink=None):
    m = scores[0].max(axis=-1, keepdims=True)
    for s in scores[1:]:
        m = jnp.maximum(m, s.max(axis=-1, keepdims=True))
    if sink is not None:
        m = jnp.maximum(m, sink)
    denom = None
    out = None
    for s, f in zip(scores, pv):
        e = jnp.exp(s - m)
        l = e.sum(axis=-1, keepdims=True)
        o = f(e.astype(_BF16))
        denom = l if denom is None else denom + l
        out = o if out is None else out + o
    if sink is not None:
        denom = denom + jnp.exp(sink - m)
    return out / denom


def _resident(shape):
    return pl.BlockSpec(shape, lambda *_: (0,) * len(shape), pipeline_mode=pl.Buffered(1))


def _mod_kernel(c_ref, *refs):
    w_refs, b_ref, o_ref = refs[:-2], refs[-2], refs[-1]
    i = pl.program_id(0)

    @pl.when(i == 0)
    def _():
        o_ref[...] = jnp.broadcast_to(b_ref[...], o_ref.shape)

    acc = o_ref[...]
    for j, w_ref in enumerate(w_refs):
        c = c_ref[i * len(w_refs) + j]
        s = (c * jax.nn.sigmoid(c)).astype(_BF16)
        acc = acc + _dot(s, w_ref[...].astype(_BF16))
    o_ref[...] = acc


def _mod_call(cond8, w_mod, b_mod):
    n = w_mod.shape[1]
    kb = MOD_ROW_BLOCK
    nblk = D_MODEL // kb
    cond_blocks = cond8.reshape(8, nblk, kb).transpose(1, 0, 2)
    return pl.pallas_call(
        _mod_kernel,
        grid=(nblk // MOD_STREAMS,),
        in_specs=[pl.BlockSpec((nblk, 8, kb), lambda i: (0, 0, 0))]
        + [pl.BlockSpec((kb, n), lambda i, j=j: (i * MOD_STREAMS + j, 0)) for j in range(MOD_STREAMS)]
        + [pl.BlockSpec((1, n), lambda i: (0, 0))],
        out_specs=pl.BlockSpec((8, n), lambda i: (0, 0)),
        out_shape=jax.ShapeDtypeStruct((8, n), _F32),
        compiler_params=pltpu.CompilerParams(dimension_semantics=("arbitrary",), vmem_limit_bytes=VMEM_LIMIT),
        name="mod",
    )(cond_blocks, *([w_mod] * MOD_STREAMS), b_mod.reshape(1, n))


def _prepare_in_weight(w_ref, w_scr):
    for c0 in list(range(0, C_QB, CAST_COLS)) + [C_KB]:
        n = min(CAST_COLS, IN_WIDTH - c0)
        w_scr[:, c0:c0 + n] = w_ref[:, c0:c0 + n].astype(_BF16)
    lo = lax.broadcasted_iota(jnp.int32, (D_MODEL, LANES), 1) < HEAD_DIM
    for j in range(G_B):
        h_lo, h_hi = _QB_HEAD_ORDER[2 * j], _QB_HEAD_ORDER[2 * j + 1]

        def head_at(h, want_hi):
            blk = w_ref[:, C_QB + LANES * (h // 2):C_QB + LANES * (h // 2 + 1)]
            return blk if (h % 2 == 1) == want_hi else pltpu.roll(blk, HEAD_DIM, 1)

        w_scr[:, C_QB + LANES * j:C_QB + LANES * (j + 1)] = jnp.where(
            lo, head_at(h_lo, False), head_at(h_hi, True)).astype(_BF16)


def _rope(z, cos, sin):
    lane = lax.broadcasted_iota(jnp.int32, z.shape, 1)
    partner = jnp.where((lane & ROPE_HALF) == 0,
                        pltpu.roll(z, LANES - ROPE_HALF, 1), pltpu.roll(z, ROPE_HALF, 1))
    return z * cos + partner * sin


def _inproj_kernel(*refs, rope, kv_transposed):
    if rope:
        x_ref, mod_ref, g_ref, w_ref, cos_ref, sin_ref, q_ref, ka_ref, va_ref, kb_ref, vb_ref, w_scr = refs
    else:
        x_ref, mod_ref, g_ref, w_ref, q_ref, ka_ref, va_ref, kb_ref, vb_ref, w_scr = refs

    @pl.when(pl.program_id(0) == 0)
    def _():
        _prepare_in_weight(w_ref, w_scr)

    mod = mod_ref[0]
    sh1 = mod[:, 0:D_MODEL]
    sc1 = mod[:, D_MODEL:2 * D_MODEL]
    tm = x_ref.shape[0]
    rb = PROJ_ROW_BLOCK

    for r in range(0, tm, rb):
        rows = slice(r, r + rb)
        h = (_rms(x_ref[rows, :], g_ref[...]) * (1 + sc1) + sh1).astype(_BF16)

        def proj(c0, n):
            return _dot(h, w_scr[:, c0:c0 + n])

        def put_kv(ref, z):
            if not kv_transposed:
                ref[rows, :] = z.astype(ref.dtype)
                return
            l = ref.shape[2]
            for t0 in range(0, rb, l):
                for c0 in range(0, z.shape[1], LANES):
                    ref[(r + t0) // l, c0:c0 + LANES, :] = z[t0:t0 + l, c0:c0 + LANES].T.astype(ref.dtype)

        q_ref[rows, 0:W_A] = (proj(C_QA, W_A) * Q_SCALE).astype(q_ref.dtype)
        put_kv(ka_ref, proj(C_KA, W_A))
        put_kv(va_ref, proj(C_VA, W_A))
        put_kv(vb_ref, proj(C_VB, KV_W_B))
        if rope:
            cos = cos_ref[rows, :]
            sin = sin_ref[rows, :]
            for j in range(W_B // LANES):
                zq = proj(C_QB + LANES * j, LANES)
                q_ref[rows, W_A + LANES * j:W_A + LANES * (j + 1)] = (
                    _rope(zq, cos, sin) * Q_SCALE).astype(q_ref.dtype)
            put_kv(kb_ref, _rope(proj(C_KB, KV_W_B), cos, sin))
        else:
            q_ref[rows, W_A:W_A + W_B] = (proj(C_QB, W_B) * Q_SCALE).astype(q_ref.dtype)
            put_kv(kb_ref, proj(C_KB, KV_W_B))


def _inproj_call(x2d, mod3, mod_row, g_pre, w_in, rope_tabs, seq, kv_transposed, name):
    t = x2d.shape[0]
    tm = TOKEN_TILE
    rope = rope_tabs is not None
    in_specs = [
        pl.BlockSpec((tm, D_MODEL), lambda i: (i, 0)),
        pl.BlockSpec((1, 1, 6 * D_MODEL), lambda i: (mod_row(i), 0, 0)),
        pl.BlockSpec((1, D_MODEL), lambda i: (0, 0)),
        _resident((D_MODEL, IN_WIDTH)),
    ]
    args = [x2d, mod3, g_pre.reshape(1, D_MODEL), w_in]
    if rope:
        tiles_per_seq = rope_tabs[0].shape[0] // tm
        in_specs += [pl.BlockSpec((tm, LANES), lambda i: (i % tiles_per_seq, 0))] * 2
        args += list(rope_tabs)
    if kv_transposed:
        nseq = tm // seq
        kv_shape = lambda w: jax.ShapeDtypeStruct((t // seq, w, seq), _F32)
        kv_spec = lambda w: pl.BlockSpec((nseq, w, seq), lambda i: (i, 0, 0))
    else:
        kv_shape = lambda w: jax.ShapeDtypeStruct((t, w), _BF16)
        kv_spec = lambda w: pl.BlockSpec((tm, w), lambda i: (i, 0))
    out_shape = (jax.ShapeDtypeStruct((t, W_A + W_B), _BF16),
                 kv_shape(W_A), kv_shape(W_A), kv_shape(KV_W_B), kv_shape(KV_W_B))
    out_specs = (pl.BlockSpec((tm, W_A + W_B), lambda i: (i, 0)),
                 kv_spec(W_A), kv_spec(W_A), kv_spec(KV_W_B), kv_spec(KV_W_B))
    return pl.pallas_call(
        functools.partial(_inproj_kernel, rope=rope, kv_transposed=kv_transposed),
        grid=(t // tm,),
        in_specs=in_specs,
        out_specs=out_specs,
        out_shape=out_shape,
        scratch_shapes=[pltpu.VMEM((D_MODEL, IN_WIDTH), _BF16)],
        compiler_params=pltpu.CompilerParams(
            dimension_semantics=("arbitrary",), vmem_limit_bytes=VMEM_LIMIT),
        name=name,
    )(*args)


def _ctx_attn_kernel(sink_ref, q_ref, ka_ref, va_ref, kb_ref, vb_ref, ga_ref, gb_ref, a_ref, b_ref):
    l = q_ref.shape[0]
    lo = lax.broadcasted_iota(jnp.int32, (l, LANES), 1) < HEAD_DIM
    row_lo = lax.broadcasted_iota(jnp.int32, (2 * l, 1), 0) < l
    oa = []
    for p in range(W_A // LANES):
        sl = slice(LANES * p, LANES * (p + 1))
        qs = _stack_heads(q_ref[:, sl], lo)
        s = _dot(qs, ka_ref[0, sl, :].astype(_BF16))
        vt = va_ref[0, sl, :].astype(_BF16)
        oa.append(_unstack_heads(_softmax_pv([s], [lambda e, vt=vt: _dot_nt(e, vt)]), lo))
    a_ref[...] = _rms(jnp.concatenate(oa, axis=1), ga_ref[...]).astype(a_ref.dtype)
    kb = kb_ref[0].astype(_BF16)
    vb = vb_ref[0].astype(_BF16)
    ob = []
    for j in range(G_B):
        qs = _stack_heads(q_ref[:, W_A + LANES * j:W_A + LANES * (j + 1)], lo)
        s = _dot(qs, kb)
        sink = jnp.where(row_lo, sink_ref[0, j], sink_ref[1, j])
        ob.append(_unstack_heads(_softmax_pv([s], [lambda e: _dot_nt(e, vb)], sink), lo))
    b_ref[...] = _rms(jnp.concatenate(ob, axis=1), gb_ref[...]).astype(b_ref.dtype)


def _ctx_attn_call(sink, q, ka, va, kb, vb, g_a, g_b, seq):
    t = q.shape[0]
    row = lambda b: (b, 0)
    seq3 = lambda b: (b, 0, 0)
    const = lambda b: (0, 0)
    return pl.pallas_call(
        _ctx_attn_kernel,
        grid=(t // seq,),
        in_specs=[
            pl.BlockSpec(memory_space=pltpu.SMEM),
            pl.BlockSpec((seq, W_A + W_B), row),
            pl.BlockSpec((1, W_A, seq), seq3),
            pl.BlockSpec((1, W_A, seq), seq3),
            pl.BlockSpec((1, KV_W_B, seq), seq3),
            pl.BlockSpec((1, KV_W_B, seq), seq3),
            pl.BlockSpec((1, W_A), const),
            pl.BlockSpec((1, W_B), const),
        ],
        out_specs=(pl.BlockSpec((seq, W_A), row), pl.BlockSpec((seq, W_B), row)),
        out_shape=(jax.ShapeDtypeStruct((t, W_A), _BF16), jax.ShapeDtypeStruct((t, W_B), _BF16)),
        compiler_params=pltpu.CompilerParams(
            dimension_semantics=("parallel",), vmem_limit_bytes=VMEM_LIMIT),
        name="ctx_attn",
    )(sink, q, ka, va, kb, vb, g_a.reshape(1, W_A), g_b.reshape(1, W_B))


def _na_plan(rows):
    kr = min(NA_ROWS, rows)
    win = kr + NA_QROWS
    row_start = lambda r: min(max(r - kr // 2, 0), rows - kr)
    blocks, classes = [], []
    for r0 in range(0, rows, NA_QROWS):
        ws = min(row_start(r0), rows - win)
        keys = []
        for r in range(r0, r0 + NA_QROWS):
            key = (ws - r + NA_ROWS - 1, row_start(r) - ws)
            if key not in classes:
                classes.append(key)
            keys.append(classes.index(key))
        blocks.append((r0, ws, keys))
    return kr, win, blocks, classes


def _na_kernel(q_ref, k_ref, v_ref, ck_ref, cv_ref, rpb_ref, ga_ref, a_ref, bias_ref, o_scr):
    rows = q_ref.shape[0] // GRID_W
    kr, win, blocks, classes = _na_plan(rows)
    m = NA_QROWS * GRID_W
    lane = lax.broadcasted_iota(jnp.int32, (GRID_W, LANES), 1)
    lo = lane < HEAD_DIM
    lo_q = lax.broadcasted_iota(jnp.int32, (m, LANES), 1) < HEAD_DIM
    cq = lax.broadcasted_iota(jnp.int32, (GRID_W, LANES), 0)
    ck = lane & (GRID_W - 1)
    cs = jnp.clip(cq - NA_COLS // 2, 0, GRID_W - NA_COLS)
    valid = (ck >= cs) & (ck < cs + NA_COLS)
    neg = jnp.full((GRID_W, LANES), NEG_INF, _F32)
    n_drow = 2 * NA_ROWS - 1

    for p in range(W_A // LANES):
        sl = slice(LANES * p, LANES * (p + 1))
        for hh in range(2):
            t_lo, t_hi = [], []
            for d in range(n_drow):
                base = jnp.broadcast_to(rpb_ref[2 * p + hh, d:d + 1, :], (GRID_W, LANES))
                t_lo.append(pltpu.roll(base, LANES - (NA_COLS - 1), 1, stride=1, stride_axis=0))
                t_hi.append(pltpu.roll(base, GRID_W - (NA_COLS - 1), 1, stride=1, stride_axis=0))
            tiles = {}
            for ci, (d_first, off) in enumerate(classes):
                for j in range(win // 2):
                    d_of = lambda i: d_first + i if off <= i < off + kr else None
                    key = (d_of(2 * j), d_of(2 * j + 1))
                    if key not in tiles:
                        left = neg if key[0] is None else t_lo[key[0]]
                        right = neg if key[1] is None else t_hi[key[1]]
                        tiles[key] = jnp.where(valid, jnp.where(lo, left, right), NEG_INF)
                    bias_ref[ci, GRID_W * hh:GRID_W * (hh + 1), LANES * j:LANES * (j + 1)] = tiles[key]

        ckt = ck_ref[0, sl, :].astype(_BF16)
        cvt = cv_ref[0, sl, :].astype(_BF16)

        for r0, ws, keys in blocks:
            q0, k0 = r0 * GRID_W, ws * GRID_W
            qs = _stack_heads(q_ref[q0:q0 + m, sl], lo_q)
            kw = k_ref[k0:k0 + win * GRID_W, sl]
            vw = v_ref[k0:k0 + win * GRID_W, sl]
            bias = jnp.concatenate(
                [bias_ref[ci, GRID_W * hh:GRID_W * (hh + 1), :] for hh in range(2) for ci in keys], axis=0)
            s_nb = _dot_nt(qs, kw) + bias
            s_cx = _dot(qs, ckt)
            o2 = _softmax_pv([s_nb, s_cx], [lambda e, vw=vw: _dot(e, vw), lambda e: _dot_nt(e, cvt)])
            o_scr[q0:q0 + m, sl] = _unstack_heads(o2, lo_q)

    a_ref[...] = _rms(o_scr[...], ga_ref[...]).astype(a_ref.dtype)


def _na_call(q, k, v, ckt, cvt, rpb_pad, g_a, nb, seq):
    past = ckt.shape[2]
    _, win, _, classes = _na_plan(seq // GRID_W)
    blk = lambda b: (b, 0)
    return pl.pallas_call(
        _na_kernel,
        grid=(nb,),
        in_specs=[
            pl.BlockSpec((seq, W_A), blk),
            pl.BlockSpec((seq, W_A), blk),
            pl.BlockSpec((seq, W_A), blk),
            pl.BlockSpec((1, W_A, past), lambda b: (b, 0, 0)),
            pl.BlockSpec((1, W_A, past), lambda b: (b, 0, 0)),
            pl.BlockSpec((H_A, 2 * NA_ROWS - 1, LANES), lambda b: (0, 0, 0)),
            pl.BlockSpec((1, W_A), lambda b: (0, 0)),
        ],
        out_specs=pl.BlockSpec((seq, W_A), blk),
        out_shape=jax.ShapeDtypeStruct((nb * seq, W_A), _BF16),
        scratch_shapes=[pltpu.VMEM((len(classes), 2 * GRID_W, win * GRID_W), _F32),
                        pltpu.VMEM((seq, W_A), _F32)],
        compiler_params=pltpu.CompilerParams(
            dimension_semantics=("parallel",), vmem_limit_bytes=VMEM_LIMIT),
        name="na_attn",
    )(q, k, v, ckt, cvt, rpb_pad, g_a.reshape(1, W_A))


def _win_kernel(sink_ref, q_ref, k_ref, v_ref, ck_ref, cv_ref, gb_ref, b_ref, o_scr):
    t = q_ref.shape[0]
    nblk = t // BLOCK
    band = 3 * BLOCK
    lo = lax.broadcasted_iota(jnp.int32, (BLOCK, LANES), 1) < HEAD_DIM
    row_lo = lax.broadcasted_iota(jnp.int32, (2 * BLOCK, 1), 0) < BLOCK
    qi = lax.broadcasted_iota(jnp.int32, (2 * BLOCK, band), 0) & (BLOCK - 1)
    kj = lax.broadcasted_iota(jnp.int32, (2 * BLOCK, band), 1)
    rel = qi - kj
    ckt = ck_ref[0].astype(_BF16)
    cvt = cv_ref[0].astype(_BF16)

    starts = [min(max((n - 1) * BLOCK, 0), t - band) for n in range(nblk)]
    masks = {off: jnp.where(jnp.abs(rel + off) <= SWA_WINDOW, 0.0, NEG_INF)
             for off in sorted({n * BLOCK - starts[n] for n in range(nblk)})}

    for j in range(G_B):
        sl = slice(LANES * j, LANES * (j + 1))
        sink = jnp.where(row_lo, sink_ref[0, j], sink_ref[1, j])
        for n in range(nblk):
            q0, k0 = n * BLOCK, starts[n]
            qs = _stack_heads(q_ref[q0:q0 + BLOCK, sl], lo)
            s = _dot_nt(qs, k_ref[k0:k0 + band, :]) + masks[q0 - k0]
            s_cx = _dot(qs, ckt)
            vband = v_ref[k0:k0 + band, :]
            o2 = _softmax_pv([s, s_cx], [lambda e, vband=vband: _dot(e, vband), lambda e: _dot_nt(e, cvt)],
                             sink)
            o_scr[q0:q0 + BLOCK, sl] = _unstack_heads(o2, lo)

    b_ref[...] = _rms(o_scr[...], gb_ref[...]).astype(b_ref.dtype)


def _win_call(sink, q, k, v, ckt, cvt, g_b, nb, seq):
    past = ckt.shape[2]
    return pl.pallas_call(
        _win_kernel,
        grid=(nb,),
        in_specs=[
            pl.BlockSpec(memory_space=pltpu.SMEM),
            pl.BlockSpec((seq, W_B), lambda b: (b, W_A // W_B)),
            pl.BlockSpec((seq, KV_W_B), lambda b: (b, 0)),
            pl.BlockSpec((seq, KV_W_B), lambda b: (b, 0)),
            pl.BlockSpec((1, KV_W_B, past), lambda b: (b, 0, 0)),
            pl.BlockSpec((1, KV_W_B, past), lambda b: (b, 0, 0)),
            pl.BlockSpec((1, W_B), lambda b: (0, 0)),
        ],
        out_specs=pl.BlockSpec((seq, W_B), lambda b: (b, 0)),
        out_shape=jax.ShapeDtypeStruct((nb * seq, W_B), _BF16),
        scratch_shapes=[pltpu.VMEM((seq, W_B), _F32)],
        compiler_params=pltpu.CompilerParams(
            dimension_semantics=("parallel",), vmem_limit_bytes=VMEM_LIMIT),
        name="win_attn",
    )(sink, q, k, v, ckt, cvt, g_b.reshape(1, W_B))


def _prepare_out_weight(w_ref, w_scr):
    for r0 in range(0, W_A, CAST_COLS):
        w_scr[r0:r0 + CAST_COLS, :] = w_ref[r0:r0 + CAST_COLS, :].astype(_BF16)
    for i, h in enumerate(_QB_HEAD_ORDER):
        w_scr[W_A + HEAD_DIM * i:W_A + HEAD_DIM * (i + 1), :] = (
            w_ref[W_A + HEAD_DIM * h:W_A + HEAD_DIM * (h + 1), :].astype(_BF16))


def _tail_kernel(a_ref, b_ref, x_ref, mod_ref, gpost_ref, gffn_ref, gpost2_ref, wo_ref, wg_ref, wv_ref,
                 cw_ref, cb_ref, wd_ref, out_ref, wo_scr, x1_scr, h2_scr, act_scr, *, seq_len):
    i = pl.program_id(0)
    c = pl.program_id(1)
    nc = pl.num_programs(1)
    tm = x_ref.shape[0]
    tc = wg_ref.shape[1]
    mod = mod_ref[0]

    @pl.when((i == 0) & (c == 0))
    def _():
        _prepare_out_weight(wo_ref, wo_scr)

    @pl.when(c == 0)
    def _():
        gt1 = mod[:, 2 * D_MODEL:3 * D_MODEL]
        sh2 = mod[:, 3 * D_MODEL:4 * D_MODEL]
        sc2 = mod[:, 4 * D_MODEL:5 * D_MODEL]
        for r in range(0, tm, FFN_ROW_BLOCK):
            rows = slice(r, r + FFN_ROW_BLOCK)
            y = _dot(a_ref[rows, :], wo_scr[0:W_A, :]) + _dot(b_ref[rows, :], wo_scr[W_A:W_A + W_B, :])
            x1 = x_ref[rows, :] + gt1 * _rms(y, gpost_ref[...])
            x1_scr[rows, :] = x1
            h2_scr[rows, :] = (_rms(x1, gffn_ref[...]) * (1 + sc2) + sh2).astype(h2_scr.dtype)

    sub = lax.broadcasted_iota(jnp.int32, (8, 1), 0)

    def zero_row(x, row):
        r0 = row - row % 8
        slab = jnp.where(sub == row % 8, 0.0, x[r0:r0 + 8])
        parts = ([x[:r0]] if r0 else []) + [slab] + ([x[r0 + 8:]] if r0 + 8 < x.shape[0] else [])
        return jnp.concatenate(parts, axis=0)

    def conv_up(w_ref, col0):
        w = w_ref[...]
        u = jnp.concatenate(
            [_dot(h2_scr[r:r + FFN_ROW_BLOCK, :], w) for r in range(0, tm, FFN_ROW_BLOCK)], axis=0)
        cw = cw_ref[:, pl.ds(pl.multiple_of(col0, tc), tc)]
        cb = cb_ref[:, pl.ds(pl.multiple_of(col0, tc), tc)]
        prev = pltpu.roll(u, 1, 0)
        nxt = pltpu.roll(u, tm - 1, 0)
        for s0 in range(0, tm, seq_len):
            prev = zero_row(prev, s0)
            nxt = zero_row(nxt, s0 + seq_len - 1)
        return prev * cw[0:1, :] + u * cw[1:2, :] + nxt * cw[2:3, :] + cb

    gate = conv_up(wg_ref, c * tc)
    val = conv_up(wv_ref, D_FF + c * tc)
    half = 0.5 * gate
    act_scr[:, pl.ds(pl.multiple_of(c * tc, tc), tc)] = ((half + half * jnp.tanh(half)) * val).astype(_BF16)

    @pl.when(c == nc - 1)
    def _():
        gt2 = mod[:, 5 * D_MODEL:6 * D_MODEL]
        for r in range(0, tm, DOWN_ROW_BLOCK):
            rows = slice(r, r + DOWN_ROW_BLOCK)
            y = _dot(act_scr[rows, :], wd_ref[...])
            out_ref[rows, :] = x1_scr[rows, :] + gt2 * _rms(y, gpost2_ref[...])


def _tail_call(a, b, x2d, mod3, mod_row, g_post, g_ffn, g_post2, w_out, w_up, conv_w, conv_b, w_down,
               seq_len, name):
    t = x2d.shape[0]
    tm = FFN_TILE
    tc = FF_CHUNK
    nc = D_FF // tc
    row = lambda i, c: (i, 0)
    const = lambda i, c: (0, 0)
    return pl.pallas_call(
        functools.partial(_tail_kernel, seq_len=seq_len),
        grid=(t // tm, nc),
        in_specs=[
            pl.BlockSpec((tm, W_A), row),
            pl.BlockSpec((tm, W_B), row),
            pl.BlockSpec((tm, D_MODEL), row),
            pl.BlockSpec((1, 1, 6 * D_MODEL), lambda i, c: (mod_row(i), 0, 0)),
            pl.BlockSpec((1, D_MODEL), const),
            pl.BlockSpec((1, D_MODEL), const),
            pl.BlockSpec((1, D_MODEL), const),
            _resident((W_A + W_B, D_MODEL)),
            pl.BlockSpec((D_MODEL, tc), lambda i, c: (0, c)),
            pl.BlockSpec((D_MODEL, tc), lambda i, c: (0, nc + c)),
            _resident((3, 2 * D_FF)),
            _resident((1, 2 * D_FF)),
            _resident((D_FF, D_MODEL)),
        ],
        out_specs=pl.BlockSpec((tm, D_MODEL), row),
        out_shape=jax.ShapeDtypeStruct((t, D_MODEL), _F32),
        scratch_shapes=[
            pltpu.VMEM((W_A + W_B, D_MODEL), _BF16),
            pltpu.VMEM((tm, D_MODEL), _F32),
            pltpu.VMEM((tm, D_MODEL), _BF16),
            pltpu.VMEM((tm, D_FF), _BF16),
        ],
        compiler_params=pltpu.CompilerParams(
            dimension_semantics=("arbitrary", "arbitrary"), vmem_limit_bytes=TAIL_VMEM_LIMIT),
        name=name,
    )(a, b, x2d, mod3, g_post.reshape(1, D_MODEL), g_ffn.reshape(1, D_MODEL), g_post2.reshape(1, D_MODEL),
      w_out, w_up, w_up, conv_w, conv_b.reshape(1, 2 * D_FF), w_down)


def _rope_tables(seq):
    n = ROPE_HALF
    t = np.arange(seq)
    lane = np.arange(HEAD_DIM)
    pos = np.where(lane[None, :] < HEAD_DIM // 2, (t // GRID_W)[:, None], (t % GRID_W)[:, None])
    inv = 1.0 / (ROPE_THETA ** (np.arange(n, dtype=np.float64) / n))
    ang = pos.astype(np.float64) * inv[lane % n][None, :]
    sign = np.where((lane & n) == 0, -1.0, 1.0)[None, :]
    cos = np.tile(np.cos(ang), (1, LANES // HEAD_DIM)).astype(np.float32)
    sin = np.tile(np.sin(ang) * sign, (1, LANES // HEAD_DIM)).astype(np.float32)
    return jnp.asarray(cos), jnp.asarray(sin)


def _to_head_dim_token(cache_l):
    b, l, h, d = cache_l.shape
    return jnp.transpose(cache_l, (0, 2, 3, 1)).reshape(b, h * d, l)


def _from_head_dim_token(kv_t, heads):
    b, _, l = kv_t.shape
    return jnp.transpose(kv_t.reshape(b, 1, heads, HEAD_DIM, l), (0, 1, 4, 2, 3))


def kernel(x_prompt, x_sample, cache_a_k, cache_a_v, cache_b_k, cache_b_v, c, c_ctx, w_mod, b_mod,
           g_mix_pre, g_mix_post, g_ffn_pre, g_ffn_post, w_in, rpb_a, sink_b, g_grp_a, g_grp_b,
           w_out, w_up, conv_w, conv_b, w_down):
    bp, lp, _ = x_prompt.shape
    bs, ts, _ = x_sample.shape
    assert w_in.shape[0] == 1
    xp = x_prompt.reshape(bp * lp, D_MODEL)
    xs = x_sample.reshape(bs * ts, D_MODEL)
    cond8 = jnp.concatenate([c_ctx[None], c, jnp.zeros((8 - 1 - bs, D_MODEL), _F32)], axis=0)
    cos_tab, sin_tab = _rope_tables(ts)

    l = 0
    mod3 = _mod_call(cond8, w_mod[l], b_mod[l]).reshape(8, 1, 6 * D_MODEL)
    g_b_p = g_grp_b[l].reshape(KV_B, G_B, HEAD_DIM).transpose(1, 0, 2).reshape(W_B)
    w_up_b = w_up[l].astype(_BF16)
    w_down_b = w_down[l].astype(_BF16)
    rpb_pad = jnp.pad(rpb_a[l], ((0, 0), (0, 0), (0, LANES - (2 * NA_COLS - 1))))

    prompt_row = lambda i: 0
    s_tiles = ts // TOKEN_TILE
    sample_row = lambda i: 1 + i // s_tiles
    sample_row_ffn = lambda i: 1 + i // (ts // FFN_TILE)

    q_p, kat_p, vat_p, kbt_p, vbt_p = _inproj_call(xp, mod3, prompt_row, g_mix_pre[l], w_in[l], None, lp,
                                                   True, "inproj_prompt")
    a_p, b_p = _ctx_attn_call(sink_b[l], q_p, kat_p, vat_p, kbt_p, vbt_p, g_grp_a[l], g_b_p, lp)
    y_p = _tail_call(a_p, b_p, xp, mod3, prompt_row, g_mix_post[l], g_ffn_pre[l], g_ffn_post[l], w_out[l],
                     w_up_b, conv_w[l], conv_b[l], w_down_b, lp, "tail_prompt")

    q_s, ka_s, va_s, kb_s, vb_s = _inproj_call(xs, mod3, sample_row, g_mix_pre[l], w_in[l],
                                               (cos_tab, sin_tab), ts, False, "inproj_sample")
    a_s = _na_call(q_s, ka_s, va_s, _to_head_dim_token(cache_a_k[:, l]), _to_head_dim_token(cache_a_v[:, l]),
                   rpb_pad, g_grp_a[l], bs, ts)
    b_s = _win_call(sink_b[l], q_s, kb_s, vb_s, _to_head_dim_token(cache_b_k[:, l]),
                    _to_head_dim_token(cache_b_v[:, l]), g_b_p, bs, ts)
    y_s = _tail_call(a_s, b_s, xs, mod3, sample_row_ffn, g_mix_post[l], g_ffn_pre[l], g_ffn_post[l], w_out[l],
                     w_up_b, conv_w[l], conv_b[l], w_down_b, ts, "tail_sample")

    return (y_p.reshape(bp, lp, D_MODEL), y_s.reshape(bs, ts, D_MODEL),
            _from_head_dim_token(kat_p, H_A), _from_head_dim_token(vat_p, H_A),
            _from_head_dim_token(kbt_p, KV_B), _from_head_dim_token(vbt_p, KV_B))
```

```python
import functools

import numpy as np
import jax
import jax.numpy as jnp
from jax import lax
from jax.experimental import pallas as pl
from jax.experimental.pallas import tpu as pltpu

D_MODEL = 1024
HEAD_DIM = 64
H_A = 8
H_B = 8
KV_B = 2
G_B = H_B // KV_B
W_A = H_A * HEAD_DIM
W_B = H_B * HEAD_DIM
KV_W_B = KV_B * HEAD_DIM
IN_WIDTH = 3 * W_A + W_B + 2 * KV_W_B
GRID_W = 64
NA_ROWS = 8
NA_COLS = 16
SWA_WINDOW = 128
BLOCK = 128
D_FF = 2816
ROPE_THETA = 10000.0
EPS = 1e-6
NEG_INF = -1e30
Q_SCALE = HEAD_DIM ** -0.5
ROPE_HALF = HEAD_DIM // 4

LANES = 128
FF_CHUNK = 256
TOKEN_TILE = 512
FFN_TILE = 1024
FFN_ROW_BLOCK = 256
DOWN_ROW_BLOCK = 256
SEQ_TAIL_TILE = 512
PROJ_ROW_BLOCK = 256
MOD_STREAMS = 2
MOD_ROW_BLOCK = 128
NA_QROWS = 2
CAST_COLS = 512
VMEM_LIMIT = 48 * 1024 * 1024
TAIL_VMEM_LIMIT = 56 * 1024 * 1024

C_QA, C_KA, C_VA, C_QB, C_KB, C_VB = 0, W_A, 2 * W_A, 3 * W_A, 3 * W_A + W_B, 3 * W_A + W_B + KV_W_B

_QB_HEAD_ORDER = tuple(kv * G_B + g for g in range(G_B) for kv in range(KV_B))

_BF16 = jnp.bfloat16
_F32 = jnp.float32


def _dot(a, b):
    return jnp.dot(a, b, preferred_element_type=_F32)


def _dot_nt(a, b):
    return lax.dot_general(a, b, (((1,), (1,)), ((), ())), preferred_element_type=_F32)


def _rms(x, g):
    var = jnp.mean(x * x, axis=-1, keepdims=True)
    return x * lax.rsqrt(var + EPS) * g


def _stack_heads(q2, lo):
    zero = jnp.zeros_like(q2)
    return jnp.concatenate([jnp.where(lo, q2, zero), jnp.where(lo, zero, q2)], axis=0)


def _unstack_heads(o2, lo):
    m = o2.shape[0] // 2
    return jnp.where(lo, o2[:m], o2[m:])


def _softmax_pv(scores, pv, sink=None):
    m = scores[0].max(axis=-1, keepdims=True)
    for s in scores[1:]:
        m = jnp.maximum(m, s.max(axis=-1, keepdims=True))
    if sink is not None:
        m = jnp.maximum(m, sink)
    denom = None
    out = None
    for s, f in zip(scores, pv):
        e = jnp.exp(s - m)
        l = e.sum(axis=-1, keepdims=True)
        o = f(e.astype(_BF16))
        denom = l if denom is None else denom + l
        out = o if out is None else out + o
    if sink is not None:
        denom = denom + jnp.exp(sink - m)
    return out / denom


def _resident(shape):
    return pl.BlockSpec(shape, lambda *_: (0,) * len(shape), pipeline_mode=pl.Buffered(1))


def _mod_kernel(c_ref, *refs):
    w_refs, b_ref, o_ref = refs[:-2], refs[-2], refs[-1]
    i = pl.program_id(0)

    @pl.when(i == 0)
    def _():
        o_ref[...] = jnp.broadcast_to(b_ref[...], o_ref.shape)

    acc = o_ref[...]
    for j, w_ref in enumerate(w_refs):
        c = c_ref[i * len(w_refs) + j]
        s = (c * jax.nn.sigmoid(c)).astype(_BF16)
        acc = acc + _dot(s, w_ref[...].astype(_BF16))
    o_ref[...] = acc


def _mod_call(cond8, w_mod, b_mod):
    n = w_mod.shape[1]
    kb = MOD_ROW_BLOCK
    nblk = D_MODEL // kb
    cond_blocks = cond8.reshape(8, nblk, kb).transpose(1, 0, 2)
    return pl.pallas_call(
        _mod_kernel,
        grid=(nblk // MOD_STREAMS,),
        in_specs=[pl.BlockSpec((nblk, 8, kb), lambda i: (0, 0, 0))]
        + [pl.BlockSpec((kb, n), lambda i, j=j: (i * MOD_STREAMS + j, 0)) for j in range(MOD_STREAMS)]
        + [pl.BlockSpec((1, n), lambda i: (0, 0))],
        out_specs=pl.BlockSpec((8, n), lambda i: (0, 0)),
        out_shape=jax.ShapeDtypeStruct((8, n), _F32),
        compiler_params=pltpu.CompilerParams(dimension_semantics=("arbitrary",), vmem_limit_bytes=VMEM_LIMIT),
        name="mod",
    )(cond_blocks, *([w_mod] * MOD_STREAMS), b_mod.reshape(1, n))


def _prepare_in_weight(w_ref, w_scr):
    for c0 in list(range(0, C_QB, CAST_COLS)) + [C_KB]:
        n = min(CAST_COLS, IN_WIDTH - c0)
        w_scr[:, c0:c0 + n] = w_ref[:, c0:c0 + n].astype(_BF16)
    lo = lax.broadcasted_iota(jnp.int32, (D_MODEL, LANES), 1) < HEAD_DIM
    for j in range(G_B):
        h_lo, h_hi = _QB_HEAD_ORDER[2 * j], _QB_HEAD_ORDER[2 * j + 1]

        def head_at(h, want_hi):
            blk = w_ref[:, C_QB + LANES * (h // 2):C_QB + LANES * (h // 2 + 1)]
            return blk if (h % 2 == 1) == want_hi else pltpu.roll(blk, HEAD_DIM, 1)

        w_scr[:, C_QB + LANES * j:C_QB + LANES * (j + 1)] = jnp.where(
            lo, head_at(h_lo, False), head_at(h_hi, True)).astype(_BF16)


def _rope(z, cos, sin):
    lane = lax.broadcasted_iota(jnp.int32, z.shape, 1)
    partner = jnp.where((lane & ROPE_HALF) == 0,
                        pltpu.roll(z, LANES - ROPE_HALF, 1), pltpu.roll(z, ROPE_HALF, 1))
    return z * cos + partner * sin


def _inproj_kernel(*refs, rope, kv_transposed):
    if rope:
        x_ref, mod_ref, g_ref, w_ref, cos_ref, sin_ref, q_ref, ka_ref, va_ref, kb_ref, vb_ref, w_scr = refs
    else:
        x_ref, mod_ref, g_ref, w_ref, q_ref, ka_ref, va_ref, kb_ref, vb_ref, w_scr = refs

    @pl.when(pl.program_id(0) == 0)
    def _():
        _prepare_in_weight(w_ref, w_scr)

    mod = mod_ref[0]
    sh1 = mod[:, 0:D_MODEL]
    sc1 = mod[:, D_MODEL:2 * D_MODEL]
    tm = x_ref.shape[0]
    rb = PROJ_ROW_BLOCK

    for r in range(0, tm, rb):
        rows = slice(r, r + rb)
        h = (_rms(x_ref[rows, :], g_ref[...]) * (1 + sc1) + sh1).astype(_BF16)

        def proj(c0, n):
            return _dot(h, w_scr[:, c0:c0 + n])

        def put_kv(ref, z):
            if not kv_transposed:
                ref[rows, :] = z.astype(ref.dtype)
                return
            l = ref.shape[2]
            for t0 in range(0, rb, l):
                for c0 in range(0, z.shape[1], LANES):
                    ref[(r + t0) // l, c0:c0 + LANES, :] = z[t0:t0 + l, c0:c0 + LANES].T.astype(ref.dtype)

        q_ref[rows, 0:W_A] = (proj(C_QA, W_A) * Q_SCALE).astype(q_ref.dtype)
        put_kv(ka_ref, proj(C_KA, W_A))
        put_kv(va_ref, proj(C_VA, W_A))
        zkv = proj(C_KB, 2 * KV_W_B)
        zkb, zvb = zkv[:, :KV_W_B], zkv[:, KV_W_B:]
        zqb = proj(C_QB, W_B)
        put_kv(vb_ref, zvb)
        if rope:
            cos = cos_ref[rows, :]
            sin = sin_ref[rows, :]
            for j in range(W_B // LANES):
                q_ref[rows, W_A + LANES * j:W_A + LANES * (j + 1)] = (
                    _rope(zqb[:, LANES * j:LANES * (j + 1)], cos, sin) * Q_SCALE).astype(q_ref.dtype)
            put_kv(kb_ref, _rope(zkb, cos, sin))
        else:
            q_ref[rows, W_A:W_A + W_B] = (zqb * Q_SCALE).astype(q_ref.dtype)
            put_kv(kb_ref, zkb)


def _inproj_call(x2d, mod3, mod_row, g_pre, w_in, rope_tabs, seq, kv_transposed, name):
    t = x2d.shape[0]
    tm = TOKEN_TILE
    rope = rope_tabs is not None
    in_specs = [
        pl.BlockSpec((tm, D_MODEL), lambda i: (i, 0)),
        pl.BlockSpec((1, 1, 6 * D_MODEL), lambda i: (mod_row(i), 0, 0)),
        pl.BlockSpec((1, D_MODEL), lambda i: (0, 0)),
        _resident((D_MODEL, IN_WIDTH)),
    ]
    args = [x2d, mod3, g_pre.reshape(1, D_MODEL), w_in]
    if rope:
        tiles_per_seq = rope_tabs[0].shape[0] // tm
        in_specs += [pl.BlockSpec((tm, LANES), lambda i: (i % tiles_per_seq, 0))] * 2
        args += list(rope_tabs)
    if kv_transposed:
        nseq = tm // seq
        kv_shape = lambda w: jax.ShapeDtypeStruct((t // seq, w, seq), _F32)
        kv_spec = lambda w: pl.BlockSpec((nseq, w, seq), lambda i: (i, 0, 0))
    else:
        kv_shape = lambda w: jax.ShapeDtypeStruct((t, w), _BF16)
        kv_spec = lambda w: pl.BlockSpec((tm, w), lambda i: (i, 0))
    out_shape = (jax.ShapeDtypeStruct((t, W_A + W_B), _BF16),
                 kv_shape(W_A), kv_shape(W_A), kv_shape(KV_W_B), kv_shape(KV_W_B))
    out_specs = (pl.BlockSpec((tm, W_A + W_B), lambda i: (i, 0)),
                 kv_spec(W_A), kv_spec(W_A), kv_spec(KV_W_B), kv_spec(KV_W_B))
    return pl.pallas_call(
        functools.partial(_inproj_kernel, rope=rope, kv_transposed=kv_transposed),
        grid=(t // tm,),
        in_specs=in_specs,
        out_specs=out_specs,
        out_shape=out_shape,
        scratch_shapes=[pltpu.VMEM((D_MODEL, IN_WIDTH), _BF16)],
        compiler_params=pltpu.CompilerParams(
            dimension_semantics=("arbitrary",), vmem_limit_bytes=VMEM_LIMIT),
        name=name,
    )(*args)


def _ctx_attn_kernel(sink_ref, q_ref, ka_ref, va_ref, kb_ref, vb_ref, ga_ref, gb_ref, a_ref, b_ref):
    l = q_ref.shape[0]
    lo = lax.broadcasted_iota(jnp.int32, (l, LANES), 1) < HEAD_DIM
    row_lo = lax.broadcasted_iota(jnp.int32, (2 * l, 1), 0) < l
    oa = []
    for p in range(W_A // LANES):
        sl = slice(LANES * p, LANES * (p + 1))
        qs = _stack_heads(q_ref[:, sl], lo)
        s = _dot(qs, ka_ref[0, sl, :].astype(_BF16))
        vt = va_ref[0, sl, :].astype(_BF16)
        oa.append(_unstack_heads(_softmax_pv([s], [lambda e, vt=vt: _dot_nt(e, vt)]), lo))
    a_ref[...] = _rms(jnp.concatenate(oa, axis=1), ga_ref[...]).astype(a_ref.dtype)
    kb = kb_ref[0].astype(_BF16)
    vb = vb_ref[0].astype(_BF16)
    ob = []
    for j in range(G_B):
        qs = _stack_heads(q_ref[:, W_A + LANES * j:W_A + LANES * (j + 1)], lo)
        s = _dot(qs, kb)
        sink = jnp.where(row_lo, sink_ref[0, j], sink_ref[1, j])
        ob.append(_unstack_heads(_softmax_pv([s], [lambda e: _dot_nt(e, vb)], sink), lo))
    b_ref[...] = _rms(jnp.concatenate(ob, axis=1), gb_ref[...]).astype(b_ref.dtype)


def _ctx_attn_call(sink, q, ka, va, kb, vb, g_a, g_b, seq):
    t = q.shape[0]
    row = lambda b: (b, 0)
    seq3 = lambda b: (b, 0, 0)
    const = lambda b: (0, 0)
    return pl.pallas_call(
        _ctx_attn_kernel,
        grid=(t // seq,),
        in_specs=[
            pl.BlockSpec(memory_space=pltpu.SMEM),
            pl.BlockSpec((seq, W_A + W_B), row),
            pl.BlockSpec((1, W_A, seq), seq3),
            pl.BlockSpec((1, W_A, seq), seq3),
            pl.BlockSpec((1, KV_W_B, seq), seq3),
            pl.BlockSpec((1, KV_W_B, seq), seq3),
            pl.BlockSpec((1, W_A), const),
            pl.BlockSpec((1, W_B), const),
        ],
        out_specs=(pl.BlockSpec((seq, W_A), row), pl.BlockSpec((seq, W_B), row)),
        out_shape=(jax.ShapeDtypeStruct((t, W_A), _BF16), jax.ShapeDtypeStruct((t, W_B), _BF16)),
        compiler_params=pltpu.CompilerParams(
            dimension_semantics=("parallel",), vmem_limit_bytes=VMEM_LIMIT),
        name="ctx_attn",
    )(sink, q, ka, va, kb, vb, g_a.reshape(1, W_A), g_b.reshape(1, W_B))


def _na_plan(rows):
    kr = min(NA_ROWS, rows)
    win = kr + NA_QROWS
    row_start = lambda r: min(max(r - kr // 2, 0), rows - kr)
    blocks, classes = [], []
    for r0 in range(0, rows, NA_QROWS):
        ws = min(row_start(r0), rows - win)
        keys = []
        for r in range(r0, r0 + NA_QROWS):
            key = (ws - r + NA_ROWS - 1, row_start(r) - ws)
            if key not in classes:
                classes.append(key)
            keys.append(classes.index(key))
        blocks.append((r0, ws, keys))
    return kr, win, blocks, classes


def _na_kernel(q_ref, k_ref, v_ref, ck_ref, cv_ref, rpb_ref, ga_ref, a_ref, bias_ref, o_scr):
    rows = q_ref.shape[0] // GRID_W
    kr, win, blocks, classes = _na_plan(rows)
    m = NA_QROWS * GRID_W
    lane = lax.broadcasted_iota(jnp.int32, (GRID_W, LANES), 1)
    lo = lane < HEAD_DIM
    lo_q = lax.broadcasted_iota(jnp.int32, (m, LANES), 1) < HEAD_DIM
    cq = lax.broadcasted_iota(jnp.int32, (GRID_W, LANES), 0)
    ck = lane & (GRID_W - 1)
    cs = jnp.clip(cq - NA_COLS // 2, 0, GRID_W - NA_COLS)
    valid = (ck >= cs) & (ck < cs + NA_COLS)
    neg = jnp.full((GRID_W, LANES), NEG_INF, _F32)
    n_drow = 2 * NA_ROWS - 1

    for p in range(W_A // LANES):
        sl = slice(LANES * p, LANES * (p + 1))
        for hh in range(2):
            t_lo, t_hi = [], []
            for d in range(n_drow):
                base = jnp.broadcast_to(rpb_ref[2 * p + hh, d:d + 1, :], (GRID_W, LANES))
                t_lo.append(pltpu.roll(base, LANES - (NA_COLS - 1), 1, stride=1, stride_axis=0))
                t_hi.append(pltpu.roll(base, GRID_W - (NA_COLS - 1), 1, stride=1, stride_axis=0))
            tiles = {}
            for ci, (d_first, off) in enumerate(classes):
                for j in range(win // 2):
                    d_of = lambda i: d_first + i if off <= i < off + kr else None
                    key = (d_of(2 * j), d_of(2 * j + 1))
                    if key not in tiles:
                        left = neg if key[0] is None else t_lo[key[0]]
                        right = neg if key[1] is None else t_hi[key[1]]
                        tiles[key] = jnp.where(valid, jnp.where(lo, left, right), NEG_INF)
                    bias_ref[ci, GRID_W * hh:GRID_W * (hh + 1), LANES * j:LANES * (j + 1)] = tiles[key]

        ckt = ck_ref[0, sl, :].astype(_BF16)
        cvt = cv_ref[0, sl, :].astype(_BF16)

        for r0, ws, keys in blocks:
            q0, k0 = r0 * GRID_W, ws * GRID_W
            qs = _stack_heads(q_ref[q0:q0 + m, sl], lo_q)
            kw = k_ref[k0:k0 + win * GRID_W, sl]
            vw = v_ref[k0:k0 + win * GRID_W, sl]
            bias = jnp.concatenate(
                [bias_ref[ci, GRID_W * hh:GRID_W * (hh + 1), :] for hh in range(2) for ci in keys], axis=0)
            s_nb = _dot_nt(qs, kw) + bias
            s_cx = _dot(qs, ckt)
            o2 = _softmax_pv([s_nb, s_cx], [lambda e, vw=vw: _dot(e, vw), lambda e: _dot_nt(e, cvt)])
            o_scr[q0:q0 + m, sl] = _unstack_heads(o2, lo_q)

    a_ref[...] = _rms(o_scr[...], ga_ref[...]).astype(a_ref.dtype)


def _na_call(q, k, v, ckt, cvt, rpb_pad, g_a, nb, seq):
    past = ckt.shape[2]
    _, win, _, classes = _na_plan(seq // GRID_W)
    blk = lambda b: (b, 0)
    return pl.pallas_call(
        _na_kernel,
        grid=(nb,),
        in_specs=[
            pl.BlockSpec((seq, W_A), blk),
            pl.BlockSpec((seq, W_A), blk),
            pl.BlockSpec((seq, W_A), blk),
            pl.BlockSpec((1, W_A, past), lambda b: (b, 0, 0)),
            pl.BlockSpec((1, W_A, past), lambda b: (b, 0, 0)),
            pl.BlockSpec((H_A, 2 * NA_ROWS - 1, LANES), lambda b: (0, 0, 0)),
            pl.BlockSpec((1, W_A), lambda b: (0, 0)),
        ],
        out_specs=pl.BlockSpec((seq, W_A), blk),
        out_shape=jax.ShapeDtypeStruct((nb * seq, W_A), _BF16),
        scratch_shapes=[pltpu.VMEM((len(classes), 2 * GRID_W, win * GRID_W), _F32),
                        pltpu.VMEM((seq, W_A), _F32)],
        compiler_params=pltpu.CompilerParams(
            dimension_semantics=("parallel",), vmem_limit_bytes=VMEM_LIMIT),
        name="na_attn",
    )(q, k, v, ckt, cvt, rpb_pad, g_a.reshape(1, W_A))


def _win_kernel(sink_ref, q_ref, k_ref, v_ref, ck_ref, cv_ref, gb_ref, b_ref, o_scr):
    t = q_ref.shape[0]
    nblk = t // BLOCK
    band = 3 * BLOCK
    lo = lax.broadcasted_iota(jnp.int32, (BLOCK, LANES), 1) < HEAD_DIM
    row_lo = lax.broadcasted_iota(jnp.int32, (2 * BLOCK, 1), 0) < BLOCK
    qi = lax.broadcasted_iota(jnp.int32, (2 * BLOCK, band), 0) & (BLOCK - 1)
    kj = lax.broadcasted_iota(jnp.int32, (2 * BLOCK, band), 1)
    rel = qi - kj
    ckt = ck_ref[0].astype(_BF16)
    cvt = cv_ref[0].astype(_BF16)

    starts = [min(max((n - 1) * BLOCK, 0), t - band) for n in range(nblk)]
    masks = {off: jnp.where(jnp.abs(rel + off) <= SWA_WINDOW, 0.0, NEG_INF)
             for off in sorted({n * BLOCK - starts[n] for n in range(nblk)})}

    for j in range(G_B):
        sl = slice(LANES * j, LANES * (j + 1))
        sink = jnp.where(row_lo, sink_ref[0, j], sink_ref[1, j])
        for n in range(nblk):
            q0, k0 = n * BLOCK, starts[n]
            qs = _stack_heads(q_ref[q0:q0 + BLOCK, sl], lo)
            s = _dot_nt(qs, k_ref[k0:k0 + band, :]) + masks[q0 - k0]
            s_cx = _dot(qs, ckt)
            vband = v_ref[k0:k0 + band, :]
            o2 = _softmax_pv([s, s_cx], [lambda e, vband=vband: _dot(e, vband), lambda e: _dot_nt(e, cvt)],
                             sink)
            o_scr[q0:q0 + BLOCK, sl] = _unstack_heads(o2, lo)

    b_ref[...] = _rms(o_scr[...], gb_ref[...]).astype(b_ref.dtype)


def _win_call(sink, q, k, v, ckt, cvt, g_b, nb, seq):
    past = ckt.shape[2]
    return pl.pallas_call(
        _win_kernel,
        grid=(nb,),
        in_specs=[
            pl.BlockSpec(memory_space=pltpu.SMEM),
            pl.BlockSpec((seq, W_B), lambda b: (b, W_A // W_B)),
            pl.BlockSpec((seq, KV_W_B), lambda b: (b, 0)),
            pl.BlockSpec((seq, KV_W_B), lambda b: (b, 0)),
            pl.BlockSpec((1, KV_W_B, past), lambda b: (b, 0, 0)),
            pl.BlockSpec((1, KV_W_B, past), lambda b: (b, 0, 0)),
            pl.BlockSpec((1, W_B), lambda b: (0, 0)),
        ],
        out_specs=pl.BlockSpec((seq, W_B), lambda b: (b, 0)),
        out_shape=jax.ShapeDtypeStruct((nb * seq, W_B), _BF16),
        scratch_shapes=[pltpu.VMEM((seq, W_B), _F32)],
        compiler_params=pltpu.CompilerParams(
            dimension_semantics=("parallel",), vmem_limit_bytes=VMEM_LIMIT),
        name="win_attn",
    )(sink, q, k, v, ckt, cvt, g_b.reshape(1, W_B))


def _prepare_out_weight(w_ref, w_scr):
    for r0 in range(0, W_A, CAST_COLS):
        w_scr[r0:r0 + CAST_COLS, :] = w_ref[r0:r0 + CAST_COLS, :].astype(_BF16)
    for i, h in enumerate(_QB_HEAD_ORDER):
        w_scr[W_A + HEAD_DIM * i:W_A + HEAD_DIM * (i + 1), :] = (
            w_ref[W_A + HEAD_DIM * h:W_A + HEAD_DIM * (h + 1), :].astype(_BF16))


def _zero_row(x, row):
    r0 = row - row % 8
    sub = lax.broadcasted_iota(jnp.int32, (8, 1), 0)
    slab = jnp.where(sub == row % 8, 0.0, x[r0:r0 + 8])
    parts = ([x[:r0]] if r0 else []) + [slab] + ([x[r0 + 8:]] if r0 + 8 < x.shape[0] else [])
    return jnp.concatenate(parts, axis=0)


def _conv3(u, cw, cb, seq_len):
    tm = u.shape[0]
    prev = pltpu.roll(u, 1, 0)
    nxt = pltpu.roll(u, tm - 1, 0)
    for s0 in range(0, tm, seq_len):
        prev = _zero_row(prev, s0)
        nxt = _zero_row(nxt, s0 + seq_len - 1)
    return prev * cw[0:1, :] + u * cw[1:2, :] + nxt * cw[2:3, :] + cb


def _gated_silu(gate, val):
    half = 0.5 * gate
    return ((half + half * jnp.tanh(half)) * val).astype(_BF16)


def _up(h2_ref, w):
    tm = h2_ref.shape[0]
    return jnp.concatenate(
        [_dot(h2_ref[r:r + FFN_ROW_BLOCK, :], w) for r in range(0, tm, FFN_ROW_BLOCK)], axis=0)


def _out_proj(a_ref, b_ref, x_ref, mod, gpost_ref, gffn_ref, wo_scr, x1_ref, h2_ref):
    gt1 = mod[:, 2 * D_MODEL:3 * D_MODEL]
    sh2 = mod[:, 3 * D_MODEL:4 * D_MODEL]
    sc2 = mod[:, 4 * D_MODEL:5 * D_MODEL]
    for r in range(0, x_ref.shape[0], FFN_ROW_BLOCK):
        rows = slice(r, r + FFN_ROW_BLOCK)
        y = _dot(a_ref[rows, :], wo_scr[0:W_A, :]) + _dot(b_ref[rows, :], wo_scr[W_A:W_A + W_B, :])
        x1 = x_ref[rows, :] + gt1 * _rms(y, gpost_ref[...])
        x1_ref[rows, :] = x1
        h2_ref[rows, :] = (_rms(x1, gffn_ref[...]) * (1 + sc2) + sh2).astype(h2_ref.dtype)


def _down_proj(act_ref, wd_ref, mod, gpost2_ref, x1_ref, out_ref):
    gt2 = mod[:, 5 * D_MODEL:6 * D_MODEL]
    for r in range(0, act_ref.shape[0], DOWN_ROW_BLOCK):
        rows = slice(r, r + DOWN_ROW_BLOCK)
        y = _dot(act_ref[rows, :], wd_ref[...])
        out_ref[rows, :] = x1_ref[rows, :] + gt2 * _rms(y, gpost2_ref[...])


def _tail_kernel(a_ref, b_ref, x_ref, mod_ref, gpost_ref, gffn_ref, gpost2_ref, wo_ref, wg_ref, wv_ref,
                 cw_ref, cb_ref, wd_ref, out_ref, wo_scr, x1_scr, h2_scr, act_scr, *, seq_len):
    i = pl.program_id(0)
    c = pl.program_id(1)
    tc = wg_ref.shape[1]
    mod = mod_ref[0]

    @pl.when((i == 0) & (c == 0))
    def _():
        _prepare_out_weight(wo_ref, wo_scr)

    @pl.when(c == 0)
    def _():
        _out_proj(a_ref, b_ref, x_ref, mod, gpost_ref, gffn_ref, wo_scr, x1_scr, h2_scr)

    def chunk(w_ref, col0):
        cols = pl.ds(pl.multiple_of(col0, tc), tc)
        return _conv3(_up(h2_scr, w_ref[...]), cw_ref[:, cols], cb_ref[:, cols], seq_len)

    act_scr[:, pl.ds(pl.multiple_of(c * tc, tc), tc)] = _gated_silu(
        chunk(wg_ref, c * tc), chunk(wv_ref, D_FF + c * tc))

    @pl.when(c == pl.num_programs(1) - 1)
    def _():
        _down_proj(act_scr, wd_ref, mod, gpost2_ref, x1_scr, out_ref)


def _tail_seq_kernel(a_ref, b_ref, x_ref, mod_ref, gpost_ref, gffn_ref, gpost2_ref, wo_ref, wu_ref,
                     cw_ref, cb_ref, wd_ref, out_ref, wo_scr, h2_scr, act_scr, *, seq_len):
    mod = mod_ref[0]

    @pl.when(pl.program_id(0) == 0)
    def _():
        _prepare_out_weight(wo_ref, wo_scr)

    _out_proj(a_ref, b_ref, x_ref, mod, gpost_ref, gffn_ref, wo_scr, out_ref, h2_scr)
    tc = FF_CHUNK

    def chunk(col0):
        cols = slice(col0, col0 + tc)
        return _conv3(_up(h2_scr, wu_ref[:, cols]), cw_ref[:, cols], cb_ref[:, cols], seq_len)

    for c0 in range(0, D_FF, tc):
        act_scr[:, c0:c0 + tc] = _gated_silu(chunk(c0), chunk(D_FF + c0))
    _down_proj(act_scr, wd_ref, mod, gpost2_ref, out_ref, out_ref)


def _tail_call(a, b, x2d, mod3, mod_row, g_post, g_ffn, g_post2, w_out, w_up, conv_w, conv_b, w_down,
               seq_len, name):
    t = x2d.shape[0]
    tm = FFN_TILE
    tc = FF_CHUNK
    nc = D_FF // tc
    row = lambda i, c: (i, 0)
    const = lambda i, c: (0, 0)
    return pl.pallas_call(
        functools.partial(_tail_kernel, seq_len=seq_len),
        grid=(t // tm, nc),
        in_specs=[
            pl.BlockSpec((tm, W_A), row),
            pl.BlockSpec((tm, W_B), row),
            pl.BlockSpec((tm, D_MODEL), row),
            pl.BlockSpec((1, 1, 6 * D_MODEL), lambda i, c: (mod_row(i), 0, 0)),
            pl.BlockSpec((1, D_MODEL), const),
            pl.BlockSpec((1, D_MODEL), const),
            pl.BlockSpec((1, D_MODEL), const),
            _resident((W_A + W_B, D_MODEL)),
            pl.BlockSpec((D_MODEL, tc), lambda i, c: (0, c)),
            pl.BlockSpec((D_MODEL, tc), lambda i, c: (0, nc + c)),
            _resident((3, 2 * D_FF)),
            _resident((1, 2 * D_FF)),
            _resident((D_FF, D_MODEL)),
        ],
        out_specs=pl.BlockSpec((tm, D_MODEL), row),
        out_shape=jax.ShapeDtypeStruct((t, D_MODEL), _F32),
        scratch_shapes=[
            pltpu.VMEM((W_A + W_B, D_MODEL), _BF16),
            pltpu.VMEM((tm, D_MODEL), _F32),
            pltpu.VMEM((tm, D_MODEL), _BF16),
            pltpu.VMEM((tm, D_FF), _BF16),
        ],
        compiler_params=pltpu.CompilerParams(
            dimension_semantics=("arbitrary", "arbitrary"), vmem_limit_bytes=TAIL_VMEM_LIMIT),
        name=name,
    )(a, b, x2d, mod3, g_post.reshape(1, D_MODEL), g_ffn.reshape(1, D_MODEL), g_post2.reshape(1, D_MODEL),
      w_out, w_up, w_up, conv_w, conv_b.reshape(1, 2 * D_FF), w_down)


def _tail_seq_call(a, b, x2d, mod3, mod_row, g_post, g_ffn, g_post2, w_out, w_up, conv_w, conv_b, w_down,
                   seq_len, name):
    t = x2d.shape[0]
    tm = SEQ_TAIL_TILE
    row = lambda i: (i, 0)
    const = lambda i: (0, 0)
    return pl.pallas_call(
        functools.partial(_tail_seq_kernel, seq_len=seq_len),
        grid=(t // tm,),
        in_specs=[
            pl.BlockSpec((tm, W_A), row),
            pl.BlockSpec((tm, W_B), row),
            pl.BlockSpec((tm, D_MODEL), row),
            pl.BlockSpec((1, 1, 6 * D_MODEL), lambda i: (mod_row(i), 0, 0)),
            pl.BlockSpec((1, D_MODEL), const),
            pl.BlockSpec((1, D_MODEL), const),
            pl.BlockSpec((1, D_MODEL), const),
            _resident((W_A + W_B, D_MODEL)),
            _resident((D_MODEL, 2 * D_FF)),
            _resident((3, 2 * D_FF)),
            _resident((1, 2 * D_FF)),
            _resident((D_FF, D_MODEL)),
        ],
        out_specs=pl.BlockSpec((tm, D_MODEL), row),
        out_shape=jax.ShapeDtypeStruct((t, D_MODEL), _F32),
        scratch_shapes=[
            pltpu.VMEM((W_A + W_B, D_MODEL), _BF16),
            pltpu.VMEM((tm, D_MODEL), _BF16),
            pltpu.VMEM((tm, D_FF), _BF16),
        ],
        compiler_params=pltpu.CompilerParams(
            dimension_semantics=("arbitrary",), vmem_limit_bytes=VMEM_LIMIT),
        name=name,
    )(a, b, x2d, mod3, g_post.reshape(1, D_MODEL), g_ffn.reshape(1, D_MODEL), g_post2.reshape(1, D_MODEL),
      w_out, w_up, conv_w, conv_b.reshape(1, 2 * D_FF), w_down)


def _rope_tables(seq):
    n = ROPE_HALF
    t = np.arange(seq)
    lane = np.arange(HEAD_DIM)
    pos = np.where(lane[None, :] < HEAD_DIM // 2, (t // GRID_W)[:, None], (t % GRID_W)[:, None])
    inv = 1.0 / (ROPE_THETA ** (np.arange(n, dtype=np.float64) / n))
    ang = pos.astype(np.float64) * inv[lane % n][None, :]
    sign = np.where((lane & n) == 0, -1.0, 1.0)[None, :]
    cos = np.tile(np.cos(ang), (1, LANES // HEAD_DIM)).astype(np.float32)
    sin = np.tile(np.sin(ang) * sign, (1, LANES // HEAD_DIM)).astype(np.float32)
    return jnp.asarray(cos), jnp.asarray(sin)


def _to_head_dim_token(cache_l):
    b, l, h, d = cache_l.shape
    return jnp.transpose(cache_l, (0, 2, 3, 1)).reshape(b, h * d, l)


def _from_head_dim_token(kv_t, heads):
    b, _, l = kv_t.shape
    return jnp.transpose(kv_t.reshape(b, 1, heads, HEAD_DIM, l), (0, 1, 4, 2, 3))


def kernel(x_prompt, x_sample, cache_a_k, cache_a_v, cache_b_k, cache_b_v, c, c_ctx, w_mod, b_mod,
           g_mix_pre, g_mix_post, g_ffn_pre, g_ffn_post, w_in, rpb_a, sink_b, g_grp_a, g_grp_b,
           w_out, w_up, conv_w, conv_b, w_down):
    bp, lp, _ = x_prompt.shape
    bs, ts, _ = x_sample.shape
    assert w_in.shape[0] == 1
    xp = x_prompt.reshape(bp * lp, D_MODEL)
    xs = x_sample.reshape(bs * ts, D_MODEL)
    cond8 = jnp.concatenate([c_ctx[None], c, jnp.zeros((8 - 1 - bs, D_MODEL), _F32)], axis=0)
    cos_tab, sin_tab = _rope_tables(ts)

    l = 0
    mod3 = _mod_call(cond8, w_mod[l], b_mod[l]).reshape(8, 1, 6 * D_MODEL)
    g_b_p = g_grp_b[l].reshape(KV_B, G_B, HEAD_DIM).transpose(1, 0, 2).reshape(W_B)
    w_up_b = w_up[l].astype(_BF16)
    w_down_b = w_down[l].astype(_BF16)
    rpb_pad = jnp.pad(rpb_a[l], ((0, 0), (0, 0), (0, LANES - (2 * NA_COLS - 1))))

    prompt_row = lambda i: 0
    s_tiles = ts // TOKEN_TILE
    sample_row = lambda i: 1 + i // s_tiles
    sample_row_ffn = lambda i: 1 + i // (ts // FFN_TILE)

    q_p, kat_p, vat_p, kbt_p, vbt_p = _inproj_call(xp, mod3, prompt_row, g_mix_pre[l], w_in[l], None, lp,
                                                   True, "inproj_prompt")
    a_p, b_p = _ctx_attn_call(sink_b[l], q_p, kat_p, vat_p, kbt_p, vbt_p, g_grp_a[l], g_b_p, lp)
    y_p = _tail_seq_call(a_p, b_p, xp, mod3, prompt_row, g_mix_post[l], g_ffn_pre[l], g_ffn_post[l], w_out[l],
                         w_up_b, conv_w[l], conv_b[l], w_down_b, lp, "tail_prompt")

    q_s, ka_s, va_s, kb_s, vb_s = _inproj_call(xs, mod3, sample_row, g_mix_pre[l], w_in[l],
                                               (cos_tab, sin_tab), ts, False, "inproj_sample")
    a_s = _na_call(q_s, ka_s, va_s, _to_head_dim_token(cache_a_k[:, l]), _to_head_dim_token(cache_a_v[:, l]),
                   rpb_pad, g_grp_a[l], bs, ts)
    b_s = _win_call(sink_b[l], q_s, kb_s, vb_s, _to_head_dim_token(cache_b_k[:, l]),
                    _to_head_dim_token(cache_b_v[:, l]), g_b_p, bs, ts)
    y_s = _tail_call(a_s, b_s, xs, mod3, sample_row_ffn, g_mix_post[l], g_ffn_pre[l], g_ffn_post[l], w_out[l],
                     w_up_b, conv_w[l], conv_b[l], w_down_b, ts, "tail_sample")

    return (y_p.reshape(bp, lp, D_MODEL), y_s.reshape(bs, ts, D_MODEL),
            _from_head_dim_token(kat_p, H_A), _from_head_dim_token(vat_p, H_A),
            _from_head_dim_token(kbt_p, KV_B), _from_head_dim_token(vbt_p, KV_B))
```

```python
import functools

import numpy as np
import jax
import jax.numpy as jnp
from jax import lax
from jax.experimental import pallas as pl
from jax.experimental.pallas import tpu as pltpu

D_MODEL = 1024
HEAD_DIM = 64
H_A = 8
H_B = 8
KV_B = 2
G_B = H_B // KV_B
W_A = H_A * HEAD_DIM
W_B = H_B * HEAD_DIM
KV_W_B = KV_B * HEAD_DIM
IN_WIDTH = 3 * W_A + W_B + 2 * KV_W_B
GRID_W = 64
NA_ROWS = 8
NA_COLS = 16
SWA_WINDOW = 128
BLOCK = 128
D_FF = 2816
ROPE_THETA = 10000.0
EPS = 1e-6
NEG_INF = -1e30
Q_SCALE = HEAD_DIM ** -0.5
ROPE_HALF = HEAD_DIM // 4

LANES = 128
FF_CHUNK = 256
TOKEN_TILE = 512
FFN_ROW_BLOCK = 256
DOWN_ROW_BLOCK = 256
SEQ_TAIL_TILE = 512
PROJ_ROW_BLOCK = 256
MOD_STREAMS = 2
MOD_ROW_BLOCK = 128
NA_QROWS = 2
CAST_COLS = 512
VMEM_LIMIT = 48 * 1024 * 1024
TAIL_VMEM_LIMIT = 56 * 1024 * 1024

C_QA, C_KA, C_VA, C_QB, C_KB, C_VB = 0, W_A, 2 * W_A, 3 * W_A, 3 * W_A + W_B, 3 * W_A + W_B + KV_W_B

_QB_HEAD_ORDER = tuple(kv * G_B + g for g in range(G_B) for kv in range(KV_B))

_BF16 = jnp.bfloat16
_F32 = jnp.float32


def _dot(a, b):
    return jnp.dot(a, b, preferred_element_type=_F32)


def _dot_nt(a, b):
    return lax.dot_general(a, b, (((1,), (1,)), ((), ())), preferred_element_type=_F32)


def _rms(x, g):
    var = jnp.mean(x * x, axis=-1, keepdims=True)
    return x * lax.rsqrt(var + EPS) * g


def _stack_heads(q2, lo):
    zero = jnp.zeros_like(q2)
    return jnp.concatenate([jnp.where(lo, q2, zero), jnp.where(lo, zero, q2)], axis=0)


def _unstack_heads(o2, lo):
    m = o2.shape[0] // 2
    return jnp.where(lo, o2[:m], o2[m:])


def _softmax_pv(scores, pv, sink=None):
    m = scores[0].max(axis=-1, keepdims=True)
    for s in scores[1:]:
        m = jnp.maximum(m, s.max(axis=-1, keepdims=True))
    if sink is not None:
        m = jnp.maximum(m, sink)
    denom = None
    out = None
    for s, f in zip(scores, pv):
        e = jnp.exp(s - m)
        l = e.sum(axis=-1, keepdims=True)
        o = f(e.astype(_BF16))
        denom = l if denom is None else denom + l
        out = o if out is None else out + o
    if sink is not None:
        denom = denom + jnp.exp(sink - m)
    return out / denom


def _resident(shape):
    return pl.BlockSpec(shape, lambda *_: (0,) * len(shape), pipeline_mode=pl.Buffered(1))


def _mod_kernel(c_ref, *refs):
    w_refs, b_ref, o_ref = refs[:-2], refs[-2], refs[-1]
    i = pl.program_id(0)

    @pl.when(i == 0)
    def _():
        o_ref[...] = jnp.broadcast_to(b_ref[...], o_ref.shape)

    acc = o_ref[...]
    for j, w_ref in enumerate(w_refs):
        c = c_ref[i * len(w_refs) + j]
        s = (c * jax.nn.sigmoid(c)).astype(_BF16)
        acc = acc + _dot(s, w_ref[...].astype(_BF16))
    o_ref[...] = acc


def _mod_call(cond8, w_mod, b_mod):
    n = w_mod.shape[1]
    kb = MOD_ROW_BLOCK
    nblk = D_MODEL // kb
    cond_blocks = cond8.reshape(8, nblk, kb).transpose(1, 0, 2)
    return pl.pallas_call(
        _mod_kernel,
        grid=(nblk // MOD_STREAMS,),
        in_specs=[pl.BlockSpec((nblk, 8, kb), lambda i: (0, 0, 0))]
        + [pl.BlockSpec((kb, n), lambda i, j=j: (i * MOD_STREAMS + j, 0)) for j in range(MOD_STREAMS)]
        + [pl.BlockSpec((1, n), lambda i: (0, 0))],
        out_specs=pl.BlockSpec((8, n), lambda i: (0, 0)),
        out_shape=jax.ShapeDtypeStruct((8, n), _F32),
        compiler_params=pltpu.CompilerParams(dimension_semantics=("arbitrary",), vmem_limit_bytes=VMEM_LIMIT),
        name="mod",
    )(cond_blocks, *([w_mod] * MOD_STREAMS), b_mod.reshape(1, n))


def _prepare_in_weight(w_ref, w_scr):
    for c0 in list(range(0, C_QB, CAST_COLS)) + [C_KB]:
        n = min(CAST_COLS, IN_WIDTH - c0)
        w_scr[:, c0:c0 + n] = w_ref[:, c0:c0 + n].astype(_BF16)
    lo = lax.broadcasted_iota(jnp.int32, (D_MODEL, LANES), 1) < HEAD_DIM
    for j in range(G_B):
        h_lo, h_hi = _QB_HEAD_ORDER[2 * j], _QB_HEAD_ORDER[2 * j + 1]

        def head_at(h, want_hi):
            blk = w_ref[:, C_QB + LANES * (h // 2):C_QB + LANES * (h // 2 + 1)]
            return blk if (h % 2 == 1) == want_hi else pltpu.roll(blk, HEAD_DIM, 1)

        w_scr[:, C_QB + LANES * j:C_QB + LANES * (j + 1)] = jnp.where(
            lo, head_at(h_lo, False), head_at(h_hi, True)).astype(_BF16)


def _rope(z, cos, sin):
    lane = lax.broadcasted_iota(jnp.int32, z.shape, 1)
    partner = jnp.where((lane & ROPE_HALF) == 0,
                        pltpu.roll(z, LANES - ROPE_HALF, 1), pltpu.roll(z, ROPE_HALF, 1))
    return z * cos + partner * sin


def _inproj_kernel(*refs, rope, kv_transposed):
    if rope:
        x_ref, mod_ref, g_ref, w_ref, cos_ref, sin_ref, q_ref, ka_ref, va_ref, kb_ref, vb_ref, w_scr = refs
    else:
        x_ref, mod_ref, g_ref, w_ref, q_ref, ka_ref, va_ref, kb_ref, vb_ref, w_scr = refs

    @pl.when(pl.program_id(0) == 0)
    def _():
        _prepare_in_weight(w_ref, w_scr)

    mod = mod_ref[0]
    sh1 = mod[:, 0:D_MODEL]
    sc1 = mod[:, D_MODEL:2 * D_MODEL]
    tm = x_ref.shape[0]
    rb = PROJ_ROW_BLOCK

    for r in range(0, tm, rb):
        rows = slice(r, r + rb)
        h = (_rms(x_ref[rows, :], g_ref[...]) * (1 + sc1) + sh1).astype(_BF16)

        def proj(c0, n):
            return _dot(h, w_scr[:, c0:c0 + n])

        def put_kv(ref, z):
            if not kv_transposed:
                ref[rows, :] = z.astype(ref.dtype)
                return
            l = ref.shape[2]
            for t0 in range(0, rb, l):
                for c0 in range(0, z.shape[1], LANES):
                    ref[(r + t0) // l, c0:c0 + LANES, :] = z[t0:t0 + l, c0:c0 + LANES].T.astype(ref.dtype)

        q_ref[rows, 0:W_A] = (proj(C_QA, W_A) * Q_SCALE).astype(q_ref.dtype)
        put_kv(ka_ref, proj(C_KA, W_A))
        put_kv(va_ref, proj(C_VA, W_A))
        zkv = proj(C_KB, 2 * KV_W_B)
        zkb, zvb = zkv[:, :KV_W_B], zkv[:, KV_W_B:]
        zqb = proj(C_QB, W_B)
        put_kv(vb_ref, zvb)
        if rope:
            cos = cos_ref[rows, :]
            sin = sin_ref[rows, :]
            for j in range(W_B // LANES):
                q_ref[rows, W_A + LANES * j:W_A + LANES * (j + 1)] = (
                    _rope(zqb[:, LANES * j:LANES * (j + 1)], cos, sin) * Q_SCALE).astype(q_ref.dtype)
            put_kv(kb_ref, _rope(zkb, cos, sin))
        else:
            q_ref[rows, W_A:W_A + W_B] = (zqb * Q_SCALE).astype(q_ref.dtype)
            put_kv(kb_ref, zkb)


def _inproj_call(x2d, mod3, mod_row, g_pre, w_in, rope_tabs, seq, kv_transposed, name):
    t = x2d.shape[0]
    tm = TOKEN_TILE
    rope = rope_tabs is not None
    in_specs = [
        pl.BlockSpec((tm, D_MODEL), lambda i: (i, 0)),
        pl.BlockSpec((1, 1, 6 * D_MODEL), lambda i: (mod_row(i), 0, 0)),
        pl.BlockSpec((1, D_MODEL), lambda i: (0, 0)),
        _resident((D_MODEL, IN_WIDTH)),
    ]
    args = [x2d, mod3, g_pre.reshape(1, D_MODEL), w_in]
    if rope:
        tiles_per_seq = rope_tabs[0].shape[0] // tm
        in_specs += [pl.BlockSpec((tm, LANES), lambda i: (i % tiles_per_seq, 0))] * 2
        args += list(rope_tabs)
    if kv_transposed:
        nseq = tm // seq
        kv_shape = lambda w: jax.ShapeDtypeStruct((t // seq, w, seq), _F32)
        kv_spec = lambda w: pl.BlockSpec((nseq, w, seq), lambda i: (i, 0, 0))
    else:
        kv_shape = lambda w: jax.ShapeDtypeStruct((t, w), _BF16)
        kv_spec = lambda w: pl.BlockSpec((tm, w), lambda i: (i, 0))
    out_shape = (jax.ShapeDtypeStruct((t, W_A + W_B), _BF16),
                 kv_shape(W_A), kv_shape(W_A), kv_shape(KV_W_B), kv_shape(KV_W_B))
    out_specs = (pl.BlockSpec((tm, W_A + W_B), lambda i: (i, 0)),
                 kv_spec(W_A), kv_spec(W_A), kv_spec(KV_W_B), kv_spec(KV_W_B))
    return pl.pallas_call(
        functools.partial(_inproj_kernel, rope=rope, kv_transposed=kv_transposed),
        grid=(t // tm,),
        in_specs=in_specs,
        out_specs=out_specs,
        out_shape=out_shape,
        scratch_shapes=[pltpu.VMEM((D_MODEL, IN_WIDTH), _BF16)],
        compiler_params=pltpu.CompilerParams(
            dimension_semantics=("arbitrary",), vmem_limit_bytes=VMEM_LIMIT),
        name=name,
    )(*args)


def _ctx_attn_kernel(sink_ref, q_ref, ka_ref, va_ref, kb_ref, vb_ref, ga_ref, gb_ref, a_ref, b_ref):
    l = q_ref.shape[0]
    lo = lax.broadcasted_iota(jnp.int32, (l, LANES), 1) < HEAD_DIM
    row_lo = lax.broadcasted_iota(jnp.int32, (2 * l, 1), 0) < l
    oa = []
    for p in range(W_A // LANES):
        sl = slice(LANES * p, LANES * (p + 1))
        qs = _stack_heads(q_ref[:, sl], lo)
        s = _dot(qs, ka_ref[0, sl, :].astype(_BF16))
        vt = va_ref[0, sl, :].astype(_BF16)
        oa.append(_unstack_heads(_softmax_pv([s], [lambda e, vt=vt: _dot_nt(e, vt)]), lo))
    a_ref[...] = _rms(jnp.concatenate(oa, axis=1), ga_ref[...]).astype(a_ref.dtype)
    kb = kb_ref[0].astype(_BF16)
    vb = vb_ref[0].astype(_BF16)
    ob = []
    for j in range(G_B):
        qs = _stack_heads(q_ref[:, W_A + LANES * j:W_A + LANES * (j + 1)], lo)
        s = _dot(qs, kb)
        sink = jnp.where(row_lo, sink_ref[0, j], sink_ref[1, j])
        ob.append(_unstack_heads(_softmax_pv([s], [lambda e: _dot_nt(e, vb)], sink), lo))
    b_ref[...] = _rms(jnp.concatenate(ob, axis=1), gb_ref[...]).astype(b_ref.dtype)


def _ctx_attn_call(sink, q, ka, va, kb, vb, g_a, g_b, seq):
    t = q.shape[0]
    row = lambda b: (b, 0)
    seq3 = lambda b: (b, 0, 0)
    const = lambda b: (0, 0)
    return pl.pallas_call(
        _ctx_attn_kernel,
        grid=(t // seq,),
        in_specs=[
            pl.BlockSpec(memory_space=pltpu.SMEM),
            pl.BlockSpec((seq, W_A + W_B), row),
            pl.BlockSpec((1, W_A, seq), seq3),
            pl.BlockSpec((1, W_A, seq), seq3),
            pl.BlockSpec((1, KV_W_B, seq), seq3),
            pl.BlockSpec((1, KV_W_B, seq), seq3),
            pl.BlockSpec((1, W_A), const),
            pl.BlockSpec((1, W_B), const),
        ],
        out_specs=(pl.BlockSpec((seq, W_A), row), pl.BlockSpec((seq, W_B), row)),
        out_shape=(jax.ShapeDtypeStruct((t, W_A), _BF16), jax.ShapeDtypeStruct((t, W_B), _BF16)),
        compiler_params=pltpu.CompilerParams(
            dimension_semantics=("parallel",), vmem_limit_bytes=VMEM_LIMIT),
        name="ctx_attn",
    )(sink, q, ka, va, kb, vb, g_a.reshape(1, W_A), g_b.reshape(1, W_B))


def _na_plan(rows):
    kr = min(NA_ROWS, rows)
    win = kr + NA_QROWS
    row_start = lambda r: min(max(r - kr // 2, 0), rows - kr)
    blocks, classes = [], []
    for r0 in range(0, rows, NA_QROWS):
        ws = min(row_start(r0), rows - win)
        keys = []
        for r in range(r0, r0 + NA_QROWS):
            key = (ws - r + NA_ROWS - 1, row_start(r) - ws)
            if key not in classes:
                classes.append(key)
            keys.append(classes.index(key))
        blocks.append((r0, ws, keys))
    return kr, win, blocks, classes


def _na_kernel(q_ref, k_ref, v_ref, ck_ref, cv_ref, rpb_ref, ga_ref, a_ref, bias_ref, o_scr):
    rows = q_ref.shape[0] // GRID_W
    kr, win, blocks, classes = _na_plan(rows)
    m = NA_QROWS * GRID_W
    lane = lax.broadcasted_iota(jnp.int32, (GRID_W, LANES), 1)
    lo = lane < HEAD_DIM
    lo_q = lax.broadcasted_iota(jnp.int32, (m, LANES), 1) < HEAD_DIM
    cq = lax.broadcasted_iota(jnp.int32, (GRID_W, LANES), 0)
    ck = lane & (GRID_W - 1)
    cs = jnp.clip(cq - NA_COLS // 2, 0, GRID_W - NA_COLS)
    valid = (ck >= cs) & (ck < cs + NA_COLS)
    neg = jnp.full((GRID_W, LANES), NEG_INF, _F32)
    n_drow = 2 * NA_ROWS - 1

    for p in range(W_A // LANES):
        sl = slice(LANES * p, LANES * (p + 1))
        for hh in range(2):
            t_lo, t_hi = [], []
            for d in range(n_drow):
                base = jnp.broadcast_to(rpb_ref[2 * p + hh, d:d + 1, :], (GRID_W, LANES))
                t_lo.append(pltpu.roll(base, LANES - (NA_COLS - 1), 1, stride=1, stride_axis=0))
                t_hi.append(pltpu.roll(base, GRID_W - (NA_COLS - 1), 1, stride=1, stride_axis=0))
            tiles = {}
            for ci, (d_first, off) in enumerate(classes):
                for j in range(win // 2):
                    d_of = lambda i: d_first + i if off <= i < off + kr else None
                    key = (d_of(2 * j), d_of(2 * j + 1))
                    if key not in tiles:
                        left = neg if key[0] is None else t_lo[key[0]]
                        right = neg if key[1] is None else t_hi[key[1]]
                        tiles[key] = jnp.where(valid, jnp.where(lo, left, right), NEG_INF)
                    bias_ref[ci, GRID_W * hh:GRID_W * (hh + 1), LANES * j:LANES * (j + 1)] = tiles[key]

        ckt = ck_ref[0, sl, :].astype(_BF16)
        cvt = cv_ref[0, sl, :].astype(_BF16)

        for r0, ws, keys in blocks:
            q0, k0 = r0 * GRID_W, ws * GRID_W
            qs = _stack_heads(q_ref[q0:q0 + m, sl], lo_q)
            kw = k_ref[k0:k0 + win * GRID_W, sl]
            vw = v_ref[k0:k0 + win * GRID_W, sl]
            bias = jnp.concatenate(
                [bias_ref[ci, GRID_W * hh:GRID_W * (hh + 1), :] for hh in range(2) for ci in keys], axis=0)
            s_nb = _dot_nt(qs, kw) + bias
            s_cx = _dot(qs, ckt)
            o2 = _softmax_pv([s_nb, s_cx], [lambda e, vw=vw: _dot(e, vw), lambda e: _dot_nt(e, cvt)])
            o_scr[q0:q0 + m, sl] = _unstack_heads(o2, lo_q)

    a_ref[...] = _rms(o_scr[...], ga_ref[...]).astype(a_ref.dtype)


def _na_call(q, k, v, ckt, cvt, rpb_pad, g_a, nb, seq):
    past = ckt.shape[2]
    _, win, _, classes = _na_plan(seq // GRID_W)
    blk = lambda b: (b, 0)
    return pl.pallas_call(
        _na_kernel,
        grid=(nb,),
        in_specs=[
            pl.BlockSpec((seq, W_A), blk),
            pl.BlockSpec((seq, W_A), blk),
            pl.BlockSpec((seq, W_A), blk),
            pl.BlockSpec((1, W_A, past), lambda b: (b, 0, 0)),
            pl.BlockSpec((1, W_A, past), lambda b: (b, 0, 0)),
            pl.BlockSpec((H_A, 2 * NA_ROWS - 1, LANES), lambda b: (0, 0, 0)),
            pl.BlockSpec((1, W_A), lambda b: (0, 0)),
        ],
        out_specs=pl.BlockSpec((seq, W_A), blk),
        out_shape=jax.ShapeDtypeStruct((nb * seq, W_A), _BF16),
        scratch_shapes=[pltpu.VMEM((len(classes), 2 * GRID_W, win * GRID_W), _F32),
                        pltpu.VMEM((seq, W_A), _F32)],
        compiler_params=pltpu.CompilerParams(
            dimension_semantics=("parallel",), vmem_limit_bytes=VMEM_LIMIT),
        name="na_attn",
    )(q, k, v, ckt, cvt, rpb_pad, g_a.reshape(1, W_A))


def _win_kernel(sink_ref, q_ref, k_ref, v_ref, ck_ref, cv_ref, gb_ref, b_ref, o_scr):
    t = q_ref.shape[0]
    nblk = t // BLOCK
    band = 3 * BLOCK
    lo = lax.broadcasted_iota(jnp.int32, (BLOCK, LANES), 1) < HEAD_DIM
    row_lo = lax.broadcasted_iota(jnp.int32, (2 * BLOCK, 1), 0) < BLOCK
    qi = lax.broadcasted_iota(jnp.int32, (2 * BLOCK, band), 0) & (BLOCK - 1)
    kj = lax.broadcasted_iota(jnp.int32, (2 * BLOCK, band), 1)
    rel = qi - kj
    ckt = ck_ref[0].astype(_BF16)
    cvt = cv_ref[0].astype(_BF16)

    starts = [min(max((n - 1) * BLOCK, 0), t - band) for n in range(nblk)]
    masks = {off: jnp.where(jnp.abs(rel + off) <= SWA_WINDOW, 0.0, NEG_INF)
             for off in sorted({n * BLOCK - starts[n] for n in range(nblk)})}

    for j in range(G_B):
        sl = slice(LANES * j, LANES * (j + 1))
        sink = jnp.where(row_lo, sink_ref[0, j], sink_ref[1, j])
        for n in range(nblk):
            q0, k0 = n * BLOCK, starts[n]
            qs = _stack_heads(q_ref[q0:q0 + BLOCK, sl], lo)
            s = _dot_nt(qs, k_ref[k0:k0 + band, :]) + masks[q0 - k0]
            s_cx = _dot(qs, ckt)
            vband = v_ref[k0:k0 + band, :]
            o2 = _softmax_pv([s, s_cx], [lambda e, vband=vband: _dot(e, vband), lambda e: _dot_nt(e, cvt)],
                             sink)
            o_scr[q0:q0 + BLOCK, sl] = _unstack_heads(o2, lo)

    b_ref[...] = _rms(o_scr[...], gb_ref[...]).astype(b_ref.dtype)


def _win_call(sink, q, k, v, ckt, cvt, g_b, nb, seq):
    past = ckt.shape[2]
    return pl.pallas_call(
        _win_kernel,
        grid=(nb,),
        in_specs=[
            pl.BlockSpec(memory_space=pltpu.SMEM),
            pl.BlockSpec((seq, W_B), lambda b: (b, W_A // W_B)),
            pl.BlockSpec((seq, KV_W_B), lambda b: (b, 0)),
            pl.BlockSpec((seq, KV_W_B), lambda b: (b, 0)),
            pl.BlockSpec((1, KV_W_B, past), lambda b: (b, 0, 0)),
            pl.BlockSpec((1, KV_W_B, past), lambda b: (b, 0, 0)),
            pl.BlockSpec((1, W_B), lambda b: (0, 0)),
        ],
        out_specs=pl.BlockSpec((seq, W_B), lambda b: (b, 0)),
        out_shape=jax.ShapeDtypeStruct((nb * seq, W_B), _BF16),
        scratch_shapes=[pltpu.VMEM((seq, W_B), _F32)],
        compiler_params=pltpu.CompilerParams(
            dimension_semantics=("parallel",), vmem_limit_bytes=VMEM_LIMIT),
        name="win_attn",
    )(sink, q, k, v, ckt, cvt, g_b.reshape(1, W_B))


def _prepare_out_weight(w_ref, w_scr):
    for r0 in range(0, W_A, CAST_COLS):
        w_scr[r0:r0 + CAST_COLS, :] = w_ref[r0:r0 + CAST_COLS, :].astype(_BF16)
    for i, h in enumerate(_QB_HEAD_ORDER):
        w_scr[W_A + HEAD_DIM * i:W_A + HEAD_DIM * (i + 1), :] = (
            w_ref[W_A + HEAD_DIM * h:W_A + HEAD_DIM * (h + 1), :].astype(_BF16))


def _zero_row(x, row):
    r0 = row - row % 8
    sub = lax.broadcasted_iota(jnp.int32, (8, 1), 0)
    slab = jnp.where(sub == row % 8, 0.0, x[r0:r0 + 8])
    parts = ([x[:r0]] if r0 else []) + [slab] + ([x[r0 + 8:]] if r0 + 8 < x.shape[0] else [])
    return jnp.concatenate(parts, axis=0)


def _conv3(u, cw, cb, seq_len):
    tm = u.shape[0]
    prev = pltpu.roll(u, 1, 0)
    nxt = pltpu.roll(u, tm - 1, 0)
    for s0 in range(0, tm, seq_len):
        prev = _zero_row(prev, s0)
        nxt = _zero_row(nxt, s0 + seq_len - 1)
    return prev * cw[0:1, :] + u * cw[1:2, :] + nxt * cw[2:3, :] + cb


def _gated_silu(gate, val):
    half = 0.5 * gate
    return ((half + half * jnp.tanh(half)) * val).astype(_BF16)


def _up(h2_ref, w):
    tm = h2_ref.shape[0]
    return jnp.concatenate(
        [_dot(h2_ref[r:r + FFN_ROW_BLOCK, :], w) for r in range(0, tm, FFN_ROW_BLOCK)], axis=0)


def _out_proj(a_ref, b_ref, x_ref, mod, gpost_ref, gffn_ref, wo_scr, x1_ref, h2_ref):
    gt1 = mod[:, 2 * D_MODEL:3 * D_MODEL]
    sh2 = mod[:, 3 * D_MODEL:4 * D_MODEL]
    sc2 = mod[:, 4 * D_MODEL:5 * D_MODEL]
    for r in range(0, x_ref.shape[0], FFN_ROW_BLOCK):
        rows = slice(r, r + FFN_ROW_BLOCK)
        y = _dot(a_ref[rows, :], wo_scr[0:W_A, :]) + _dot(b_ref[rows, :], wo_scr[W_A:W_A + W_B, :])
        x1 = x_ref[rows, :] + gt1 * _rms(y, gpost_ref[...])
        x1_ref[rows, :] = x1
        h2_ref[rows, :] = (_rms(x1, gffn_ref[...]) * (1 + sc2) + sh2).astype(h2_ref.dtype)


def _down_proj(act_ref, wd_ref, mod, gpost2_ref, x1_ref, out_ref):
    gt2 = mod[:, 5 * D_MODEL:6 * D_MODEL]
    for r in range(0, act_ref.shape[0], DOWN_ROW_BLOCK):
        rows = slice(r, r + DOWN_ROW_BLOCK)
        y = _dot(act_ref[rows, :], wd_ref[...])
        out_ref[rows, :] = x1_ref[rows, :] + gt2 * _rms(y, gpost2_ref[...])


def _tail_seq_kernel(a_ref, b_ref, x_ref, mod_ref, gpost_ref, gffn_ref, gpost2_ref, wo_ref, wu_ref,
                     cw_ref, cb_ref, wd_ref, out_ref, wo_scr, h2_scr, act_scr, *, seq_len):
    mod = mod_ref[0]

    @pl.when(pl.program_id(0) == 0)
    def _():
        _prepare_out_weight(wo_ref, wo_scr)

    _out_proj(a_ref, b_ref, x_ref, mod, gpost_ref, gffn_ref, wo_scr, out_ref, h2_scr)
    tc = FF_CHUNK

    def chunk(col0):
        cols = slice(col0, col0 + tc)
        return _conv3(_up(h2_scr, wu_ref[:, cols]), cw_ref[:, cols], cb_ref[:, cols], seq_len)

    for c0 in range(0, D_FF, tc):
        act_scr[:, c0:c0 + tc] = _gated_silu(chunk(c0), chunk(D_FF + c0))
    _down_proj(act_scr, wd_ref, mod, gpost2_ref, out_ref, out_ref)


def _tail_seq_call(a, b, x2d, mod3, mod_row, g_post, g_ffn, g_post2, w_out, w_up, conv_w, conv_b, w_down,
                   seq_len, name):
    t = x2d.shape[0]
    tm = max(SEQ_TAIL_TILE, seq_len)
    big = tm > SEQ_TAIL_TILE
    vmem_limit = TAIL_VMEM_LIMIT if big else VMEM_LIMIT
    x_mode = dict(pipeline_mode=pl.Buffered(1)) if big else {}
    row = lambda i: (i, 0)
    const = lambda i: (0, 0)
    return pl.pallas_call(
        functools.partial(_tail_seq_kernel, seq_len=seq_len),
        grid=(t // tm,),
        in_specs=[
            pl.BlockSpec((tm, W_A), row),
            pl.BlockSpec((tm, W_B), row),
            pl.BlockSpec((tm, D_MODEL), row, **x_mode),
            pl.BlockSpec((1, 1, 6 * D_MODEL), lambda i: (mod_row(i), 0, 0)),
            pl.BlockSpec((1, D_MODEL), const),
            pl.BlockSpec((1, D_MODEL), const),
            pl.BlockSpec((1, D_MODEL), const),
            _resident((W_A + W_B, D_MODEL)),
            _resident((D_MODEL, 2 * D_FF)),
            _resident((3, 2 * D_FF)),
            _resident((1, 2 * D_FF)),
            _resident((D_FF, D_MODEL)),
        ],
        out_specs=pl.BlockSpec((tm, D_MODEL), row),
        out_shape=jax.ShapeDtypeStruct((t, D_MODEL), _F32),
        scratch_shapes=[
            pltpu.VMEM((W_A + W_B, D_MODEL), _BF16),
            pltpu.VMEM((tm, D_MODEL), _BF16),
            pltpu.VMEM((tm, D_FF), _BF16),
        ],
        compiler_params=pltpu.CompilerParams(
            dimension_semantics=("arbitrary",), vmem_limit_bytes=vmem_limit),
        name=name,
    )(a, b, x2d, mod3, g_post.reshape(1, D_MODEL), g_ffn.reshape(1, D_MODEL), g_post2.reshape(1, D_MODEL),
      w_out, w_up, conv_w, conv_b.reshape(1, 2 * D_FF), w_down)


def _rope_tables(seq):
    n = ROPE_HALF
    t = np.arange(seq)
    lane = np.arange(HEAD_DIM)
    pos = np.where(lane[None, :] < HEAD_DIM // 2, (t // GRID_W)[:, None], (t % GRID_W)[:, None])
    inv = 1.0 / (ROPE_THETA ** (np.arange(n, dtype=np.float64) / n))
    ang = pos.astype(np.float64) * inv[lane % n][None, :]
    sign = np.where((lane & n) == 0, -1.0, 1.0)[None, :]
    cos = np.tile(np.cos(ang), (1, LANES // HEAD_DIM)).astype(np.float32)
    sin = np.tile(np.sin(ang) * sign, (1, LANES // HEAD_DIM)).astype(np.float32)
    return jnp.asarray(cos), jnp.asarray(sin)


def _to_head_dim_token(cache_l):
    b, l, h, d = cache_l.shape
    return jnp.transpose(cache_l, (0, 2, 3, 1)).reshape(b, h * d, l)


def _from_head_dim_token(kv_t, heads):
    b, _, l = kv_t.shape
    return jnp.transpose(kv_t.reshape(b, 1, heads, HEAD_DIM, l), (0, 1, 4, 2, 3))


def kernel(x_prompt, x_sample, cache_a_k, cache_a_v, cache_b_k, cache_b_v, c, c_ctx, w_mod, b_mod,
           g_mix_pre, g_mix_post, g_ffn_pre, g_ffn_post, w_in, rpb_a, sink_b, g_grp_a, g_grp_b,
           w_out, w_up, conv_w, conv_b, w_down):
    bp, lp, _ = x_prompt.shape
    bs, ts, _ = x_sample.shape
    assert w_in.shape[0] == 1
    xp = x_prompt.reshape(bp * lp, D_MODEL)
    xs = x_sample.reshape(bs * ts, D_MODEL)
    cond8 = jnp.concatenate([c_ctx[None], c, jnp.zeros((8 - 1 - bs, D_MODEL), _F32)], axis=0)
    cos_tab, sin_tab = _rope_tables(ts)

    l = 0
    mod3 = _mod_call(cond8, w_mod[l], b_mod[l]).reshape(8, 1, 6 * D_MODEL)
    g_b_p = g_grp_b[l].reshape(KV_B, G_B, HEAD_DIM).transpose(1, 0, 2).reshape(W_B)
    w_up_b = w_up[l].astype(_BF16)
    w_down_b = w_down[l].astype(_BF16)
    rpb_pad = jnp.pad(rpb_a[l], ((0, 0), (0, 0), (0, LANES - (2 * NA_COLS - 1))))

    prompt_row = lambda i: 0
    s_tiles = ts // TOKEN_TILE
    sample_row = lambda i: 1 + i // s_tiles
    assert ts >= SEQ_TAIL_TILE
    sample_row_ffn = lambda i: 1 + i

    q_p, kat_p, vat_p, kbt_p, vbt_p = _inproj_call(xp, mod3, prompt_row, g_mix_pre[l], w_in[l], None, lp,
                                                   True, "inproj_prompt")
    a_p, b_p = _ctx_attn_call(sink_b[l], q_p, kat_p, vat_p, kbt_p, vbt_p, g_grp_a[l], g_b_p, lp)
    y_p = _tail_seq_call(a_p, b_p, xp, mod3, prompt_row, g_mix_post[l], g_ffn_pre[l], g_ffn_post[l], w_out[l],
                         w_up_b, conv_w[l], conv_b[l], w_down_b, lp, "tail_prompt")

    q_s, ka_s, va_s, kb_s, vb_s = _inproj_call(xs, mod3, sample_row, g_mix_pre[l], w_in[l],
                                               (cos_tab, sin_tab), ts, False, "inproj_sample")
    a_s = _na_call(q_s, ka_s, va_s, _to_head_dim_token(cache_a_k[:, l]), _to_head_dim_token(cache_a_v[:, l]),
                   rpb_pad, g_grp_a[l], bs, ts)
    b_s = _win_call(sink_b[l], q_s, kb_s, vb_s, _to_head_dim_token(cache_b_k[:, l]),
                    _to_head_dim_token(cache_b_v[:, l]), g_b_p, bs, ts)
    y_s = _tail_seq_call(a_s, b_s, xs, mod3, sample_row_ffn, g_mix_post[l], g_ffn_pre[l], g_ffn_post[l], w_out[l],
                     w_up_b, conv_w[l], conv_b[l], w_down_b, ts, "tail_sample")

    return (y_p.reshape(bp, lp, D_MODEL), y_s.reshape(bs, ts, D_MODEL),
            _from_head_dim_token(kat_p, H_A), _from_head_dim_token(vat_p, H_A),
            _from_head_dim_token(kbt_p, KV_B), _from_head_dim_token(vbt_p, KV_B))
```

```python
import functools

import numpy as np
import jax
import jax.numpy as jnp
from jax import lax
from jax.experimental import pallas as pl
from jax.experimental.pallas import tpu as pltpu

D_MODEL = 1024
HEAD_DIM = 64
H_A = 8
H_B = 8
KV_B = 2
G_B = H_B // KV_B
W_A = H_A * HEAD_DIM
W_B = H_B * HEAD_DIM
KV_W_B = KV_B * HEAD_DIM
IN_WIDTH = 3 * W_A + W_B + 2 * KV_W_B
GRID_W = 64
NA_ROWS = 8
NA_COLS = 16
SWA_WINDOW = 128
BLOCK = 128
D_FF = 2816
ROPE_THETA = 10000.0
EPS = 1e-6
NEG_INF = -1e30
Q_SCALE = HEAD_DIM ** -0.5
ROPE_HALF = HEAD_DIM // 4

LANES = 128
FF_CHUNK = 256
TOKEN_TILE = 512
FFN_ROW_BLOCK = 256
DOWN_ROW_BLOCK = 256
SEQ_TAIL_TILE = 512
PROJ_ROW_BLOCK = 256
MOD_STREAMS = 2
MOD_ROW_BLOCK = 128
NA_QROWS = 2
CAST_COLS = 512
VMEM_LIMIT = 48 * 1024 * 1024
TAIL_VMEM_LIMIT = 56 * 1024 * 1024

C_QA, C_KA, C_VA, C_QB, C_KB, C_VB = 0, W_A, 2 * W_A, 3 * W_A, 3 * W_A + W_B, 3 * W_A + W_B + KV_W_B

_QB_HEAD_ORDER = tuple(kv * G_B + g for g in range(G_B) for kv in range(KV_B))

_BF16 = jnp.bfloat16
_F32 = jnp.float32


def _dot(a, b):
    return jnp.dot(a, b, preferred_element_type=_F32)


def _dot_nt(a, b):
    return lax.dot_general(a, b, (((1,), (1,)), ((), ())), preferred_element_type=_F32)


def _rms(x, g):
    var = jnp.mean(x * x, axis=-1, keepdims=True)
    return x * lax.rsqrt(var + EPS) * g


def _stack_heads(q2, lo):
    zero = jnp.zeros_like(q2)
    return jnp.concatenate([jnp.where(lo, q2, zero), jnp.where(lo, zero, q2)], axis=0)


def _unstack_heads(o2, lo):
    m = o2.shape[0] // 2
    return jnp.where(lo, o2[:m], o2[m:])


def _softmax_pv(scores, pv, sink=None):
    m = scores[0].max(axis=-1, keepdims=True)
    for s in scores[1:]:
        m = jnp.maximum(m, s.max(axis=-1, keepdims=True))
    if sink is not None:
        m = jnp.maximum(m, sink)
    denom = None
    out = None
    for s, f in zip(scores, pv):
        e = jnp.exp(s - m)
        l = e.sum(axis=-1, keepdims=True)
        o = f(e.astype(_BF16))
        denom = l if denom is None else denom + l
        out = o if out is None else out + o
    if sink is not None:
        denom = denom + jnp.exp(sink - m)
    return out / denom


def _resident(shape):
    return pl.BlockSpec(shape, lambda *_: (0,) * len(shape), pipeline_mode=pl.Buffered(1))


def _mod_kernel(c_ref, *refs):
    w_refs, b_ref, o_ref = refs[:-2], refs[-2], refs[-1]
    i = pl.program_id(0)

    @pl.when(i == 0)
    def _():
        o_ref[...] = jnp.broadcast_to(b_ref[...], o_ref.shape)

    acc = o_ref[...]
    for j, w_ref in enumerate(w_refs):
        c = c_ref[i * len(w_refs) + j]
        s = (c * jax.nn.sigmoid(c)).astype(_BF16)
        acc = acc + _dot(s, w_ref[...].astype(_BF16))
    o_ref[...] = acc


def _mod_call(cond8, w_mod, b_mod):
    n = w_mod.shape[1]
    kb = MOD_ROW_BLOCK
    nblk = D_MODEL // kb
    cond_blocks = cond8.reshape(8, nblk, kb).transpose(1, 0, 2)
    return pl.pallas_call(
        _mod_kernel,
        grid=(nblk // MOD_STREAMS,),
        in_specs=[pl.BlockSpec((nblk, 8, kb), lambda i: (0, 0, 0))]
        + [pl.BlockSpec((kb, n), lambda i, j=j: (i * MOD_STREAMS + j, 0)) for j in range(MOD_STREAMS)]
        + [pl.BlockSpec((1, n), lambda i: (0, 0))],
        out_specs=pl.BlockSpec((8, n), lambda i: (0, 0)),
        out_shape=jax.ShapeDtypeStruct((8, n), _F32),
        compiler_params=pltpu.CompilerParams(dimension_semantics=("arbitrary",), vmem_limit_bytes=VMEM_LIMIT),
        name="mod",
    )(cond_blocks, *([w_mod] * MOD_STREAMS), b_mod.reshape(1, n))


def _prepare_in_weight(w_ref, w_scr):
    for c0 in list(range(0, C_QB, CAST_COLS)) + [C_KB]:
        n = min(CAST_COLS, IN_WIDTH - c0)
        w_scr[:, c0:c0 + n] = w_ref[:, c0:c0 + n].astype(_BF16)
    lo = lax.broadcasted_iota(jnp.int32, (D_MODEL, LANES), 1) < HEAD_DIM
    for j in range(G_B):
        h_lo, h_hi = _QB_HEAD_ORDER[2 * j], _QB_HEAD_ORDER[2 * j + 1]

        def head_at(h, want_hi):
            blk = w_ref[:, C_QB + LANES * (h // 2):C_QB + LANES * (h // 2 + 1)]
            return blk if (h % 2 == 1) == want_hi else pltpu.roll(blk, HEAD_DIM, 1)

        w_scr[:, C_QB + LANES * j:C_QB + LANES * (j + 1)] = jnp.where(
            lo, head_at(h_lo, False), head_at(h_hi, True)).astype(_BF16)


def _rope(z, cos, sin):
    lane = lax.broadcasted_iota(jnp.int32, z.shape, 1)
    partner = jnp.where((lane & ROPE_HALF) == 0,
                        pltpu.roll(z, LANES - ROPE_HALF, 1), pltpu.roll(z, ROPE_HALF, 1))
    return z * cos + partner * sin


def _inproj_kernel(*refs, rope, kv_transposed):
    if rope:
        x_ref, mod_ref, g_ref, w_ref, cos_ref, sin_ref, q_ref, ka_ref, va_ref, kb_ref, vb_ref, w_scr = refs
    else:
        x_ref, mod_ref, g_ref, w_ref, q_ref, ka_ref, va_ref, kb_ref, vb_ref, w_scr = refs

    @pl.when(pl.program_id(0) == 0)
    def _():
        _prepare_in_weight(w_ref, w_scr)

    mod = mod_ref[0]
    sh1 = mod[:, 0:D_MODEL]
    sc1 = mod[:, D_MODEL:2 * D_MODEL]
    tm = x_ref.shape[0]
    rb = PROJ_ROW_BLOCK

    for r in range(0, tm, rb):
        rows = slice(r, r + rb)
        h = (_rms(x_ref[rows, :], g_ref[...]) * (1 + sc1) + sh1).astype(_BF16)

        def proj(c0, n):
            return _dot(h, w_scr[:, c0:c0 + n])

        def put_kv(ref, z):
            if not kv_transposed:
                ref[rows, :] = z.astype(ref.dtype)
                return
            l = ref.shape[2]
            for t0 in range(0, rb, l):
                for c0 in range(0, z.shape[1], LANES):
                    ref[(r + t0) // l, c0:c0 + LANES, :] = z[t0:t0 + l, c0:c0 + LANES].T.astype(ref.dtype)

        q_ref[rows, 0:W_A] = (proj(C_QA, W_A) * Q_SCALE).astype(q_ref.dtype)
        put_kv(ka_ref, proj(C_KA, W_A))
        put_kv(va_ref, proj(C_VA, W_A))
        zkv = proj(C_KB, 2 * KV_W_B)
        zkb, zvb = zkv[:, :KV_W_B], zkv[:, KV_W_B:]
        zqb = proj(C_QB, W_B)
        put_kv(vb_ref, zvb)
        if rope:
            cos = cos_ref[rows, :]
            sin = sin_ref[rows, :]
            for j in range(W_B // LANES):
                q_ref[rows, W_A + LANES * j:W_A + LANES * (j + 1)] = (
                    _rope(zqb[:, LANES * j:LANES * (j + 1)], cos, sin) * Q_SCALE).astype(q_ref.dtype)
            put_kv(kb_ref, _rope(zkb, cos, sin))
        else:
            q_ref[rows, W_A:W_A + W_B] = (zqb * Q_SCALE).astype(q_ref.dtype)
            put_kv(kb_ref, zkb)


def _inproj_call(x2d, mod3, mod_row, g_pre, w_in, rope_tabs, seq, kv_transposed, name):
    t = x2d.shape[0]
    tm = TOKEN_TILE
    rope = rope_tabs is not None
    in_specs = [
        pl.BlockSpec((tm, D_MODEL), lambda i: (i, 0)),
        pl.BlockSpec((1, 1, 6 * D_MODEL), lambda i: (mod_row(i), 0, 0)),
        pl.BlockSpec((1, D_MODEL), lambda i: (0, 0)),
        _resident((D_MODEL, IN_WIDTH)),
    ]
    args = [x2d, mod3, g_pre.reshape(1, D_MODEL), w_in]
    if rope:
        tiles_per_seq = rope_tabs[0].shape[0] // tm
        in_specs += [pl.BlockSpec((tm, LANES), lambda i: (i % tiles_per_seq, 0))] * 2
        args += list(rope_tabs)
    if kv_transposed:
        nseq = tm // seq
        kv_shape = lambda w: jax.ShapeDtypeStruct((t // seq, w, seq), _F32)
        kv_spec = lambda w: pl.BlockSpec((nseq, w, seq), lambda i: (i, 0, 0))
    else:
        kv_shape = lambda w: jax.ShapeDtypeStruct((t, w), _BF16)
        kv_spec = lambda w: pl.BlockSpec((tm, w), lambda i: (i, 0))
    out_shape = (jax.ShapeDtypeStruct((t, W_A + W_B), _BF16),
                 kv_shape(W_A), kv_shape(W_A), kv_shape(KV_W_B), kv_shape(KV_W_B))
    out_specs = (pl.BlockSpec((tm, W_A + W_B), lambda i: (i, 0)),
                 kv_spec(W_A), kv_spec(W_A), kv_spec(KV_W_B), kv_spec(KV_W_B))
    return pl.pallas_call(
        functools.partial(_inproj_kernel, rope=rope, kv_transposed=kv_transposed),
        grid=(t // tm,),
        in_specs=in_specs,
        out_specs=out_specs,
        out_shape=out_shape,
        scratch_shapes=[pltpu.VMEM((D_MODEL, IN_WIDTH), _BF16)],
        compiler_params=pltpu.CompilerParams(
            dimension_semantics=("arbitrary",), vmem_limit_bytes=VMEM_LIMIT),
        name=name,
    )(*args)


def _ctx_attn_kernel(sink_ref, q_ref, ka_ref, va_ref, kb_ref, vb_ref, ga_ref, gb_ref, wu_ref, wd_ref,
                     a_ref, b_ref, wu_out, wd_out):
    wu_out[...] = wu_ref[...].astype(wu_out.dtype)
    wd_out[...] = wd_ref[...].astype(wd_out.dtype)
    l = q_ref.shape[0]
    lo = lax.broadcasted_iota(jnp.int32, (l, LANES), 1) < HEAD_DIM
    row_lo = lax.broadcasted_iota(jnp.int32, (2 * l, 1), 0) < l
    oa = []
    for p in range(W_A // LANES):
        sl = slice(LANES * p, LANES * (p + 1))
        qs = _stack_heads(q_ref[:, sl], lo)
        s = _dot(qs, ka_ref[0, sl, :].astype(_BF16))
        vt = va_ref[0, sl, :].astype(_BF16)
        oa.append(_unstack_heads(_softmax_pv([s], [lambda e, vt=vt: _dot_nt(e, vt)]), lo))
    a_ref[...] = _rms(jnp.concatenate(oa, axis=1), ga_ref[...]).astype(a_ref.dtype)
    kb = kb_ref[0].astype(_BF16)
    vb = vb_ref[0].astype(_BF16)
    ob = []
    for j in range(G_B):
        qs = _stack_heads(q_ref[:, W_A + LANES * j:W_A + LANES * (j + 1)], lo)
        s = _dot(qs, kb)
        sink = jnp.where(row_lo, sink_ref[0, j], sink_ref[1, j])
        ob.append(_unstack_heads(_softmax_pv([s], [lambda e: _dot_nt(e, vb)], sink), lo))
    b_ref[...] = _rms(jnp.concatenate(ob, axis=1), gb_ref[...]).astype(b_ref.dtype)


def _ctx_attn_call(sink, q, ka, va, kb, vb, g_a, g_b, w_up, w_down, seq):
    t = q.shape[0]
    steps = t // seq
    ru, rd = w_up.shape[0] // steps, w_down.shape[0] // steps
    assert ru * steps == w_up.shape[0] and rd * steps == w_down.shape[0] and ru % 16 == 0 and rd % 16 == 0
    row = lambda b: (b, 0)
    seq3 = lambda b: (b, 0, 0)
    const = lambda b: (0, 0)
    return pl.pallas_call(
        _ctx_attn_kernel,
        grid=(steps,),
        in_specs=[
            pl.BlockSpec(memory_space=pltpu.SMEM),
            pl.BlockSpec((seq, W_A + W_B), row),
            pl.BlockSpec((1, W_A, seq), seq3),
            pl.BlockSpec((1, W_A, seq), seq3),
            pl.BlockSpec((1, KV_W_B, seq), seq3),
            pl.BlockSpec((1, KV_W_B, seq), seq3),
            pl.BlockSpec((1, W_A), const),
            pl.BlockSpec((1, W_B), const),
            pl.BlockSpec((ru, w_up.shape[1]), row),
            pl.BlockSpec((rd, w_down.shape[1]), row),
        ],
        out_specs=(pl.BlockSpec((seq, W_A), row), pl.BlockSpec((seq, W_B), row),
                   pl.BlockSpec((ru, w_up.shape[1]), row), pl.BlockSpec((rd, w_down.shape[1]), row)),
        out_shape=(jax.ShapeDtypeStruct((t, W_A), _BF16), jax.ShapeDtypeStruct((t, W_B), _BF16),
                   jax.ShapeDtypeStruct(w_up.shape, _BF16), jax.ShapeDtypeStruct(w_down.shape, _BF16)),
        compiler_params=pltpu.CompilerParams(
            dimension_semantics=("parallel",), vmem_limit_bytes=VMEM_LIMIT),
        name="ctx_attn",
    )(sink, q, ka, va, kb, vb, g_a.reshape(1, W_A), g_b.reshape(1, W_B), w_up, w_down)


def _na_plan(rows):
    kr = min(NA_ROWS, rows)
    win = kr + NA_QROWS
    row_start = lambda r: min(max(r - kr // 2, 0), rows - kr)
    blocks, classes = [], []
    for r0 in range(0, rows, NA_QROWS):
        ws = min(row_start(r0), rows - win)
        keys = []
        for r in range(r0, r0 + NA_QROWS):
            key = (ws - r + NA_ROWS - 1, row_start(r) - ws)
            if key not in classes:
                classes.append(key)
            keys.append(classes.index(key))
        blocks.append((r0, ws, keys))
    return kr, win, blocks, classes


def _na_kernel(q_ref, k_ref, v_ref, ck_ref, cv_ref, rpb_ref, ga_ref, a_ref, bias_ref, o_scr):
    rows = q_ref.shape[0] // GRID_W
    kr, win, blocks, classes = _na_plan(rows)
    m = NA_QROWS * GRID_W
    lane = lax.broadcasted_iota(jnp.int32, (GRID_W, LANES), 1)
    lo = lane < HEAD_DIM
    lo_q = lax.broadcasted_iota(jnp.int32, (m, LANES), 1) < HEAD_DIM
    cq = lax.broadcasted_iota(jnp.int32, (GRID_W, LANES), 0)
    ck = lane & (GRID_W - 1)
    cs = jnp.clip(cq - NA_COLS // 2, 0, GRID_W - NA_COLS)
    valid = (ck >= cs) & (ck < cs + NA_COLS)
    neg = jnp.full((GRID_W, LANES), NEG_INF, _F32)
    n_drow = 2 * NA_ROWS - 1

    for p in range(W_A // LANES):
        sl = slice(LANES * p, LANES * (p + 1))
        for hh in range(2):
            t_lo, t_hi = [], []
            for d in range(n_drow):
                base = jnp.broadcast_to(rpb_ref[2 * p + hh, d:d + 1, :], (GRID_W, LANES))
                t_lo.append(pltpu.roll(base, LANES - (NA_COLS - 1), 1, stride=1, stride_axis=0))
                t_hi.append(pltpu.roll(base, GRID_W - (NA_COLS - 1), 1, stride=1, stride_axis=0))
            tiles = {}
            for ci, (d_first, off) in enumerate(classes):
                for j in range(win // 2):
                    d_of = lambda i: d_first + i if off <= i < off + kr else None
                    key = (d_of(2 * j), d_of(2 * j + 1))
                    if key not in tiles:
                        left = neg if key[0] is None else t_lo[key[0]]
                        right = neg if key[1] is None else t_hi[key[1]]
                        tiles[key] = jnp.where(valid, jnp.where(lo, left, right), NEG_INF)
                    bias_ref[ci, GRID_W * hh:GRID_W * (hh + 1), LANES * j:LANES * (j + 1)] = tiles[key]

        ckt = ck_ref[0, sl, :].astype(_BF16)
        cvt = cv_ref[0, sl, :].astype(_BF16)

        for r0, ws, keys in blocks:
            q0, k0 = r0 * GRID_W, ws * GRID_W
            qs = _stack_heads(q_ref[q0:q0 + m, sl], lo_q)
            kw = k_ref[k0:k0 + win * GRID_W, sl]
            vw = v_ref[k0:k0 + win * GRID_W, sl]
            bias = jnp.concatenate(
                [bias_ref[ci, GRID_W * hh:GRID_W * (hh + 1), :] for hh in range(2) for ci in keys], axis=0)
            s_nb = _dot_nt(qs, kw) + bias
            s_cx = _dot(qs, ckt)
            o2 = _softmax_pv([s_nb, s_cx], [lambda e, vw=vw: _dot(e, vw), lambda e: _dot_nt(e, cvt)])
            o_scr[q0:q0 + m, sl] = _unstack_heads(o2, lo_q)

    a_ref[...] = _rms(o_scr[...], ga_ref[...]).astype(a_ref.dtype)


def _na_call(q, k, v, ckt, cvt, rpb_pad, g_a, nb, seq):
    past = ckt.shape[2]
    _, win, _, classes = _na_plan(seq // GRID_W)
    blk = lambda b: (b, 0)
    return pl.pallas_call(
        _na_kernel,
        grid=(nb,),
        in_specs=[
            pl.BlockSpec((seq, W_A), blk),
            pl.BlockSpec((seq, W_A), blk),
            pl.BlockSpec((seq, W_A), blk),
            pl.BlockSpec((1, W_A, past), lambda b: (b, 0, 0)),
            pl.BlockSpec((1, W_A, past), lambda b: (b, 0, 0)),
            pl.BlockSpec((H_A, 2 * NA_ROWS - 1, LANES), lambda b: (0, 0, 0)),
            pl.BlockSpec((1, W_A), lambda b: (0, 0)),
        ],
        out_specs=pl.BlockSpec((seq, W_A), blk),
        out_shape=jax.ShapeDtypeStruct((nb * seq, W_A), _BF16),
        scratch_shapes=[pltpu.VMEM((len(classes), 2 * GRID_W, win * GRID_W), _F32),
                        pltpu.VMEM((seq, W_A), _F32)],
        compiler_params=pltpu.CompilerParams(
            dimension_semantics=("parallel",), vmem_limit_bytes=VMEM_LIMIT),
        name="na_attn",
    )(q, k, v, ckt, cvt, rpb_pad, g_a.reshape(1, W_A))


def _win_kernel(sink_ref, q_ref, k_ref, v_ref, ck_ref, cv_ref, gb_ref, b_ref, o_scr):
    t = q_ref.shape[0]
    nblk = t // BLOCK
    band = 3 * BLOCK
    lo = lax.broadcasted_iota(jnp.int32, (BLOCK, LANES), 1) < HEAD_DIM
    row_lo = lax.broadcasted_iota(jnp.int32, (2 * BLOCK, 1), 0) < BLOCK
    qi = lax.broadcasted_iota(jnp.int32, (2 * BLOCK, band), 0) & (BLOCK - 1)
    kj = lax.broadcasted_iota(jnp.int32, (2 * BLOCK, band), 1)
    rel = qi - kj
    ckt = ck_ref[0].astype(_BF16)
    cvt = cv_ref[0].astype(_BF16)

    starts = [min(max((n - 1) * BLOCK, 0), t - band) for n in range(nblk)]
    masks = {off: jnp.where(jnp.abs(rel + off) <= SWA_WINDOW, 0.0, NEG_INF)
             for off in sorted({n * BLOCK - starts[n] for n in range(nblk)})}

    for j in range(G_B):
        sl = slice(LANES * j, LANES * (j + 1))
        sink = jnp.where(row_lo, sink_ref[0, j], sink_ref[1, j])
        for n in range(nblk):
            q0, k0 = n * BLOCK, starts[n]
            qs = _stack_heads(q_ref[q0:q0 + BLOCK, sl], lo)
            s = _dot_nt(qs, k_ref[k0:k0 + band, :]) + masks[q0 - k0]
            s_cx = _dot(qs, ckt)
            vband = v_ref[k0:k0 + band, :]
            o2 = _softmax_pv([s, s_cx], [lambda e, vband=vband: _dot(e, vband), lambda e: _dot_nt(e, cvt)],
                             sink)
            o_scr[q0:q0 + BLOCK, sl] = _unstack_heads(o2, lo)

    b_ref[...] = _rms(o_scr[...], gb_ref[...]).astype(b_ref.dtype)


def _win_call(sink, q, k, v, ckt, cvt, g_b, nb, seq):
    past = ckt.shape[2]
    return pl.pallas_call(
        _win_kernel,
        grid=(nb,),
        in_specs=[
            pl.BlockSpec(memory_space=pltpu.SMEM),
            pl.BlockSpec((seq, W_B), lambda b: (b, W_A // W_B)),
            pl.BlockSpec((seq, KV_W_B), lambda b: (b, 0)),
            pl.BlockSpec((seq, KV_W_B), lambda b: (b, 0)),
            pl.BlockSpec((1, KV_W_B, past), lambda b: (b, 0, 0)),
            pl.BlockSpec((1, KV_W_B, past), lambda b: (b, 0, 0)),
            pl.BlockSpec((1, W_B), lambda b: (0, 0)),
        ],
        out_specs=pl.BlockSpec((seq, W_B), lambda b: (b, 0)),
        out_shape=jax.ShapeDtypeStruct((nb * seq, W_B), _BF16),
        scratch_shapes=[pltpu.VMEM((seq, W_B), _F32)],
        compiler_params=pltpu.CompilerParams(
            dimension_semantics=("parallel",), vmem_limit_bytes=VMEM_LIMIT),
        name="win_attn",
    )(sink, q, k, v, ckt, cvt, g_b.reshape(1, W_B))


def _prepare_out_weight(w_ref, w_scr):
    for r0 in range(0, W_A, CAST_COLS):
        w_scr[r0:r0 + CAST_COLS, :] = w_ref[r0:r0 + CAST_COLS, :].astype(_BF16)
    for i, h in enumerate(_QB_HEAD_ORDER):
        w_scr[W_A + HEAD_DIM * i:W_A + HEAD_DIM * (i + 1), :] = (
            w_ref[W_A + HEAD_DIM * h:W_A + HEAD_DIM * (h + 1), :].astype(_BF16))


def _zero_row(x, row):
    r0 = row - row % 8
    sub = lax.broadcasted_iota(jnp.int32, (8, 1), 0)
    slab = jnp.where(sub == row % 8, 0.0, x[r0:r0 + 8])
    parts = ([x[:r0]] if r0 else []) + [slab] + ([x[r0 + 8:]] if r0 + 8 < x.shape[0] else [])
    return jnp.concatenate(parts, axis=0)


def _conv3(u, cw, cb, seq_len):
    tm = u.shape[0]
    prev = pltpu.roll(u, 1, 0)
    nxt = pltpu.roll(u, tm - 1, 0)
    for s0 in range(0, tm, seq_len):
        prev = _zero_row(prev, s0)
        nxt = _zero_row(nxt, s0 + seq_len - 1)
    return prev * cw[0:1, :] + u * cw[1:2, :] + nxt * cw[2:3, :] + cb


def _gated_silu(gate, val):
    half = 0.5 * gate
    return ((half + half * jnp.tanh(half)) * val).astype(_BF16)


def _up(h2_ref, w):
    tm = h2_ref.shape[0]
    return jnp.concatenate(
        [_dot(h2_ref[r:r + FFN_ROW_BLOCK, :], w) for r in range(0, tm, FFN_ROW_BLOCK)], axis=0)


def _out_proj(a_ref, b_ref, x_ref, mod, gpost_ref, gffn_ref, wo_scr, x1_ref, h2_ref):
    gt1 = mod[:, 2 * D_MODEL:3 * D_MODEL]
    sh2 = mod[:, 3 * D_MODEL:4 * D_MODEL]
    sc2 = mod[:, 4 * D_MODEL:5 * D_MODEL]
    for r in range(0, x_ref.shape[0], FFN_ROW_BLOCK):
        rows = slice(r, r + FFN_ROW_BLOCK)
        y = _dot(a_ref[rows, :], wo_scr[0:W_A, :]) + _dot(b_ref[rows, :], wo_scr[W_A:W_A + W_B, :])
        x1 = x_ref[rows, :] + gt1 * _rms(y, gpost_ref[...])
        x1_ref[rows, :] = x1
        h2_ref[rows, :] = (_rms(x1, gffn_ref[...]) * (1 + sc2) + sh2).astype(h2_ref.dtype)


def _down_proj(act_ref, wd_ref, mod, gpost2_ref, x1_ref, out_ref):
    gt2 = mod[:, 5 * D_MODEL:6 * D_MODEL]
    for r in range(0, act_ref.shape[0], DOWN_ROW_BLOCK):
        rows = slice(r, r + DOWN_ROW_BLOCK)
        y = _dot(act_ref[rows, :], wd_ref[...])
        out_ref[rows, :] = x1_ref[rows, :] + gt2 * _rms(y, gpost2_ref[...])


def _tail_seq_kernel(a_ref, b_ref, x_ref, mod_ref, gpost_ref, gffn_ref, gpost2_ref, wo_ref, wu_ref,
                     cw_ref, cb_ref, wd_ref, out_ref, wo_scr, h2_scr, act_scr, *, seq_len):
    mod = mod_ref[0]

    @pl.when(pl.program_id(0) == 0)
    def _():
        _prepare_out_weight(wo_ref, wo_scr)

    _out_proj(a_ref, b_ref, x_ref, mod, gpost_ref, gffn_ref, wo_scr, out_ref, h2_scr)
    tc = FF_CHUNK

    def chunk(col0):
        cols = slice(col0, col0 + tc)
        return _conv3(_up(h2_scr, wu_ref[:, cols]), cw_ref[:, cols], cb_ref[:, cols], seq_len)

    for c0 in range(0, D_FF, tc):
        act_scr[:, c0:c0 + tc] = _gated_silu(chunk(c0), chunk(D_FF + c0))
    _down_proj(act_scr, wd_ref, mod, gpost2_ref, out_ref, out_ref)


def _tail_seq_call(a, b, x2d, mod3, mod_row, g_post, g_ffn, g_post2, w_out, w_up, conv_w, conv_b, w_down,
                   seq_len, name):
    t = x2d.shape[0]
    tm = max(SEQ_TAIL_TILE, seq_len)
    big = tm > SEQ_TAIL_TILE
    vmem_limit = TAIL_VMEM_LIMIT if big else VMEM_LIMIT
    x_mode = dict(pipeline_mode=pl.Buffered(1)) if big else {}
    row = lambda i: (i, 0)
    const = lambda i: (0, 0)
    return pl.pallas_call(
        functools.partial(_tail_seq_kernel, seq_len=seq_len),
        grid=(t // tm,),
        in_specs=[
            pl.BlockSpec((tm, W_A), row),
            pl.BlockSpec((tm, W_B), row),
            pl.BlockSpec((tm, D_MODEL), row, **x_mode),
            pl.BlockSpec((1, 1, 6 * D_MODEL), lambda i: (mod_row(i), 0, 0)),
            pl.BlockSpec((1, D_MODEL), const),
            pl.BlockSpec((1, D_MODEL), const),
            pl.BlockSpec((1, D_MODEL), const),
            _resident((W_A + W_B, D_MODEL)),
            _resident((D_MODEL, 2 * D_FF)),
            _resident((3, 2 * D_FF)),
            _resident((1, 2 * D_FF)),
            _resident((D_FF, D_MODEL)),
        ],
        out_specs=pl.BlockSpec((tm, D_MODEL), row),
        out_shape=jax.ShapeDtypeStruct((t, D_MODEL), _F32),
        scratch_shapes=[
            pltpu.VMEM((W_A + W_B, D_MODEL), _BF16),
            pltpu.VMEM((tm, D_MODEL), _BF16),
            pltpu.VMEM((tm, D_FF), _BF16),
        ],
        compiler_params=pltpu.CompilerParams(
            dimension_semantics=("arbitrary",), vmem_limit_bytes=vmem_limit),
        name=name,
    )(a, b, x2d, mod3, g_post.reshape(1, D_MODEL), g_ffn.reshape(1, D_MODEL), g_post2.reshape(1, D_MODEL),
      w_out, w_up, conv_w, conv_b.reshape(1, 2 * D_FF), w_down)


def _rope_tables(seq):
    n = ROPE_HALF
    t = np.arange(seq)
    lane = np.arange(HEAD_DIM)
    pos = np.where(lane[None, :] < HEAD_DIM // 2, (t // GRID_W)[:, None], (t % GRID_W)[:, None])
    inv = 1.0 / (ROPE_THETA ** (np.arange(n, dtype=np.float64) / n))
    ang = pos.astype(np.float64) * inv[lane % n][None, :]
    sign = np.where((lane & n) == 0, -1.0, 1.0)[None, :]
    cos = np.tile(np.cos(ang), (1, LANES // HEAD_DIM)).astype(np.float32)
    sin = np.tile(np.sin(ang) * sign, (1, LANES // HEAD_DIM)).astype(np.float32)
    return jnp.asarray(cos), jnp.asarray(sin)


def _to_head_dim_token(cache_l):
    b, l, h, d = cache_l.shape
    return jnp.transpose(cache_l, (0, 2, 3, 1)).reshape(b, h * d, l)


def _from_head_dim_token(kv_t, heads):
    b, _, l = kv_t.shape
    return jnp.transpose(kv_t.reshape(b, 1, heads, HEAD_DIM, l), (0, 1, 4, 2, 3))


def kernel(x_prompt, x_sample, cache_a_k, cache_a_v, cache_b_k, cache_b_v, c, c_ctx, w_mod, b_mod,
           g_mix_pre, g_mix_post, g_ffn_pre, g_ffn_post, w_in, rpb_a, sink_b, g_grp_a, g_grp_b,
           w_out, w_up, conv_w, conv_b, w_down):
    bp, lp, _ = x_prompt.shape
    bs, ts, _ = x_sample.shape
    assert w_in.shape[0] == 1
    xp = x_prompt.reshape(bp * lp, D_MODEL)
    xs = x_sample.reshape(bs * ts, D_MODEL)
    cond8 = jnp.concatenate([c_ctx[None], c, jnp.zeros((8 - 1 - bs, D_MODEL), _F32)], axis=0)
    cos_tab, sin_tab = _rope_tables(ts)

    l = 0
    mod3 = _mod_call(cond8, w_mod[l], b_mod[l]).reshape(8, 1, 6 * D_MODEL)
    g_b_p = g_grp_b[l].reshape(KV_B, G_B, HEAD_DIM).transpose(1, 0, 2).reshape(W_B)
    rpb_pad = jnp.pad(rpb_a[l], ((0, 0), (0, 0), (0, LANES - (2 * NA_COLS - 1))))

    prompt_row = lambda i: 0
    s_tiles = ts // TOKEN_TILE
    sample_row = lambda i: 1 + i // s_tiles
    assert ts >= SEQ_TAIL_TILE
    sample_row_ffn = lambda i: 1 + i

    q_p, kat_p, vat_p, kbt_p, vbt_p = _inproj_call(xp, mod3, prompt_row, g_mix_pre[l], w_in[l], None, lp,
                                                   True, "inproj_prompt")
    a_p, b_p, w_up_b, w_down_b = _ctx_attn_call(sink_b[l], q_p, kat_p, vat_p, kbt_p, vbt_p, g_grp_a[l], g_b_p,
                                                w_up[l], w_down[l], lp)
    y_p = _tail_seq_call(a_p, b_p, xp, mod3, prompt_row, g_mix_post[l], g_ffn_pre[l], g_ffn_post[l], w_out[l],
                         w_up_b, conv_w[l], conv_b[l], w_down_b, lp, "tail_prompt")

    q_s, ka_s, va_s, kb_s, vb_s = _inproj_call(xs, mod3, sample_row, g_mix_pre[l], w_in[l],
                                               (cos_tab, sin_tab), ts, False, "inproj_sample")
    a_s = _na_call(q_s, ka_s, va_s, _to_head_dim_token(cache_a_k[:, l]), _to_head_dim_token(cache_a_v[:, l]),
                   rpb_pad, g_grp_a[l], bs, ts)
    b_s = _win_call(sink_b[l], q_s, kb_s, vb_s, _to_head_dim_token(cache_b_k[:, l]),
                    _to_head_dim_token(cache_b_v[:, l]), g_b_p, bs, ts)
    y_s = _tail_seq_call(a_s, b_s, xs, mod3, sample_row_ffn, g_mix_post[l], g_ffn_pre[l], g_ffn_post[l], w_out[l],
                     w_up_b, conv_w[l], conv_b[l], w_down_b, ts, "tail_sample")

    return (y_p.reshape(bp, lp, D_MODEL), y_s.reshape(bs, ts, D_MODEL),
            _from_head_dim_token(kat_p, H_A), _from_head_dim_token(vat_p, H_A),
            _from_head_dim_token(kbt_p, KV_B), _from_head_dim_token(vbt_p, KV_B))
```

```python
import functools

import numpy as np
import jax
import jax.numpy as jnp
from jax import lax
from jax.experimental import pallas as pl
from jax.experimental.pallas import tpu as pltpu

D_MODEL = 1024
HEAD_DIM = 64
H_A = 8
H_B = 8
KV_B = 2
G_B = H_B // KV_B
W_A = H_A * HEAD_DIM
W_B = H_B * HEAD_DIM
KV_W_B = KV_B * HEAD_DIM
IN_WIDTH = 3 * W_A + W_B + 2 * KV_W_B
GRID_W = 64
NA_ROWS = 8
NA_COLS = 16
SWA_WINDOW = 128
BLOCK = 128
D_FF = 2816
ROPE_THETA = 10000.0
EPS = 1e-6
NEG_INF = -1e30
Q_SCALE = HEAD_DIM ** -0.5
ROPE_HALF = HEAD_DIM // 4

LANES = 128
FF_CHUNK = 256
TOKEN_TILE = 512
FFN_ROW_BLOCK = 256
DOWN_ROW_BLOCK = 256
SEQ_TAIL_TILE = 512
PROJ_ROW_BLOCK = 256
MOD_STREAMS = 2
MOD_ROW_BLOCK = 128
NA_QROWS = 2
CAST_COLS = 512
VMEM_LIMIT = 48 * 1024 * 1024
TAIL_VMEM_LIMIT = 56 * 1024 * 1024

C_QA, C_KA, C_VA, C_QB, C_KB, C_VB = 0, W_A, 2 * W_A, 3 * W_A, 3 * W_A + W_B, 3 * W_A + W_B + KV_W_B

_QB_HEAD_ORDER = tuple(kv * G_B + g for g in range(G_B) for kv in range(KV_B))

_BF16 = jnp.bfloat16
_F32 = jnp.float32


def _dot(a, b):
    return jnp.dot(a, b, preferred_element_type=_F32)


def _dot_nt(a, b):
    return lax.dot_general(a, b, (((1,), (1,)), ((), ())), preferred_element_type=_F32)


def _rms(x, g):
    var = jnp.mean(x * x, axis=-1, keepdims=True)
    return x * lax.rsqrt(var + EPS) * g


def _stack_heads(q2, lo):
    zero = jnp.zeros_like(q2)
    return jnp.concatenate([jnp.where(lo, q2, zero), jnp.where(lo, zero, q2)], axis=0)


def _unstack_heads(o2, lo):
    m = o2.shape[0] // 2
    return jnp.where(lo, o2[:m], o2[m:])


def _softmax_pv(scores, pv, sink=None):
    m = scores[0].max(axis=-1, keepdims=True)
    for s in scores[1:]:
        m = jnp.maximum(m, s.max(axis=-1, keepdims=True))
    if sink is not None:
        m = jnp.maximum(m, sink)
    denom = None
    out = None
    for s, f in zip(scores, pv):
        e = jnp.exp(s - m)
        l = e.sum(axis=-1, keepdims=True)
        o = f(e.astype(_BF16))
        denom = l if denom is None else denom + l
        out = o if out is None else out + o
    if sink is not None:
        denom = denom + jnp.exp(sink - m)
    return out / denom


def _resident(shape):
    return pl.BlockSpec(shape, lambda *_: (0,) * len(shape), pipeline_mode=pl.Buffered(1))


def _mod_kernel(c_ref, *refs):
    w_refs, b_ref, o_ref = refs[:-2], refs[-2], refs[-1]
    i = pl.program_id(0)

    @pl.when(i == 0)
    def _():
        o_ref[...] = jnp.broadcast_to(b_ref[...], o_ref.shape)

    acc = o_ref[...]
    for j, w_ref in enumerate(w_refs):
        c = c_ref[i * len(w_refs) + j]
        s = (c * jax.nn.sigmoid(c)).astype(_BF16)
        acc = acc + _dot(s, w_ref[...].astype(_BF16))
    o_ref[...] = acc


def _mod_call(cond8, w_mod, b_mod):
    n = w_mod.shape[1]
    kb = MOD_ROW_BLOCK
    nblk = D_MODEL // kb
    cond_blocks = cond8.reshape(8, nblk, kb).transpose(1, 0, 2)
    return pl.pallas_call(
        _mod_kernel,
        grid=(nblk // MOD_STREAMS,),
        in_specs=[pl.BlockSpec((nblk, 8, kb), lambda i: (0, 0, 0))]
        + [pl.BlockSpec((kb, n), lambda i, j=j: (i * MOD_STREAMS + j, 0)) for j in range(MOD_STREAMS)]
        + [pl.BlockSpec((1, n), lambda i: (0, 0))],
        out_specs=pl.BlockSpec((8, n), lambda i: (0, 0)),
        out_shape=jax.ShapeDtypeStruct((8, n), _F32),
        compiler_params=pltpu.CompilerParams(dimension_semantics=("arbitrary",), vmem_limit_bytes=VMEM_LIMIT),
        name="mod",
    )(cond_blocks, *([w_mod] * MOD_STREAMS), b_mod.reshape(1, n))


def _prepare_in_weight(w_ref, w_scr):
    for c0 in list(range(0, C_QB, CAST_COLS)) + [C_KB]:
        n = min(CAST_COLS, IN_WIDTH - c0)
        w_scr[:, c0:c0 + n] = w_ref[:, c0:c0 + n].astype(_BF16)
    lo = lax.broadcasted_iota(jnp.int32, (D_MODEL, LANES), 1) < HEAD_DIM
    for j in range(G_B):
        h_lo, h_hi = _QB_HEAD_ORDER[2 * j], _QB_HEAD_ORDER[2 * j + 1]

        def head_at(h, want_hi):
            blk = w_ref[:, C_QB + LANES * (h // 2):C_QB + LANES * (h // 2 + 1)]
            return blk if (h % 2 == 1) == want_hi else pltpu.roll(blk, HEAD_DIM, 1)

        w_scr[:, C_QB + LANES * j:C_QB + LANES * (j + 1)] = jnp.where(
            lo, head_at(h_lo, False), head_at(h_hi, True)).astype(_BF16)


def _rope(z, cos, sin):
    lane = lax.broadcasted_iota(jnp.int32, z.shape, 1)
    partner = jnp.where((lane & ROPE_HALF) == 0,
                        pltpu.roll(z, LANES - ROPE_HALF, 1), pltpu.roll(z, ROPE_HALF, 1))
    return z * cos + partner * sin


def _inproj_rows(x_ref, mod, g_ref, w_scr, rope_refs, outs):
    q_ref, ka_ref, va_ref, kb_ref, vb_ref = outs
    kv_transposed = rope_refs is None
    sh1 = mod[:, 0:D_MODEL]
    sc1 = mod[:, D_MODEL:2 * D_MODEL]
    rb = PROJ_ROW_BLOCK

    for r in range(0, x_ref.shape[0], rb):
        rows = slice(r, r + rb)
        h = (_rms(x_ref[rows, :], g_ref[...]) * (1 + sc1) + sh1).astype(_BF16)

        def proj(c0, n):
            return _dot(h, w_scr[:, c0:c0 + n])

        def put_kv(ref, z):
            if not kv_transposed:
                ref[rows, :] = z.astype(ref.dtype)
                return
            l = ref.shape[2]
            for t0 in range(0, rb, l):
                for c0 in range(0, z.shape[1], LANES):
                    ref[(r + t0) // l, c0:c0 + LANES, :] = z[t0:t0 + l, c0:c0 + LANES].T.astype(ref.dtype)

        q_ref[rows, 0:W_A] = (proj(C_QA, W_A) * Q_SCALE).astype(q_ref.dtype)
        put_kv(ka_ref, proj(C_KA, W_A))
        put_kv(va_ref, proj(C_VA, W_A))
        zkv = proj(C_KB, 2 * KV_W_B)
        zkb, zvb = zkv[:, :KV_W_B], zkv[:, KV_W_B:]
        zqb = proj(C_QB, W_B)
        put_kv(vb_ref, zvb)
        if rope_refs is not None:
            cos = rope_refs[0][rows, :]
            sin = rope_refs[1][rows, :]
            for j in range(W_B // LANES):
                q_ref[rows, W_A + LANES * j:W_A + LANES * (j + 1)] = (
                    _rope(zqb[:, LANES * j:LANES * (j + 1)], cos, sin) * Q_SCALE).astype(q_ref.dtype)
            put_kv(kb_ref, _rope(zkb, cos, sin))
        else:
            q_ref[rows, W_A:W_A + W_B] = (zqb * Q_SCALE).astype(q_ref.dtype)
            put_kv(kb_ref, zkb)


def _inproj_kernel(xp_ref, xs_ref, mod_ref, g_ref, w_ref, cos_ref, sin_ref, *refs, prompt_steps):
    outs_p, outs_s, w_scr = refs[0:5], refs[5:10], refs[10]
    i = pl.program_id(0)

    @pl.when(i == 0)
    def _():
        _prepare_in_weight(w_ref, w_scr)

    mod = mod_ref[0]

    @pl.when(i < prompt_steps)
    def _():
        _inproj_rows(xp_ref, mod, g_ref, w_scr, None, outs_p)

    @pl.when(i >= prompt_steps)
    def _():
        _inproj_rows(xs_ref, mod, g_ref, w_scr, (cos_ref, sin_ref), outs_s)


def _inproj_call(xp, xs, mod3, g_pre, w_in, rope_tabs, lp, ts):
    tm = TOKEN_TILE
    tp, tsamp = xp.shape[0], xs.shape[0]
    n_p, n_s = tp // tm, tsamp // tm
    assert n_p * tm == tp and n_s * tm == tsamp and tm % lp == 0 and ts % tm == 0
    tiles_per_seq = ts // tm
    p_tile = lambda i: jnp.minimum(i, n_p - 1)
    s_tile = lambda i: jnp.maximum(i - n_p, 0)
    mod_row = lambda i: jnp.where(i < n_p, 0, 1 + s_tile(i) // tiles_per_seq)
    nseq = tm // lp
    in_specs = [
        pl.BlockSpec((tm, D_MODEL), lambda i: (p_tile(i), 0)),
        pl.BlockSpec((tm, D_MODEL), lambda i: (s_tile(i), 0)),
        pl.BlockSpec((1, 1, 6 * D_MODEL), lambda i: (mod_row(i), 0, 0)),
        pl.BlockSpec((1, D_MODEL), lambda i: (0, 0)),
        _resident((D_MODEL, IN_WIDTH)),
        pl.BlockSpec((tm, LANES), lambda i: (s_tile(i) % tiles_per_seq, 0)),
        pl.BlockSpec((tm, LANES), lambda i: (s_tile(i) % tiles_per_seq, 0)),
    ]
    kvt_shape = lambda w: jax.ShapeDtypeStruct((tp // lp, w, lp), _F32)
    kvt_spec = lambda w: pl.BlockSpec((nseq, w, lp), lambda i: (p_tile(i), 0, 0))
    kv_shape = lambda w: jax.ShapeDtypeStruct((tsamp, w), _BF16)
    kv_spec = lambda w: pl.BlockSpec((tm, w), lambda i: (s_tile(i), 0))
    out_shape = (jax.ShapeDtypeStruct((tp, W_A + W_B), _BF16),
                 kvt_shape(W_A), kvt_shape(W_A), kvt_shape(KV_W_B), kvt_shape(KV_W_B),
                 jax.ShapeDtypeStruct((tsamp, W_A + W_B), _BF16),
                 kv_shape(W_A), kv_shape(W_A), kv_shape(KV_W_B), kv_shape(KV_W_B))
    out_specs = (pl.BlockSpec((tm, W_A + W_B), lambda i: (p_tile(i), 0)),
                 kvt_spec(W_A), kvt_spec(W_A), kvt_spec(KV_W_B), kvt_spec(KV_W_B),
                 pl.BlockSpec((tm, W_A + W_B), lambda i: (s_tile(i), 0)),
                 kv_spec(W_A), kv_spec(W_A), kv_spec(KV_W_B), kv_spec(KV_W_B))
    outs = pl.pallas_call(
        functools.partial(_inproj_kernel, prompt_steps=n_p),
        grid=(n_p + n_s,),
        in_specs=in_specs,
        out_specs=out_specs,
        out_shape=out_shape,
        scratch_shapes=[pltpu.VMEM((D_MODEL, IN_WIDTH), _BF16)],
        compiler_params=pltpu.CompilerParams(
            dimension_semantics=("arbitrary",), vmem_limit_bytes=VMEM_LIMIT),
        name="inproj",
    )(xp, xs, mod3, g_pre.reshape(1, D_MODEL), w_in, *rope_tabs)
    return outs[:5], outs[5:]


def _ctx_attn_kernel(sink_ref, q_ref, ka_ref, va_ref, kb_ref, vb_ref, ga_ref, gb_ref, wu_ref, wd_ref,
                     a_ref, b_ref, wu_out, wd_out):
    wu_out[...] = wu_ref[...].astype(wu_out.dtype)
    wd_out[...] = wd_ref[...].astype(wd_out.dtype)
    l = q_ref.shape[0]
    lo = lax.broadcasted_iota(jnp.int32, (l, LANES), 1) < HEAD_DIM
    row_lo = lax.broadcasted_iota(jnp.int32, (2 * l, 1), 0) < l
    oa = []
    for p in range(W_A // LANES):
        sl = slice(LANES * p, LANES * (p + 1))
        qs = _stack_heads(q_ref[:, sl], lo)
        s = _dot(qs, ka_ref[0, sl, :].astype(_BF16))
        vt = va_ref[0, sl, :].astype(_BF16)
        oa.append(_unstack_heads(_softmax_pv([s], [lambda e, vt=vt: _dot_nt(e, vt)]), lo))
    a_ref[...] = _rms(jnp.concatenate(oa, axis=1), ga_ref[...]).astype(a_ref.dtype)
    kb = kb_ref[0].astype(_BF16)
    vb = vb_ref[0].astype(_BF16)
    ob = []
    for j in range(G_B):
        qs = _stack_heads(q_ref[:, W_A + LANES * j:W_A + LANES * (j + 1)], lo)
        s = _dot(qs, kb)
        sink = jnp.where(row_lo, sink_ref[0, j], sink_ref[1, j])
        ob.append(_unstack_heads(_softmax_pv([s], [lambda e: _dot_nt(e, vb)], sink), lo))
    b_ref[...] = _rms(jnp.concatenate(ob, axis=1), gb_ref[...]).astype(b_ref.dtype)


def _ctx_attn_call(sink, q, ka, va, kb, vb, g_a, g_b, w_up, w_down, seq):
    t = q.shape[0]
    steps = t // seq
    ru, rd = w_up.shape[0] // steps, w_down.shape[0] // steps
    assert ru * steps == w_up.shape[0] and rd * steps == w_down.shape[0] and ru % 16 == 0 and rd % 16 == 0
    row = lambda b: (b, 0)
    seq3 = lambda b: (b, 0, 0)
    const = lambda b: (0, 0)
    return pl.pallas_call(
        _ctx_attn_kernel,
        grid=(steps,),
        in_specs=[
            pl.BlockSpec(memory_space=pltpu.SMEM),
            pl.BlockSpec((seq, W_A + W_B), row),
            pl.BlockSpec((1, W_A, seq), seq3),
            pl.BlockSpec((1, W_A, seq), seq3),
            pl.BlockSpec((1, KV_W_B, seq), seq3),
            pl.BlockSpec((1, KV_W_B, seq), seq3),
            pl.BlockSpec((1, W_A), const),
            pl.BlockSpec((1, W_B), const),
            pl.BlockSpec((ru, w_up.shape[1]), row),
            pl.BlockSpec((rd, w_down.shape[1]), row),
        ],
        out_specs=(pl.BlockSpec((seq, W_A), row), pl.BlockSpec((seq, W_B), row),
                   pl.BlockSpec((ru, w_up.shape[1]), row), pl.BlockSpec((rd, w_down.shape[1]), row)),
        out_shape=(jax.ShapeDtypeStruct((t, W_A), _BF16), jax.ShapeDtypeStruct((t, W_B), _BF16),
                   jax.ShapeDtypeStruct(w_up.shape, _BF16), jax.ShapeDtypeStruct(w_down.shape, _BF16)),
        compiler_params=pltpu.CompilerParams(
            dimension_semantics=("parallel",), vmem_limit_bytes=VMEM_LIMIT),
        name="ctx_attn",
    )(sink, q, ka, va, kb, vb, g_a.reshape(1, W_A), g_b.reshape(1, W_B), w_up, w_down)


def _na_plan(rows):
    kr = min(NA_ROWS, rows)
    win = kr + NA_QROWS
    row_start = lambda r: min(max(r - kr // 2, 0), rows - kr)
    blocks, classes = [], []
    for r0 in range(0, rows, NA_QROWS):
        ws = min(row_start(r0), rows - win)
        keys = []
        for r in range(r0, r0 + NA_QROWS):
            key = (ws - r + NA_ROWS - 1, row_start(r) - ws)
            if key not in classes:
                classes.append(key)
            keys.append(classes.index(key))
        blocks.append((r0, ws, keys))
    return kr, win, blocks, classes


def _na_kernel(q_ref, k_ref, v_ref, ck_ref, cv_ref, rpb_ref, ga_ref, a_ref, bias_ref, o_scr):
    rows = q_ref.shape[0] // GRID_W
    kr, win, blocks, classes = _na_plan(rows)
    m = NA_QROWS * GRID_W
    lane = lax.broadcasted_iota(jnp.int32, (GRID_W, LANES), 1)
    lo = lane < HEAD_DIM
    lo_q = lax.broadcasted_iota(jnp.int32, (m, LANES), 1) < HEAD_DIM
    cq = lax.broadcasted_iota(jnp.int32, (GRID_W, LANES), 0)
    ck = lane & (GRID_W - 1)
    cs = jnp.clip(cq - NA_COLS // 2, 0, GRID_W - NA_COLS)
    valid = (ck >= cs) & (ck < cs + NA_COLS)
    neg = jnp.full((GRID_W, LANES), NEG_INF, _F32)
    n_drow = 2 * NA_ROWS - 1

    for p in range(W_A // LANES):
        sl = slice(LANES * p, LANES * (p + 1))
        for hh in range(2):
            t_lo, t_hi = [], []
            for d in range(n_drow):
                base = jnp.broadcast_to(rpb_ref[2 * p + hh, d:d + 1, :], (GRID_W, LANES))
                t_lo.append(pltpu.roll(base, LANES - (NA_COLS - 1), 1, stride=1, stride_axis=0))
                t_hi.append(pltpu.roll(base, GRID_W - (NA_COLS - 1), 1, stride=1, stride_axis=0))
            tiles = {}
            for ci, (d_first, off) in enumerate(classes):
                for j in range(win // 2):
                    d_of = lambda i: d_first + i if off <= i < off + kr else None
                    key = (d_of(2 * j), d_of(2 * j + 1))
                    if key not in tiles:
                        left = neg if key[0] is None else t_lo[key[0]]
                        right = neg if key[1] is None else t_hi[key[1]]
                        tiles[key] = jnp.where(valid, jnp.where(lo, left, right), NEG_INF)
                    bias_ref[ci, GRID_W * hh:GRID_W * (hh + 1), LANES * j:LANES * (j + 1)] = tiles[key]

        ckt = ck_ref[0, sl, :].astype(_BF16)
        cvt = cv_ref[0, sl, :].astype(_BF16)

        for r0, ws, keys in blocks:
            q0, k0 = r0 * GRID_W, ws * GRID_W
            qs = _stack_heads(q_ref[q0:q0 + m, sl], lo_q)
            kw = k_ref[k0:k0 + win * GRID_W, sl]
            vw = v_ref[k0:k0 + win * GRID_W, sl]
            bias = jnp.concatenate(
                [bias_ref[ci, GRID_W * hh:GRID_W * (hh + 1), :] for hh in range(2) for ci in keys], axis=0)
            s_nb = _dot_nt(qs, kw) + bias
            s_cx = _dot(qs, ckt)
            o2 = _softmax_pv([s_nb, s_cx], [lambda e, vw=vw: _dot(e, vw), lambda e: _dot_nt(e, cvt)])
            o_scr[q0:q0 + m, sl] = _unstack_heads(o2, lo_q)

    a_ref[...] = _rms(o_scr[...], ga_ref[...]).astype(a_ref.dtype)


def _na_call(q, k, v, ckt, cvt, rpb_pad, g_a, nb, seq):
    past = ckt.shape[2]
    _, win, _, classes = _na_plan(seq // GRID_W)
    blk = lambda b: (b, 0)
    return pl.pallas_call(
        _na_kernel,
        grid=(nb,),
        in_specs=[
            pl.BlockSpec((seq, W_A), blk),
            pl.BlockSpec((seq, W_A), blk),
            pl.BlockSpec((seq, W_A), blk),
            pl.BlockSpec((1, W_A, past), lambda b: (b, 0, 0)),
            pl.BlockSpec((1, W_A, past), lambda b: (b, 0, 0)),
            pl.BlockSpec((H_A, 2 * NA_ROWS - 1, LANES), lambda b: (0, 0, 0)),
            pl.BlockSpec((1, W_A), lambda b: (0, 0)),
        ],
        out_specs=pl.BlockSpec((seq, W_A), blk),
        out_shape=jax.ShapeDtypeStruct((nb * seq, W_A), _BF16),
        scratch_shapes=[pltpu.VMEM((len(classes), 2 * GRID_W, win * GRID_W), _F32),
                        pltpu.VMEM((seq, W_A), _F32)],
        compiler_params=pltpu.CompilerParams(
            dimension_semantics=("parallel",), vmem_limit_bytes=VMEM_LIMIT),
        name="na_attn",
    )(q, k, v, ckt, cvt, rpb_pad, g_a.reshape(1, W_A))


def _win_kernel(sink_ref, q_ref, k_ref, v_ref, ck_ref, cv_ref, gb_ref, b_ref, o_scr):
    t = q_ref.shape[0]
    nblk = t // BLOCK
    band = 3 * BLOCK
    lo = lax.broadcasted_iota(jnp.int32, (BLOCK, LANES), 1) < HEAD_DIM
    row_lo = lax.broadcasted_iota(jnp.int32, (2 * BLOCK, 1), 0) < BLOCK
    qi = lax.broadcasted_iota(jnp.int32, (2 * BLOCK, band), 0) & (BLOCK - 1)
    kj = lax.broadcasted_iota(jnp.int32, (2 * BLOCK, band), 1)
    rel = qi - kj
    ckt = ck_ref[0].astype(_BF16)
    cvt = cv_ref[0].astype(_BF16)

    starts = [min(max((n - 1) * BLOCK, 0), t - band) for n in range(nblk)]
    masks = {off: jnp.where(jnp.abs(rel + off) <= SWA_WINDOW, 0.0, NEG_INF)
             for off in sorted({n * BLOCK - starts[n] for n in range(nblk)})}

    for j in range(G_B):
        sl = slice(LANES * j, LANES * (j + 1))
        sink = jnp.where(row_lo, sink_ref[0, j], sink_ref[1, j])
        for n in range(nblk):
            q0, k0 = n * BLOCK, starts[n]
            qs = _stack_heads(q_ref[q0:q0 + BLOCK, sl], lo)
            s = _dot_nt(qs, k_ref[k0:k0 + band, :]) + masks[q0 - k0]
            s_cx = _dot(qs, ckt)
            vband = v_ref[k0:k0 + band, :]
            o2 = _softmax_pv([s, s_cx], [lambda e, vband=vband: _dot(e, vband), lambda e: _dot_nt(e, cvt)],
                             sink)
            o_scr[q0:q0 + BLOCK, sl] = _unstack_heads(o2, lo)

    b_ref[...] = _rms(o_scr[...], gb_ref[...]).astype(b_ref.dtype)


def _win_call(sink, q, k, v, ckt, cvt, g_b, nb, seq):
    past = ckt.shape[2]
    return pl.pallas_call(
        _win_kernel,
        grid=(nb,),
        in_specs=[
            pl.BlockSpec(memory_space=pltpu.SMEM),
            pl.BlockSpec((seq, W_B), lambda b: (b, W_A // W_B)),
            pl.BlockSpec((seq, KV_W_B), lambda b: (b, 0)),
            pl.BlockSpec((seq, KV_W_B), lambda b: (b, 0)),
            pl.BlockSpec((1, KV_W_B, past), lambda b: (b, 0, 0)),
            pl.BlockSpec((1, KV_W_B, past), lambda b: (b, 0, 0)),
            pl.BlockSpec((1, W_B), lambda b: (0, 0)),
        ],
        out_specs=pl.BlockSpec((seq, W_B), lambda b: (b, 0)),
        out_shape=jax.ShapeDtypeStruct((nb * seq, W_B), _BF16),
        scratch_shapes=[pltpu.VMEM((seq, W_B), _F32)],
        compiler_params=pltpu.CompilerParams(
            dimension_semantics=("parallel",), vmem_limit_bytes=VMEM_LIMIT),
        name="win_attn",
    )(sink, q, k, v, ckt, cvt, g_b.reshape(1, W_B))


def _prepare_out_weight(w_ref, w_scr):
    for r0 in range(0, W_A, CAST_COLS):
        w_scr[r0:r0 + CAST_COLS, :] = w_ref[r0:r0 + CAST_COLS, :].astype(_BF16)
    for i, h in enumerate(_QB_HEAD_ORDER):
        w_scr[W_A + HEAD_DIM * i:W_A + HEAD_DIM * (i + 1), :] = (
            w_ref[W_A + HEAD_DIM * h:W_A + HEAD_DIM * (h + 1), :].astype(_BF16))


def _zero_row(x, row):
    r0 = row - row % 8
    sub = lax.broadcasted_iota(jnp.int32, (8, 1), 0)
    slab = jnp.where(sub == row % 8, 0.0, x[r0:r0 + 8])
    parts = ([x[:r0]] if r0 else []) + [slab] + ([x[r0 + 8:]] if r0 + 8 < x.shape[0] else [])
    return jnp.concatenate(parts, axis=0)


def _conv3(u, cw, cb, seq_len):
    tm = u.shape[0]
    prev = pltpu.roll(u, 1, 0)
    nxt = pltpu.roll(u, tm - 1, 0)
    for s0 in range(0, tm, seq_len):
        prev = _zero_row(prev, s0)
        nxt = _zero_row(nxt, s0 + seq_len - 1)
    return prev * cw[0:1, :] + u * cw[1:2, :] + nxt * cw[2:3, :] + cb


def _gated_silu(gate, val):
    half = 0.5 * gate
    return ((half + half * jnp.tanh(half)) * val).astype(_BF16)


def _up(h2_ref, w):
    tm = h2_ref.shape[0]
    return jnp.concatenate(
        [_dot(h2_ref[r:r + FFN_ROW_BLOCK, :], w) for r in range(0, tm, FFN_ROW_BLOCK)], axis=0)


def _out_proj(a_ref, b_ref, x_ref, mod, gpost_ref, gffn_ref, wo_scr, x1_ref, h2_ref):
    gt1 = mod[:, 2 * D_MODEL:3 * D_MODEL]
    sh2 = mod[:, 3 * D_MODEL:4 * D_MODEL]
    sc2 = mod[:, 4 * D_MODEL:5 * D_MODEL]
    for r in range(0, x_ref.shape[0], FFN_ROW_BLOCK):
        rows = slice(r, r + FFN_ROW_BLOCK)
        y = _dot(a_ref[rows, :], wo_scr[0:W_A, :]) + _dot(b_ref[rows, :], wo_scr[W_A:W_A + W_B, :])
        x1 = x_ref[rows, :] + gt1 * _rms(y, gpost_ref[...])
        x1_ref[rows, :] = x1
        h2_ref[rows, :] = (_rms(x1, gffn_ref[...]) * (1 + sc2) + sh2).astype(h2_ref.dtype)


def _down_proj(act_ref, wd_ref, mod, gpost2_ref, x1_ref, out_ref):
    gt2 = mod[:, 5 * D_MODEL:6 * D_MODEL]
    for r in range(0, act_ref.shape[0], DOWN_ROW_BLOCK):
        rows = slice(r, r + DOWN_ROW_BLOCK)
        y = _dot(act_ref[rows, :], wd_ref[...])
        out_ref[rows, :] = x1_ref[rows, :] + gt2 * _rms(y, gpost2_ref[...])


def _tail_seq_kernel(a_ref, b_ref, x_ref, mod_ref, gpost_ref, gffn_ref, gpost2_ref, wo_ref, wu_ref,
                     cw_ref, cb_ref, wd_ref, out_ref, wo_scr, h2_scr, act_scr, *, seq_len):
    mod = mod_ref[0]

    @pl.when(pl.program_id(0) == 0)
    def _():
        _prepare_out_weight(wo_ref, wo_scr)

    _out_proj(a_ref, b_ref, x_ref, mod, gpost_ref, gffn_ref, wo_scr, out_ref, h2_scr)
    tc = FF_CHUNK

    def chunk(col0):
        cols = slice(col0, col0 + tc)
        return _conv3(_up(h2_scr, wu_ref[:, cols]), cw_ref[:, cols], cb_ref[:, cols], seq_len)

    for c0 in range(0, D_FF, tc):
        act_scr[:, c0:c0 + tc] = _gated_silu(chunk(c0), chunk(D_FF + c0))
    _down_proj(act_scr, wd_ref, mod, gpost2_ref, out_ref, out_ref)


def _tail_seq_call(a, b, x2d, mod3, mod_row, g_post, g_ffn, g_post2, w_out, w_up, conv_w, conv_b, w_down,
                   seq_len, name):
    t = x2d.shape[0]
    tm = max(SEQ_TAIL_TILE, seq_len)
    big = tm > SEQ_TAIL_TILE
    vmem_limit = TAIL_VMEM_LIMIT if big else VMEM_LIMIT
    x_mode = dict(pipeline_mode=pl.Buffered(1)) if big else {}
    row = lambda i: (i, 0)
    const = lambda i: (0, 0)
    return pl.pallas_call(
        functools.partial(_tail_seq_kernel, seq_len=seq_len),
        grid=(t // tm,),
        in_specs=[
            pl.BlockSpec((tm, W_A), row),
            pl.BlockSpec((tm, W_B), row),
            pl.BlockSpec((tm, D_MODEL), row, **x_mode),
            pl.BlockSpec((1, 1, 6 * D_MODEL), lambda i: (mod_row(i), 0, 0)),
            pl.BlockSpec((1, D_MODEL), const),
            pl.BlockSpec((1, D_MODEL), const),
            pl.BlockSpec((1, D_MODEL), const),
            _resident((W_A + W_B, D_MODEL)),
            _resident((D_MODEL, 2 * D_FF)),
            _resident((3, 2 * D_FF)),
            _resident((1, 2 * D_FF)),
            _resident((D_FF, D_MODEL)),
        ],
        out_specs=pl.BlockSpec((tm, D_MODEL), row),
        out_shape=jax.ShapeDtypeStruct((t, D_MODEL), _F32),
        scratch_shapes=[
            pltpu.VMEM((W_A + W_B, D_MODEL), _BF16),
            pltpu.VMEM((tm, D_MODEL), _BF16),
            pltpu.VMEM((tm, D_FF), _BF16),
        ],
        compiler_params=pltpu.CompilerParams(
            dimension_semantics=("arbitrary",), vmem_limit_bytes=vmem_limit),
        name=name,
    )(a, b, x2d, mod3, g_post.reshape(1, D_MODEL), g_ffn.reshape(1, D_MODEL), g_post2.reshape(1, D_MODEL),
      w_out, w_up, conv_w, conv_b.reshape(1, 2 * D_FF), w_down)


def _rope_tables(seq):
    n = ROPE_HALF
    t = np.arange(seq)
    lane = np.arange(HEAD_DIM)
    pos = np.where(lane[None, :] < HEAD_DIM // 2, (t // GRID_W)[:, None], (t % GRID_W)[:, None])
    inv = 1.0 / (ROPE_THETA ** (np.arange(n, dtype=np.float64) / n))
    ang = pos.astype(np.float64) * inv[lane % n][None, :]
    sign = np.where((lane & n) == 0, -1.0, 1.0)[None, :]
    cos = np.tile(np.cos(ang), (1, LANES // HEAD_DIM)).astype(np.float32)
    sin = np.tile(np.sin(ang) * sign, (1, LANES // HEAD_DIM)).astype(np.float32)
    return jnp.asarray(cos), jnp.asarray(sin)


def _to_head_dim_token(cache_l):
    b, l, h, d = cache_l.shape
    return jnp.transpose(cache_l, (0, 2, 3, 1)).reshape(b, h * d, l)


def _from_head_dim_token(kv_t, heads):
    b, _, l = kv_t.shape
    return jnp.transpose(kv_t.reshape(b, 1, heads, HEAD_DIM, l), (0, 1, 4, 2, 3))


def kernel(x_prompt, x_sample, cache_a_k, cache_a_v, cache_b_k, cache_b_v, c, c_ctx, w_mod, b_mod,
           g_mix_pre, g_mix_post, g_ffn_pre, g_ffn_post, w_in, rpb_a, sink_b, g_grp_a, g_grp_b,
           w_out, w_up, conv_w, conv_b, w_down):
    bp, lp, _ = x_prompt.shape
    bs, ts, _ = x_sample.shape
    assert w_in.shape[0] == 1
    xp = x_prompt.reshape(bp * lp, D_MODEL)
    xs = x_sample.reshape(bs * ts, D_MODEL)
    cond8 = jnp.concatenate([c_ctx[None], c, jnp.zeros((8 - 1 - bs, D_MODEL), _F32)], axis=0)
    cos_tab, sin_tab = _rope_tables(ts)

    l = 0
    mod3 = _mod_call(cond8, w_mod[l], b_mod[l]).reshape(8, 1, 6 * D_MODEL)
    g_b_p = g_grp_b[l].reshape(KV_B, G_B, HEAD_DIM).transpose(1, 0, 2).reshape(W_B)
    rpb_pad = jnp.pad(rpb_a[l], ((0, 0), (0, 0), (0, LANES - (2 * NA_COLS - 1))))

    prompt_row = lambda i: 0
    assert ts >= SEQ_TAIL_TILE
    sample_row = lambda i: 1 + i

    (q_p, kat_p, vat_p, kbt_p, vbt_p), (q_s, ka_s, va_s, kb_s, vb_s) = _inproj_call(
        xp, xs, mod3, g_mix_pre[l], w_in[l], (cos_tab, sin_tab), lp, ts)

    a_p, b_p, w_up_b, w_down_b = _ctx_attn_call(sink_b[l], q_p, kat_p, vat_p, kbt_p, vbt_p, g_grp_a[l], g_b_p,
                                                w_up[l], w_down[l], lp)
    y_p = _tail_seq_call(a_p, b_p, xp, mod3, prompt_row, g_mix_post[l], g_ffn_pre[l], g_ffn_post[l], w_out[l],
                         w_up_b, conv_w[l], conv_b[l], w_down_b, lp, "tail_prompt")

    a_s = _na_call(q_s, ka_s, va_s, _to_head_dim_token(cache_a_k[:, l]), _to_head_dim_token(cache_a_v[:, l]),
                   rpb_pad, g_grp_a[l], bs, ts)
    b_s = _win_call(sink_b[l], q_s, kb_s, vb_s, _to_head_dim_token(cache_b_k[:, l]),
                    _to_head_dim_token(cache_b_v[:, l]), g_b_p, bs, ts)
    y_s = _tail_seq_call(a_s, b_s, xs, mod3, sample_row, g_mix_post[l], g_ffn_pre[l], g_ffn_post[l], w_out[l],
                         w_up_b, conv_w[l], conv_b[l], w_down_b, ts, "tail_sample")

    return (y_p.reshape(bp, lp, D_MODEL), y_s.reshape(bs, ts, D_MODEL),
            _from_head_dim_token(kat_p, H_A), _from_head_dim_token(vat_p, H_A),
            _from_head_dim_token(kbt_p, KV_B), _from_head_dim_token(vbt_p, KV_B))
```

```python
import functools

import numpy as np
import jax
import jax.numpy as jnp
from jax import lax
from jax.experimental import pallas as pl
from jax.experimental.pallas import tpu as pltpu

D_MODEL = 1024
HEAD_DIM = 64
H_A = 8
H_B = 8
KV_B = 2
G_B = H_B // KV_B
W_A = H_A * HEAD_DIM
W_B = H_B * HEAD_DIM
KV_W_B = KV_B * HEAD_DIM
IN_WIDTH = 3 * W_A + W_B + 2 * KV_W_B
GRID_W = 64
NA_ROWS = 8
NA_COLS = 16
SWA_WINDOW = 128
BLOCK = 128
D_FF = 2816
ROPE_THETA = 10000.0
EPS = 1e-6
NEG_INF = -1e30
Q_SCALE = HEAD_DIM ** -0.5
ROPE_HALF = HEAD_DIM // 4

LANES = 128
FF_CHUNK = 256
TOKEN_TILE = 512
FFN_ROW_BLOCK = 256
DOWN_ROW_BLOCK = 256
SEQ_TAIL_TILE = 512
PROJ_ROW_BLOCK = 256
MOD_STREAMS = 2
MOD_ROW_BLOCK = 128
NA_QROWS = 2
CAST_COLS = 512
VMEM_LIMIT = 48 * 1024 * 1024
TAIL_VMEM_LIMIT = 56 * 1024 * 1024

C_QA, C_KA, C_VA, C_QB, C_KB, C_VB = 0, W_A, 2 * W_A, 3 * W_A, 3 * W_A + W_B, 3 * W_A + W_B + KV_W_B

_QB_HEAD_ORDER = tuple(kv * G_B + g for g in range(G_B) for kv in range(KV_B))

_BF16 = jnp.bfloat16
_F32 = jnp.float32


def _dot(a, b):
    return jnp.dot(a, b, preferred_element_type=_F32)


def _dot_nt(a, b):
    return lax.dot_general(a, b, (((1,), (1,)), ((), ())), preferred_element_type=_F32)


def _rms(x, g):
    var = jnp.mean(x * x, axis=-1, keepdims=True)
    return x * lax.rsqrt(var + EPS) * g


def _stack_heads(q2, lo):
    zero = jnp.zeros_like(q2)
    return jnp.concatenate([jnp.where(lo, q2, zero), jnp.where(lo, zero, q2)], axis=0)


def _unstack_heads(o2, lo):
    m = o2.shape[0] // 2
    return jnp.where(lo, o2[:m], o2[m:])


def _softmax_pv(scores, pv, sink=None):
    m = scores[0].max(axis=-1, keepdims=True)
    for s in scores[1:]:
        m = jnp.maximum(m, s.max(axis=-1, keepdims=True))
    if sink is not None:
        m = jnp.maximum(m, sink)
    denom = None
    out = None
    for s, f in zip(scores, pv):
        e = jnp.exp(s - m)
        l = e.sum(axis=-1, keepdims=True)
        o = f(e.astype(_BF16))
        denom = l if denom is None else denom + l
        out = o if out is None else out + o
    if sink is not None:
        denom = denom + jnp.exp(sink - m)
    return out / denom


def _resident(shape):
    return pl.BlockSpec(shape, lambda *_: (0,) * len(shape), pipeline_mode=pl.Buffered(1))


def _mod_kernel(c_ref, *refs):
    w_refs, b_ref, o_ref = refs[:-2], refs[-2], refs[-1]
    i = pl.program_id(0)

    @pl.when(i == 0)
    def _():
        o_ref[...] = jnp.broadcast_to(b_ref[...], o_ref.shape)

    acc = o_ref[...]
    for j, w_ref in enumerate(w_refs):
        c = c_ref[i * len(w_refs) + j]
        s = (c * jax.nn.sigmoid(c)).astype(_BF16)
        acc = acc + _dot(s, w_ref[...].astype(_BF16))
    o_ref[...] = acc


def _mod_call(cond8, w_mod, b_mod):
    n = w_mod.shape[1]
    kb = MOD_ROW_BLOCK
    nblk = D_MODEL // kb
    cond_blocks = cond8.reshape(8, nblk, kb).transpose(1, 0, 2)
    return pl.pallas_call(
        _mod_kernel,
        grid=(nblk // MOD_STREAMS,),
        in_specs=[pl.BlockSpec((nblk, 8, kb), lambda i: (0, 0, 0))]
        + [pl.BlockSpec((kb, n), lambda i, j=j: (i * MOD_STREAMS + j, 0)) for j in range(MOD_STREAMS)]
        + [pl.BlockSpec((1, n), lambda i: (0, 0))],
        out_specs=pl.BlockSpec((8, n), lambda i: (0, 0)),
        out_shape=jax.ShapeDtypeStruct((8, n), _F32),
        compiler_params=pltpu.CompilerParams(dimension_semantics=("arbitrary",), vmem_limit_bytes=VMEM_LIMIT),
        name="mod",
    )(cond_blocks, *([w_mod] * MOD_STREAMS), b_mod.reshape(1, n))


def _prepare_in_weight(w_ref, w_scr):
    for c0 in list(range(0, C_QB, CAST_COLS)) + [C_KB]:
        n = min(CAST_COLS, IN_WIDTH - c0)
        w_scr[:, c0:c0 + n] = w_ref[:, c0:c0 + n].astype(_BF16)
    lo = lax.broadcasted_iota(jnp.int32, (D_MODEL, LANES), 1) < HEAD_DIM
    for j in range(G_B):
        h_lo, h_hi = _QB_HEAD_ORDER[2 * j], _QB_HEAD_ORDER[2 * j + 1]

        def head_at(h, want_hi):
            blk = w_ref[:, C_QB + LANES * (h // 2):C_QB + LANES * (h // 2 + 1)]
            return blk if (h % 2 == 1) == want_hi else pltpu.roll(blk, HEAD_DIM, 1)

        w_scr[:, C_QB + LANES * j:C_QB + LANES * (j + 1)] = jnp.where(
            lo, head_at(h_lo, False), head_at(h_hi, True)).astype(_BF16)


def _rope(z, cos, sin):
    lane = lax.broadcasted_iota(jnp.int32, z.shape, 1)
    partner = jnp.where((lane & ROPE_HALF) == 0,
                        pltpu.roll(z, LANES - ROPE_HALF, 1), pltpu.roll(z, ROPE_HALF, 1))
    return z * cos + partner * sin


def _inproj_rows(x_ref, mod, g_ref, w_scr, rope_refs, outs):
    q_ref, ka_ref, va_ref, kb_ref, vb_ref = outs
    kv_transposed = rope_refs is None
    sh1 = mod[:, 0:D_MODEL]
    sc1 = mod[:, D_MODEL:2 * D_MODEL]
    rb = PROJ_ROW_BLOCK

    for r in range(0, x_ref.shape[0], rb):
        rows = slice(r, r + rb)
        h = (_rms(x_ref[rows, :], g_ref[...]) * (1 + sc1) + sh1).astype(_BF16)

        def proj(c0, n):
            return _dot(h, w_scr[:, c0:c0 + n])

        def put_kv(ref, z):
            if not kv_transposed:
                ref[rows, :] = z.astype(ref.dtype)
                return
            l = ref.shape[2]
            for t0 in range(0, rb, l):
                for c0 in range(0, z.shape[1], LANES):
                    ref[(r + t0) // l, c0:c0 + LANES, :] = z[t0:t0 + l, c0:c0 + LANES].T.astype(ref.dtype)

        q_ref[rows, 0:W_A] = (proj(C_QA, W_A) * Q_SCALE).astype(q_ref.dtype)
        put_kv(ka_ref, proj(C_KA, W_A))
        put_kv(va_ref, proj(C_VA, W_A))
        zkv = proj(C_KB, 2 * KV_W_B)
        zkb, zvb = zkv[:, :KV_W_B], zkv[:, KV_W_B:]
        zqb = proj(C_QB, W_B)
        put_kv(vb_ref, zvb)
        if rope_refs is not None:
            cos = rope_refs[0][rows, :]
            sin = rope_refs[1][rows, :]
            for j in range(W_B // LANES):
                q_ref[rows, W_A + LANES * j:W_A + LANES * (j + 1)] = (
                    _rope(zqb[:, LANES * j:LANES * (j + 1)], cos, sin) * Q_SCALE).astype(q_ref.dtype)
            put_kv(kb_ref, _rope(zkb, cos, sin))
        else:
            q_ref[rows, W_A:W_A + W_B] = (zqb * Q_SCALE).astype(q_ref.dtype)
            put_kv(kb_ref, zkb)


def _inproj_kernel(xp_ref, xs_ref, mod_ref, g_ref, w_ref, cos_ref, sin_ref, *refs, prompt_steps):
    outs_p, outs_s, w_scr = refs[0:5], refs[5:10], refs[10]
    i = pl.program_id(0)

    @pl.when(i == 0)
    def _():
        _prepare_in_weight(w_ref, w_scr)

    mod = mod_ref[0]

    @pl.when(i < prompt_steps)
    def _():
        _inproj_rows(xp_ref, mod, g_ref, w_scr, None, outs_p)

    @pl.when(i >= prompt_steps)
    def _():
        _inproj_rows(xs_ref, mod, g_ref, w_scr, (cos_ref, sin_ref), outs_s)


def _inproj_call(xp, xs, mod3, g_pre, w_in, rope_tabs, lp, ts):
    tm = TOKEN_TILE
    tp, tsamp = xp.shape[0], xs.shape[0]
    n_p, n_s = tp // tm, tsamp // tm
    assert n_p * tm == tp and n_s * tm == tsamp and tm % lp == 0 and ts % tm == 0
    tiles_per_seq = ts // tm
    p_tile = lambda i: jnp.minimum(i, n_p - 1)
    s_tile = lambda i: jnp.maximum(i - n_p, 0)
    mod_row = lambda i: jnp.where(i < n_p, 0, 1 + s_tile(i) // tiles_per_seq)
    nseq = tm // lp
    in_specs = [
        pl.BlockSpec((tm, D_MODEL), lambda i: (p_tile(i), 0)),
        pl.BlockSpec((tm, D_MODEL), lambda i: (s_tile(i), 0)),
        pl.BlockSpec((1, 1, 6 * D_MODEL), lambda i: (mod_row(i), 0, 0)),
        pl.BlockSpec((1, D_MODEL), lambda i: (0, 0)),
        _resident((D_MODEL, IN_WIDTH)),
        pl.BlockSpec((tm, LANES), lambda i: (s_tile(i) % tiles_per_seq, 0)),
        pl.BlockSpec((tm, LANES), lambda i: (s_tile(i) % tiles_per_seq, 0)),
    ]
    kvt_shape = lambda w: jax.ShapeDtypeStruct((tp // lp, w, lp), _F32)
    kvt_spec = lambda w: pl.BlockSpec((nseq, w, lp), lambda i: (p_tile(i), 0, 0))
    kv_shape = lambda w: jax.ShapeDtypeStruct((tsamp, w), _BF16)
    kv_spec = lambda w: pl.BlockSpec((tm, w), lambda i: (s_tile(i), 0))
    out_shape = (jax.ShapeDtypeStruct((tp, W_A + W_B), _BF16),
                 kvt_shape(W_A), kvt_shape(W_A), kvt_shape(KV_W_B), kvt_shape(KV_W_B),
                 jax.ShapeDtypeStruct((tsamp, W_A + W_B), _BF16),
                 kv_shape(W_A), kv_shape(W_A), kv_shape(KV_W_B), kv_shape(KV_W_B))
    out_specs = (pl.BlockSpec((tm, W_A + W_B), lambda i: (p_tile(i), 0)),
                 kvt_spec(W_A), kvt_spec(W_A), kvt_spec(KV_W_B), kvt_spec(KV_W_B),
                 pl.BlockSpec((tm, W_A + W_B), lambda i: (s_tile(i), 0)),
                 kv_spec(W_A), kv_spec(W_A), kv_spec(KV_W_B), kv_spec(KV_W_B))
    outs = pl.pallas_call(
        functools.partial(_inproj_kernel, prompt_steps=n_p),
        grid=(n_p + n_s,),
        in_specs=in_specs,
        out_specs=out_specs,
        out_shape=out_shape,
        scratch_shapes=[pltpu.VMEM((D_MODEL, IN_WIDTH), _BF16)],
        compiler_params=pltpu.CompilerParams(
            dimension_semantics=("arbitrary",), vmem_limit_bytes=VMEM_LIMIT),
        name="inproj",
    )(xp, xs, mod3, g_pre.reshape(1, D_MODEL), w_in, *rope_tabs)
    return outs[:5], outs[5:]


def _ctx_attn_kernel(sink_ref, q_ref, ka_ref, va_ref, kb_ref, vb_ref, ga_ref, gb_ref, wu_ref, wd_ref,
                     a_ref, b_ref, wu_out, wd_out):
    wu_out[...] = wu_ref[...].astype(wu_out.dtype)
    wd_out[...] = wd_ref[...].astype(wd_out.dtype)
    l = q_ref.shape[0]
    lo = lax.broadcasted_iota(jnp.int32, (l, LANES), 1) < HEAD_DIM
    row_lo = lax.broadcasted_iota(jnp.int32, (2 * l, 1), 0) < l
    oa = []
    for p in range(W_A // LANES):
        sl = slice(LANES * p, LANES * (p + 1))
        qs = _stack_heads(q_ref[:, sl], lo)
        s = _dot(qs, ka_ref[0, sl, :].astype(_BF16))
        vt = va_ref[0, sl, :].astype(_BF16)
        oa.append(_unstack_heads(_softmax_pv([s], [lambda e, vt=vt: _dot_nt(e, vt)]), lo))
    a_ref[...] = _rms(jnp.concatenate(oa, axis=1), ga_ref[...]).astype(a_ref.dtype)
    kb = kb_ref[0].astype(_BF16)
    vb = vb_ref[0].astype(_BF16)
    ob = []
    for j in range(G_B):
        qs = _stack_heads(q_ref[:, W_A + LANES * j:W_A + LANES * (j + 1)], lo)
        s = _dot(qs, kb)
        sink = jnp.where(row_lo, sink_ref[0, j], sink_ref[1, j])
        ob.append(_unstack_heads(_softmax_pv([s], [lambda e: _dot_nt(e, vb)], sink), lo))
    b_ref[...] = _rms(jnp.concatenate(ob, axis=1), gb_ref[...]).astype(b_ref.dtype)


def _ctx_attn_call(sink, q, ka, va, kb, vb, g_a, g_b, w_up, w_down, seq):
    t = q.shape[0]
    steps = t // seq
    ru, rd = w_up.shape[0] // steps, w_down.shape[0] // steps
    assert ru * steps == w_up.shape[0] and rd * steps == w_down.shape[0] and ru % 16 == 0 and rd % 16 == 0
    row = lambda b: (b, 0)
    seq3 = lambda b: (b, 0, 0)
    const = lambda b: (0, 0)
    return pl.pallas_call(
        _ctx_attn_kernel,
        grid=(steps,),
        in_specs=[
            pl.BlockSpec(memory_space=pltpu.SMEM),
            pl.BlockSpec((seq, W_A + W_B), row),
            pl.BlockSpec((1, W_A, seq), seq3),
            pl.BlockSpec((1, W_A, seq), seq3),
            pl.BlockSpec((1, KV_W_B, seq), seq3),
            pl.BlockSpec((1, KV_W_B, seq), seq3),
            pl.BlockSpec((1, W_A), const),
            pl.BlockSpec((1, W_B), const),
            pl.BlockSpec((ru, w_up.shape[1]), row),
            pl.BlockSpec((rd, w_down.shape[1]), row),
        ],
        out_specs=(pl.BlockSpec((seq, W_A), row), pl.BlockSpec((seq, W_B), row),
                   pl.BlockSpec((ru, w_up.shape[1]), row), pl.BlockSpec((rd, w_down.shape[1]), row)),
        out_shape=(jax.ShapeDtypeStruct((t, W_A), _BF16), jax.ShapeDtypeStruct((t, W_B), _BF16),
                   jax.ShapeDtypeStruct(w_up.shape, _BF16), jax.ShapeDtypeStruct(w_down.shape, _BF16)),
        compiler_params=pltpu.CompilerParams(
            dimension_semantics=("parallel",), vmem_limit_bytes=VMEM_LIMIT),
        name="ctx_attn",
    )(sink, q, ka, va, kb, vb, g_a.reshape(1, W_A), g_b.reshape(1, W_B), w_up, w_down)


def _na_plan(rows):
    kr = min(NA_ROWS, rows)
    win = kr + NA_QROWS
    row_start = lambda r: min(max(r - kr // 2, 0), rows - kr)
    blocks, classes = [], []
    for r0 in range(0, rows, NA_QROWS):
        ws = min(row_start(r0), rows - win)
        keys = []
        for r in range(r0, r0 + NA_QROWS):
            key = (ws - r + NA_ROWS - 1, row_start(r) - ws)
            if key not in classes:
                classes.append(key)
            keys.append(classes.index(key))
        blocks.append((r0, ws, keys))
    return kr, win, blocks, classes


def _na_kernel(q_ref, k_ref, v_ref, ck_ref, cv_ref, rpb_ref, ga_ref, a_ref, bias_ref, o_scr):
    rows = q_ref.shape[0] // GRID_W
    kr, win, blocks, classes = _na_plan(rows)
    m = NA_QROWS * GRID_W
    lane = lax.broadcasted_iota(jnp.int32, (GRID_W, LANES), 1)
    lo = lane < HEAD_DIM
    lo_q = lax.broadcasted_iota(jnp.int32, (m, LANES), 1) < HEAD_DIM
    cq = lax.broadcasted_iota(jnp.int32, (GRID_W, LANES), 0)
    ck = lane & (GRID_W - 1)
    cs = jnp.clip(cq - NA_COLS // 2, 0, GRID_W - NA_COLS)
    valid = (ck >= cs) & (ck < cs + NA_COLS)
    neg = jnp.full((GRID_W, LANES), NEG_INF, _F32)
    n_drow = 2 * NA_ROWS - 1

    for p in range(W_A // LANES):
        sl = slice(LANES * p, LANES * (p + 1))
        for hh in range(2):
            t_lo, t_hi = [], []
            for d in range(n_drow):
                base = jnp.broadcast_to(rpb_ref[2 * p + hh, d:d + 1, :], (GRID_W, LANES))
                t_lo.append(pltpu.roll(base, LANES - (NA_COLS - 1), 1, stride=1, stride_axis=0))
                t_hi.append(pltpu.roll(base, GRID_W - (NA_COLS - 1), 1, stride=1, stride_axis=0))
            tiles = {}
            for ci, (d_first, off) in enumerate(classes):
                for j in range(win // 2):
                    d_of = lambda i: d_first + i if off <= i < off + kr else None
                    key = (d_of(2 * j), d_of(2 * j + 1))
                    if key not in tiles:
                        left = neg if key[0] is None else t_lo[key[0]]
                        right = neg if key[1] is None else t_hi[key[1]]
                        tiles[key] = jnp.where(valid, jnp.where(lo, left, right), NEG_INF)
                    bias_ref[ci, GRID_W * hh:GRID_W * (hh + 1), LANES * j:LANES * (j + 1)] = tiles[key]

        ckt = ck_ref[0, sl, :].astype(_BF16)
        cvt = cv_ref[0, sl, :].astype(_BF16)

        for r0, ws, keys in blocks:
            q0, k0 = r0 * GRID_W, ws * GRID_W
            qs = _stack_heads(q_ref[q0:q0 + m, sl], lo_q)
            kw = k_ref[k0:k0 + win * GRID_W, sl]
            vw = v_ref[k0:k0 + win * GRID_W, sl]
            bias = jnp.concatenate(
                [bias_ref[ci, GRID_W * hh:GRID_W * (hh + 1), :] for hh in range(2) for ci in keys], axis=0)
            s_nb = _dot_nt(qs, kw) + bias
            s_cx = _dot(qs, ckt)
            o2 = _softmax_pv([s_nb, s_cx], [lambda e, vw=vw: _dot(e, vw), lambda e: _dot_nt(e, cvt)])
            o_scr[q0:q0 + m, sl] = _unstack_heads(o2, lo_q)

    a_ref[...] = _rms(o_scr[...], ga_ref[...]).astype(a_ref.dtype)


def _na_call(q, k, v, ckt, cvt, rpb_pad, g_a, nb, seq):
    past = ckt.shape[2]
    _, win, _, classes = _na_plan(seq // GRID_W)
    blk = lambda b: (b, 0)
    return pl.pallas_call(
        _na_kernel,
        grid=(nb,),
        in_specs=[
            pl.BlockSpec((seq, W_A), blk),
            pl.BlockSpec((seq, W_A), blk),
            pl.BlockSpec((seq, W_A), blk),
            pl.BlockSpec((1, W_A, past), lambda b: (b, 0, 0)),
            pl.BlockSpec((1, W_A, past), lambda b: (b, 0, 0)),
            pl.BlockSpec((H_A, 2 * NA_ROWS - 1, LANES), lambda b: (0, 0, 0)),
            pl.BlockSpec((1, W_A), lambda b: (0, 0)),
        ],
        out_specs=pl.BlockSpec((seq, W_A), blk),
        out_shape=jax.ShapeDtypeStruct((nb * seq, W_A), _BF16),
        scratch_shapes=[pltpu.VMEM((len(classes), 2 * GRID_W, win * GRID_W), _F32),
                        pltpu.VMEM((seq, W_A), _F32)],
        compiler_params=pltpu.CompilerParams(
            dimension_semantics=("parallel",), vmem_limit_bytes=VMEM_LIMIT),
        name="na_attn",
    )(q, k, v, ckt, cvt, rpb_pad, g_a.reshape(1, W_A))


def _win_kernel(sink_ref, q_ref, k_ref, v_ref, ck_ref, cv_ref, gb_ref, b_ref, o_scr):
    t = q_ref.shape[0]
    nblk = t // BLOCK
    band = 3 * BLOCK
    lo = lax.broadcasted_iota(jnp.int32, (BLOCK, LANES), 1) < HEAD_DIM
    row_lo = lax.broadcasted_iota(jnp.int32, (2 * BLOCK, 1), 0) < BLOCK
    qi = lax.broadcasted_iota(jnp.int32, (2 * BLOCK, band), 0) & (BLOCK - 1)
    kj = lax.broadcasted_iota(jnp.int32, (2 * BLOCK, band), 1)
    rel = qi - kj
    ckt = ck_ref[0].astype(_BF16)
    cvt = cv_ref[0].astype(_BF16)

    starts = [min(max((n - 1) * BLOCK, 0), t - band) for n in range(nblk)]
    masks = {off: jnp.where(jnp.abs(rel + off) <= SWA_WINDOW, 0.0, NEG_INF)
             for off in sorted({n * BLOCK - starts[n] for n in range(nblk)})}

    for j in range(G_B):
        sl = slice(LANES * j, LANES * (j + 1))
        sink = jnp.where(row_lo, sink_ref[0, j], sink_ref[1, j])
        for n in range(nblk):
            q0, k0 = n * BLOCK, starts[n]
            qs = _stack_heads(q_ref[q0:q0 + BLOCK, sl], lo)
            s = _dot_nt(qs, k_ref[k0:k0 + band, :]) + masks[q0 - k0]
            s_cx = _dot(qs, ckt)
            vband = v_ref[k0:k0 + band, :]
            o2 = _softmax_pv([s, s_cx], [lambda e, vband=vband: _dot(e, vband), lambda e: _dot_nt(e, cvt)],
                             sink)
            o_scr[q0:q0 + BLOCK, sl] = _unstack_heads(o2, lo)

    b_ref[...] = _rms(o_scr[...], gb_ref[...]).astype(b_ref.dtype)


def _win_call(sink, q, k, v, ckt, cvt, g_b, nb, seq):
    past = ckt.shape[2]
    return pl.pallas_call(
        _win_kernel,
        grid=(nb,),
        in_specs=[
            pl.BlockSpec(memory_space=pltpu.SMEM),
            pl.BlockSpec((seq, W_B), lambda b: (b, W_A // W_B)),
            pl.BlockSpec((seq, KV_W_B), lambda b: (b, 0)),
            pl.BlockSpec((seq, KV_W_B), lambda b: (b, 0)),
            pl.BlockSpec((1, KV_W_B, past), lambda b: (b, 0, 0)),
            pl.BlockSpec((1, KV_W_B, past), lambda b: (b, 0, 0)),
            pl.BlockSpec((1, W_B), lambda b: (0, 0)),
        ],
        out_specs=pl.BlockSpec((seq, W_B), lambda b: (b, 0)),
        out_shape=jax.ShapeDtypeStruct((nb * seq, W_B), _BF16),
        scratch_shapes=[pltpu.VMEM((seq, W_B), _F32)],
        compiler_params=pltpu.CompilerParams(
            dimension_semantics=("parallel",), vmem_limit_bytes=VMEM_LIMIT),
        name="win_attn",
    )(sink, q, k, v, ckt, cvt, g_b.reshape(1, W_B))


def _prepare_out_weight(w_ref, w_scr):
    for r0 in range(0, W_A, CAST_COLS):
        w_scr[r0:r0 + CAST_COLS, :] = w_ref[r0:r0 + CAST_COLS, :].astype(_BF16)
    for i, h in enumerate(_QB_HEAD_ORDER):
        w_scr[W_A + HEAD_DIM * i:W_A + HEAD_DIM * (i + 1), :] = (
            w_ref[W_A + HEAD_DIM * h:W_A + HEAD_DIM * (h + 1), :].astype(_BF16))


def _zero_row(x, row):
    r0 = row - row % 8
    sub = lax.broadcasted_iota(jnp.int32, (8, 1), 0)
    slab = jnp.where(sub == row % 8, 0.0, x[r0:r0 + 8])
    parts = ([x[:r0]] if r0 else []) + [slab] + ([x[r0 + 8:]] if r0 + 8 < x.shape[0] else [])
    return jnp.concatenate(parts, axis=0)


def _conv3(u, cw, cb, seq_len):
    tm = u.shape[0]
    prev = pltpu.roll(u, 1, 0)
    nxt = pltpu.roll(u, tm - 1, 0)
    for s0 in range(0, tm, seq_len):
        prev = _zero_row(prev, s0)
        nxt = _zero_row(nxt, s0 + seq_len - 1)
    return prev * cw[0:1, :] + u * cw[1:2, :] + nxt * cw[2:3, :] + cb


def _gated_silu(gate, val):
    half = 0.5 * gate
    return ((half + half * jnp.tanh(half)) * val).astype(_BF16)


def _up(h2_ref, w):
    tm = h2_ref.shape[0]
    return jnp.concatenate(
        [_dot(h2_ref[r:r + FFN_ROW_BLOCK, :], w) for r in range(0, tm, FFN_ROW_BLOCK)], axis=0)


def _out_proj(a_ref, b_ref, x_ref, mod, gpost_ref, gffn_ref, wo_scr, x1_ref, h2_ref):
    gt1 = mod[:, 2 * D_MODEL:3 * D_MODEL]
    sh2 = mod[:, 3 * D_MODEL:4 * D_MODEL]
    sc2 = mod[:, 4 * D_MODEL:5 * D_MODEL]
    for r in range(0, x_ref.shape[0], FFN_ROW_BLOCK):
        rows = slice(r, r + FFN_ROW_BLOCK)
        y = _dot(a_ref[rows, :], wo_scr[0:W_A, :]) + _dot(b_ref[rows, :], wo_scr[W_A:W_A + W_B, :])
        x1 = x_ref[rows, :] + gt1 * _rms(y, gpost_ref[...])
        x1_ref[rows, :] = x1
        h2_ref[rows, :] = (_rms(x1, gffn_ref[...]) * (1 + sc2) + sh2).astype(h2_ref.dtype)


def _down_proj(act_ref, wd_ref, mod, gpost2_ref, x1_ref, out_ref):
    gt2 = mod[:, 5 * D_MODEL:6 * D_MODEL]
    for r in range(0, act_ref.shape[0], DOWN_ROW_BLOCK):
        rows = slice(r, r + DOWN_ROW_BLOCK)
        y = _dot(act_ref[rows, :], wd_ref[...])
        out_ref[rows, :] = x1_ref[rows, :] + gt2 * _rms(y, gpost2_ref[...])


def _tail_seq_kernel(a_ref, b_ref, x_ref, mod_ref, gpost_ref, gffn_ref, gpost2_ref, wo_ref, wu_ref,
                     cw_ref, cb_ref, wd_ref, out_ref, wo_scr, h2_scr, act_scr, *, seq_len):
    mod = mod_ref[0]

    @pl.when(pl.program_id(0) == 0)
    def _():
        _prepare_out_weight(wo_ref, wo_scr)

    _out_proj(a_ref, b_ref, x_ref, mod, gpost_ref, gffn_ref, wo_scr, out_ref, h2_scr)
    tc = FF_CHUNK

    def chunk(col0):
        cols = slice(col0, col0 + tc)
        return _conv3(_up(h2_scr, wu_ref[:, cols]), cw_ref[:, cols], cb_ref[:, cols], seq_len)

    for c0 in range(0, D_FF, tc):
        act_scr[:, c0:c0 + tc] = _gated_silu(chunk(c0), chunk(D_FF + c0))
    _down_proj(act_scr, wd_ref, mod, gpost2_ref, out_ref, out_ref)


def _tail_seq_call(a, b, x2d, mod3, mod_row, g_post, g_ffn, g_post2, w_out, w_up, conv_w, conv_b, w_down,
                   seq_len, name):
    t = x2d.shape[0]
    tm = max(SEQ_TAIL_TILE, seq_len)
    big = tm > SEQ_TAIL_TILE
    vmem_limit = TAIL_VMEM_LIMIT if big else VMEM_LIMIT
    x_mode = dict(pipeline_mode=pl.Buffered(1)) if big else {}
    row = lambda i: (i, 0)
    const = lambda i: (0, 0)
    return pl.pallas_call(
        functools.partial(_tail_seq_kernel, seq_len=seq_len),
        grid=(t // tm,),
        in_specs=[
            pl.BlockSpec((tm, W_A), row),
            pl.BlockSpec((tm, W_B), row),
            pl.BlockSpec((tm, D_MODEL), row, **x_mode),
            pl.BlockSpec((1, 1, 6 * D_MODEL), lambda i: (mod_row(i), 0, 0)),
            pl.BlockSpec((1, D_MODEL), const),
            pl.BlockSpec((1, D_MODEL), const),
            pl.BlockSpec((1, D_MODEL), const),
            _resident((W_A + W_B, D_MODEL)),
            pl.BlockSpec(memory_space=pltpu.VMEM),
            _resident((3, 2 * D_FF)),
            _resident((1, 2 * D_FF)),
            pl.BlockSpec(memory_space=pltpu.VMEM),
        ],
        out_specs=pl.BlockSpec((tm, D_MODEL), row),
        out_shape=jax.ShapeDtypeStruct((t, D_MODEL), _F32),
        scratch_shapes=[
            pltpu.VMEM((W_A + W_B, D_MODEL), _BF16),
            pltpu.VMEM((tm, D_MODEL), _BF16),
            pltpu.VMEM((tm, D_FF), _BF16),
        ],
        compiler_params=pltpu.CompilerParams(
            dimension_semantics=("arbitrary",), vmem_limit_bytes=vmem_limit),
        name=name,
    )(a, b, x2d, mod3, g_post.reshape(1, D_MODEL), g_ffn.reshape(1, D_MODEL), g_post2.reshape(1, D_MODEL),
      w_out, w_up, conv_w, conv_b.reshape(1, 2 * D_FF), w_down)


def _rope_tables(seq):
    n = ROPE_HALF
    t = np.arange(seq)
    lane = np.arange(HEAD_DIM)
    pos = np.where(lane[None, :] < HEAD_DIM // 2, (t // GRID_W)[:, None], (t % GRID_W)[:, None])
    inv = 1.0 / (ROPE_THETA ** (np.arange(n, dtype=np.float64) / n))
    ang = pos.astype(np.float64) * inv[lane % n][None, :]
    sign = np.where((lane & n) == 0, -1.0, 1.0)[None, :]
    cos = np.tile(np.cos(ang), (1, LANES // HEAD_DIM)).astype(np.float32)
    sin = np.tile(np.sin(ang) * sign, (1, LANES // HEAD_DIM)).astype(np.float32)
    return jnp.asarray(cos), jnp.asarray(sin)


def _to_head_dim_token(cache_l):
    b, l, h, d = cache_l.shape
    return jnp.transpose(cache_l, (0, 2, 3, 1)).reshape(b, h * d, l)


def _from_head_dim_token(kv_t, heads):
    b, _, l = kv_t.shape
    return jnp.transpose(kv_t.reshape(b, 1, heads, HEAD_DIM, l), (0, 1, 4, 2, 3))


def kernel(x_prompt, x_sample, cache_a_k, cache_a_v, cache_b_k, cache_b_v, c, c_ctx, w_mod, b_mod,
           g_mix_pre, g_mix_post, g_ffn_pre, g_ffn_post, w_in, rpb_a, sink_b, g_grp_a, g_grp_b,
           w_out, w_up, conv_w, conv_b, w_down):
    bp, lp, _ = x_prompt.shape
    bs, ts, _ = x_sample.shape
    assert w_in.shape[0] == 1
    xp = x_prompt.reshape(bp * lp, D_MODEL)
    xs = x_sample.reshape(bs * ts, D_MODEL)
    cond8 = jnp.concatenate([c_ctx[None], c, jnp.zeros((8 - 1 - bs, D_MODEL), _F32)], axis=0)
    cos_tab, sin_tab = _rope_tables(ts)

    l = 0
    mod3 = _mod_call(cond8, w_mod[l], b_mod[l]).reshape(8, 1, 6 * D_MODEL)
    g_b_p = g_grp_b[l].reshape(KV_B, G_B, HEAD_DIM).transpose(1, 0, 2).reshape(W_B)
    rpb_pad = jnp.pad(rpb_a[l], ((0, 0), (0, 0), (0, LANES - (2 * NA_COLS - 1))))

    prompt_row = lambda i: 0
    assert ts >= SEQ_TAIL_TILE
    sample_row = lambda i: 1 + i

    (q_p, kat_p, vat_p, kbt_p, vbt_p), (q_s, ka_s, va_s, kb_s, vb_s) = _inproj_call(
        xp, xs, mod3, g_mix_pre[l], w_in[l], (cos_tab, sin_tab), lp, ts)

    a_p, b_p, w_up_b, w_down_b = _ctx_attn_call(sink_b[l], q_p, kat_p, vat_p, kbt_p, vbt_p, g_grp_a[l], g_b_p,
                                                w_up[l], w_down[l], lp)
    y_p = _tail_seq_call(a_p, b_p, xp, mod3, prompt_row, g_mix_post[l], g_ffn_pre[l], g_ffn_post[l], w_out[l],
                         w_up_b, conv_w[l], conv_b[l], w_down_b, lp, "tail_prompt")

    a_s = _na_call(q_s, ka_s, va_s, _to_head_dim_token(cache_a_k[:, l]), _to_head_dim_token(cache_a_v[:, l]),
                   rpb_pad, g_grp_a[l], bs, ts)
    b_s = _win_call(sink_b[l], q_s, kb_s, vb_s, _to_head_dim_token(cache_b_k[:, l]),
                    _to_head_dim_token(cache_b_v[:, l]), g_b_p, bs, ts)
    y_s = _tail_seq_call(a_s, b_s, xs, mod3, sample_row, g_mix_post[l], g_ffn_pre[l], g_ffn_post[l], w_out[l],
                         w_up_b, conv_w[l], conv_b[l], w_down_b, ts, "tail_sample")

    return (y_p.reshape(bp, lp, D_MODEL), y_s.reshape(bs, ts, D_MODEL),
            _from_head_dim_token(kat_p, H_A), _from_head_dim_token(vat_p, H_A),
            _from_head_dim_token(kbt_p, KV_B), _from_head_dim_token(vbt_p, KV_B))
```

```python
import functools

import numpy as np
import jax
import jax.numpy as jnp
from jax import lax
from jax.experimental import pallas as pl
from jax.experimental.pallas import tpu as pltpu

D_MODEL = 1024
HEAD_DIM = 64
H_A = 8
H_B = 8
KV_B = 2
G_B = H_B // KV_B
W_A = H_A * HEAD_DIM
W_B = H_B * HEAD_DIM
KV_W_B = KV_B * HEAD_DIM
IN_WIDTH = 3 * W_A + W_B + 2 * KV_W_B
GRID_W = 64
NA_ROWS = 8
NA_COLS = 16
SWA_WINDOW = 128
BLOCK = 128
D_FF = 2816
ROPE_THETA = 10000.0
EPS = 1e-6
NEG_INF = -1e30
Q_SCALE = HEAD_DIM ** -0.5
ROPE_HALF = HEAD_DIM // 4

LANES = 128
FF_CHUNK = 256
TOKEN_TILE = 512
FFN_ROW_BLOCK = 256
DOWN_ROW_BLOCK = 256
SEQ_TAIL_TILE = 512
PROJ_ROW_BLOCK = 256
MOD_STREAMS = 2
MOD_ROW_BLOCK = 128
NA_QROWS = 4
CAST_COLS = 512
VMEM_LIMIT = 48 * 1024 * 1024
TAIL_VMEM_LIMIT = 56 * 1024 * 1024

C_QA, C_KA, C_VA, C_QB, C_KB, C_VB = 0, W_A, 2 * W_A, 3 * W_A, 3 * W_A + W_B, 3 * W_A + W_B + KV_W_B

_QB_HEAD_ORDER = tuple(kv * G_B + g for g in range(G_B) for kv in range(KV_B))

_BF16 = jnp.bfloat16
_F32 = jnp.float32


def _dot(a, b):
    return jnp.dot(a, b, preferred_element_type=_F32)


def _dot_nt(a, b):
    return lax.dot_general(a, b, (((1,), (1,)), ((), ())), preferred_element_type=_F32)


def _rms(x, g):
    var = jnp.mean(x * x, axis=-1, keepdims=True)
    return x * lax.rsqrt(var + EPS) * g


def _stack_heads(q2, lo):
    zero = jnp.zeros_like(q2)
    return jnp.concatenate([jnp.where(lo, q2, zero), jnp.where(lo, zero, q2)], axis=0)


def _unstack_heads(o2, lo):
    m = o2.shape[0] // 2
    return jnp.where(lo, o2[:m], o2[m:])


def _softmax_pv(scores, pv, sink=None):
    m = scores[0].max(axis=-1, keepdims=True)
    for s in scores[1:]:
        m = jnp.maximum(m, s.max(axis=-1, keepdims=True))
    if sink is not None:
        m = jnp.maximum(m, sink)
    denom = None
    out = None
    for s, f in zip(scores, pv):
        e = jnp.exp(s - m)
        l = e.sum(axis=-1, keepdims=True)
        o = f(e.astype(_BF16))
        denom = l if denom is None else denom + l
        out = o if out is None else out + o
    if sink is not None:
        denom = denom + jnp.exp(sink - m)
    return out / denom


def _resident(shape):
    return pl.BlockSpec(shape, lambda *_: (0,) * len(shape), pipeline_mode=pl.Buffered(1))


def _mod_kernel(c_ref, *refs):
    w_refs, b_ref, o_ref = refs[:-2], refs[-2], refs[-1]
    i = pl.program_id(0)

    @pl.when(i == 0)
    def _():
        o_ref[...] = jnp.broadcast_to(b_ref[...], o_ref.shape)

    acc = o_ref[...]
    for j, w_ref in enumerate(w_refs):
        c = c_ref[i * len(w_refs) + j]
        s = (c * jax.nn.sigmoid(c)).astype(_BF16)
        acc = acc + _dot(s, w_ref[...].astype(_BF16))
    o_ref[...] = acc


def _mod_call(cond8, w_mod, b_mod):
    n = w_mod.shape[1]
    kb = MOD_ROW_BLOCK
    nblk = D_MODEL // kb
    cond_blocks = cond8.reshape(8, nblk, kb).transpose(1, 0, 2)
    return pl.pallas_call(
        _mod_kernel,
        grid=(nblk // MOD_STREAMS,),
        in_specs=[pl.BlockSpec((nblk, 8, kb), lambda i: (0, 0, 0))]
        + [pl.BlockSpec((kb, n), lambda i, j=j: (i * MOD_STREAMS + j, 0)) for j in range(MOD_STREAMS)]
        + [pl.BlockSpec((1, n), lambda i: (0, 0))],
        out_specs=pl.BlockSpec((8, n), lambda i: (0, 0)),
        out_shape=jax.ShapeDtypeStruct((8, n), _F32),
        compiler_params=pltpu.CompilerParams(dimension_semantics=("arbitrary",), vmem_limit_bytes=VMEM_LIMIT),
        name="mod",
    )(cond_blocks, *([w_mod] * MOD_STREAMS), b_mod.reshape(1, n))


def _prepare_in_weight(w_ref, w_scr):
    for c0 in list(range(0, C_QB, CAST_COLS)) + [C_KB]:
        n = min(CAST_COLS, IN_WIDTH - c0)
        w_scr[:, c0:c0 + n] = w_ref[:, c0:c0 + n].astype(_BF16)
    lo = lax.broadcasted_iota(jnp.int32, (D_MODEL, LANES), 1) < HEAD_DIM
    for j in range(G_B):
        h_lo, h_hi = _QB_HEAD_ORDER[2 * j], _QB_HEAD_ORDER[2 * j + 1]

        def head_at(h, want_hi):
            blk = w_ref[:, C_QB + LANES * (h // 2):C_QB + LANES * (h // 2 + 1)]
            return blk if (h % 2 == 1) == want_hi else pltpu.roll(blk, HEAD_DIM, 1)

        w_scr[:, C_QB + LANES * j:C_QB + LANES * (j + 1)] = jnp.where(
            lo, head_at(h_lo, False), head_at(h_hi, True)).astype(_BF16)


def _rope(z, cos, sin):
    lane = lax.broadcasted_iota(jnp.int32, z.shape, 1)
    partner = jnp.where((lane & ROPE_HALF) == 0,
                        pltpu.roll(z, LANES - ROPE_HALF, 1), pltpu.roll(z, ROPE_HALF, 1))
    return z * cos + partner * sin


def _inproj_rows(x_ref, mod, g_ref, w_scr, rope_refs, outs):
    q_ref, ka_ref, va_ref, kb_ref, vb_ref = outs
    kv_transposed = rope_refs is None
    sh1 = mod[:, 0:D_MODEL]
    sc1 = mod[:, D_MODEL:2 * D_MODEL]
    rb = PROJ_ROW_BLOCK

    for r in range(0, x_ref.shape[0], rb):
        rows = slice(r, r + rb)
        h = (_rms(x_ref[rows, :], g_ref[...]) * (1 + sc1) + sh1).astype(_BF16)

        def proj(c0, n):
            return _dot(h, w_scr[:, c0:c0 + n])

        def put_kv(ref, z):
            if not kv_transposed:
                ref[rows, :] = z.astype(ref.dtype)
                return
            l = ref.shape[2]
            for t0 in range(0, rb, l):
                for c0 in range(0, z.shape[1], LANES):
                    ref[(r + t0) // l, c0:c0 + LANES, :] = z[t0:t0 + l, c0:c0 + LANES].T.astype(ref.dtype)

        q_ref[rows, 0:W_A] = (proj(C_QA, W_A) * Q_SCALE).astype(q_ref.dtype)
        put_kv(ka_ref, proj(C_KA, W_A))
        put_kv(va_ref, proj(C_VA, W_A))
        zkv = proj(C_KB, 2 * KV_W_B)
        zkb, zvb = zkv[:, :KV_W_B], zkv[:, KV_W_B:]
        zqb = proj(C_QB, W_B)
        put_kv(vb_ref, zvb)
        if rope_refs is not None:
            cos = rope_refs[0][rows, :]
            sin = rope_refs[1][rows, :]
            for j in range(W_B // LANES):
                q_ref[rows, W_A + LANES * j:W_A + LANES * (j + 1)] = (
                    _rope(zqb[:, LANES * j:LANES * (j + 1)], cos, sin) * Q_SCALE).astype(q_ref.dtype)
            put_kv(kb_ref, _rope(zkb, cos, sin))
        else:
            q_ref[rows, W_A:W_A + W_B] = (zqb * Q_SCALE).astype(q_ref.dtype)
            put_kv(kb_ref, zkb)


def _inproj_kernel(xp_ref, xs_ref, mod_ref, g_ref, w_ref, cos_ref, sin_ref, *refs, prompt_steps):
    outs_p, outs_s, w_scr = refs[0:5], refs[5:10], refs[10]
    i = pl.program_id(0)

    @pl.when(i == 0)
    def _():
        _prepare_in_weight(w_ref, w_scr)

    mod = mod_ref[0]

    @pl.when(i < prompt_steps)
    def _():
        _inproj_rows(xp_ref, mod, g_ref, w_scr, None, outs_p)

    @pl.when(i >= prompt_steps)
    def _():
        _inproj_rows(xs_ref, mod, g_ref, w_scr, (cos_ref, sin_ref), outs_s)


def _inproj_call(xp, xs, mod3, g_pre, w_in, rope_tabs, lp, ts):
    tm = TOKEN_TILE
    tp, tsamp = xp.shape[0], xs.shape[0]
    n_p, n_s = tp // tm, tsamp // tm
    assert n_p * tm == tp and n_s * tm == tsamp and tm % lp == 0 and ts % tm == 0
    tiles_per_seq = ts // tm
    p_tile = lambda i: jnp.minimum(i, n_p - 1)
    s_tile = lambda i: jnp.maximum(i - n_p, 0)
    mod_row = lambda i: jnp.where(i < n_p, 0, 1 + s_tile(i) // tiles_per_seq)
    nseq = tm // lp
    in_specs = [
        pl.BlockSpec((tm, D_MODEL), lambda i: (p_tile(i), 0)),
        pl.BlockSpec((tm, D_MODEL), lambda i: (s_tile(i), 0)),
        pl.BlockSpec((1, 1, 6 * D_MODEL), lambda i: (mod_row(i), 0, 0)),
        pl.BlockSpec((1, D_MODEL), lambda i: (0, 0)),
        _resident((D_MODEL, IN_WIDTH)),
        pl.BlockSpec((tm, LANES), lambda i: (s_tile(i) % tiles_per_seq, 0)),
        pl.BlockSpec((tm, LANES), lambda i: (s_tile(i) % tiles_per_seq, 0)),
    ]
    kvt_shape = lambda w: jax.ShapeDtypeStruct((tp // lp, w, lp), _F32)
    kvt_spec = lambda w: pl.BlockSpec((nseq, w, lp), lambda i: (p_tile(i), 0, 0))
    kv_shape = lambda w: jax.ShapeDtypeStruct((tsamp, w), _BF16)
    kv_spec = lambda w: pl.BlockSpec((tm, w), lambda i: (s_tile(i), 0))
    out_shape = (jax.ShapeDtypeStruct((tp, W_A + W_B), _BF16),
                 kvt_shape(W_A), kvt_shape(W_A), kvt_shape(KV_W_B), kvt_shape(KV_W_B),
                 jax.ShapeDtypeStruct((tsamp, W_A + W_B), _BF16),
                 kv_shape(W_A), kv_shape(W_A), kv_shape(KV_W_B), kv_shape(KV_W_B))
    out_specs = (pl.BlockSpec((tm, W_A + W_B), lambda i: (p_tile(i), 0)),
                 kvt_spec(W_A), kvt_spec(W_A), kvt_spec(KV_W_B), kvt_spec(KV_W_B),
                 pl.BlockSpec((tm, W_A + W_B), lambda i: (s_tile(i), 0)),
                 kv_spec(W_A), kv_spec(W_A), kv_spec(KV_W_B), kv_spec(KV_W_B))
    outs = pl.pallas_call(
        functools.partial(_inproj_kernel, prompt_steps=n_p),
        grid=(n_p + n_s,),
        in_specs=in_specs,
        out_specs=out_specs,
        out_shape=out_shape,
        scratch_shapes=[pltpu.VMEM((D_MODEL, IN_WIDTH), _BF16)],
        compiler_params=pltpu.CompilerParams(
            dimension_semantics=("arbitrary",), vmem_limit_bytes=VMEM_LIMIT),
        name="inproj",
    )(xp, xs, mod3, g_pre.reshape(1, D_MODEL), w_in, *rope_tabs)
    return outs[:5], outs[5:]


def _ctx_attn_kernel(sink_ref, q_ref, ka_ref, va_ref, kb_ref, vb_ref, ga_ref, gb_ref, wu_ref, wd_ref,
                     a_ref, b_ref, wu_out, wd_out):
    wu_out[...] = wu_ref[...].astype(wu_out.dtype)
    wd_out[...] = wd_ref[...].astype(wd_out.dtype)
    l = q_ref.shape[0]
    lo = lax.broadcasted_iota(jnp.int32, (l, LANES), 1) < HEAD_DIM
    row_lo = lax.broadcasted_iota(jnp.int32, (2 * l, 1), 0) < l
    oa = []
    for p in range(W_A // LANES):
        sl = slice(LANES * p, LANES * (p + 1))
        qs = _stack_heads(q_ref[:, sl], lo)
        s = _dot(qs, ka_ref[0, sl, :].astype(_BF16))
        vt = va_ref[0, sl, :].astype(_BF16)
        oa.append(_unstack_heads(_softmax_pv([s], [lambda e, vt=vt: _dot_nt(e, vt)]), lo))
    a_ref[...] = _rms(jnp.concatenate(oa, axis=1), ga_ref[...]).astype(a_ref.dtype)
    kb = kb_ref[0].astype(_BF16)
    vb = vb_ref[0].astype(_BF16)
    ob = []
    for j in range(G_B):
        qs = _stack_heads(q_ref[:, W_A + LANES * j:W_A + LANES * (j + 1)], lo)
        s = _dot(qs, kb)
        sink = jnp.where(row_lo, sink_ref[0, j], sink_ref[1, j])
        ob.append(_unstack_heads(_softmax_pv([s], [lambda e: _dot_nt(e, vb)], sink), lo))
    b_ref[...] = _rms(jnp.concatenate(ob, axis=1), gb_ref[...]).astype(b_ref.dtype)


def _ctx_attn_call(sink, q, ka, va, kb, vb, g_a, g_b, w_up, w_down, seq):
    t = q.shape[0]
    steps = t // seq
    ru, rd = w_up.shape[0] // steps, w_down.shape[0] // steps
    assert ru * steps == w_up.shape[0] and rd * steps == w_down.shape[0] and ru % 16 == 0 and rd % 16 == 0
    row = lambda b: (b, 0)
    seq3 = lambda b: (b, 0, 0)
    const = lambda b: (0, 0)
    return pl.pallas_call(
        _ctx_attn_kernel,
        grid=(steps,),
        in_specs=[
            pl.BlockSpec(memory_space=pltpu.SMEM),
            pl.BlockSpec((seq, W_A + W_B), row),
            pl.BlockSpec((1, W_A, seq), seq3),
            pl.BlockSpec((1, W_A, seq), seq3),
            pl.BlockSpec((1, KV_W_B, seq), seq3),
            pl.BlockSpec((1, KV_W_B, seq), seq3),
            pl.BlockSpec((1, W_A), const),
            pl.BlockSpec((1, W_B), const),
            pl.BlockSpec((ru, w_up.shape[1]), row),
            pl.BlockSpec((rd, w_down.shape[1]), row),
        ],
        out_specs=(pl.BlockSpec((seq, W_A), row), pl.BlockSpec((seq, W_B), row),
                   pl.BlockSpec((ru, w_up.shape[1]), row), pl.BlockSpec((rd, w_down.shape[1]), row)),
        out_shape=(jax.ShapeDtypeStruct((t, W_A), _BF16), jax.ShapeDtypeStruct((t, W_B), _BF16),
                   jax.ShapeDtypeStruct(w_up.shape, _BF16), jax.ShapeDtypeStruct(w_down.shape, _BF16)),
        compiler_params=pltpu.CompilerParams(
            dimension_semantics=("parallel",), vmem_limit_bytes=VMEM_LIMIT),
        name="ctx_attn",
    )(sink, q, ka, va, kb, vb, g_a.reshape(1, W_A), g_b.reshape(1, W_B), w_up, w_down)


def _na_plan(rows):
    kr = min(NA_ROWS, rows)
    win = kr + NA_QROWS
    row_start = lambda r: min(max(r - kr // 2, 0), rows - kr)
    blocks, classes = [], []
    for r0 in range(0, rows, NA_QROWS):
        ws = min(row_start(r0), rows - win)
        keys = []
        for r in range(r0, r0 + NA_QROWS):
            key = (ws - r + NA_ROWS - 1, row_start(r) - ws)
            if key not in classes:
                classes.append(key)
            keys.append(classes.index(key))
        blocks.append((r0, ws, keys))
    return kr, win, blocks, classes


def _na_kernel(q_ref, k_ref, v_ref, ck_ref, cv_ref, rpb_ref, ga_ref, a_ref, bias_ref, o_scr):
    rows = q_ref.shape[0] // GRID_W
    kr, win, blocks, classes = _na_plan(rows)
    m = NA_QROWS * GRID_W
    lane = lax.broadcasted_iota(jnp.int32, (GRID_W, LANES), 1)
    lo = lane < HEAD_DIM
    lo_q = lax.broadcasted_iota(jnp.int32, (m, LANES), 1) < HEAD_DIM
    cq = lax.broadcasted_iota(jnp.int32, (GRID_W, LANES), 0)
    ck = lane & (GRID_W - 1)
    cs = jnp.clip(cq - NA_COLS // 2, 0, GRID_W - NA_COLS)
    valid = (ck >= cs) & (ck < cs + NA_COLS)
    neg = jnp.full((GRID_W, LANES), NEG_INF, _F32)
    n_drow = 2 * NA_ROWS - 1

    for p in range(W_A // LANES):
        sl = slice(LANES * p, LANES * (p + 1))
        for hh in range(2):
            t_lo, t_hi = [], []
            for d in range(n_drow):
                base = jnp.broadcast_to(rpb_ref[2 * p + hh, d:d + 1, :], (GRID_W, LANES))
                t_lo.append(pltpu.roll(base, LANES - (NA_COLS - 1), 1, stride=1, stride_axis=0))
                t_hi.append(pltpu.roll(base, GRID_W - (NA_COLS - 1), 1, stride=1, stride_axis=0))
            tiles = {}
            for ci, (d_first, off) in enumerate(classes):
                for j in range(win // 2):
                    d_of = lambda i: d_first + i if off <= i < off + kr else None
                    key = (d_of(2 * j), d_of(2 * j + 1))
                    if key not in tiles:
                        left = neg if key[0] is None else t_lo[key[0]]
                        right = neg if key[1] is None else t_hi[key[1]]
                        tiles[key] = jnp.where(valid, jnp.where(lo, left, right), NEG_INF)
                    bias_ref[ci, GRID_W * hh:GRID_W * (hh + 1), LANES * j:LANES * (j + 1)] = tiles[key]

        ckt = ck_ref[0, sl, :].astype(_BF16)
        cvt = cv_ref[0, sl, :].astype(_BF16)

        for r0, ws, keys in blocks:
            q0, k0 = r0 * GRID_W, ws * GRID_W
            qs = _stack_heads(q_ref[q0:q0 + m, sl], lo_q)
            kw = k_ref[k0:k0 + win * GRID_W, sl]
            vw = v_ref[k0:k0 + win * GRID_W, sl]
            bias = jnp.concatenate(
                [bias_ref[ci, GRID_W * hh:GRID_W * (hh + 1), :] for hh in range(2) for ci in keys], axis=0)
            s_nb = _dot_nt(qs, kw) + bias
            s_cx = _dot(qs, ckt)
            o2 = _softmax_pv([s_nb, s_cx], [lambda e, vw=vw: _dot(e, vw), lambda e: _dot_nt(e, cvt)])
            o_scr[q0:q0 + m, sl] = _unstack_heads(o2, lo_q)

    a_ref[...] = _rms(o_scr[...], ga_ref[...]).astype(a_ref.dtype)


def _na_call(q, k, v, ckt, cvt, rpb_pad, g_a, nb, seq):
    past = ckt.shape[2]
    _, win, _, classes = _na_plan(seq // GRID_W)
    blk = lambda b: (b, 0)
    return pl.pallas_call(
        _na_kernel,
        grid=(nb,),
        in_specs=[
            pl.BlockSpec((seq, W_A), blk),
            pl.BlockSpec((seq, W_A), blk),
            pl.BlockSpec((seq, W_A), blk),
            pl.BlockSpec((1, W_A, past), lambda b: (b, 0, 0)),
            pl.BlockSpec((1, W_A, past), lambda b: (b, 0, 0)),
            pl.BlockSpec((H_A, 2 * NA_ROWS - 1, LANES), lambda b: (0, 0, 0)),
            pl.BlockSpec((1, W_A), lambda b: (0, 0)),
        ],
        out_specs=pl.BlockSpec((seq, W_A), blk),
        out_shape=jax.ShapeDtypeStruct((nb * seq, W_A), _BF16),
        scratch_shapes=[pltpu.VMEM((len(classes), 2 * GRID_W, win * GRID_W), _F32),
                        pltpu.VMEM((seq, W_A), _F32)],
        compiler_params=pltpu.CompilerParams(
            dimension_semantics=("parallel",), vmem_limit_bytes=VMEM_LIMIT),
        name="na_attn",
    )(q, k, v, ckt, cvt, rpb_pad, g_a.reshape(1, W_A))


def _win_kernel(sink_ref, q_ref, k_ref, v_ref, ck_ref, cv_ref, gb_ref, b_ref, o_scr):
    t = q_ref.shape[0]
    nblk = t // BLOCK
    band = 3 * BLOCK
    lo = lax.broadcasted_iota(jnp.int32, (BLOCK, LANES), 1) < HEAD_DIM
    row_lo = lax.broadcasted_iota(jnp.int32, (2 * BLOCK, 1), 0) < BLOCK
    qi = lax.broadcasted_iota(jnp.int32, (2 * BLOCK, band), 0) & (BLOCK - 1)
    kj = lax.broadcasted_iota(jnp.int32, (2 * BLOCK, band), 1)
    rel = qi - kj
    ckt = ck_ref[0].astype(_BF16)
    cvt = cv_ref[0].astype(_BF16)

    starts = [min(max((n - 1) * BLOCK, 0), t - band) for n in range(nblk)]
    masks = {off: jnp.where(jnp.abs(rel + off) <= SWA_WINDOW, 0.0, NEG_INF)
             for off in sorted({n * BLOCK - starts[n] for n in range(nblk)})}

    for j in range(G_B):
        sl = slice(LANES * j, LANES * (j + 1))
        sink = jnp.where(row_lo, sink_ref[0, j], sink_ref[1, j])
        for n in range(nblk):
            q0, k0 = n * BLOCK, starts[n]
            qs = _stack_heads(q_ref[q0:q0 + BLOCK, sl], lo)
            s = _dot_nt(qs, k_ref[k0:k0 + band, :]) + masks[q0 - k0]
            s_cx = _dot(qs, ckt)
            vband = v_ref[k0:k0 + band, :]
            o2 = _softmax_pv([s, s_cx], [lambda e, vband=vband: _dot(e, vband), lambda e: _dot_nt(e, cvt)],
                             sink)
            o_scr[q0:q0 + BLOCK, sl] = _unstack_heads(o2, lo)

    b_ref[...] = _rms(o_scr[...], gb_ref[...]).astype(b_ref.dtype)


def _win_call(sink, q, k, v, ckt, cvt, g_b, nb, seq):
    past = ckt.shape[2]
    return pl.pallas_call(
        _win_kernel,
        grid=(nb,),
        in_specs=[
            pl.BlockSpec(memory_space=pltpu.SMEM),
            pl.BlockSpec((seq, W_B), lambda b: (b, W_A // W_B)),
            pl.BlockSpec((seq, KV_W_B), lambda b: (b, 0)),
            pl.BlockSpec((seq, KV_W_B), lambda b: (b, 0)),
            pl.BlockSpec((1, KV_W_B, past), lambda b: (b, 0, 0)),
            pl.BlockSpec((1, KV_W_B, past), lambda b: (b, 0, 0)),
            pl.BlockSpec((1, W_B), lambda b: (0, 0)),
        ],
        out_specs=pl.BlockSpec((seq, W_B), lambda b: (b, 0)),
        out_shape=jax.ShapeDtypeStruct((nb * seq, W_B), _BF16),
        scratch_shapes=[pltpu.VMEM((seq, W_B), _F32)],
        compiler_params=pltpu.CompilerParams(
            dimension_semantics=("parallel",), vmem_limit_bytes=VMEM_LIMIT),
        name="win_attn",
    )(sink, q, k, v, ckt, cvt, g_b.reshape(1, W_B))


def _prepare_out_weight(w_ref, w_scr):
    for r0 in range(0, W_A, CAST_COLS):
        w_scr[r0:r0 + CAST_COLS, :] = w_ref[r0:r0 + CAST_COLS, :].astype(_BF16)
    for i, h in enumerate(_QB_HEAD_ORDER):
        w_scr[W_A + HEAD_DIM * i:W_A + HEAD_DIM * (i + 1), :] = (
            w_ref[W_A + HEAD_DIM * h:W_A + HEAD_DIM * (h + 1), :].astype(_BF16))


def _zero_row(x, row):
    r0 = row - row % 8
    sub = lax.broadcasted_iota(jnp.int32, (8, 1), 0)
    slab = jnp.where(sub == row % 8, 0.0, x[r0:r0 + 8])
    parts = ([x[:r0]] if r0 else []) + [slab] + ([x[r0 + 8:]] if r0 + 8 < x.shape[0] else [])
    return jnp.concatenate(parts, axis=0)


def _conv3(u, cw, cb, seq_len):
    tm = u.shape[0]
    prev = pltpu.roll(u, 1, 0)
    nxt = pltpu.roll(u, tm - 1, 0)
    for s0 in range(0, tm, seq_len):
        prev = _zero_row(prev, s0)
        nxt = _zero_row(nxt, s0 + seq_len - 1)
    return prev * cw[0:1, :] + u * cw[1:2, :] + nxt * cw[2:3, :] + cb


def _gated_silu(gate, val):
    half = 0.5 * gate
    return ((half + half * jnp.tanh(half)) * val).astype(_BF16)


def _up(h2_ref, w):
    tm = h2_ref.shape[0]
    return jnp.concatenate(
        [_dot(h2_ref[r:r + FFN_ROW_BLOCK, :], w) for r in range(0, tm, FFN_ROW_BLOCK)], axis=0)


def _out_proj(a_ref, b_ref, x_ref, mod, gpost_ref, gffn_ref, wo_scr, x1_ref, h2_ref):
    gt1 = mod[:, 2 * D_MODEL:3 * D_MODEL]
    sh2 = mod[:, 3 * D_MODEL:4 * D_MODEL]
    sc2 = mod[:, 4 * D_MODEL:5 * D_MODEL]
    for r in range(0, x_ref.shape[0], FFN_ROW_BLOCK):
        rows = slice(r, r + FFN_ROW_BLOCK)
        y = _dot(a_ref[rows, :], wo_scr[0:W_A, :]) + _dot(b_ref[rows, :], wo_scr[W_A:W_A + W_B, :])
        x1 = x_ref[rows, :] + gt1 * _rms(y, gpost_ref[...])
        x1_ref[rows, :] = x1
        h2_ref[rows, :] = (_rms(x1, gffn_ref[...]) * (1 + sc2) + sh2).astype(h2_ref.dtype)


def _down_proj(act_ref, wd_ref, mod, gpost2_ref, x1_ref, out_ref):
    gt2 = mod[:, 5 * D_MODEL:6 * D_MODEL]
    for r in range(0, act_ref.shape[0], DOWN_ROW_BLOCK):
        rows = slice(r, r + DOWN_ROW_BLOCK)
        y = _dot(act_ref[rows, :], wd_ref[...])
        out_ref[rows, :] = x1_ref[rows, :] + gt2 * _rms(y, gpost2_ref[...])


def _tail_seq_kernel(a_ref, b_ref, x_ref, mod_ref, gpost_ref, gffn_ref, gpost2_ref, wo_ref, wu_ref,
                     cw_ref, cb_ref, wd_ref, out_ref, wo_scr, h2_scr, act_scr, *, seq_len):
    mod = mod_ref[0]

    @pl.when(pl.program_id(0) == 0)
    def _():
        _prepare_out_weight(wo_ref, wo_scr)

    _out_proj(a_ref, b_ref, x_ref, mod, gpost_ref, gffn_ref, wo_scr, out_ref, h2_scr)
    tc = FF_CHUNK

    def chunk(col0):
        cols = slice(col0, col0 + tc)
        return _conv3(_up(h2_scr, wu_ref[:, cols]), cw_ref[:, cols], cb_ref[:, cols], seq_len)

    for c0 in range(0, D_FF, tc):
        act_scr[:, c0:c0 + tc] = _gated_silu(chunk(c0), chunk(D_FF + c0))
    _down_proj(act_scr, wd_ref, mod, gpost2_ref, out_ref, out_ref)


def _tail_seq_call(a, b, x2d, mod3, mod_row, g_post, g_ffn, g_post2, w_out, w_up, conv_w, conv_b, w_down,
                   seq_len, name):
    t = x2d.shape[0]
    tm = max(SEQ_TAIL_TILE, seq_len)
    big = tm > SEQ_TAIL_TILE
    vmem_limit = TAIL_VMEM_LIMIT if big else VMEM_LIMIT
    x_mode = dict(pipeline_mode=pl.Buffered(1)) if big else {}
    row = lambda i: (i, 0)
    const = lambda i: (0, 0)
    return pl.pallas_call(
        functools.partial(_tail_seq_kernel, seq_len=seq_len),
        grid=(t // tm,),
        in_specs=[
            pl.BlockSpec((tm, W_A), row),
            pl.BlockSpec((tm, W_B), row),
            pl.BlockSpec((tm, D_MODEL), row, **x_mode),
            pl.BlockSpec((1, 1, 6 * D_MODEL), lambda i: (mod_row(i), 0, 0)),
            pl.BlockSpec((1, D_MODEL), const),
            pl.BlockSpec((1, D_MODEL), const),
            pl.BlockSpec((1, D_MODEL), const),
            _resident((W_A + W_B, D_MODEL)),
            _resident((D_MODEL, 2 * D_FF)),
            _resident((3, 2 * D_FF)),
            _resident((1, 2 * D_FF)),
            _resident((D_FF, D_MODEL)),
        ],
        out_specs=pl.BlockSpec((tm, D_MODEL), row),
        out_shape=jax.ShapeDtypeStruct((t, D_MODEL), _F32),
        scratch_shapes=[
            pltpu.VMEM((W_A + W_B, D_MODEL), _BF16),
            pltpu.VMEM((tm, D_MODEL), _BF16),
            pltpu.VMEM((tm, D_FF), _BF16),
        ],
        compiler_params=pltpu.CompilerParams(
            dimension_semantics=("arbitrary",), vmem_limit_bytes=vmem_limit),
        name=name,
    )(a, b, x2d, mod3, g_post.reshape(1, D_MODEL), g_ffn.reshape(1, D_MODEL), g_post2.reshape(1, D_MODEL),
      w_out, w_up, conv_w, conv_b.reshape(1, 2 * D_FF), w_down)


def _rope_tables(seq):
    n = ROPE_HALF
    t = np.arange(seq)
    lane = np.arange(HEAD_DIM)
    pos = np.where(lane[None, :] < HEAD_DIM // 2, (t // GRID_W)[:, None], (t % GRID_W)[:, None])
    inv = 1.0 / (ROPE_THETA ** (np.arange(n, dtype=np.float64) / n))
    ang = pos.astype(np.float64) * inv[lane % n][None, :]
    sign = np.where((lane & n) == 0, -1.0, 1.0)[None, :]
    cos = np.tile(np.cos(ang), (1, LANES // HEAD_DIM)).astype(np.float32)
    sin = np.tile(np.sin(ang) * sign, (1, LANES // HEAD_DIM)).astype(np.float32)
    return jnp.asarray(cos), jnp.asarray(sin)


def _to_head_dim_token(cache_l):
    b, l, h, d = cache_l.shape
    return jnp.transpose(cache_l, (0, 2, 3, 1)).reshape(b, h * d, l)


def _from_head_dim_token(kv_t, heads):
    b, _, l = kv_t.shape
    return jnp.transpose(kv_t.reshape(b, 1, heads, HEAD_DIM, l), (0, 1, 4, 2, 3))


def kernel(x_prompt, x_sample, cache_a_k, cache_a_v, cache_b_k, cache_b_v, c, c_ctx, w_mod, b_mod,
           g_mix_pre, g_mix_post, g_ffn_pre, g_ffn_post, w_in, rpb_a, sink_b, g_grp_a, g_grp_b,
           w_out, w_up, conv_w, conv_b, w_down):
    bp, lp, _ = x_prompt.shape
    bs, ts, _ = x_sample.shape
    assert w_in.shape[0] == 1
    xp = x_prompt.reshape(bp * lp, D_MODEL)
    xs = x_sample.reshape(bs * ts, D_MODEL)
    cond8 = jnp.concatenate([c_ctx[None], c, jnp.zeros((8 - 1 - bs, D_MODEL), _F32)], axis=0)
    cos_tab, sin_tab = _rope_tables(ts)

    l = 0
    mod3 = _mod_call(cond8, w_mod[l], b_mod[l]).reshape(8, 1, 6 * D_MODEL)
    g_b_p = g_grp_b[l].reshape(KV_B, G_B, HEAD_DIM).transpose(1, 0, 2).reshape(W_B)
    rpb_pad = jnp.pad(rpb_a[l], ((0, 0), (0, 0), (0, LANES - (2 * NA_COLS - 1))))

    prompt_row = lambda i: 0
    assert ts >= SEQ_TAIL_TILE
    sample_row = lambda i: 1 + i

    (q_p, kat_p, vat_p, kbt_p, vbt_p), (q_s, ka_s, va_s, kb_s, vb_s) = _inproj_call(
        xp, xs, mod3, g_mix_pre[l], w_in[l], (cos_tab, sin_tab), lp, ts)

    a_p, b_p, w_up_b, w_down_b = _ctx_attn_call(sink_b[l], q_p, kat_p, vat_p, kbt_p, vbt_p, g_grp_a[l], g_b_p,
                                                w_up[l], w_down[l], lp)
    y_p = _tail_seq_call(a_p, b_p, xp, mod3, prompt_row, g_mix_post[l], g_ffn_pre[l], g_ffn_post[l], w_out[l],
                         w_up_b, conv_w[l], conv_b[l], w_down_b, lp, "tail_prompt")

    a_s = _na_call(q_s, ka_s, va_s, _to_head_dim_token(cache_a_k[:, l]), _to_head_dim_token(cache_a_v[:, l]),
                   rpb_pad, g_grp_a[l], bs, ts)
    b_s = _win_call(sink_b[l], q_s, kb_s, vb_s, _to_head_dim_token(cache_b_k[:, l]),
                    _to_head_dim_token(cache_b_v[:, l]), g_b_p, bs, ts)
    y_s = _tail_seq_call(a_s, b_s, xs, mod3, sample_row, g_mix_post[l], g_ffn_pre[l], g_ffn_post[l], w_out[l],
                         w_up_b, conv_w[l], conv_b[l], w_down_b, ts, "tail_sample")

    return (y_p.reshape(bp, lp, D_MODEL), y_s.reshape(bs, ts, D_MODEL),
            _from_head_dim_token(kat_p, H_A), _from_head_dim_token(vat_p, H_A),
            _from_head_dim_token(kbt_p, KV_B), _from_head_dim_token(vbt_p, KV_B))
```

```python
import functools

import numpy as np
import jax
import jax.numpy as jnp
from jax import lax
from jax.experimental import pallas as pl
from jax.experimental.pallas import tpu as pltpu

D_MODEL = 1024
HEAD_DIM = 64
H_A = 8
H_B = 8
KV_B = 2
G_B = H_B // KV_B
W_A = H_A * HEAD_DIM
W_B = H_B * HEAD_DIM
KV_W_B = KV_B * HEAD_DIM
IN_WIDTH = 3 * W_A + W_B + 2 * KV_W_B
GRID_W = 64
NA_ROWS = 8
NA_COLS = 16
SWA_WINDOW = 128
BLOCK = 128
D_FF = 2816
ROPE_THETA = 10000.0
EPS = 1e-6
NEG_INF = -1e30
Q_SCALE = HEAD_DIM ** -0.5
ROPE_HALF = HEAD_DIM // 4

LANES = 128
FF_CHUNK = 256
TOKEN_TILE = 512
FFN_ROW_BLOCK = 256
DOWN_ROW_BLOCK = 256
SEQ_TAIL_TILE = 512
PROJ_ROW_BLOCK = 256
MOD_STREAMS = 2
MOD_ROW_BLOCK = 128
NA_QROWS = 4
WIN_QBLOCKS = 2
CAST_COLS = 512
VMEM_LIMIT = 48 * 1024 * 1024
TAIL_VMEM_LIMIT = 56 * 1024 * 1024

C_QA, C_KA, C_VA, C_QB, C_KB, C_VB = 0, W_A, 2 * W_A, 3 * W_A, 3 * W_A + W_B, 3 * W_A + W_B + KV_W_B

_QB_HEAD_ORDER = tuple(kv * G_B + g for g in range(G_B) for kv in range(KV_B))

_BF16 = jnp.bfloat16
_F32 = jnp.float32


def _dot(a, b):
    return jnp.dot(a, b, preferred_element_type=_F32)


def _dot_nt(a, b):
    return lax.dot_general(a, b, (((1,), (1,)), ((), ())), preferred_element_type=_F32)


def _rms(x, g):
    var = jnp.mean(x * x, axis=-1, keepdims=True)
    return x * lax.rsqrt(var + EPS) * g


def _stack_heads(q2, lo):
    zero = jnp.zeros_like(q2)
    return jnp.concatenate([jnp.where(lo, q2, zero), jnp.where(lo, zero, q2)], axis=0)


def _unstack_heads(o2, lo):
    m = o2.shape[0] // 2
    return jnp.where(lo, o2[:m], o2[m:])


def _softmax_pv(scores, pv, sink=None):
    m = scores[0].max(axis=-1, keepdims=True)
    for s in scores[1:]:
        m = jnp.maximum(m, s.max(axis=-1, keepdims=True))
    if sink is not None:
        m = jnp.maximum(m, sink)
    denom = None
    out = None
    for s, f in zip(scores, pv):
        e = jnp.exp(s - m)
        l = e.sum(axis=-1, keepdims=True)
        o = f(e.astype(_BF16))
        denom = l if denom is None else denom + l
        out = o if out is None else out + o
    if sink is not None:
        denom = denom + jnp.exp(sink - m)
    return out / denom


def _resident(shape):
    return pl.BlockSpec(shape, lambda *_: (0,) * len(shape), pipeline_mode=pl.Buffered(1))


def _mod_kernel(c_ref, *refs):
    w_refs, b_ref, o_ref = refs[:-2], refs[-2], refs[-1]
    i = pl.program_id(0)

    @pl.when(i == 0)
    def _():
        o_ref[...] = jnp.broadcast_to(b_ref[...], o_ref.shape)

    acc = o_ref[...]
    for j, w_ref in enumerate(w_refs):
        c = c_ref[i * len(w_refs) + j]
        s = (c * jax.nn.sigmoid(c)).astype(_BF16)
        acc = acc + _dot(s, w_ref[...].astype(_BF16))
    o_ref[...] = acc


def _mod_call(cond8, w_mod, b_mod):
    n = w_mod.shape[1]
    kb = MOD_ROW_BLOCK
    nblk = D_MODEL // kb
    cond_blocks = cond8.reshape(8, nblk, kb).transpose(1, 0, 2)
    return pl.pallas_call(
        _mod_kernel,
        grid=(nblk // MOD_STREAMS,),
        in_specs=[pl.BlockSpec((nblk, 8, kb), lambda i: (0, 0, 0))]
        + [pl.BlockSpec((kb, n), lambda i, j=j: (i * MOD_STREAMS + j, 0)) for j in range(MOD_STREAMS)]
        + [pl.BlockSpec((1, n), lambda i: (0, 0))],
        out_specs=pl.BlockSpec((8, n), lambda i: (0, 0)),
        out_shape=jax.ShapeDtypeStruct((8, n), _F32),
        compiler_params=pltpu.CompilerParams(dimension_semantics=("arbitrary",), vmem_limit_bytes=VMEM_LIMIT),
        name="mod",
    )(cond_blocks, *([w_mod] * MOD_STREAMS), b_mod.reshape(1, n))


def _prepare_in_weight(w_ref, w_scr):
    for c0 in list(range(0, C_QB, CAST_COLS)) + [C_KB]:
        n = min(CAST_COLS, IN_WIDTH - c0)
        w_scr[:, c0:c0 + n] = w_ref[:, c0:c0 + n].astype(_BF16)
    lo = lax.broadcasted_iota(jnp.int32, (D_MODEL, LANES), 1) < HEAD_DIM
    for j in range(G_B):
        h_lo, h_hi = _QB_HEAD_ORDER[2 * j], _QB_HEAD_ORDER[2 * j + 1]

        def head_at(h, want_hi):
            blk = w_ref[:, C_QB + LANES * (h // 2):C_QB + LANES * (h // 2 + 1)]
            return blk if (h % 2 == 1) == want_hi else pltpu.roll(blk, HEAD_DIM, 1)

        w_scr[:, C_QB + LANES * j:C_QB + LANES * (j + 1)] = jnp.where(
            lo, head_at(h_lo, False), head_at(h_hi, True)).astype(_BF16)


def _rope(z, cos, sin):
    lane = lax.broadcasted_iota(jnp.int32, z.shape, 1)
    partner = jnp.where((lane & ROPE_HALF) == 0,
                        pltpu.roll(z, LANES - ROPE_HALF, 1), pltpu.roll(z, ROPE_HALF, 1))
    return z * cos + partner * sin


def _inproj_rows(x_ref, mod, g_ref, w_scr, rope_refs, outs):
    q_ref, ka_ref, va_ref, kb_ref, vb_ref = outs
    kv_transposed = rope_refs is None
    sh1 = mod[:, 0:D_MODEL]
    sc1 = mod[:, D_MODEL:2 * D_MODEL]
    rb = PROJ_ROW_BLOCK

    for r in range(0, x_ref.shape[0], rb):
        rows = slice(r, r + rb)
        h = (_rms(x_ref[rows, :], g_ref[...]) * (1 + sc1) + sh1).astype(_BF16)

        def proj(c0, n):
            return _dot(h, w_scr[:, c0:c0 + n])

        def put_kv(ref, z):
            if not kv_transposed:
                ref[rows, :] = z.astype(ref.dtype)
                return
            l = ref.shape[2]
            for t0 in range(0, rb, l):
                for c0 in range(0, z.shape[1], LANES):
                    ref[(r + t0) // l, c0:c0 + LANES, :] = z[t0:t0 + l, c0:c0 + LANES].T.astype(ref.dtype)

        q_ref[rows, 0:W_A] = (proj(C_QA, W_A) * Q_SCALE).astype(q_ref.dtype)
        put_kv(ka_ref, proj(C_KA, W_A))
        put_kv(va_ref, proj(C_VA, W_A))
        zkv = proj(C_KB, 2 * KV_W_B)
        zkb, zvb = zkv[:, :KV_W_B], zkv[:, KV_W_B:]
        zqb = proj(C_QB, W_B)
        put_kv(vb_ref, zvb)
        if rope_refs is not None:
            cos = rope_refs[0][rows, :]
            sin = rope_refs[1][rows, :]
            for j in range(W_B // LANES):
                q_ref[rows, W_A + LANES * j:W_A + LANES * (j + 1)] = (
                    _rope(zqb[:, LANES * j:LANES * (j + 1)], cos, sin) * Q_SCALE).astype(q_ref.dtype)
            put_kv(kb_ref, _rope(zkb, cos, sin))
        else:
            q_ref[rows, W_A:W_A + W_B] = (zqb * Q_SCALE).astype(q_ref.dtype)
            put_kv(kb_ref, zkb)


def _inproj_kernel(xp_ref, xs_ref, mod_ref, g_ref, w_ref, cos_ref, sin_ref, *refs, prompt_steps):
    outs_p, outs_s, w_scr = refs[0:5], refs[5:10], refs[10]
    i = pl.program_id(0)

    @pl.when(i == 0)
    def _():
        _prepare_in_weight(w_ref, w_scr)

    mod = mod_ref[0]

    @pl.when(i < prompt_steps)
    def _():
        _inproj_rows(xp_ref, mod, g_ref, w_scr, None, outs_p)

    @pl.when(i >= prompt_steps)
    def _():
        _inproj_rows(xs_ref, mod, g_ref, w_scr, (cos_ref, sin_ref), outs_s)


def _inproj_call(xp, xs, mod3, g_pre, w_in, rope_tabs, lp, ts):
    tm = TOKEN_TILE
    tp, tsamp = xp.shape[0], xs.shape[0]
    n_p, n_s = tp // tm, tsamp // tm
    assert n_p * tm == tp and n_s * tm == tsamp and tm % lp == 0 and ts % tm == 0
    tiles_per_seq = ts // tm
    p_tile = lambda i: jnp.minimum(i, n_p - 1)
    s_tile = lambda i: jnp.maximum(i - n_p, 0)
    mod_row = lambda i: jnp.where(i < n_p, 0, 1 + s_tile(i) // tiles_per_seq)
    nseq = tm // lp
    in_specs = [
        pl.BlockSpec((tm, D_MODEL), lambda i: (p_tile(i), 0)),
        pl.BlockSpec((tm, D_MODEL), lambda i: (s_tile(i), 0)),
        pl.BlockSpec((1, 1, 6 * D_MODEL), lambda i: (mod_row(i), 0, 0)),
        pl.BlockSpec((1, D_MODEL), lambda i: (0, 0)),
        _resident((D_MODEL, IN_WIDTH)),
        pl.BlockSpec((tm, LANES), lambda i: (s_tile(i) % tiles_per_seq, 0)),
        pl.BlockSpec((tm, LANES), lambda i: (s_tile(i) % tiles_per_seq, 0)),
    ]
    kvt_shape = lambda w: jax.ShapeDtypeStruct((tp // lp, w, lp), _F32)
    kvt_spec = lambda w: pl.BlockSpec((nseq, w, lp), lambda i: (p_tile(i), 0, 0))
    kv_shape = lambda w: jax.ShapeDtypeStruct((tsamp, w), _BF16)
    kv_spec = lambda w: pl.BlockSpec((tm, w), lambda i: (s_tile(i), 0))
    out_shape = (jax.ShapeDtypeStruct((tp, W_A + W_B), _BF16),
                 kvt_shape(W_A), kvt_shape(W_A), kvt_shape(KV_W_B), kvt_shape(KV_W_B),
                 jax.ShapeDtypeStruct((tsamp, W_A + W_B), _BF16),
                 kv_shape(W_A), kv_shape(W_A), kv_shape(KV_W_B), kv_shape(KV_W_B))
    out_specs = (pl.BlockSpec((tm, W_A + W_B), lambda i: (p_tile(i), 0)),
                 kvt_spec(W_A), kvt_spec(W_A), kvt_spec(KV_W_B), kvt_spec(KV_W_B),
                 pl.BlockSpec((tm, W_A + W_B), lambda i: (s_tile(i), 0)),
                 kv_spec(W_A), kv_spec(W_A), kv_spec(KV_W_B), kv_spec(KV_W_B))
    outs = pl.pallas_call(
        functools.partial(_inproj_kernel, prompt_steps=n_p),
        grid=(n_p + n_s,),
        in_specs=in_specs,
        out_specs=out_specs,
        out_shape=out_shape,
        scratch_shapes=[pltpu.VMEM((D_MODEL, IN_WIDTH), _BF16)],
        compiler_params=pltpu.CompilerParams(
            dimension_semantics=("arbitrary",), vmem_limit_bytes=VMEM_LIMIT),
        name="inproj",
    )(xp, xs, mod3, g_pre.reshape(1, D_MODEL), w_in, *rope_tabs)
    return outs[:5], outs[5:]


def _ctx_attn_kernel(sink_ref, q_ref, ka_ref, va_ref, kb_ref, vb_ref, ga_ref, gb_ref, wu_ref, wd_ref,
                     a_ref, b_ref, wu_out, wd_out):
    wu_out[...] = wu_ref[...].astype(wu_out.dtype)
    wd_out[...] = wd_ref[...].astype(wd_out.dtype)
    l = q_ref.shape[0]
    lo = lax.broadcasted_iota(jnp.int32, (l, LANES), 1) < HEAD_DIM
    row_lo = lax.broadcasted_iota(jnp.int32, (2 * l, 1), 0) < l
    oa = []
    for p in range(W_A // LANES):
        sl = slice(LANES * p, LANES * (p + 1))
        qs = _stack_heads(q_ref[:, sl], lo)
        s = _dot(qs, ka_ref[0, sl, :].astype(_BF16))
        vt = va_ref[0, sl, :].astype(_BF16)
        oa.append(_unstack_heads(_softmax_pv([s], [lambda e, vt=vt: _dot_nt(e, vt)]), lo))
    a_ref[...] = _rms(jnp.concatenate(oa, axis=1), ga_ref[...]).astype(a_ref.dtype)
    kb = kb_ref[0].astype(_BF16)
    vb = vb_ref[0].astype(_BF16)
    ob = []
    for j in range(G_B):
        qs = _stack_heads(q_ref[:, W_A + LANES * j:W_A + LANES * (j + 1)], lo)
        s = _dot(qs, kb)
        sink = jnp.where(row_lo, sink_ref[0, j], sink_ref[1, j])
        ob.append(_unstack_heads(_softmax_pv([s], [lambda e: _dot_nt(e, vb)], sink), lo))
    b_ref[...] = _rms(jnp.concatenate(ob, axis=1), gb_ref[...]).astype(b_ref.dtype)


def _ctx_attn_call(sink, q, ka, va, kb, vb, g_a, g_b, w_up, w_down, seq):
    t = q.shape[0]
    steps = t // seq
    ru, rd = w_up.shape[0] // steps, w_down.shape[0] // steps
    assert ru * steps == w_up.shape[0] and rd * steps == w_down.shape[0] and ru % 16 == 0 and rd % 16 == 0
    row = lambda b: (b, 0)
    seq3 = lambda b: (b, 0, 0)
    const = lambda b: (0, 0)
    return pl.pallas_call(
        _ctx_attn_kernel,
        grid=(steps,),
        in_specs=[
            pl.BlockSpec(memory_space=pltpu.SMEM),
            pl.BlockSpec((seq, W_A + W_B), row),
            pl.BlockSpec((1, W_A, seq), seq3),
            pl.BlockSpec((1, W_A, seq), seq3),
            pl.BlockSpec((1, KV_W_B, seq), seq3),
            pl.BlockSpec((1, KV_W_B, seq), seq3),
            pl.BlockSpec((1, W_A), const),
            pl.BlockSpec((1, W_B), const),
            pl.BlockSpec((ru, w_up.shape[1]), row),
            pl.BlockSpec((rd, w_down.shape[1]), row),
        ],
        out_specs=(pl.BlockSpec((seq, W_A), row), pl.BlockSpec((seq, W_B), row),
                   pl.BlockSpec((ru, w_up.shape[1]), row), pl.BlockSpec((rd, w_down.shape[1]), row)),
        out_shape=(jax.ShapeDtypeStruct((t, W_A), _BF16), jax.ShapeDtypeStruct((t, W_B), _BF16),
                   jax.ShapeDtypeStruct(w_up.shape, _BF16), jax.ShapeDtypeStruct(w_down.shape, _BF16)),
        compiler_params=pltpu.CompilerParams(
            dimension_semantics=("parallel",), vmem_limit_bytes=VMEM_LIMIT),
        name="ctx_attn",
    )(sink, q, ka, va, kb, vb, g_a.reshape(1, W_A), g_b.reshape(1, W_B), w_up, w_down)


def _na_plan(rows):
    kr = min(NA_ROWS, rows)
    win = kr + NA_QROWS
    row_start = lambda r: min(max(r - kr // 2, 0), rows - kr)
    blocks, classes = [], []
    for r0 in range(0, rows, NA_QROWS):
        ws = min(row_start(r0), rows - win)
        keys = []
        for r in range(r0, r0 + NA_QROWS):
            key = (ws - r + NA_ROWS - 1, row_start(r) - ws)
            if key not in classes:
                classes.append(key)
            keys.append(classes.index(key))
        blocks.append((r0, ws, keys))
    return kr, win, blocks, classes


def _na_kernel(q_ref, k_ref, v_ref, ck_ref, cv_ref, rpb_ref, ga_ref, a_ref, bias_ref, o_scr):
    rows = q_ref.shape[0] // GRID_W
    kr, win, blocks, classes = _na_plan(rows)
    m = NA_QROWS * GRID_W
    lane = lax.broadcasted_iota(jnp.int32, (GRID_W, LANES), 1)
    lo = lane < HEAD_DIM
    lo_q = lax.broadcasted_iota(jnp.int32, (m, LANES), 1) < HEAD_DIM
    cq = lax.broadcasted_iota(jnp.int32, (GRID_W, LANES), 0)
    ck = lane & (GRID_W - 1)
    cs = jnp.clip(cq - NA_COLS // 2, 0, GRID_W - NA_COLS)
    valid = (ck >= cs) & (ck < cs + NA_COLS)
    neg = jnp.full((GRID_W, LANES), NEG_INF, _F32)
    n_drow = 2 * NA_ROWS - 1

    for p in range(W_A // LANES):
        sl = slice(LANES * p, LANES * (p + 1))
        for hh in range(2):
            t_lo, t_hi = [], []
            for d in range(n_drow):
                base = jnp.broadcast_to(rpb_ref[2 * p + hh, d:d + 1, :], (GRID_W, LANES))
                t_lo.append(pltpu.roll(base, LANES - (NA_COLS - 1), 1, stride=1, stride_axis=0))
                t_hi.append(pltpu.roll(base, GRID_W - (NA_COLS - 1), 1, stride=1, stride_axis=0))
            tiles = {}
            for ci, (d_first, off) in enumerate(classes):
                for j in range(win // 2):
                    d_of = lambda i: d_first + i if off <= i < off + kr else None
                    key = (d_of(2 * j), d_of(2 * j + 1))
                    if key not in tiles:
                        left = neg if key[0] is None else t_lo[key[0]]
                        right = neg if key[1] is None else t_hi[key[1]]
                        tiles[key] = jnp.where(valid, jnp.where(lo, left, right), NEG_INF)
                    bias_ref[ci, GRID_W * hh:GRID_W * (hh + 1), LANES * j:LANES * (j + 1)] = tiles[key]

        ckt = ck_ref[0, sl, :].astype(_BF16)
        cvt = cv_ref[0, sl, :].astype(_BF16)

        for r0, ws, keys in blocks:
            q0, k0 = r0 * GRID_W, ws * GRID_W
            qs = _stack_heads(q_ref[q0:q0 + m, sl], lo_q)
            kw = k_ref[k0:k0 + win * GRID_W, sl]
            vw = v_ref[k0:k0 + win * GRID_W, sl]
            bias = jnp.concatenate(
                [bias_ref[ci, GRID_W * hh:GRID_W * (hh + 1), :] for hh in range(2) for ci in keys], axis=0)
            s_nb = _dot_nt(qs, kw) + bias
            s_cx = _dot(qs, ckt)
            o2 = _softmax_pv([s_nb, s_cx], [lambda e, vw=vw: _dot(e, vw), lambda e: _dot_nt(e, cvt)])
            o_scr[q0:q0 + m, sl] = _unstack_heads(o2, lo_q)

    a_ref[...] = _rms(o_scr[...], ga_ref[...]).astype(a_ref.dtype)


def _na_call(q, k, v, ckt, cvt, rpb_pad, g_a, nb, seq):
    past = ckt.shape[2]
    _, win, _, classes = _na_plan(seq // GRID_W)
    blk = lambda b: (b, 0)
    return pl.pallas_call(
        _na_kernel,
        grid=(nb,),
        in_specs=[
            pl.BlockSpec((seq, W_A), blk),
            pl.BlockSpec((seq, W_A), blk),
            pl.BlockSpec((seq, W_A), blk),
            pl.BlockSpec((1, W_A, past), lambda b: (b, 0, 0)),
            pl.BlockSpec((1, W_A, past), lambda b: (b, 0, 0)),
            pl.BlockSpec((H_A, 2 * NA_ROWS - 1, LANES), lambda b: (0, 0, 0)),
            pl.BlockSpec((1, W_A), lambda b: (0, 0)),
        ],
        out_specs=pl.BlockSpec((seq, W_A), blk),
        out_shape=jax.ShapeDtypeStruct((nb * seq, W_A), _BF16),
        scratch_shapes=[pltpu.VMEM((len(classes), 2 * GRID_W, win * GRID_W), _F32),
                        pltpu.VMEM((seq, W_A), _F32)],
        compiler_params=pltpu.CompilerParams(
            dimension_semantics=("parallel",), vmem_limit_bytes=VMEM_LIMIT),
        name="na_attn",
    )(q, k, v, ckt, cvt, rpb_pad, g_a.reshape(1, W_A))


def _win_kernel(sink_ref, q_ref, k_ref, v_ref, ck_ref, cv_ref, gb_ref, b_ref, o_scr):
    t = q_ref.shape[0]
    qb = WIN_QBLOCKS * BLOCK
    band = qb + 2 * BLOCK
    lo = lax.broadcasted_iota(jnp.int32, (qb, LANES), 1) < HEAD_DIM
    row_lo = lax.broadcasted_iota(jnp.int32, (2 * qb, 1), 0) < qb
    qi = lax.broadcasted_iota(jnp.int32, (2 * qb, band), 0) & (qb - 1)
    kj = lax.broadcasted_iota(jnp.int32, (2 * qb, band), 1)
    rel = qi - kj
    ckt = ck_ref[0].astype(_BF16)
    cvt = cv_ref[0].astype(_BF16)

    starts = {q0: min(max(q0 - BLOCK, 0), t - band) for q0 in range(0, t, qb)}
    masks = {off: jnp.where(jnp.abs(rel + off) <= SWA_WINDOW, 0.0, NEG_INF)
             for off in sorted({q0 - k0 for q0, k0 in starts.items()})}

    for j in range(G_B):
        sl = slice(LANES * j, LANES * (j + 1))
        sink = jnp.where(row_lo, sink_ref[0, j], sink_ref[1, j])
        for q0, k0 in starts.items():
            qs = _stack_heads(q_ref[q0:q0 + qb, sl], lo)
            s = _dot_nt(qs, k_ref[k0:k0 + band, :]) + masks[q0 - k0]
            s_cx = _dot(qs, ckt)
            vband = v_ref[k0:k0 + band, :]
            o2 = _softmax_pv([s, s_cx], [lambda e, vband=vband: _dot(e, vband), lambda e: _dot_nt(e, cvt)],
                             sink)
            o_scr[q0:q0 + qb, sl] = _unstack_heads(o2, lo)

    b_ref[...] = _rms(o_scr[...], gb_ref[...]).astype(b_ref.dtype)


def _win_call(sink, q, k, v, ckt, cvt, g_b, nb, seq):
    past = ckt.shape[2]
    return pl.pallas_call(
        _win_kernel,
        grid=(nb,),
        in_specs=[
            pl.BlockSpec(memory_space=pltpu.SMEM),
            pl.BlockSpec((seq, W_B), lambda b: (b, W_A // W_B)),
            pl.BlockSpec((seq, KV_W_B), lambda b: (b, 0)),
            pl.BlockSpec((seq, KV_W_B), lambda b: (b, 0)),
            pl.BlockSpec((1, KV_W_B, past), lambda b: (b, 0, 0)),
            pl.BlockSpec((1, KV_W_B, past), lambda b: (b, 0, 0)),
            pl.BlockSpec((1, W_B), lambda b: (0, 0)),
        ],
        out_specs=pl.BlockSpec((seq, W_B), lambda b: (b, 0)),
        out_shape=jax.ShapeDtypeStruct((nb * seq, W_B), _BF16),
        scratch_shapes=[pltpu.VMEM((seq, W_B), _F32)],
        compiler_params=pltpu.CompilerParams(
            dimension_semantics=("parallel",), vmem_limit_bytes=VMEM_LIMIT),
        name="win_attn",
    )(sink, q, k, v, ckt, cvt, g_b.reshape(1, W_B))


def _prepare_out_weight(w_ref, w_scr):
    for r0 in range(0, W_A, CAST_COLS):
        w_scr[r0:r0 + CAST_COLS, :] = w_ref[r0:r0 + CAST_COLS, :].astype(_BF16)
    for i, h in enumerate(_QB_HEAD_ORDER):
        w_scr[W_A + HEAD_DIM * i:W_A + HEAD_DIM * (i + 1), :] = (
            w_ref[W_A + HEAD_DIM * h:W_A + HEAD_DIM * (h + 1), :].astype(_BF16))


def _zero_row(x, row):
    r0 = row - row % 8
    sub = lax.broadcasted_iota(jnp.int32, (8, 1), 0)
    slab = jnp.where(sub == row % 8, 0.0, x[r0:r0 + 8])
    parts = ([x[:r0]] if r0 else []) + [slab] + ([x[r0 + 8:]] if r0 + 8 < x.shape[0] else [])
    return jnp.concatenate(parts, axis=0)


def _conv3(u, cw, cb, seq_len):
    tm = u.shape[0]
    prev = pltpu.roll(u, 1, 0)
    nxt = pltpu.roll(u, tm - 1, 0)
    for s0 in range(0, tm, seq_len):
        prev = _zero_row(prev, s0)
        nxt = _zero_row(nxt, s0 + seq_len - 1)
    return prev * cw[0:1, :] + u * cw[1:2, :] + nxt * cw[2:3, :] + cb


def _gated_silu(gate, val):
    half = 0.5 * gate
    return ((half + half * jnp.tanh(half)) * val).astype(_BF16)


def _up(h2_ref, w):
    tm = h2_ref.shape[0]
    return jnp.concatenate(
        [_dot(h2_ref[r:r + FFN_ROW_BLOCK, :], w) for r in range(0, tm, FFN_ROW_BLOCK)], axis=0)


def _out_proj(a_ref, b_ref, x_ref, mod, gpost_ref, gffn_ref, wo_scr, x1_ref, h2_ref):
    gt1 = mod[:, 2 * D_MODEL:3 * D_MODEL]
    sh2 = mod[:, 3 * D_MODEL:4 * D_MODEL]
    sc2 = mod[:, 4 * D_MODEL:5 * D_MODEL]
    for r in range(0, x_ref.shape[0], FFN_ROW_BLOCK):
        rows = slice(r, r + FFN_ROW_BLOCK)
        y = _dot(a_ref[rows, :], wo_scr[0:W_A, :]) + _dot(b_ref[rows, :], wo_scr[W_A:W_A + W_B, :])
        x1 = x_ref[rows, :] + gt1 * _rms(y, gpost_ref[...])
        x1_ref[rows, :] = x1
        h2_ref[rows, :] = (_rms(x1, gffn_ref[...]) * (1 + sc2) + sh2).astype(h2_ref.dtype)


def _down_proj(act_ref, wd_ref, mod, gpost2_ref, x1_ref, out_ref):
    gt2 = mod[:, 5 * D_MODEL:6 * D_MODEL]
    for r in range(0, act_ref.shape[0], DOWN_ROW_BLOCK):
        rows = slice(r, r + DOWN_ROW_BLOCK)
        y = _dot(act_ref[rows, :], wd_ref[...])
        out_ref[rows, :] = x1_ref[rows, :] + gt2 * _rms(y, gpost2_ref[...])


def _tail_seq_kernel(a_ref, b_ref, x_ref, mod_ref, gpost_ref, gffn_ref, gpost2_ref, wo_ref, wu_ref,
                     cw_ref, cb_ref, wd_ref, out_ref, wo_scr, h2_scr, act_scr, *, seq_len):
    mod = mod_ref[0]

    @pl.when(pl.program_id(0) == 0)
    def _():
        _prepare_out_weight(wo_ref, wo_scr)

    _out_proj(a_ref, b_ref, x_ref, mod, gpost_ref, gffn_ref, wo_scr, out_ref, h2_scr)
    tc = FF_CHUNK

    def chunk(col0):
        cols = slice(col0, col0 + tc)
        return _conv3(_up(h2_scr, wu_ref[:, cols]), cw_ref[:, cols], cb_ref[:, cols], seq_len)

    for c0 in range(0, D_FF, tc):
        act_scr[:, c0:c0 + tc] = _gated_silu(chunk(c0), chunk(D_FF + c0))
    _down_proj(act_scr, wd_ref, mod, gpost2_ref, out_ref, out_ref)


def _tail_seq_call(a, b, x2d, mod3, mod_row, g_post, g_ffn, g_post2, w_out, w_up, conv_w, conv_b, w_down,
                   seq_len, name):
    t = x2d.shape[0]
    tm = max(SEQ_TAIL_TILE, seq_len)
    big = tm > SEQ_TAIL_TILE
    vmem_limit = TAIL_VMEM_LIMIT if big else VMEM_LIMIT
    x_mode = dict(pipeline_mode=pl.Buffered(1)) if big else {}
    row = lambda i: (i, 0)
    const = lambda i: (0, 0)
    return pl.pallas_call(
        functools.partial(_tail_seq_kernel, seq_len=seq_len),
        grid=(t // tm,),
        in_specs=[
            pl.BlockSpec((tm, W_A), row),
            pl.BlockSpec((tm, W_B), row),
            pl.BlockSpec((tm, D_MODEL), row, **x_mode),
            pl.BlockSpec((1, 1, 6 * D_MODEL), lambda i: (mod_row(i), 0, 0)),
            pl.BlockSpec((1, D_MODEL), const),
            pl.BlockSpec((1, D_MODEL), const),
            pl.BlockSpec((1, D_MODEL), const),
            _resident((W_A + W_B, D_MODEL)),
            _resident((D_MODEL, 2 * D_FF)),
            _resident((3, 2 * D_FF)),
            _resident((1, 2 * D_FF)),
            _resident((D_FF, D_MODEL)),
        ],
        out_specs=pl.BlockSpec((tm, D_MODEL), row),
        out_shape=jax.ShapeDtypeStruct((t, D_MODEL), _F32),
        scratch_shapes=[
            pltpu.VMEM((W_A + W_B, D_MODEL), _BF16),
            pltpu.VMEM((tm, D_MODEL), _BF16),
            pltpu.VMEM((tm, D_FF), _BF16),
        ],
        compiler_params=pltpu.CompilerParams(
            dimension_semantics=("arbitrary",), vmem_limit_bytes=vmem_limit),
        name=name,
    )(a, b, x2d, mod3, g_post.reshape(1, D_MODEL), g_ffn.reshape(1, D_MODEL), g_post2.reshape(1, D_MODEL),
      w_out, w_up, conv_w, conv_b.reshape(1, 2 * D_FF), w_down)


def _rope_tables(seq):
    n = ROPE_HALF
    t = np.arange(seq)
    lane = np.arange(HEAD_DIM)
    pos = np.where(lane[None, :] < HEAD_DIM // 2, (t // GRID_W)[:, None], (t % GRID_W)[:, None])
    inv = 1.0 / (ROPE_THETA ** (np.arange(n, dtype=np.float64) / n))
    ang = pos.astype(np.float64) * inv[lane % n][None, :]
    sign = np.where((lane & n) == 0, -1.0, 1.0)[None, :]
    cos = np.tile(np.cos(ang), (1, LANES // HEAD_DIM)).astype(np.float32)
    sin = np.tile(np.sin(ang) * sign, (1, LANES // HEAD_DIM)).astype(np.float32)
    return jnp.asarray(cos), jnp.asarray(sin)


def _to_head_dim_token(cache_l):
    b, l, h, d = cache_l.shape
    return jnp.transpose(cache_l, (0, 2, 3, 1)).reshape(b, h * d, l)


def _from_head_dim_token(kv_t, heads):
    b, _, l = kv_t.shape
    return jnp.transpose(kv_t.reshape(b, 1, heads, HEAD_DIM, l), (0, 1, 4, 2, 3))


def kernel(x_prompt, x_sample, cache_a_k, cache_a_v, cache_b_k, cache_b_v, c, c_ctx, w_mod, b_mod,
           g_mix_pre, g_mix_post, g_ffn_pre, g_ffn_post, w_in, rpb_a, sink_b, g_grp_a, g_grp_b,
           w_out, w_up, conv_w, conv_b, w_down):
    bp, lp, _ = x_prompt.shape
    bs, ts, _ = x_sample.shape
    assert w_in.shape[0] == 1
    xp = x_prompt.reshape(bp * lp, D_MODEL)
    xs = x_sample.reshape(bs * ts, D_MODEL)
    cond8 = jnp.concatenate([c_ctx[None], c, jnp.zeros((8 - 1 - bs, D_MODEL), _F32)], axis=0)
    cos_tab, sin_tab = _rope_tables(ts)

    l = 0
    mod3 = _mod_call(cond8, w_mod[l], b_mod[l]).reshape(8, 1, 6 * D_MODEL)
    g_b_p = g_grp_b[l].reshape(KV_B, G_B, HEAD_DIM).transpose(1, 0, 2).reshape(W_B)
    rpb_pad = jnp.pad(rpb_a[l], ((0, 0), (0, 0), (0, LANES - (2 * NA_COLS - 1))))

    prompt_row = lambda i: 0
    assert ts >= SEQ_TAIL_TILE
    sample_row = lambda i: 1 + i

    (q_p, kat_p, vat_p, kbt_p, vbt_p), (q_s, ka_s, va_s, kb_s, vb_s) = _inproj_call(
        xp, xs, mod3, g_mix_pre[l], w_in[l], (cos_tab, sin_tab), lp, ts)

    a_p, b_p, w_up_b, w_down_b = _ctx_attn_call(sink_b[l], q_p, kat_p, vat_p, kbt_p, vbt_p, g_grp_a[l], g_b_p,
                                                w_up[l], w_down[l], lp)
    y_p = _tail_seq_call(a_p, b_p, xp, mod3, prompt_row, g_mix_post[l], g_ffn_pre[l], g_ffn_post[l], w_out[l],
                         w_up_b, conv_w[l], conv_b[l], w_down_b, lp, "tail_prompt")

    a_s = _na_call(q_s, ka_s, va_s, _to_head_dim_token(cache_a_k[:, l]), _to_head_dim_token(cache_a_v[:, l]),
                   rpb_pad, g_grp_a[l], bs, ts)
    b_s = _win_call(sink_b[l], q_s, kb_s, vb_s, _to_head_dim_token(cache_b_k[:, l]),
                    _to_head_dim_token(cache_b_v[:, l]), g_b_p, bs, ts)
    y_s = _tail_seq_call(a_s, b_s, xs, mod3, sample_row, g_mix_post[l], g_ffn_pre[l], g_ffn_post[l], w_out[l],
                         w_up_b, conv_w[l], conv_b[l], w_down_b, ts, "tail_sample")

    return (y_p.reshape(bp, lp, D_MODEL), y_s.reshape(bs, ts, D_MODEL),
            _from_head_dim_token(kat_p, H_A), _from_head_dim_token(vat_p, H_A),
            _from_head_dim_token(kbt_p, KV_B), _from_head_dim_token(vbt_p, KV_B))
```

```python
import functools

import numpy as np
import jax
import jax.numpy as jnp
from jax import lax
from jax.experimental import pallas as pl
from jax.experimental.pallas import tpu as pltpu

D_MODEL = 1024
HEAD_DIM = 64
H_A = 8
H_B = 8
KV_B = 2
G_B = H_B // KV_B
W_A = H_A * HEAD_DIM
W_B = H_B * HEAD_DIM
KV_W_B = KV_B * HEAD_DIM
IN_WIDTH = 3 * W_A + W_B + 2 * KV_W_B
GRID_W = 64
NA_ROWS = 8
NA_COLS = 16
SWA_WINDOW = 128
BLOCK = 128
D_FF = 2816
ROPE_THETA = 10000.0
EPS = 1e-6
NEG_INF = -1e30
Q_SCALE = HEAD_DIM ** -0.5
ROPE_HALF = HEAD_DIM // 4

LANES = 128
FF_CHUNK = 256
TOKEN_TILE = 512
FFN_ROW_BLOCK = 256
DOWN_ROW_BLOCK = 256
SEQ_TAIL_TILE = 512
PROJ_ROW_BLOCK = 256
MOD_STREAMS = 2
MOD_ROW_BLOCK = 128
NA_QROWS = 4
WIN_QBLOCKS = 4
CAST_COLS = 512
VMEM_LIMIT = 48 * 1024 * 1024
TAIL_VMEM_LIMIT = 56 * 1024 * 1024

C_QA, C_KA, C_VA, C_QB, C_KB, C_VB = 0, W_A, 2 * W_A, 3 * W_A, 3 * W_A + W_B, 3 * W_A + W_B + KV_W_B

_QB_HEAD_ORDER = tuple(kv * G_B + g for g in range(G_B) for kv in range(KV_B))

_BF16 = jnp.bfloat16
_F32 = jnp.float32


def _dot(a, b):
    return jnp.dot(a, b, preferred_element_type=_F32)


def _dot_nt(a, b):
    return lax.dot_general(a, b, (((1,), (1,)), ((), ())), preferred_element_type=_F32)


def _rms(x, g):
    var = jnp.mean(x * x, axis=-1, keepdims=True)
    return x * lax.rsqrt(var + EPS) * g


def _stack_heads(q2, lo):
    zero = jnp.zeros_like(q2)
    return jnp.concatenate([jnp.where(lo, q2, zero), jnp.where(lo, zero, q2)], axis=0)


def _unstack_heads(o2, lo):
    m = o2.shape[0] // 2
    return jnp.where(lo, o2[:m], o2[m:])


def _softmax_pv(scores, pv, sink=None):
    m = scores[0].max(axis=-1, keepdims=True)
    for s in scores[1:]:
        m = jnp.maximum(m, s.max(axis=-1, keepdims=True))
    if sink is not None:
        m = jnp.maximum(m, sink)
    denom = None
    out = None
    for s, f in zip(scores, pv):
        e = jnp.exp(s - m)
        l = e.sum(axis=-1, keepdims=True)
        o = f(e.astype(_BF16))
        denom = l if denom is None else denom + l
        out = o if out is None else out + o
    if sink is not None:
        denom = denom + jnp.exp(sink - m)
    return out / denom


def _resident(shape):
    return pl.BlockSpec(shape, lambda *_: (0,) * len(shape), pipeline_mode=pl.Buffered(1))


def _mod_kernel(c_ref, *refs):
    w_refs, b_ref, o_ref = refs[:-2], refs[-2], refs[-1]
    i = pl.program_id(0)

    @pl.when(i == 0)
    def _():
        o_ref[...] = jnp.broadcast_to(b_ref[...], o_ref.shape)

    acc = o_ref[...]
    for j, w_ref in enumerate(w_refs):
        c = c_ref[i * len(w_refs) + j]
        s = (c * jax.nn.sigmoid(c)).astype(_BF16)
        acc = acc + _dot(s, w_ref[...].astype(_BF16))
    o_ref[...] = acc


def _mod_call(cond8, w_mod, b_mod):
    n = w_mod.shape[1]
    kb = MOD_ROW_BLOCK
    nblk = D_MODEL // kb
    cond_blocks = cond8.reshape(8, nblk, kb).transpose(1, 0, 2)
    return pl.pallas_call(
        _mod_kernel,
        grid=(nblk // MOD_STREAMS,),
        in_specs=[pl.BlockSpec((nblk, 8, kb), lambda i: (0, 0, 0))]
        + [pl.BlockSpec((kb, n), lambda i, j=j: (i * MOD_STREAMS + j, 0)) for j in range(MOD_STREAMS)]
        + [pl.BlockSpec((1, n), lambda i: (0, 0))],
        out_specs=pl.BlockSpec((8, n), lambda i: (0, 0)),
        out_shape=jax.ShapeDtypeStruct((8, n), _F32),
        compiler_params=pltpu.CompilerParams(dimension_semantics=("arbitrary",), vmem_limit_bytes=VMEM_LIMIT),
        name="mod",
    )(cond_blocks, *([w_mod] * MOD_STREAMS), b_mod.reshape(1, n))


def _prepare_in_weight(w_ref, w_scr):
    for c0 in list(range(0, C_QB, CAST_COLS)) + [C_KB]:
        n = min(CAST_COLS, IN_WIDTH - c0)
        w_scr[:, c0:c0 + n] = w_ref[:, c0:c0 + n].astype(_BF16)
    lo = lax.broadcasted_iota(jnp.int32, (D_MODEL, LANES), 1) < HEAD_DIM
    for j in range(G_B):
        h_lo, h_hi = _QB_HEAD_ORDER[2 * j], _QB_HEAD_ORDER[2 * j + 1]

        def head_at(h, want_hi):
            blk = w_ref[:, C_QB + LANES * (h // 2):C_QB + LANES * (h // 2 + 1)]
            return blk if (h % 2 == 1) == want_hi else pltpu.roll(blk, HEAD_DIM, 1)

        w_scr[:, C_QB + LANES * j:C_QB + LANES * (j + 1)] = jnp.where(
            lo, head_at(h_lo, False), head_at(h_hi, True)).astype(_BF16)


def _rope(z, cos, sin):
    lane = lax.broadcasted_iota(jnp.int32, z.shape, 1)
    partner = jnp.where((lane & ROPE_HALF) == 0,
                        pltpu.roll(z, LANES - ROPE_HALF, 1), pltpu.roll(z, ROPE_HALF, 1))
    return z * cos + partner * sin


def _inproj_rows(x_ref, mod, g_ref, w_scr, rope_refs, outs):
    q_ref, ka_ref, va_ref, kb_ref, vb_ref = outs
    kv_transposed = rope_refs is None
    sh1 = mod[:, 0:D_MODEL]
    sc1 = mod[:, D_MODEL:2 * D_MODEL]
    rb = PROJ_ROW_BLOCK

    for r in range(0, x_ref.shape[0], rb):
        rows = slice(r, r + rb)
        h = (_rms(x_ref[rows, :], g_ref[...]) * (1 + sc1) + sh1).astype(_BF16)

        def proj(c0, n):
            return _dot(h, w_scr[:, c0:c0 + n])

        def put_kv(ref, z):
            if not kv_transposed:
                ref[rows, :] = z.astype(ref.dtype)
                return
            l = ref.shape[2]
            for t0 in range(0, rb, l):
                for c0 in range(0, z.shape[1], LANES):
                    ref[(r + t0) // l, c0:c0 + LANES, :] = z[t0:t0 + l, c0:c0 + LANES].T.astype(ref.dtype)

        q_ref[rows, 0:W_A] = (proj(C_QA, W_A) * Q_SCALE).astype(q_ref.dtype)
        put_kv(ka_ref, proj(C_KA, W_A))
        put_kv(va_ref, proj(C_VA, W_A))
        zkv = proj(C_KB, 2 * KV_W_B)
        zkb, zvb = zkv[:, :KV_W_B], zkv[:, KV_W_B:]
        zqb = proj(C_QB, W_B)
        put_kv(vb_ref, zvb)
        if rope_refs is not None:
            cos = rope_refs[0][rows, :]
            sin = rope_refs[1][rows, :]
            for j in range(W_B // LANES):
                q_ref[rows, W_A + LANES * j:W_A + LANES * (j + 1)] = (
                    _rope(zqb[:, LANES * j:LANES * (j + 1)], cos, sin) * Q_SCALE).astype(q_ref.dtype)
            put_kv(kb_ref, _rope(zkb, cos, sin))
        else:
            q_ref[rows, W_A:W_A + W_B] = (zqb * Q_SCALE).astype(q_ref.dtype)
            put_kv(kb_ref, zkb)


def _inproj_kernel(xp_ref, xs_ref, mod_ref, g_ref, w_ref, cos_ref, sin_ref, *refs, prompt_steps):
    outs_p, outs_s, w_scr = refs[0:5], refs[5:10], refs[10]
    i = pl.program_id(0)

    @pl.when(i == 0)
    def _():
        _prepare_in_weight(w_ref, w_scr)

    mod = mod_ref[0]

    @pl.when(i < prompt_steps)
    def _():
        _inproj_rows(xp_ref, mod, g_ref, w_scr, None, outs_p)

    @pl.when(i >= prompt_steps)
    def _():
        _inproj_rows(xs_ref, mod, g_ref, w_scr, (cos_ref, sin_ref), outs_s)


def _inproj_call(xp, xs, mod3, g_pre, w_in, rope_tabs, lp, ts):
    tm = TOKEN_TILE
    tp, tsamp = xp.shape[0], xs.shape[0]
    n_p, n_s = tp // tm, tsamp // tm
    assert n_p * tm == tp and n_s * tm == tsamp and tm % lp == 0 and ts % tm == 0
    tiles_per_seq = ts // tm
    p_tile = lambda i: jnp.minimum(i, n_p - 1)
    s_tile = lambda i: jnp.maximum(i - n_p, 0)
    mod_row = lambda i: jnp.where(i < n_p, 0, 1 + s_tile(i) // tiles_per_seq)
    nseq = tm // lp
    in_specs = [
        pl.BlockSpec((tm, D_MODEL), lambda i: (p_tile(i), 0)),
        pl.BlockSpec((tm, D_MODEL), lambda i: (s_tile(i), 0)),
        pl.BlockSpec((1, 1, 6 * D_MODEL), lambda i: (mod_row(i), 0, 0)),
        pl.BlockSpec((1, D_MODEL), lambda i: (0, 0)),
        _resident((D_MODEL, IN_WIDTH)),
        pl.BlockSpec((tm, LANES), lambda i: (s_tile(i) % tiles_per_seq, 0)),
        pl.BlockSpec((tm, LANES), lambda i: (s_tile(i) % tiles_per_seq, 0)),
    ]
    kvt_shape = lambda w: jax.ShapeDtypeStruct((tp // lp, w, lp), _F32)
    kvt_spec = lambda w: pl.BlockSpec((nseq, w, lp), lambda i: (p_tile(i), 0, 0))
    kv_shape = lambda w: jax.ShapeDtypeStruct((tsamp, w), _BF16)
    kv_spec = lambda w: pl.BlockSpec((tm, w), lambda i: (s_tile(i), 0))
    out_shape = (jax.ShapeDtypeStruct((tp, W_A + W_B), _BF16),
                 kvt_shape(W_A), kvt_shape(W_A), kvt_shape(KV_W_B), kvt_shape(KV_W_B),
                 jax.ShapeDtypeStruct((tsamp, W_A + W_B), _BF16),
                 kv_shape(W_A), kv_shape(W_A), kv_shape(KV_W_B), kv_shape(KV_W_B))
    out_specs = (pl.BlockSpec((tm, W_A + W_B), lambda i: (p_tile(i), 0)),
                 kvt_spec(W_A), kvt_spec(W_A), kvt_spec(KV_W_B), kvt_spec(KV_W_B),
                 pl.BlockSpec((tm, W_A + W_B), lambda i: (s_tile(i), 0)),
                 kv_spec(W_A), kv_spec(W_A), kv_spec(KV_W_B), kv_spec(KV_W_B))
    outs = pl.pallas_call(
        functools.partial(_inproj_kernel, prompt_steps=n_p),
        grid=(n_p + n_s,),
        in_specs=in_specs,
        out_specs=out_specs,
        out_shape=out_shape,
        scratch_shapes=[pltpu.VMEM((D_MODEL, IN_WIDTH), _BF16)],
        compiler_params=pltpu.CompilerParams(
            dimension_semantics=("arbitrary",), vmem_limit_bytes=VMEM_LIMIT),
        name="inproj",
    )(xp, xs, mod3, g_pre.reshape(1, D_MODEL), w_in, *rope_tabs)
    return outs[:5], outs[5:]


def _ctx_attn_kernel(sink_ref, q_ref, ka_ref, va_ref, kb_ref, vb_ref, ga_ref, gb_ref, wu_ref, wd_ref,
                     a_ref, b_ref, wu_out, wd_out):
    wu_out[...] = wu_ref[...].astype(wu_out.dtype)
    wd_out[...] = wd_ref[...].astype(wd_out.dtype)
    l = q_ref.shape[0]
    lo = lax.broadcasted_iota(jnp.int32, (l, LANES), 1) < HEAD_DIM
    row_lo = lax.broadcasted_iota(jnp.int32, (2 * l, 1), 0) < l
    oa = []
    for p in range(W_A // LANES):
        sl = slice(LANES * p, LANES * (p + 1))
        qs = _stack_heads(q_ref[:, sl], lo)
        s = _dot(qs, ka_ref[0, sl, :].astype(_BF16))
        vt = va_ref[0, sl, :].astype(_BF16)
        oa.append(_unstack_heads(_softmax_pv([s], [lambda e, vt=vt: _dot_nt(e, vt)]), lo))
    a_ref[...] = _rms(jnp.concatenate(oa, axis=1), ga_ref[...]).astype(a_ref.dtype)
    kb = kb_ref[0].astype(_BF16)
    vb = vb_ref[0].astype(_BF16)
    qs = jnp.concatenate(
        [_stack_heads(q_ref[:, W_A + LANES * j:W_A + LANES * (j + 1)], lo) for j in range(G_B)], axis=0)
    sink = jnp.concatenate(
        [jnp.where(row_lo, sink_ref[0, j], sink_ref[1, j]) for j in range(G_B)], axis=0)
    o2 = _softmax_pv([_dot(qs, kb)], [lambda e: _dot_nt(e, vb)], sink)
    ob = [_unstack_heads(o2[2 * l * j:2 * l * (j + 1)], lo) for j in range(G_B)]
    b_ref[...] = _rms(jnp.concatenate(ob, axis=1), gb_ref[...]).astype(b_ref.dtype)


def _ctx_attn_call(sink, q, ka, va, kb, vb, g_a, g_b, w_up, w_down, seq):
    t = q.shape[0]
    steps = t // seq
    ru, rd = w_up.shape[0] // steps, w_down.shape[0] // steps
    assert ru * steps == w_up.shape[0] and rd * steps == w_down.shape[0] and ru % 16 == 0 and rd % 16 == 0
    row = lambda b: (b, 0)
    seq3 = lambda b: (b, 0, 0)
    const = lambda b: (0, 0)
    return pl.pallas_call(
        _ctx_attn_kernel,
        grid=(steps,),
        in_specs=[
            pl.BlockSpec(memory_space=pltpu.SMEM),
            pl.BlockSpec((seq, W_A + W_B), row),
            pl.BlockSpec((1, W_A, seq), seq3),
            pl.BlockSpec((1, W_A, seq), seq3),
            pl.BlockSpec((1, KV_W_B, seq), seq3),
            pl.BlockSpec((1, KV_W_B, seq), seq3),
            pl.BlockSpec((1, W_A), const),
            pl.BlockSpec((1, W_B), const),
            pl.BlockSpec((ru, w_up.shape[1]), row),
            pl.BlockSpec((rd, w_down.shape[1]), row),
        ],
        out_specs=(pl.BlockSpec((seq, W_A), row), pl.BlockSpec((seq, W_B), row),
                   pl.BlockSpec((ru, w_up.shape[1]), row), pl.BlockSpec((rd, w_down.shape[1]), row)),
        out_shape=(jax.ShapeDtypeStruct((t, W_A), _BF16), jax.ShapeDtypeStruct((t, W_B), _BF16),
                   jax.ShapeDtypeStruct(w_up.shape, _BF16), jax.ShapeDtypeStruct(w_down.shape, _BF16)),
        compiler_params=pltpu.CompilerParams(
            dimension_semantics=("parallel",), vmem_limit_bytes=VMEM_LIMIT),
        name="ctx_attn",
    )(sink, q, ka, va, kb, vb, g_a.reshape(1, W_A), g_b.reshape(1, W_B), w_up, w_down)


def _na_plan(rows):
    kr = min(NA_ROWS, rows)
    win = kr + NA_QROWS
    row_start = lambda r: min(max(r - kr // 2, 0), rows - kr)
    blocks, classes = [], []
    for r0 in range(0, rows, NA_QROWS):
        ws = min(row_start(r0), rows - win)
        keys = []
        for r in range(r0, r0 + NA_QROWS):
            key = (ws - r + NA_ROWS - 1, row_start(r) - ws)
            if key not in classes:
                classes.append(key)
            keys.append(classes.index(key))
        blocks.append((r0, ws, keys))
    return kr, win, blocks, classes


def _na_kernel(q_ref, k_ref, v_ref, ck_ref, cv_ref, rpb_ref, ga_ref, a_ref, bias_ref, o_scr):
    rows = q_ref.shape[0] // GRID_W
    kr, win, blocks, classes = _na_plan(rows)
    m = NA_QROWS * GRID_W
    lane = lax.broadcasted_iota(jnp.int32, (GRID_W, LANES), 1)
    lo = lane < HEAD_DIM
    lo_q = lax.broadcasted_iota(jnp.int32, (m, LANES), 1) < HEAD_DIM
    cq = lax.broadcasted_iota(jnp.int32, (GRID_W, LANES), 0)
    ck = lane & (GRID_W - 1)
    cs = jnp.clip(cq - NA_COLS // 2, 0, GRID_W - NA_COLS)
    valid = (ck >= cs) & (ck < cs + NA_COLS)
    neg = jnp.full((GRID_W, LANES), NEG_INF, _F32)
    n_drow = 2 * NA_ROWS - 1

    for p in range(W_A // LANES):
        sl = slice(LANES * p, LANES * (p + 1))
        for hh in range(2):
            t_lo, t_hi = [], []
            for d in range(n_drow):
                base = jnp.broadcast_to(rpb_ref[2 * p + hh, d:d + 1, :], (GRID_W, LANES))
                t_lo.append(pltpu.roll(base, LANES - (NA_COLS - 1), 1, stride=1, stride_axis=0))
                t_hi.append(pltpu.roll(base, GRID_W - (NA_COLS - 1), 1, stride=1, stride_axis=0))
            tiles = {}
            for ci, (d_first, off) in enumerate(classes):
                for j in range(win // 2):
                    d_of = lambda i: d_first + i if off <= i < off + kr else None
                    key = (d_of(2 * j), d_of(2 * j + 1))
                    if key not in tiles:
                        left = neg if key[0] is None else t_lo[key[0]]
                        right = neg if key[1] is None else t_hi[key[1]]
                        tiles[key] = jnp.where(valid, jnp.where(lo, left, right), NEG_INF)
                    bias_ref[ci, GRID_W * hh:GRID_W * (hh + 1), LANES * j:LANES * (j + 1)] = tiles[key]

        ckt = ck_ref[0, sl, :].astype(_BF16)
        cvt = cv_ref[0, sl, :].astype(_BF16)

        for r0, ws, keys in blocks:
            q0, k0 = r0 * GRID_W, ws * GRID_W
            qs = _stack_heads(q_ref[q0:q0 + m, sl], lo_q)
            kw = k_ref[k0:k0 + win * GRID_W, sl]
            vw = v_ref[k0:k0 + win * GRID_W, sl]
            bias = jnp.concatenate(
                [bias_ref[ci, GRID_W * hh:GRID_W * (hh + 1), :] for hh in range(2) for ci in keys], axis=0)
            s_nb = _dot_nt(qs, kw) + bias
            s_cx = _dot(qs, ckt)
            o2 = _softmax_pv([s_nb, s_cx], [lambda e, vw=vw: _dot(e, vw), lambda e: _dot_nt(e, cvt)])
            o_scr[q0:q0 + m, sl] = _unstack_heads(o2, lo_q)

    a_ref[...] = _rms(o_scr[...], ga_ref[...]).astype(a_ref.dtype)


def _na_call(q, k, v, ckt, cvt, rpb_pad, g_a, nb, seq):
    past = ckt.shape[2]
    _, win, _, classes = _na_plan(seq // GRID_W)
    blk = lambda b: (b, 0)
    return pl.pallas_call(
        _na_kernel,
        grid=(nb,),
        in_specs=[
            pl.BlockSpec((seq, W_A), blk),
            pl.BlockSpec((seq, W_A), blk),
            pl.BlockSpec((seq, W_A), blk),
            pl.BlockSpec((1, W_A, past), lambda b: (b, 0, 0)),
            pl.BlockSpec((1, W_A, past), lambda b: (b, 0, 0)),
            pl.BlockSpec((H_A, 2 * NA_ROWS - 1, LANES), lambda b: (0, 0, 0)),
            pl.BlockSpec((1, W_A), lambda b: (0, 0)),
        ],
        out_specs=pl.BlockSpec((seq, W_A), blk),
        out_shape=jax.ShapeDtypeStruct((nb * seq, W_A), _BF16),
        scratch_shapes=[pltpu.VMEM((len(classes), 2 * GRID_W, win * GRID_W), _F32),
                        pltpu.VMEM((seq, W_A), _F32)],
        compiler_params=pltpu.CompilerParams(
            dimension_semantics=("parallel",), vmem_limit_bytes=VMEM_LIMIT),
        name="na_attn",
    )(q, k, v, ckt, cvt, rpb_pad, g_a.reshape(1, W_A))


def _win_kernel(sink_ref, q_ref, k_ref, v_ref, ck_ref, cv_ref, gb_ref, b_ref, o_scr):
    t = q_ref.shape[0]
    qb = WIN_QBLOCKS * BLOCK
    band = qb + 2 * BLOCK
    lo = lax.broadcasted_iota(jnp.int32, (qb, LANES), 1) < HEAD_DIM
    row_lo = lax.broadcasted_iota(jnp.int32, (2 * qb, 1), 0) < qb
    qi = lax.broadcasted_iota(jnp.int32, (2 * qb, band), 0) & (qb - 1)
    kj = lax.broadcasted_iota(jnp.int32, (2 * qb, band), 1)
    rel = qi - kj
    ckt = ck_ref[0].astype(_BF16)
    cvt = cv_ref[0].astype(_BF16)

    starts = {q0: min(max(q0 - BLOCK, 0), t - band) for q0 in range(0, t, qb)}
    masks = {off: jnp.where(jnp.abs(rel + off) <= SWA_WINDOW, 0.0, NEG_INF)
             for off in sorted({q0 - k0 for q0, k0 in starts.items()})}

    for j in range(G_B):
        sl = slice(LANES * j, LANES * (j + 1))
        sink = jnp.where(row_lo, sink_ref[0, j], sink_ref[1, j])
        for q0, k0 in starts.items():
            qs = _stack_heads(q_ref[q0:q0 + qb, sl], lo)
            s = _dot_nt(qs, k_ref[k0:k0 + band, :]) + masks[q0 - k0]
            s_cx = _dot(qs, ckt)
            vband = v_ref[k0:k0 + band, :]
            o2 = _softmax_pv([s, s_cx], [lambda e, vband=vband: _dot(e, vband), lambda e: _dot_nt(e, cvt)],
                             sink)
            o_scr[q0:q0 + qb, sl] = _unstack_heads(o2, lo)

    b_ref[...] = _rms(o_scr[...], gb_ref[...]).astype(b_ref.dtype)


def _win_call(sink, q, k, v, ckt, cvt, g_b, nb, seq):
    past = ckt.shape[2]
    return pl.pallas_call(
        _win_kernel,
        grid=(nb,),
        in_specs=[
            pl.BlockSpec(memory_space=pltpu.SMEM),
            pl.BlockSpec((seq, W_B), lambda b: (b, W_A // W_B)),
            pl.BlockSpec((seq, KV_W_B), lambda b: (b, 0)),
            pl.BlockSpec((seq, KV_W_B), lambda b: (b, 0)),
            pl.BlockSpec((1, KV_W_B, past), lambda b: (b, 0, 0)),
            pl.BlockSpec((1, KV_W_B, past), lambda b: (b, 0, 0)),
            pl.BlockSpec((1, W_B), lambda b: (0, 0)),
        ],
        out_specs=pl.BlockSpec((seq, W_B), lambda b: (b, 0)),
        out_shape=jax.ShapeDtypeStruct((nb * seq, W_B), _BF16),
        scratch_shapes=[pltpu.VMEM((seq, W_B), _F32)],
        compiler_params=pltpu.CompilerParams(
            dimension_semantics=("parallel",), vmem_limit_bytes=VMEM_LIMIT),
        name="win_attn",
    )(sink, q, k, v, ckt, cvt, g_b.reshape(1, W_B))


def _prepare_out_weight(w_ref, w_scr):
    for r0 in range(0, W_A, CAST_COLS):
        w_scr[r0:r0 + CAST_COLS, :] = w_ref[r0:r0 + CAST_COLS, :].astype(_BF16)
    for i, h in enumerate(_QB_HEAD_ORDER):
        w_scr[W_A + HEAD_DIM * i:W_A + HEAD_DIM * (i + 1), :] = (
            w_ref[W_A + HEAD_DIM * h:W_A + HEAD_DIM * (h + 1), :].astype(_BF16))


def _zero_row(x, row):
    r0 = row - row % 8
    sub = lax.broadcasted_iota(jnp.int32, (8, 1), 0)
    slab = jnp.where(sub == row % 8, 0.0, x[r0:r0 + 8])
    parts = ([x[:r0]] if r0 else []) + [slab] + ([x[r0 + 8:]] if r0 + 8 < x.shape[0] else [])
    return jnp.concatenate(parts, axis=0)


def _conv3(u, cw, cb, seq_len):
    tm = u.shape[0]
    prev = pltpu.roll(u, 1, 0)
    nxt = pltpu.roll(u, tm - 1, 0)
    for s0 in range(0, tm, seq_len):
        prev = _zero_row(prev, s0)
        nxt = _zero_row(nxt, s0 + seq_len - 1)
    return prev * cw[0:1, :] + u * cw[1:2, :] + nxt * cw[2:3, :] + cb


def _gated_silu(gate, val):
    half = 0.5 * gate
    return ((half + half * jnp.tanh(half)) * val).astype(_BF16)


def _up(h2_ref, w):
    tm = h2_ref.shape[0]
    return jnp.concatenate(
        [_dot(h2_ref[r:r + FFN_ROW_BLOCK, :], w) for r in range(0, tm, FFN_ROW_BLOCK)], axis=0)


def _out_proj(a_ref, b_ref, x_ref, mod, gpost_ref, gffn_ref, wo_scr, x1_ref, h2_ref):
    gt1 = mod[:, 2 * D_MODEL:3 * D_MODEL]
    sh2 = mod[:, 3 * D_MODEL:4 * D_MODEL]
    sc2 = mod[:, 4 * D_MODEL:5 * D_MODEL]
    for r in range(0, x_ref.shape[0], FFN_ROW_BLOCK):
        rows = slice(r, r + FFN_ROW_BLOCK)
        y = _dot(a_ref[rows, :], wo_scr[0:W_A, :]) + _dot(b_ref[rows, :], wo_scr[W_A:W_A + W_B, :])
        x1 = x_ref[rows, :] + gt1 * _rms(y, gpost_ref[...])
        x1_ref[rows, :] = x1
        h2_ref[rows, :] = (_rms(x1, gffn_ref[...]) * (1 + sc2) + sh2).astype(h2_ref.dtype)


def _down_proj(act_ref, wd_ref, mod, gpost2_ref, x1_ref, out_ref):
    gt2 = mod[:, 5 * D_MODEL:6 * D_MODEL]
    for r in range(0, act_ref.shape[0], DOWN_ROW_BLOCK):
        rows = slice(r, r + DOWN_ROW_BLOCK)
        y = _dot(act_ref[rows, :], wd_ref[...])
        out_ref[rows, :] = x1_ref[rows, :] + gt2 * _rms(y, gpost2_ref[...])


def _tail_seq_kernel(a_ref, b_ref, x_ref, mod_ref, gpost_ref, gffn_ref, gpost2_ref, wo_ref, wu_ref,
                     cw_ref, cb_ref, wd_ref, out_ref, wo_scr, h2_scr, act_scr, *, seq_len):
    mod = mod_ref[0]

    @pl.when(pl.program_id(0) == 0)
    def _():
        _prepare_out_weight(wo_ref, wo_scr)

    _out_proj(a_ref, b_ref, x_ref, mod, gpost_ref, gffn_ref, wo_scr, out_ref, h2_scr)
    tc = FF_CHUNK

    def chunk(col0):
        cols = slice(col0, col0 + tc)
        return _conv3(_up(h2_scr, wu_ref[:, cols]), cw_ref[:, cols], cb_ref[:, cols], seq_len)

    for c0 in range(0, D_FF, tc):
        act_scr[:, c0:c0 + tc] = _gated_silu(chunk(c0), chunk(D_FF + c0))
    _down_proj(act_scr, wd_ref, mod, gpost2_ref, out_ref, out_ref)


def _tail_seq_call(a, b, x2d, mod3, mod_row, g_post, g_ffn, g_post2, w_out, w_up, conv_w, conv_b, w_down,
                   seq_len, name):
    t = x2d.shape[0]
    tm = max(SEQ_TAIL_TILE, seq_len)
    big = tm > SEQ_TAIL_TILE
    vmem_limit = TAIL_VMEM_LIMIT if big else VMEM_LIMIT
    x_mode = dict(pipeline_mode=pl.Buffered(1)) if big else {}
    row = lambda i: (i, 0)
    const = lambda i: (0, 0)
    return pl.pallas_call(
        functools.partial(_tail_seq_kernel, seq_len=seq_len),
        grid=(t // tm,),
        in_specs=[
            pl.BlockSpec((tm, W_A), row),
            pl.BlockSpec((tm, W_B), row),
            pl.BlockSpec((tm, D_MODEL), row, **x_mode),
            pl.BlockSpec((1, 1, 6 * D_MODEL), lambda i: (mod_row(i), 0, 0)),
            pl.BlockSpec((1, D_MODEL), const),
            pl.BlockSpec((1, D_MODEL), const),
            pl.BlockSpec((1, D_MODEL), const),
            _resident((W_A + W_B, D_MODEL)),
            _resident((D_MODEL, 2 * D_FF)),
            _resident((3, 2 * D_FF)),
            _resident((1, 2 * D_FF)),
            _resident((D_FF, D_MODEL)),
        ],
        out_specs=pl.BlockSpec((tm, D_MODEL), row),
        out_shape=jax.ShapeDtypeStruct((t, D_MODEL), _F32),
        scratch_shapes=[
            pltpu.VMEM((W_A + W_B, D_MODEL), _BF16),
            pltpu.VMEM((tm, D_MODEL), _BF16),
            pltpu.VMEM((tm, D_FF), _BF16),
        ],
        compiler_params=pltpu.CompilerParams(
            dimension_semantics=("arbitrary",), vmem_limit_bytes=vmem_limit),
        name=name,
    )(a, b, x2d, mod3, g_post.reshape(1, D_MODEL), g_ffn.reshape(1, D_MODEL), g_post2.reshape(1, D_MODEL),
      w_out, w_up, conv_w, conv_b.reshape(1, 2 * D_FF), w_down)


def _rope_tables(seq):
    n = ROPE_HALF
    t = np.arange(seq)
    lane = np.arange(HEAD_DIM)
    pos = np.where(lane[None, :] < HEAD_DIM // 2, (t // GRID_W)[:, None], (t % GRID_W)[:, None])
    inv = 1.0 / (ROPE_THETA ** (np.arange(n, dtype=np.float64) / n))
    ang = pos.astype(np.float64) * inv[lane % n][None, :]
    sign = np.where((lane & n) == 0, -1.0, 1.0)[None, :]
    cos = np.tile(np.cos(ang), (1, LANES // HEAD_DIM)).astype(np.float32)
    sin = np.tile(np.sin(ang) * sign, (1, LANES // HEAD_DIM)).astype(np.float32)
    return jnp.asarray(cos), jnp.asarray(sin)


def _to_head_dim_token(cache_l):
    b, l, h, d = cache_l.shape
    return jnp.transpose(cache_l, (0, 2, 3, 1)).reshape(b, h * d, l)


def _from_head_dim_token(kv_t, heads):
    b, _, l = kv_t.shape
    return jnp.transpose(kv_t.reshape(b, 1, heads, HEAD_DIM, l), (0, 1, 4, 2, 3))


def kernel(x_prompt, x_sample, cache_a_k, cache_a_v, cache_b_k, cache_b_v, c, c_ctx, w_mod, b_mod,
           g_mix_pre, g_mix_post, g_ffn_pre, g_ffn_post, w_in, rpb_a, sink_b, g_grp_a, g_grp_b,
           w_out, w_up, conv_w, conv_b, w_down):
    bp, lp, _ = x_prompt.shape
    bs, ts, _ = x_sample.shape
    assert w_in.shape[0] == 1
    xp = x_prompt.reshape(bp * lp, D_MODEL)
    xs = x_sample.reshape(bs * ts, D_MODEL)
    cond8 = jnp.concatenate([c_ctx[None], c, jnp.zeros((8 - 1 - bs, D_MODEL), _F32)], axis=0)
    cos_tab, sin_tab = _rope_tables(ts)

    l = 0
    mod3 = _mod_call(cond8, w_mod[l], b_mod[l]).reshape(8, 1, 6 * D_MODEL)
    g_b_p = g_grp_b[l].reshape(KV_B, G_B, HEAD_DIM).transpose(1, 0, 2).reshape(W_B)
    rpb_pad = jnp.pad(rpb_a[l], ((0, 0), (0, 0), (0, LANES - (2 * NA_COLS - 1))))

    prompt_row = lambda i: 0
    assert ts >= SEQ_TAIL_TILE
    sample_row = lambda i: 1 + i

    (q_p, kat_p, vat_p, kbt_p, vbt_p), (q_s, ka_s, va_s, kb_s, vb_s) = _inproj_call(
        xp, xs, mod3, g_mix_pre[l], w_in[l], (cos_tab, sin_tab), lp, ts)

    a_p, b_p, w_up_b, w_down_b = _ctx_attn_call(sink_b[l], q_p, kat_p, vat_p, kbt_p, vbt_p, g_grp_a[l], g_b_p,
                                                w_up[l], w_down[l], lp)
    y_p = _tail_seq_call(a_p, b_p, xp, mod3, prompt_row, g_mix_post[l], g_ffn_pre[l], g_ffn_post[l], w_out[l],
                         w_up_b, conv_w[l], conv_b[l], w_down_b, lp, "tail_prompt")

    a_s = _na_call(q_s, ka_s, va_s, _to_head_dim_token(cache_a_k[:, l]), _to_head_dim_token(cache_a_v[:, l]),
                   rpb_pad, g_grp_a[l], bs, ts)
    b_s = _win_call(sink_b[l], q_s, kb_s, vb_s, _to_head_dim_token(cache_b_k[:, l]),
                    _to_head_dim_token(cache_b_v[:, l]), g_b_p, bs, ts)
    y_s = _tail_seq_call(a_s, b_s, xs, mod3, sample_row, g_mix_post[l], g_ffn_pre[l], g_ffn_post[l], w_out[l],
                         w_up_b, conv_w[l], conv_b[l], w_down_b, ts, "tail_sample")

    return (y_p.reshape(bp, lp, D_MODEL), y_s.reshape(bs, ts, D_MODEL),
            _from_head_dim_token(kat_p, H_A), _from_head_dim_token(vat_p, H_A),
            _from_head_dim_token(kbt_p, KV_B), _from_head_dim_token(vbt_p, KV_B))
```

```python
import functools

import numpy as np
import jax
import jax.numpy as jnp
from jax import lax
from jax.experimental import pallas as pl
from jax.experimental.pallas import tpu as pltpu

D_MODEL = 1024
HEAD_DIM = 64
H_A = 8
H_B = 8
KV_B = 2
G_B = H_B // KV_B
W_A = H_A * HEAD_DIM
W_B = H_B * HEAD_DIM
KV_W_B = KV_B * HEAD_DIM
IN_WIDTH = 3 * W_A + W_B + 2 * KV_W_B
GRID_W = 64
NA_ROWS = 8
NA_COLS = 16
SWA_WINDOW = 128
BLOCK = 128
D_FF = 2816
ROPE_THETA = 10000.0
EPS = 1e-6
NEG_INF = -1e30
Q_SCALE = HEAD_DIM ** -0.5
ROPE_HALF = HEAD_DIM // 4

LANES = 128
FF_CHUNK = 256
TOKEN_TILE = 512
FFN_ROW_BLOCK = 256
DOWN_ROW_BLOCK = 256
SEQ_TAIL_TILE = 512
PROJ_ROW_BLOCK = 256
MOD_STREAMS = 2
MOD_ROW_BLOCK = 128
NA_QROWS = 4
WIN_QBLOCKS = 2
CAST_COLS = 512
VMEM_LIMIT = 48 * 1024 * 1024
TAIL_VMEM_LIMIT = 56 * 1024 * 1024

C_QA, C_KA, C_VA, C_QB, C_KB, C_VB = 0, W_A, 2 * W_A, 3 * W_A, 3 * W_A + W_B, 3 * W_A + W_B + KV_W_B

_QB_HEAD_ORDER = tuple(kv * G_B + g for g in range(G_B) for kv in range(KV_B))

_BF16 = jnp.bfloat16
_F32 = jnp.float32


def _dot(a, b):
    return jnp.dot(a, b, preferred_element_type=_F32)


def _dot_nt(a, b):
    return lax.dot_general(a, b, (((1,), (1,)), ((), ())), preferred_element_type=_F32)


def _rms(x, g):
    var = jnp.mean(x * x, axis=-1, keepdims=True)
    return x * lax.rsqrt(var + EPS) * g


def _stack_heads(q2, lo):
    zero = jnp.zeros_like(q2)
    return jnp.concatenate([jnp.where(lo, q2, zero), jnp.where(lo, zero, q2)], axis=0)


def _unstack_heads(o2, lo):
    m = o2.shape[0] // 2
    return jnp.where(lo, o2[:m], o2[m:])


def _softmax_pv(scores, pv, sink=None):
    m = scores[0].max(axis=-1, keepdims=True)
    for s in scores[1:]:
        m = jnp.maximum(m, s.max(axis=-1, keepdims=True))
    if sink is not None:
        m = jnp.maximum(m, sink)
    denom = None
    out = None
    for s, f in zip(scores, pv):
        e = jnp.exp(s - m)
        l = e.sum(axis=-1, keepdims=True)
        o = f(e.astype(_BF16))
        denom = l if denom is None else denom + l
        out = o if out is None else out + o
    if sink is not None:
        denom = denom + jnp.exp(sink - m)
    return out / denom


def _resident(shape):
    return pl.BlockSpec(shape, lambda *_: (0,) * len(shape), pipeline_mode=pl.Buffered(1))


def _mod_kernel(c_ref, *refs):
    w_refs, b_ref, o_ref = refs[:-2], refs[-2], refs[-1]
    i = pl.program_id(0)

    @pl.when(i == 0)
    def _():
        o_ref[...] = jnp.broadcast_to(b_ref[...], o_ref.shape)

    acc = o_ref[...]
    for j, w_ref in enumerate(w_refs):
        c = c_ref[i * len(w_refs) + j]
        s = (c * jax.nn.sigmoid(c)).astype(_BF16)
        acc = acc + _dot(s, w_ref[...].astype(_BF16))
    o_ref[...] = acc


def _mod_call(cond8, w_mod, b_mod):
    n = w_mod.shape[1]
    kb = MOD_ROW_BLOCK
    nblk = D_MODEL // kb
    cond_blocks = cond8.reshape(8, nblk, kb).transpose(1, 0, 2)
    return pl.pallas_call(
        _mod_kernel,
        grid=(nblk // MOD_STREAMS,),
        in_specs=[pl.BlockSpec((nblk, 8, kb), lambda i: (0, 0, 0))]
        + [pl.BlockSpec((kb, n), lambda i, j=j: (i * MOD_STREAMS + j, 0)) for j in range(MOD_STREAMS)]
        + [pl.BlockSpec((1, n), lambda i: (0, 0))],
        out_specs=pl.BlockSpec((8, n), lambda i: (0, 0)),
        out_shape=jax.ShapeDtypeStruct((8, n), _F32),
        compiler_params=pltpu.CompilerParams(dimension_semantics=("arbitrary",), vmem_limit_bytes=VMEM_LIMIT),
        name="mod",
    )(cond_blocks, *([w_mod] * MOD_STREAMS), b_mod.reshape(1, n))


def _prepare_in_weight(w_ref, w_scr):
    for c0 in list(range(0, C_QB, CAST_COLS)) + [C_KB]:
        n = min(CAST_COLS, IN_WIDTH - c0)
        w_scr[:, c0:c0 + n] = w_ref[:, c0:c0 + n].astype(_BF16)
    lo = lax.broadcasted_iota(jnp.int32, (D_MODEL, LANES), 1) < HEAD_DIM
    for j in range(G_B):
        h_lo, h_hi = _QB_HEAD_ORDER[2 * j], _QB_HEAD_ORDER[2 * j + 1]

        def head_at(h, want_hi):
            blk = w_ref[:, C_QB + LANES * (h // 2):C_QB + LANES * (h // 2 + 1)]
            return blk if (h % 2 == 1) == want_hi else pltpu.roll(blk, HEAD_DIM, 1)

        w_scr[:, C_QB + LANES * j:C_QB + LANES * (j + 1)] = jnp.where(
            lo, head_at(h_lo, False), head_at(h_hi, True)).astype(_BF16)


def _rope(z, cos, sin):
    lane = lax.broadcasted_iota(jnp.int32, z.shape, 1)
    partner = jnp.where((lane & ROPE_HALF) == 0,
                        pltpu.roll(z, LANES - ROPE_HALF, 1), pltpu.roll(z, ROPE_HALF, 1))
    return z * cos + partner * sin


def _inproj_rows(x_ref, mod, g_ref, w_scr, rope_refs, outs):
    q_ref, ka_ref, va_ref, kb_ref, vb_ref = outs
    kv_transposed = rope_refs is None
    sh1 = mod[:, 0:D_MODEL]
    sc1 = mod[:, D_MODEL:2 * D_MODEL]
    rb = PROJ_ROW_BLOCK

    for r in range(0, x_ref.shape[0], rb):
        rows = slice(r, r + rb)
        h = (_rms(x_ref[rows, :], g_ref[...]) * (1 + sc1) + sh1).astype(_BF16)

        def proj(c0, n):
            return _dot(h, w_scr[:, c0:c0 + n])

        def put_kv(ref, z):
            if not kv_transposed:
                ref[rows, :] = z.astype(ref.dtype)
                return
            l = ref.shape[2]
            for t0 in range(0, rb, l):
                for c0 in range(0, z.shape[1], LANES):
                    ref[(r + t0) // l, c0:c0 + LANES, :] = z[t0:t0 + l, c0:c0 + LANES].T.astype(ref.dtype)

        q_ref[rows, 0:W_A] = (proj(C_QA, W_A) * Q_SCALE).astype(q_ref.dtype)
        put_kv(ka_ref, proj(C_KA, W_A))
        put_kv(va_ref, proj(C_VA, W_A))
        zkv = proj(C_KB, 2 * KV_W_B)
        zkb, zvb = zkv[:, :KV_W_B], zkv[:, KV_W_B:]
        zqb = proj(C_QB, W_B)
        put_kv(vb_ref, zvb)
        if rope_refs is not None:
            cos = rope_refs[0][rows, :]
            sin = rope_refs[1][rows, :]
            for j in range(W_B // LANES):
                q_ref[rows, W_A + LANES * j:W_A + LANES * (j + 1)] = (
                    _rope(zqb[:, LANES * j:LANES * (j + 1)], cos, sin) * Q_SCALE).astype(q_ref.dtype)
            put_kv(kb_ref, _rope(zkb, cos, sin))
        else:
            q_ref[rows, W_A:W_A + W_B] = (zqb * Q_SCALE).astype(q_ref.dtype)
            put_kv(kb_ref, zkb)


def _ctx_attention(q_ref, rows, seq, ka_ref, va_ref, kb_ref, vb_ref, sink_ref, ga_ref, gb_ref, a_ref, b_ref):
    l = rows.stop - rows.start
    lo = lax.broadcasted_iota(jnp.int32, (l, LANES), 1) < HEAD_DIM
    row_lo = lax.broadcasted_iota(jnp.int32, (2 * l, 1), 0) < l
    oa = []
    for p in range(W_A // LANES):
        sl = slice(LANES * p, LANES * (p + 1))
        qs = _stack_heads(q_ref[rows, sl], lo)
        s = _dot(qs, ka_ref[seq, sl, :].astype(_BF16))
        vt = va_ref[seq, sl, :].astype(_BF16)
        oa.append(_unstack_heads(_softmax_pv([s], [lambda e, vt=vt: _dot_nt(e, vt)]), lo))
    a_ref[rows, :] = _rms(jnp.concatenate(oa, axis=1), ga_ref[...]).astype(a_ref.dtype)
    kb = kb_ref[seq].astype(_BF16)
    vb = vb_ref[seq].astype(_BF16)
    ob = []
    for j in range(G_B):
        qs = _stack_heads(q_ref[rows, W_A + LANES * j:W_A + LANES * (j + 1)], lo)
        s = _dot(qs, kb)
        sink = jnp.where(row_lo, sink_ref[0, j], sink_ref[1, j])
        ob.append(_unstack_heads(_softmax_pv([s], [lambda e: _dot_nt(e, vb)], sink), lo))
    b_ref[rows, :] = _rms(jnp.concatenate(ob, axis=1), gb_ref[...]).astype(b_ref.dtype)


def _front_kernel(sink_ref, xp_ref, xs_ref, mod_ref, g_ref, w_ref, cos_ref, sin_ref, ga_ref, gb_ref,
                  wu_ref, wd_ref, *refs, prompt_steps):
    kat_ref, vat_ref, kbt_ref, vbt_ref, a_ref, b_ref, wu_out, wd_out = refs[0:8]
    outs_s, w_scr, q_scr = refs[8:13], refs[13], refs[14]
    i = pl.program_id(0)

    @pl.when(i == 0)
    def _():
        _prepare_in_weight(w_ref, w_scr)

    mod = mod_ref[0]

    @pl.when(i < prompt_steps)
    def _():
        wu_out[...] = wu_ref[...].astype(wu_out.dtype)
        wd_out[...] = wd_ref[...].astype(wd_out.dtype)
        _inproj_rows(xp_ref, mod, g_ref, w_scr, None, (q_scr, kat_ref, vat_ref, kbt_ref, vbt_ref))
        l = kat_ref.shape[2]
        for seq in range(kat_ref.shape[0]):
            _ctx_attention(q_scr, slice(seq * l, (seq + 1) * l), seq, kat_ref, vat_ref, kbt_ref, vbt_ref,
                           sink_ref, ga_ref, gb_ref, a_ref, b_ref)

    @pl.when(i >= prompt_steps)
    def _():
        _inproj_rows(xs_ref, mod, g_ref, w_scr, (cos_ref, sin_ref), outs_s)


def _front_call(sink, xp, xs, mod3, g_pre, w_in, rope_tabs, g_a, g_b, w_up, w_down, lp, ts):
    tm = TOKEN_TILE
    tp, tsamp = xp.shape[0], xs.shape[0]
    n_p, n_s = tp // tm, tsamp // tm
    assert n_p * tm == tp and n_s * tm == tsamp and tm % lp == 0 and ts % tm == 0
    ru, rd = w_up.shape[0] // n_p, w_down.shape[0] // n_p
    assert ru * n_p == w_up.shape[0] and rd * n_p == w_down.shape[0] and ru % 16 == 0 and rd % 16 == 0
    tiles_per_seq = ts // tm
    p_tile = lambda i: jnp.minimum(i, n_p - 1)
    s_tile = lambda i: jnp.maximum(i - n_p, 0)
    mod_row = lambda i: jnp.where(i < n_p, 0, 1 + s_tile(i) // tiles_per_seq)
    p_row = lambda i: (p_tile(i), 0)
    s_row = lambda i: (s_tile(i), 0)
    const = lambda i: (0, 0)
    nseq = tm // lp
    in_specs = [
        pl.BlockSpec(memory_space=pltpu.SMEM),
        pl.BlockSpec((tm, D_MODEL), p_row),
        pl.BlockSpec((tm, D_MODEL), s_row),
        pl.BlockSpec((1, 1, 6 * D_MODEL), lambda i: (mod_row(i), 0, 0)),
        pl.BlockSpec((1, D_MODEL), const),
        _resident((D_MODEL, IN_WIDTH)),
        pl.BlockSpec((tm, LANES), lambda i: (s_tile(i) % tiles_per_seq, 0)),
        pl.BlockSpec((tm, LANES), lambda i: (s_tile(i) % tiles_per_seq, 0)),
        pl.BlockSpec((1, W_A), const),
        pl.BlockSpec((1, W_B), const),
        pl.BlockSpec((ru, w_up.shape[1]), p_row),
        pl.BlockSpec((rd, w_down.shape[1]), p_row),
    ]
    kvt_shape = lambda w: jax.ShapeDtypeStruct((tp // lp, w, lp), _F32)
    kvt_spec = lambda w: pl.BlockSpec((nseq, w, lp), lambda i: (p_tile(i), 0, 0))
    kv_shape = lambda w: jax.ShapeDtypeStruct((tsamp, w), _BF16)
    kv_spec = lambda w: pl.BlockSpec((tm, w), s_row)
    out_shape = (kvt_shape(W_A), kvt_shape(W_A), kvt_shape(KV_W_B), kvt_shape(KV_W_B),
                 jax.ShapeDtypeStruct((tp, W_A), _BF16), jax.ShapeDtypeStruct((tp, W_B), _BF16),
                 jax.ShapeDtypeStruct(w_up.shape, _BF16), jax.ShapeDtypeStruct(w_down.shape, _BF16),
                 jax.ShapeDtypeStruct((tsamp, W_A + W_B), _BF16),
                 kv_shape(W_A), kv_shape(W_A), kv_shape(KV_W_B), kv_shape(KV_W_B))
    out_specs = (kvt_spec(W_A), kvt_spec(W_A), kvt_spec(KV_W_B), kvt_spec(KV_W_B),
                 pl.BlockSpec((tm, W_A), p_row), pl.BlockSpec((tm, W_B), p_row),
                 pl.BlockSpec((ru, w_up.shape[1]), p_row), pl.BlockSpec((rd, w_down.shape[1]), p_row),
                 pl.BlockSpec((tm, W_A + W_B), s_row),
                 kv_spec(W_A), kv_spec(W_A), kv_spec(KV_W_B), kv_spec(KV_W_B))
    outs = pl.pallas_call(
        functools.partial(_front_kernel, prompt_steps=n_p),
        grid=(n_p + n_s,),
        in_specs=in_specs,
        out_specs=out_specs,
        out_shape=out_shape,
        scratch_shapes=[pltpu.VMEM((D_MODEL, IN_WIDTH), _BF16), pltpu.VMEM((tm, W_A + W_B), _BF16)],
        compiler_params=pltpu.CompilerParams(
            dimension_semantics=("arbitrary",), vmem_limit_bytes=TAIL_VMEM_LIMIT),
        name="front",
    )(sink, xp, xs, mod3, g_pre.reshape(1, D_MODEL), w_in, *rope_tabs, g_a.reshape(1, W_A),
      g_b.reshape(1, W_B), w_up, w_down)
    return outs[:6], outs[6:8], outs[8:]


def _na_plan(rows):
    kr = min(NA_ROWS, rows)
    win = kr + NA_QROWS
    row_start = lambda r: min(max(r - kr // 2, 0), rows - kr)
    blocks, classes = [], []
    for r0 in range(0, rows, NA_QROWS):
        ws = min(row_start(r0), rows - win)
        keys = []
        for r in range(r0, r0 + NA_QROWS):
            key = (ws - r + NA_ROWS - 1, row_start(r) - ws)
            if key not in classes:
                classes.append(key)
            keys.append(classes.index(key))
        blocks.append((r0, ws, keys))
    return kr, win, blocks, classes


def _na_kernel(q_ref, k_ref, v_ref, ck_ref, cv_ref, rpb_ref, ga_ref, a_ref, bias_ref, o_scr):
    rows = q_ref.shape[0] // GRID_W
    kr, win, blocks, classes = _na_plan(rows)
    m = NA_QROWS * GRID_W
    lane = lax.broadcasted_iota(jnp.int32, (GRID_W, LANES), 1)
    lo = lane < HEAD_DIM
    lo_q = lax.broadcasted_iota(jnp.int32, (m, LANES), 1) < HEAD_DIM
    cq = lax.broadcasted_iota(jnp.int32, (GRID_W, LANES), 0)
    ck = lane & (GRID_W - 1)
    cs = jnp.clip(cq - NA_COLS // 2, 0, GRID_W - NA_COLS)
    valid = (ck >= cs) & (ck < cs + NA_COLS)
    neg = jnp.full((GRID_W, LANES), NEG_INF, _F32)
    n_drow = 2 * NA_ROWS - 1

    for p in range(W_A // LANES):
        sl = slice(LANES * p, LANES * (p + 1))
        for hh in range(2):
            t_lo, t_hi = [], []
            for d in range(n_drow):
                base = jnp.broadcast_to(rpb_ref[2 * p + hh, d:d + 1, :], (GRID_W, LANES))
                t_lo.append(pltpu.roll(base, LANES - (NA_COLS - 1), 1, stride=1, stride_axis=0))
                t_hi.append(pltpu.roll(base, GRID_W - (NA_COLS - 1), 1, stride=1, stride_axis=0))
            tiles = {}
            for ci, (d_first, off) in enumerate(classes):
                for j in range(win // 2):
                    d_of = lambda i: d_first + i if off <= i < off + kr else None
                    key = (d_of(2 * j), d_of(2 * j + 1))
                    if key not in tiles:
                        left = neg if key[0] is None else t_lo[key[0]]
                        right = neg if key[1] is None else t_hi[key[1]]
                        tiles[key] = jnp.where(valid, jnp.where(lo, left, right), NEG_INF)
                    bias_ref[ci, GRID_W * hh:GRID_W * (hh + 1), LANES * j:LANES * (j + 1)] = tiles[key]

        ckt = ck_ref[0, sl, :].astype(_BF16)
        cvt = cv_ref[0, sl, :].astype(_BF16)

        for r0, ws, keys in blocks:
            q0, k0 = r0 * GRID_W, ws * GRID_W
            qs = _stack_heads(q_ref[q0:q0 + m, sl], lo_q)
            kw = k_ref[k0:k0 + win * GRID_W, sl]
            vw = v_ref[k0:k0 + win * GRID_W, sl]
            bias = jnp.concatenate(
                [bias_ref[ci, GRID_W * hh:GRID_W * (hh + 1), :] for hh in range(2) for ci in keys], axis=0)
            s_nb = _dot_nt(qs, kw) + bias
            s_cx = _dot(qs, ckt)
            o2 = _softmax_pv([s_nb, s_cx], [lambda e, vw=vw: _dot(e, vw), lambda e: _dot_nt(e, cvt)])
            o_scr[q0:q0 + m, sl] = _unstack_heads(o2, lo_q)

    a_ref[...] = _rms(o_scr[...], ga_ref[...]).astype(a_ref.dtype)


def _na_call(q, k, v, ckt, cvt, rpb_pad, g_a, nb, seq):
    past = ckt.shape[2]
    _, win, _, classes = _na_plan(seq // GRID_W)
    blk = lambda b: (b, 0)
    return pl.pallas_call(
        _na_kernel,
        grid=(nb,),
        in_specs=[
            pl.BlockSpec((seq, W_A), blk),
            pl.BlockSpec((seq, W_A), blk),
            pl.BlockSpec((seq, W_A), blk),
            pl.BlockSpec((1, W_A, past), lambda b: (b, 0, 0)),
            pl.BlockSpec((1, W_A, past), lambda b: (b, 0, 0)),
            pl.BlockSpec((H_A, 2 * NA_ROWS - 1, LANES), lambda b: (0, 0, 0)),
            pl.BlockSpec((1, W_A), lambda b: (0, 0)),
        ],
        out_specs=pl.BlockSpec((seq, W_A), blk),
        out_shape=jax.ShapeDtypeStruct((nb * seq, W_A), _BF16),
        scratch_shapes=[pltpu.VMEM((len(classes), 2 * GRID_W, win * GRID_W), _F32),
                        pltpu.VMEM((seq, W_A), _F32)],
        compiler_params=pltpu.CompilerParams(
            dimension_semantics=("parallel",), vmem_limit_bytes=VMEM_LIMIT),
        name="na_attn",
    )(q, k, v, ckt, cvt, rpb_pad, g_a.reshape(1, W_A))


def _win_kernel(sink_ref, q_ref, k_ref, v_ref, ck_ref, cv_ref, gb_ref, b_ref, o_scr):
    t = q_ref.shape[0]
    qb = WIN_QBLOCKS * BLOCK
    band = qb + 2 * BLOCK
    lo = lax.broadcasted_iota(jnp.int32, (qb, LANES), 1) < HEAD_DIM
    row_lo = lax.broadcasted_iota(jnp.int32, (2 * qb, 1), 0) < qb
    qi = lax.broadcasted_iota(jnp.int32, (2 * qb, band), 0) & (qb - 1)
    kj = lax.broadcasted_iota(jnp.int32, (2 * qb, band), 1)
    rel = qi - kj
    ckt = ck_ref[0].astype(_BF16)
    cvt = cv_ref[0].astype(_BF16)

    starts = {q0: min(max(q0 - BLOCK, 0), t - band) for q0 in range(0, t, qb)}
    masks = {off: jnp.where(jnp.abs(rel + off) <= SWA_WINDOW, 0.0, NEG_INF)
             for off in sorted({q0 - k0 for q0, k0 in starts.items()})}

    for j in range(G_B):
        sl = slice(LANES * j, LANES * (j + 1))
        sink = jnp.where(row_lo, sink_ref[0, j], sink_ref[1, j])
        for q0, k0 in starts.items():
            qs = _stack_heads(q_ref[q0:q0 + qb, sl], lo)
            s = _dot_nt(qs, k_ref[k0:k0 + band, :]) + masks[q0 - k0]
            s_cx = _dot(qs, ckt)
            vband = v_ref[k0:k0 + band, :]
            o2 = _softmax_pv([s, s_cx], [lambda e, vband=vband: _dot(e, vband), lambda e: _dot_nt(e, cvt)],
                             sink)
            o_scr[q0:q0 + qb, sl] = _unstack_heads(o2, lo)

    b_ref[...] = _rms(o_scr[...], gb_ref[...]).astype(b_ref.dtype)


def _win_call(sink, q, k, v, ckt, cvt, g_b, nb, seq):
    past = ckt.shape[2]
    return pl.pallas_call(
        _win_kernel,
        grid=(nb,),
        in_specs=[
            pl.BlockSpec(memory_space=pltpu.SMEM),
            pl.BlockSpec((seq, W_B), lambda b: (b, W_A // W_B)),
            pl.BlockSpec((seq, KV_W_B), lambda b: (b, 0)),
            pl.BlockSpec((seq, KV_W_B), lambda b: (b, 0)),
            pl.BlockSpec((1, KV_W_B, past), lambda b: (b, 0, 0)),
            pl.BlockSpec((1, KV_W_B, past), lambda b: (b, 0, 0)),
            pl.BlockSpec((1, W_B), lambda b: (0, 0)),
        ],
        out_specs=pl.BlockSpec((seq, W_B), lambda b: (b, 0)),
        out_shape=jax.ShapeDtypeStruct((nb * seq, W_B), _BF16),
        scratch_shapes=[pltpu.VMEM((seq, W_B), _F32)],
        compiler_params=pltpu.CompilerParams(
            dimension_semantics=("parallel",), vmem_limit_bytes=VMEM_LIMIT),
        name="win_attn",
    )(sink, q, k, v, ckt, cvt, g_b.reshape(1, W_B))


def _prepare_out_weight(w_ref, w_scr):
    for r0 in range(0, W_A, CAST_COLS):
        w_scr[r0:r0 + CAST_COLS, :] = w_ref[r0:r0 + CAST_COLS, :].astype(_BF16)
    for i, h in enumerate(_QB_HEAD_ORDER):
        w_scr[W_A + HEAD_DIM * i:W_A + HEAD_DIM * (i + 1), :] = (
            w_ref[W_A + HEAD_DIM * h:W_A + HEAD_DIM * (h + 1), :].astype(_BF16))


def _zero_row(x, row):
    r0 = row - row % 8
    sub = lax.broadcasted_iota(jnp.int32, (8, 1), 0)
    slab = jnp.where(sub == row % 8, 0.0, x[r0:r0 + 8])
    parts = ([x[:r0]] if r0 else []) + [slab] + ([x[r0 + 8:]] if r0 + 8 < x.shape[0] else [])
    return jnp.concatenate(parts, axis=0)


def _conv3(u, cw, cb, seq_len):
    tm = u.shape[0]
    prev = pltpu.roll(u, 1, 0)
    nxt = pltpu.roll(u, tm - 1, 0)
    for s0 in range(0, tm, seq_len):
        prev = _zero_row(prev, s0)
        nxt = _zero_row(nxt, s0 + seq_len - 1)
    return prev * cw[0:1, :] + u * cw[1:2, :] + nxt * cw[2:3, :] + cb


def _gated_silu(gate, val):
    half = 0.5 * gate
    return ((half + half * jnp.tanh(half)) * val).astype(_BF16)


def _up(h2_ref, w):
    tm = h2_ref.shape[0]
    return jnp.concatenate(
        [_dot(h2_ref[r:r + FFN_ROW_BLOCK, :], w) for r in range(0, tm, FFN_ROW_BLOCK)], axis=0)


def _out_proj(a_ref, b_ref, x_ref, mod, gpost_ref, gffn_ref, wo_scr, x1_ref, h2_ref):
    gt1 = mod[:, 2 * D_MODEL:3 * D_MODEL]
    sh2 = mod[:, 3 * D_MODEL:4 * D_MODEL]
    sc2 = mod[:, 4 * D_MODEL:5 * D_MODEL]
    for r in range(0, x_ref.shape[0], FFN_ROW_BLOCK):
        rows = slice(r, r + FFN_ROW_BLOCK)
        y = _dot(a_ref[rows, :], wo_scr[0:W_A, :]) + _dot(b_ref[rows, :], wo_scr[W_A:W_A + W_B, :])
        x1 = x_ref[rows, :] + gt1 * _rms(y, gpost_ref[...])
        x1_ref[rows, :] = x1
        h2_ref[rows, :] = (_rms(x1, gffn_ref[...]) * (1 + sc2) + sh2).astype(h2_ref.dtype)


def _down_proj(act_ref, wd_ref, mod, gpost2_ref, x1_ref, out_ref):
    gt2 = mod[:, 5 * D_MODEL:6 * D_MODEL]
    for r in range(0, act_ref.shape[0], DOWN_ROW_BLOCK):
        rows = slice(r, r + DOWN_ROW_BLOCK)
        y = _dot(act_ref[rows, :], wd_ref[...])
        out_ref[rows, :] = x1_ref[rows, :] + gt2 * _rms(y, gpost2_ref[...])


def _tail_seq_kernel(a_ref, b_ref, x_ref, mod_ref, gpost_ref, gffn_ref, gpost2_ref, wo_ref, wu_ref,
                     cw_ref, cb_ref, wd_ref, out_ref, wo_scr, h2_scr, act_scr, *, seq_len):
    mod = mod_ref[0]

    @pl.when(pl.program_id(0) == 0)
    def _():
        _prepare_out_weight(wo_ref, wo_scr)

    _out_proj(a_ref, b_ref, x_ref, mod, gpost_ref, gffn_ref, wo_scr, out_ref, h2_scr)
    tc = FF_CHUNK

    def chunk(col0):
        cols = slice(col0, col0 + tc)
        return _conv3(_up(h2_scr, wu_ref[:, cols]), cw_ref[:, cols], cb_ref[:, cols], seq_len)

    for c0 in range(0, D_FF, tc):
        act_scr[:, c0:c0 + tc] = _gated_silu(chunk(c0), chunk(D_FF + c0))
    _down_proj(act_scr, wd_ref, mod, gpost2_ref, out_ref, out_ref)


def _tail_seq_call(a, b, x2d, mod3, mod_row, g_post, g_ffn, g_post2, w_out, w_up, conv_w, conv_b, w_down,
                   seq_len, name):
    t = x2d.shape[0]
    tm = max(SEQ_TAIL_TILE, seq_len)
    big = tm > SEQ_TAIL_TILE
    vmem_limit = TAIL_VMEM_LIMIT if big else VMEM_LIMIT
    x_mode = dict(pipeline_mode=pl.Buffered(1)) if big else {}
    row = lambda i: (i, 0)
    const = lambda i: (0, 0)
    return pl.pallas_call(
        functools.partial(_tail_seq_kernel, seq_len=seq_len),
        grid=(t // tm,),
        in_specs=[
            pl.BlockSpec((tm, W_A), row),
            pl.BlockSpec((tm, W_B), row),
            pl.BlockSpec((tm, D_MODEL), row, **x_mode),
            pl.BlockSpec((1, 1, 6 * D_MODEL), lambda i: (mod_row(i), 0, 0)),
            pl.BlockSpec((1, D_MODEL), const),
            pl.BlockSpec((1, D_MODEL), const),
            pl.BlockSpec((1, D_MODEL), const),
            _resident((W_A + W_B, D_MODEL)),
            _resident((D_MODEL, 2 * D_FF)),
            _resident((3, 2 * D_FF)),
            _resident((1, 2 * D_FF)),
            _resident((D_FF, D_MODEL)),
        ],
        out_specs=pl.BlockSpec((tm, D_MODEL), row),
        out_shape=jax.ShapeDtypeStruct((t, D_MODEL), _F32),
        scratch_shapes=[
            pltpu.VMEM((W_A + W_B, D_MODEL), _BF16),
            pltpu.VMEM((tm, D_MODEL), _BF16),
            pltpu.VMEM((tm, D_FF), _BF16),
        ],
        compiler_params=pltpu.CompilerParams(
            dimension_semantics=("arbitrary",), vmem_limit_bytes=vmem_limit),
        name=name,
    )(a, b, x2d, mod3, g_post.reshape(1, D_MODEL), g_ffn.reshape(1, D_MODEL), g_post2.reshape(1, D_MODEL),
      w_out, w_up, conv_w, conv_b.reshape(1, 2 * D_FF), w_down)


def _rope_tables(seq):
    n = ROPE_HALF
    t = np.arange(seq)
    lane = np.arange(HEAD_DIM)
    pos = np.where(lane[None, :] < HEAD_DIM // 2, (t // GRID_W)[:, None], (t % GRID_W)[:, None])
    inv = 1.0 / (ROPE_THETA ** (np.arange(n, dtype=np.float64) / n))
    ang = pos.astype(np.float64) * inv[lane % n][None, :]
    sign = np.where((lane & n) == 0, -1.0, 1.0)[None, :]
    cos = np.tile(np.cos(ang), (1, LANES // HEAD_DIM)).astype(np.float32)
    sin = np.tile(np.sin(ang) * sign, (1, LANES // HEAD_DIM)).astype(np.float32)
    return jnp.asarray(cos), jnp.asarray(sin)


def _to_head_dim_token(cache_l):
    b, l, h, d = cache_l.shape
    return jnp.transpose(cache_l, (0, 2, 3, 1)).reshape(b, h * d, l)


def _from_head_dim_token(kv_t, heads):
    b, _, l = kv_t.shape
    return jnp.transpose(kv_t.reshape(b, 1, heads, HEAD_DIM, l), (0, 1, 4, 2, 3))


def kernel(x_prompt, x_sample, cache_a_k, cache_a_v, cache_b_k, cache_b_v, c, c_ctx, w_mod, b_mod,
           g_mix_pre, g_mix_post, g_ffn_pre, g_ffn_post, w_in, rpb_a, sink_b, g_grp_a, g_grp_b,
           w_out, w_up, conv_w, conv_b, w_down):
    bp, lp, _ = x_prompt.shape
    bs, ts, _ = x_sample.shape
    assert w_in.shape[0] == 1
    xp = x_prompt.reshape(bp * lp, D_MODEL)
    xs = x_sample.reshape(bs * ts, D_MODEL)
    cond8 = jnp.concatenate([c_ctx[None], c, jnp.zeros((8 - 1 - bs, D_MODEL), _F32)], axis=0)
    cos_tab, sin_tab = _rope_tables(ts)

    l = 0
    mod3 = _mod_call(cond8, w_mod[l], b_mod[l]).reshape(8, 1, 6 * D_MODEL)
    g_b_p = g_grp_b[l].reshape(KV_B, G_B, HEAD_DIM).transpose(1, 0, 2).reshape(W_B)
    rpb_pad = jnp.pad(rpb_a[l], ((0, 0), (0, 0), (0, LANES - (2 * NA_COLS - 1))))

    prompt_row = lambda i: 0
    assert ts >= SEQ_TAIL_TILE
    sample_row = lambda i: 1 + i

    (kat_p, vat_p, kbt_p, vbt_p, a_p, b_p), (w_up_b, w_down_b), (q_s, ka_s, va_s, kb_s, vb_s) = _front_call(
        sink_b[l], xp, xs, mod3, g_mix_pre[l], w_in[l], (cos_tab, sin_tab), g_grp_a[l], g_b_p,
        w_up[l], w_down[l], lp, ts)

    y_p = _tail_seq_call(a_p, b_p, xp, mod3, prompt_row, g_mix_post[l], g_ffn_pre[l], g_ffn_post[l], w_out[l],
                         w_up_b, conv_w[l], conv_b[l], w_down_b, lp, "tail_prompt")

    a_s = _na_call(q_s, ka_s, va_s, _to_head_dim_token(cache_a_k[:, l]), _to_head_dim_token(cache_a_v[:, l]),
                   rpb_pad, g_grp_a[l], bs, ts)
    b_s = _win_call(sink_b[l], q_s, kb_s, vb_s, _to_head_dim_token(cache_b_k[:, l]),
                    _to_head_dim_token(cache_b_v[:, l]), g_b_p, bs, ts)
    y_s = _tail_seq_call(a_s, b_s, xs, mod3, sample_row, g_mix_post[l], g_ffn_pre[l], g_ffn_post[l], w_out[l],
                         w_up_b, conv_w[l], conv_b[l], w_down_b, ts, "tail_sample")

    return (y_p.reshape(bp, lp, D_MODEL), y_s.reshape(bs, ts, D_MODEL),
            _from_head_dim_token(kat_p, H_A), _from_head_dim_token(vat_p, H_A),
            _from_head_dim_token(kbt_p, KV_B), _from_head_dim_token(vbt_p, KV_B))
```

```python
import functools

import numpy as np
import jax
import jax.numpy as jnp
from jax import lax
from jax.experimental import pallas as pl
from jax.experimental.pallas import tpu as pltpu

D_MODEL = 1024
HEAD_DIM = 64
H_A = 8
H_B = 8
KV_B = 2
G_B = H_B // KV_B
W_A = H_A * HEAD_DIM
W_B = H_B * HEAD_DIM
KV_W_B = KV_B * HEAD_DIM
IN_WIDTH = 3 * W_A + W_B + 2 * KV_W_B
GRID_W = 64
NA_ROWS = 8
NA_COLS = 16
SWA_WINDOW = 128
BLOCK = 128
D_FF = 2816
ROPE_THETA = 10000.0
EPS = 1e-6
NEG_INF = -1e30
Q_SCALE = HEAD_DIM ** -0.5
ROPE_HALF = HEAD_DIM // 4

LANES = 128
FF_CHUNK = 256
TOKEN_TILE = 512
FFN_ROW_BLOCK = 256
DOWN_ROW_BLOCK = 256
SEQ_TAIL_TILE = 512
PROJ_ROW_BLOCK = 256
MOD_STREAMS = 2
MOD_ROW_BLOCK = 128
NA_QROWS = 4
WIN_QBLOCKS = 2
CAST_COLS = 512
VMEM_LIMIT = 48 * 1024 * 1024
TAIL_VMEM_LIMIT = 56 * 1024 * 1024

C_QA, C_KA, C_VA, C_QB, C_KB, C_VB = 0, W_A, 2 * W_A, 3 * W_A, 3 * W_A + W_B, 3 * W_A + W_B + KV_W_B

_QB_HEAD_ORDER = tuple(kv * G_B + g for g in range(G_B) for kv in range(KV_B))

_BF16 = jnp.bfloat16
_F32 = jnp.float32


def _dot(a, b):
    return jnp.dot(a, b, preferred_element_type=_F32)


def _dot_nt(a, b):
    return lax.dot_general(a, b, (((1,), (1,)), ((), ())), preferred_element_type=_F32)


def _rms(x, g):
    var = jnp.mean(x * x, axis=-1, keepdims=True)
    return x * lax.rsqrt(var + EPS) * g


def _stack_heads(q2, lo):
    zero = jnp.zeros_like(q2)
    return jnp.concatenate([jnp.where(lo, q2, zero), jnp.where(lo, zero, q2)], axis=0)


def _unstack_heads(o2, lo):
    m = o2.shape[0] // 2
    return jnp.where(lo, o2[:m], o2[m:])


def _softmax_pv(scores, pv, sink=None):
    m = scores[0].max(axis=-1, keepdims=True)
    for s in scores[1:]:
        m = jnp.maximum(m, s.max(axis=-1, keepdims=True))
    if sink is not None:
        m = jnp.maximum(m, sink)
    denom = None
    out = None
    for s, f in zip(scores, pv):
        e = jnp.exp(s - m)
        l = e.sum(axis=-1, keepdims=True)
        o = f(e.astype(_BF16))
        denom = l if denom is None else denom + l
        out = o if out is None else out + o
    if sink is not None:
        denom = denom + jnp.exp(sink - m)
    return out / denom


def _resident(shape):
    return pl.BlockSpec(shape, lambda *_: (0,) * len(shape), pipeline_mode=pl.Buffered(1))


def _mod_kernel(c_ref, *refs):
    w_refs, b_ref, o_ref = refs[:-2], refs[-2], refs[-1]
    i = pl.program_id(0)

    @pl.when(i == 0)
    def _():
        o_ref[...] = jnp.broadcast_to(b_ref[...], o_ref.shape)

    acc = o_ref[...]
    for j, w_ref in enumerate(w_refs):
        c = c_ref[i * len(w_refs) + j]
        s = (c * jax.nn.sigmoid(c)).astype(_BF16)
        acc = acc + _dot(s, w_ref[...].astype(_BF16))
    o_ref[...] = acc


def _mod_call(cond8, w_mod, b_mod):
    n = w_mod.shape[1]
    kb = MOD_ROW_BLOCK
    nblk = D_MODEL // kb
    cond_blocks = cond8.reshape(8, nblk, kb).transpose(1, 0, 2)
    return pl.pallas_call(
        _mod_kernel,
        grid=(nblk // MOD_STREAMS,),
        in_specs=[pl.BlockSpec((nblk, 8, kb), lambda i: (0, 0, 0))]
        + [pl.BlockSpec((kb, n), lambda i, j=j: (i * MOD_STREAMS + j, 0)) for j in range(MOD_STREAMS)]
        + [pl.BlockSpec((1, n), lambda i: (0, 0))],
        out_specs=pl.BlockSpec((8, n), lambda i: (0, 0)),
        out_shape=jax.ShapeDtypeStruct((8, n), _F32),
        compiler_params=pltpu.CompilerParams(dimension_semantics=("arbitrary",), vmem_limit_bytes=VMEM_LIMIT),
        name="mod",
    )(cond_blocks, *([w_mod] * MOD_STREAMS), b_mod.reshape(1, n))


def _pair_window_heads(block_of, j):
    def head_at(h, want_hi):
        blk = block_of(h // 2)
        return blk if (h % 2 == 1) == want_hi else pltpu.roll(blk, HEAD_DIM, 1)

    left, right = head_at(_QB_HEAD_ORDER[2 * j], False), head_at(_QB_HEAD_ORDER[2 * j + 1], True)
    return jnp.where(lax.broadcasted_iota(jnp.int32, left.shape, 1) < HEAD_DIM, left, right)


def _window_head_gain(gb_ref):
    block_of = lambda k: jnp.broadcast_to(gb_ref[:, LANES * k:LANES * (k + 1)], (8, LANES))
    return jnp.concatenate([_pair_window_heads(block_of, j)[0:1] for j in range(G_B)], axis=1)


def _prepare_in_weight(w_ref, w_scr):
    for c0 in list(range(0, C_QB, CAST_COLS)) + [C_KB]:
        n = min(CAST_COLS, IN_WIDTH - c0)
        w_scr[:, c0:c0 + n] = w_ref[:, c0:c0 + n].astype(_BF16)
    block_of = lambda k: w_ref[:, C_QB + LANES * k:C_QB + LANES * (k + 1)]
    for j in range(G_B):
        w_scr[:, C_QB + LANES * j:C_QB + LANES * (j + 1)] = _pair_window_heads(block_of, j).astype(_BF16)


def _rope(z, cos, sin):
    lane = lax.broadcasted_iota(jnp.int32, z.shape, 1)
    partner = jnp.where((lane & ROPE_HALF) == 0,
                        pltpu.roll(z, LANES - ROPE_HALF, 1), pltpu.roll(z, ROPE_HALF, 1))
    return z * cos + partner * sin


def _inproj_rows(x_ref, mod, g_ref, w_scr, rope_refs, outs):
    q_ref, ka_ref, va_ref, kb_ref, vb_ref = outs
    kv_transposed = rope_refs is None
    sh1 = mod[:, 0:D_MODEL]
    sc1 = mod[:, D_MODEL:2 * D_MODEL]
    rb = PROJ_ROW_BLOCK

    for r in range(0, x_ref.shape[0], rb):
        rows = slice(r, r + rb)
        h = (_rms(x_ref[rows, :], g_ref[...]) * (1 + sc1) + sh1).astype(_BF16)

        def proj(c0, n):
            return _dot(h, w_scr[:, c0:c0 + n])

        def put_kv(ref, z):
            if not kv_transposed:
                ref[rows, :] = z.astype(ref.dtype)
                return
            l = ref.shape[2]
            for t0 in range(0, rb, l):
                for c0 in range(0, z.shape[1], LANES):
                    ref[(r + t0) // l, c0:c0 + LANES, :] = z[t0:t0 + l, c0:c0 + LANES].T.astype(ref.dtype)

        q_ref[rows, 0:W_A] = (proj(C_QA, W_A) * Q_SCALE).astype(q_ref.dtype)
        put_kv(ka_ref, proj(C_KA, W_A))
        put_kv(va_ref, proj(C_VA, W_A))
        zkv = proj(C_KB, 2 * KV_W_B)
        zkb, zvb = zkv[:, :KV_W_B], zkv[:, KV_W_B:]
        zqb = proj(C_QB, W_B)
        put_kv(vb_ref, zvb)
        if rope_refs is not None:
            cos = rope_refs[0][rows, :]
            sin = rope_refs[1][rows, :]
            for j in range(W_B // LANES):
                q_ref[rows, W_A + LANES * j:W_A + LANES * (j + 1)] = (
                    _rope(zqb[:, LANES * j:LANES * (j + 1)], cos, sin) * Q_SCALE).astype(q_ref.dtype)
            put_kv(kb_ref, _rope(zkb, cos, sin))
        else:
            q_ref[rows, W_A:W_A + W_B] = (zqb * Q_SCALE).astype(q_ref.dtype)
            put_kv(kb_ref, zkb)


def _ctx_attention(q_ref, rows, seq, ka_ref, va_ref, kb_ref, vb_ref, sink_ref, ga_ref, gb_ref, a_ref, b_ref):
    l = rows.stop - rows.start
    lo = lax.broadcasted_iota(jnp.int32, (l, LANES), 1) < HEAD_DIM
    row_lo = lax.broadcasted_iota(jnp.int32, (2 * l, 1), 0) < l
    oa = []
    for p in range(W_A // LANES):
        sl = slice(LANES * p, LANES * (p + 1))
        qs = _stack_heads(q_ref[rows, sl], lo)
        s = _dot(qs, ka_ref[seq, sl, :].astype(_BF16))
        vt = va_ref[seq, sl, :].astype(_BF16)
        oa.append(_unstack_heads(_softmax_pv([s], [lambda e, vt=vt: _dot_nt(e, vt)]), lo))
    a_ref[rows, :] = _rms(jnp.concatenate(oa, axis=1), ga_ref[...]).astype(a_ref.dtype)
    kb = kb_ref[seq].astype(_BF16)
    vb = vb_ref[seq].astype(_BF16)
    ob = []
    for j in range(G_B):
        qs = _stack_heads(q_ref[rows, W_A + LANES * j:W_A + LANES * (j + 1)], lo)
        s = _dot(qs, kb)
        sink = jnp.where(row_lo, sink_ref[0, j], sink_ref[1, j])
        ob.append(_unstack_heads(_softmax_pv([s], [lambda e: _dot_nt(e, vb)], sink), lo))
    b_ref[rows, :] = _rms(jnp.concatenate(ob, axis=1), _window_head_gain(gb_ref)).astype(b_ref.dtype)


def _front_kernel(sink_ref, xp_ref, xs_ref, mod_ref, g_ref, w_ref, cos_ref, sin_ref, ga_ref, gb_ref,
                  wu_ref, wd_ref, *refs, prompt_steps, mod_row):
    kat_ref, vat_ref, kbt_ref, vbt_ref, a_ref, b_ref, wu_out, wd_out = refs[0:8]
    outs_s, w_scr, q_scr = refs[8:13], refs[13], refs[14]
    i = pl.program_id(0)

    @pl.when(i == 0)
    def _():
        _prepare_in_weight(w_ref, w_scr)

    mod = mod_ref[pl.ds(mod_row(i), 1), :]

    @pl.when(i < prompt_steps)
    def _():
        wu_out[...] = wu_ref[...].astype(wu_out.dtype)
        wd_out[...] = wd_ref[...].astype(wd_out.dtype)
        _inproj_rows(xp_ref, mod, g_ref, w_scr, None, (q_scr, kat_ref, vat_ref, kbt_ref, vbt_ref))
        l = kat_ref.shape[2]
        for seq in range(kat_ref.shape[0]):
            _ctx_attention(q_scr, slice(seq * l, (seq + 1) * l), seq, kat_ref, vat_ref, kbt_ref, vbt_ref,
                           sink_ref, ga_ref, gb_ref, a_ref, b_ref)

    @pl.when(i >= prompt_steps)
    def _():
        _inproj_rows(xs_ref, mod, g_ref, w_scr, (cos_ref, sin_ref), outs_s)


def _front_call(sink, xp, xs, mod, g_pre, w_in, rope_tabs, g_a, g_b, w_up, w_down, lp, ts):
    tm = TOKEN_TILE
    tp, tsamp = xp.shape[0], xs.shape[0]
    n_p, n_s = tp // tm, tsamp // tm
    assert n_p * tm == tp and n_s * tm == tsamp and tm % lp == 0 and ts % tm == 0
    ru, rd = w_up.shape[0] // n_p, w_down.shape[0] // n_p
    assert ru * n_p == w_up.shape[0] and rd * n_p == w_down.shape[0] and ru % 16 == 0 and rd % 16 == 0
    tiles_per_seq = ts // tm
    p_tile = lambda i: jnp.minimum(i, n_p - 1)
    s_tile = lambda i: jnp.maximum(i - n_p, 0)
    mod_row = lambda i: jnp.where(i < n_p, 0, 1 + s_tile(i) // tiles_per_seq)
    p_row = lambda i: (p_tile(i), 0)
    s_row = lambda i: (s_tile(i), 0)
    const = lambda i: (0, 0)
    nseq = tm // lp
    in_specs = [
        pl.BlockSpec(memory_space=pltpu.SMEM),
        pl.BlockSpec((tm, D_MODEL), p_row),
        pl.BlockSpec((tm, D_MODEL), s_row),
        pl.BlockSpec((8, 6 * D_MODEL), const),
        pl.BlockSpec((1, D_MODEL), const),
        _resident((D_MODEL, IN_WIDTH)),
        pl.BlockSpec((tm, LANES), lambda i: (s_tile(i) % tiles_per_seq, 0)),
        pl.BlockSpec((tm, LANES), lambda i: (s_tile(i) % tiles_per_seq, 0)),
        pl.BlockSpec((1, W_A), const),
        pl.BlockSpec((1, W_B), const),
        pl.BlockSpec((ru, w_up.shape[1]), p_row),
        pl.BlockSpec((rd, w_down.shape[1]), p_row),
    ]
    kvt_shape = lambda w: jax.ShapeDtypeStruct((tp // lp, w, lp), _F32)
    kvt_spec = lambda w: pl.BlockSpec((nseq, w, lp), lambda i: (p_tile(i), 0, 0))
    kv_shape = lambda w: jax.ShapeDtypeStruct((tsamp, w), _BF16)
    kv_spec = lambda w: pl.BlockSpec((tm, w), s_row)
    out_shape = (kvt_shape(W_A), kvt_shape(W_A), kvt_shape(KV_W_B), kvt_shape(KV_W_B),
                 jax.ShapeDtypeStruct((tp, W_A), _BF16), jax.ShapeDtypeStruct((tp, W_B), _BF16),
                 jax.ShapeDtypeStruct(w_up.shape, _BF16), jax.ShapeDtypeStruct(w_down.shape, _BF16),
                 jax.ShapeDtypeStruct((tsamp, W_A + W_B), _BF16),
                 kv_shape(W_A), kv_shape(W_A), kv_shape(KV_W_B), kv_shape(KV_W_B))
    out_specs = (kvt_spec(W_A), kvt_spec(W_A), kvt_spec(KV_W_B), kvt_spec(KV_W_B),
                 pl.BlockSpec((tm, W_A), p_row), pl.BlockSpec((tm, W_B), p_row),
                 pl.BlockSpec((ru, w_up.shape[1]), p_row), pl.BlockSpec((rd, w_down.shape[1]), p_row),
                 pl.BlockSpec((tm, W_A + W_B), s_row),
                 kv_spec(W_A), kv_spec(W_A), kv_spec(KV_W_B), kv_spec(KV_W_B))
    outs = pl.pallas_call(
        functools.partial(_front_kernel, prompt_steps=n_p, mod_row=mod_row),
        grid=(n_p + n_s,),
        in_specs=in_specs,
        out_specs=out_specs,
        out_shape=out_shape,
        scratch_shapes=[pltpu.VMEM((D_MODEL, IN_WIDTH), _BF16), pltpu.VMEM((tm, W_A + W_B), _BF16)],
        compiler_params=pltpu.CompilerParams(
            dimension_semantics=("arbitrary",), vmem_limit_bytes=TAIL_VMEM_LIMIT),
        name="front",
    )(sink, xp, xs, mod, g_pre.reshape(1, D_MODEL), w_in, *rope_tabs, g_a.reshape(1, W_A),
      g_b.reshape(1, W_B), w_up, w_down)
    return outs[:6], outs[6:8], outs[8:]


def _na_plan(rows):
    kr = min(NA_ROWS, rows)
    win = kr + NA_QROWS
    row_start = lambda r: min(max(r - kr // 2, 0), rows - kr)
    blocks, classes = [], []
    for r0 in range(0, rows, NA_QROWS):
        ws = min(row_start(r0), rows - win)
        keys = []
        for r in range(r0, r0 + NA_QROWS):
            key = (ws - r + NA_ROWS - 1, row_start(r) - ws)
            if key not in classes:
                classes.append(key)
            keys.append(classes.index(key))
        blocks.append((r0, ws, keys))
    return kr, win, blocks, classes


def _na_kernel(q_ref, k_ref, v_ref, ck_ref, cv_ref, rpb_ref, ga_ref, a_ref, bias_ref, o_scr):
    rows = q_ref.shape[0] // GRID_W
    kr, win, blocks, classes = _na_plan(rows)
    m = NA_QROWS * GRID_W
    lane = lax.broadcasted_iota(jnp.int32, (GRID_W, LANES), 1)
    lo = lane < HEAD_DIM
    lo_q = lax.broadcasted_iota(jnp.int32, (m, LANES), 1) < HEAD_DIM
    cq = lax.broadcasted_iota(jnp.int32, (GRID_W, LANES), 0)
    ck = lane & (GRID_W - 1)
    cs = jnp.clip(cq - NA_COLS // 2, 0, GRID_W - NA_COLS)
    valid = (ck >= cs) & (ck < cs + NA_COLS)
    neg = jnp.full((GRID_W, LANES), NEG_INF, _F32)
    n_drow = 2 * NA_ROWS - 1

    for p in range(W_A // LANES):
        sl = slice(LANES * p, LANES * (p + 1))
        for hh in range(2):
            t_lo, t_hi = [], []
            for d in range(n_drow):
                base = jnp.broadcast_to(rpb_ref[2 * p + hh, d:d + 1, :], (GRID_W, LANES))
                t_lo.append(pltpu.roll(base, LANES - (NA_COLS - 1), 1, stride=1, stride_axis=0))
                t_hi.append(pltpu.roll(base, GRID_W - (NA_COLS - 1), 1, stride=1, stride_axis=0))
            tiles = {}
            for ci, (d_first, off) in enumerate(classes):
                for j in range(win // 2):
                    d_of = lambda i: d_first + i if off <= i < off + kr else None
                    key = (d_of(2 * j), d_of(2 * j + 1))
                    if key not in tiles:
                        left = neg if key[0] is None else t_lo[key[0]]
                        right = neg if key[1] is None else t_hi[key[1]]
                        tiles[key] = jnp.where(valid, jnp.where(lo, left, right), NEG_INF)
                    bias_ref[ci, GRID_W * hh:GRID_W * (hh + 1), LANES * j:LANES * (j + 1)] = tiles[key]

        ckt = ck_ref[0, sl, :].astype(_BF16)
        cvt = cv_ref[0, sl, :].astype(_BF16)

        for r0, ws, keys in blocks:
            q0, k0 = r0 * GRID_W, ws * GRID_W
            qs = _stack_heads(q_ref[q0:q0 + m, sl], lo_q)
            kw = k_ref[k0:k0 + win * GRID_W, sl]
            vw = v_ref[k0:k0 + win * GRID_W, sl]
            bias = jnp.concatenate(
                [bias_ref[ci, GRID_W * hh:GRID_W * (hh + 1), :] for hh in range(2) for ci in keys], axis=0)
            s_nb = _dot_nt(qs, kw) + bias
            s_cx = _dot(qs, ckt)
            o2 = _softmax_pv([s_nb, s_cx], [lambda e, vw=vw: _dot(e, vw), lambda e: _dot_nt(e, cvt)])
            o_scr[q0:q0 + m, sl] = _unstack_heads(o2, lo_q)

    a_ref[...] = _rms(o_scr[...], ga_ref[...]).astype(a_ref.dtype)


def _na_call(q, k, v, ckt, cvt, rpb_pad, g_a, nb, seq):
    past = ckt.shape[2]
    _, win, _, classes = _na_plan(seq // GRID_W)
    blk = lambda b: (b, 0)
    return pl.pallas_call(
        _na_kernel,
        grid=(nb,),
        in_specs=[
            pl.BlockSpec((seq, W_A), blk),
            pl.BlockSpec((seq, W_A), blk),
            pl.BlockSpec((seq, W_A), blk),
            pl.BlockSpec((1, W_A, past), lambda b: (b, 0, 0)),
            pl.BlockSpec((1, W_A, past), lambda b: (b, 0, 0)),
            pl.BlockSpec((H_A, 2 * NA_ROWS - 1, LANES), lambda b: (0, 0, 0)),
            pl.BlockSpec((1, W_A), lambda b: (0, 0)),
        ],
        out_specs=pl.BlockSpec((seq, W_A), blk),
        out_shape=jax.ShapeDtypeStruct((nb * seq, W_A), _BF16),
        scratch_shapes=[pltpu.VMEM((len(classes), 2 * GRID_W, win * GRID_W), _F32),
                        pltpu.VMEM((seq, W_A), _F32)],
        compiler_params=pltpu.CompilerParams(
            dimension_semantics=("parallel",), vmem_limit_bytes=VMEM_LIMIT),
        name="na_attn",
    )(q, k, v, ckt, cvt, rpb_pad, g_a.reshape(1, W_A))


def _win_kernel(sink_ref, q_ref, k_ref, v_ref, ck_ref, cv_ref, gb_ref, b_ref, o_scr):
    t = q_ref.shape[0]
    qb = WIN_QBLOCKS * BLOCK
    band = qb + 2 * BLOCK
    lo = lax.broadcasted_iota(jnp.int32, (qb, LANES), 1) < HEAD_DIM
    row_lo = lax.broadcasted_iota(jnp.int32, (2 * qb, 1), 0) < qb
    qi = lax.broadcasted_iota(jnp.int32, (2 * qb, band), 0) & (qb - 1)
    kj = lax.broadcasted_iota(jnp.int32, (2 * qb, band), 1)
    rel = qi - kj
    ckt = ck_ref[0].astype(_BF16)
    cvt = cv_ref[0].astype(_BF16)

    starts = {q0: min(max(q0 - BLOCK, 0), t - band) for q0 in range(0, t, qb)}
    masks = {off: jnp.where(jnp.abs(rel + off) <= SWA_WINDOW, 0.0, NEG_INF)
             for off in sorted({q0 - k0 for q0, k0 in starts.items()})}

    for j in range(G_B):
        sl = slice(LANES * j, LANES * (j + 1))
        sink = jnp.where(row_lo, sink_ref[0, j], sink_ref[1, j])
        for q0, k0 in starts.items():
            qs = _stack_heads(q_ref[q0:q0 + qb, sl], lo)
            s = _dot_nt(qs, k_ref[k0:k0 + band, :]) + masks[q0 - k0]
            s_cx = _dot(qs, ckt)
            vband = v_ref[k0:k0 + band, :]
            o2 = _softmax_pv([s, s_cx], [lambda e, vband=vband: _dot(e, vband), lambda e: _dot_nt(e, cvt)],
                             sink)
            o_scr[q0:q0 + qb, sl] = _unstack_heads(o2, lo)

    b_ref[...] = _rms(o_scr[...], _window_head_gain(gb_ref)).astype(b_ref.dtype)


def _win_call(sink, q, k, v, ckt, cvt, g_b, nb, seq):
    past = ckt.shape[2]
    return pl.pallas_call(
        _win_kernel,
        grid=(nb,),
        in_specs=[
            pl.BlockSpec(memory_space=pltpu.SMEM),
            pl.BlockSpec((seq, W_B), lambda b: (b, W_A // W_B)),
            pl.BlockSpec((seq, KV_W_B), lambda b: (b, 0)),
            pl.BlockSpec((seq, KV_W_B), lambda b: (b, 0)),
            pl.BlockSpec((1, KV_W_B, past), lambda b: (b, 0, 0)),
            pl.BlockSpec((1, KV_W_B, past), lambda b: (b, 0, 0)),
            pl.BlockSpec((1, W_B), lambda b: (0, 0)),
        ],
        out_specs=pl.BlockSpec((seq, W_B), lambda b: (b, 0)),
        out_shape=jax.ShapeDtypeStruct((nb * seq, W_B), _BF16),
        scratch_shapes=[pltpu.VMEM((seq, W_B), _F32)],
        compiler_params=pltpu.CompilerParams(
            dimension_semantics=("parallel",), vmem_limit_bytes=VMEM_LIMIT),
        name="win_attn",
    )(sink, q, k, v, ckt, cvt, g_b.reshape(1, W_B))


def _prepare_out_weight(w_ref, w_scr):
    for r0 in range(0, W_A, CAST_COLS):
        w_scr[r0:r0 + CAST_COLS, :] = w_ref[r0:r0 + CAST_COLS, :].astype(_BF16)
    for i, h in enumerate(_QB_HEAD_ORDER):
        w_scr[W_A + HEAD_DIM * i:W_A + HEAD_DIM * (i + 1), :] = (
            w_ref[W_A + HEAD_DIM * h:W_A + HEAD_DIM * (h + 1), :].astype(_BF16))


def _zero_row(x, row):
    r0 = row - row % 8
    sub = lax.broadcasted_iota(jnp.int32, (8, 1), 0)
    slab = jnp.where(sub == row % 8, 0.0, x[r0:r0 + 8])
    parts = ([x[:r0]] if r0 else []) + [slab] + ([x[r0 + 8:]] if r0 + 8 < x.shape[0] else [])
    return jnp.concatenate(parts, axis=0)


def _conv3(u, cw, cb, seq_len):
    tm = u.shape[0]
    prev = pltpu.roll(u, 1, 0)
    nxt = pltpu.roll(u, tm - 1, 0)
    for s0 in range(0, tm, seq_len):
        prev = _zero_row(prev, s0)
        nxt = _zero_row(nxt, s0 + seq_len - 1)
    return prev * cw[0:1, :] + u * cw[1:2, :] + nxt * cw[2:3, :] + cb


def _gated_silu(gate, val):
    half = 0.5 * gate
    return ((half + half * jnp.tanh(half)) * val).astype(_BF16)


def _up(h2_ref, w):
    tm = h2_ref.shape[0]
    return jnp.concatenate(
        [_dot(h2_ref[r:r + FFN_ROW_BLOCK, :], w) for r in range(0, tm, FFN_ROW_BLOCK)], axis=0)


def _out_proj(a_ref, b_ref, x_ref, mod, gpost_ref, gffn_ref, wo_scr, x1_ref, h2_ref):
    gt1 = mod[:, 2 * D_MODEL:3 * D_MODEL]
    sh2 = mod[:, 3 * D_MODEL:4 * D_MODEL]
    sc2 = mod[:, 4 * D_MODEL:5 * D_MODEL]
    for r in range(0, x_ref.shape[0], FFN_ROW_BLOCK):
        rows = slice(r, r + FFN_ROW_BLOCK)
        y = _dot(a_ref[rows, :], wo_scr[0:W_A, :]) + _dot(b_ref[rows, :], wo_scr[W_A:W_A + W_B, :])
        x1 = x_ref[rows, :] + gt1 * _rms(y, gpost_ref[...])
        x1_ref[rows, :] = x1
        h2_ref[rows, :] = (_rms(x1, gffn_ref[...]) * (1 + sc2) + sh2).astype(h2_ref.dtype)


def _down_proj(act_ref, wd_ref, mod, gpost2_ref, x1_ref, out_ref):
    gt2 = mod[:, 5 * D_MODEL:6 * D_MODEL]
    for r in range(0, act_ref.shape[0], DOWN_ROW_BLOCK):
        rows = slice(r, r + DOWN_ROW_BLOCK)
        y = _dot(act_ref[rows, :], wd_ref[...])
        out_ref[rows, :] = x1_ref[rows, :] + gt2 * _rms(y, gpost2_ref[...])


def _tail_seq_kernel(a_ref, b_ref, x_ref, mod_ref, gpost_ref, gffn_ref, gpost2_ref, wo_ref, wu_ref,
                     cw_ref, cb_ref, wd_ref, out_ref, wo_scr, h2_scr, act_scr, *, seq_len, mod_row):
    mod = mod_ref[pl.ds(mod_row(pl.program_id(0)), 1), :]

    @pl.when(pl.program_id(0) == 0)
    def _():
        _prepare_out_weight(wo_ref, wo_scr)

    _out_proj(a_ref, b_ref, x_ref, mod, gpost_ref, gffn_ref, wo_scr, out_ref, h2_scr)
    tc = FF_CHUNK

    def chunk(col0):
        cols = slice(col0, col0 + tc)
        return _conv3(_up(h2_scr, wu_ref[:, cols]), cw_ref[:, cols], cb_ref[:, cols], seq_len)

    for c0 in range(0, D_FF, tc):
        act_scr[:, c0:c0 + tc] = _gated_silu(chunk(c0), chunk(D_FF + c0))
    _down_proj(act_scr, wd_ref, mod, gpost2_ref, out_ref, out_ref)


def _tail_seq_call(a, b, x2d, mod, mod_row, g_post, g_ffn, g_post2, w_out, w_up, conv_w, conv_b, w_down,
                   seq_len, name):
    t = x2d.shape[0]
    tm = max(SEQ_TAIL_TILE, seq_len)
    big = tm > SEQ_TAIL_TILE
    vmem_limit = TAIL_VMEM_LIMIT if big else VMEM_LIMIT
    x_mode = dict(pipeline_mode=pl.Buffered(1)) if big else {}
    row = lambda i: (i, 0)
    const = lambda i: (0, 0)
    return pl.pallas_call(
        functools.partial(_tail_seq_kernel, seq_len=seq_len, mod_row=mod_row),
        grid=(t // tm,),
        in_specs=[
            pl.BlockSpec((tm, W_A), row),
            pl.BlockSpec((tm, W_B), row),
            pl.BlockSpec((tm, D_MODEL), row, **x_mode),
            pl.BlockSpec((8, 6 * D_MODEL), const),
            pl.BlockSpec((1, D_MODEL), const),
            pl.BlockSpec((1, D_MODEL), const),
            pl.BlockSpec((1, D_MODEL), const),
            _resident((W_A + W_B, D_MODEL)),
            _resident((D_MODEL, 2 * D_FF)),
            _resident((3, 2 * D_FF)),
            _resident((1, 2 * D_FF)),
            _resident((D_FF, D_MODEL)),
        ],
        out_specs=pl.BlockSpec((tm, D_MODEL), row),
        out_shape=jax.ShapeDtypeStruct((t, D_MODEL), _F32),
        scratch_shapes=[
            pltpu.VMEM((W_A + W_B, D_MODEL), _BF16),
            pltpu.VMEM((tm, D_MODEL), _BF16),
            pltpu.VMEM((tm, D_FF), _BF16),
        ],
        compiler_params=pltpu.CompilerParams(
            dimension_semantics=("arbitrary",), vmem_limit_bytes=vmem_limit),
        name=name,
    )(a, b, x2d, mod, g_post.reshape(1, D_MODEL), g_ffn.reshape(1, D_MODEL), g_post2.reshape(1, D_MODEL),
      w_out, w_up, conv_w, conv_b.reshape(1, 2 * D_FF), w_down)


def _rope_tables(seq):
    n = ROPE_HALF
    t = np.arange(seq)
    lane = np.arange(HEAD_DIM)
    pos = np.where(lane[None, :] < HEAD_DIM // 2, (t // GRID_W)[:, None], (t % GRID_W)[:, None])
    inv = 1.0 / (ROPE_THETA ** (np.arange(n, dtype=np.float64) / n))
    ang = pos.astype(np.float64) * inv[lane % n][None, :]
    sign = np.where((lane & n) == 0, -1.0, 1.0)[None, :]
    cos = np.tile(np.cos(ang), (1, LANES // HEAD_DIM)).astype(np.float32)
    sin = np.tile(np.sin(ang) * sign, (1, LANES // HEAD_DIM)).astype(np.float32)
    return jnp.asarray(cos), jnp.asarray(sin)


def _to_head_dim_token(cache_l):
    b, l, h, d = cache_l.shape
    return jnp.transpose(cache_l, (0, 2, 3, 1)).reshape(b, h * d, l)


def _from_head_dim_token(kv_t, heads):
    b, _, l = kv_t.shape
    return jnp.transpose(kv_t.reshape(b, 1, heads, HEAD_DIM, l), (0, 1, 4, 2, 3))


def kernel(x_prompt, x_sample, cache_a_k, cache_a_v, cache_b_k, cache_b_v, c, c_ctx, w_mod, b_mod,
           g_mix_pre, g_mix_post, g_ffn_pre, g_ffn_post, w_in, rpb_a, sink_b, g_grp_a, g_grp_b,
           w_out, w_up, conv_w, conv_b, w_down):
    bp, lp, _ = x_prompt.shape
    bs, ts, _ = x_sample.shape
    assert w_in.shape[0] == 1
    xp = x_prompt.reshape(bp * lp, D_MODEL)
    xs = x_sample.reshape(bs * ts, D_MODEL)
    cond8 = jnp.concatenate([c_ctx[None], c, jnp.zeros((8 - 1 - bs, D_MODEL), _F32)], axis=0)
    cos_tab, sin_tab = _rope_tables(ts)

    l = 0
    mod = _mod_call(cond8, w_mod[l], b_mod[l])
    rpb_pad = jnp.pad(rpb_a[l], ((0, 0), (0, 0), (0, LANES - (2 * NA_COLS - 1))))

    prompt_row = lambda i: 0
    assert ts >= SEQ_TAIL_TILE
    sample_row = lambda i: 1 + i

    (kat_p, vat_p, kbt_p, vbt_p, a_p, b_p), (w_up_b, w_down_b), (q_s, ka_s, va_s, kb_s, vb_s) = _front_call(
        sink_b[l], xp, xs, mod, g_mix_pre[l], w_in[l], (cos_tab, sin_tab), g_grp_a[l], g_grp_b[l],
        w_up[l], w_down[l], lp, ts)

    y_p = _tail_seq_call(a_p, b_p, xp, mod, prompt_row, g_mix_post[l], g_ffn_pre[l], g_ffn_post[l], w_out[l],
                         w_up_b, conv_w[l], conv_b[l], w_down_b, lp, "tail_prompt")

    a_s = _na_call(q_s, ka_s, va_s, _to_head_dim_token(cache_a_k[:, l]), _to_head_dim_token(cache_a_v[:, l]),
                   rpb_pad, g_grp_a[l], bs, ts)
    b_s = _win_call(sink_b[l], q_s, kb_s, vb_s, _to_head_dim_token(cache_b_k[:, l]),
                    _to_head_dim_token(cache_b_v[:, l]), g_grp_b[l], bs, ts)
    y_s = _tail_seq_call(a_s, b_s, xs, mod, sample_row, g_mix_post[l], g_ffn_pre[l], g_ffn_post[l], w_out[l],
                         w_up_b, conv_w[l], conv_b[l], w_down_b, ts, "tail_sample")

    return (y_p.reshape(bp, lp, D_MODEL), y_s.reshape(bs, ts, D_MODEL),
            _from_head_dim_token(kat_p, H_A), _from_head_dim_token(vat_p, H_A),
            _from_head_dim_token(kbt_p, KV_B), _from_head_dim_token(vbt_p, KV_B))
```

```python
import functools

import numpy as np
import jax
import jax.numpy as jnp
from jax import lax
from jax.experimental import pallas as pl
from jax.experimental.pallas import tpu as pltpu

D_MODEL = 1024
HEAD_DIM = 64
H_A = 8
H_B = 8
KV_B = 2
G_B = H_B // KV_B
W_A = H_A * HEAD_DIM
W_B = H_B * HEAD_DIM
KV_W_B = KV_B * HEAD_DIM
IN_WIDTH = 3 * W_A + W_B + 2 * KV_W_B
GRID_W = 64
NA_ROWS = 8
NA_COLS = 16
SWA_WINDOW = 128
BLOCK = 128
D_FF = 2816
ROPE_THETA = 10000.0
EPS = 1e-6
NEG_INF = -1e30
Q_SCALE = HEAD_DIM ** -0.5
ROPE_HALF = HEAD_DIM // 4

LANES = 128
FF_CHUNK = 256
TOKEN_TILE = 512
FFN_ROW_BLOCK = 512
DOWN_ROW_BLOCK = 512
SEQ_TAIL_TILE = 512
PROJ_ROW_BLOCK = 256
MOD_STREAMS = 2
MOD_ROW_BLOCK = 128
NA_QROWS = 4
WIN_QBLOCKS = 2
CAST_COLS = 512
VMEM_LIMIT = 48 * 1024 * 1024
TAIL_VMEM_LIMIT = 56 * 1024 * 1024

C_QA, C_KA, C_VA, C_QB, C_KB, C_VB = 0, W_A, 2 * W_A, 3 * W_A, 3 * W_A + W_B, 3 * W_A + W_B + KV_W_B

_QB_HEAD_ORDER = tuple(kv * G_B + g for g in range(G_B) for kv in range(KV_B))

_BF16 = jnp.bfloat16
_F32 = jnp.float32


def _dot(a, b):
    return jnp.dot(a, b, preferred_element_type=_F32)


def _dot_nt(a, b):
    return lax.dot_general(a, b, (((1,), (1,)), ((), ())), preferred_element_type=_F32)


def _rms(x, g):
    var = jnp.mean(x * x, axis=-1, keepdims=True)
    return x * lax.rsqrt(var + EPS) * g


def _stack_heads(q2, lo):
    zero = jnp.zeros_like(q2)
    return jnp.concatenate([jnp.where(lo, q2, zero), jnp.where(lo, zero, q2)], axis=0)


def _unstack_heads(o2, lo):
    m = o2.shape[0] // 2
    return jnp.where(lo, o2[:m], o2[m:])


def _softmax_pv(scores, pv, sink=None):
    m = scores[0].max(axis=-1, keepdims=True)
    for s in scores[1:]:
        m = jnp.maximum(m, s.max(axis=-1, keepdims=True))
    if sink is not None:
        m = jnp.maximum(m, sink)
    denom = None
    out = None
    for s, f in zip(scores, pv):
        e = jnp.exp(s - m)
        l = e.sum(axis=-1, keepdims=True)
        o = f(e.astype(_BF16))
        denom = l if denom is None else denom + l
        out = o if out is None else out + o
    if sink is not None:
        denom = denom + jnp.exp(sink - m)
    return out / denom


def _resident(shape):
    return pl.BlockSpec(shape, lambda *_: (0,) * len(shape), pipeline_mode=pl.Buffered(1))


def _mod_kernel(c_ref, *refs):
    w_refs, b_ref, o_ref = refs[:-2], refs[-2], refs[-1]
    i = pl.program_id(0)

    @pl.when(i == 0)
    def _():
        o_ref[...] = jnp.broadcast_to(b_ref[...], o_ref.shape)

    acc = o_ref[...]
    for j, w_ref in enumerate(w_refs):
        c = c_ref[i * len(w_refs) + j]
        s = (c * jax.nn.sigmoid(c)).astype(_BF16)
        acc = acc + _dot(s, w_ref[...].astype(_BF16))
    o_ref[...] = acc


def _mod_call(cond8, w_mod, b_mod):
    n = w_mod.shape[1]
    kb = MOD_ROW_BLOCK
    nblk = D_MODEL // kb
    cond_blocks = cond8.reshape(8, nblk, kb).transpose(1, 0, 2)
    return pl.pallas_call(
        _mod_kernel,
        grid=(nblk // MOD_STREAMS,),
        in_specs=[pl.BlockSpec((nblk, 8, kb), lambda i: (0, 0, 0))]
        + [pl.BlockSpec((kb, n), lambda i, j=j: (i * MOD_STREAMS + j, 0)) for j in range(MOD_STREAMS)]
        + [pl.BlockSpec((1, n), lambda i: (0, 0))],
        out_specs=pl.BlockSpec((8, n), lambda i: (0, 0)),
        out_shape=jax.ShapeDtypeStruct((8, n), _F32),
        compiler_params=pltpu.CompilerParams(dimension_semantics=("arbitrary",), vmem_limit_bytes=VMEM_LIMIT),
        name="mod",
    )(cond_blocks, *([w_mod] * MOD_STREAMS), b_mod.reshape(1, n))


def _pair_window_heads(block_of, j):
    def head_at(h, want_hi):
        blk = block_of(h // 2)
        return blk if (h % 2 == 1) == want_hi else pltpu.roll(blk, HEAD_DIM, 1)

    left, right = head_at(_QB_HEAD_ORDER[2 * j], False), head_at(_QB_HEAD_ORDER[2 * j + 1], True)
    return jnp.where(lax.broadcasted_iota(jnp.int32, left.shape, 1) < HEAD_DIM, left, right)


def _window_head_gain(gb_ref):
    block_of = lambda k: jnp.broadcast_to(gb_ref[:, LANES * k:LANES * (k + 1)], (8, LANES))
    return jnp.concatenate([_pair_window_heads(block_of, j)[0:1] for j in range(G_B)], axis=1)


def _prepare_in_weight(w_ref, w_scr):
    for c0 in list(range(0, C_QB, CAST_COLS)) + [C_KB]:
        n = min(CAST_COLS, IN_WIDTH - c0)
        w_scr[:, c0:c0 + n] = w_ref[:, c0:c0 + n].astype(_BF16)
    block_of = lambda k: w_ref[:, C_QB + LANES * k:C_QB + LANES * (k + 1)]
    for j in range(G_B):
        w_scr[:, C_QB + LANES * j:C_QB + LANES * (j + 1)] = _pair_window_heads(block_of, j).astype(_BF16)


def _rope(z, cos, sin):
    lane = lax.broadcasted_iota(jnp.int32, z.shape, 1)
    partner = jnp.where((lane & ROPE_HALF) == 0,
                        pltpu.roll(z, LANES - ROPE_HALF, 1), pltpu.roll(z, ROPE_HALF, 1))
    return z * cos + partner * sin


def _inproj_rows(x_ref, mod, g_ref, w_scr, rope_refs, outs):
    q_ref, ka_ref, va_ref, kb_ref, vb_ref = outs
    kv_transposed = rope_refs is None
    sh1 = mod[:, 0:D_MODEL]
    sc1 = mod[:, D_MODEL:2 * D_MODEL]
    rb = PROJ_ROW_BLOCK

    for r in range(0, x_ref.shape[0], rb):
        rows = slice(r, r + rb)
        h = (_rms(x_ref[rows, :], g_ref[...]) * (1 + sc1) + sh1).astype(_BF16)

        def proj(c0, n):
            return _dot(h, w_scr[:, c0:c0 + n])

        def put_kv(ref, z):
            if not kv_transposed:
                ref[rows, :] = z.astype(ref.dtype)
                return
            l = ref.shape[2]
            for t0 in range(0, rb, l):
                for c0 in range(0, z.shape[1], LANES):
                    ref[(r + t0) // l, c0:c0 + LANES, :] = z[t0:t0 + l, c0:c0 + LANES].T.astype(ref.dtype)

        q_ref[rows, 0:W_A] = (proj(C_QA, W_A) * Q_SCALE).astype(q_ref.dtype)
        put_kv(ka_ref, proj(C_KA, W_A))
        put_kv(va_ref, proj(C_VA, W_A))
        zkv = proj(C_KB, 2 * KV_W_B)
        zkb, zvb = zkv[:, :KV_W_B], zkv[:, KV_W_B:]
        zqb = proj(C_QB, W_B)
        put_kv(vb_ref, zvb)
        if rope_refs is not None:
            cos = rope_refs[0][rows, :]
            sin = rope_refs[1][rows, :]
            for j in range(W_B // LANES):
                q_ref[rows, W_A + LANES * j:W_A + LANES * (j + 1)] = (
                    _rope(zqb[:, LANES * j:LANES * (j + 1)], cos, sin) * Q_SCALE).astype(q_ref.dtype)
            put_kv(kb_ref, _rope(zkb, cos, sin))
        else:
            q_ref[rows, W_A:W_A + W_B] = (zqb * Q_SCALE).astype(q_ref.dtype)
            put_kv(kb_ref, zkb)


def _ctx_attention(q_ref, rows, seq, ka_ref, va_ref, kb_ref, vb_ref, sink_ref, ga_ref, gb_ref, a_ref, b_ref):
    l = rows.stop - rows.start
    lo = lax.broadcasted_iota(jnp.int32, (l, LANES), 1) < HEAD_DIM
    row_lo = lax.broadcasted_iota(jnp.int32, (2 * l, 1), 0) < l
    oa = []
    for p in range(W_A // LANES):
        sl = slice(LANES * p, LANES * (p + 1))
        qs = _stack_heads(q_ref[rows, sl], lo)
        s = _dot(qs, ka_ref[seq, sl, :].astype(_BF16))
        vt = va_ref[seq, sl, :].astype(_BF16)
        oa.append(_unstack_heads(_softmax_pv([s], [lambda e, vt=vt: _dot_nt(e, vt)]), lo))
    a_ref[rows, :] = _rms(jnp.concatenate(oa, axis=1), ga_ref[...]).astype(a_ref.dtype)
    kb = kb_ref[seq].astype(_BF16)
    vb = vb_ref[seq].astype(_BF16)
    ob = []
    for j in range(G_B):
        qs = _stack_heads(q_ref[rows, W_A + LANES * j:W_A + LANES * (j + 1)], lo)
        s = _dot(qs, kb)
        sink = jnp.where(row_lo, sink_ref[0, j], sink_ref[1, j])
        ob.append(_unstack_heads(_softmax_pv([s], [lambda e: _dot_nt(e, vb)], sink), lo))
    b_ref[rows, :] = _rms(jnp.concatenate(ob, axis=1), _window_head_gain(gb_ref)).astype(b_ref.dtype)


def _front_kernel(sink_ref, xp_ref, xs_ref, mod_ref, g_ref, w_ref, cos_ref, sin_ref, ga_ref, gb_ref,
                  wu_ref, wd_ref, *refs, prompt_steps, mod_row):
    kat_ref, vat_ref, kbt_ref, vbt_ref, a_ref, b_ref, wu_out, wd_out = refs[0:8]
    outs_s, w_scr, q_scr = refs[8:13], refs[13], refs[14]
    i = pl.program_id(0)

    @pl.when(i == 0)
    def _():
        _prepare_in_weight(w_ref, w_scr)

    mod = mod_ref[pl.ds(mod_row(i), 1), :]

    @pl.when(i < prompt_steps)
    def _():
        wu_out[...] = wu_ref[...].astype(wu_out.dtype)
        wd_out[...] = wd_ref[...].astype(wd_out.dtype)
        _inproj_rows(xp_ref, mod, g_ref, w_scr, None, (q_scr, kat_ref, vat_ref, kbt_ref, vbt_ref))
        l = kat_ref.shape[2]
        for seq in range(kat_ref.shape[0]):
            _ctx_attention(q_scr, slice(seq * l, (seq + 1) * l), seq, kat_ref, vat_ref, kbt_ref, vbt_ref,
                           sink_ref, ga_ref, gb_ref, a_ref, b_ref)

    @pl.when(i >= prompt_steps)
    def _():
        _inproj_rows(xs_ref, mod, g_ref, w_scr, (cos_ref, sin_ref), outs_s)


def _front_call(sink, xp, xs, mod, g_pre, w_in, rope_tabs, g_a, g_b, w_up, w_down, lp, ts):
    tm = TOKEN_TILE
    tp, tsamp = xp.shape[0], xs.shape[0]
    n_p, n_s = tp // tm, tsamp // tm
    assert n_p * tm == tp and n_s * tm == tsamp and tm % lp == 0 and ts % tm == 0
    ru, rd = w_up.shape[0] // n_p, w_down.shape[0] // n_p
    assert ru * n_p == w_up.shape[0] and rd * n_p == w_down.shape[0] and ru % 16 == 0 and rd % 16 == 0
    tiles_per_seq = ts // tm
    p_tile = lambda i: jnp.minimum(i, n_p - 1)
    s_tile = lambda i: jnp.maximum(i - n_p, 0)
    mod_row = lambda i: jnp.where(i < n_p, 0, 1 + s_tile(i) // tiles_per_seq)
    p_row = lambda i: (p_tile(i), 0)
    s_row = lambda i: (s_tile(i), 0)
    const = lambda i: (0, 0)
    nseq = tm // lp
    in_specs = [
        pl.BlockSpec(memory_space=pltpu.SMEM),
        pl.BlockSpec((tm, D_MODEL), p_row),
        pl.BlockSpec((tm, D_MODEL), s_row),
        pl.BlockSpec((8, 6 * D_MODEL), const),
        pl.BlockSpec((1, D_MODEL), const),
        _resident((D_MODEL, IN_WIDTH)),
        pl.BlockSpec((tm, LANES), lambda i: (s_tile(i) % tiles_per_seq, 0)),
        pl.BlockSpec((tm, LANES), lambda i: (s_tile(i) % tiles_per_seq, 0)),
        pl.BlockSpec((1, W_A), const),
        pl.BlockSpec((1, W_B), const),
        pl.BlockSpec((ru, w_up.shape[1]), p_row),
        pl.BlockSpec((rd, w_down.shape[1]), p_row),
    ]
    kvt_shape = lambda w: jax.ShapeDtypeStruct((tp // lp, w, lp), _F32)
    kvt_spec = lambda w: pl.BlockSpec((nseq, w, lp), lambda i: (p_tile(i), 0, 0))
    kv_shape = lambda w: jax.ShapeDtypeStruct((tsamp, w), _BF16)
    kv_spec = lambda w: pl.BlockSpec((tm, w), s_row)
    out_shape = (kvt_shape(W_A), kvt_shape(W_A), kvt_shape(KV_W_B), kvt_shape(KV_W_B),
                 jax.ShapeDtypeStruct((tp, W_A), _BF16), jax.ShapeDtypeStruct((tp, W_B), _BF16),
                 jax.ShapeDtypeStruct(w_up.shape, _BF16), jax.ShapeDtypeStruct(w_down.shape, _BF16),
                 jax.ShapeDtypeStruct((tsamp, W_A + W_B), _BF16),
                 kv_shape(W_A), kv_shape(W_A), kv_shape(KV_W_B), kv_shape(KV_W_B))
    out_specs = (kvt_spec(W_A), kvt_spec(W_A), kvt_spec(KV_W_B), kvt_spec(KV_W_B),
                 pl.BlockSpec((tm, W_A), p_row), pl.BlockSpec((tm, W_B), p_row),
                 pl.BlockSpec((ru, w_up.shape[1]), p_row), pl.BlockSpec((rd, w_down.shape[1]), p_row),
                 pl.BlockSpec((tm, W_A + W_B), s_row),
                 kv_spec(W_A), kv_spec(W_A), kv_spec(KV_W_B), kv_spec(KV_W_B))
    outs = pl.pallas_call(
        functools.partial(_front_kernel, prompt_steps=n_p, mod_row=mod_row),
        grid=(n_p + n_s,),
        in_specs=in_specs,
        out_specs=out_specs,
        out_shape=out_shape,
        scratch_shapes=[pltpu.VMEM((D_MODEL, IN_WIDTH), _BF16), pltpu.VMEM((tm, W_A + W_B), _BF16)],
        compiler_params=pltpu.CompilerParams(
            dimension_semantics=("arbitrary",), vmem_limit_bytes=TAIL_VMEM_LIMIT),
        name="front",
    )(sink, xp, xs, mod, g_pre.reshape(1, D_MODEL), w_in, *rope_tabs, g_a.reshape(1, W_A),
      g_b.reshape(1, W_B), w_up, w_down)
    return outs[:6], outs[6:8], outs[8:]


def _na_plan(rows):
    kr = min(NA_ROWS, rows)
    win = kr + NA_QROWS
    row_start = lambda r: min(max(r - kr // 2, 0), rows - kr)
    blocks, classes = [], []
    for r0 in range(0, rows, NA_QROWS):
        ws = min(row_start(r0), rows - win)
        keys = []
        for r in range(r0, r0 + NA_QROWS):
            key = (ws - r + NA_ROWS - 1, row_start(r) - ws)
            if key not in classes:
                classes.append(key)
            keys.append(classes.index(key))
        blocks.append((r0, ws, keys))
    return kr, win, blocks, classes


def _na_kernel(q_ref, k_ref, v_ref, ck_ref, cv_ref, rpb_ref, ga_ref, a_ref, bias_ref, o_scr):
    rows = q_ref.shape[0] // GRID_W
    kr, win, blocks, classes = _na_plan(rows)
    m = NA_QROWS * GRID_W
    lane = lax.broadcasted_iota(jnp.int32, (GRID_W, LANES), 1)
    lo = lane < HEAD_DIM
    lo_q = lax.broadcasted_iota(jnp.int32, (m, LANES), 1) < HEAD_DIM
    cq = lax.broadcasted_iota(jnp.int32, (GRID_W, LANES), 0)
    ck = lane & (GRID_W - 1)
    cs = jnp.clip(cq - NA_COLS // 2, 0, GRID_W - NA_COLS)
    valid = (ck >= cs) & (ck < cs + NA_COLS)
    neg = jnp.full((GRID_W, LANES), NEG_INF, _F32)
    n_drow = 2 * NA_ROWS - 1

    for p in range(W_A // LANES):
        sl = slice(LANES * p, LANES * (p + 1))
        for hh in range(2):
            t_lo, t_hi = [], []
            for d in range(n_drow):
                base = jnp.broadcast_to(rpb_ref[2 * p + hh, d:d + 1, :], (GRID_W, LANES))
                t_lo.append(pltpu.roll(base, LANES - (NA_COLS - 1), 1, stride=1, stride_axis=0))
                t_hi.append(pltpu.roll(base, GRID_W - (NA_COLS - 1), 1, stride=1, stride_axis=0))
            tiles = {}
            for ci, (d_first, off) in enumerate(classes):
                for j in range(win // 2):
                    d_of = lambda i: d_first + i if off <= i < off + kr else None
                    key = (d_of(2 * j), d_of(2 * j + 1))
                    if key not in tiles:
                        left = neg if key[0] is None else t_lo[key[0]]
                        right = neg if key[1] is None else t_hi[key[1]]
                        tiles[key] = jnp.where(valid, jnp.where(lo, left, right), NEG_INF)
                    bias_ref[ci, GRID_W * hh:GRID_W * (hh + 1), LANES * j:LANES * (j + 1)] = tiles[key]

        ckt = ck_ref[0, sl, :].astype(_BF16)
        cvt = cv_ref[0, sl, :].astype(_BF16)

        for r0, ws, keys in blocks:
            q0, k0 = r0 * GRID_W, ws * GRID_W
            qs = _stack_heads(q_ref[q0:q0 + m, sl], lo_q)
            kw = k_ref[k0:k0 + win * GRID_W, sl]
            vw = v_ref[k0:k0 + win * GRID_W, sl]
            bias = jnp.concatenate(
                [bias_ref[ci, GRID_W * hh:GRID_W * (hh + 1), :] for hh in range(2) for ci in keys], axis=0)
            s_nb = _dot_nt(qs, kw) + bias
            s_cx = _dot(qs, ckt)
            o2 = _softmax_pv([s_nb, s_cx], [lambda e, vw=vw: _dot(e, vw), lambda e: _dot_nt(e, cvt)])
            o_scr[q0:q0 + m, sl] = _unstack_heads(o2, lo_q)

    a_ref[...] = _rms(o_scr[...], ga_ref[...]).astype(a_ref.dtype)


def _na_call(q, k, v, ckt, cvt, rpb_pad, g_a, nb, seq):
    past = ckt.shape[2]
    _, win, _, classes = _na_plan(seq // GRID_W)
    blk = lambda b: (b, 0)
    return pl.pallas_call(
        _na_kernel,
        grid=(nb,),
        in_specs=[
            pl.BlockSpec((seq, W_A), blk),
            pl.BlockSpec((seq, W_A), blk),
            pl.BlockSpec((seq, W_A), blk),
            pl.BlockSpec((1, W_A, past), lambda b: (b, 0, 0)),
            pl.BlockSpec((1, W_A, past), lambda b: (b, 0, 0)),
            pl.BlockSpec((H_A, 2 * NA_ROWS - 1, LANES), lambda b: (0, 0, 0)),
            pl.BlockSpec((1, W_A), lambda b: (0, 0)),
        ],
        out_specs=pl.BlockSpec((seq, W_A), blk),
        out_shape=jax.ShapeDtypeStruct((nb * seq, W_A), _BF16),
        scratch_shapes=[pltpu.VMEM((len(classes), 2 * GRID_W, win * GRID_W), _F32),
                        pltpu.VMEM((seq, W_A), _F32)],
        compiler_params=pltpu.CompilerParams(
            dimension_semantics=("parallel",), vmem_limit_bytes=VMEM_LIMIT),
        name="na_attn",
    )(q, k, v, ckt, cvt, rpb_pad, g_a.reshape(1, W_A))


def _win_kernel(sink_ref, q_ref, k_ref, v_ref, ck_ref, cv_ref, gb_ref, b_ref, o_scr):
    t = q_ref.shape[0]
    qb = WIN_QBLOCKS * BLOCK
    band = qb + 2 * BLOCK
    lo = lax.broadcasted_iota(jnp.int32, (qb, LANES), 1) < HEAD_DIM
    row_lo = lax.broadcasted_iota(jnp.int32, (2 * qb, 1), 0) < qb
    qi = lax.broadcasted_iota(jnp.int32, (2 * qb, band), 0) & (qb - 1)
    kj = lax.broadcasted_iota(jnp.int32, (2 * qb, band), 1)
    rel = qi - kj
    ckt = ck_ref[0].astype(_BF16)
    cvt = cv_ref[0].astype(_BF16)

    starts = {q0: min(max(q0 - BLOCK, 0), t - band) for q0 in range(0, t, qb)}
    masks = {off: jnp.where(jnp.abs(rel + off) <= SWA_WINDOW, 0.0, NEG_INF)
             for off in sorted({q0 - k0 for q0, k0 in starts.items()})}

    for j in range(G_B):
        sl = slice(LANES * j, LANES * (j + 1))
        sink = jnp.where(row_lo, sink_ref[0, j], sink_ref[1, j])
        for q0, k0 in starts.items():
            qs = _stack_heads(q_ref[q0:q0 + qb, sl], lo)
            s = _dot_nt(qs, k_ref[k0:k0 + band, :]) + masks[q0 - k0]
            s_cx = _dot(qs, ckt)
            vband = v_ref[k0:k0 + band, :]
            o2 = _softmax_pv([s, s_cx], [lambda e, vband=vband: _dot(e, vband), lambda e: _dot_nt(e, cvt)],
                             sink)
            o_scr[q0:q0 + qb, sl] = _unstack_heads(o2, lo)

    b_ref[...] = _rms(o_scr[...], _window_head_gain(gb_ref)).astype(b_ref.dtype)


def _win_call(sink, q, k, v, ckt, cvt, g_b, nb, seq):
    past = ckt.shape[2]
    return pl.pallas_call(
        _win_kernel,
        grid=(nb,),
        in_specs=[
            pl.BlockSpec(memory_space=pltpu.SMEM),
            pl.BlockSpec((seq, W_B), lambda b: (b, W_A // W_B)),
            pl.BlockSpec((seq, KV_W_B), lambda b: (b, 0)),
            pl.BlockSpec((seq, KV_W_B), lambda b: (b, 0)),
            pl.BlockSpec((1, KV_W_B, past), lambda b: (b, 0, 0)),
            pl.BlockSpec((1, KV_W_B, past), lambda b: (b, 0, 0)),
            pl.BlockSpec((1, W_B), lambda b: (0, 0)),
        ],
        out_specs=pl.BlockSpec((seq, W_B), lambda b: (b, 0)),
        out_shape=jax.ShapeDtypeStruct((nb * seq, W_B), _BF16),
        scratch_shapes=[pltpu.VMEM((seq, W_B), _F32)],
        compiler_params=pltpu.CompilerParams(
            dimension_semantics=("parallel",), vmem_limit_bytes=VMEM_LIMIT),
        name="win_attn",
    )(sink, q, k, v, ckt, cvt, g_b.reshape(1, W_B))


def _prepare_out_weight(w_ref, w_scr):
    for r0 in range(0, W_A, CAST_COLS):
        w_scr[r0:r0 + CAST_COLS, :] = w_ref[r0:r0 + CAST_COLS, :].astype(_BF16)
    for i, h in enumerate(_QB_HEAD_ORDER):
        w_scr[W_A + HEAD_DIM * i:W_A + HEAD_DIM * (i + 1), :] = (
            w_ref[W_A + HEAD_DIM * h:W_A + HEAD_DIM * (h + 1), :].astype(_BF16))


def _zero_row(x, row):
    r0 = row - row % 8
    sub = lax.broadcasted_iota(jnp.int32, (8, 1), 0)
    slab = jnp.where(sub == row % 8, 0.0, x[r0:r0 + 8])
    parts = ([x[:r0]] if r0 else []) + [slab] + ([x[r0 + 8:]] if r0 + 8 < x.shape[0] else [])
    return jnp.concatenate(parts, axis=0)


def _conv3(u, cw, cb, seq_len):
    tm = u.shape[0]
    prev = pltpu.roll(u, 1, 0)
    nxt = pltpu.roll(u, tm - 1, 0)
    for s0 in range(0, tm, seq_len):
        prev = _zero_row(prev, s0)
        nxt = _zero_row(nxt, s0 + seq_len - 1)
    return prev * cw[0:1, :] + u * cw[1:2, :] + nxt * cw[2:3, :] + cb


def _gated_silu(gate, val):
    half = 0.5 * gate
    return ((half + half * jnp.tanh(half)) * val).astype(_BF16)


def _up(h2_ref, w):
    tm = h2_ref.shape[0]
    return jnp.concatenate(
        [_dot(h2_ref[r:r + FFN_ROW_BLOCK, :], w) for r in range(0, tm, FFN_ROW_BLOCK)], axis=0)


def _out_proj(a_ref, b_ref, x_ref, mod, gpost_ref, gffn_ref, wo_scr, x1_ref, h2_ref):
    gt1 = mod[:, 2 * D_MODEL:3 * D_MODEL]
    sh2 = mod[:, 3 * D_MODEL:4 * D_MODEL]
    sc2 = mod[:, 4 * D_MODEL:5 * D_MODEL]
    for r in range(0, x_ref.shape[0], FFN_ROW_BLOCK):
        rows = slice(r, r + FFN_ROW_BLOCK)
        y = _dot(a_ref[rows, :], wo_scr[0:W_A, :]) + _dot(b_ref[rows, :], wo_scr[W_A:W_A + W_B, :])
        x1 = x_ref[rows, :] + gt1 * _rms(y, gpost_ref[...])
        x1_ref[rows, :] = x1
        h2_ref[rows, :] = (_rms(x1, gffn_ref[...]) * (1 + sc2) + sh2).astype(h2_ref.dtype)


def _down_proj(act_ref, wd_ref, mod, gpost2_ref, x1_ref, out_ref):
    gt2 = mod[:, 5 * D_MODEL:6 * D_MODEL]
    for r in range(0, act_ref.shape[0], DOWN_ROW_BLOCK):
        rows = slice(r, r + DOWN_ROW_BLOCK)
        y = _dot(act_ref[rows, :], wd_ref[...])
        out_ref[rows, :] = x1_ref[rows, :] + gt2 * _rms(y, gpost2_ref[...])


def _tail_seq_kernel(a_ref, b_ref, x_ref, mod_ref, gpost_ref, gffn_ref, gpost2_ref, wo_ref, wu_ref,
                     cw_ref, cb_ref, wd_ref, out_ref, wo_scr, h2_scr, act_scr, *, seq_len, mod_row):
    mod = mod_ref[pl.ds(mod_row(pl.program_id(0)), 1), :]

    @pl.when(pl.program_id(0) == 0)
    def _():
        _prepare_out_weight(wo_ref, wo_scr)

    _out_proj(a_ref, b_ref, x_ref, mod, gpost_ref, gffn_ref, wo_scr, out_ref, h2_scr)
    tc = FF_CHUNK

    def chunk(col0):
        cols = slice(col0, col0 + tc)
        return _conv3(_up(h2_scr, wu_ref[:, cols]), cw_ref[:, cols], cb_ref[:, cols], seq_len)

    for c0 in range(0, D_FF, tc):
        act_scr[:, c0:c0 + tc] = _gated_silu(chunk(c0), chunk(D_FF + c0))
    _down_proj(act_scr, wd_ref, mod, gpost2_ref, out_ref, out_ref)


def _tail_seq_call(a, b, x2d, mod, mod_row, g_post, g_ffn, g_post2, w_out, w_up, conv_w, conv_b, w_down,
                   seq_len, name):
    t = x2d.shape[0]
    tm = max(SEQ_TAIL_TILE, seq_len)
    big = tm > SEQ_TAIL_TILE
    vmem_limit = TAIL_VMEM_LIMIT if big else VMEM_LIMIT
    x_mode = dict(pipeline_mode=pl.Buffered(1)) if big else {}
    row = lambda i: (i, 0)
    const = lambda i: (0, 0)
    return pl.pallas_call(
        functools.partial(_tail_seq_kernel, seq_len=seq_len, mod_row=mod_row),
        grid=(t // tm,),
        in_specs=[
            pl.BlockSpec((tm, W_A), row),
            pl.BlockSpec((tm, W_B), row),
            pl.BlockSpec((tm, D_MODEL), row, **x_mode),
            pl.BlockSpec((8, 6 * D_MODEL), const),
            pl.BlockSpec((1, D_MODEL), const),
            pl.BlockSpec((1, D_MODEL), const),
            pl.BlockSpec((1, D_MODEL), const),
            _resident((W_A + W_B, D_MODEL)),
            _resident((D_MODEL, 2 * D_FF)),
            _resident((3, 2 * D_FF)),
            _resident((1, 2 * D_FF)),
            _resident((D_FF, D_MODEL)),
        ],
        out_specs=pl.BlockSpec((tm, D_MODEL), row),
        out_shape=jax.ShapeDtypeStruct((t, D_MODEL), _F32),
        scratch_shapes=[
            pltpu.VMEM((W_A + W_B, D_MODEL), _BF16),
            pltpu.VMEM((tm, D_MODEL), _BF16),
            pltpu.VMEM((tm, D_FF), _BF16),
        ],
        compiler_params=pltpu.CompilerParams(
            dimension_semantics=("arbitrary",), vmem_limit_bytes=vmem_limit),
        name=name,
    )(a, b, x2d, mod, g_post.reshape(1, D_MODEL), g_ffn.reshape(1, D_MODEL), g_post2.reshape(1, D_MODEL),
      w_out, w_up, conv_w, conv_b.reshape(1, 2 * D_FF), w_down)


def _rope_tables(seq):
    n = ROPE_HALF
    t = np.arange(seq)
    lane = np.arange(HEAD_DIM)
    pos = np.where(lane[None, :] < HEAD_DIM // 2, (t // GRID_W)[:, None], (t % GRID_W)[:, None])
    inv = 1.0 / (ROPE_THETA ** (np.arange(n, dtype=np.float64) / n))
    ang = pos.astype(np.float64) * inv[lane % n][None, :]
    sign = np.where((lane & n) == 0, -1.0, 1.0)[None, :]
    cos = np.tile(np.cos(ang), (1, LANES // HEAD_DIM)).astype(np.float32)
    sin = np.tile(np.sin(ang) * sign, (1, LANES // HEAD_DIM)).astype(np.float32)
    return jnp.asarray(cos), jnp.asarray(sin)


def _to_head_dim_token(cache_l):
    b, l, h, d = cache_l.shape
    return jnp.transpose(cache_l, (0, 2, 3, 1)).reshape(b, h * d, l)


def _from_head_dim_token(kv_t, heads):
    b, _, l = kv_t.shape
    return jnp.transpose(kv_t.reshape(b, 1, heads, HEAD_DIM, l), (0, 1, 4, 2, 3))


def kernel(x_prompt, x_sample, cache_a_k, cache_a_v, cache_b_k, cache_b_v, c, c_ctx, w_mod, b_mod,
           g_mix_pre, g_mix_post, g_ffn_pre, g_ffn_post, w_in, rpb_a, sink_b, g_grp_a, g_grp_b,
           w_out, w_up, conv_w, conv_b, w_down):
    bp, lp, _ = x_prompt.shape
    bs, ts, _ = x_sample.shape
    assert w_in.shape[0] == 1
    xp = x_prompt.reshape(bp * lp, D_MODEL)
    xs = x_sample.reshape(bs * ts, D_MODEL)
    cond8 = jnp.concatenate([c_ctx[None], c, jnp.zeros((8 - 1 - bs, D_MODEL), _F32)], axis=0)
    cos_tab, sin_tab = _rope_tables(ts)

    l = 0
    mod = _mod_call(cond8, w_mod[l], b_mod[l])
    rpb_pad = jnp.pad(rpb_a[l], ((0, 0), (0, 0), (0, LANES - (2 * NA_COLS - 1))))

    prompt_row = lambda i: 0
    assert ts >= SEQ_TAIL_TILE
    sample_row = lambda i: 1 + i

    (kat_p, vat_p, kbt_p, vbt_p, a_p, b_p), (w_up_b, w_down_b), (q_s, ka_s, va_s, kb_s, vb_s) = _front_call(
        sink_b[l], xp, xs, mod, g_mix_pre[l], w_in[l], (cos_tab, sin_tab), g_grp_a[l], g_grp_b[l],
        w_up[l], w_down[l], lp, ts)

    y_p = _tail_seq_call(a_p, b_p, xp, mod, prompt_row, g_mix_post[l], g_ffn_pre[l], g_ffn_post[l], w_out[l],
                         w_up_b, conv_w[l], conv_b[l], w_down_b, lp, "tail_prompt")

    a_s = _na_call(q_s, ka_s, va_s, _to_head_dim_token(cache_a_k[:, l]), _to_head_dim_token(cache_a_v[:, l]),
                   rpb_pad, g_grp_a[l], bs, ts)
    b_s = _win_call(sink_b[l], q_s, kb_s, vb_s, _to_head_dim_token(cache_b_k[:, l]),
                    _to_head_dim_token(cache_b_v[:, l]), g_grp_b[l], bs, ts)
    y_s = _tail_seq_call(a_s, b_s, xs, mod, sample_row, g_mix_post[l], g_ffn_pre[l], g_ffn_post[l], w_out[l],
                         w_up_b, conv_w[l], conv_b[l], w_down_b, ts, "tail_sample")

    return (y_p.reshape(bp, lp, D_MODEL), y_s.reshape(bs, ts, D_MODEL),
            _from_head_dim_token(kat_p, H_A), _from_head_dim_token(vat_p, H_A),
            _from_head_dim_token(kbt_p, KV_B), _from_head_dim_token(vbt_p, KV_B))
```

```python
import functools

import numpy as np
import jax
import jax.numpy as jnp
from jax import lax
from jax.experimental import pallas as pl
from jax.experimental.pallas import tpu as pltpu

D_MODEL = 1024
HEAD_DIM = 64
H_A = 8
H_B = 8
KV_B = 2
G_B = H_B // KV_B
W_A = H_A * HEAD_DIM
W_B = H_B * HEAD_DIM
KV_W_B = KV_B * HEAD_DIM
IN_WIDTH = 3 * W_A + W_B + 2 * KV_W_B
GRID_W = 64
NA_ROWS = 8
NA_COLS = 16
SWA_WINDOW = 128
BLOCK = 128
D_FF = 2816
ROPE_THETA = 10000.0
EPS = 1e-6
NEG_INF = -1e30
Q_SCALE = HEAD_DIM ** -0.5
ROPE_HALF = HEAD_DIM // 4

LANES = 128
FF_CHUNK = 256
TOKEN_TILE = 512
FFN_ROW_BLOCK = 256
DOWN_ROW_BLOCK = 256
SEQ_TAIL_TILE = 512
PROJ_ROW_BLOCK = 512
MOD_STREAMS = 2
MOD_ROW_BLOCK = 128
NA_QROWS = 4
WIN_QBLOCKS = 2
CAST_COLS = 512
VMEM_LIMIT = 48 * 1024 * 1024
TAIL_VMEM_LIMIT = 56 * 1024 * 1024

C_QA, C_KA, C_VA, C_QB, C_KB, C_VB = 0, W_A, 2 * W_A, 3 * W_A, 3 * W_A + W_B, 3 * W_A + W_B + KV_W_B

_QB_HEAD_ORDER = tuple(kv * G_B + g for g in range(G_B) for kv in range(KV_B))

_BF16 = jnp.bfloat16
_F32 = jnp.float32


def _dot(a, b):
    return jnp.dot(a, b, preferred_element_type=_F32)


def _dot_nt(a, b):
    return lax.dot_general(a, b, (((1,), (1,)), ((), ())), preferred_element_type=_F32)


def _rms(x, g):
    var = jnp.mean(x * x, axis=-1, keepdims=True)
    return x * lax.rsqrt(var + EPS) * g


def _stack_heads(q2, lo):
    zero = jnp.zeros_like(q2)
    return jnp.concatenate([jnp.where(lo, q2, zero), jnp.where(lo, zero, q2)], axis=0)


def _unstack_heads(o2, lo):
    m = o2.shape[0] // 2
    return jnp.where(lo, o2[:m], o2[m:])


def _softmax_pv(scores, pv, sink=None):
    m = scores[0].max(axis=-1, keepdims=True)
    for s in scores[1:]:
        m = jnp.maximum(m, s.max(axis=-1, keepdims=True))
    if sink is not None:
        m = jnp.maximum(m, sink)
    denom = None
    out = None
    for s, f in zip(scores, pv):
        e = jnp.exp(s - m)
        l = e.sum(axis=-1, keepdims=True)
        o = f(e.astype(_BF16))
        denom = l if denom is None else denom + l
        out = o if out is None else out + o
    if sink is not None:
        denom = denom + jnp.exp(sink - m)
    return out / denom


def _resident(shape):
    return pl.BlockSpec(shape, lambda *_: (0,) * len(shape), pipeline_mode=pl.Buffered(1))


def _mod_kernel(c_ref, *refs):
    w_refs, b_ref, o_ref = refs[:-2], refs[-2], refs[-1]
    i = pl.program_id(0)

    @pl.when(i == 0)
    def _():
        o_ref[...] = jnp.broadcast_to(b_ref[...], o_ref.shape)

    acc = o_ref[...]
    for j, w_ref in enumerate(w_refs):
        c = c_ref[i * len(w_refs) + j]
        s = (c * jax.nn.sigmoid(c)).astype(_BF16)
        acc = acc + _dot(s, w_ref[...].astype(_BF16))
    o_ref[...] = acc


def _mod_call(cond8, w_mod, b_mod):
    n = w_mod.shape[1]
    kb = MOD_ROW_BLOCK
    nblk = D_MODEL // kb
    cond_blocks = cond8.reshape(8, nblk, kb).transpose(1, 0, 2)
    return pl.pallas_call(
        _mod_kernel,
        grid=(nblk // MOD_STREAMS,),
        in_specs=[pl.BlockSpec((nblk, 8, kb), lambda i: (0, 0, 0))]
        + [pl.BlockSpec((kb, n), lambda i, j=j: (i * MOD_STREAMS + j, 0)) for j in range(MOD_STREAMS)]
        + [pl.BlockSpec((1, n), lambda i: (0, 0))],
        out_specs=pl.BlockSpec((8, n), lambda i: (0, 0)),
        out_shape=jax.ShapeDtypeStruct((8, n), _F32),
        compiler_params=pltpu.CompilerParams(dimension_semantics=("arbitrary",), vmem_limit_bytes=VMEM_LIMIT),
        name="mod",
    )(cond_blocks, *([w_mod] * MOD_STREAMS), b_mod.reshape(1, n))


def _pair_window_heads(block_of, j):
    def head_at(h, want_hi):
        blk = block_of(h // 2)
        return blk if (h % 2 == 1) == want_hi else pltpu.roll(blk, HEAD_DIM, 1)

    left, right = head_at(_QB_HEAD_ORDER[2 * j], False), head_at(_QB_HEAD_ORDER[2 * j + 1], True)
    return jnp.where(lax.broadcasted_iota(jnp.int32, left.shape, 1) < HEAD_DIM, left, right)


def _window_head_gain(gb_ref):
    block_of = lambda k: jnp.broadcast_to(gb_ref[:, LANES * k:LANES * (k + 1)], (8, LANES))
    return jnp.concatenate([_pair_window_heads(block_of, j)[0:1] for j in range(G_B)], axis=1)


def _prepare_in_weight(w_ref, w_scr):
    for c0 in list(range(0, C_QB, CAST_COLS)) + [C_KB]:
        n = min(CAST_COLS, IN_WIDTH - c0)
        w_scr[:, c0:c0 + n] = w_ref[:, c0:c0 + n].astype(_BF16)
    block_of = lambda k: w_ref[:, C_QB + LANES * k:C_QB + LANES * (k + 1)]
    for j in range(G_B):
        w_scr[:, C_QB + LANES * j:C_QB + LANES * (j + 1)] = _pair_window_heads(block_of, j).astype(_BF16)


def _rope(z, cos, sin):
    lane = lax.broadcasted_iota(jnp.int32, z.shape, 1)
    partner = jnp.where((lane & ROPE_HALF) == 0,
                        pltpu.roll(z, LANES - ROPE_HALF, 1), pltpu.roll(z, ROPE_HALF, 1))
    return z * cos + partner * sin


def _inproj_rows(x_ref, mod, g_ref, w_scr, rope_refs, outs):
    q_ref, ka_ref, va_ref, kb_ref, vb_ref = outs
    kv_transposed = rope_refs is None
    sh1 = mod[:, 0:D_MODEL]
    sc1 = mod[:, D_MODEL:2 * D_MODEL]
    rb = PROJ_ROW_BLOCK

    for r in range(0, x_ref.shape[0], rb):
        rows = slice(r, r + rb)
        h = (_rms(x_ref[rows, :], g_ref[...]) * (1 + sc1) + sh1).astype(_BF16)

        def proj(c0, n):
            return _dot(h, w_scr[:, c0:c0 + n])

        def put_kv(ref, z):
            if not kv_transposed:
                ref[rows, :] = z.astype(ref.dtype)
                return
            l = ref.shape[2]
            for t0 in range(0, rb, l):
                for c0 in range(0, z.shape[1], LANES):
                    ref[(r + t0) // l, c0:c0 + LANES, :] = z[t0:t0 + l, c0:c0 + LANES].T.astype(ref.dtype)

        q_ref[rows, 0:W_A] = (proj(C_QA, W_A) * Q_SCALE).astype(q_ref.dtype)
        put_kv(ka_ref, proj(C_KA, W_A))
        put_kv(va_ref, proj(C_VA, W_A))
        zkv = proj(C_KB, 2 * KV_W_B)
        zkb, zvb = zkv[:, :KV_W_B], zkv[:, KV_W_B:]
        zqb = proj(C_QB, W_B)
        put_kv(vb_ref, zvb)
        if rope_refs is not None:
            cos = rope_refs[0][rows, :]
            sin = rope_refs[1][rows, :]
            for j in range(W_B // LANES):
                q_ref[rows, W_A + LANES * j:W_A + LANES * (j + 1)] = (
                    _rope(zqb[:, LANES * j:LANES * (j + 1)], cos, sin) * Q_SCALE).astype(q_ref.dtype)
            put_kv(kb_ref, _rope(zkb, cos, sin))
        else:
            q_ref[rows, W_A:W_A + W_B] = (zqb * Q_SCALE).astype(q_ref.dtype)
            put_kv(kb_ref, zkb)


def _ctx_attention(q_ref, rows, seq, ka_ref, va_ref, kb_ref, vb_ref, sink_ref, ga_ref, gb_ref, a_ref, b_ref):
    l = rows.stop - rows.start
    lo = lax.broadcasted_iota(jnp.int32, (l, LANES), 1) < HEAD_DIM
    row_lo = lax.broadcasted_iota(jnp.int32, (2 * l, 1), 0) < l
    oa = []
    for p in range(W_A // LANES):
        sl = slice(LANES * p, LANES * (p + 1))
        qs = _stack_heads(q_ref[rows, sl], lo)
        s = _dot(qs, ka_ref[seq, sl, :].astype(_BF16))
        vt = va_ref[seq, sl, :].astype(_BF16)
        oa.append(_unstack_heads(_softmax_pv([s], [lambda e, vt=vt: _dot_nt(e, vt)]), lo))
    a_ref[rows, :] = _rms(jnp.concatenate(oa, axis=1), ga_ref[...]).astype(a_ref.dtype)
    kb = kb_ref[seq].astype(_BF16)
    vb = vb_ref[seq].astype(_BF16)
    ob = []
    for j in range(G_B):
        qs = _stack_heads(q_ref[rows, W_A + LANES * j:W_A + LANES * (j + 1)], lo)
        s = _dot(qs, kb)
        sink = jnp.where(row_lo, sink_ref[0, j], sink_ref[1, j])
        ob.append(_unstack_heads(_softmax_pv([s], [lambda e: _dot_nt(e, vb)], sink), lo))
    b_ref[rows, :] = _rms(jnp.concatenate(ob, axis=1), _window_head_gain(gb_ref)).astype(b_ref.dtype)


def _front_kernel(sink_ref, xp_ref, xs_ref, mod_ref, g_ref, w_ref, cos_ref, sin_ref, ga_ref, gb_ref,
                  wu_ref, wd_ref, *refs, prompt_steps, mod_row):
    kat_ref, vat_ref, kbt_ref, vbt_ref, a_ref, b_ref, wu_out, wd_out = refs[0:8]
    outs_s, w_scr, q_scr = refs[8:13], refs[13], refs[14]
    i = pl.program_id(0)

    @pl.when(i == 0)
    def _():
        _prepare_in_weight(w_ref, w_scr)

    mod = mod_ref[pl.ds(mod_row(i), 1), :]

    @pl.when(i < prompt_steps)
    def _():
        wu_out[...] = wu_ref[...].astype(wu_out.dtype)
        wd_out[...] = wd_ref[...].astype(wd_out.dtype)
        _inproj_rows(xp_ref, mod, g_ref, w_scr, None, (q_scr, kat_ref, vat_ref, kbt_ref, vbt_ref))
        l = kat_ref.shape[2]
        for seq in range(kat_ref.shape[0]):
            _ctx_attention(q_scr, slice(seq * l, (seq + 1) * l), seq, kat_ref, vat_ref, kbt_ref, vbt_ref,
                           sink_ref, ga_ref, gb_ref, a_ref, b_ref)

    @pl.when(i >= prompt_steps)
    def _():
        _inproj_rows(xs_ref, mod, g_ref, w_scr, (cos_ref, sin_ref), outs_s)


def _front_call(sink, xp, xs, mod, g_pre, w_in, rope_tabs, g_a, g_b, w_up, w_down, lp, ts):
    tm = TOKEN_TILE
    tp, tsamp = xp.shape[0], xs.shape[0]
    n_p, n_s = tp // tm, tsamp // tm
    assert n_p * tm == tp and n_s * tm == tsamp and tm % lp == 0 and ts % tm == 0
    ru, rd = w_up.shape[0] // n_p, w_down.shape[0] // n_p
    assert ru * n_p == w_up.shape[0] and rd * n_p == w_down.shape[0] and ru % 16 == 0 and rd % 16 == 0
    tiles_per_seq = ts // tm
    p_tile = lambda i: jnp.minimum(i, n_p - 1)
    s_tile = lambda i: jnp.maximum(i - n_p, 0)
    mod_row = lambda i: jnp.where(i < n_p, 0, 1 + s_tile(i) // tiles_per_seq)
    p_row = lambda i: (p_tile(i), 0)
    s_row = lambda i: (s_tile(i), 0)
    const = lambda i: (0, 0)
    nseq = tm // lp
    in_specs = [
        pl.BlockSpec(memory_space=pltpu.SMEM),
        pl.BlockSpec((tm, D_MODEL), p_row),
        pl.BlockSpec((tm, D_MODEL), s_row),
        pl.BlockSpec((8, 6 * D_MODEL), const),
        pl.BlockSpec((1, D_MODEL), const),
        _resident((D_MODEL, IN_WIDTH)),
        pl.BlockSpec((tm, LANES), lambda i: (s_tile(i) % tiles_per_seq, 0)),
        pl.BlockSpec((tm, LANES), lambda i: (s_tile(i) % tiles_per_seq, 0)),
        pl.BlockSpec((1, W_A), const),
        pl.BlockSpec((1, W_B), const),
        pl.BlockSpec((ru, w_up.shape[1]), p_row),
        pl.BlockSpec((rd, w_down.shape[1]), p_row),
    ]
    kvt_shape = lambda w: jax.ShapeDtypeStruct((tp // lp, w, lp), _F32)
    kvt_spec = lambda w: pl.BlockSpec((nseq, w, lp), lambda i: (p_tile(i), 0, 0))
    kv_shape = lambda w: jax.ShapeDtypeStruct((tsamp, w), _BF16)
    kv_spec = lambda w: pl.BlockSpec((tm, w), s_row)
    out_shape = (kvt_shape(W_A), kvt_shape(W_A), kvt_shape(KV_W_B), kvt_shape(KV_W_B),
                 jax.ShapeDtypeStruct((tp, W_A), _BF16), jax.ShapeDtypeStruct((tp, W_B), _BF16),
                 jax.ShapeDtypeStruct(w_up.shape, _BF16), jax.ShapeDtypeStruct(w_down.shape, _BF16),
                 jax.ShapeDtypeStruct((tsamp, W_A + W_B), _BF16),
                 kv_shape(W_A), kv_shape(W_A), kv_shape(KV_W_B), kv_shape(KV_W_B))
    out_specs = (kvt_spec(W_A), kvt_spec(W_A), kvt_spec(KV_W_B), kvt_spec(KV_W_B),
                 pl.BlockSpec((tm, W_A), p_row), pl.BlockSpec((tm, W_B), p_row),
                 pl.BlockSpec((ru, w_up.shape[1]), p_row), pl.BlockSpec((rd, w_down.shape[1]), p_row),
                 pl.BlockSpec((tm, W_A + W_B), s_row),
                 kv_spec(W_A), kv_spec(W_A), kv_spec(KV_W_B), kv_spec(KV_W_B))
    outs = pl.pallas_call(
        functools.partial(_front_kernel, prompt_steps=n_p, mod_row=mod_row),
        grid=(n_p + n_s,),
        in_specs=in_specs,
        out_specs=out_specs,
        out_shape=out_shape,
        scratch_shapes=[pltpu.VMEM((D_MODEL, IN_WIDTH), _BF16), pltpu.VMEM((tm, W_A + W_B), _BF16)],
        compiler_params=pltpu.CompilerParams(
            dimension_semantics=("arbitrary",), vmem_limit_bytes=TAIL_VMEM_LIMIT),
        name="front",
    )(sink, xp, xs, mod, g_pre.reshape(1, D_MODEL), w_in, *rope_tabs, g_a.reshape(1, W_A),
      g_b.reshape(1, W_B), w_up, w_down)
    return outs[:6], outs[6:8], outs[8:]


def _na_plan(rows):
    kr = min(NA_ROWS, rows)
    win = kr + NA_QROWS
    row_start = lambda r: min(max(r - kr // 2, 0), rows - kr)
    blocks, classes = [], []
    for r0 in range(0, rows, NA_QROWS):
        ws = min(row_start(r0), rows - win)
        keys = []
        for r in range(r0, r0 + NA_QROWS):
            key = (ws - r + NA_ROWS - 1, row_start(r) - ws)
            if key not in classes:
                classes.append(key)
            keys.append(classes.index(key))
        blocks.append((r0, ws, keys))
    return kr, win, blocks, classes


def _na_kernel(q_ref, k_ref, v_ref, ck_ref, cv_ref, rpb_ref, ga_ref, a_ref, bias_ref, o_scr):
    rows = q_ref.shape[0] // GRID_W
    kr, win, blocks, classes = _na_plan(rows)
    m = NA_QROWS * GRID_W
    lane = lax.broadcasted_iota(jnp.int32, (GRID_W, LANES), 1)
    lo = lane < HEAD_DIM
    lo_q = lax.broadcasted_iota(jnp.int32, (m, LANES), 1) < HEAD_DIM
    cq = lax.broadcasted_iota(jnp.int32, (GRID_W, LANES), 0)
    ck = lane & (GRID_W - 1)
    cs = jnp.clip(cq - NA_COLS // 2, 0, GRID_W - NA_COLS)
    valid = (ck >= cs) & (ck < cs + NA_COLS)
    neg = jnp.full((GRID_W, LANES), NEG_INF, _F32)
    n_drow = 2 * NA_ROWS - 1

    for p in range(W_A // LANES):
        sl = slice(LANES * p, LANES * (p + 1))
        for hh in range(2):
            t_lo, t_hi = [], []
            for d in range(n_drow):
                base = jnp.broadcast_to(rpb_ref[2 * p + hh, d:d + 1, :], (GRID_W, LANES))
                t_lo.append(pltpu.roll(base, LANES - (NA_COLS - 1), 1, stride=1, stride_axis=0))
                t_hi.append(pltpu.roll(base, GRID_W - (NA_COLS - 1), 1, stride=1, stride_axis=0))
            tiles = {}
            for ci, (d_first, off) in enumerate(classes):
                for j in range(win // 2):
                    d_of = lambda i: d_first + i if off <= i < off + kr else None
                    key = (d_of(2 * j), d_of(2 * j + 1))
                    if key not in tiles:
                        left = neg if key[0] is None else t_lo[key[0]]
                        right = neg if key[1] is None else t_hi[key[1]]
                        tiles[key] = jnp.where(valid, jnp.where(lo, left, right), NEG_INF)
                    bias_ref[ci, GRID_W * hh:GRID_W * (hh + 1), LANES * j:LANES * (j + 1)] = tiles[key]

        ckt = ck_ref[0, sl, :].astype(_BF16)
        cvt = cv_ref[0, sl, :].astype(_BF16)

        for r0, ws, keys in blocks:
            q0, k0 = r0 * GRID_W, ws * GRID_W
            qs = _stack_heads(q_ref[q0:q0 + m, sl], lo_q)
            kw = k_ref[k0:k0 + win * GRID_W, sl]
            vw = v_ref[k0:k0 + win * GRID_W, sl]
            bias = jnp.concatenate(
                [bias_ref[ci, GRID_W * hh:GRID_W * (hh + 1), :] for hh in range(2) for ci in keys], axis=0)
            s_nb = _dot_nt(qs, kw) + bias
            s_cx = _dot(qs, ckt)
            o2 = _softmax_pv([s_nb, s_cx], [lambda e, vw=vw: _dot(e, vw), lambda e: _dot_nt(e, cvt)])
            o_scr[q0:q0 + m, sl] = _unstack_heads(o2, lo_q)

    a_ref[...] = _rms(o_scr[...], ga_ref[...]).astype(a_ref.dtype)


def _na_call(q, k, v, ckt, cvt, rpb_pad, g_a, nb, seq):
    past = ckt.shape[2]
    _, win, _, classes = _na_plan(seq // GRID_W)
    blk = lambda b: (b, 0)
    return pl.pallas_call(
        _na_kernel,
        grid=(nb,),
        in_specs=[
            pl.BlockSpec((seq, W_A), blk),
            pl.BlockSpec((seq, W_A), blk),
            pl.BlockSpec((seq, W_A), blk),
            pl.BlockSpec((1, W_A, past), lambda b: (b, 0, 0)),
            pl.BlockSpec((1, W_A, past), lambda b: (b, 0, 0)),
            pl.BlockSpec((H_A, 2 * NA_ROWS - 1, LANES), lambda b: (0, 0, 0)),
            pl.BlockSpec((1, W_A), lambda b: (0, 0)),
        ],
        out_specs=pl.BlockSpec((seq, W_A), blk),
        out_shape=jax.ShapeDtypeStruct((nb * seq, W_A), _BF16),
        scratch_shapes=[pltpu.VMEM((len(classes), 2 * GRID_W, win * GRID_W), _F32),
                        pltpu.VMEM((seq, W_A), _F32)],
        compiler_params=pltpu.CompilerParams(
            dimension_semantics=("parallel",), vmem_limit_bytes=VMEM_LIMIT),
        name="na_attn",
    )(q, k, v, ckt, cvt, rpb_pad, g_a.reshape(1, W_A))


def _win_kernel(sink_ref, q_ref, k_ref, v_ref, ck_ref, cv_ref, gb_ref, b_ref, o_scr):
    t = q_ref.shape[0]
    qb = WIN_QBLOCKS * BLOCK
    band = qb + 2 * BLOCK
    lo = lax.broadcasted_iota(jnp.int32, (qb, LANES), 1) < HEAD_DIM
    row_lo = lax.broadcasted_iota(jnp.int32, (2 * qb, 1), 0) < qb
    qi = lax.broadcasted_iota(jnp.int32, (2 * qb, band), 0) & (qb - 1)
    kj = lax.broadcasted_iota(jnp.int32, (2 * qb, band), 1)
    rel = qi - kj
    ckt = ck_ref[0].astype(_BF16)
    cvt = cv_ref[0].astype(_BF16)

    starts = {q0: min(max(q0 - BLOCK, 0), t - band) for q0 in range(0, t, qb)}
    masks = {off: jnp.where(jnp.abs(rel + off) <= SWA_WINDOW, 0.0, NEG_INF)
             for off in sorted({q0 - k0 for q0, k0 in starts.items()})}

    for j in range(G_B):
        sl = slice(LANES * j, LANES * (j + 1))
        sink = jnp.where(row_lo, sink_ref[0, j], sink_ref[1, j])
        for q0, k0 in starts.items():
            qs = _stack_heads(q_ref[q0:q0 + qb, sl], lo)
            s = _dot_nt(qs, k_ref[k0:k0 + band, :]) + masks[q0 - k0]
            s_cx = _dot(qs, ckt)
            vband = v_ref[k0:k0 + band, :]
            o2 = _softmax_pv([s, s_cx], [lambda e, vband=vband: _dot(e, vband), lambda e: _dot_nt(e, cvt)],
                             sink)
            o_scr[q0:q0 + qb, sl] = _unstack_heads(o2, lo)

    b_ref[...] = _rms(o_scr[...], _window_head_gain(gb_ref)).astype(b_ref.dtype)


def _win_call(sink, q, k, v, ckt, cvt, g_b, nb, seq):
    past = ckt.shape[2]
    return pl.pallas_call(
        _win_kernel,
        grid=(nb,),
        in_specs=[
            pl.BlockSpec(memory_space=pltpu.SMEM),
            pl.BlockSpec((seq, W_B), lambda b: (b, W_A // W_B)),
            pl.BlockSpec((seq, KV_W_B), lambda b: (b, 0)),
            pl.BlockSpec((seq, KV_W_B), lambda b: (b, 0)),
            pl.BlockSpec((1, KV_W_B, past), lambda b: (b, 0, 0)),
            pl.BlockSpec((1, KV_W_B, past), lambda b: (b, 0, 0)),
            pl.BlockSpec((1, W_B), lambda b: (0, 0)),
        ],
        out_specs=pl.BlockSpec((seq, W_B), lambda b: (b, 0)),
        out_shape=jax.ShapeDtypeStruct((nb * seq, W_B), _BF16),
        scratch_shapes=[pltpu.VMEM((seq, W_B), _F32)],
        compiler_params=pltpu.CompilerParams(
            dimension_semantics=("parallel",), vmem_limit_bytes=VMEM_LIMIT),
        name="win_attn",
    )(sink, q, k, v, ckt, cvt, g_b.reshape(1, W_B))


def _prepare_out_weight(w_ref, w_scr):
    for r0 in range(0, W_A, CAST_COLS):
        w_scr[r0:r0 + CAST_COLS, :] = w_ref[r0:r0 + CAST_COLS, :].astype(_BF16)
    for i, h in enumerate(_QB_HEAD_ORDER):
        w_scr[W_A + HEAD_DIM * i:W_A + HEAD_DIM * (i + 1), :] = (
            w_ref[W_A + HEAD_DIM * h:W_A + HEAD_DIM * (h + 1), :].astype(_BF16))


def _zero_row(x, row):
    r0 = row - row % 8
    sub = lax.broadcasted_iota(jnp.int32, (8, 1), 0)
    slab = jnp.where(sub == row % 8, 0.0, x[r0:r0 + 8])
    parts = ([x[:r0]] if r0 else []) + [slab] + ([x[r0 + 8:]] if r0 + 8 < x.shape[0] else [])
    return jnp.concatenate(parts, axis=0)


def _conv3(u, cw, cb, seq_len):
    tm = u.shape[0]
    prev = pltpu.roll(u, 1, 0)
    nxt = pltpu.roll(u, tm - 1, 0)
    for s0 in range(0, tm, seq_len):
        prev = _zero_row(prev, s0)
        nxt = _zero_row(nxt, s0 + seq_len - 1)
    return prev * cw[0:1, :] + u * cw[1:2, :] + nxt * cw[2:3, :] + cb


def _gated_silu(gate, val):
    half = 0.5 * gate
    return ((half + half * jnp.tanh(half)) * val).astype(_BF16)


def _up(h2_ref, w):
    tm = h2_ref.shape[0]
    return jnp.concatenate(
        [_dot(h2_ref[r:r + FFN_ROW_BLOCK, :], w) for r in range(0, tm, FFN_ROW_BLOCK)], axis=0)


def _out_proj(a_ref, b_ref, x_ref, mod, gpost_ref, gffn_ref, wo_scr, x1_ref, h2_ref):
    gt1 = mod[:, 2 * D_MODEL:3 * D_MODEL]
    sh2 = mod[:, 3 * D_MODEL:4 * D_MODEL]
    sc2 = mod[:, 4 * D_MODEL:5 * D_MODEL]
    for r in range(0, x_ref.shape[0], FFN_ROW_BLOCK):
        rows = slice(r, r + FFN_ROW_BLOCK)
        y = _dot(a_ref[rows, :], wo_scr[0:W_A, :]) + _dot(b_ref[rows, :], wo_scr[W_A:W_A + W_B, :])
        x1 = x_ref[rows, :] + gt1 * _rms(y, gpost_ref[...])
        x1_ref[rows, :] = x1
        h2_ref[rows, :] = (_rms(x1, gffn_ref[...]) * (1 + sc2) + sh2).astype(h2_ref.dtype)


def _down_proj(act_ref, wd_ref, mod, gpost2_ref, x1_ref, out_ref):
    gt2 = mod[:, 5 * D_MODEL:6 * D_MODEL]
    for r in range(0, act_ref.shape[0], DOWN_ROW_BLOCK):
        rows = slice(r, r + DOWN_ROW_BLOCK)
        y = _dot(act_ref[rows, :], wd_ref[...])
        out_ref[rows, :] = x1_ref[rows, :] + gt2 * _rms(y, gpost2_ref[...])


def _tail_seq_kernel(a_ref, b_ref, x_ref, mod_ref, gpost_ref, gffn_ref, gpost2_ref, wo_ref, wu_ref,
                     cw_ref, cb_ref, wd_ref, out_ref, wo_scr, h2_scr, act_scr, *, seq_len, mod_row):
    mod = mod_ref[pl.ds(mod_row(pl.program_id(0)), 1), :]

    @pl.when(pl.program_id(0) == 0)
    def _():
        _prepare_out_weight(wo_ref, wo_scr)

    _out_proj(a_ref, b_ref, x_ref, mod, gpost_ref, gffn_ref, wo_scr, out_ref, h2_scr)
    tc = FF_CHUNK

    def chunk(col0):
        cols = slice(col0, col0 + tc)
        return _conv3(_up(h2_scr, wu_ref[:, cols]), cw_ref[:, cols], cb_ref[:, cols], seq_len)

    for c0 in range(0, D_FF, tc):
        act_scr[:, c0:c0 + tc] = _gated_silu(chunk(c0), chunk(D_FF + c0))
    _down_proj(act_scr, wd_ref, mod, gpost2_ref, out_ref, out_ref)


def _tail_seq_call(a, b, x2d, mod, mod_row, g_post, g_ffn, g_post2, w_out, w_up, conv_w, conv_b, w_down,
                   seq_len, name):
    t = x2d.shape[0]
    tm = max(SEQ_TAIL_TILE, seq_len)
    big = tm > SEQ_TAIL_TILE
    vmem_limit = TAIL_VMEM_LIMIT if big else VMEM_LIMIT
    x_mode = dict(pipeline_mode=pl.Buffered(1)) if big else {}
    row = lambda i: (i, 0)
    const = lambda i: (0, 0)
    return pl.pallas_call(
        functools.partial(_tail_seq_kernel, seq_len=seq_len, mod_row=mod_row),
        grid=(t // tm,),
        in_specs=[
            pl.BlockSpec((tm, W_A), row),
            pl.BlockSpec((tm, W_B), row),
            pl.BlockSpec((tm, D_MODEL), row, **x_mode),
            pl.BlockSpec((8, 6 * D_MODEL), const),
            pl.BlockSpec((1, D_MODEL), const),
            pl.BlockSpec((1, D_MODEL), const),
            pl.BlockSpec((1, D_MODEL), const),
            _resident((W_A + W_B, D_MODEL)),
            _resident((D_MODEL, 2 * D_FF)),
            _resident((3, 2 * D_FF)),
            _resident((1, 2 * D_FF)),
            _resident((D_FF, D_MODEL)),
        ],
        out_specs=pl.BlockSpec((tm, D_MODEL), row),
        out_shape=jax.ShapeDtypeStruct((t, D_MODEL), _F32),
        scratch_shapes=[
            pltpu.VMEM((W_A + W_B, D_MODEL), _BF16),
            pltpu.VMEM((tm, D_MODEL), _BF16),
            pltpu.VMEM((tm, D_FF), _BF16),
        ],
        compiler_params=pltpu.CompilerParams(
            dimension_semantics=("arbitrary",), vmem_limit_bytes=vmem_limit),
        name=name,
    )(a, b, x2d, mod, g_post.reshape(1, D_MODEL), g_ffn.reshape(1, D_MODEL), g_post2.reshape(1, D_MODEL),
      w_out, w_up, conv_w, conv_b.reshape(1, 2 * D_FF), w_down)


def _rope_tables(seq):
    n = ROPE_HALF
    t = np.arange(seq)
    lane = np.arange(HEAD_DIM)
    pos = np.where(lane[None, :] < HEAD_DIM // 2, (t // GRID_W)[:, None], (t % GRID_W)[:, None])
    inv = 1.0 / (ROPE_THETA ** (np.arange(n, dtype=np.float64) / n))
    ang = pos.astype(np.float64) * inv[lane % n][None, :]
    sign = np.where((lane & n) == 0, -1.0, 1.0)[None, :]
    cos = np.tile(np.cos(ang), (1, LANES // HEAD_DIM)).astype(np.float32)
    sin = np.tile(np.sin(ang) * sign, (1, LANES // HEAD_DIM)).astype(np.float32)
    return jnp.asarray(cos), jnp.asarray(sin)


def _to_head_dim_token(cache_l):
    b, l, h, d = cache_l.shape
    return jnp.transpose(cache_l, (0, 2, 3, 1)).reshape(b, h * d, l)


def _from_head_dim_token(kv_t, heads):
    b, _, l = kv_t.shape
    return jnp.transpose(kv_t.reshape(b, 1, heads, HEAD_DIM, l), (0, 1, 4, 2, 3))


def kernel(x_prompt, x_sample, cache_a_k, cache_a_v, cache_b_k, cache_b_v, c, c_ctx, w_mod, b_mod,
           g_mix_pre, g_mix_post, g_ffn_pre, g_ffn_post, w_in, rpb_a, sink_b, g_grp_a, g_grp_b,
           w_out, w_up, conv_w, conv_b, w_down):
    bp, lp, _ = x_prompt.shape
    bs, ts, _ = x_sample.shape
    assert w_in.shape[0] == 1
    xp = x_prompt.reshape(bp * lp, D_MODEL)
    xs = x_sample.reshape(bs * ts, D_MODEL)
    cond8 = jnp.concatenate([c_ctx[None], c, jnp.zeros((8 - 1 - bs, D_MODEL), _F32)], axis=0)
    cos_tab, sin_tab = _rope_tables(ts)

    l = 0
    mod = _mod_call(cond8, w_mod[l], b_mod[l])
    rpb_pad = jnp.pad(rpb_a[l], ((0, 0), (0, 0), (0, LANES - (2 * NA_COLS - 1))))

    prompt_row = lambda i: 0
    assert ts >= SEQ_TAIL_TILE
    sample_row = lambda i: 1 + i

    (kat_p, vat_p, kbt_p, vbt_p, a_p, b_p), (w_up_b, w_down_b), (q_s, ka_s, va_s, kb_s, vb_s) = _front_call(
        sink_b[l], xp, xs, mod, g_mix_pre[l], w_in[l], (cos_tab, sin_tab), g_grp_a[l], g_grp_b[l],
        w_up[l], w_down[l], lp, ts)

    y_p = _tail_seq_call(a_p, b_p, xp, mod, prompt_row, g_mix_post[l], g_ffn_pre[l], g_ffn_post[l], w_out[l],
                         w_up_b, conv_w[l], conv_b[l], w_down_b, lp, "tail_prompt")

    a_s = _na_call(q_s, ka_s, va_s, _to_head_dim_token(cache_a_k[:, l]), _to_head_dim_token(cache_a_v[:, l]),
                   rpb_pad, g_grp_a[l], bs, ts)
    b_s = _win_call(sink_b[l], q_s, kb_s, vb_s, _to_head_dim_token(cache_b_k[:, l]),
                    _to_head_dim_token(cache_b_v[:, l]), g_grp_b[l], bs, ts)
    y_s = _tail_seq_call(a_s, b_s, xs, mod, sample_row, g_mix_post[l], g_ffn_pre[l], g_ffn_post[l], w_out[l],
                         w_up_b, conv_w[l], conv_b[l], w_down_b, ts, "tail_sample")

    return (y_p.reshape(bp, lp, D_MODEL), y_s.reshape(bs, ts, D_MODEL),
            _from_head_dim_token(kat_p, H_A), _from_head_dim_token(vat_p, H_A),
            _from_head_dim_token(kbt_p, KV_B), _from_head_dim_token(vbt_p, KV_B))
```

```python
import functools

import numpy as np
import jax
import jax.numpy as jnp
from jax import lax
from jax.experimental import pallas as pl
from jax.experimental.pallas import tpu as pltpu

D_MODEL = 1024
HEAD_DIM = 64
H_A = 8
H_B = 8
KV_B = 2
G_B = H_B // KV_B
W_A = H_A * HEAD_DIM
W_B = H_B * HEAD_DIM
KV_W_B = KV_B * HEAD_DIM
IN_WIDTH = 3 * W_A + W_B + 2 * KV_W_B
GRID_W = 64
NA_ROWS = 8
NA_COLS = 16
SWA_WINDOW = 128
BLOCK = 128
D_FF = 2816
ROPE_THETA = 10000.0
EPS = 1e-6
NEG_INF = -1e30
Q_SCALE = HEAD_DIM ** -0.5
ROPE_HALF = HEAD_DIM // 4

LANES = 128
SUBLANES = 8
BF16_SUBLANES = 16
FF_CHUNK = 256
TOKEN_TILE = 512
PROJ_ROW_BLOCK = 256
SEQ_TAIL_TILE = 512
FFN_ROW_BLOCK = 256
DOWN_ROW_BLOCK = 256
MOD_ROWS = SUBLANES
MOD_STREAMS = 2
MOD_ROW_BLOCK = 128
NA_QROWS = 4
WIN_QBLOCKS = 2
CAST_COLS = 512
VMEM_LIMIT = 48 * 1024 * 1024
BIG_VMEM_LIMIT = 56 * 1024 * 1024

C_QA, C_KA, C_VA, C_QB, C_KB, C_VB = 0, W_A, 2 * W_A, 3 * W_A, 3 * W_A + W_B, 3 * W_A + W_B + KV_W_B

_QB_HEAD_ORDER = tuple(kv * G_B + g for g in range(G_B) for kv in range(KV_B))

_BF16 = jnp.bfloat16
_F32 = jnp.float32


def _dot(a, b):
    return jnp.dot(a, b, preferred_element_type=_F32)


def _dot_nt(a, b):
    return lax.dot_general(a, b, (((1,), (1,)), ((), ())), preferred_element_type=_F32)


def _rms(x, g):
    var = jnp.mean(x * x, axis=-1, keepdims=True)
    return x * lax.rsqrt(var + EPS) * g


def _stack_heads(q2, lo):
    zero = jnp.zeros_like(q2)
    return jnp.concatenate([jnp.where(lo, q2, zero), jnp.where(lo, zero, q2)], axis=0)


def _unstack_heads(o2, lo):
    m = o2.shape[0] // 2
    return jnp.where(lo, o2[:m], o2[m:])


def _softmax_pv(scores, pv, sink=None):
    m = scores[0].max(axis=-1, keepdims=True)
    for s in scores[1:]:
        m = jnp.maximum(m, s.max(axis=-1, keepdims=True))
    if sink is not None:
        m = jnp.maximum(m, sink)
    denom = None
    out = None
    for s, f in zip(scores, pv):
        e = jnp.exp(s - m)
        l = e.sum(axis=-1, keepdims=True)
        o = f(e.astype(_BF16))
        denom = l if denom is None else denom + l
        out = o if out is None else out + o
    if sink is not None:
        denom = denom + jnp.exp(sink - m)
    return out / denom


def _resident(shape):
    return pl.BlockSpec(shape, lambda *_: (0,) * len(shape), pipeline_mode=pl.Buffered(1))


def _mod_kernel(c_ref, *refs):
    w_refs, b_ref, o_ref = refs[:-2], refs[-2], refs[-1]
    i = pl.program_id(0)

    @pl.when(i == 0)
    def _():
        o_ref[...] = jnp.broadcast_to(b_ref[...], o_ref.shape)

    acc = o_ref[...]
    for j, w_ref in enumerate(w_refs):
        c = c_ref[i * len(w_refs) + j]
        s = (c * jax.nn.sigmoid(c)).astype(_BF16)
        acc = acc + _dot(s, w_ref[...].astype(_BF16))
    o_ref[...] = acc


def _mod_call(cond, w_mod, b_mod):
    n = w_mod.shape[1]
    kb = MOD_ROW_BLOCK
    nblk = D_MODEL // kb
    cond_blocks = cond.reshape(MOD_ROWS, nblk, kb).transpose(1, 0, 2)
    return pl.pallas_call(
        _mod_kernel,
        grid=(nblk // MOD_STREAMS,),
        in_specs=[pl.BlockSpec((nblk, MOD_ROWS, kb), lambda i: (0, 0, 0))]
        + [pl.BlockSpec((kb, n), lambda i, j=j: (i * MOD_STREAMS + j, 0)) for j in range(MOD_STREAMS)]
        + [pl.BlockSpec((1, n), lambda i: (0, 0))],
        out_specs=pl.BlockSpec((MOD_ROWS, n), lambda i: (0, 0)),
        out_shape=jax.ShapeDtypeStruct((MOD_ROWS, n), _F32),
        compiler_params=pltpu.CompilerParams(dimension_semantics=("arbitrary",), vmem_limit_bytes=VMEM_LIMIT),
        name="mod",
    )(cond_blocks, *([w_mod] * MOD_STREAMS), b_mod.reshape(1, n))


def _pair_window_heads(block_of, j):
    def head_at(h, want_hi):
        blk = block_of(h // 2)
        return blk if (h % 2 == 1) == want_hi else pltpu.roll(blk, HEAD_DIM, 1)

    left, right = head_at(_QB_HEAD_ORDER[2 * j], False), head_at(_QB_HEAD_ORDER[2 * j + 1], True)
    return jnp.where(lax.broadcasted_iota(jnp.int32, left.shape, 1) < HEAD_DIM, left, right)


def _window_head_gain(gb_ref):
    block_of = lambda k: jnp.broadcast_to(gb_ref[:, LANES * k:LANES * (k + 1)], (SUBLANES, LANES))
    return jnp.concatenate([_pair_window_heads(block_of, j)[0:1] for j in range(G_B)], axis=1)


def _prepare_in_weight(w_ref, w_scr):
    for c0 in list(range(0, C_QB, CAST_COLS)) + [C_KB]:
        n = min(CAST_COLS, IN_WIDTH - c0)
        w_scr[:, c0:c0 + n] = w_ref[:, c0:c0 + n].astype(_BF16)
    block_of = lambda k: w_ref[:, C_QB + LANES * k:C_QB + LANES * (k + 1)]
    for j in range(G_B):
        w_scr[:, C_QB + LANES * j:C_QB + LANES * (j + 1)] = _pair_window_heads(block_of, j).astype(_BF16)


def _rope(z, cos, sin):
    lane = lax.broadcasted_iota(jnp.int32, z.shape, 1)
    partner = jnp.where((lane & ROPE_HALF) == 0,
                        pltpu.roll(z, LANES - ROPE_HALF, 1), pltpu.roll(z, ROPE_HALF, 1))
    return z * cos + partner * sin


def _inproj_rows(x_ref, mod, g_ref, w_scr, rope_refs, outs):
    q_ref, ka_ref, va_ref, kb_ref, vb_ref = outs
    kv_transposed = rope_refs is None
    sh1 = mod[:, 0:D_MODEL]
    sc1 = mod[:, D_MODEL:2 * D_MODEL]
    rb = PROJ_ROW_BLOCK

    for r in range(0, x_ref.shape[0], rb):
        rows = slice(r, r + rb)
        h = (_rms(x_ref[rows, :], g_ref[...]) * (1 + sc1) + sh1).astype(_BF16)

        def proj(c0, n):
            return _dot(h, w_scr[:, c0:c0 + n])

        def put_kv(ref, z):
            if not kv_transposed:
                ref[rows, :] = z.astype(ref.dtype)
                return
            l = ref.shape[2]
            for t0 in range(0, rb, l):
                for c0 in range(0, z.shape[1], LANES):
                    ref[(r + t0) // l, c0:c0 + LANES, :] = z[t0:t0 + l, c0:c0 + LANES].T.astype(ref.dtype)

        q_ref[rows, 0:W_A] = (proj(C_QA, W_A) * Q_SCALE).astype(q_ref.dtype)
        put_kv(ka_ref, proj(C_KA, W_A))
        put_kv(va_ref, proj(C_VA, W_A))
        zkv = proj(C_KB, 2 * KV_W_B)
        zkb, zvb = zkv[:, :KV_W_B], zkv[:, KV_W_B:]
        zqb = proj(C_QB, W_B)
        put_kv(vb_ref, zvb)
        if rope_refs is not None:
            cos = rope_refs[0][rows, :]
            sin = rope_refs[1][rows, :]
            for j in range(W_B // LANES):
                q_ref[rows, W_A + LANES * j:W_A + LANES * (j + 1)] = (
                    _rope(zqb[:, LANES * j:LANES * (j + 1)], cos, sin) * Q_SCALE).astype(q_ref.dtype)
            put_kv(kb_ref, _rope(zkb, cos, sin))
        else:
            q_ref[rows, W_A:W_A + W_B] = (zqb * Q_SCALE).astype(q_ref.dtype)
            put_kv(kb_ref, zkb)


def _ctx_attention(q_ref, rows, seq, ka_ref, va_ref, kb_ref, vb_ref, sink_ref, ga_ref, gb_ref, a_ref, b_ref):
    l = rows.stop - rows.start
    lo = lax.broadcasted_iota(jnp.int32, (l, LANES), 1) < HEAD_DIM
    row_lo = lax.broadcasted_iota(jnp.int32, (2 * l, 1), 0) < l
    oa = []
    for p in range(W_A // LANES):
        sl = slice(LANES * p, LANES * (p + 1))
        qs = _stack_heads(q_ref[rows, sl], lo)
        s = _dot(qs, ka_ref[seq, sl, :].astype(_BF16))
        vt = va_ref[seq, sl, :].astype(_BF16)
        oa.append(_unstack_heads(_softmax_pv([s], [lambda e, vt=vt: _dot_nt(e, vt)]), lo))
    a_ref[rows, :] = _rms(jnp.concatenate(oa, axis=1), ga_ref[...]).astype(a_ref.dtype)
    kb = kb_ref[seq].astype(_BF16)
    vb = vb_ref[seq].astype(_BF16)
    ob = []
    for j in range(G_B):
        qs = _stack_heads(q_ref[rows, W_A + LANES * j:W_A + LANES * (j + 1)], lo)
        s = _dot(qs, kb)
        sink = jnp.where(row_lo, sink_ref[0, j], sink_ref[1, j])
        ob.append(_unstack_heads(_softmax_pv([s], [lambda e: _dot_nt(e, vb)], sink), lo))
    b_ref[rows, :] = _rms(jnp.concatenate(ob, axis=1), _window_head_gain(gb_ref)).astype(b_ref.dtype)


def _front_kernel(sink_ref, xp_ref, xs_ref, mod_ref, g_ref, w_ref, cos_ref, sin_ref, ga_ref, gb_ref,
                  wu_ref, wd_ref, *refs, prompt_steps, mod_row):
    kat_ref, vat_ref, kbt_ref, vbt_ref, a_ref, b_ref, wu_out, wd_out = refs[0:8]
    outs_s, w_scr, q_scr = refs[8:13], refs[13], refs[14]
    i = pl.program_id(0)

    @pl.when(i == 0)
    def _():
        _prepare_in_weight(w_ref, w_scr)

    mod = mod_ref[pl.ds(mod_row(i), 1), :]

    @pl.when(i < prompt_steps)
    def _():
        wu_out[...] = wu_ref[...].astype(wu_out.dtype)
        wd_out[...] = wd_ref[...].astype(wd_out.dtype)
        _inproj_rows(xp_ref, mod, g_ref, w_scr, None, (q_scr, kat_ref, vat_ref, kbt_ref, vbt_ref))
        l = kat_ref.shape[2]
        for seq in range(kat_ref.shape[0]):
            _ctx_attention(q_scr, slice(seq * l, (seq + 1) * l), seq, kat_ref, vat_ref, kbt_ref, vbt_ref,
                           sink_ref, ga_ref, gb_ref, a_ref, b_ref)

    @pl.when(i >= prompt_steps)
    def _():
        _inproj_rows(xs_ref, mod, g_ref, w_scr, (cos_ref, sin_ref), outs_s)


def _front_call(sink, xp, xs, mod, g_pre, w_in, rope_tabs, g_a, g_b, w_up, w_down, lp, ts):
    tm = TOKEN_TILE
    tp, tsamp = xp.shape[0], xs.shape[0]
    n_p, n_s = tp // tm, tsamp // tm
    assert n_p * tm == tp and n_s * tm == tsamp and tm % lp == 0 and ts % tm == 0
    ru, rd = w_up.shape[0] // n_p, w_down.shape[0] // n_p
    assert ru * n_p == w_up.shape[0] and rd * n_p == w_down.shape[0]
    assert ru % BF16_SUBLANES == 0 and rd % BF16_SUBLANES == 0
    tiles_per_seq = ts // tm
    p_tile = lambda i: jnp.minimum(i, n_p - 1)
    s_tile = lambda i: jnp.maximum(i - n_p, 0)
    mod_row = lambda i: jnp.where(i < n_p, 0, 1 + s_tile(i) // tiles_per_seq)
    p_row = lambda i: (p_tile(i), 0)
    s_row = lambda i: (s_tile(i), 0)
    const = lambda i: (0, 0)
    nseq = tm // lp
    in_specs = [
        pl.BlockSpec(memory_space=pltpu.SMEM),
        pl.BlockSpec((tm, D_MODEL), p_row),
        pl.BlockSpec((tm, D_MODEL), s_row),
        pl.BlockSpec((MOD_ROWS, 6 * D_MODEL), const),
        pl.BlockSpec((1, D_MODEL), const),
        _resident((D_MODEL, IN_WIDTH)),
        pl.BlockSpec((tm, LANES), lambda i: (s_tile(i) % tiles_per_seq, 0)),
        pl.BlockSpec((tm, LANES), lambda i: (s_tile(i) % tiles_per_seq, 0)),
        pl.BlockSpec((1, W_A), const),
        pl.BlockSpec((1, W_B), const),
        pl.BlockSpec((ru, w_up.shape[1]), p_row),
        pl.BlockSpec((rd, w_down.shape[1]), p_row),
    ]
    kvt_shape = lambda w: jax.ShapeDtypeStruct((tp // lp, w, lp), _F32)
    kvt_spec = lambda w: pl.BlockSpec((nseq, w, lp), lambda i: (p_tile(i), 0, 0))
    kv_shape = lambda w: jax.ShapeDtypeStruct((tsamp, w), _BF16)
    kv_spec = lambda w: pl.BlockSpec((tm, w), s_row)
    out_shape = (kvt_shape(W_A), kvt_shape(W_A), kvt_shape(KV_W_B), kvt_shape(KV_W_B),
                 jax.ShapeDtypeStruct((tp, W_A), _BF16), jax.ShapeDtypeStruct((tp, W_B), _BF16),
                 jax.ShapeDtypeStruct(w_up.shape, _BF16), jax.ShapeDtypeStruct(w_down.shape, _BF16),
                 jax.ShapeDtypeStruct((tsamp, W_A + W_B), _BF16),
                 kv_shape(W_A), kv_shape(W_A), kv_shape(KV_W_B), kv_shape(KV_W_B))
    out_specs = (kvt_spec(W_A), kvt_spec(W_A), kvt_spec(KV_W_B), kvt_spec(KV_W_B),
                 pl.BlockSpec((tm, W_A), p_row), pl.BlockSpec((tm, W_B), p_row),
                 pl.BlockSpec((ru, w_up.shape[1]), p_row), pl.BlockSpec((rd, w_down.shape[1]), p_row),
                 pl.BlockSpec((tm, W_A + W_B), s_row),
                 kv_spec(W_A), kv_spec(W_A), kv_spec(KV_W_B), kv_spec(KV_W_B))
    outs = pl.pallas_call(
        functools.partial(_front_kernel, prompt_steps=n_p, mod_row=mod_row),
        grid=(n_p + n_s,),
        in_specs=in_specs,
        out_specs=out_specs,
        out_shape=out_shape,
        scratch_shapes=[pltpu.VMEM((D_MODEL, IN_WIDTH), _BF16), pltpu.VMEM((tm, W_A + W_B), _BF16)],
        compiler_params=pltpu.CompilerParams(
            dimension_semantics=("arbitrary",), vmem_limit_bytes=BIG_VMEM_LIMIT),
        name="front",
    )(sink, xp, xs, mod, g_pre.reshape(1, D_MODEL), w_in, *rope_tabs, g_a.reshape(1, W_A),
      g_b.reshape(1, W_B), w_up, w_down)
    return outs[:6], outs[6:8], outs[8:]


def _na_plan(rows):
    kr = min(NA_ROWS, rows)
    win = kr + NA_QROWS
    row_start = lambda r: min(max(r - kr // 2, 0), rows - kr)
    blocks, classes = [], []
    for r0 in range(0, rows, NA_QROWS):
        ws = min(row_start(r0), rows - win)
        keys = []
        for r in range(r0, r0 + NA_QROWS):
            key = (ws - r + NA_ROWS - 1, row_start(r) - ws)
            if key not in classes:
                classes.append(key)
            keys.append(classes.index(key))
        blocks.append((r0, ws, keys))
    return kr, win, blocks, classes


def _na_kernel(q_ref, k_ref, v_ref, ck_ref, cv_ref, rpb_ref, ga_ref, a_ref, bias_ref, o_scr):
    rows = q_ref.shape[0] // GRID_W
    kr, win, blocks, classes = _na_plan(rows)
    m = NA_QROWS * GRID_W
    lane = lax.broadcasted_iota(jnp.int32, (GRID_W, LANES), 1)
    lo = lane < HEAD_DIM
    lo_q = lax.broadcasted_iota(jnp.int32, (m, LANES), 1) < HEAD_DIM
    cq = lax.broadcasted_iota(jnp.int32, (GRID_W, LANES), 0)
    ck = lane & (GRID_W - 1)
    cs = jnp.clip(cq - NA_COLS // 2, 0, GRID_W - NA_COLS)
    valid = (ck >= cs) & (ck < cs + NA_COLS)
    neg = jnp.full((GRID_W, LANES), NEG_INF, _F32)
    n_drow = 2 * NA_ROWS - 1

    for p in range(W_A // LANES):
        sl = slice(LANES * p, LANES * (p + 1))
        for hh in range(2):
            t_lo, t_hi = [], []
            for d in range(n_drow):
                base = jnp.broadcast_to(rpb_ref[2 * p + hh, d:d + 1, :], (GRID_W, LANES))
                t_lo.append(pltpu.roll(base, LANES - (NA_COLS - 1), 1, stride=1, stride_axis=0))
                t_hi.append(pltpu.roll(base, GRID_W - (NA_COLS - 1), 1, stride=1, stride_axis=0))
            tiles = {}
            for ci, (d_first, off) in enumerate(classes):
                for j in range(win // 2):
                    d_of = lambda i: d_first + i if off <= i < off + kr else None
                    key = (d_of(2 * j), d_of(2 * j + 1))
                    if key not in tiles:
                        left = neg if key[0] is None else t_lo[key[0]]
                        right = neg if key[1] is None else t_hi[key[1]]
                        tiles[key] = jnp.where(valid, jnp.where(lo, left, right), NEG_INF)
                    bias_ref[ci, GRID_W * hh:GRID_W * (hh + 1), LANES * j:LANES * (j + 1)] = tiles[key]

        ckt = ck_ref[0, sl, :].astype(_BF16)
        cvt = cv_ref[0, sl, :].astype(_BF16)

        for r0, ws, keys in blocks:
            q0, k0 = r0 * GRID_W, ws * GRID_W
            qs = _stack_heads(q_ref[q0:q0 + m, sl], lo_q)
            kw = k_ref[k0:k0 + win * GRID_W, sl]
            vw = v_ref[k0:k0 + win * GRID_W, sl]
            bias = jnp.concatenate(
                [bias_ref[ci, GRID_W * hh:GRID_W * (hh + 1), :] for hh in range(2) for ci in keys], axis=0)
            s_nb = _dot_nt(qs, kw) + bias
            s_cx = _dot(qs, ckt)
            o2 = _softmax_pv([s_nb, s_cx], [lambda e, vw=vw: _dot(e, vw), lambda e: _dot_nt(e, cvt)])
            o_scr[q0:q0 + m, sl] = _unstack_heads(o2, lo_q)

    a_ref[...] = _rms(o_scr[...], ga_ref[...]).astype(a_ref.dtype)


def _na_call(q, k, v, ckt, cvt, rpb_pad, g_a, nb, seq):
    past = ckt.shape[2]
    _, win, _, classes = _na_plan(seq // GRID_W)
    blk = lambda b: (b, 0)
    return pl.pallas_call(
        _na_kernel,
        grid=(nb,),
        in_specs=[
            pl.BlockSpec((seq, W_A), blk),
            pl.BlockSpec((seq, W_A), blk),
            pl.BlockSpec((seq, W_A), blk),
            pl.BlockSpec((1, W_A, past), lambda b: (b, 0, 0)),
            pl.BlockSpec((1, W_A, past), lambda b: (b, 0, 0)),
            pl.BlockSpec((H_A, 2 * NA_ROWS - 1, LANES), lambda b: (0, 0, 0)),
            pl.BlockSpec((1, W_A), lambda b: (0, 0)),
        ],
        out_specs=pl.BlockSpec((seq, W_A), blk),
        out_shape=jax.ShapeDtypeStruct((nb * seq, W_A), _BF16),
        scratch_shapes=[pltpu.VMEM((len(classes), 2 * GRID_W, win * GRID_W), _F32),
                        pltpu.VMEM((seq, W_A), _F32)],
        compiler_params=pltpu.CompilerParams(
            dimension_semantics=("parallel",), vmem_limit_bytes=VMEM_LIMIT),
        name="na_attn",
    )(q, k, v, ckt, cvt, rpb_pad, g_a.reshape(1, W_A))


def _win_kernel(sink_ref, q_ref, k_ref, v_ref, ck_ref, cv_ref, gb_ref, b_ref, o_scr):
    t = q_ref.shape[0]
    qb = WIN_QBLOCKS * BLOCK
    band = qb + 2 * BLOCK
    lo = lax.broadcasted_iota(jnp.int32, (qb, LANES), 1) < HEAD_DIM
    row_lo = lax.broadcasted_iota(jnp.int32, (2 * qb, 1), 0) < qb
    qi = lax.broadcasted_iota(jnp.int32, (2 * qb, band), 0) & (qb - 1)
    kj = lax.broadcasted_iota(jnp.int32, (2 * qb, band), 1)
    rel = qi - kj
    ckt = ck_ref[0].astype(_BF16)
    cvt = cv_ref[0].astype(_BF16)

    starts = {q0: min(max(q0 - BLOCK, 0), t - band) for q0 in range(0, t, qb)}
    masks = {off: jnp.where(jnp.abs(rel + off) <= SWA_WINDOW, 0.0, NEG_INF)
             for off in sorted({q0 - k0 for q0, k0 in starts.items()})}

    for j in range(G_B):
        sl = slice(LANES * j, LANES * (j + 1))
        sink = jnp.where(row_lo, sink_ref[0, j], sink_ref[1, j])
        for q0, k0 in starts.items():
            qs = _stack_heads(q_ref[q0:q0 + qb, sl], lo)
            s = _dot_nt(qs, k_ref[k0:k0 + band, :]) + masks[q0 - k0]
            s_cx = _dot(qs, ckt)
            vband = v_ref[k0:k0 + band, :]
            o2 = _softmax_pv([s, s_cx], [lambda e, vband=vband: _dot(e, vband), lambda e: _dot_nt(e, cvt)],
                             sink)
            o_scr[q0:q0 + qb, sl] = _unstack_heads(o2, lo)

    b_ref[...] = _rms(o_scr[...], _window_head_gain(gb_ref)).astype(b_ref.dtype)


def _win_call(sink, q, k, v, ckt, cvt, g_b, nb, seq):
    past = ckt.shape[2]
    return pl.pallas_call(
        _win_kernel,
        grid=(nb,),
        in_specs=[
            pl.BlockSpec(memory_space=pltpu.SMEM),
            pl.BlockSpec((seq, W_B), lambda b: (b, W_A // W_B)),
            pl.BlockSpec((seq, KV_W_B), lambda b: (b, 0)),
            pl.BlockSpec((seq, KV_W_B), lambda b: (b, 0)),
            pl.BlockSpec((1, KV_W_B, past), lambda b: (b, 0, 0)),
            pl.BlockSpec((1, KV_W_B, past), lambda b: (b, 0, 0)),
            pl.BlockSpec((1, W_B), lambda b: (0, 0)),
        ],
        out_specs=pl.BlockSpec((seq, W_B), lambda b: (b, 0)),
        out_shape=jax.ShapeDtypeStruct((nb * seq, W_B), _BF16),
        scratch_shapes=[pltpu.VMEM((seq, W_B), _F32)],
        compiler_params=pltpu.CompilerParams(
            dimension_semantics=("parallel",), vmem_limit_bytes=VMEM_LIMIT),
        name="win_attn",
    )(sink, q, k, v, ckt, cvt, g_b.reshape(1, W_B))


def _prepare_out_weight(w_ref, w_scr):
    for r0 in range(0, W_A, CAST_COLS):
        w_scr[r0:r0 + CAST_COLS, :] = w_ref[r0:r0 + CAST_COLS, :].astype(_BF16)
    for i, h in enumerate(_QB_HEAD_ORDER):
        w_scr[W_A + HEAD_DIM * i:W_A + HEAD_DIM * (i + 1), :] = (
            w_ref[W_A + HEAD_DIM * h:W_A + HEAD_DIM * (h + 1), :].astype(_BF16))


def _zero_row(x, row):
    r0 = row - row % SUBLANES
    sub = lax.broadcasted_iota(jnp.int32, (SUBLANES, 1), 0)
    slab = jnp.where(sub == row % SUBLANES, 0.0, x[r0:r0 + SUBLANES])
    parts = ([x[:r0]] if r0 else []) + [slab] + ([x[r0 + SUBLANES:]] if r0 + SUBLANES < x.shape[0] else [])
    return jnp.concatenate(parts, axis=0)


def _conv3(u, cw, cb, seq_len):
    tm = u.shape[0]
    prev = pltpu.roll(u, 1, 0)
    nxt = pltpu.roll(u, tm - 1, 0)
    for s0 in range(0, tm, seq_len):
        prev = _zero_row(prev, s0)
        nxt = _zero_row(nxt, s0 + seq_len - 1)
    return prev * cw[0:1, :] + u * cw[1:2, :] + nxt * cw[2:3, :] + cb


def _gated_silu(gate, val):
    half = 0.5 * gate
    return ((half + half * jnp.tanh(half)) * val).astype(_BF16)


def _up(h2_ref, w):
    tm = h2_ref.shape[0]
    return jnp.concatenate(
        [_dot(h2_ref[r:r + FFN_ROW_BLOCK, :], w) for r in range(0, tm, FFN_ROW_BLOCK)], axis=0)


def _out_proj(a_ref, b_ref, x_ref, mod, gpost_ref, gffn_ref, wo_scr, x1_ref, h2_ref):
    gt1 = mod[:, 2 * D_MODEL:3 * D_MODEL]
    sh2 = mod[:, 3 * D_MODEL:4 * D_MODEL]
    sc2 = mod[:, 4 * D_MODEL:5 * D_MODEL]
    for r in range(0, x_ref.shape[0], FFN_ROW_BLOCK):
        rows = slice(r, r + FFN_ROW_BLOCK)
        y = _dot(a_ref[rows, :], wo_scr[0:W_A, :]) + _dot(b_ref[rows, :], wo_scr[W_A:W_A + W_B, :])
        x1 = x_ref[rows, :] + gt1 * _rms(y, gpost_ref[...])
        x1_ref[rows, :] = x1
        h2_ref[rows, :] = (_rms(x1, gffn_ref[...]) * (1 + sc2) + sh2).astype(h2_ref.dtype)


def _down_proj(act_ref, wd_ref, mod, gpost2_ref, x1_ref, out_ref):
    gt2 = mod[:, 5 * D_MODEL:6 * D_MODEL]
    for r in range(0, act_ref.shape[0], DOWN_ROW_BLOCK):
        rows = slice(r, r + DOWN_ROW_BLOCK)
        y = _dot(act_ref[rows, :], wd_ref[...])
        out_ref[rows, :] = x1_ref[rows, :] + gt2 * _rms(y, gpost2_ref[...])


def _tail_seq_kernel(a_ref, b_ref, x_ref, mod_ref, gpost_ref, gffn_ref, gpost2_ref, wo_ref, wu_ref,
                     cw_ref, cb_ref, wd_ref, out_ref, wo_scr, h2_scr, act_scr, *, seq_len, mod_row):
    mod = mod_ref[pl.ds(mod_row(pl.program_id(0)), 1), :]

    @pl.when(pl.program_id(0) == 0)
    def _():
        _prepare_out_weight(wo_ref, wo_scr)

    _out_proj(a_ref, b_ref, x_ref, mod, gpost_ref, gffn_ref, wo_scr, out_ref, h2_scr)
    tc = FF_CHUNK

    def chunk(col0):
        cols = slice(col0, col0 + tc)
        return _conv3(_up(h2_scr, wu_ref[:, cols]), cw_ref[:, cols], cb_ref[:, cols], seq_len)

    for c0 in range(0, D_FF, tc):
        act_scr[:, c0:c0 + tc] = _gated_silu(chunk(c0), chunk(D_FF + c0))
    _down_proj(act_scr, wd_ref, mod, gpost2_ref, out_ref, out_ref)


def _tail_seq_call(a, b, x2d, mod, mod_row, g_post, g_ffn, g_post2, w_out, w_up, conv_w, conv_b, w_down,
                   seq_len, name):
    t = x2d.shape[0]
    tm = max(SEQ_TAIL_TILE, seq_len)
    big = tm > SEQ_TAIL_TILE
    vmem_limit = BIG_VMEM_LIMIT if big else VMEM_LIMIT
    x_mode = dict(pipeline_mode=pl.Buffered(1)) if big else {}
    row = lambda i: (i, 0)
    const = lambda i: (0, 0)
    return pl.pallas_call(
        functools.partial(_tail_seq_kernel, seq_len=seq_len, mod_row=mod_row),
        grid=(t // tm,),
        in_specs=[
            pl.BlockSpec((tm, W_A), row),
            pl.BlockSpec((tm, W_B), row),
            pl.BlockSpec((tm, D_MODEL), row, **x_mode),
            pl.BlockSpec((MOD_ROWS, 6 * D_MODEL), const),
            pl.BlockSpec((1, D_MODEL), const),
            pl.BlockSpec((1, D_MODEL), const),
            pl.BlockSpec((1, D_MODEL), const),
            _resident((W_A + W_B, D_MODEL)),
            _resident((D_MODEL, 2 * D_FF)),
            _resident((3, 2 * D_FF)),
            _resident((1, 2 * D_FF)),
            _resident((D_FF, D_MODEL)),
        ],
        out_specs=pl.BlockSpec((tm, D_MODEL), row),
        out_shape=jax.ShapeDtypeStruct((t, D_MODEL), _F32),
        scratch_shapes=[
            pltpu.VMEM((W_A + W_B, D_MODEL), _BF16),
            pltpu.VMEM((tm, D_MODEL), _BF16),
            pltpu.VMEM((tm, D_FF), _BF16),
        ],
        compiler_params=pltpu.CompilerParams(
            dimension_semantics=("arbitrary",), vmem_limit_bytes=vmem_limit),
        name=name,
    )(a, b, x2d, mod, g_post.reshape(1, D_MODEL), g_ffn.reshape(1, D_MODEL), g_post2.reshape(1, D_MODEL),
      w_out, w_up, conv_w, conv_b.reshape(1, 2 * D_FF), w_down)


def _rope_tables(seq):
    n = ROPE_HALF
    t = np.arange(seq)
    lane = np.arange(HEAD_DIM)
    pos = np.where(lane[None, :] < HEAD_DIM // 2, (t // GRID_W)[:, None], (t % GRID_W)[:, None])
    inv = 1.0 / (ROPE_THETA ** (np.arange(n, dtype=np.float64) / n))
    ang = pos.astype(np.float64) * inv[lane % n][None, :]
    sign = np.where((lane & n) == 0, -1.0, 1.0)[None, :]
    cos = np.tile(np.cos(ang), (1, LANES // HEAD_DIM)).astype(np.float32)
    sin = np.tile(np.sin(ang) * sign, (1, LANES // HEAD_DIM)).astype(np.float32)
    return jnp.asarray(cos), jnp.asarray(sin)


def _to_head_dim_token(cache_l):
    b, l, h, d = cache_l.shape
    return jnp.transpose(cache_l, (0, 2, 3, 1)).reshape(b, h * d, l)


def _from_head_dim_token(kv_t, heads):
    b, _, l = kv_t.shape
    return jnp.transpose(kv_t.reshape(b, 1, heads, HEAD_DIM, l), (0, 1, 4, 2, 3))


def kernel(x_prompt, x_sample, cache_a_k, cache_a_v, cache_b_k, cache_b_v, c, c_ctx, w_mod, b_mod,
           g_mix_pre, g_mix_post, g_ffn_pre, g_ffn_post, w_in, rpb_a, sink_b, g_grp_a, g_grp_b,
           w_out, w_up, conv_w, conv_b, w_down):
    bp, lp, _ = x_prompt.shape
    bs, ts, _ = x_sample.shape
    assert w_in.shape[0] == 1
    xp = x_prompt.reshape(bp * lp, D_MODEL)
    xs = x_sample.reshape(bs * ts, D_MODEL)
    assert 1 + bs <= MOD_ROWS
    cond = jnp.concatenate([c_ctx[None], c, jnp.zeros((MOD_ROWS - 1 - bs, D_MODEL), _F32)], axis=0)
    cos_tab, sin_tab = _rope_tables(ts)

    l = 0
    mod = _mod_call(cond, w_mod[l], b_mod[l])
    rpb_pad = jnp.pad(rpb_a[l], ((0, 0), (0, 0), (0, LANES - (2 * NA_COLS - 1))))

    prompt_row = lambda i: 0
    assert ts >= SEQ_TAIL_TILE
    sample_row = lambda i: 1 + i

    (kat_p, vat_p, kbt_p, vbt_p, a_p, b_p), (w_up_b, w_down_b), (q_s, ka_s, va_s, kb_s, vb_s) = _front_call(
        sink_b[l], xp, xs, mod, g_mix_pre[l], w_in[l], (cos_tab, sin_tab), g_grp_a[l], g_grp_b[l],
        w_up[l], w_down[l], lp, ts)

    y_p = _tail_seq_call(a_p, b_p, xp, mod, prompt_row, g_mix_post[l], g_ffn_pre[l], g_ffn_post[l], w_out[l],
                         w_up_b, conv_w[l], conv_b[l], w_down_b, lp, "tail_prompt")

    a_s = _na_call(q_s, ka_s, va_s, _to_head_dim_token(cache_a_k[:, l]), _to_head_dim_token(cache_a_v[:, l]),
                   rpb_pad, g_grp_a[l], bs, ts)
    b_s = _win_call(sink_b[l], q_s, kb_s, vb_s, _to_head_dim_token(cache_b_k[:, l]),
                    _to_head_dim_token(cache_b_v[:, l]), g_grp_b[l], bs, ts)
    y_s = _tail_seq_call(a_s, b_s, xs, mod, sample_row, g_mix_post[l], g_ffn_pre[l], g_ffn_post[l], w_out[l],
                         w_up_b, conv_w[l], conv_b[l], w_down_b, ts, "tail_sample")

    return (y_p.reshape(bp, lp, D_MODEL), y_s.reshape(bs, ts, D_MODEL),
            _from_head_dim_token(kat_p, H_A), _from_head_dim_token(vat_p, H_A),
            _from_head_dim_token(kbt_p, KV_B), _from_head_dim_token(vbt_p, KV_B))
```

```python
import functools

import numpy as np
import jax
import jax.numpy as jnp
from jax import lax
from jax.experimental import pallas as pl
from jax.experimental.pallas import tpu as pltpu

D_MODEL = 1024
HEAD_DIM = 64
H_A = 8
H_B = 8
KV_B = 2
G_B = H_B // KV_B
W_A = H_A * HEAD_DIM
W_B = H_B * HEAD_DIM
KV_W_B = KV_B * HEAD_DIM
IN_WIDTH = 3 * W_A + W_B + 2 * KV_W_B
GRID_W = 64
NA_ROWS = 8
NA_COLS = 16
SWA_WINDOW = 128
BLOCK = 128
D_FF = 2816
ROPE_THETA = 10000.0
EPS = 1e-6
NEG_INF = -1e30
Q_SCALE = HEAD_DIM ** -0.5
ROPE_HALF = HEAD_DIM // 4

LANES = 128
SUBLANES = 8
BF16_SUBLANES = 16
FF_CHUNK = 256
TOKEN_TILE = 512
PROJ_ROW_BLOCK = 256
SEQ_TAIL_TILE = 512
FFN_ROW_BLOCK = 256
DOWN_ROW_BLOCK = 256
MOD_ROWS = SUBLANES
MOD_STREAMS = 2
MOD_ROW_BLOCK = 128
NA_QROWS = 4
WIN_QBLOCKS = 2
CAST_COLS = 512
VMEM_LIMIT = 48 * 1024 * 1024
BIG_VMEM_LIMIT = 56 * 1024 * 1024

C_QA, C_KA, C_VA, C_QB, C_KB, C_VB = 0, W_A, 2 * W_A, 3 * W_A, 3 * W_A + W_B, 3 * W_A + W_B + KV_W_B

_QB_HEAD_ORDER = tuple(kv * G_B + g for g in range(G_B) for kv in range(KV_B))

_BF16 = jnp.bfloat16
_F32 = jnp.float32


def _dot(a, b):
    return jnp.dot(a, b, preferred_element_type=_F32)


def _dot_nt(a, b):
    return lax.dot_general(a, b, (((1,), (1,)), ((), ())), preferred_element_type=_F32)


def _rms(x, g):
    var = jnp.mean(x * x, axis=-1, keepdims=True)
    return x * lax.rsqrt(var + EPS) * g


def _stack_heads(q2, lo):
    zero = jnp.zeros_like(q2)
    return jnp.concatenate([jnp.where(lo, q2, zero), jnp.where(lo, zero, q2)], axis=0)


def _unstack_heads(o2, lo):
    m = o2.shape[0] // 2
    return jnp.where(lo, o2[:m], o2[m:])


def _softmax_pv(scores, pv, sink=None):
    m = scores[0].max(axis=-1, keepdims=True)
    for s in scores[1:]:
        m = jnp.maximum(m, s.max(axis=-1, keepdims=True))
    if sink is not None:
        m = jnp.maximum(m, sink)
    denom = None
    out = None
    for s, f in zip(scores, pv):
        e = jnp.exp(s - m)
        l = e.sum(axis=-1, keepdims=True)
        o = f(e.astype(_BF16))
        denom = l if denom is None else denom + l
        out = o if out is None else out + o
    if sink is not None:
        denom = denom + jnp.exp(sink - m)
    return out / denom


def _resident(shape):
    return pl.BlockSpec(shape, lambda *_: (0,) * len(shape), pipeline_mode=pl.Buffered(1))


def _mod_kernel(c_ref, *refs):
    w_refs, b_ref, o_ref = refs[:-2], refs[-2], refs[-1]
    i = pl.program_id(0)

    @pl.when(i == 0)
    def _():
        o_ref[...] = jnp.broadcast_to(b_ref[...], o_ref.shape)

    acc = o_ref[...]
    for j, w_ref in enumerate(w_refs):
        c = c_ref[i * len(w_refs) + j]
        s = (c * jax.nn.sigmoid(c)).astype(_BF16)
        acc = acc + _dot(s, w_ref[...].astype(_BF16))
    o_ref[...] = acc


def _mod_call(cond, w_mod, b_mod):
    n = w_mod.shape[1]
    kb = MOD_ROW_BLOCK
    nblk = D_MODEL // kb
    cond_blocks = cond.reshape(MOD_ROWS, nblk, kb).transpose(1, 0, 2)
    return pl.pallas_call(
        _mod_kernel,
        grid=(nblk // MOD_STREAMS,),
        in_specs=[pl.BlockSpec((nblk, MOD_ROWS, kb), lambda i: (0, 0, 0))]
        + [pl.BlockSpec((kb, n), lambda i, j=j: (i * MOD_STREAMS + j, 0)) for j in range(MOD_STREAMS)]
        + [pl.BlockSpec((1, n), lambda i: (0, 0))],
        out_specs=pl.BlockSpec((MOD_ROWS, n), lambda i: (0, 0)),
        out_shape=jax.ShapeDtypeStruct((MOD_ROWS, n), _F32),
        compiler_params=pltpu.CompilerParams(dimension_semantics=("arbitrary",), vmem_limit_bytes=VMEM_LIMIT),
        name="mod",
    )(cond_blocks, *([w_mod] * MOD_STREAMS), b_mod.reshape(1, n))


def _pair_window_heads(block_of, j):
    def head_at(h, want_hi):
        blk = block_of(h // 2)
        return blk if (h % 2 == 1) == want_hi else pltpu.roll(blk, HEAD_DIM, 1)

    left, right = head_at(_QB_HEAD_ORDER[2 * j], False), head_at(_QB_HEAD_ORDER[2 * j + 1], True)
    return jnp.where(lax.broadcasted_iota(jnp.int32, left.shape, 1) < HEAD_DIM, left, right)


def _window_head_gain(gb_ref):
    block_of = lambda k: jnp.broadcast_to(gb_ref[:, LANES * k:LANES * (k + 1)], (SUBLANES, LANES))
    return jnp.concatenate([_pair_window_heads(block_of, j)[0:1] for j in range(G_B)], axis=1)


def _prepare_in_weight(w_ref, w_scr):
    for c0 in list(range(0, C_QB, CAST_COLS)) + [C_KB]:
        n = min(CAST_COLS, IN_WIDTH - c0)
        w_scr[:, c0:c0 + n] = w_ref[:, c0:c0 + n].astype(_BF16)
    block_of = lambda k: w_ref[:, C_QB + LANES * k:C_QB + LANES * (k + 1)]
    for j in range(G_B):
        w_scr[:, C_QB + LANES * j:C_QB + LANES * (j + 1)] = _pair_window_heads(block_of, j).astype(_BF16)


def _rope(z, cos, sin):
    lane = lax.broadcasted_iota(jnp.int32, z.shape, 1)
    partner = jnp.where((lane & ROPE_HALF) == 0,
                        pltpu.roll(z, LANES - ROPE_HALF, 1), pltpu.roll(z, ROPE_HALF, 1))
    return z * cos + partner * sin


def _inproj_rows(x_ref, mod, g_ref, w_scr, rope_refs, outs):
    q_ref, ka_ref, va_ref, kb_ref, vb_ref = outs
    kv_transposed = rope_refs is None
    sh1 = mod[:, 0:D_MODEL]
    sc1 = mod[:, D_MODEL:2 * D_MODEL]
    rb = PROJ_ROW_BLOCK

    for r in range(0, x_ref.shape[0], rb):
        rows = slice(r, r + rb)
        h = (_rms(x_ref[rows, :], g_ref[...]) * (1 + sc1) + sh1).astype(_BF16)

        def proj(c0, n):
            return _dot(h, w_scr[:, c0:c0 + n])

        def put_kv(ref, z):
            if not kv_transposed:
                ref[rows, :] = z.astype(ref.dtype)
                return
            l = ref.shape[2]
            for t0 in range(0, rb, l):
                for c0 in range(0, z.shape[1], LANES):
                    ref[(r + t0) // l, c0:c0 + LANES, :] = z[t0:t0 + l, c0:c0 + LANES].T.astype(ref.dtype)

        q_ref[rows, 0:W_A] = (proj(C_QA, W_A) * Q_SCALE).astype(q_ref.dtype)
        put_kv(ka_ref, proj(C_KA, W_A))
        put_kv(va_ref, proj(C_VA, W_A))
        zkv = proj(C_KB, 2 * KV_W_B)
        zkb, zvb = zkv[:, :KV_W_B], zkv[:, KV_W_B:]
        zqb = proj(C_QB, W_B)
        put_kv(vb_ref, zvb)
        if rope_refs is not None:
            cos = rope_refs[0][rows, :]
            sin = rope_refs[1][rows, :]
            for j in range(W_B // LANES):
                q_ref[rows, W_A + LANES * j:W_A + LANES * (j + 1)] = (
                    _rope(zqb[:, LANES * j:LANES * (j + 1)], cos, sin) * Q_SCALE).astype(q_ref.dtype)
            put_kv(kb_ref, _rope(zkb, cos, sin))
        else:
            q_ref[rows, W_A:W_A + W_B] = (zqb * Q_SCALE).astype(q_ref.dtype)
            put_kv(kb_ref, zkb)


def _ctx_attention(q_ref, rows, seq, ka_ref, va_ref, kb_ref, vb_ref, sink_ref, ga_ref, gb_ref, a_ref, b_ref):
    l = rows.stop - rows.start
    lo = lax.broadcasted_iota(jnp.int32, (l, LANES), 1) < HEAD_DIM
    row_lo = lax.broadcasted_iota(jnp.int32, (2 * l, 1), 0) < l
    oa = []
    for p in range(W_A // LANES):
        sl = slice(LANES * p, LANES * (p + 1))
        qs = _stack_heads(q_ref[rows, sl], lo)
        s = _dot(qs, ka_ref[seq, sl, :].astype(_BF16))
        vt = va_ref[seq, sl, :].astype(_BF16)
        oa.append(_unstack_heads(_softmax_pv([s], [lambda e, vt=vt: _dot_nt(e, vt)]), lo))
    a_ref[rows, :] = _rms(jnp.concatenate(oa, axis=1), ga_ref[...]).astype(a_ref.dtype)
    kb = kb_ref[seq].astype(_BF16)
    vb = vb_ref[seq].astype(_BF16)
    ob = []
    for j in range(G_B):
        qs = _stack_heads(q_ref[rows, W_A + LANES * j:W_A + LANES * (j + 1)], lo)
        s = _dot(qs, kb)
        sink = jnp.where(row_lo, sink_ref[0, j], sink_ref[1, j])
        ob.append(_unstack_heads(_softmax_pv([s], [lambda e: _dot_nt(e, vb)], sink), lo))
    b_ref[rows, :] = _rms(jnp.concatenate(ob, axis=1), _window_head_gain(gb_ref)).astype(b_ref.dtype)


def _front_kernel(sink_ref, xp_ref, xs_ref, mod_ref, g_ref, w_ref, cos_ref, sin_ref, ga_ref, gb_ref,
                  wu_ref, wd_ref, *refs, prompt_steps, mod_row):
    kat_ref, vat_ref, kbt_ref, vbt_ref, a_ref, b_ref, wu_out, wd_out = refs[0:8]
    outs_s, w_scr, q_scr = refs[8:13], refs[13], refs[14]
    i = pl.program_id(0)

    @pl.when(i == 0)
    def _():
        _prepare_in_weight(w_ref, w_scr)

    mod = mod_ref[pl.ds(mod_row(i), 1), :]

    @pl.when(i < prompt_steps)
    def _():
        wu_out[...] = wu_ref[...].astype(wu_out.dtype)
        wd_out[...] = wd_ref[...].astype(wd_out.dtype)
        _inproj_rows(xp_ref, mod, g_ref, w_scr, None, (q_scr, kat_ref, vat_ref, kbt_ref, vbt_ref))
        l = kat_ref.shape[2]
        for seq in range(kat_ref.shape[0]):
            _ctx_attention(q_scr, slice(seq * l, (seq + 1) * l), seq, kat_ref, vat_ref, kbt_ref, vbt_ref,
                           sink_ref, ga_ref, gb_ref, a_ref, b_ref)

    @pl.when(i >= prompt_steps)
    def _():
        _inproj_rows(xs_ref, mod, g_ref, w_scr, (cos_ref, sin_ref), outs_s)


def _front_call(sink, xp, xs, mod, g_pre, w_in, rope_tabs, g_a, g_b, w_up, w_down, lp, ts):
    tm = TOKEN_TILE
    tp, tsamp = xp.shape[0], xs.shape[0]
    n_p, n_s = tp // tm, tsamp // tm
    assert n_p * tm == tp and n_s * tm == tsamp and tm % lp == 0 and ts % tm == 0
    ru, rd = w_up.shape[0] // n_p, w_down.shape[0] // n_p
    assert ru * n_p == w_up.shape[0] and rd * n_p == w_down.shape[0]
    assert ru % BF16_SUBLANES == 0 and rd % BF16_SUBLANES == 0
    tiles_per_seq = ts // tm
    p_tile = lambda i: jnp.minimum(i, n_p - 1)
    s_tile = lambda i: jnp.maximum(i - n_p, 0)
    mod_row = lambda i: jnp.where(i < n_p, 0, 1 + s_tile(i) // tiles_per_seq)
    p_row = lambda i: (p_tile(i), 0)
    s_row = lambda i: (s_tile(i), 0)
    const = lambda i: (0, 0)
    nseq = tm // lp
    in_specs = [
        pl.BlockSpec(memory_space=pltpu.SMEM),
        pl.BlockSpec((tm, D_MODEL), p_row),
        pl.BlockSpec((tm, D_MODEL), s_row),
        pl.BlockSpec((MOD_ROWS, 6 * D_MODEL), const),
        pl.BlockSpec((1, D_MODEL), const),
        _resident((D_MODEL, IN_WIDTH)),
        pl.BlockSpec((tm, LANES), lambda i: (s_tile(i) % tiles_per_seq, 0)),
        pl.BlockSpec((tm, LANES), lambda i: (s_tile(i) % tiles_per_seq, 0)),
        pl.BlockSpec((1, W_A), const),
        pl.BlockSpec((1, W_B), const),
        pl.BlockSpec((ru, w_up.shape[1]), p_row),
        pl.BlockSpec((rd, w_down.shape[1]), p_row),
    ]
    kvt_shape = lambda w: jax.ShapeDtypeStruct((tp // lp, w, lp), _F32)
    kvt_spec = lambda w: pl.BlockSpec((nseq, w, lp), lambda i: (p_tile(i), 0, 0))
    kv_shape = lambda w: jax.ShapeDtypeStruct((tsamp, w), _BF16)
    kv_spec = lambda w: pl.BlockSpec((tm, w), s_row)
    out_shape = (kvt_shape(W_A), kvt_shape(W_A), kvt_shape(KV_W_B), kvt_shape(KV_W_B),
                 jax.ShapeDtypeStruct((tp, W_A), _BF16), jax.ShapeDtypeStruct((tp, W_B), _BF16),
                 jax.ShapeDtypeStruct(w_up.shape, _BF16), jax.ShapeDtypeStruct(w_down.shape, _BF16),
                 jax.ShapeDtypeStruct((tsamp, W_A + W_B), _BF16),
                 kv_shape(W_A), kv_shape(W_A), kv_shape(KV_W_B), kv_shape(KV_W_B))
    out_specs = (kvt_spec(W_A), kvt_spec(W_A), kvt_spec(KV_W_B), kvt_spec(KV_W_B),
                 pl.BlockSpec((tm, W_A), p_row), pl.BlockSpec((tm, W_B), p_row),
                 pl.BlockSpec((ru, w_up.shape[1]), p_row), pl.BlockSpec((rd, w_down.shape[1]), p_row),
                 pl.BlockSpec((tm, W_A + W_B), s_row),
                 kv_spec(W_A), kv_spec(W_A), kv_spec(KV_W_B), kv_spec(KV_W_B))
    outs = pl.pallas_call(
        functools.partial(_front_kernel, prompt_steps=n_p, mod_row=mod_row),
        grid=(n_p + n_s,),
        in_specs=in_specs,
        out_specs=out_specs,
        out_shape=out_shape,
        scratch_shapes=[pltpu.VMEM((D_MODEL, IN_WIDTH), _BF16), pltpu.VMEM((tm, W_A + W_B), _BF16)],
        compiler_params=pltpu.CompilerParams(
            dimension_semantics=("arbitrary",), vmem_limit_bytes=BIG_VMEM_LIMIT),
        name="front",
    )(sink, xp, xs, mod, g_pre.reshape(1, D_MODEL), w_in, *rope_tabs, g_a.reshape(1, W_A),
      g_b.reshape(1, W_B), w_up, w_down)
    return outs[:6], outs[6:8], outs[8:]


def _na_plan(rows):
    kr = min(NA_ROWS, rows)
    win = kr + NA_QROWS
    row_start = lambda r: min(max(r - kr // 2, 0), rows - kr)
    blocks, classes = [], []
    for r0 in range(0, rows, NA_QROWS):
        ws = min(row_start(r0), rows - win)
        keys = []
        for r in range(r0, r0 + NA_QROWS):
            key = (ws - r + NA_ROWS - 1, row_start(r) - ws)
            if key not in classes:
                classes.append(key)
            keys.append(classes.index(key))
        blocks.append((r0, ws, keys))
    return kr, win, blocks, classes


def _na_kernel(q_ref, k_ref, v_ref, ck_ref, cv_ref, rpb_ref, ga_ref, a_ref, bias_ref, o_scr):
    rows = q_ref.shape[0] // GRID_W
    kr, win, blocks, classes = _na_plan(rows)
    m = NA_QROWS * GRID_W
    lane = lax.broadcasted_iota(jnp.int32, (GRID_W, LANES), 1)
    lo = lane < HEAD_DIM
    lo_q = lax.broadcasted_iota(jnp.int32, (m, LANES), 1) < HEAD_DIM
    cq = lax.broadcasted_iota(jnp.int32, (GRID_W, LANES), 0)
    ck = lane & (GRID_W - 1)
    cs = jnp.clip(cq - NA_COLS // 2, 0, GRID_W - NA_COLS)
    valid = (ck >= cs) & (ck < cs + NA_COLS)
    neg = jnp.full((GRID_W, LANES), NEG_INF, _F32)
    n_drow = 2 * NA_ROWS - 1

    for p in range(W_A // LANES):
        sl = slice(LANES * p, LANES * (p + 1))
        for hh in range(2):
            t_lo, t_hi = [], []
            for d in range(n_drow):
                base = jnp.broadcast_to(rpb_ref[2 * p + hh, d:d + 1, :], (GRID_W, LANES))
                t_lo.append(pltpu.roll(base, LANES - (NA_COLS - 1), 1, stride=1, stride_axis=0))
                t_hi.append(pltpu.roll(base, GRID_W - (NA_COLS - 1), 1, stride=1, stride_axis=0))
            tiles = {}
            for ci, (d_first, off) in enumerate(classes):
                for j in range(win // 2):
                    d_of = lambda i: d_first + i if off <= i < off + kr else None
                    key = (d_of(2 * j), d_of(2 * j + 1))
                    if key not in tiles:
                        left = neg if key[0] is None else t_lo[key[0]]
                        right = neg if key[1] is None else t_hi[key[1]]
                        tiles[key] = jnp.where(valid, jnp.where(lo, left, right), NEG_INF)
                    bias_ref[ci, GRID_W * hh:GRID_W * (hh + 1), LANES * j:LANES * (j + 1)] = tiles[key]

        ckt = ck_ref[0, sl, :].astype(_BF16)
        cvt = cv_ref[0, sl, :].astype(_BF16)

        for r0, ws, keys in blocks:
            q0, k0 = r0 * GRID_W, ws * GRID_W
            qs = _stack_heads(q_ref[q0:q0 + m, sl], lo_q)
            kw = k_ref[k0:k0 + win * GRID_W, sl]
            vw = v_ref[k0:k0 + win * GRID_W, sl]
            bias = jnp.concatenate(
                [bias_ref[ci, GRID_W * hh:GRID_W * (hh + 1), :] for hh in range(2) for ci in keys], axis=0)
            s_nb = _dot_nt(qs, kw) + bias
            s_cx = _dot(qs, ckt)
            o2 = _softmax_pv([s_nb, s_cx], [lambda e, vw=vw: _dot(e, vw), lambda e: _dot_nt(e, cvt)])
            o_scr[q0:q0 + m, sl] = _unstack_heads(o2, lo_q)

    a_ref[...] = _rms(o_scr[...], ga_ref[...]).astype(a_ref.dtype)


def _na_call(q, k, v, ckt, cvt, rpb_pad, g_a, nb, seq):
    past = ckt.shape[2]
    _, win, _, classes = _na_plan(seq // GRID_W)
    blk = lambda b: (b, 0)
    return pl.pallas_call(
        _na_kernel,
        grid=(nb,),
        in_specs=[
            pl.BlockSpec((seq, W_A), blk),
            pl.BlockSpec((seq, W_A), blk),
            pl.BlockSpec((seq, W_A), blk),
            pl.BlockSpec((1, W_A, past), lambda b: (b, 0, 0)),
            pl.BlockSpec((1, W_A, past), lambda b: (b, 0, 0)),
            pl.BlockSpec((H_A, 2 * NA_ROWS - 1, LANES), lambda b: (0, 0, 0)),
            pl.BlockSpec((1, W_A), lambda b: (0, 0)),
        ],
        out_specs=pl.BlockSpec((seq, W_A), blk),
        out_shape=jax.ShapeDtypeStruct((nb * seq, W_A), _BF16),
        scratch_shapes=[pltpu.VMEM((len(classes), 2 * GRID_W, win * GRID_W), _F32),
                        pltpu.VMEM((seq, W_A), _F32)],
        compiler_params=pltpu.CompilerParams(
            dimension_semantics=("parallel",), vmem_limit_bytes=VMEM_LIMIT),
        name="na_attn",
    )(q, k, v, ckt, cvt, rpb_pad, g_a.reshape(1, W_A))


def _win_kernel(sink_ref, q_ref, k_ref, v_ref, ck_ref, cv_ref, gb_ref, b_ref, o_scr):
    t = q_ref.shape[0]
    qb = WIN_QBLOCKS * BLOCK
    band = qb + 2 * BLOCK
    lo = lax.broadcasted_iota(jnp.int32, (qb, LANES), 1) < HEAD_DIM
    row_lo = lax.broadcasted_iota(jnp.int32, (2 * qb, 1), 0) < qb
    qi = lax.broadcasted_iota(jnp.int32, (2 * qb, band), 0) & (qb - 1)
    kj = lax.broadcasted_iota(jnp.int32, (2 * qb, band), 1)
    rel = qi - kj
    ckt = ck_ref[0].astype(_BF16)
    cvt = cv_ref[0].astype(_BF16)

    starts = {q0: min(max(q0 - BLOCK, 0), t - band) for q0 in range(0, t, qb)}
    masks = {off: jnp.where(jnp.abs(rel + off) <= SWA_WINDOW, 0.0, NEG_INF)
             for off in sorted({q0 - k0 for q0, k0 in starts.items()})}

    for j in range(G_B):
        sl = slice(LANES * j, LANES * (j + 1))
        sink = jnp.where(row_lo, sink_ref[0, j], sink_ref[1, j])
        for q0, k0 in starts.items():
            qs = _stack_heads(q_ref[q0:q0 + qb, sl], lo)
            s = _dot_nt(qs, k_ref[k0:k0 + band, :]) + masks[q0 - k0]
            s_cx = _dot(qs, ckt)
            vband = v_ref[k0:k0 + band, :]
            o2 = _softmax_pv([s, s_cx], [lambda e, vband=vband: _dot(e, vband), lambda e: _dot_nt(e, cvt)],
                             sink)
            o_scr[q0:q0 + qb, sl] = _unstack_heads(o2, lo)

    b_ref[...] = _rms(o_scr[...], _window_head_gain(gb_ref)).astype(b_ref.dtype)


def _win_call(sink, q, k, v, ckt, cvt, g_b, nb, seq):
    past = ckt.shape[2]
    return pl.pallas_call(
        _win_kernel,
        grid=(nb,),
        in_specs=[
            pl.BlockSpec(memory_space=pltpu.SMEM),
            pl.BlockSpec((seq, W_B), lambda b: (b, W_A // W_B)),
            pl.BlockSpec((seq, KV_W_B), lambda b: (b, 0)),
            pl.BlockSpec((seq, KV_W_B), lambda b: (b, 0)),
            pl.BlockSpec((1, KV_W_B, past), lambda b: (b, 0, 0)),
            pl.BlockSpec((1, KV_W_B, past), lambda b: (b, 0, 0)),
            pl.BlockSpec((1, W_B), lambda b: (0, 0)),
        ],
        out_specs=pl.BlockSpec((seq, W_B), lambda b: (b, 0)),
        out_shape=jax.ShapeDtypeStruct((nb * seq, W_B), _BF16),
        scratch_shapes=[pltpu.VMEM((seq, W_B), _F32)],
        compiler_params=pltpu.CompilerParams(
            dimension_semantics=("parallel",), vmem_limit_bytes=VMEM_LIMIT),
        name="win_attn",
    )(sink, q, k, v, ckt, cvt, g_b.reshape(1, W_B))


def _prepare_out_weight(w_ref, w_scr):
    for r0 in range(0, W_A, CAST_COLS):
        w_scr[r0:r0 + CAST_COLS, :] = w_ref[r0:r0 + CAST_COLS, :].astype(_BF16)
    for i, h in enumerate(_QB_HEAD_ORDER):
        w_scr[W_A + HEAD_DIM * i:W_A + HEAD_DIM * (i + 1), :] = (
            w_ref[W_A + HEAD_DIM * h:W_A + HEAD_DIM * (h + 1), :].astype(_BF16))


def _zero_row(x, row):
    r0 = row - row % SUBLANES
    sub = lax.broadcasted_iota(jnp.int32, (SUBLANES, 1), 0)
    slab = jnp.where(sub == row % SUBLANES, 0.0, x[r0:r0 + SUBLANES])
    parts = ([x[:r0]] if r0 else []) + [slab] + ([x[r0 + SUBLANES:]] if r0 + SUBLANES < x.shape[0] else [])
    return jnp.concatenate(parts, axis=0)


def _conv3(u, cw, cb, seq_len):
    tm = u.shape[0]
    prev = pltpu.roll(u, 1, 0)
    nxt = pltpu.roll(u, tm - 1, 0)
    for s0 in range(0, tm, seq_len):
        prev = _zero_row(prev, s0)
        nxt = _zero_row(nxt, s0 + seq_len - 1)
    return prev * cw[0:1, :] + u * cw[1:2, :] + nxt * cw[2:3, :] + cb


def _gated_silu(gate, val):
    half = 0.5 * gate
    return ((half + half * jnp.tanh(half)) * val).astype(_BF16)


def _up(h2_ref, w):
    tm = h2_ref.shape[0]
    return jnp.concatenate(
        [_dot(h2_ref[r:r + FFN_ROW_BLOCK, :], w) for r in range(0, tm, FFN_ROW_BLOCK)], axis=0)


def _out_proj(a_ref, b_ref, x_ref, mod, gpost_ref, gffn_ref, wo_scr, x1_ref, h2_ref):
    gt1 = mod[:, 2 * D_MODEL:3 * D_MODEL]
    sh2 = mod[:, 3 * D_MODEL:4 * D_MODEL]
    sc2 = mod[:, 4 * D_MODEL:5 * D_MODEL]
    for r in range(0, x_ref.shape[0], FFN_ROW_BLOCK):
        rows = slice(r, r + FFN_ROW_BLOCK)
        y = _dot(a_ref[rows, :], wo_scr[0:W_A, :]) + _dot(b_ref[rows, :], wo_scr[W_A:W_A + W_B, :])
        x1 = x_ref[rows, :] + gt1 * _rms(y, gpost_ref[...])
        x1_ref[rows, :] = x1
        h2_ref[rows, :] = (_rms(x1, gffn_ref[...]) * (1 + sc2) + sh2).astype(h2_ref.dtype)


def _down_proj(act_ref, wd_ref, mod, gpost2_ref, x1_ref, out_ref):
    gt2 = mod[:, 5 * D_MODEL:6 * D_MODEL]
    for r in range(0, act_ref.shape[0], DOWN_ROW_BLOCK):
        rows = slice(r, r + DOWN_ROW_BLOCK)
        y = _dot(act_ref[rows, :], wd_ref[...])
        out_ref[rows, :] = x1_ref[rows, :] + gt2 * _rms(y, gpost2_ref[...])


def _tail_seq_kernel(a_ref, b_ref, x_ref, mod_ref, gpost_ref, gffn_ref, gpost2_ref, wo_ref, wu_ref,
                     cw_ref, cb_ref, wd_ref, out_ref, wo_scr, h2_scr, act_scr, *, seq_len, mod_row):
    mod = mod_ref[pl.ds(mod_row(pl.program_id(0)), 1), :]

    @pl.when(pl.program_id(0) == 0)
    def _():
        _prepare_out_weight(wo_ref, wo_scr)

    _out_proj(a_ref, b_ref, x_ref, mod, gpost_ref, gffn_ref, wo_scr, out_ref, h2_scr)
    tc = FF_CHUNK

    nl = cb_ref.shape[0]

    def channels(ref, row0, col0):
        r = row0 + col0 // LANES
        return jnp.concatenate([ref[r + j:r + j + 1, :] for j in range(tc // LANES)], axis=1)

    def chunk(col0):
        cw = jnp.concatenate([channels(cw_ref, k * nl, col0) for k in range(3)], axis=0)
        return _conv3(_up(h2_scr, wu_ref[:, col0:col0 + tc]), cw, channels(cb_ref, 0, col0), seq_len)

    for c0 in range(0, D_FF, tc):
        act_scr[:, c0:c0 + tc] = _gated_silu(chunk(c0), chunk(D_FF + c0))
    _down_proj(act_scr, wd_ref, mod, gpost2_ref, out_ref, out_ref)


def _tail_seq_call(a, b, x2d, mod, mod_row, g_post, g_ffn, g_post2, w_out, w_up, conv_w, conv_b, w_down,
                   seq_len, name):
    t = x2d.shape[0]
    tm = max(SEQ_TAIL_TILE, seq_len)
    nl = 2 * D_FF // LANES
    big = tm > SEQ_TAIL_TILE
    vmem_limit = BIG_VMEM_LIMIT if big else VMEM_LIMIT
    x_mode = dict(pipeline_mode=pl.Buffered(1)) if big else {}
    row = lambda i: (i, 0)
    const = lambda i: (0, 0)
    return pl.pallas_call(
        functools.partial(_tail_seq_kernel, seq_len=seq_len, mod_row=mod_row),
        grid=(t // tm,),
        in_specs=[
            pl.BlockSpec((tm, W_A), row),
            pl.BlockSpec((tm, W_B), row),
            pl.BlockSpec((tm, D_MODEL), row, **x_mode),
            pl.BlockSpec((MOD_ROWS, 6 * D_MODEL), const),
            pl.BlockSpec((1, D_MODEL), const),
            pl.BlockSpec((1, D_MODEL), const),
            pl.BlockSpec((1, D_MODEL), const),
            _resident((W_A + W_B, D_MODEL)),
            _resident((D_MODEL, 2 * D_FF)),
            _resident((3 * nl, LANES)),
            _resident((nl, LANES)),
            _resident((D_FF, D_MODEL)),
        ],
        out_specs=pl.BlockSpec((tm, D_MODEL), row),
        out_shape=jax.ShapeDtypeStruct((t, D_MODEL), _F32),
        scratch_shapes=[
            pltpu.VMEM((W_A + W_B, D_MODEL), _BF16),
            pltpu.VMEM((tm, D_MODEL), _BF16),
            pltpu.VMEM((tm, D_FF), _BF16),
        ],
        compiler_params=pltpu.CompilerParams(
            dimension_semantics=("arbitrary",), vmem_limit_bytes=vmem_limit),
        name=name,
    )(a, b, x2d, mod, g_post.reshape(1, D_MODEL), g_ffn.reshape(1, D_MODEL), g_post2.reshape(1, D_MODEL),
      w_out, w_up, conv_w.reshape(3 * nl, LANES), conv_b.reshape(nl, LANES), w_down)


def _rope_tables(seq):
    n = ROPE_HALF
    t = np.arange(seq)
    lane = np.arange(HEAD_DIM)
    pos = np.where(lane[None, :] < HEAD_DIM // 2, (t // GRID_W)[:, None], (t % GRID_W)[:, None])
    inv = 1.0 / (ROPE_THETA ** (np.arange(n, dtype=np.float64) / n))
    ang = pos.astype(np.float64) * inv[lane % n][None, :]
    sign = np.where((lane & n) == 0, -1.0, 1.0)[None, :]
    cos = np.tile(np.cos(ang), (1, LANES // HEAD_DIM)).astype(np.float32)
    sin = np.tile(np.sin(ang) * sign, (1, LANES // HEAD_DIM)).astype(np.float32)
    return jnp.asarray(cos), jnp.asarray(sin)


def _to_head_dim_token(cache_l):
    b, l, h, d = cache_l.shape
    return jnp.transpose(cache_l, (0, 2, 3, 1)).reshape(b, h * d, l)


def _from_head_dim_token(kv_t, heads):
    b, _, l = kv_t.shape
    return jnp.transpose(kv_t.reshape(b, 1, heads, HEAD_DIM, l), (0, 1, 4, 2, 3))


def kernel(x_prompt, x_sample, cache_a_k, cache_a_v, cache_b_k, cache_b_v, c, c_ctx, w_mod, b_mod,
           g_mix_pre, g_mix_post, g_ffn_pre, g_ffn_post, w_in, rpb_a, sink_b, g_grp_a, g_grp_b,
           w_out, w_up, conv_w, conv_b, w_down):
    bp, lp, _ = x_prompt.shape
    bs, ts, _ = x_sample.shape
    assert w_in.shape[0] == 1
    xp = x_prompt.reshape(bp * lp, D_MODEL)
    xs = x_sample.reshape(bs * ts, D_MODEL)
    assert 1 + bs <= MOD_ROWS
    cond = jnp.concatenate([c_ctx[None], c, jnp.zeros((MOD_ROWS - 1 - bs, D_MODEL), _F32)], axis=0)
    cos_tab, sin_tab = _rope_tables(ts)

    l = 0
    mod = _mod_call(cond, w_mod[l], b_mod[l])
    rpb_pad = jnp.pad(rpb_a[l], ((0, 0), (0, 0), (0, LANES - (2 * NA_COLS - 1))))

    prompt_row = lambda i: 0
    assert ts >= SEQ_TAIL_TILE
    sample_row = lambda i: 1 + i

    (kat_p, vat_p, kbt_p, vbt_p, a_p, b_p), (w_up_b, w_down_b), (q_s, ka_s, va_s, kb_s, vb_s) = _front_call(
        sink_b[l], xp, xs, mod, g_mix_pre[l], w_in[l], (cos_tab, sin_tab), g_grp_a[l], g_grp_b[l],
        w_up[l], w_down[l], lp, ts)

    y_p = _tail_seq_call(a_p, b_p, xp, mod, prompt_row, g_mix_post[l], g_ffn_pre[l], g_ffn_post[l], w_out[l],
                         w_up_b, conv_w[l], conv_b[l], w_down_b, lp, "tail_prompt")

    a_s = _na_call(q_s, ka_s, va_s, _to_head_dim_token(cache_a_k[:, l]), _to_head_dim_token(cache_a_v[:, l]),
                   rpb_pad, g_grp_a[l], bs, ts)
    b_s = _win_call(sink_b[l], q_s, kb_s, vb_s, _to_head_dim_token(cache_b_k[:, l]),
                    _to_head_dim_token(cache_b_v[:, l]), g_grp_b[l], bs, ts)
    y_s = _tail_seq_call(a_s, b_s, xs, mod, sample_row, g_mix_post[l], g_ffn_pre[l], g_ffn_post[l], w_out[l],
                         w_up_b, conv_w[l], conv_b[l], w_down_b, ts, "tail_sample")

    return (y_p.reshape(bp, lp, D_MODEL), y_s.reshape(bs, ts, D_MODEL),
            _from_head_dim_token(kat_p, H_A), _from_head_dim_token(vat_p, H_A),
            _from_head_dim_token(kbt_p, KV_B), _from_head_dim_token(vbt_p, KV_B))
```

```python
import functools

import numpy as np
import jax
import jax.numpy as jnp
from jax import lax
from jax.experimental import pallas as pl
from jax.experimental.pallas import tpu as pltpu

D_MODEL = 1024
HEAD_DIM = 64
H_A = 8
H_B = 8
KV_B = 2
G_B = H_B // KV_B
W_A = H_A * HEAD_DIM
W_B = H_B * HEAD_DIM
KV_W_B = KV_B * HEAD_DIM
IN_WIDTH = 3 * W_A + W_B + 2 * KV_W_B
GRID_W = 64
NA_ROWS = 8
NA_COLS = 16
SWA_WINDOW = 128
BLOCK = 128
D_FF = 2816
ROPE_THETA = 10000.0
EPS = 1e-6
NEG_INF = -1e30
Q_SCALE = HEAD_DIM ** -0.5
ROPE_HALF = HEAD_DIM // 4

LANES = 128
SUBLANES = 8
BF16_SUBLANES = 16
FF_CHUNK = 512
TOKEN_TILE = 512
PROJ_ROW_BLOCK = 256
SEQ_TAIL_TILE = 512
FFN_ROW_BLOCK = 256
DOWN_ROW_BLOCK = 256
MOD_ROWS = SUBLANES
MOD_STREAMS = 2
MOD_ROW_BLOCK = 128
NA_QROWS = 4
WIN_QBLOCKS = 2
CAST_COLS = 512
VMEM_LIMIT = 48 * 1024 * 1024
BIG_VMEM_LIMIT = 56 * 1024 * 1024

C_QA, C_KA, C_VA, C_QB, C_KB, C_VB = 0, W_A, 2 * W_A, 3 * W_A, 3 * W_A + W_B, 3 * W_A + W_B + KV_W_B

_QB_HEAD_ORDER = tuple(kv * G_B + g for g in range(G_B) for kv in range(KV_B))

_BF16 = jnp.bfloat16
_F32 = jnp.float32


def _dot(a, b):
    return jnp.dot(a, b, preferred_element_type=_F32)


def _dot_nt(a, b):
    return lax.dot_general(a, b, (((1,), (1,)), ((), ())), preferred_element_type=_F32)


def _rms(x, g):
    var = jnp.mean(x * x, axis=-1, keepdims=True)
    return x * lax.rsqrt(var + EPS) * g


def _stack_heads(q2, lo):
    zero = jnp.zeros_like(q2)
    return jnp.concatenate([jnp.where(lo, q2, zero), jnp.where(lo, zero, q2)], axis=0)


def _unstack_heads(o2, lo):
    m = o2.shape[0] // 2
    return jnp.where(lo, o2[:m], o2[m:])


def _softmax_pv(scores, pv, sink=None):
    m = scores[0].max(axis=-1, keepdims=True)
    for s in scores[1:]:
        m = jnp.maximum(m, s.max(axis=-1, keepdims=True))
    if sink is not None:
        m = jnp.maximum(m, sink)
    denom = None
    out = None
    for s, f in zip(scores, pv):
        e = jnp.exp(s - m)
        l = e.sum(axis=-1, keepdims=True)
        o = f(e.astype(_BF16))
        denom = l if denom is None else denom + l
        out = o if out is None else out + o
    if sink is not None:
        denom = denom + jnp.exp(sink - m)
    return out / denom


def _resident(shape):
    return pl.BlockSpec(shape, lambda *_: (0,) * len(shape), pipeline_mode=pl.Buffered(1))


def _mod_kernel(c_ref, *refs):
    w_refs, b_ref, o_ref = refs[:-2], refs[-2], refs[-1]
    i = pl.program_id(0)

    @pl.when(i == 0)
    def _():
        o_ref[...] = jnp.broadcast_to(b_ref[...], o_ref.shape)

    acc = o_ref[...]
    for j, w_ref in enumerate(w_refs):
        c = c_ref[i * len(w_refs) + j]
        s = (c * jax.nn.sigmoid(c)).astype(_BF16)
        acc = acc + _dot(s, w_ref[...].astype(_BF16))
    o_ref[...] = acc


def _mod_call(cond, w_mod, b_mod):
    n = w_mod.shape[1]
    kb = MOD_ROW_BLOCK
    nblk = D_MODEL // kb
    cond_blocks = cond.reshape(MOD_ROWS, nblk, kb).transpose(1, 0, 2)
    return pl.pallas_call(
        _mod_kernel,
        grid=(nblk // MOD_STREAMS,),
        in_specs=[pl.BlockSpec((nblk, MOD_ROWS, kb), lambda i: (0, 0, 0))]
        + [pl.BlockSpec((kb, n), lambda i, j=j: (i * MOD_STREAMS + j, 0)) for j in range(MOD_STREAMS)]
        + [pl.BlockSpec((1, n), lambda i: (0, 0))],
        out_specs=pl.BlockSpec((MOD_ROWS, n), lambda i: (0, 0)),
        out_shape=jax.ShapeDtypeStruct((MOD_ROWS, n), _F32),
        compiler_params=pltpu.CompilerParams(dimension_semantics=("arbitrary",), vmem_limit_bytes=VMEM_LIMIT),
        name="mod",
    )(cond_blocks, *([w_mod] * MOD_STREAMS), b_mod.reshape(1, n))


def _pair_window_heads(block_of, j):
    def head_at(h, want_hi):
        blk = block_of(h // 2)
        return blk if (h % 2 == 1) == want_hi else pltpu.roll(blk, HEAD_DIM, 1)

    left, right = head_at(_QB_HEAD_ORDER[2 * j], False), head_at(_QB_HEAD_ORDER[2 * j + 1], True)
    return jnp.where(lax.broadcasted_iota(jnp.int32, left.shape, 1) < HEAD_DIM, left, right)


def _window_head_gain(gb_ref):
    block_of = lambda k: jnp.broadcast_to(gb_ref[:, LANES * k:LANES * (k + 1)], (SUBLANES, LANES))
    return jnp.concatenate([_pair_window_heads(block_of, j)[0:1] for j in range(G_B)], axis=1)


def _prepare_in_weight(w_ref, w_scr):
    for c0 in list(range(0, C_QB, CAST_COLS)) + [C_KB]:
        n = min(CAST_COLS, IN_WIDTH - c0)
        w_scr[:, c0:c0 + n] = w_ref[:, c0:c0 + n].astype(_BF16)
    block_of = lambda k: w_ref[:, C_QB + LANES * k:C_QB + LANES * (k + 1)]
    for j in range(G_B):
        w_scr[:, C_QB + LANES * j:C_QB + LANES * (j + 1)] = _pair_window_heads(block_of, j).astype(_BF16)


def _rope(z, cos, sin):
    lane = lax.broadcasted_iota(jnp.int32, z.shape, 1)
    partner = jnp.where((lane & ROPE_HALF) == 0,
                        pltpu.roll(z, LANES - ROPE_HALF, 1), pltpu.roll(z, ROPE_HALF, 1))
    return z * cos + partner * sin


def _inproj_rows(x_ref, mod, g_ref, w_scr, rope_refs, outs):
    q_ref, ka_ref, va_ref, kb_ref, vb_ref = outs
    kv_transposed = rope_refs is None
    sh1 = mod[:, 0:D_MODEL]
    sc1 = mod[:, D_MODEL:2 * D_MODEL]
    rb = PROJ_ROW_BLOCK

    for r in range(0, x_ref.shape[0], rb):
        rows = slice(r, r + rb)
        h = (_rms(x_ref[rows, :], g_ref[...]) * (1 + sc1) + sh1).astype(_BF16)

        def proj(c0, n):
            return _dot(h, w_scr[:, c0:c0 + n])

        def put_kv(ref, z):
            if not kv_transposed:
                ref[rows, :] = z.astype(ref.dtype)
                return
            l = ref.shape[2]
            for t0 in range(0, rb, l):
                for c0 in range(0, z.shape[1], LANES):
                    ref[(r + t0) // l, c0:c0 + LANES, :] = z[t0:t0 + l, c0:c0 + LANES].T.astype(ref.dtype)

        q_ref[rows, 0:W_A] = (proj(C_QA, W_A) * Q_SCALE).astype(q_ref.dtype)
        put_kv(ka_ref, proj(C_KA, W_A))
        put_kv(va_ref, proj(C_VA, W_A))
        zkv = proj(C_KB, 2 * KV_W_B)
        zkb, zvb = zkv[:, :KV_W_B], zkv[:, KV_W_B:]
        zqb = proj(C_QB, W_B)
        put_kv(vb_ref, zvb)
        if rope_refs is not None:
            cos = rope_refs[0][rows, :]
            sin = rope_refs[1][rows, :]
            for j in range(W_B // LANES):
                q_ref[rows, W_A + LANES * j:W_A + LANES * (j + 1)] = (
                    _rope(zqb[:, LANES * j:LANES * (j + 1)], cos, sin) * Q_SCALE).astype(q_ref.dtype)
            put_kv(kb_ref, _rope(zkb, cos, sin))
        else:
            q_ref[rows, W_A:W_A + W_B] = (zqb * Q_SCALE).astype(q_ref.dtype)
            put_kv(kb_ref, zkb)


def _ctx_attention(q_ref, rows, seq, ka_ref, va_ref, kb_ref, vb_ref, sink_ref, ga_ref, gb_ref, a_ref, b_ref):
    l = rows.stop - rows.start
    lo = lax.broadcasted_iota(jnp.int32, (l, LANES), 1) < HEAD_DIM
    row_lo = lax.broadcasted_iota(jnp.int32, (2 * l, 1), 0) < l
    oa = []
    for p in range(W_A // LANES):
        sl = slice(LANES * p, LANES * (p + 1))
        qs = _stack_heads(q_ref[rows, sl], lo)
        s = _dot(qs, ka_ref[seq, sl, :].astype(_BF16))
        vt = va_ref[seq, sl, :].astype(_BF16)
        oa.append(_unstack_heads(_softmax_pv([s], [lambda e, vt=vt: _dot_nt(e, vt)]), lo))
    a_ref[rows, :] = _rms(jnp.concatenate(oa, axis=1), ga_ref[...]).astype(a_ref.dtype)
    kb = kb_ref[seq].astype(_BF16)
    vb = vb_ref[seq].astype(_BF16)
    ob = []
    for j in range(G_B):
        qs = _stack_heads(q_ref[rows, W_A + LANES * j:W_A + LANES * (j + 1)], lo)
        s = _dot(qs, kb)
        sink = jnp.where(row_lo, sink_ref[0, j], sink_ref[1, j])
        ob.append(_unstack_heads(_softmax_pv([s], [lambda e: _dot_nt(e, vb)], sink), lo))
    b_ref[rows, :] = _rms(jnp.concatenate(ob, axis=1), _window_head_gain(gb_ref)).astype(b_ref.dtype)


def _front_kernel(sink_ref, xp_ref, xs_ref, mod_ref, g_ref, w_ref, cos_ref, sin_ref, ga_ref, gb_ref,
                  wu_ref, wd_ref, *refs, prompt_steps, mod_row):
    kat_ref, vat_ref, kbt_ref, vbt_ref, a_ref, b_ref, wu_out, wd_out = refs[0:8]
    outs_s, w_scr, q_scr = refs[8:13], refs[13], refs[14]
    i = pl.program_id(0)

    @pl.when(i == 0)
    def _():
        _prepare_in_weight(w_ref, w_scr)

    mod = mod_ref[pl.ds(mod_row(i), 1), :]

    @pl.when(i < prompt_steps)
    def _():
        wu_out[...] = wu_ref[...].astype(wu_out.dtype)
        wd_out[...] = wd_ref[...].astype(wd_out.dtype)
        _inproj_rows(xp_ref, mod, g_ref, w_scr, None, (q_scr, kat_ref, vat_ref, kbt_ref, vbt_ref))
        l = kat_ref.shape[2]
        for seq in range(kat_ref.shape[0]):
            _ctx_attention(q_scr, slice(seq * l, (seq + 1) * l), seq, kat_ref, vat_ref, kbt_ref, vbt_ref,
                           sink_ref, ga_ref, gb_ref, a_ref, b_ref)

    @pl.when(i >= prompt_steps)
    def _():
        _inproj_rows(xs_ref, mod, g_ref, w_scr, (cos_ref, sin_ref), outs_s)


def _front_call(sink, xp, xs, mod, g_pre, w_in, rope_tabs, g_a, g_b, w_up, w_down, lp, ts):
    tm = TOKEN_TILE
    tp, tsamp = xp.shape[0], xs.shape[0]
    n_p, n_s = tp // tm, tsamp // tm
    assert n_p * tm == tp and n_s * tm == tsamp and tm % lp == 0 and ts % tm == 0
    ru, rd = w_up.shape[0] // n_p, w_down.shape[0] // n_p
    assert ru * n_p == w_up.shape[0] and rd * n_p == w_down.shape[0]
    assert ru % BF16_SUBLANES == 0 and rd % BF16_SUBLANES == 0
    tiles_per_seq = ts // tm
    p_tile = lambda i: jnp.minimum(i, n_p - 1)
    s_tile = lambda i: jnp.maximum(i - n_p, 0)
    mod_row = lambda i: jnp.where(i < n_p, 0, 1 + s_tile(i) // tiles_per_seq)
    p_row = lambda i: (p_tile(i), 0)
    s_row = lambda i: (s_tile(i), 0)
    const = lambda i: (0, 0)
    nseq = tm // lp
    in_specs = [
        pl.BlockSpec(memory_space=pltpu.SMEM),
        pl.BlockSpec((tm, D_MODEL), p_row),
        pl.BlockSpec((tm, D_MODEL), s_row),
        pl.BlockSpec((MOD_ROWS, 6 * D_MODEL), const),
        pl.BlockSpec((1, D_MODEL), const),
        _resident((D_MODEL, IN_WIDTH)),
        pl.BlockSpec((tm, LANES), lambda i: (s_tile(i) % tiles_per_seq, 0)),
        pl.BlockSpec((tm, LANES), lambda i: (s_tile(i) % tiles_per_seq, 0)),
        pl.BlockSpec((1, W_A), const),
        pl.BlockSpec((1, W_B), const),
        pl.BlockSpec((ru, w_up.shape[1]), p_row),
        pl.BlockSpec((rd, w_down.shape[1]), p_row),
    ]
    kvt_shape = lambda w: jax.ShapeDtypeStruct((tp // lp, w, lp), _F32)
    kvt_spec = lambda w: pl.BlockSpec((nseq, w, lp), lambda i: (p_tile(i), 0, 0))
    kv_shape = lambda w: jax.ShapeDtypeStruct((tsamp, w), _BF16)
    kv_spec = lambda w: pl.BlockSpec((tm, w), s_row)
    out_shape = (kvt_shape(W_A), kvt_shape(W_A), kvt_shape(KV_W_B), kvt_shape(KV_W_B),
                 jax.ShapeDtypeStruct((tp, W_A), _BF16), jax.ShapeDtypeStruct((tp, W_B), _BF16),
                 jax.ShapeDtypeStruct(w_up.shape, _BF16), jax.ShapeDtypeStruct(w_down.shape, _BF16),
                 jax.ShapeDtypeStruct((tsamp, W_A + W_B), _BF16),
                 kv_shape(W_A), kv_shape(W_A), kv_shape(KV_W_B), kv_shape(KV_W_B))
    out_specs = (kvt_spec(W_A), kvt_spec(W_A), kvt_spec(KV_W_B), kvt_spec(KV_W_B),
                 pl.BlockSpec((tm, W_A), p_row), pl.BlockSpec((tm, W_B), p_row),
                 pl.BlockSpec((ru, w_up.shape[1]), p_row), pl.BlockSpec((rd, w_down.shape[1]), p_row),
                 pl.BlockSpec((tm, W_A + W_B), s_row),
                 kv_spec(W_A), kv_spec(W_A), kv_spec(KV_W_B), kv_spec(KV_W_B))
    outs = pl.pallas_call(
        functools.partial(_front_kernel, prompt_steps=n_p, mod_row=mod_row),
        grid=(n_p + n_s,),
        in_specs=in_specs,
        out_specs=out_specs,
        out_shape=out_shape,
        scratch_shapes=[pltpu.VMEM((D_MODEL, IN_WIDTH), _BF16), pltpu.VMEM((tm, W_A + W_B), _BF16)],
        compiler_params=pltpu.CompilerParams(
            dimension_semantics=("arbitrary",), vmem_limit_bytes=BIG_VMEM_LIMIT),
        name="front",
    )(sink, xp, xs, mod, g_pre.reshape(1, D_MODEL), w_in, *rope_tabs, g_a.reshape(1, W_A),
      g_b.reshape(1, W_B), w_up, w_down)
    return outs[:6], outs[6:8], outs[8:]


def _na_plan(rows):
    kr = min(NA_ROWS, rows)
    win = kr + NA_QROWS
    row_start = lambda r: min(max(r - kr // 2, 0), rows - kr)
    blocks, classes = [], []
    for r0 in range(0, rows, NA_QROWS):
        ws = min(row_start(r0), rows - win)
        keys = []
        for r in range(r0, r0 + NA_QROWS):
            key = (ws - r + NA_ROWS - 1, row_start(r) - ws)
            if key not in classes:
                classes.append(key)
            keys.append(classes.index(key))
        blocks.append((r0, ws, keys))
    return kr, win, blocks, classes


def _na_kernel(q_ref, k_ref, v_ref, ck_ref, cv_ref, rpb_ref, ga_ref, a_ref, bias_ref, o_scr):
    rows = q_ref.shape[0] // GRID_W
    kr, win, blocks, classes = _na_plan(rows)
    m = NA_QROWS * GRID_W
    lane = lax.broadcasted_iota(jnp.int32, (GRID_W, LANES), 1)
    lo = lane < HEAD_DIM
    lo_q = lax.broadcasted_iota(jnp.int32, (m, LANES), 1) < HEAD_DIM
    cq = lax.broadcasted_iota(jnp.int32, (GRID_W, LANES), 0)
    ck = lane & (GRID_W - 1)
    cs = jnp.clip(cq - NA_COLS // 2, 0, GRID_W - NA_COLS)
    valid = (ck >= cs) & (ck < cs + NA_COLS)
    neg = jnp.full((GRID_W, LANES), NEG_INF, _F32)
    n_drow = 2 * NA_ROWS - 1

    for p in range(W_A // LANES):
        sl = slice(LANES * p, LANES * (p + 1))
        for hh in range(2):
            t_lo, t_hi = [], []
            for d in range(n_drow):
                base = jnp.broadcast_to(rpb_ref[2 * p + hh, d:d + 1, :], (GRID_W, LANES))
                t_lo.append(pltpu.roll(base, LANES - (NA_COLS - 1), 1, stride=1, stride_axis=0))
                t_hi.append(pltpu.roll(base, GRID_W - (NA_COLS - 1), 1, stride=1, stride_axis=0))
            tiles = {}
            for ci, (d_first, off) in enumerate(classes):
                for j in range(win // 2):
                    d_of = lambda i: d_first + i if off <= i < off + kr else None
                    key = (d_of(2 * j), d_of(2 * j + 1))
                    if key not in tiles:
                        left = neg if key[0] is None else t_lo[key[0]]
                        right = neg if key[1] is None else t_hi[key[1]]
                        tiles[key] = jnp.where(valid, jnp.where(lo, left, right), NEG_INF)
                    bias_ref[ci, GRID_W * hh:GRID_W * (hh + 1), LANES * j:LANES * (j + 1)] = tiles[key]

        ckt = ck_ref[0, sl, :].astype(_BF16)
        cvt = cv_ref[0, sl, :].astype(_BF16)

        for r0, ws, keys in blocks:
            q0, k0 = r0 * GRID_W, ws * GRID_W
            qs = _stack_heads(q_ref[q0:q0 + m, sl], lo_q)
            kw = k_ref[k0:k0 + win * GRID_W, sl]
            vw = v_ref[k0:k0 + win * GRID_W, sl]
            bias = jnp.concatenate(
                [bias_ref[ci, GRID_W * hh:GRID_W * (hh + 1), :] for hh in range(2) for ci in keys], axis=0)
            s_nb = _dot_nt(qs, kw) + bias
            s_cx = _dot(qs, ckt)
            o2 = _softmax_pv([s_nb, s_cx], [lambda e, vw=vw: _dot(e, vw), lambda e: _dot_nt(e, cvt)])
            o_scr[q0:q0 + m, sl] = _unstack_heads(o2, lo_q)

    a_ref[...] = _rms(o_scr[...], ga_ref[...]).astype(a_ref.dtype)


def _na_call(q, k, v, ckt, cvt, rpb_pad, g_a, nb, seq):
    past = ckt.shape[2]
    _, win, _, classes = _na_plan(seq // GRID_W)
    blk = lambda b: (b, 0)
    return pl.pallas_call(
        _na_kernel,
        grid=(nb,),
        in_specs=[
            pl.BlockSpec((seq, W_A), blk),
            pl.BlockSpec((seq, W_A), blk),
            pl.BlockSpec((seq, W_A), blk),
            pl.BlockSpec((1, W_A, past), lambda b: (b, 0, 0)),
            pl.BlockSpec((1, W_A, past), lambda b: (b, 0, 0)),
            pl.BlockSpec((H_A, 2 * NA_ROWS - 1, LANES), lambda b: (0, 0, 0)),
            pl.BlockSpec((1, W_A), lambda b: (0, 0)),
        ],
        out_specs=pl.BlockSpec((seq, W_A), blk),
        out_shape=jax.ShapeDtypeStruct((nb * seq, W_A), _BF16),
        scratch_shapes=[pltpu.VMEM((len(classes), 2 * GRID_W, win * GRID_W), _F32),
                        pltpu.VMEM((seq, W_A), _F32)],
        compiler_params=pltpu.CompilerParams(
            dimension_semantics=("parallel",), vmem_limit_bytes=VMEM_LIMIT),
        name="na_attn",
    )(q, k, v, ckt, cvt, rpb_pad, g_a.reshape(1, W_A))


def _win_kernel(sink_ref, q_ref, k_ref, v_ref, ck_ref, cv_ref, gb_ref, b_ref, o_scr):
    t = q_ref.shape[0]
    qb = WIN_QBLOCKS * BLOCK
    band = qb + 2 * BLOCK
    lo = lax.broadcasted_iota(jnp.int32, (qb, LANES), 1) < HEAD_DIM
    row_lo = lax.broadcasted_iota(jnp.int32, (2 * qb, 1), 0) < qb
    qi = lax.broadcasted_iota(jnp.int32, (2 * qb, band), 0) & (qb - 1)
    kj = lax.broadcasted_iota(jnp.int32, (2 * qb, band), 1)
    rel = qi - kj
    ckt = ck_ref[0].astype(_BF16)
    cvt = cv_ref[0].astype(_BF16)

    starts = {q0: min(max(q0 - BLOCK, 0), t - band) for q0 in range(0, t, qb)}
    masks = {off: jnp.where(jnp.abs(rel + off) <= SWA_WINDOW, 0.0, NEG_INF)
             for off in sorted({q0 - k0 for q0, k0 in starts.items()})}

    for j in range(G_B):
        sl = slice(LANES * j, LANES * (j + 1))
        sink = jnp.where(row_lo, sink_ref[0, j], sink_ref[1, j])
        for q0, k0 in starts.items():
            qs = _stack_heads(q_ref[q0:q0 + qb, sl], lo)
            s = _dot_nt(qs, k_ref[k0:k0 + band, :]) + masks[q0 - k0]
            s_cx = _dot(qs, ckt)
            vband = v_ref[k0:k0 + band, :]
            o2 = _softmax_pv([s, s_cx], [lambda e, vband=vband: _dot(e, vband), lambda e: _dot_nt(e, cvt)],
                             sink)
            o_scr[q0:q0 + qb, sl] = _unstack_heads(o2, lo)

    b_ref[...] = _rms(o_scr[...], _window_head_gain(gb_ref)).astype(b_ref.dtype)


def _win_call(sink, q, k, v, ckt, cvt, g_b, nb, seq):
    past = ckt.shape[2]
    return pl.pallas_call(
        _win_kernel,
        grid=(nb,),
        in_specs=[
            pl.BlockSpec(memory_space=pltpu.SMEM),
            pl.BlockSpec((seq, W_B), lambda b: (b, W_A // W_B)),
            pl.BlockSpec((seq, KV_W_B), lambda b: (b, 0)),
            pl.BlockSpec((seq, KV_W_B), lambda b: (b, 0)),
            pl.BlockSpec((1, KV_W_B, past), lambda b: (b, 0, 0)),
            pl.BlockSpec((1, KV_W_B, past), lambda b: (b, 0, 0)),
            pl.BlockSpec((1, W_B), lambda b: (0, 0)),
        ],
        out_specs=pl.BlockSpec((seq, W_B), lambda b: (b, 0)),
        out_shape=jax.ShapeDtypeStruct((nb * seq, W_B), _BF16),
        scratch_shapes=[pltpu.VMEM((seq, W_B), _F32)],
        compiler_params=pltpu.CompilerParams(
            dimension_semantics=("parallel",), vmem_limit_bytes=VMEM_LIMIT),
        name="win_attn",
    )(sink, q, k, v, ckt, cvt, g_b.reshape(1, W_B))


def _prepare_out_weight(w_ref, w_scr):
    for r0 in range(0, W_A, CAST_COLS):
        w_scr[r0:r0 + CAST_COLS, :] = w_ref[r0:r0 + CAST_COLS, :].astype(_BF16)
    for i, h in enumerate(_QB_HEAD_ORDER):
        w_scr[W_A + HEAD_DIM * i:W_A + HEAD_DIM * (i + 1), :] = (
            w_ref[W_A + HEAD_DIM * h:W_A + HEAD_DIM * (h + 1), :].astype(_BF16))


def _zero_row(x, row):
    r0 = row - row % SUBLANES
    sub = lax.broadcasted_iota(jnp.int32, (SUBLANES, 1), 0)
    slab = jnp.where(sub == row % SUBLANES, 0.0, x[r0:r0 + SUBLANES])
    parts = ([x[:r0]] if r0 else []) + [slab] + ([x[r0 + SUBLANES:]] if r0 + SUBLANES < x.shape[0] else [])
    return jnp.concatenate(parts, axis=0)


def _conv3(u, cw, cb, seq_len):
    tm = u.shape[0]
    prev = pltpu.roll(u, 1, 0)
    nxt = pltpu.roll(u, tm - 1, 0)
    for s0 in range(0, tm, seq_len):
        prev = _zero_row(prev, s0)
        nxt = _zero_row(nxt, s0 + seq_len - 1)
    return prev * cw[0:1, :] + u * cw[1:2, :] + nxt * cw[2:3, :] + cb


def _gated_silu(gate, val):
    half = 0.5 * gate
    return ((half + half * jnp.tanh(half)) * val).astype(_BF16)


def _up(h2_ref, w):
    tm = h2_ref.shape[0]
    return jnp.concatenate(
        [_dot(h2_ref[r:r + FFN_ROW_BLOCK, :], w) for r in range(0, tm, FFN_ROW_BLOCK)], axis=0)


def _out_proj(a_ref, b_ref, x_ref, mod, gpost_ref, gffn_ref, wo_scr, x1_ref, h2_ref):
    gt1 = mod[:, 2 * D_MODEL:3 * D_MODEL]
    sh2 = mod[:, 3 * D_MODEL:4 * D_MODEL]
    sc2 = mod[:, 4 * D_MODEL:5 * D_MODEL]
    for r in range(0, x_ref.shape[0], FFN_ROW_BLOCK):
        rows = slice(r, r + FFN_ROW_BLOCK)
        y = _dot(a_ref[rows, :], wo_scr[0:W_A, :]) + _dot(b_ref[rows, :], wo_scr[W_A:W_A + W_B, :])
        x1 = x_ref[rows, :] + gt1 * _rms(y, gpost_ref[...])
        x1_ref[rows, :] = x1
        h2_ref[rows, :] = (_rms(x1, gffn_ref[...]) * (1 + sc2) + sh2).astype(h2_ref.dtype)


def _down_proj(act_ref, wd_ref, mod, gpost2_ref, x1_ref, out_ref):
    gt2 = mod[:, 5 * D_MODEL:6 * D_MODEL]
    for r in range(0, act_ref.shape[0], DOWN_ROW_BLOCK):
        rows = slice(r, r + DOWN_ROW_BLOCK)
        y = _dot(act_ref[rows, :], wd_ref[...])
        out_ref[rows, :] = x1_ref[rows, :] + gt2 * _rms(y, gpost2_ref[...])


def _tail_seq_kernel(a_ref, b_ref, x_ref, mod_ref, gpost_ref, gffn_ref, gpost2_ref, wo_ref, wu_ref,
                     cw_ref, cb_ref, wd_ref, out_ref, wo_scr, h2_scr, act_scr, *, seq_len, mod_row):
    mod = mod_ref[pl.ds(mod_row(pl.program_id(0)), 1), :]

    @pl.when(pl.program_id(0) == 0)
    def _():
        _prepare_out_weight(wo_ref, wo_scr)

    _out_proj(a_ref, b_ref, x_ref, mod, gpost_ref, gffn_ref, wo_scr, out_ref, h2_scr)
    def chunk(col0, tc):
        cols = slice(col0, col0 + tc)
        return _conv3(_up(h2_scr, wu_ref[:, cols]), cw_ref[:, cols], cb_ref[:, cols], seq_len)

    for c0 in range(0, D_FF, FF_CHUNK):
        tc = min(FF_CHUNK, D_FF - c0)
        act_scr[:, c0:c0 + tc] = _gated_silu(chunk(c0, tc), chunk(D_FF + c0, tc))
    _down_proj(act_scr, wd_ref, mod, gpost2_ref, out_ref, out_ref)


def _tail_seq_call(a, b, x2d, mod, mod_row, g_post, g_ffn, g_post2, w_out, w_up, conv_w, conv_b, w_down,
                   seq_len, name):
    t = x2d.shape[0]
    tm = max(SEQ_TAIL_TILE, seq_len)
    big = tm > SEQ_TAIL_TILE
    vmem_limit = BIG_VMEM_LIMIT if big else VMEM_LIMIT
    x_mode = dict(pipeline_mode=pl.Buffered(1)) if big else {}
    row = lambda i: (i, 0)
    const = lambda i: (0, 0)
    return pl.pallas_call(
        functools.partial(_tail_seq_kernel, seq_len=seq_len, mod_row=mod_row),
        grid=(t // tm,),
        in_specs=[
            pl.BlockSpec((tm, W_A), row),
            pl.BlockSpec((tm, W_B), row),
            pl.BlockSpec((tm, D_MODEL), row, **x_mode),
            pl.BlockSpec((MOD_ROWS, 6 * D_MODEL), const),
            pl.BlockSpec((1, D_MODEL), const),
            pl.BlockSpec((1, D_MODEL), const),
            pl.BlockSpec((1, D_MODEL), const),
            _resident((W_A + W_B, D_MODEL)),
            _resident((D_MODEL, 2 * D_FF)),
            _resident((3, 2 * D_FF)),
            _resident((1, 2 * D_FF)),
            _resident((D_FF, D_MODEL)),
        ],
        out_specs=pl.BlockSpec((tm, D_MODEL), row),
        out_shape=jax.ShapeDtypeStruct((t, D_MODEL), _F32),
        scratch_shapes=[
            pltpu.VMEM((W_A + W_B, D_MODEL), _BF16),
            pltpu.VMEM((tm, D_MODEL), _BF16),
            pltpu.VMEM((tm, D_FF), _BF16),
        ],
        compiler_params=pltpu.CompilerParams(
            dimension_semantics=("arbitrary",), vmem_limit_bytes=vmem_limit),
        name=name,
    )(a, b, x2d, mod, g_post.reshape(1, D_MODEL), g_ffn.reshape(1, D_MODEL), g_post2.reshape(1, D_MODEL),
      w_out, w_up, conv_w, conv_b.reshape(1, 2 * D_FF), w_down)


def _rope_tables(seq):
    n = ROPE_HALF
    t = np.arange(seq)
    lane = np.arange(HEAD_DIM)
    pos = np.where(lane[None, :] < HEAD_DIM // 2, (t // GRID_W)[:, None], (t % GRID_W)[:, None])
    inv = 1.0 / (ROPE_THETA ** (np.arange(n, dtype=np.float64) / n))
    ang = pos.astype(np.float64) * inv[lane % n][None, :]
    sign = np.where((lane & n) == 0, -1.0, 1.0)[None, :]
    cos = np.tile(np.cos(ang), (1, LANES // HEAD_DIM)).astype(np.float32)
    sin = np.tile(np.sin(ang) * sign, (1, LANES // HEAD_DIM)).astype(np.float32)
    return jnp.asarray(cos), jnp.asarray(sin)


def _to_head_dim_token(cache_l):
    b, l, h, d = cache_l.shape
    return jnp.transpose(cache_l, (0, 2, 3, 1)).reshape(b, h * d, l)


def _from_head_dim_token(kv_t, heads):
    b, _, l = kv_t.shape
    return jnp.transpose(kv_t.reshape(b, 1, heads, HEAD_DIM, l), (0, 1, 4, 2, 3))


def kernel(x_prompt, x_sample, cache_a_k, cache_a_v, cache_b_k, cache_b_v, c, c_ctx, w_mod, b_mod,
           g_mix_pre, g_mix_post, g_ffn_pre, g_ffn_post, w_in, rpb_a, sink_b, g_grp_a, g_grp_b,
           w_out, w_up, conv_w, conv_b, w_down):
    bp, lp, _ = x_prompt.shape
    bs, ts, _ = x_sample.shape
    assert w_in.shape[0] == 1
    xp = x_prompt.reshape(bp * lp, D_MODEL)
    xs = x_sample.reshape(bs * ts, D_MODEL)
    assert 1 + bs <= MOD_ROWS
    cond = jnp.concatenate([c_ctx[None], c, jnp.zeros((MOD_ROWS - 1 - bs, D_MODEL), _F32)], axis=0)
    cos_tab, sin_tab = _rope_tables(ts)

    l = 0
    mod = _mod_call(cond, w_mod[l], b_mod[l])
    rpb_pad = jnp.pad(rpb_a[l], ((0, 0), (0, 0), (0, LANES - (2 * NA_COLS - 1))))

    prompt_row = lambda i: 0
    assert ts >= SEQ_TAIL_TILE
    sample_row = lambda i: 1 + i

    (kat_p, vat_p, kbt_p, vbt_p, a_p, b_p), (w_up_b, w_down_b), (q_s, ka_s, va_s, kb_s, vb_s) = _front_call(
        sink_b[l], xp, xs, mod, g_mix_pre[l], w_in[l], (cos_tab, sin_tab), g_grp_a[l], g_grp_b[l],
        w_up[l], w_down[l], lp, ts)

    y_p = _tail_seq_call(a_p, b_p, xp, mod, prompt_row, g_mix_post[l], g_ffn_pre[l], g_ffn_post[l], w_out[l],
                         w_up_b, conv_w[l], conv_b[l], w_down_b, lp, "tail_prompt")

    a_s = _na_call(q_s, ka_s, va_s, _to_head_dim_token(cache_a_k[:, l]), _to_head_dim_token(cache_a_v[:, l]),
                   rpb_pad, g_grp_a[l], bs, ts)
    b_s = _win_call(sink_b[l], q_s, kb_s, vb_s, _to_head_dim_token(cache_b_k[:, l]),
                    _to_head_dim_token(cache_b_v[:, l]), g_grp_b[l], bs, ts)
    y_s = _tail_seq_call(a_s, b_s, xs, mod, sample_row, g_mix_post[l], g_ffn_pre[l], g_ffn_post[l], w_out[l],
                         w_up_b, conv_w[l], conv_b[l], w_down_b, ts, "tail_sample")

    return (y_p.reshape(bp, lp, D_MODEL), y_s.reshape(bs, ts, D_MODEL),
            _from_head_dim_token(kat_p, H_A), _from_head_dim_token(vat_p, H_A),
            _from_head_dim_token(kbt_p, KV_B), _from_head_dim_token(vbt_p, KV_B))
```

```python
import functools

import numpy as np
import jax
import jax.numpy as jnp
from jax import lax
from jax.experimental import pallas as pl
from jax.experimental.pallas import tpu as pltpu

D_MODEL = 1024
HEAD_DIM = 64
H_A = 8
H_B = 8
KV_B = 2
G_B = H_B // KV_B
W_A = H_A * HEAD_DIM
W_B = H_B * HEAD_DIM
KV_W_B = KV_B * HEAD_DIM
IN_WIDTH = 3 * W_A + W_B + 2 * KV_W_B
GRID_W = 64
NA_ROWS = 8
NA_COLS = 16
SWA_WINDOW = 128
BLOCK = 128
D_FF = 2816
ROPE_THETA = 10000.0
EPS = 1e-6
NEG_INF = -1e30
Q_SCALE = HEAD_DIM ** -0.5
ROPE_HALF = HEAD_DIM // 4

LANES = 128
SUBLANES = 8
BF16_SUBLANES = 16
FF_CHUNK = 1024
TOKEN_TILE = 512
PROJ_ROW_BLOCK = 256
SEQ_TAIL_TILE = 512
FFN_ROW_BLOCK = 256
DOWN_ROW_BLOCK = 256
MOD_ROWS = SUBLANES
MOD_STREAMS = 2
MOD_ROW_BLOCK = 128
NA_QROWS = 4
WIN_QBLOCKS = 2
CAST_COLS = 512
VMEM_LIMIT = 48 * 1024 * 1024
BIG_VMEM_LIMIT = 56 * 1024 * 1024

C_QA, C_KA, C_VA, C_QB, C_KB, C_VB = 0, W_A, 2 * W_A, 3 * W_A, 3 * W_A + W_B, 3 * W_A + W_B + KV_W_B

_QB_HEAD_ORDER = tuple(kv * G_B + g for g in range(G_B) for kv in range(KV_B))

_BF16 = jnp.bfloat16
_F32 = jnp.float32


def _dot(a, b):
    return jnp.dot(a, b, preferred_element_type=_F32)


def _dot_nt(a, b):
    return lax.dot_general(a, b, (((1,), (1,)), ((), ())), preferred_element_type=_F32)


def _rms(x, g):
    var = jnp.mean(x * x, axis=-1, keepdims=True)
    return x * lax.rsqrt(var + EPS) * g


def _stack_heads(q2, lo):
    zero = jnp.zeros_like(q2)
    return jnp.concatenate([jnp.where(lo, q2, zero), jnp.where(lo, zero, q2)], axis=0)


def _unstack_heads(o2, lo):
    m = o2.shape[0] // 2
    return jnp.where(lo, o2[:m], o2[m:])


def _softmax_pv(scores, pv, sink=None):
    m = scores[0].max(axis=-1, keepdims=True)
    for s in scores[1:]:
        m = jnp.maximum(m, s.max(axis=-1, keepdims=True))
    if sink is not None:
        m = jnp.maximum(m, sink)
    denom = None
    out = None
    for s, f in zip(scores, pv):
        e = jnp.exp(s - m)
        l = e.sum(axis=-1, keepdims=True)
        o = f(e.astype(_BF16))
        denom = l if denom is None else denom + l
        out = o if out is None else out + o
    if sink is not None:
        denom = denom + jnp.exp(sink - m)
    return out / denom


def _resident(shape):
    return pl.BlockSpec(shape, lambda *_: (0,) * len(shape), pipeline_mode=pl.Buffered(1))


def _mod_kernel(c_ref, *refs):
    w_refs, b_ref, o_ref = refs[:-2], refs[-2], refs[-1]
    i = pl.program_id(0)

    @pl.when(i == 0)
    def _():
        o_ref[...] = jnp.broadcast_to(b_ref[...], o_ref.shape)

    acc = o_ref[...]
    for j, w_ref in enumerate(w_refs):
        c = c_ref[i * len(w_refs) + j]
        s = (c * jax.nn.sigmoid(c)).astype(_BF16)
        acc = acc + _dot(s, w_ref[...].astype(_BF16))
    o_ref[...] = acc


def _mod_call(cond, w_mod, b_mod):
    n = w_mod.shape[1]
    kb = MOD_ROW_BLOCK
    nblk = D_MODEL // kb
    cond_blocks = cond.reshape(MOD_ROWS, nblk, kb).transpose(1, 0, 2)
    return pl.pallas_call(
        _mod_kernel,
        grid=(nblk // MOD_STREAMS,),
        in_specs=[pl.BlockSpec((nblk, MOD_ROWS, kb), lambda i: (0, 0, 0))]
        + [pl.BlockSpec((kb, n), lambda i, j=j: (i * MOD_STREAMS + j, 0)) for j in range(MOD_STREAMS)]
        + [pl.BlockSpec((1, n), lambda i: (0, 0))],
        out_specs=pl.BlockSpec((MOD_ROWS, n), lambda i: (0, 0)),
        out_shape=jax.ShapeDtypeStruct((MOD_ROWS, n), _F32),
        compiler_params=pltpu.CompilerParams(dimension_semantics=("arbitrary",), vmem_limit_bytes=VMEM_LIMIT),
        name="mod",
    )(cond_blocks, *([w_mod] * MOD_STREAMS), b_mod.reshape(1, n))


def _pair_window_heads(block_of, j):
    def head_at(h, want_hi):
        blk = block_of(h // 2)
        return blk if (h % 2 == 1) == want_hi else pltpu.roll(blk, HEAD_DIM, 1)

    left, right = head_at(_QB_HEAD_ORDER[2 * j], False), head_at(_QB_HEAD_ORDER[2 * j + 1], True)
    return jnp.where(lax.broadcasted_iota(jnp.int32, left.shape, 1) < HEAD_DIM, left, right)


def _window_head_gain(gb_ref):
    block_of = lambda k: jnp.broadcast_to(gb_ref[:, LANES * k:LANES * (k + 1)], (SUBLANES, LANES))
    return jnp.concatenate([_pair_window_heads(block_of, j)[0:1] for j in range(G_B)], axis=1)


def _prepare_in_weight(w_ref, w_scr):
    for c0 in list(range(0, C_QB, CAST_COLS)) + [C_KB]:
        n = min(CAST_COLS, IN_WIDTH - c0)
        w_scr[:, c0:c0 + n] = w_ref[:, c0:c0 + n].astype(_BF16)
    block_of = lambda k: w_ref[:, C_QB + LANES * k:C_QB + LANES * (k + 1)]
    for j in range(G_B):
        w_scr[:, C_QB + LANES * j:C_QB + LANES * (j + 1)] = _pair_window_heads(block_of, j).astype(_BF16)


def _rope(z, cos, sin):
    lane = lax.broadcasted_iota(jnp.int32, z.shape, 1)
    partner = jnp.where((lane & ROPE_HALF) == 0,
                        pltpu.roll(z, LANES - ROPE_HALF, 1), pltpu.roll(z, ROPE_HALF, 1))
    return z * cos + partner * sin


def _inproj_rows(x_ref, mod, g_ref, w_scr, rope_refs, outs):
    q_ref, ka_ref, va_ref, kb_ref, vb_ref = outs
    kv_transposed = rope_refs is None
    sh1 = mod[:, 0:D_MODEL]
    sc1 = mod[:, D_MODEL:2 * D_MODEL]
    rb = PROJ_ROW_BLOCK

    for r in range(0, x_ref.shape[0], rb):
        rows = slice(r, r + rb)
        h = (_rms(x_ref[rows, :], g_ref[...]) * (1 + sc1) + sh1).astype(_BF16)

        def proj(c0, n):
            return _dot(h, w_scr[:, c0:c0 + n])

        def put_kv(ref, z):
            if not kv_transposed:
                ref[rows, :] = z.astype(ref.dtype)
                return
            l = ref.shape[2]
            for t0 in range(0, rb, l):
                for c0 in range(0, z.shape[1], LANES):
                    ref[(r + t0) // l, c0:c0 + LANES, :] = z[t0:t0 + l, c0:c0 + LANES].T.astype(ref.dtype)

        q_ref[rows, 0:W_A] = (proj(C_QA, W_A) * Q_SCALE).astype(q_ref.dtype)
        put_kv(ka_ref, proj(C_KA, W_A))
        put_kv(va_ref, proj(C_VA, W_A))
        zkv = proj(C_KB, 2 * KV_W_B)
        zkb, zvb = zkv[:, :KV_W_B], zkv[:, KV_W_B:]
        zqb = proj(C_QB, W_B)
        put_kv(vb_ref, zvb)
        if rope_refs is not None:
            cos = rope_refs[0][rows, :]
            sin = rope_refs[1][rows, :]
            for j in range(W_B // LANES):
                q_ref[rows, W_A + LANES * j:W_A + LANES * (j + 1)] = (
                    _rope(zqb[:, LANES * j:LANES * (j + 1)], cos, sin) * Q_SCALE).astype(q_ref.dtype)
            put_kv(kb_ref, _rope(zkb, cos, sin))
        else:
            q_ref[rows, W_A:W_A + W_B] = (zqb * Q_SCALE).astype(q_ref.dtype)
            put_kv(kb_ref, zkb)


def _ctx_attention(q_ref, rows, seq, ka_ref, va_ref, kb_ref, vb_ref, sink_ref, ga_ref, gb_ref, a_ref, b_ref):
    l = rows.stop - rows.start
    lo = lax.broadcasted_iota(jnp.int32, (l, LANES), 1) < HEAD_DIM
    row_lo = lax.broadcasted_iota(jnp.int32, (2 * l, 1), 0) < l
    oa = []
    for p in range(W_A // LANES):
        sl = slice(LANES * p, LANES * (p + 1))
        qs = _stack_heads(q_ref[rows, sl], lo)
        s = _dot(qs, ka_ref[seq, sl, :].astype(_BF16))
        vt = va_ref[seq, sl, :].astype(_BF16)
        oa.append(_unstack_heads(_softmax_pv([s], [lambda e, vt=vt: _dot_nt(e, vt)]), lo))
    a_ref[rows, :] = _rms(jnp.concatenate(oa, axis=1), ga_ref[...]).astype(a_ref.dtype)
    kb = kb_ref[seq].astype(_BF16)
    vb = vb_ref[seq].astype(_BF16)
    ob = []
    for j in range(G_B):
        qs = _stack_heads(q_ref[rows, W_A + LANES * j:W_A + LANES * (j + 1)], lo)
        s = _dot(qs, kb)
        sink = jnp.where(row_lo, sink_ref[0, j], sink_ref[1, j])
        ob.append(_unstack_heads(_softmax_pv([s], [lambda e: _dot_nt(e, vb)], sink), lo))
    b_ref[rows, :] = _rms(jnp.concatenate(ob, axis=1), _window_head_gain(gb_ref)).astype(b_ref.dtype)


def _front_kernel(sink_ref, xp_ref, xs_ref, mod_ref, g_ref, w_ref, cos_ref, sin_ref, ga_ref, gb_ref,
                  wu_ref, wd_ref, *refs, prompt_steps, mod_row):
    kat_ref, vat_ref, kbt_ref, vbt_ref, a_ref, b_ref, wu_out, wd_out = refs[0:8]
    outs_s, w_scr, q_scr = refs[8:13], refs[13], refs[14]
    i = pl.program_id(0)

    @pl.when(i == 0)
    def _():
        _prepare_in_weight(w_ref, w_scr)

    mod = mod_ref[pl.ds(mod_row(i), 1), :]

    @pl.when(i < prompt_steps)
    def _():
        wu_out[...] = wu_ref[...].astype(wu_out.dtype)
        wd_out[...] = wd_ref[...].astype(wd_out.dtype)
        _inproj_rows(xp_ref, mod, g_ref, w_scr, None, (q_scr, kat_ref, vat_ref, kbt_ref, vbt_ref))
        l = kat_ref.shape[2]
        for seq in range(kat_ref.shape[0]):
            _ctx_attention(q_scr, slice(seq * l, (seq + 1) * l), seq, kat_ref, vat_ref, kbt_ref, vbt_ref,
                           sink_ref, ga_ref, gb_ref, a_ref, b_ref)

    @pl.when(i >= prompt_steps)
    def _():
        _inproj_rows(xs_ref, mod, g_ref, w_scr, (cos_ref, sin_ref), outs_s)


def _front_call(sink, xp, xs, mod, g_pre, w_in, rope_tabs, g_a, g_b, w_up, w_down, lp, ts):
    tm = TOKEN_TILE
    tp, tsamp = xp.shape[0], xs.shape[0]
    n_p, n_s = tp // tm, tsamp // tm
    assert n_p * tm == tp and n_s * tm == tsamp and tm % lp == 0 and ts % tm == 0
    ru, rd = w_up.shape[0] // n_p, w_down.shape[0] // n_p
    assert ru * n_p == w_up.shape[0] and rd * n_p == w_down.shape[0]
    assert ru % BF16_SUBLANES == 0 and rd % BF16_SUBLANES == 0
    tiles_per_seq = ts // tm
    p_tile = lambda i: jnp.minimum(i, n_p - 1)
    s_tile = lambda i: jnp.maximum(i - n_p, 0)
    mod_row = lambda i: jnp.where(i < n_p, 0, 1 + s_tile(i) // tiles_per_seq)
    p_row = lambda i: (p_tile(i), 0)
    s_row = lambda i: (s_tile(i), 0)
    const = lambda i: (0, 0)
    nseq = tm // lp
    in_specs = [
        pl.BlockSpec(memory_space=pltpu.SMEM),
        pl.BlockSpec((tm, D_MODEL), p_row),
        pl.BlockSpec((tm, D_MODEL), s_row),
        pl.BlockSpec((MOD_ROWS, 6 * D_MODEL), const),
        pl.BlockSpec((1, D_MODEL), const),
        _resident((D_MODEL, IN_WIDTH)),
        pl.BlockSpec((tm, LANES), lambda i: (s_tile(i) % tiles_per_seq, 0)),
        pl.BlockSpec((tm, LANES), lambda i: (s_tile(i) % tiles_per_seq, 0)),
        pl.BlockSpec((1, W_A), const),
        pl.BlockSpec((1, W_B), const),
        pl.BlockSpec((ru, w_up.shape[1]), p_row),
        pl.BlockSpec((rd, w_down.shape[1]), p_row),
    ]
    kvt_shape = lambda w: jax.ShapeDtypeStruct((tp // lp, w, lp), _F32)
    kvt_spec = lambda w: pl.BlockSpec((nseq, w, lp), lambda i: (p_tile(i), 0, 0))
    kv_shape = lambda w: jax.ShapeDtypeStruct((tsamp, w), _BF16)
    kv_spec = lambda w: pl.BlockSpec((tm, w), s_row)
    out_shape = (kvt_shape(W_A), kvt_shape(W_A), kvt_shape(KV_W_B), kvt_shape(KV_W_B),
                 jax.ShapeDtypeStruct((tp, W_A), _BF16), jax.ShapeDtypeStruct((tp, W_B), _BF16),
                 jax.ShapeDtypeStruct(w_up.shape, _BF16), jax.ShapeDtypeStruct(w_down.shape, _BF16),
                 jax.ShapeDtypeStruct((tsamp, W_A + W_B), _BF16),
                 kv_shape(W_A), kv_shape(W_A), kv_shape(KV_W_B), kv_shape(KV_W_B))
    out_specs = (kvt_spec(W_A), kvt_spec(W_A), kvt_spec(KV_W_B), kvt_spec(KV_W_B),
                 pl.BlockSpec((tm, W_A), p_row), pl.BlockSpec((tm, W_B), p_row),
                 pl.BlockSpec((ru, w_up.shape[1]), p_row), pl.BlockSpec((rd, w_down.shape[1]), p_row),
                 pl.BlockSpec((tm, W_A + W_B), s_row),
                 kv_spec(W_A), kv_spec(W_A), kv_spec(KV_W_B), kv_spec(KV_W_B))
    outs = pl.pallas_call(
        functools.partial(_front_kernel, prompt_steps=n_p, mod_row=mod_row),
        grid=(n_p + n_s,),
        in_specs=in_specs,
        out_specs=out_specs,
        out_shape=out_shape,
        scratch_shapes=[pltpu.VMEM((D_MODEL, IN_WIDTH), _BF16), pltpu.VMEM((tm, W_A + W_B), _BF16)],
        compiler_params=pltpu.CompilerParams(
            dimension_semantics=("arbitrary",), vmem_limit_bytes=BIG_VMEM_LIMIT),
        name="front",
    )(sink, xp, xs, mod, g_pre.reshape(1, D_MODEL), w_in, *rope_tabs, g_a.reshape(1, W_A),
      g_b.reshape(1, W_B), w_up, w_down)
    return outs[:6], outs[6:8], outs[8:]


def _na_plan(rows):
    kr = min(NA_ROWS, rows)
    win = kr + NA_QROWS
    row_start = lambda r: min(max(r - kr // 2, 0), rows - kr)
    blocks, classes = [], []
    for r0 in range(0, rows, NA_QROWS):
        ws = min(row_start(r0), rows - win)
        keys = []
        for r in range(r0, r0 + NA_QROWS):
            key = (ws - r + NA_ROWS - 1, row_start(r) - ws)
            if key not in classes:
                classes.append(key)
            keys.append(classes.index(key))
        blocks.append((r0, ws, keys))
    return kr, win, blocks, classes


def _na_kernel(q_ref, k_ref, v_ref, ck_ref, cv_ref, rpb_ref, ga_ref, a_ref, bias_ref, o_scr):
    rows = q_ref.shape[0] // GRID_W
    kr, win, blocks, classes = _na_plan(rows)
    m = NA_QROWS * GRID_W
    lane = lax.broadcasted_iota(jnp.int32, (GRID_W, LANES), 1)
    lo = lane < HEAD_DIM
    lo_q = lax.broadcasted_iota(jnp.int32, (m, LANES), 1) < HEAD_DIM
    cq = lax.broadcasted_iota(jnp.int32, (GRID_W, LANES), 0)
    ck = lane & (GRID_W - 1)
    cs = jnp.clip(cq - NA_COLS // 2, 0, GRID_W - NA_COLS)
    valid = (ck >= cs) & (ck < cs + NA_COLS)
    neg = jnp.full((GRID_W, LANES), NEG_INF, _F32)
    n_drow = 2 * NA_ROWS - 1

    for p in range(W_A // LANES):
        sl = slice(LANES * p, LANES * (p + 1))
        for hh in range(2):
            t_lo, t_hi = [], []
            for d in range(n_drow):
                base = jnp.broadcast_to(rpb_ref[2 * p + hh, d:d + 1, :], (GRID_W, LANES))
                t_lo.append(pltpu.roll(base, LANES - (NA_COLS - 1), 1, stride=1, stride_axis=0))
                t_hi.append(pltpu.roll(base, GRID_W - (NA_COLS - 1), 1, stride=1, stride_axis=0))
            tiles = {}
            for ci, (d_first, off) in enumerate(classes):
                for j in range(win // 2):
                    d_of = lambda i: d_first + i if off <= i < off + kr else None
                    key = (d_of(2 * j), d_of(2 * j + 1))
                    if key not in tiles:
                        left = neg if key[0] is None else t_lo[key[0]]
                        right = neg if key[1] is None else t_hi[key[1]]
                        tiles[key] = jnp.where(valid, jnp.where(lo, left, right), NEG_INF)
                    bias_ref[ci, GRID_W * hh:GRID_W * (hh + 1), LANES * j:LANES * (j + 1)] = tiles[key]

        ckt = ck_ref[0, sl, :].astype(_BF16)
        cvt = cv_ref[0, sl, :].astype(_BF16)

        for r0, ws, keys in blocks:
            q0, k0 = r0 * GRID_W, ws * GRID_W
            qs = _stack_heads(q_ref[q0:q0 + m, sl], lo_q)
            kw = k_ref[k0:k0 + win * GRID_W, sl]
            vw = v_ref[k0:k0 + win * GRID_W, sl]
            bias = jnp.concatenate(
                [bias_ref[ci, GRID_W * hh:GRID_W * (hh + 1), :] for hh in range(2) for ci in keys], axis=0)
            s_nb = _dot_nt(qs, kw) + bias
            s_cx = _dot(qs, ckt)
            o2 = _softmax_pv([s_nb, s_cx], [lambda e, vw=vw: _dot(e, vw), lambda e: _dot_nt(e, cvt)])
            o_scr[q0:q0 + m, sl] = _unstack_heads(o2, lo_q)

    a_ref[...] = _rms(o_scr[...], ga_ref[...]).astype(a_ref.dtype)


def _na_call(q, k, v, ckt, cvt, rpb_pad, g_a, nb, seq):
    past = ckt.shape[2]
    _, win, _, classes = _na_plan(seq // GRID_W)
    blk = lambda b: (b, 0)
    return pl.pallas_call(
        _na_kernel,
        grid=(nb,),
        in_specs=[
            pl.BlockSpec((seq, W_A), blk),
            pl.BlockSpec((seq, W_A), blk),
            pl.BlockSpec((seq, W_A), blk),
            pl.BlockSpec((1, W_A, past), lambda b: (b, 0, 0)),
            pl.BlockSpec((1, W_A, past), lambda b: (b, 0, 0)),
            pl.BlockSpec((H_A, 2 * NA_ROWS - 1, LANES), lambda b: (0, 0, 0)),
            pl.BlockSpec((1, W_A), lambda b: (0, 0)),
        ],
        out_specs=pl.BlockSpec((seq, W_A), blk),
        out_shape=jax.ShapeDtypeStruct((nb * seq, W_A), _BF16),
        scratch_shapes=[pltpu.VMEM((len(classes), 2 * GRID_W, win * GRID_W), _F32),
                        pltpu.VMEM((seq, W_A), _F32)],
        compiler_params=pltpu.CompilerParams(
            dimension_semantics=("parallel",), vmem_limit_bytes=VMEM_LIMIT),
        name="na_attn",
    )(q, k, v, ckt, cvt, rpb_pad, g_a.reshape(1, W_A))


def _win_kernel(sink_ref, q_ref, k_ref, v_ref, ck_ref, cv_ref, gb_ref, b_ref, o_scr):
    t = q_ref.shape[0]
    qb = WIN_QBLOCKS * BLOCK
    band = qb + 2 * BLOCK
    lo = lax.broadcasted_iota(jnp.int32, (qb, LANES), 1) < HEAD_DIM
    row_lo = lax.broadcasted_iota(jnp.int32, (2 * qb, 1), 0) < qb
    qi = lax.broadcasted_iota(jnp.int32, (2 * qb, band), 0) & (qb - 1)
    kj = lax.broadcasted_iota(jnp.int32, (2 * qb, band), 1)
    rel = qi - kj
    ckt = ck_ref[0].astype(_BF16)
    cvt = cv_ref[0].astype(_BF16)

    starts = {q0: min(max(q0 - BLOCK, 0), t - band) for q0 in range(0, t, qb)}
    masks = {off: jnp.where(jnp.abs(rel + off) <= SWA_WINDOW, 0.0, NEG_INF)
             for off in sorted({q0 - k0 for q0, k0 in starts.items()})}

    for j in range(G_B):
        sl = slice(LANES * j, LANES * (j + 1))
        sink = jnp.where(row_lo, sink_ref[0, j], sink_ref[1, j])
        for q0, k0 in starts.items():
            qs = _stack_heads(q_ref[q0:q0 + qb, sl], lo)
            s = _dot_nt(qs, k_ref[k0:k0 + band, :]) + masks[q0 - k0]
            s_cx = _dot(qs, ckt)
            vband = v_ref[k0:k0 + band, :]
            o2 = _softmax_pv([s, s_cx], [lambda e, vband=vband: _dot(e, vband), lambda e: _dot_nt(e, cvt)],
                             sink)
            o_scr[q0:q0 + qb, sl] = _unstack_heads(o2, lo)

    b_ref[...] = _rms(o_scr[...], _window_head_gain(gb_ref)).astype(b_ref.dtype)


def _win_call(sink, q, k, v, ckt, cvt, g_b, nb, seq):
    past = ckt.shape[2]
    return pl.pallas_call(
        _win_kernel,
        grid=(nb,),
        in_specs=[
            pl.BlockSpec(memory_space=pltpu.SMEM),
            pl.BlockSpec((seq, W_B), lambda b: (b, W_A // W_B)),
            pl.BlockSpec((seq, KV_W_B), lambda b: (b, 0)),
            pl.BlockSpec((seq, KV_W_B), lambda b: (b, 0)),
            pl.BlockSpec((1, KV_W_B, past), lambda b: (b, 0, 0)),
            pl.BlockSpec((1, KV_W_B, past), lambda b: (b, 0, 0)),
            pl.BlockSpec((1, W_B), lambda b: (0, 0)),
        ],
        out_specs=pl.BlockSpec((seq, W_B), lambda b: (b, 0)),
        out_shape=jax.ShapeDtypeStruct((nb * seq, W_B), _BF16),
        scratch_shapes=[pltpu.VMEM((seq, W_B), _F32)],
        compiler_params=pltpu.CompilerParams(
            dimension_semantics=("parallel",), vmem_limit_bytes=VMEM_LIMIT),
        name="win_attn",
    )(sink, q, k, v, ckt, cvt, g_b.reshape(1, W_B))


def _prepare_out_weight(w_ref, w_scr):
    for r0 in range(0, W_A, CAST_COLS):
        w_scr[r0:r0 + CAST_COLS, :] = w_ref[r0:r0 + CAST_COLS, :].astype(_BF16)
    for i, h in enumerate(_QB_HEAD_ORDER):
        w_scr[W_A + HEAD_DIM * i:W_A + HEAD_DIM * (i + 1), :] = (
            w_ref[W_A + HEAD_DIM * h:W_A + HEAD_DIM * (h + 1), :].astype(_BF16))


def _zero_row(x, row):
    r0 = row - row % SUBLANES
    sub = lax.broadcasted_iota(jnp.int32, (SUBLANES, 1), 0)
    slab = jnp.where(sub == row % SUBLANES, 0.0, x[r0:r0 + SUBLANES])
    parts = ([x[:r0]] if r0 else []) + [slab] + ([x[r0 + SUBLANES:]] if r0 + SUBLANES < x.shape[0] else [])
    return jnp.concatenate(parts, axis=0)


def _conv3(u, cw, cb, seq_len):
    tm = u.shape[0]
    prev = pltpu.roll(u, 1, 0)
    nxt = pltpu.roll(u, tm - 1, 0)
    for s0 in range(0, tm, seq_len):
        prev = _zero_row(prev, s0)
        nxt = _zero_row(nxt, s0 + seq_len - 1)
    return prev * cw[0:1, :] + u * cw[1:2, :] + nxt * cw[2:3, :] + cb


def _gated_silu(gate, val):
    half = 0.5 * gate
    return ((half + half * jnp.tanh(half)) * val).astype(_BF16)


def _up(h2_ref, w):
    tm = h2_ref.shape[0]
    return jnp.concatenate(
        [_dot(h2_ref[r:r + FFN_ROW_BLOCK, :], w) for r in range(0, tm, FFN_ROW_BLOCK)], axis=0)


def _out_proj(a_ref, b_ref, x_ref, mod, gpost_ref, gffn_ref, wo_scr, x1_ref, h2_ref):
    gt1 = mod[:, 2 * D_MODEL:3 * D_MODEL]
    sh2 = mod[:, 3 * D_MODEL:4 * D_MODEL]
    sc2 = mod[:, 4 * D_MODEL:5 * D_MODEL]
    for r in range(0, x_ref.shape[0], FFN_ROW_BLOCK):
        rows = slice(r, r + FFN_ROW_BLOCK)
        y = _dot(a_ref[rows, :], wo_scr[0:W_A, :]) + _dot(b_ref[rows, :], wo_scr[W_A:W_A + W_B, :])
        x1 = x_ref[rows, :] + gt1 * _rms(y, gpost_ref[...])
        x1_ref[rows, :] = x1
        h2_ref[rows, :] = (_rms(x1, gffn_ref[...]) * (1 + sc2) + sh2).astype(h2_ref.dtype)


def _down_proj(act_ref, wd_ref, mod, gpost2_ref, x1_ref, out_ref):
    gt2 = mod[:, 5 * D_MODEL:6 * D_MODEL]
    for r in range(0, act_ref.shape[0], DOWN_ROW_BLOCK):
        rows = slice(r, r + DOWN_ROW_BLOCK)
        y = _dot(act_ref[rows, :], wd_ref[...])
        out_ref[rows, :] = x1_ref[rows, :] + gt2 * _rms(y, gpost2_ref[...])


def _tail_seq_kernel(a_ref, b_ref, x_ref, mod_ref, gpost_ref, gffn_ref, gpost2_ref, wo_ref, wu_ref,
                     cw_ref, cb_ref, wd_ref, out_ref, wo_scr, h2_scr, act_scr, *, seq_len, mod_row):
    mod = mod_ref[pl.ds(mod_row(pl.program_id(0)), 1), :]

    @pl.when(pl.program_id(0) == 0)
    def _():
        _prepare_out_weight(wo_ref, wo_scr)

    _out_proj(a_ref, b_ref, x_ref, mod, gpost_ref, gffn_ref, wo_scr, out_ref, h2_scr)
    def chunk(col0, tc):
        cols = slice(col0, col0 + tc)
        return _conv3(_up(h2_scr, wu_ref[:, cols]), cw_ref[:, cols], cb_ref[:, cols], seq_len)

    for c0 in range(0, D_FF, FF_CHUNK):
        tc = min(FF_CHUNK, D_FF - c0)
        act_scr[:, c0:c0 + tc] = _gated_silu(chunk(c0, tc), chunk(D_FF + c0, tc))
    _down_proj(act_scr, wd_ref, mod, gpost2_ref, out_ref, out_ref)


def _tail_seq_call(a, b, x2d, mod, mod_row, g_post, g_ffn, g_post2, w_out, w_up, conv_w, conv_b, w_down,
                   seq_len, name):
    t = x2d.shape[0]
    tm = max(SEQ_TAIL_TILE, seq_len)
    big = tm > SEQ_TAIL_TILE
    vmem_limit = BIG_VMEM_LIMIT if big else VMEM_LIMIT
    x_mode = dict(pipeline_mode=pl.Buffered(1)) if big else {}
    row = lambda i: (i, 0)
    const = lambda i: (0, 0)
    return pl.pallas_call(
        functools.partial(_tail_seq_kernel, seq_len=seq_len, mod_row=mod_row),
        grid=(t // tm,),
        in_specs=[
            pl.BlockSpec((tm, W_A), row),
            pl.BlockSpec((tm, W_B), row),
            pl.BlockSpec((tm, D_MODEL), row, **x_mode),
            pl.BlockSpec((MOD_ROWS, 6 * D_MODEL), const),
            pl.BlockSpec((1, D_MODEL), const),
            pl.BlockSpec((1, D_MODEL), const),
            pl.BlockSpec((1, D_MODEL), const),
            _resident((W_A + W_B, D_MODEL)),
            _resident((D_MODEL, 2 * D_FF)),
            _resident((3, 2 * D_FF)),
            _resident((1, 2 * D_FF)),
            _resident((D_FF, D_MODEL)),
        ],
        out_specs=pl.BlockSpec((tm, D_MODEL), row),
        out_shape=jax.ShapeDtypeStruct((t, D_MODEL), _F32),
        scratch_shapes=[
            pltpu.VMEM((W_A + W_B, D_MODEL), _BF16),
            pltpu.VMEM((tm, D_MODEL), _BF16),
            pltpu.VMEM((tm, D_FF), _BF16),
        ],
        compiler_params=pltpu.CompilerParams(
            dimension_semantics=("arbitrary",), vmem_limit_bytes=vmem_limit),
        name=name,
    )(a, b, x2d, mod, g_post.reshape(1, D_MODEL), g_ffn.reshape(1, D_MODEL), g_post2.reshape(1, D_MODEL),
      w_out, w_up, conv_w, conv_b.reshape(1, 2 * D_FF), w_down)


def _rope_tables(seq):
    n = ROPE_HALF
    t = np.arange(seq)
    lane = np.arange(HEAD_DIM)
    pos = np.where(lane[None, :] < HEAD_DIM // 2, (t // GRID_W)[:, None], (t % GRID_W)[:, None])
    inv = 1.0 / (ROPE_THETA ** (np.arange(n, dtype=np.float64) / n))
    ang = pos.astype(np.float64) * inv[lane % n][None, :]
    sign = np.where((lane & n) == 0, -1.0, 1.0)[None, :]
    cos = np.tile(np.cos(ang), (1, LANES // HEAD_DIM)).astype(np.float32)
    sin = np.tile(np.sin(ang) * sign, (1, LANES // HEAD_DIM)).astype(np.float32)
    return jnp.asarray(cos), jnp.asarray(sin)


def _to_head_dim_token(cache_l):
    b, l, h, d = cache_l.shape
    return jnp.transpose(cache_l, (0, 2, 3, 1)).reshape(b, h * d, l)


def _from_head_dim_token(kv_t, heads):
    b, _, l = kv_t.shape
    return jnp.transpose(kv_t.reshape(b, 1, heads, HEAD_DIM, l), (0, 1, 4, 2, 3))


def kernel(x_prompt, x_sample, cache_a_k, cache_a_v, cache_b_k, cache_b_v, c, c_ctx, w_mod, b_mod,
           g_mix_pre, g_mix_post, g_ffn_pre, g_ffn_post, w_in, rpb_a, sink_b, g_grp_a, g_grp_b,
           w_out, w_up, conv_w, conv_b, w_down):
    bp, lp, _ = x_prompt.shape
    bs, ts, _ = x_sample.shape
    assert w_in.shape[0] == 1
    xp = x_prompt.reshape(bp * lp, D_MODEL)
    xs = x_sample.reshape(bs * ts, D_MODEL)
    assert 1 + bs <= MOD_ROWS
    cond = jnp.concatenate([c_ctx[None], c, jnp.zeros((MOD_ROWS - 1 - bs, D_MODEL), _F32)], axis=0)
    cos_tab, sin_tab = _rope_tables(ts)

    l = 0
    mod = _mod_call(cond, w_mod[l], b_mod[l])
    rpb_pad = jnp.pad(rpb_a[l], ((0, 0), (0, 0), (0, LANES - (2 * NA_COLS - 1))))

    prompt_row = lambda i: 0
    assert ts >= SEQ_TAIL_TILE
    sample_row = lambda i: 1 + i

    (kat_p, vat_p, kbt_p, vbt_p, a_p, b_p), (w_up_b, w_down_b), (q_s, ka_s, va_s, kb_s, vb_s) = _front_call(
        sink_b[l], xp, xs, mod, g_mix_pre[l], w_in[l], (cos_tab, sin_tab), g_grp_a[l], g_grp_b[l],
        w_up[l], w_down[l], lp, ts)

    y_p = _tail_seq_call(a_p, b_p, xp, mod, prompt_row, g_mix_post[l], g_ffn_pre[l], g_ffn_post[l], w_out[l],
                         w_up_b, conv_w[l], conv_b[l], w_down_b, lp, "tail_prompt")

    a_s = _na_call(q_s, ka_s, va_s, _to_head_dim_token(cache_a_k[:, l]), _to_head_dim_token(cache_a_v[:, l]),
                   rpb_pad, g_grp_a[l], bs, ts)
    b_s = _win_call(sink_b[l], q_s, kb_s, vb_s, _to_head_dim_token(cache_b_k[:, l]),
                    _to_head_dim_token(cache_b_v[:, l]), g_grp_b[l], bs, ts)
    y_s = _tail_seq_call(a_s, b_s, xs, mod, sample_row, g_mix_post[l], g_ffn_pre[l], g_ffn_post[l], w_out[l],
                         w_up_b, conv_w[l], conv_b[l], w_down_b, ts, "tail_sample")

    return (y_p.reshape(bp, lp, D_MODEL), y_s.reshape(bs, ts, D_MODEL),
            _from_head_dim_token(kat_p, H_A), _from_head_dim_token(vat_p, H_A),
            _from_head_dim_token(kbt_p, KV_B), _from_head_dim_token(vbt_p, KV_B))
```

```python
import functools

import numpy as np
import jax
import jax.numpy as jnp
from jax import lax
from jax.experimental import pallas as pl
from jax.experimental.pallas import tpu as pltpu

D_MODEL = 1024
HEAD_DIM = 64
H_A = 8
H_B = 8
KV_B = 2
G_B = H_B // KV_B
W_A = H_A * HEAD_DIM
W_B = H_B * HEAD_DIM
KV_W_B = KV_B * HEAD_DIM
IN_WIDTH = 3 * W_A + W_B + 2 * KV_W_B
GRID_W = 64
NA_ROWS = 8
NA_COLS = 16
SWA_WINDOW = 128
BLOCK = 128
D_FF = 2816
ROPE_THETA = 10000.0
EPS = 1e-6
NEG_INF = -1e30
Q_SCALE = HEAD_DIM ** -0.5
ROPE_HALF = HEAD_DIM // 4

LANES = 128
SUBLANES = 8
BF16_SUBLANES = 16
FF_CHUNK = 512
TOKEN_TILE = 512
PROJ_ROW_BLOCK = 256
SEQ_TAIL_TILE = 512
FFN_ROW_BLOCK = 256
DOWN_ROW_BLOCK = 256
MOD_ROWS = SUBLANES
MOD_STREAMS = 2
MOD_ROW_BLOCK = 128
NA_QROWS = 4
WIN_QBLOCKS = 2
CAST_COLS = 512
VMEM_LIMIT = 48 * 1024 * 1024
BIG_VMEM_LIMIT = 56 * 1024 * 1024

C_QA, C_KA, C_VA, C_QB, C_KB, C_VB = 0, W_A, 2 * W_A, 3 * W_A, 3 * W_A + W_B, 3 * W_A + W_B + KV_W_B

_QB_HEAD_ORDER = tuple(kv * G_B + g for g in range(G_B) for kv in range(KV_B))

_BF16 = jnp.bfloat16
_F32 = jnp.float32


def _dot(a, b):
    return jnp.dot(a, b, preferred_element_type=_F32)


def _dot_nt(a, b):
    return lax.dot_general(a, b, (((1,), (1,)), ((), ())), preferred_element_type=_F32)


def _rms(x, g):
    var = jnp.mean(x * x, axis=-1, keepdims=True)
    return x * lax.rsqrt(var + EPS) * g


def _stack_heads(q2, lo):
    zero = jnp.zeros_like(q2)
    return jnp.concatenate([jnp.where(lo, q2, zero), jnp.where(lo, zero, q2)], axis=0)


def _unstack_heads(o2, lo):
    m = o2.shape[0] // 2
    return jnp.where(lo, o2[:m], o2[m:])


def _softmax_pv(scores, pv, sink=None):
    m = scores[0].max(axis=-1, keepdims=True)
    for s in scores[1:]:
        m = jnp.maximum(m, s.max(axis=-1, keepdims=True))
    if sink is not None:
        m = jnp.maximum(m, sink)
    denom = None
    out = None
    for s, f in zip(scores, pv):
        e = jnp.exp(s - m)
        l = e.sum(axis=-1, keepdims=True)
        o = f(e.astype(_BF16))
        denom = l if denom is None else denom + l
        out = o if out is None else out + o
    if sink is not None:
        denom = denom + jnp.exp(sink - m)
    return out / denom


def _resident(shape):
    return pl.BlockSpec(shape, lambda *_: (0,) * len(shape), pipeline_mode=pl.Buffered(1))


def _mod_kernel(c_ref, *refs):
    w_refs, b_ref, o_ref = refs[:-2], refs[-2], refs[-1]
    i = pl.program_id(0)

    @pl.when(i == 0)
    def _():
        o_ref[...] = jnp.broadcast_to(b_ref[...], o_ref.shape)

    acc = o_ref[...]
    for j, w_ref in enumerate(w_refs):
        c = c_ref[i * len(w_refs) + j]
        s = (c * jax.nn.sigmoid(c)).astype(_BF16)
        acc = acc + _dot(s, w_ref[...].astype(_BF16))
    o_ref[...] = acc


def _mod_call(cond, w_mod, b_mod):
    n = w_mod.shape[1]
    kb = MOD_ROW_BLOCK
    nblk = D_MODEL // kb
    cond_blocks = cond.reshape(MOD_ROWS, nblk, kb).transpose(1, 0, 2)
    return pl.pallas_call(
        _mod_kernel,
        grid=(nblk // MOD_STREAMS,),
        in_specs=[pl.BlockSpec((nblk, MOD_ROWS, kb), lambda i: (0, 0, 0))]
        + [pl.BlockSpec((kb, n), lambda i, j=j: (i * MOD_STREAMS + j, 0)) for j in range(MOD_STREAMS)]
        + [pl.BlockSpec((1, n), lambda i: (0, 0))],
        out_specs=pl.BlockSpec((MOD_ROWS, n), lambda i: (0, 0)),
        out_shape=jax.ShapeDtypeStruct((MOD_ROWS, n), _F32),
        compiler_params=pltpu.CompilerParams(dimension_semantics=("arbitrary",), vmem_limit_bytes=VMEM_LIMIT),
        name="mod",
    )(cond_blocks, *([w_mod] * MOD_STREAMS), b_mod.reshape(1, n))


def _pair_window_heads(block_of, j):
    def head_at(h, want_hi):
        blk = block_of(h // 2)
        return blk if (h % 2 == 1) == want_hi else pltpu.roll(blk, HEAD_DIM, 1)

    left, right = head_at(_QB_HEAD_ORDER[2 * j], False), head_at(_QB_HEAD_ORDER[2 * j + 1], True)
    return jnp.where(lax.broadcasted_iota(jnp.int32, left.shape, 1) < HEAD_DIM, left, right)


def _window_head_gain(gb_ref):
    block_of = lambda k: jnp.broadcast_to(gb_ref[:, LANES * k:LANES * (k + 1)], (SUBLANES, LANES))
    return jnp.concatenate([_pair_window_heads(block_of, j)[0:1] for j in range(G_B)], axis=1)


def _prepare_in_weight(w_ref, w_scr):
    for c0 in list(range(0, C_QB, CAST_COLS)) + [C_KB]:
        n = min(CAST_COLS, IN_WIDTH - c0)
        w_scr[:, c0:c0 + n] = w_ref[:, c0:c0 + n].astype(_BF16)
    block_of = lambda k: w_ref[:, C_QB + LANES * k:C_QB + LANES * (k + 1)]
    for j in range(G_B):
        w_scr[:, C_QB + LANES * j:C_QB + LANES * (j + 1)] = _pair_window_heads(block_of, j).astype(_BF16)


def _rope(z, cos, sin):
    lane = lax.broadcasted_iota(jnp.int32, z.shape, 1)
    partner = jnp.where((lane & ROPE_HALF) == 0,
                        pltpu.roll(z, LANES - ROPE_HALF, 1), pltpu.roll(z, ROPE_HALF, 1))
    return z * cos + partner * sin


def _inproj_rows(x_ref, mod, g_ref, w_scr, rope_refs, outs):
    q_ref, ka_ref, va_ref, kb_ref, vb_ref = outs
    kv_transposed = rope_refs is None
    sh1 = mod[:, 0:D_MODEL]
    sc1 = mod[:, D_MODEL:2 * D_MODEL]
    rb = PROJ_ROW_BLOCK

    for r in range(0, x_ref.shape[0], rb):
        rows = slice(r, r + rb)
        h = (_rms(x_ref[rows, :], g_ref[...]) * (1 + sc1) + sh1).astype(_BF16)

        def proj(c0, n):
            return _dot(h, w_scr[:, c0:c0 + n])

        def put_kv(ref, z):
            if not kv_transposed:
                ref[rows, :] = z.astype(ref.dtype)
                return
            l = ref.shape[2]
            for t0 in range(0, rb, l):
                for c0 in range(0, z.shape[1], LANES):
                    ref[(r + t0) // l, c0:c0 + LANES, :] = z[t0:t0 + l, c0:c0 + LANES].T.astype(ref.dtype)

        za = proj(C_QA, C_QB - C_QA)
        q_ref[rows, 0:W_A] = (za[:, C_QA:C_KA] * Q_SCALE).astype(q_ref.dtype)
        put_kv(ka_ref, za[:, C_KA:C_VA])
        put_kv(va_ref, za[:, C_VA:C_QB])
        zb = proj(C_QB, IN_WIDTH - C_QB)
        zqb, zkb, zvb = zb[:, :W_B], zb[:, C_KB - C_QB:C_VB - C_QB], zb[:, C_VB - C_QB:]
        put_kv(vb_ref, zvb)
        if rope_refs is not None:
            cos = rope_refs[0][rows, :]
            sin = rope_refs[1][rows, :]
            for j in range(W_B // LANES):
                q_ref[rows, W_A + LANES * j:W_A + LANES * (j + 1)] = (
                    _rope(zqb[:, LANES * j:LANES * (j + 1)], cos, sin) * Q_SCALE).astype(q_ref.dtype)
            put_kv(kb_ref, _rope(zkb, cos, sin))
        else:
            q_ref[rows, W_A:W_A + W_B] = (zqb * Q_SCALE).astype(q_ref.dtype)
            put_kv(kb_ref, zkb)


def _ctx_attention(q_ref, rows, seq, ka_ref, va_ref, kb_ref, vb_ref, sink_ref, ga_ref, gb_ref, a_ref, b_ref):
    l = rows.stop - rows.start
    lo = lax.broadcasted_iota(jnp.int32, (l, LANES), 1) < HEAD_DIM
    row_lo = lax.broadcasted_iota(jnp.int32, (2 * l, 1), 0) < l
    oa = []
    for p in range(W_A // LANES):
        sl = slice(LANES * p, LANES * (p + 1))
        qs = _stack_heads(q_ref[rows, sl], lo)
        s = _dot(qs, ka_ref[seq, sl, :].astype(_BF16))
        vt = va_ref[seq, sl, :].astype(_BF16)
        oa.append(_unstack_heads(_softmax_pv([s], [lambda e, vt=vt: _dot_nt(e, vt)]), lo))
    a_ref[rows, :] = _rms(jnp.concatenate(oa, axis=1), ga_ref[...]).astype(a_ref.dtype)
    kb = kb_ref[seq].astype(_BF16)
    vb = vb_ref[seq].astype(_BF16)
    ob = []
    for j in range(G_B):
        qs = _stack_heads(q_ref[rows, W_A + LANES * j:W_A + LANES * (j + 1)], lo)
        s = _dot(qs, kb)
        sink = jnp.where(row_lo, sink_ref[0, j], sink_ref[1, j])
        ob.append(_unstack_heads(_softmax_pv([s], [lambda e: _dot_nt(e, vb)], sink), lo))
    b_ref[rows, :] = _rms(jnp.concatenate(ob, axis=1), _window_head_gain(gb_ref)).astype(b_ref.dtype)


def _front_kernel(sink_ref, xp_ref, xs_ref, mod_ref, g_ref, w_ref, cos_ref, sin_ref, ga_ref, gb_ref,
                  wu_ref, wd_ref, *refs, prompt_steps, mod_row):
    kat_ref, vat_ref, kbt_ref, vbt_ref, a_ref, b_ref, wu_out, wd_out = refs[0:8]
    outs_s, w_scr, q_scr = refs[8:13], refs[13], refs[14]
    i = pl.program_id(0)

    @pl.when(i == 0)
    def _():
        _prepare_in_weight(w_ref, w_scr)

    mod = mod_ref[pl.ds(mod_row(i), 1), :]

    @pl.when(i < prompt_steps)
    def _():
        wu_out[...] = wu_ref[...].astype(wu_out.dtype)
        wd_out[...] = wd_ref[...].astype(wd_out.dtype)
        _inproj_rows(xp_ref, mod, g_ref, w_scr, None, (q_scr, kat_ref, vat_ref, kbt_ref, vbt_ref))
        l = kat_ref.shape[2]
        for seq in range(kat_ref.shape[0]):
            _ctx_attention(q_scr, slice(seq * l, (seq + 1) * l), seq, kat_ref, vat_ref, kbt_ref, vbt_ref,
                           sink_ref, ga_ref, gb_ref, a_ref, b_ref)

    @pl.when(i >= prompt_steps)
    def _():
        _inproj_rows(xs_ref, mod, g_ref, w_scr, (cos_ref, sin_ref), outs_s)


def _front_call(sink, xp, xs, mod, g_pre, w_in, rope_tabs, g_a, g_b, w_up, w_down, lp, ts):
    tm = TOKEN_TILE
    tp, tsamp = xp.shape[0], xs.shape[0]
    n_p, n_s = tp // tm, tsamp // tm
    assert n_p * tm == tp and n_s * tm == tsamp and tm % lp == 0 and ts % tm == 0
    ru, rd = w_up.shape[0] // n_p, w_down.shape[0] // n_p
    assert ru * n_p == w_up.shape[0] and rd * n_p == w_down.shape[0]
    assert ru % BF16_SUBLANES == 0 and rd % BF16_SUBLANES == 0
    tiles_per_seq = ts // tm
    p_tile = lambda i: jnp.minimum(i, n_p - 1)
    s_tile = lambda i: jnp.maximum(i - n_p, 0)
    mod_row = lambda i: jnp.where(i < n_p, 0, 1 + s_tile(i) // tiles_per_seq)
    p_row = lambda i: (p_tile(i), 0)
    s_row = lambda i: (s_tile(i), 0)
    const = lambda i: (0, 0)
    nseq = tm // lp
    in_specs = [
        pl.BlockSpec(memory_space=pltpu.SMEM),
        pl.BlockSpec((tm, D_MODEL), p_row),
        pl.BlockSpec((tm, D_MODEL), s_row),
        pl.BlockSpec((MOD_ROWS, 6 * D_MODEL), const),
        pl.BlockSpec((1, D_MODEL), const),
        _resident((D_MODEL, IN_WIDTH)),
        pl.BlockSpec((tm, LANES), lambda i: (s_tile(i) % tiles_per_seq, 0)),
        pl.BlockSpec((tm, LANES), lambda i: (s_tile(i) % tiles_per_seq, 0)),
        pl.BlockSpec((1, W_A), const),
        pl.BlockSpec((1, W_B), const),
        pl.BlockSpec((ru, w_up.shape[1]), p_row),
        pl.BlockSpec((rd, w_down.shape[1]), p_row),
    ]
    kvt_shape = lambda w: jax.ShapeDtypeStruct((tp // lp, w, lp), _F32)
    kvt_spec = lambda w: pl.BlockSpec((nseq, w, lp), lambda i: (p_tile(i), 0, 0))
    kv_shape = lambda w: jax.ShapeDtypeStruct((tsamp, w), _BF16)
    kv_spec = lambda w: pl.BlockSpec((tm, w), s_row)
    out_shape = (kvt_shape(W_A), kvt_shape(W_A), kvt_shape(KV_W_B), kvt_shape(KV_W_B),
                 jax.ShapeDtypeStruct((tp, W_A), _BF16), jax.ShapeDtypeStruct((tp, W_B), _BF16),
                 jax.ShapeDtypeStruct(w_up.shape, _BF16), jax.ShapeDtypeStruct(w_down.shape, _BF16),
                 jax.ShapeDtypeStruct((tsamp, W_A + W_B), _BF16),
                 kv_shape(W_A), kv_shape(W_A), kv_shape(KV_W_B), kv_shape(KV_W_B))
    out_specs = (kvt_spec(W_A), kvt_spec(W_A), kvt_spec(KV_W_B), kvt_spec(KV_W_B),
                 pl.BlockSpec((tm, W_A), p_row), pl.BlockSpec((tm, W_B), p_row),
                 pl.BlockSpec((ru, w_up.shape[1]), p_row), pl.BlockSpec((rd, w_down.shape[1]), p_row),
                 pl.BlockSpec((tm, W_A + W_B), s_row),
                 kv_spec(W_A), kv_spec(W_A), kv_spec(KV_W_B), kv_spec(KV_W_B))
    outs = pl.pallas_call(
        functools.partial(_front_kernel, prompt_steps=n_p, mod_row=mod_row),
        grid=(n_p + n_s,),
        in_specs=in_specs,
        out_specs=out_specs,
        out_shape=out_shape,
        scratch_shapes=[pltpu.VMEM((D_MODEL, IN_WIDTH), _BF16), pltpu.VMEM((tm, W_A + W_B), _BF16)],
        compiler_params=pltpu.CompilerParams(
            dimension_semantics=("arbitrary",), vmem_limit_bytes=BIG_VMEM_LIMIT),
        name="front",
    )(sink, xp, xs, mod, g_pre.reshape(1, D_MODEL), w_in, *rope_tabs, g_a.reshape(1, W_A),
      g_b.reshape(1, W_B), w_up, w_down)
    return outs[:6], outs[6:8], outs[8:]


def _na_plan(rows):
    kr = min(NA_ROWS, rows)
    win = kr + NA_QROWS
    row_start = lambda r: min(max(r - kr // 2, 0), rows - kr)
    blocks, classes = [], []
    for r0 in range(0, rows, NA_QROWS):
        ws = min(row_start(r0), rows - win)
        keys = []
        for r in range(r0, r0 + NA_QROWS):
            key = (ws - r + NA_ROWS - 1, row_start(r) - ws)
            if key not in classes:
                classes.append(key)
            keys.append(classes.index(key))
        blocks.append((r0, ws, keys))
    return kr, win, blocks, classes


def _na_kernel(q_ref, k_ref, v_ref, ck_ref, cv_ref, rpb_ref, ga_ref, a_ref, bias_ref, o_scr):
    rows = q_ref.shape[0] // GRID_W
    kr, win, blocks, classes = _na_plan(rows)
    m = NA_QROWS * GRID_W
    lane = lax.broadcasted_iota(jnp.int32, (GRID_W, LANES), 1)
    lo = lane < HEAD_DIM
    lo_q = lax.broadcasted_iota(jnp.int32, (m, LANES), 1) < HEAD_DIM
    cq = lax.broadcasted_iota(jnp.int32, (GRID_W, LANES), 0)
    ck = lane & (GRID_W - 1)
    cs = jnp.clip(cq - NA_COLS // 2, 0, GRID_W - NA_COLS)
    valid = (ck >= cs) & (ck < cs + NA_COLS)
    neg = jnp.full((GRID_W, LANES), NEG_INF, _F32)
    n_drow = 2 * NA_ROWS - 1

    for p in range(W_A // LANES):
        sl = slice(LANES * p, LANES * (p + 1))
        for hh in range(2):
            t_lo, t_hi = [], []
            for d in range(n_drow):
                base = jnp.broadcast_to(rpb_ref[2 * p + hh, d:d + 1, :], (GRID_W, LANES))
                t_lo.append(pltpu.roll(base, LANES - (NA_COLS - 1), 1, stride=1, stride_axis=0))
                t_hi.append(pltpu.roll(base, GRID_W - (NA_COLS - 1), 1, stride=1, stride_axis=0))
            tiles = {}
            for ci, (d_first, off) in enumerate(classes):
                for j in range(win // 2):
                    d_of = lambda i: d_first + i if off <= i < off + kr else None
                    key = (d_of(2 * j), d_of(2 * j + 1))
                    if key not in tiles:
                        left = neg if key[0] is None else t_lo[key[0]]
                        right = neg if key[1] is None else t_hi[key[1]]
                        tiles[key] = jnp.where(valid, jnp.where(lo, left, right), NEG_INF)
                    bias_ref[ci, GRID_W * hh:GRID_W * (hh + 1), LANES * j:LANES * (j + 1)] = tiles[key]

        ckt = ck_ref[0, sl, :].astype(_BF16)
        cvt = cv_ref[0, sl, :].astype(_BF16)

        for r0, ws, keys in blocks:
            q0, k0 = r0 * GRID_W, ws * GRID_W
            qs = _stack_heads(q_ref[q0:q0 + m, sl], lo_q)
            kw = k_ref[k0:k0 + win * GRID_W, sl]
            vw = v_ref[k0:k0 + win * GRID_W, sl]
            bias = jnp.concatenate(
                [bias_ref[ci, GRID_W * hh:GRID_W * (hh + 1), :] for hh in range(2) for ci in keys], axis=0)
            s_nb = _dot_nt(qs, kw) + bias
            s_cx = _dot(qs, ckt)
            o2 = _softmax_pv([s_nb, s_cx], [lambda e, vw=vw: _dot(e, vw), lambda e: _dot_nt(e, cvt)])
            o_scr[q0:q0 + m, sl] = _unstack_heads(o2, lo_q)

    a_ref[...] = _rms(o_scr[...], ga_ref[...]).astype(a_ref.dtype)


def _na_call(q, k, v, ckt, cvt, rpb_pad, g_a, nb, seq):
    past = ckt.shape[2]
    _, win, _, classes = _na_plan(seq // GRID_W)
    blk = lambda b: (b, 0)
    return pl.pallas_call(
        _na_kernel,
        grid=(nb,),
        in_specs=[
            pl.BlockSpec((seq, W_A), blk),
            pl.BlockSpec((seq, W_A), blk),
            pl.BlockSpec((seq, W_A), blk),
            pl.BlockSpec((1, W_A, past), lambda b: (b, 0, 0)),
            pl.BlockSpec((1, W_A, past), lambda b: (b, 0, 0)),
            pl.BlockSpec((H_A, 2 * NA_ROWS - 1, LANES), lambda b: (0, 0, 0)),
            pl.BlockSpec((1, W_A), lambda b: (0, 0)),
        ],
        out_specs=pl.BlockSpec((seq, W_A), blk),
        out_shape=jax.ShapeDtypeStruct((nb * seq, W_A), _BF16),
        scratch_shapes=[pltpu.VMEM((len(classes), 2 * GRID_W, win * GRID_W), _F32),
                        pltpu.VMEM((seq, W_A), _F32)],
        compiler_params=pltpu.CompilerParams(
            dimension_semantics=("parallel",), vmem_limit_bytes=VMEM_LIMIT),
        name="na_attn",
    )(q, k, v, ckt, cvt, rpb_pad, g_a.reshape(1, W_A))


def _win_kernel(sink_ref, q_ref, k_ref, v_ref, ck_ref, cv_ref, gb_ref, b_ref, o_scr):
    t = q_ref.shape[0]
    qb = WIN_QBLOCKS * BLOCK
    band = qb + 2 * BLOCK
    lo = lax.broadcasted_iota(jnp.int32, (qb, LANES), 1) < HEAD_DIM
    row_lo = lax.broadcasted_iota(jnp.int32, (2 * qb, 1), 0) < qb
    qi = lax.broadcasted_iota(jnp.int32, (2 * qb, band), 0) & (qb - 1)
    kj = lax.broadcasted_iota(jnp.int32, (2 * qb, band), 1)
    rel = qi - kj
    ckt = ck_ref[0].astype(_BF16)
    cvt = cv_ref[0].astype(_BF16)

    starts = {q0: min(max(q0 - BLOCK, 0), t - band) for q0 in range(0, t, qb)}
    masks = {off: jnp.where(jnp.abs(rel + off) <= SWA_WINDOW, 0.0, NEG_INF)
             for off in sorted({q0 - k0 for q0, k0 in starts.items()})}

    for j in range(G_B):
        sl = slice(LANES * j, LANES * (j + 1))
        sink = jnp.where(row_lo, sink_ref[0, j], sink_ref[1, j])
        for q0, k0 in starts.items():
            qs = _stack_heads(q_ref[q0:q0 + qb, sl], lo)
            s = _dot_nt(qs, k_ref[k0:k0 + band, :]) + masks[q0 - k0]
            s_cx = _dot(qs, ckt)
            vband = v_ref[k0:k0 + band, :]
            o2 = _softmax_pv([s, s_cx], [lambda e, vband=vband: _dot(e, vband), lambda e: _dot_nt(e, cvt)],
                             sink)
            o_scr[q0:q0 + qb, sl] = _unstack_heads(o2, lo)

    b_ref[...] = _rms(o_scr[...], _window_head_gain(gb_ref)).astype(b_ref.dtype)


def _win_call(sink, q, k, v, ckt, cvt, g_b, nb, seq):
    past = ckt.shape[2]
    return pl.pallas_call(
        _win_kernel,
        grid=(nb,),
        in_specs=[
            pl.BlockSpec(memory_space=pltpu.SMEM),
            pl.BlockSpec((seq, W_B), lambda b: (b, W_A // W_B)),
            pl.BlockSpec((seq, KV_W_B), lambda b: (b, 0)),
            pl.BlockSpec((seq, KV_W_B), lambda b: (b, 0)),
            pl.BlockSpec((1, KV_W_B, past), lambda b: (b, 0, 0)),
            pl.BlockSpec((1, KV_W_B, past), lambda b: (b, 0, 0)),
            pl.BlockSpec((1, W_B), lambda b: (0, 0)),
        ],
        out_specs=pl.BlockSpec((seq, W_B), lambda b: (b, 0)),
        out_shape=jax.ShapeDtypeStruct((nb * seq, W_B), _BF16),
        scratch_shapes=[pltpu.VMEM((seq, W_B), _F32)],
        compiler_params=pltpu.CompilerParams(
            dimension_semantics=("parallel",), vmem_limit_bytes=VMEM_LIMIT),
        name="win_attn",
    )(sink, q, k, v, ckt, cvt, g_b.reshape(1, W_B))


def _prepare_out_weight(w_ref, w_scr):
    for r0 in range(0, W_A, CAST_COLS):
        w_scr[r0:r0 + CAST_COLS, :] = w_ref[r0:r0 + CAST_COLS, :].astype(_BF16)
    for i, h in enumerate(_QB_HEAD_ORDER):
        w_scr[W_A + HEAD_DIM * i:W_A + HEAD_DIM * (i + 1), :] = (
            w_ref[W_A + HEAD_DIM * h:W_A + HEAD_DIM * (h + 1), :].astype(_BF16))


def _zero_row(x, row):
    r0 = row - row % SUBLANES
    sub = lax.broadcasted_iota(jnp.int32, (SUBLANES, 1), 0)
    slab = jnp.where(sub == row % SUBLANES, 0.0, x[r0:r0 + SUBLANES])
    parts = ([x[:r0]] if r0 else []) + [slab] + ([x[r0 + SUBLANES:]] if r0 + SUBLANES < x.shape[0] else [])
    return jnp.concatenate(parts, axis=0)


def _conv3(u, cw, cb, seq_len):
    tm = u.shape[0]
    prev = pltpu.roll(u, 1, 0)
    nxt = pltpu.roll(u, tm - 1, 0)
    for s0 in range(0, tm, seq_len):
        prev = _zero_row(prev, s0)
        nxt = _zero_row(nxt, s0 + seq_len - 1)
    return prev * cw[0:1, :] + u * cw[1:2, :] + nxt * cw[2:3, :] + cb


def _gated_silu(gate, val):
    half = 0.5 * gate
    return ((half + half * jnp.tanh(half)) * val).astype(_BF16)


def _up(h2_ref, w):
    tm = h2_ref.shape[0]
    return jnp.concatenate(
        [_dot(h2_ref[r:r + FFN_ROW_BLOCK, :], w) for r in range(0, tm, FFN_ROW_BLOCK)], axis=0)


def _out_proj(a_ref, b_ref, x_ref, mod, gpost_ref, gffn_ref, wo_scr, x1_ref, h2_ref):
    gt1 = mod[:, 2 * D_MODEL:3 * D_MODEL]
    sh2 = mod[:, 3 * D_MODEL:4 * D_MODEL]
    sc2 = mod[:, 4 * D_MODEL:5 * D_MODEL]
    for r in range(0, x_ref.shape[0], FFN_ROW_BLOCK):
        rows = slice(r, r + FFN_ROW_BLOCK)
        y = _dot(jnp.concatenate([a_ref[rows, :], b_ref[rows, :]], axis=1), wo_scr[...])
        x1 = x_ref[rows, :] + gt1 * _rms(y, gpost_ref[...])
        x1_ref[rows, :] = x1
        h2_ref[rows, :] = (_rms(x1, gffn_ref[...]) * (1 + sc2) + sh2).astype(h2_ref.dtype)


def _down_proj(act_ref, wd_ref, mod, gpost2_ref, x1_ref, out_ref):
    gt2 = mod[:, 5 * D_MODEL:6 * D_MODEL]
    for r in range(0, act_ref.shape[0], DOWN_ROW_BLOCK):
        rows = slice(r, r + DOWN_ROW_BLOCK)
        y = _dot(act_ref[rows, :], wd_ref[...])
        out_ref[rows, :] = x1_ref[rows, :] + gt2 * _rms(y, gpost2_ref[...])


def _tail_seq_kernel(a_ref, b_ref, x_ref, mod_ref, gpost_ref, gffn_ref, gpost2_ref, wo_ref, wu_ref,
                     cw_ref, cb_ref, wd_ref, out_ref, wo_scr, h2_scr, act_scr, *, seq_len, mod_row):
    mod = mod_ref[pl.ds(mod_row(pl.program_id(0)), 1), :]

    @pl.when(pl.program_id(0) == 0)
    def _():
        _prepare_out_weight(wo_ref, wo_scr)

    _out_proj(a_ref, b_ref, x_ref, mod, gpost_ref, gffn_ref, wo_scr, out_ref, h2_scr)
    def chunk(col0, tc):
        cols = slice(col0, col0 + tc)
        return _conv3(_up(h2_scr, wu_ref[:, cols]), cw_ref[:, cols], cb_ref[:, cols], seq_len)

    for c0 in range(0, D_FF, FF_CHUNK):
        tc = min(FF_CHUNK, D_FF - c0)
        act_scr[:, c0:c0 + tc] = _gated_silu(chunk(c0, tc), chunk(D_FF + c0, tc))
    _down_proj(act_scr, wd_ref, mod, gpost2_ref, out_ref, out_ref)


def _tail_seq_call(a, b, x2d, mod, mod_row, g_post, g_ffn, g_post2, w_out, w_up, conv_w, conv_b, w_down,
                   seq_len, name):
    t = x2d.shape[0]
    tm = max(SEQ_TAIL_TILE, seq_len)
    big = tm > SEQ_TAIL_TILE
    vmem_limit = BIG_VMEM_LIMIT if big else VMEM_LIMIT
    x_mode = dict(pipeline_mode=pl.Buffered(1)) if big else {}
    row = lambda i: (i, 0)
    const = lambda i: (0, 0)
    return pl.pallas_call(
        functools.partial(_tail_seq_kernel, seq_len=seq_len, mod_row=mod_row),
        grid=(t // tm,),
        in_specs=[
            pl.BlockSpec((tm, W_A), row),
            pl.BlockSpec((tm, W_B), row),
            pl.BlockSpec((tm, D_MODEL), row, **x_mode),
            pl.BlockSpec((MOD_ROWS, 6 * D_MODEL), const),
            pl.BlockSpec((1, D_MODEL), const),
            pl.BlockSpec((1, D_MODEL), const),
            pl.BlockSpec((1, D_MODEL), const),
            _resident((W_A + W_B, D_MODEL)),
            _resident((D_MODEL, 2 * D_FF)),
            _resident((3, 2 * D_FF)),
            _resident((1, 2 * D_FF)),
            _resident((D_FF, D_MODEL)),
        ],
        out_specs=pl.BlockSpec((tm, D_MODEL), row),
        out_shape=jax.ShapeDtypeStruct((t, D_MODEL), _F32),
        scratch_shapes=[
            pltpu.VMEM((W_A + W_B, D_MODEL), _BF16),
            pltpu.VMEM((tm, D_MODEL), _BF16),
            pltpu.VMEM((tm, D_FF), _BF16),
        ],
        compiler_params=pltpu.CompilerParams(
            dimension_semantics=("arbitrary",), vmem_limit_bytes=vmem_limit),
        name=name,
    )(a, b, x2d, mod, g_post.reshape(1, D_MODEL), g_ffn.reshape(1, D_MODEL), g_post2.reshape(1, D_MODEL),
      w_out, w_up, conv_w, conv_b.reshape(1, 2 * D_FF), w_down)


def _rope_tables(seq):
    n = ROPE_HALF
    t = np.arange(seq)
    lane = np.arange(HEAD_DIM)
    pos = np.where(lane[None, :] < HEAD_DIM // 2, (t // GRID_W)[:, None], (t % GRID_W)[:, None])
    inv = 1.0 / (ROPE_THETA ** (np.arange(n, dtype=np.float64) / n))
    ang = pos.astype(np.float64) * inv[lane % n][None, :]
    sign = np.where((lane & n) == 0, -1.0, 1.0)[None, :]
    cos = np.tile(np.cos(ang), (1, LANES // HEAD_DIM)).astype(np.float32)
    sin = np.tile(np.sin(ang) * sign, (1, LANES // HEAD_DIM)).astype(np.float32)
    return jnp.asarray(cos), jnp.asarray(sin)


def _to_head_dim_token(cache_l):
    b, l, h, d = cache_l.shape
    return jnp.transpose(cache_l, (0, 2, 3, 1)).reshape(b, h * d, l)


def _from_head_dim_token(kv_t, heads):
    b, _, l = kv_t.shape
    return jnp.transpose(kv_t.reshape(b, 1, heads, HEAD_DIM, l), (0, 1, 4, 2, 3))


def kernel(x_prompt, x_sample, cache_a_k, cache_a_v, cache_b_k, cache_b_v, c, c_ctx, w_mod, b_mod,
           g_mix_pre, g_mix_post, g_ffn_pre, g_ffn_post, w_in, rpb_a, sink_b, g_grp_a, g_grp_b,
           w_out, w_up, conv_w, conv_b, w_down):
    bp, lp, _ = x_prompt.shape
    bs, ts, _ = x_sample.shape
    assert w_in.shape[0] == 1
    xp = x_prompt.reshape(bp * lp, D_MODEL)
    xs = x_sample.reshape(bs * ts, D_MODEL)
    assert 1 + bs <= MOD_ROWS
    cond = jnp.concatenate([c_ctx[None], c, jnp.zeros((MOD_ROWS - 1 - bs, D_MODEL), _F32)], axis=0)
    cos_tab, sin_tab = _rope_tables(ts)

    l = 0
    mod = _mod_call(cond, w_mod[l], b_mod[l])
    rpb_pad = jnp.pad(rpb_a[l], ((0, 0), (0, 0), (0, LANES - (2 * NA_COLS - 1))))

    prompt_row = lambda i: 0
    assert ts >= SEQ_TAIL_TILE
    sample_row = lambda i: 1 + i

    (kat_p, vat_p, kbt_p, vbt_p, a_p, b_p), (w_up_b, w_down_b), (q_s, ka_s, va_s, kb_s, vb_s) = _front_call(
        sink_b[l], xp, xs, mod, g_mix_pre[l], w_in[l], (cos_tab, sin_tab), g_grp_a[l], g_grp_b[l],
        w_up[l], w_down[l], lp, ts)

    y_p = _tail_seq_call(a_p, b_p, xp, mod, prompt_row, g_mix_post[l], g_ffn_pre[l], g_ffn_post[l], w_out[l],
                         w_up_b, conv_w[l], conv_b[l], w_down_b, lp, "tail_prompt")

    a_s = _na_call(q_s, ka_s, va_s, _to_head_dim_token(cache_a_k[:, l]), _to_head_dim_token(cache_a_v[:, l]),
                   rpb_pad, g_grp_a[l], bs, ts)
    b_s = _win_call(sink_b[l], q_s, kb_s, vb_s, _to_head_dim_token(cache_b_k[:, l]),
                    _to_head_dim_token(cache_b_v[:, l]), g_grp_b[l], bs, ts)
    y_s = _tail_seq_call(a_s, b_s, xs, mod, sample_row, g_mix_post[l], g_ffn_pre[l], g_ffn_post[l], w_out[l],
                         w_up_b, conv_w[l], conv_b[l], w_down_b, ts, "tail_sample")

    return (y_p.reshape(bp, lp, D_MODEL), y_s.reshape(bs, ts, D_MODEL),
            _from_head_dim_token(kat_p, H_A), _from_head_dim_token(vat_p, H_A),
            _from_head_dim_token(kbt_p, KV_B), _from_head_dim_token(vbt_p, KV_B))
```

```python
import functools

import numpy as np
import jax
import jax.numpy as jnp
from jax import lax
from jax.experimental import pallas as pl
from jax.experimental.pallas import tpu as pltpu

D_MODEL = 1024
HEAD_DIM = 64
H_A = 8
H_B = 8
KV_B = 2
G_B = H_B // KV_B
W_A = H_A * HEAD_DIM
W_B = H_B * HEAD_DIM
KV_W_B = KV_B * HEAD_DIM
IN_WIDTH = 3 * W_A + W_B + 2 * KV_W_B
GRID_W = 64
NA_ROWS = 8
NA_COLS = 16
SWA_WINDOW = 128
BLOCK = 128
D_FF = 2816
ROPE_THETA = 10000.0
EPS = 1e-6
NEG_INF = -1e30
Q_SCALE = HEAD_DIM ** -0.5
ROPE_HALF = HEAD_DIM // 4

LANES = 128
SUBLANES = 8
BF16_SUBLANES = 16
FF_CHUNK = 512
TOKEN_TILE = 512
PROJ_ROW_BLOCK = 256
SEQ_TAIL_TILE = 512
FFN_ROW_BLOCK = 256
DOWN_ROW_BLOCK = 256
MOD_ROWS = SUBLANES
MOD_STREAMS = 2
MOD_ROW_BLOCK = 128
NA_QROWS = 4
WIN_QBLOCKS = 2
CAST_COLS = 512
VMEM_LIMIT = 48 * 1024 * 1024
BIG_VMEM_LIMIT = 56 * 1024 * 1024

C_QA, C_KA, C_VA, C_QB, C_KB, C_VB = 0, W_A, 2 * W_A, 3 * W_A, 3 * W_A + W_B, 3 * W_A + W_B + KV_W_B

_QB_HEAD_ORDER = tuple(kv * G_B + g for g in range(G_B) for kv in range(KV_B))

_BF16 = jnp.bfloat16
_F32 = jnp.float32


def _dot(a, b):
    return jnp.dot(a, b, preferred_element_type=_F32)


def _dot_nt(a, b):
    return lax.dot_general(a, b, (((1,), (1,)), ((), ())), preferred_element_type=_F32)


def _rms(x, g):
    var = jnp.mean(x * x, axis=-1, keepdims=True)
    return x * lax.rsqrt(var + EPS) * g


def _stack_heads(q2, lo):
    zero = jnp.zeros_like(q2)
    return jnp.concatenate([jnp.where(lo, q2, zero), jnp.where(lo, zero, q2)], axis=0)


def _unstack_heads(o2, lo):
    m = o2.shape[0] // 2
    return jnp.where(lo, o2[:m], o2[m:])


def _softmax_pv(scores, pv, sink=None):
    m = scores[0].max(axis=-1, keepdims=True)
    for s in scores[1:]:
        m = jnp.maximum(m, s.max(axis=-1, keepdims=True))
    if sink is not None:
        m = jnp.maximum(m, sink)
    denom = None
    out = None
    for s, f in zip(scores, pv):
        e = jnp.exp(s - m)
        l = e.sum(axis=-1, keepdims=True)
        o = f(e.astype(_BF16))
        denom = l if denom is None else denom + l
        out = o if out is None else out + o
    if sink is not None:
        denom = denom + jnp.exp(sink - m)
    return out / denom


def _resident(shape):
    return pl.BlockSpec(shape, lambda *_: (0,) * len(shape), pipeline_mode=pl.Buffered(1))


def _mod_kernel(c_ref, *refs):
    w_refs, b_ref, o_ref = refs[:-2], refs[-2], refs[-1]
    i = pl.program_id(0)

    @pl.when(i == 0)
    def _():
        o_ref[...] = jnp.broadcast_to(b_ref[...], o_ref.shape)

    acc = o_ref[...]
    for j, w_ref in enumerate(w_refs):
        c = c_ref[i * len(w_refs) + j]
        s = (c * jax.nn.sigmoid(c)).astype(_BF16)
        acc = acc + _dot(s, w_ref[...].astype(_BF16))
    o_ref[...] = acc


def _mod_call(cond, w_mod, b_mod):
    n = w_mod.shape[1]
    kb = MOD_ROW_BLOCK
    nblk = D_MODEL // kb
    cond_blocks = cond.reshape(MOD_ROWS, nblk, kb).transpose(1, 0, 2)
    return pl.pallas_call(
        _mod_kernel,
        grid=(nblk // MOD_STREAMS,),
        in_specs=[pl.BlockSpec((nblk, MOD_ROWS, kb), lambda i: (0, 0, 0))]
        + [pl.BlockSpec((kb, n), lambda i, j=j: (i * MOD_STREAMS + j, 0)) for j in range(MOD_STREAMS)]
        + [pl.BlockSpec((1, n), lambda i: (0, 0))],
        out_specs=pl.BlockSpec((MOD_ROWS, n), lambda i: (0, 0)),
        out_shape=jax.ShapeDtypeStruct((MOD_ROWS, n), _F32),
        compiler_params=pltpu.CompilerParams(dimension_semantics=("arbitrary",), vmem_limit_bytes=VMEM_LIMIT),
        name="mod",
    )(cond_blocks, *([w_mod] * MOD_STREAMS), b_mod.reshape(1, n))


def _pair_window_heads(block_of, j):
    def head_at(h, want_hi):
        blk = block_of(h // 2)
        return blk if (h % 2 == 1) == want_hi else pltpu.roll(blk, HEAD_DIM, 1)

    left, right = head_at(_QB_HEAD_ORDER[2 * j], False), head_at(_QB_HEAD_ORDER[2 * j + 1], True)
    return jnp.where(lax.broadcasted_iota(jnp.int32, left.shape, 1) < HEAD_DIM, left, right)


def _window_head_gain(gb_ref):
    block_of = lambda k: jnp.broadcast_to(gb_ref[:, LANES * k:LANES * (k + 1)], (SUBLANES, LANES))
    return jnp.concatenate([_pair_window_heads(block_of, j)[0:1] for j in range(G_B)], axis=1)


def _prepare_in_weight(w_ref, w_scr):
    for c0 in list(range(0, C_QB, CAST_COLS)) + [C_KB]:
        n = min(CAST_COLS, IN_WIDTH - c0)
        w_scr[:, c0:c0 + n] = w_ref[:, c0:c0 + n].astype(_BF16)
    block_of = lambda k: w_ref[:, C_QB + LANES * k:C_QB + LANES * (k + 1)]
    for j in range(G_B):
        w_scr[:, C_QB + LANES * j:C_QB + LANES * (j + 1)] = _pair_window_heads(block_of, j).astype(_BF16)


def _rope(z, cos, sin):
    lane = lax.broadcasted_iota(jnp.int32, z.shape, 1)
    partner = jnp.where((lane & ROPE_HALF) == 0,
                        pltpu.roll(z, LANES - ROPE_HALF, 1), pltpu.roll(z, ROPE_HALF, 1))
    return z * cos + partner * sin


def _inproj_rows(x_ref, mod, g_ref, w_scr, rope_refs, outs):
    q_ref, ka_ref, va_ref, kb_ref, vb_ref = outs
    kv_transposed = rope_refs is None
    sh1 = mod[:, 0:D_MODEL]
    sc1 = mod[:, D_MODEL:2 * D_MODEL]
    rb = PROJ_ROW_BLOCK

    for r in range(0, x_ref.shape[0], rb):
        rows = slice(r, r + rb)
        h = (_rms(x_ref[rows, :], g_ref[...]) * (1 + sc1) + sh1).astype(_BF16)

        def proj(c0, n):
            return _dot(h, w_scr[:, c0:c0 + n])

        def put_kv(ref, z):
            if not kv_transposed:
                ref[rows, :] = z.astype(ref.dtype)
                return
            l = ref.shape[2]
            for t0 in range(0, rb, l):
                for c0 in range(0, z.shape[1], LANES):
                    ref[(r + t0) // l, c0:c0 + LANES, :] = z[t0:t0 + l, c0:c0 + LANES].T.astype(ref.dtype)

        q_ref[rows, 0:W_A] = (proj(C_QA, W_A) * Q_SCALE).astype(q_ref.dtype)
        put_kv(ka_ref, proj(C_KA, W_A))
        put_kv(va_ref, proj(C_VA, W_A))
        zkv = proj(C_KB, 2 * KV_W_B)
        zkb, zvb = zkv[:, :KV_W_B], zkv[:, KV_W_B:]
        zqb = proj(C_QB, W_B)
        put_kv(vb_ref, zvb)
        if rope_refs is not None:
            cos = rope_refs[0][rows, :]
            sin = rope_refs[1][rows, :]
            for j in range(W_B // LANES):
                q_ref[rows, W_A + LANES * j:W_A + LANES * (j + 1)] = (
                    _rope(zqb[:, LANES * j:LANES * (j + 1)], cos, sin) * Q_SCALE).astype(q_ref.dtype)
            put_kv(kb_ref, _rope(zkb, cos, sin))
        else:
            q_ref[rows, W_A:W_A + W_B] = (zqb * Q_SCALE).astype(q_ref.dtype)
            put_kv(kb_ref, zkb)


def _ctx_attention(q_ref, rows, seq, ka_ref, va_ref, kb_ref, vb_ref, sink_ref, ga_ref, gb_ref, a_ref, b_ref):
    l = rows.stop - rows.start
    lo = lax.broadcasted_iota(jnp.int32, (l, LANES), 1) < HEAD_DIM
    row_lo = lax.broadcasted_iota(jnp.int32, (2 * l, 1), 0) < l
    oa = []
    for p in range(W_A // LANES):
        sl = slice(LANES * p, LANES * (p + 1))
        qs = _stack_heads(q_ref[rows, sl], lo)
        s = _dot(qs, ka_ref[seq, sl, :].astype(_BF16))
        vt = va_ref[seq, sl, :].astype(_BF16)
        oa.append(_unstack_heads(_softmax_pv([s], [lambda e, vt=vt: _dot_nt(e, vt)]), lo))
    a_ref[rows, :] = _rms(jnp.concatenate(oa, axis=1), ga_ref[...]).astype(a_ref.dtype)
    kb = kb_ref[seq].astype(_BF16)
    vb = vb_ref[seq].astype(_BF16)
    ob = []
    for j in range(G_B):
        qs = _stack_heads(q_ref[rows, W_A + LANES * j:W_A + LANES * (j + 1)], lo)
        s = _dot(qs, kb)
        sink = jnp.where(row_lo, sink_ref[0, j], sink_ref[1, j])
        ob.append(_unstack_heads(_softmax_pv([s], [lambda e: _dot_nt(e, vb)], sink), lo))
    b_ref[rows, :] = _rms(jnp.concatenate(ob, axis=1), _window_head_gain(gb_ref)).astype(b_ref.dtype)


def _front_kernel(sink_ref, xp_ref, xs_ref, mod_ref, g_ref, w_ref, cos_ref, sin_ref, ga_ref, gb_ref,
                  wu_ref, wd_ref, *refs, prompt_steps, mod_row):
    kat_ref, vat_ref, kbt_ref, vbt_ref, a_ref, b_ref, wu_out, wd_out = refs[0:8]
    outs_s, w_scr, q_scr = refs[8:13], refs[13], refs[14]
    i = pl.program_id(0)

    @pl.when(i == 0)
    def _():
        _prepare_in_weight(w_ref, w_scr)

    mod = mod_ref[pl.ds(mod_row(i), 1), :]

    @pl.when(i < prompt_steps)
    def _():
        wu_out[...] = wu_ref[...].astype(wu_out.dtype)
        wd_out[...] = wd_ref[...].astype(wd_out.dtype)
        _inproj_rows(xp_ref, mod, g_ref, w_scr, None, (q_scr, kat_ref, vat_ref, kbt_ref, vbt_ref))
        l = kat_ref.shape[2]
        for seq in range(kat_ref.shape[0]):
            _ctx_attention(q_scr, slice(seq * l, (seq + 1) * l), seq, kat_ref, vat_ref, kbt_ref, vbt_ref,
                           sink_ref, ga_ref, gb_ref, a_ref, b_ref)

    @pl.when(i >= prompt_steps)
    def _():
        _inproj_rows(xs_ref, mod, g_ref, w_scr, (cos_ref, sin_ref), outs_s)


def _front_call(sink, xp, xs, mod, g_pre, w_in, rope_tabs, g_a, g_b, w_up, w_down, lp, ts):
    tm = TOKEN_TILE
    tp, tsamp = xp.shape[0], xs.shape[0]
    n_p, n_s = tp // tm, tsamp // tm
    assert n_p * tm == tp and n_s * tm == tsamp and tm % lp == 0 and ts % tm == 0
    ru, rd = w_up.shape[0] // n_p, w_down.shape[0] // n_p
    assert ru * n_p == w_up.shape[0] and rd * n_p == w_down.shape[0]
    assert ru % BF16_SUBLANES == 0 and rd % BF16_SUBLANES == 0
    tiles_per_seq = ts // tm
    p_tile = lambda i: jnp.minimum(i, n_p - 1)
    s_tile = lambda i: jnp.maximum(i - n_p, 0)
    mod_row = lambda i: jnp.where(i < n_p, 0, 1 + s_tile(i) // tiles_per_seq)
    p_row = lambda i: (p_tile(i), 0)
    s_row = lambda i: (s_tile(i), 0)
    const = lambda i: (0, 0)
    nseq = tm // lp
    in_specs = [
        pl.BlockSpec(memory_space=pltpu.SMEM),
        pl.BlockSpec((tm, D_MODEL), p_row),
        pl.BlockSpec((tm, D_MODEL), s_row),
        pl.BlockSpec((MOD_ROWS, 6 * D_MODEL), const),
        pl.BlockSpec((1, D_MODEL), const),
        _resident((D_MODEL, IN_WIDTH)),
        pl.BlockSpec((tm, LANES), lambda i: (s_tile(i) % tiles_per_seq, 0)),
        pl.BlockSpec((tm, LANES), lambda i: (s_tile(i) % tiles_per_seq, 0)),
        pl.BlockSpec((1, W_A), const),
        pl.BlockSpec((1, W_B), const),
        pl.BlockSpec((ru, w_up.shape[1]), p_row),
        pl.BlockSpec((rd, w_down.shape[1]), p_row),
    ]
    kvt_shape = lambda w: jax.ShapeDtypeStruct((tp // lp, w, lp), _F32)
    kvt_spec = lambda w: pl.BlockSpec((nseq, w, lp), lambda i: (p_tile(i), 0, 0))
    kv_shape = lambda w: jax.ShapeDtypeStruct((tsamp, w), _BF16)
    kv_spec = lambda w: pl.BlockSpec((tm, w), s_row)
    out_shape = (kvt_shape(W_A), kvt_shape(W_A), kvt_shape(KV_W_B), kvt_shape(KV_W_B),
                 jax.ShapeDtypeStruct((tp, W_A), _BF16), jax.ShapeDtypeStruct((tp, W_B), _BF16),
                 jax.ShapeDtypeStruct(w_up.shape, _BF16), jax.ShapeDtypeStruct(w_down.shape, _BF16),
                 jax.ShapeDtypeStruct((tsamp, W_A + W_B), _BF16),
                 kv_shape(W_A), kv_shape(W_A), kv_shape(KV_W_B), kv_shape(KV_W_B))
    out_specs = (kvt_spec(W_A), kvt_spec(W_A), kvt_spec(KV_W_B), kvt_spec(KV_W_B),
                 pl.BlockSpec((tm, W_A), p_row), pl.BlockSpec((tm, W_B), p_row),
                 pl.BlockSpec((ru, w_up.shape[1]), p_row), pl.BlockSpec((rd, w_down.shape[1]), p_row),
                 pl.BlockSpec((tm, W_A + W_B), s_row),
                 kv_spec(W_A), kv_spec(W_A), kv_spec(KV_W_B), kv_spec(KV_W_B))
    outs = pl.pallas_call(
        functools.partial(_front_kernel, prompt_steps=n_p, mod_row=mod_row),
        grid=(n_p + n_s,),
        in_specs=in_specs,
        out_specs=out_specs,
        out_shape=out_shape,
        scratch_shapes=[pltpu.VMEM((D_MODEL, IN_WIDTH), _BF16), pltpu.VMEM((tm, W_A + W_B), _BF16)],
        compiler_params=pltpu.CompilerParams(
            dimension_semantics=("arbitrary",), vmem_limit_bytes=BIG_VMEM_LIMIT),
        name="front",
    )(sink, xp, xs, mod, g_pre.reshape(1, D_MODEL), w_in, *rope_tabs, g_a.reshape(1, W_A),
      g_b.reshape(1, W_B), w_up, w_down)
    return outs[:6], outs[6:8], outs[8:]


def _na_plan(rows):
    kr = min(NA_ROWS, rows)
    win = kr + NA_QROWS
    row_start = lambda r: min(max(r - kr // 2, 0), rows - kr)
    blocks, classes = [], []
    for r0 in range(0, rows, NA_QROWS):
        ws = min(row_start(r0), rows - win)
        keys = []
        for r in range(r0, r0 + NA_QROWS):
            key = (ws - r + NA_ROWS - 1, row_start(r) - ws)
            if key not in classes:
                classes.append(key)
            keys.append(classes.index(key))
        blocks.append((r0, ws, keys))
    return kr, win, blocks, classes


def _na_kernel(q_ref, k_ref, v_ref, ck_ref, cv_ref, rpb_ref, ga_ref, a_ref, bias_ref, o_scr):
    rows = q_ref.shape[0] // GRID_W
    kr, win, blocks, classes = _na_plan(rows)
    m = NA_QROWS * GRID_W
    lane = lax.broadcasted_iota(jnp.int32, (GRID_W, LANES), 1)
    lo = lane < HEAD_DIM
    lo_q = lax.broadcasted_iota(jnp.int32, (m, LANES), 1) < HEAD_DIM
    cq = lax.broadcasted_iota(jnp.int32, (GRID_W, LANES), 0)
    ck = lane & (GRID_W - 1)
    cs = jnp.clip(cq - NA_COLS // 2, 0, GRID_W - NA_COLS)
    valid = (ck >= cs) & (ck < cs + NA_COLS)
    neg = jnp.full((GRID_W, LANES), NEG_INF, _F32)
    n_drow = 2 * NA_ROWS - 1

    for p in range(W_A // LANES):
        sl = slice(LANES * p, LANES * (p + 1))
        for hh in range(2):
            t_lo, t_hi = [], []
            for d in range(n_drow):
                base = jnp.broadcast_to(rpb_ref[2 * p + hh, d:d + 1, :], (GRID_W, LANES))
                t_lo.append(pltpu.roll(base, LANES - (NA_COLS - 1), 1, stride=1, stride_axis=0))
                t_hi.append(pltpu.roll(base, GRID_W - (NA_COLS - 1), 1, stride=1, stride_axis=0))
            tiles = {}
            for ci, (d_first, off) in enumerate(classes):
                for j in range(win // 2):
                    d_of = lambda i: d_first + i if off <= i < off + kr else None
                    key = (d_of(2 * j), d_of(2 * j + 1))
                    if key not in tiles:
                        left = neg if key[0] is None else t_lo[key[0]]
                        right = neg if key[1] is None else t_hi[key[1]]
                        tiles[key] = jnp.where(valid, jnp.where(lo, left, right), NEG_INF)
                    bias_ref[ci, GRID_W * hh:GRID_W * (hh + 1), LANES * j:LANES * (j + 1)] = tiles[key]

        ckt = ck_ref[0, sl, :].astype(_BF16)
        cvt = cv_ref[0, sl, :].astype(_BF16)

        for r0, ws, keys in blocks:
            q0, k0 = r0 * GRID_W, ws * GRID_W
            qs = _stack_heads(q_ref[q0:q0 + m, sl], lo_q)
            kw = k_ref[k0:k0 + win * GRID_W, sl]
            vw = v_ref[k0:k0 + win * GRID_W, sl]
            bias = jnp.concatenate(
                [bias_ref[ci, GRID_W * hh:GRID_W * (hh + 1), :] for hh in range(2) for ci in keys], axis=0)
            s_nb = _dot_nt(qs, kw) + bias
            s_cx = _dot(qs, ckt)
            o2 = _softmax_pv([s_nb, s_cx], [lambda e, vw=vw: _dot(e, vw), lambda e: _dot_nt(e, cvt)])
            o_scr[q0:q0 + m, sl] = _unstack_heads(o2, lo_q)

    a_ref[...] = _rms(o_scr[...], ga_ref[...]).astype(a_ref.dtype)


def _na_call(q, k, v, ckt, cvt, rpb_pad, g_a, nb, seq):
    past = ckt.shape[2]
    _, win, _, classes = _na_plan(seq // GRID_W)
    blk = lambda b: (b, 0)
    return pl.pallas_call(
        _na_kernel,
        grid=(nb,),
        in_specs=[
            pl.BlockSpec((seq, W_A), blk),
            pl.BlockSpec((seq, W_A), blk),
            pl.BlockSpec((seq, W_A), blk),
            pl.BlockSpec((1, W_A, past), lambda b: (b, 0, 0)),
            pl.BlockSpec((1, W_A, past), lambda b: (b, 0, 0)),
            pl.BlockSpec((H_A, 2 * NA_ROWS - 1, LANES), lambda b: (0, 0, 0)),
            pl.BlockSpec((1, W_A), lambda b: (0, 0)),
        ],
        out_specs=pl.BlockSpec((seq, W_A), blk),
        out_shape=jax.ShapeDtypeStruct((nb * seq, W_A), _BF16),
        scratch_shapes=[pltpu.VMEM((len(classes), 2 * GRID_W, win * GRID_W), _F32),
                        pltpu.VMEM((seq, W_A), _F32)],
        compiler_params=pltpu.CompilerParams(
            dimension_semantics=("parallel",), vmem_limit_bytes=VMEM_LIMIT),
        name="na_attn",
    )(q, k, v, ckt, cvt, rpb_pad, g_a.reshape(1, W_A))


def _win_kernel(sink_ref, q_ref, k_ref, v_ref, ck_ref, cv_ref, gb_ref, b_ref, o_scr):
    t = q_ref.shape[0]
    qb = WIN_QBLOCKS * BLOCK
    band = qb + 2 * BLOCK
    lo = lax.broadcasted_iota(jnp.int32, (qb, LANES), 1) < HEAD_DIM
    row_lo = lax.broadcasted_iota(jnp.int32, (2 * qb, 1), 0) < qb
    qi = lax.broadcasted_iota(jnp.int32, (2 * qb, band), 0) & (qb - 1)
    kj = lax.broadcasted_iota(jnp.int32, (2 * qb, band), 1)
    rel = qi - kj
    ckt = ck_ref[0].astype(_BF16)
    cvt = cv_ref[0].astype(_BF16)

    starts = {q0: min(max(q0 - BLOCK, 0), t - band) for q0 in range(0, t, qb)}
    masks = {off: jnp.where(jnp.abs(rel + off) <= SWA_WINDOW, 0.0, NEG_INF)
             for off in sorted({q0 - k0 for q0, k0 in starts.items()})}

    for j in range(G_B):
        sl = slice(LANES * j, LANES * (j + 1))
        sink = jnp.where(row_lo, sink_ref[0, j], sink_ref[1, j])
        for q0, k0 in starts.items():
            qs = _stack_heads(q_ref[q0:q0 + qb, sl], lo)
            s = _dot_nt(qs, k_ref[k0:k0 + band, :]) + masks[q0 - k0]
            s_cx = _dot(qs, ckt)
            vband = v_ref[k0:k0 + band, :]
            o2 = _softmax_pv([s, s_cx], [lambda e, vband=vband: _dot(e, vband), lambda e: _dot_nt(e, cvt)],
                             sink)
            o_scr[q0:q0 + qb, sl] = _unstack_heads(o2, lo)

    b_ref[...] = _rms(o_scr[...], _window_head_gain(gb_ref)).astype(b_ref.dtype)


def _win_call(sink, q, k, v, ckt, cvt, g_b, nb, seq):
    past = ckt.shape[2]
    return pl.pallas_call(
        _win_kernel,
        grid=(nb,),
        in_specs=[
            pl.BlockSpec(memory_space=pltpu.SMEM),
            pl.BlockSpec((seq, W_B), lambda b: (b, W_A // W_B)),
            pl.BlockSpec((seq, KV_W_B), lambda b: (b, 0)),
            pl.BlockSpec((seq, KV_W_B), lambda b: (b, 0)),
            pl.BlockSpec((1, KV_W_B, past), lambda b: (b, 0, 0)),
            pl.BlockSpec((1, KV_W_B, past), lambda b: (b, 0, 0)),
            pl.BlockSpec((1, W_B), lambda b: (0, 0)),
        ],
        out_specs=pl.BlockSpec((seq, W_B), lambda b: (b, 0)),
        out_shape=jax.ShapeDtypeStruct((nb * seq, W_B), _BF16),
        scratch_shapes=[pltpu.VMEM((seq, W_B), _F32)],
        compiler_params=pltpu.CompilerParams(
            dimension_semantics=("parallel",), vmem_limit_bytes=VMEM_LIMIT),
        name="win_attn",
    )(sink, q, k, v, ckt, cvt, g_b.reshape(1, W_B))


def _prepare_out_weight(w_ref, w_scr):
    for r0 in range(0, W_A, CAST_COLS):
        w_scr[r0:r0 + CAST_COLS, :] = w_ref[r0:r0 + CAST_COLS, :].astype(_BF16)
    for i, h in enumerate(_QB_HEAD_ORDER):
        w_scr[W_A + HEAD_DIM * i:W_A + HEAD_DIM * (i + 1), :] = (
            w_ref[W_A + HEAD_DIM * h:W_A + HEAD_DIM * (h + 1), :].astype(_BF16))


def _zero_row(x, row):
    r0 = row - row % SUBLANES
    sub = lax.broadcasted_iota(jnp.int32, (SUBLANES, 1), 0)
    slab = jnp.where(sub == row % SUBLANES, 0.0, x[r0:r0 + SUBLANES])
    parts = ([x[:r0]] if r0 else []) + [slab] + ([x[r0 + SUBLANES:]] if r0 + SUBLANES < x.shape[0] else [])
    return jnp.concatenate(parts, axis=0)


def _conv3(u, cw, cb, seq_len, first_row=0):
    n = u.shape[0]
    prev = pltpu.roll(u, 1, 0)
    nxt = pltpu.roll(u, n - 1, 0)
    for i in range(n):
        if (first_row + i) % seq_len == 0:
            prev = _zero_row(prev, i)
        if (first_row + i) % seq_len == seq_len - 1:
            nxt = _zero_row(nxt, i)
    return prev * cw[0:1, :] + u * cw[1:2, :] + nxt * cw[2:3, :] + cb


def _gated_silu(gate, val):
    half = 0.5 * gate
    return ((half + half * jnp.tanh(half)) * val).astype(_BF16)


def _up(h2_ref, r0, r1, w):
    starts = list(range(r0, r1 - FFN_ROW_BLOCK + 1, FFN_ROW_BLOCK)) or [r0]
    bounds = starts + [r1]
    return jnp.concatenate([_dot(h2_ref[s:e, :], w) for s, e in zip(bounds[:-1], bounds[1:])], axis=0)


def _out_proj(a_ref, b_ref, x_ref, mod, gpost_ref, gffn_ref, wo_scr, x1_ref, h2_ref):
    gt1 = mod[:, 2 * D_MODEL:3 * D_MODEL]
    sh2 = mod[:, 3 * D_MODEL:4 * D_MODEL]
    sc2 = mod[:, 4 * D_MODEL:5 * D_MODEL]
    for r in range(0, x_ref.shape[0], FFN_ROW_BLOCK):
        rows = slice(r, r + FFN_ROW_BLOCK)
        y = _dot(a_ref[rows, :], wo_scr[0:W_A, :]) + _dot(b_ref[rows, :], wo_scr[W_A:W_A + W_B, :])
        x1 = x_ref[rows, :] + gt1 * _rms(y, gpost_ref[...])
        x1_ref[rows, :] = x1
        h2_ref[rows, :] = (_rms(x1, gffn_ref[...]) * (1 + sc2) + sh2).astype(h2_ref.dtype)


def _down_proj(act_ref, wd_ref, mod, gpost2_ref, x1_ref, out_ref, r0, r1):
    gt2 = mod[:, 5 * D_MODEL:6 * D_MODEL]
    for r in range(r0, r1, DOWN_ROW_BLOCK):
        rows = slice(r, r + DOWN_ROW_BLOCK)
        y = _dot(act_ref[rows, :], wd_ref[...])
        out_ref[rows, :] = x1_ref[rows, :] + gt2 * _rms(y, gpost2_ref[...])


def _tail_seq_kernel(a_ref, b_ref, x_ref, mod_ref, gpost_ref, gffn_ref, gpost2_ref, wo_ref, wu_ref,
                     cw_ref, cb_ref, wd_ref, out_ref, wo_scr, h2_scr, act_scr, *, seq_len, mod_row):
    mod = mod_ref[pl.ds(mod_row(pl.program_id(0)), 1), :]

    @pl.when(pl.program_id(0) == 0)
    def _():
        _prepare_out_weight(wo_ref, wo_scr)

    _out_proj(a_ref, b_ref, x_ref, mod, gpost_ref, gffn_ref, wo_scr, out_ref, h2_scr)

    tm = x_ref.shape[0]
    sub = min(tm, SEQ_TAIL_TILE)
    for s0 in range(0, tm, sub):
        lo_r = s0 if s0 % seq_len == 0 else s0 - BF16_SUBLANES
        hi_r = s0 + sub if (s0 + sub) % seq_len == 0 else s0 + sub + BF16_SUBLANES

        def chunk(col0, tc):
            cols = slice(col0, col0 + tc)
            u = _conv3(_up(h2_scr, lo_r, hi_r, wu_ref[:, cols]), cw_ref[:, cols], cb_ref[:, cols], seq_len, lo_r)
            return u[s0 - lo_r:s0 - lo_r + sub]

        for c0 in range(0, D_FF, FF_CHUNK):
            tc = min(FF_CHUNK, D_FF - c0)
            act_scr[s0:s0 + sub, c0:c0 + tc] = _gated_silu(chunk(c0, tc), chunk(D_FF + c0, tc))
        _down_proj(act_scr, wd_ref, mod, gpost2_ref, out_ref, out_ref, s0, s0 + sub)


def _tail_seq_call(a, b, x2d, mod, mod_row, g_post, g_ffn, g_post2, w_out, w_up, conv_w, conv_b, w_down,
                   seq_len, name):
    t = x2d.shape[0]
    tm = max(SEQ_TAIL_TILE, seq_len)
    big = tm > SEQ_TAIL_TILE
    vmem_limit = BIG_VMEM_LIMIT if big else VMEM_LIMIT
    x_mode = dict(pipeline_mode=pl.Buffered(1)) if big else {}
    row = lambda i: (i, 0)
    const = lambda i: (0, 0)
    return pl.pallas_call(
        functools.partial(_tail_seq_kernel, seq_len=seq_len, mod_row=mod_row),
        grid=(t // tm,),
        in_specs=[
            pl.BlockSpec((tm, W_A), row),
            pl.BlockSpec((tm, W_B), row),
            pl.BlockSpec((tm, D_MODEL), row, **x_mode),
            pl.BlockSpec((MOD_ROWS, 6 * D_MODEL), const),
            pl.BlockSpec((1, D_MODEL), const),
            pl.BlockSpec((1, D_MODEL), const),
            pl.BlockSpec((1, D_MODEL), const),
            _resident((W_A + W_B, D_MODEL)),
            _resident((D_MODEL, 2 * D_FF)),
            _resident((3, 2 * D_FF)),
            _resident((1, 2 * D_FF)),
            _resident((D_FF, D_MODEL)),
        ],
        out_specs=pl.BlockSpec((tm, D_MODEL), row),
        out_shape=jax.ShapeDtypeStruct((t, D_MODEL), _F32),
        scratch_shapes=[
            pltpu.VMEM((W_A + W_B, D_MODEL), _BF16),
            pltpu.VMEM((tm, D_MODEL), _BF16),
            pltpu.VMEM((tm, D_FF), _BF16),
        ],
        compiler_params=pltpu.CompilerParams(
            dimension_semantics=("arbitrary",), vmem_limit_bytes=vmem_limit),
        name=name,
    )(a, b, x2d, mod, g_post.reshape(1, D_MODEL), g_ffn.reshape(1, D_MODEL), g_post2.reshape(1, D_MODEL),
      w_out, w_up, conv_w, conv_b.reshape(1, 2 * D_FF), w_down)


def _rope_tables(seq):
    n = ROPE_HALF
    t = np.arange(seq)
    lane = np.arange(HEAD_DIM)
    pos = np.where(lane[None, :] < HEAD_DIM // 2, (t // GRID_W)[:, None], (t % GRID_W)[:, None])
    inv = 1.0 / (ROPE_THETA ** (np.arange(n, dtype=np.float64) / n))
    ang = pos.astype(np.float64) * inv[lane % n][None, :]
    sign = np.where((lane & n) == 0, -1.0, 1.0)[None, :]
    cos = np.tile(np.cos(ang), (1, LANES // HEAD_DIM)).astype(np.float32)
    sin = np.tile(np.sin(ang) * sign, (1, LANES // HEAD_DIM)).astype(np.float32)
    return jnp.asarray(cos), jnp.asarray(sin)


def _to_head_dim_token(cache_l):
    b, l, h, d = cache_l.shape
    return jnp.transpose(cache_l, (0, 2, 3, 1)).reshape(b, h * d, l)


def _from_head_dim_token(kv_t, heads):
    b, _, l = kv_t.shape
    return jnp.transpose(kv_t.reshape(b, 1, heads, HEAD_DIM, l), (0, 1, 4, 2, 3))


def kernel(x_prompt, x_sample, cache_a_k, cache_a_v, cache_b_k, cache_b_v, c, c_ctx, w_mod, b_mod,
           g_mix_pre, g_mix_post, g_ffn_pre, g_ffn_post, w_in, rpb_a, sink_b, g_grp_a, g_grp_b,
           w_out, w_up, conv_w, conv_b, w_down):
    bp, lp, _ = x_prompt.shape
    bs, ts, _ = x_sample.shape
    assert w_in.shape[0] == 1
    xp = x_prompt.reshape(bp * lp, D_MODEL)
    xs = x_sample.reshape(bs * ts, D_MODEL)
    assert 1 + bs <= MOD_ROWS
    cond = jnp.concatenate([c_ctx[None], c, jnp.zeros((MOD_ROWS - 1 - bs, D_MODEL), _F32)], axis=0)
    cos_tab, sin_tab = _rope_tables(ts)

    l = 0
    mod = _mod_call(cond, w_mod[l], b_mod[l])
    rpb_pad = jnp.pad(rpb_a[l], ((0, 0), (0, 0), (0, LANES - (2 * NA_COLS - 1))))

    prompt_row = lambda i: 0
    assert ts >= SEQ_TAIL_TILE
    sample_row = lambda i: 1 + i

    (kat_p, vat_p, kbt_p, vbt_p, a_p, b_p), (w_up_b, w_down_b), (q_s, ka_s, va_s, kb_s, vb_s) = _front_call(
        sink_b[l], xp, xs, mod, g_mix_pre[l], w_in[l], (cos_tab, sin_tab), g_grp_a[l], g_grp_b[l],
        w_up[l], w_down[l], lp, ts)

    y_p = _tail_seq_call(a_p, b_p, xp, mod, prompt_row, g_mix_post[l], g_ffn_pre[l], g_ffn_post[l], w_out[l],
                         w_up_b, conv_w[l], conv_b[l], w_down_b, lp, "tail_prompt")

    a_s = _na_call(q_s, ka_s, va_s, _to_head_dim_token(cache_a_k[:, l]), _to_head_dim_token(cache_a_v[:, l]),
                   rpb_pad, g_grp_a[l], bs, ts)
    b_s = _win_call(sink_b[l], q_s, kb_s, vb_s, _to_head_dim_token(cache_b_k[:, l]),
                    _to_head_dim_token(cache_b_v[:, l]), g_grp_b[l], bs, ts)
    y_s = _tail_seq_call(a_s, b_s, xs, mod, sample_row, g_mix_post[l], g_ffn_pre[l], g_ffn_post[l], w_out[l],
                         w_up_b, conv_w[l], conv_b[l], w_down_b, ts, "tail_sample")

    return (y_p.reshape(bp, lp, D_MODEL), y_s.reshape(bs, ts, D_MODEL),
            _from_head_dim_token(kat_p, H_A), _from_head_dim_token(vat_p, H_A),
            _from_head_dim_token(kbt_p, KV_B), _from_head_dim_token(vbt_p, KV_B))
```
